```python
import jax, jax.numpy as jnp
from jax import lax
import numpy as np

D_MODEL = 2048
BATCH = 8
SEQ = 2048
DEPTH = 4

N_MIXERS = 4
HEAD_DIM = 128
N_HEADS = D_MODEL // HEAD_DIM
Q_BLOCK = 128
EPS = 1e-6

FOX_FGATE_BIAS = 2.0

MLA_Q_RANK = 512
MLA_KV_RANK = 512
MLA_NOPE = 128
MLA_ROPE = 64
MLA_V = 128
ROPE_THETA = 10000.0

SGU_CHUNK = 128
SGU_WIDTH = D_MODEL
SGU_GROUP_DIM = 128
SGU_GROUPS = SGU_WIDTH // SGU_GROUP_DIM

D_FF = 5632
CONV_WIDTH = 3

kernel_name = 'hybrid_interleaved_fox_mla_stickbreak_sgu'


def _n_layers_of(m):
    return len(range(m, DEPTH, N_MIXERS))


def rms_norm(x, gain):
    x32 = x.astype(jnp.float32)
    y = x32 * lax.rsqrt(jnp.mean(x32 * x32, axis=-1, keepdims=True) + EPS)
    return (y * gain.astype(jnp.float32)).astype(x.dtype)


def _sweep_query_blocks(block_fn, seq):
    out = lax.map(block_fn, jnp.arange(seq // Q_BLOCK))
    nb, b, qb, h, dv = out.shape
    return out.transpose(1, 0, 2, 3, 4).reshape(b, seq, h * dv)


def _causal_softmax(s, start, v):
    seq = s.shape[-1]
    q_pos = start + jnp.arange(Q_BLOCK)
    allowed = jnp.arange(seq)[None, :] <= q_pos[:, None]
    p = jax.nn.softmax(jnp.where(allowed, s, -jnp.inf), axis=-1)
    return jnp.einsum('bhqs,bshd->bqhd', p.astype(v.dtype), v)


def fox_mixer(h, w_in, b_f, q_gain, k_gain, w_out):
    bsz, seq, _ = h.shape
    hd = N_HEADS * HEAD_DIM
    q, k, v, f_logit = jnp.split(h @ w_in, [hd, 2 * hd, 3 * hd], axis=-1)
    q = rms_norm(q.reshape(bsz, seq, N_HEADS, HEAD_DIM), q_gain)
    k = rms_norm(k.reshape(bsz, seq, N_HEADS, HEAD_DIM), k_gain)
    v = v.reshape(bsz, seq, N_HEADS, HEAD_DIM)
    log_f = jax.nn.log_sigmoid(f_logit.astype(jnp.float32) + b_f.astype(jnp.float32))
    cum = jnp.cumsum(log_f, axis=1).transpose(0, 2, 1)
    scale = HEAD_DIM ** -0.5

    def block(i):
        start = i * Q_BLOCK
        qb = lax.dynamic_slice_in_dim(q, start, Q_BLOCK, axis=1)
        cq = lax.dynamic_slice_in_dim(cum, start, Q_BLOCK, axis=2)
        s = jnp.einsum('bqhd,bshd->bhqs', qb, k, preferred_element_type=jnp.float32) * scale
        s = s + (cq[..., :, None] - cum[:, :, None, :])
        return _causal_softmax(s, start, v)

    return _sweep_query_blocks(block, seq) @ w_out


def _rope_tables(positions):
    inv_freq = ROPE_THETA ** (-jnp.arange(0, MLA_ROPE, 2, dtype=jnp.float32) / MLA_ROPE)
    ang = positions.astype(jnp.float32)[..., None] * inv_freq
    return jnp.cos(ang)[:, :, None, :], jnp.sin(ang)[:, :, None, :]


def _apply_rope(x, cos, sin):
    x1, x2 = jnp.split(x.astype(jnp.float32), 2, axis=-1)
    return jnp.concatenate([x1 * cos - x2 * sin, x1 * sin + x2 * cos], axis=-1).astype(x.dtype)


def mla_mixer(h, positions, w_in, q_a_gain, kv_a_gain, w_q_b, w_kv_b, q_gain, k_gain, w_out):
    bsz, seq, _ = h.shape
    c_q, c_kv, k_rope = jnp.split(h @ w_in, [MLA_Q_RANK, MLA_Q_RANK + MLA_KV_RANK], axis=-1)
    q = (rms_norm(c_q, q_a_gain) @ w_q_b).reshape(bsz, seq, N_HEADS, MLA_NOPE + MLA_ROPE)
    kv = (rms_norm(c_kv, kv_a_gain) @ w_kv_b).reshape(bsz, seq, N_HEADS, MLA_NOPE + MLA_V)
    q_nope, q_rope = jnp.split(q, [MLA_NOPE], axis=-1)
    k_nope, v = jnp.split(kv, [MLA_NOPE], axis=-1)
    cos, sin = _rope_tables(positions)
    q_nope = rms_norm(q_nope, q_gain[:MLA_NOPE])
    k_nope = rms_norm(k_nope, k_gain[:MLA_NOPE])
    q_rope = _apply_rope(rms_norm(q_rope, q_gain[MLA_NOPE:]), cos, sin)
    k_rope = _apply_rope(rms_norm(k_rope[:, :, None, :], k_gain[MLA_NOPE:]), cos, sin)
    q = jnp.concatenate([q_nope, q_rope], axis=-1)
    k = jnp.concatenate([k_nope, jnp.broadcast_to(k_rope, (bsz, seq, N_HEADS, MLA_ROPE))], axis=-1)
    scale = (MLA_NOPE + MLA_ROPE) ** -0.5

    def block(i):
        start = i * Q_BLOCK
        qb = lax.dynamic_slice_in_dim(q, start, Q_BLOCK, axis=1)
        s = jnp.einsum('bqhd,bshd->bhqs', qb, k, preferred_element_type=jnp.float32) * scale
        return _causal_softmax(s, start, v)

    return _sweep_query_blocks(block, seq) @ w_out


def stick_breaking_mixer(h, w_in, q_gain, k_gain, w_out):
    bsz, seq, _ = h.shape
    hd = N_HEADS * HEAD_DIM
    q, k, v = jnp.split(h @ w_in, [hd, 2 * hd], axis=-1)
    q = rms_norm(q.reshape(bsz, seq, N_HEADS, HEAD_DIM), q_gain)
    k = rms_norm(k.reshape(bsz, seq, N_HEADS, HEAD_DIM), k_gain)
    v = v.reshape(bsz, seq, N_HEADS, HEAD_DIM)
    scale = HEAD_DIM ** -0.5

    def block(i):
        start = i * Q_BLOCK
        qb = lax.dynamic_slice_in_dim(q, start, Q_BLOCK, axis=1)
        z = jnp.einsum('bqhd,bshd->bhqs', qb, k, preferred_element_type=jnp.float32) * scale
        q_pos = start + jnp.arange(Q_BLOCK)
        strict = jnp.arange(seq)[None, :] < q_pos[:, None]
        log_keep = jnp.where(strict, -jax.nn.softplus(z), 0.0)
        after = lax.cumsum(log_keep, axis=3, reverse=True) - log_keep
        a = jnp.where(strict, jnp.exp(jax.nn.log_sigmoid(z) + after), 0.0)
        return jnp.einsum('bhqs,bshd->bqhd', a.astype(v.dtype), v)

    return _sweep_query_blocks(block, seq) @ w_out


def sgu_mixer(h, w_in, v_gain, w_s, b_s, w_out):
    bsz, seq, _ = h.shape
    u, vv = jnp.split(jax.nn.gelu(h @ w_in), 2, axis=-1)
    vv = rms_norm(vv, v_gain)
    n_chunks = seq // SGU_CHUNK
    vv = vv.reshape(bsz, n_chunks, SGU_CHUNK, SGU_GROUPS, SGU_GROUP_DIM)
    causal = jnp.tril(jnp.ones((SGU_CHUNK, SGU_CHUNK), dtype=bool))
    ws = jnp.where(causal[None], w_s, 0.0).astype(vv.dtype)
    mixed = jnp.einsum('gts,bnsgc->bntgc', ws, vv) + b_s.T[:, :, None]
    return (u * mixed.reshape(bsz, seq, SGU_WIDTH)) @ w_out


def conv_ffn(h, w_up, conv_w, conv_b, w_down):
    seq = h.shape[1]
    up = h @ w_up
    padded = jnp.pad(up, ((0, 0), (CONV_WIDTH - 1, 0), (0, 0)))
    y = conv_b + conv_w[0] * padded[:, 0:seq]
    for tap in range(1, CONV_WIDTH):
        y = y + conv_w[tap] * padded[:, tap:tap + seq]
    gate, val = jnp.split(y, 2, axis=-1)
    return (jax.nn.silu(gate) * val) @ w_down


def _fwd_setup_inputs(seed: int = 0) -> dict:
    key = jax.random.key(seed)
    ks = iter(list(jax.random.split(key, 32)))

    def w(shape, fan_in):
        return jax.random.normal(next(ks), shape, jnp.float32) * fan_in ** -0.5

    def gain(shape):
        return 1.0 + 0.1 * jax.random.normal(next(ks), shape, jnp.float32)

    n_a, n_b, n_c, n_d = (_n_layers_of(m) for m in range(N_MIXERS))
    hd = N_HEADS * HEAD_DIM
    x = jax.random.normal(next(ks), (BATCH, SEQ, D_MODEL), jnp.float32)
    offset = jax.random.randint(next(ks), (BATCH, 1), 0, 4096, dtype=jnp.int32)
    positions = offset + jnp.arange(SEQ, dtype=jnp.int32)[None, :]
    return {
        'x': x,
        'positions': positions,
        'mix_norm': gain((DEPTH, D_MODEL)),
        'ffn_norm': gain((DEPTH, D_MODEL)),
        'fox_w_in': w((n_a, D_MODEL, 3 * hd + N_HEADS), D_MODEL),
        'fox_b_f': FOX_FGATE_BIAS + 0.5 * jax.random.normal(next(ks), (n_a, N_HEADS), jnp.float32),
        'fox_q_gain': gain((n_a, HEAD_DIM)),
        'fox_k_gain': gain((n_a, HEAD_DIM)),
        'fox_w_out': w((n_a, hd, D_MODEL), hd),
        'mla_w_in': w((n_b, D_MODEL, MLA_Q_RANK + MLA_KV_RANK + MLA_ROPE), D_MODEL),
        'mla_q_a_gain': gain((n_b, MLA_Q_RANK)),
        'mla_kv_a_gain': gain((n_b, MLA_KV_RANK)),
        'mla_w_q_b': w((n_b, MLA_Q_RANK, N_HEADS * (MLA_NOPE + MLA_ROPE)), MLA_Q_RANK),
        'mla_w_kv_b': w((n_b, MLA_KV_RANK, N_HEADS * (MLA_NOPE + MLA_V)), MLA_KV_RANK),
        'mla_q_gain': gain((n_b, MLA_NOPE + MLA_ROPE)),
        'mla_k_gain': gain((n_b, MLA_NOPE + MLA_ROPE)),
        'mla_w_out': w((n_b, N_HEADS * MLA_V, D_MODEL), N_HEADS * MLA_V),
        'sb_w_in': w((n_c, D_MODEL, 3 * hd), D_MODEL),
        'sb_q_gain': gain((n_c, HEAD_DIM)),
        'sb_k_gain': gain((n_c, HEAD_DIM)),
        'sb_w_out': w((n_c, hd, D_MODEL), hd),
        'sgu_w_in': w((n_d, D_MODEL, 2 * SGU_WIDTH), D_MODEL),
        'sgu_v_gain': gain((n_d, SGU_WIDTH)),
        'sgu_w_s': w((n_d, SGU_GROUPS, SGU_CHUNK, SGU_CHUNK), SGU_CHUNK),
        'sgu_b_s': gain((n_d, SGU_GROUPS, SGU_CHUNK)),
        'sgu_w_out': w((n_d, SGU_WIDTH, D_MODEL), SGU_WIDTH),
        'ffn_w_up': w((DEPTH, D_MODEL, 2 * D_FF), D_MODEL),
        'ffn_conv_w': w((DEPTH, CONV_WIDTH, 2 * D_FF), CONV_WIDTH),
        'ffn_conv_b': 0.01 * jax.random.normal(next(ks), (DEPTH, 2 * D_FF), jnp.float32),
        'ffn_w_down': w((DEPTH, D_FF, D_MODEL), D_FF),
    }


def _fwd_reference(x, positions, mix_norm, ffn_norm,
              fox_w_in, fox_b_f, fox_q_gain, fox_k_gain, fox_w_out,
              mla_w_in, mla_q_a_gain, mla_kv_a_gain, mla_w_q_b, mla_w_kv_b,
              mla_q_gain, mla_k_gain, mla_w_out,
              sb_w_in, sb_q_gain, sb_k_gain, sb_w_out,
              sgu_w_in, sgu_v_gain, sgu_w_s, sgu_b_s, sgu_w_out,
              ffn_w_up, ffn_conv_w, ffn_conv_b, ffn_w_down):
    h = x
    for i in range(DEPTH):
        m, j = i % N_MIXERS, i // N_MIXERS
        a = rms_norm(h, mix_norm[i])
        if m == 0:
            mixed = fox_mixer(a, fox_w_in[j], fox_b_f[j], fox_q_gain[j], fox_k_gain[j], fox_w_out[j])
        elif m == 1:
            mixed = mla_mixer(a, positions, mla_w_in[j], mla_q_a_gain[j], mla_kv_a_gain[j],
                              mla_w_q_b[j], mla_w_kv_b[j], mla_q_gain[j], mla_k_gain[j], mla_w_out[j])
        elif m == 2:
            mixed = stick_breaking_mixer(a, sb_w_in[j], sb_q_gain[j], sb_k_gain[j], sb_w_out[j])
        else:
            mixed = sgu_mixer(a, sgu_w_in[j], sgu_v_gain[j], sgu_w_s[j], sgu_b_s[j], sgu_w_out[j])
        h = h + mixed
        h = h + conv_ffn(rms_norm(h, ffn_norm[i]), ffn_w_up[i], ffn_conv_w[i], ffn_conv_b[i], ffn_w_down[i])
    return h


import jax as _jax
import jax.numpy as _jnp

TWIN_FORMAT = 'train_step'
FWD_PARAMS = ['x', 'positions', 'mix_norm', 'ffn_norm', 'fox_w_in', 'fox_b_f', 'fox_q_gain', 'fox_k_gain', 'fox_w_out', 'mla_w_in', 'mla_q_a_gain', 'mla_kv_a_gain', 'mla_w_q_b', 'mla_w_kv_b', 'mla_q_gain', 'mla_k_gain', 'mla_w_out', 'sb_w_in', 'sb_q_gain', 'sb_k_gain', 'sb_w_out', 'sgu_w_in', 'sgu_v_gain', 'sgu_w_s', 'sgu_b_s', 'sgu_w_out', 'ffn_w_up', 'ffn_conv_w', 'ffn_conv_b', 'ffn_w_down']
TWIN_WEIGHTS = ['mix_norm', 'ffn_norm', 'fox_w_in', 'fox_b_f', 'fox_q_gain', 'fox_k_gain', 'fox_w_out', 'mla_w_in', 'mla_q_a_gain', 'mla_kv_a_gain', 'mla_w_q_b', 'mla_w_kv_b', 'mla_q_gain', 'mla_k_gain', 'mla_w_out', 'sb_w_in', 'sb_q_gain', 'sb_k_gain', 'sb_w_out', 'sgu_w_in', 'sgu_v_gain', 'sgu_w_s', 'sgu_b_s', 'sgu_w_out', 'ffn_w_up', 'ffn_conv_w', 'ffn_conv_b', 'ffn_w_down']
TWIN_DIFF_INPUT = 'x'
TWIN_INPUTS = ['x', 'positions', 'mix_norm', 'ffn_norm', 'fox_w_in', 'fox_b_f', 'fox_q_gain', 'fox_k_gain', 'fox_w_out', 'mla_w_in', 'mla_q_a_gain', 'mla_kv_a_gain', 'mla_w_q_b', 'mla_w_kv_b', 'mla_q_gain', 'mla_k_gain', 'mla_w_out', 'sb_w_in', 'sb_q_gain', 'sb_k_gain', 'sb_w_out', 'sgu_w_in', 'sgu_v_gain', 'sgu_w_s', 'sgu_b_s', 'sgu_w_out', 'ffn_w_up', 'ffn_conv_w', 'ffn_conv_b', 'ffn_w_down', 'loss_target', 'm_mix_norm', 'm_ffn_norm', 'm_fox_w_in', 'm_fox_b_f', 'm_fox_q_gain', 'm_fox_k_gain', 'm_fox_w_out', 'm_mla_w_in', 'm_mla_q_a_gain', 'm_mla_kv_a_gain', 'm_mla_w_q_b', 'm_mla_w_kv_b', 'm_mla_q_gain', 'm_mla_k_gain', 'm_mla_w_out', 'm_sb_w_in', 'm_sb_q_gain', 'm_sb_k_gain', 'm_sb_w_out', 'm_sgu_w_in', 'm_sgu_v_gain', 'm_sgu_w_s', 'm_sgu_b_s', 'm_sgu_w_out', 'm_ffn_w_up', 'm_ffn_conv_w', 'm_ffn_conv_b', 'm_ffn_w_down', 'v_mix_norm', 'v_ffn_norm', 'v_fox_w_in', 'v_fox_b_f', 'v_fox_q_gain', 'v_fox_k_gain', 'v_fox_w_out', 'v_mla_w_in', 'v_mla_q_a_gain', 'v_mla_kv_a_gain', 'v_mla_w_q_b', 'v_mla_w_kv_b', 'v_mla_q_gain', 'v_mla_k_gain', 'v_mla_w_out', 'v_sb_w_in', 'v_sb_q_gain', 'v_sb_k_gain', 'v_sb_w_out', 'v_sgu_w_in', 'v_sgu_v_gain', 'v_sgu_w_s', 'v_sgu_b_s', 'v_sgu_w_out', 'v_ffn_w_up', 'v_ffn_conv_w', 'v_ffn_conv_b', 'v_ffn_w_down']
TWIN_OUTPUTS = ['loss', 'grad_x', 'grad_mix_norm', 'grad_ffn_norm', 'grad_fox_w_in', 'grad_fox_b_f', 'grad_fox_q_gain', 'grad_fox_k_gain', 'grad_fox_w_out', 'grad_mla_w_in', 'grad_mla_q_a_gain', 'grad_mla_kv_a_gain', 'grad_mla_w_q_b', 'grad_mla_w_kv_b', 'grad_mla_q_gain', 'grad_mla_k_gain', 'grad_mla_w_out', 'grad_sb_w_in', 'grad_sb_q_gain', 'grad_sb_k_gain', 'grad_sb_w_out', 'grad_sgu_w_in', 'grad_sgu_v_gain', 'grad_sgu_w_s', 'grad_sgu_b_s', 'grad_sgu_w_out', 'grad_ffn_w_up', 'grad_ffn_conv_w', 'grad_ffn_conv_b', 'grad_ffn_w_down', 'delta_mix_norm', 'delta_ffn_norm', 'delta_fox_w_in', 'delta_fox_b_f', 'delta_fox_q_gain', 'delta_fox_k_gain', 'delta_fox_w_out', 'delta_mla_w_in', 'delta_mla_q_a_gain', 'delta_mla_kv_a_gain', 'delta_mla_w_q_b', 'delta_mla_w_kv_b', 'delta_mla_q_gain', 'delta_mla_k_gain', 'delta_mla_w_out', 'delta_sb_w_in', 'delta_sb_q_gain', 'delta_sb_k_gain', 'delta_sb_w_out', 'delta_sgu_w_in', 'delta_sgu_v_gain', 'delta_sgu_w_s', 'delta_sgu_b_s', 'delta_sgu_w_out', 'delta_ffn_w_up', 'delta_ffn_conv_w', 'delta_ffn_conv_b', 'delta_ffn_w_down', 'new_m_mix_norm', 'new_m_ffn_norm', 'new_m_fox_w_in', 'new_m_fox_b_f', 'new_m_fox_q_gain', 'new_m_fox_k_gain', 'new_m_fox_w_out', 'new_m_mla_w_in', 'new_m_mla_q_a_gain', 'new_m_mla_kv_a_gain', 'new_m_mla_w_q_b', 'new_m_mla_w_kv_b', 'new_m_mla_q_gain', 'new_m_mla_k_gain', 'new_m_mla_w_out', 'new_m_sb_w_in', 'new_m_sb_q_gain', 'new_m_sb_k_gain', 'new_m_sb_w_out', 'new_m_sgu_w_in', 'new_m_sgu_v_gain', 'new_m_sgu_w_s', 'new_m_sgu_b_s', 'new_m_sgu_w_out', 'new_m_ffn_w_up', 'new_m_ffn_conv_w', 'new_m_ffn_conv_b', 'new_m_ffn_w_down', 'new_v_mix_norm', 'new_v_ffn_norm', 'new_v_fox_w_in', 'new_v_fox_b_f', 'new_v_fox_q_gain', 'new_v_fox_k_gain', 'new_v_fox_w_out', 'new_v_mla_w_in', 'new_v_mla_q_a_gain', 'new_v_mla_kv_a_gain', 'new_v_mla_w_q_b', 'new_v_mla_w_kv_b', 'new_v_mla_q_gain', 'new_v_mla_k_gain', 'new_v_mla_w_out', 'new_v_sb_w_in', 'new_v_sb_q_gain', 'new_v_sb_k_gain', 'new_v_sb_w_out', 'new_v_sgu_w_in', 'new_v_sgu_v_gain', 'new_v_sgu_w_s', 'new_v_sgu_b_s', 'new_v_sgu_w_out', 'new_v_ffn_w_up', 'new_v_ffn_conv_w', 'new_v_ffn_conv_b', 'new_v_ffn_w_down']
TWIN_LEAF_KINDS = {'loss': 'loss', 'grad_x': 'grad_x', 'grad_mix_norm': 'grad_w', 'grad_ffn_norm': 'grad_w', 'grad_fox_w_in': 'grad_w', 'grad_fox_b_f': 'grad_w', 'grad_fox_q_gain': 'grad_w', 'grad_fox_k_gain': 'grad_w', 'grad_fox_w_out': 'grad_w', 'grad_mla_w_in': 'grad_w', 'grad_mla_q_a_gain': 'grad_w', 'grad_mla_kv_a_gain': 'grad_w', 'grad_mla_w_q_b': 'grad_w', 'grad_mla_w_kv_b': 'grad_w', 'grad_mla_q_gain': 'grad_w', 'grad_mla_k_gain': 'grad_w', 'grad_mla_w_out': 'grad_w', 'grad_sb_w_in': 'grad_w', 'grad_sb_q_gain': 'grad_w', 'grad_sb_k_gain': 'grad_w', 'grad_sb_w_out': 'grad_w', 'grad_sgu_w_in': 'grad_w', 'grad_sgu_v_gain': 'grad_w', 'grad_sgu_w_s': 'grad_w', 'grad_sgu_b_s': 'grad_w', 'grad_sgu_w_out': 'grad_w', 'grad_ffn_w_up': 'grad_w', 'grad_ffn_conv_w': 'grad_w', 'grad_ffn_conv_b': 'grad_w', 'grad_ffn_w_down': 'grad_w', 'delta_mix_norm': 'delta_w', 'delta_ffn_norm': 'delta_w', 'delta_fox_w_in': 'delta_w', 'delta_fox_b_f': 'delta_w', 'delta_fox_q_gain': 'delta_w', 'delta_fox_k_gain': 'delta_w', 'delta_fox_w_out': 'delta_w', 'delta_mla_w_in': 'delta_w', 'delta_mla_q_a_gain': 'delta_w', 'delta_mla_kv_a_gain': 'delta_w', 'delta_mla_w_q_b': 'delta_w', 'delta_mla_w_kv_b': 'delta_w', 'delta_mla_q_gain': 'delta_w', 'delta_mla_k_gain': 'delta_w', 'delta_mla_w_out': 'delta_w', 'delta_sb_w_in': 'delta_w', 'delta_sb_q_gain': 'delta_w', 'delta_sb_k_gain': 'delta_w', 'delta_sb_w_out': 'delta_w', 'delta_sgu_w_in': 'delta_w', 'delta_sgu_v_gain': 'delta_w', 'delta_sgu_w_s': 'delta_w', 'delta_sgu_b_s': 'delta_w', 'delta_sgu_w_out': 'delta_w', 'delta_ffn_w_up': 'delta_w', 'delta_ffn_conv_w': 'delta_w', 'delta_ffn_conv_b': 'delta_w', 'delta_ffn_w_down': 'delta_w', 'new_m_mix_norm': 'new_m', 'new_m_ffn_norm': 'new_m', 'new_m_fox_w_in': 'new_m', 'new_m_fox_b_f': 'new_m', 'new_m_fox_q_gain': 'new_m', 'new_m_fox_k_gain': 'new_m', 'new_m_fox_w_out': 'new_m', 'new_m_mla_w_in': 'new_m', 'new_m_mla_q_a_gain': 'new_m', 'new_m_mla_kv_a_gain': 'new_m', 'new_m_mla_w_q_b': 'new_m', 'new_m_mla_w_kv_b': 'new_m', 'new_m_mla_q_gain': 'new_m', 'new_m_mla_k_gain': 'new_m', 'new_m_mla_w_out': 'new_m', 'new_m_sb_w_in': 'new_m', 'new_m_sb_q_gain': 'new_m', 'new_m_sb_k_gain': 'new_m', 'new_m_sb_w_out': 'new_m', 'new_m_sgu_w_in': 'new_m', 'new_m_sgu_v_gain': 'new_m', 'new_m_sgu_w_s': 'new_m', 'new_m_sgu_b_s': 'new_m', 'new_m_sgu_w_out': 'new_m', 'new_m_ffn_w_up': 'new_m', 'new_m_ffn_conv_w': 'new_m', 'new_m_ffn_conv_b': 'new_m', 'new_m_ffn_w_down': 'new_m', 'new_v_mix_norm': 'new_v', 'new_v_ffn_norm': 'new_v', 'new_v_fox_w_in': 'new_v', 'new_v_fox_b_f': 'new_v', 'new_v_fox_q_gain': 'new_v', 'new_v_fox_k_gain': 'new_v', 'new_v_fox_w_out': 'new_v', 'new_v_mla_w_in': 'new_v', 'new_v_mla_q_a_gain': 'new_v', 'new_v_mla_kv_a_gain': 'new_v', 'new_v_mla_w_q_b': 'new_v', 'new_v_mla_w_kv_b': 'new_v', 'new_v_mla_q_gain': 'new_v', 'new_v_mla_k_gain': 'new_v', 'new_v_mla_w_out': 'new_v', 'new_v_sb_w_in': 'new_v', 'new_v_sb_q_gain': 'new_v', 'new_v_sb_k_gain': 'new_v', 'new_v_sb_w_out': 'new_v', 'new_v_sgu_w_in': 'new_v', 'new_v_sgu_v_gain': 'new_v', 'new_v_sgu_w_s': 'new_v', 'new_v_sgu_b_s': 'new_v', 'new_v_sgu_w_out': 'new_v', 'new_v_ffn_w_up': 'new_v', 'new_v_ffn_conv_w': 'new_v', 'new_v_ffn_conv_b': 'new_v', 'new_v_ffn_w_down': 'new_v'}


def _forward(args):
    return _fwd_reference(*[args[k] for k in FWD_PARAMS])


def _output_shape():
    out = _jax.eval_shape(lambda: _forward(_fwd_setup_inputs(0)))
    return out.shape, out.dtype

N_MICROBATCH = 1
ADAM_LR = 0.001
ADAM_B1 = 0.9
ADAM_B2 = 0.999
ADAM_EPS = 1e-08
ADAM_WD = 0.01
ADAM_STEP = 10
PER_EXAMPLE_BATCH_AXIS = {'x': 0, 'positions': 0, 'loss_target': 0}
SHARED_INPUTS = []
_WEIGHT_DTYPES = {'mix_norm': _jnp.float32, 'ffn_norm': _jnp.float32, 'fox_w_in': _jnp.float32, 'fox_b_f': _jnp.float32, 'fox_q_gain': _jnp.float32, 'fox_k_gain': _jnp.float32, 'fox_w_out': _jnp.float32, 'mla_w_in': _jnp.float32, 'mla_q_a_gain': _jnp.float32, 'mla_kv_a_gain': _jnp.float32, 'mla_w_q_b': _jnp.float32, 'mla_w_kv_b': _jnp.float32, 'mla_q_gain': _jnp.float32, 'mla_k_gain': _jnp.float32, 'mla_w_out': _jnp.float32, 'sb_w_in': _jnp.float32, 'sb_q_gain': _jnp.float32, 'sb_k_gain': _jnp.float32, 'sb_w_out': _jnp.float32, 'sgu_w_in': _jnp.float32, 'sgu_v_gain': _jnp.float32, 'sgu_w_s': _jnp.float32, 'sgu_b_s': _jnp.float32, 'sgu_w_out': _jnp.float32, 'ffn_w_up': _jnp.float32, 'ffn_conv_w': _jnp.float32, 'ffn_conv_b': _jnp.float32, 'ffn_w_down': _jnp.float32}
MOMENT_SCALE = {'mix_norm': 3.440950e+00, 'ffn_norm': 6.497979e+00, 'fox_w_in': 1.955814e-01, 'fox_b_f': 5.221743e+01, 'fox_q_gain': 7.384223e+00, 'fox_k_gain': 7.574256e+00, 'fox_w_out': 2.390675e-01, 'mla_w_in': 2.324565e-01, 'mla_q_a_gain': 1.346388e-01, 'mla_kv_a_gain': 6.158617e-01, 'mla_w_q_b': 5.329243e-02, 'mla_w_kv_b': 1.058758e-01, 'mla_q_gain': 6.228490e-01, 'mla_k_gain': 6.263328e-01, 'mla_w_out': 1.359289e-01, 'sb_w_in': 1.396856e-01, 'sb_q_gain': 7.606902e+00, 'sb_k_gain': 7.657663e+00, 'sb_w_out': 1.950258e-01, 'sgu_w_in': 1.842124e-01, 'sgu_v_gain': 1.693669e+00, 'sgu_w_s': 1.101338e+00, 'sgu_b_s': 3.565677e+00, 'sgu_w_out': 7.466262e-01, 'ffn_w_up': 1.357081e-01, 'ffn_conv_w': 8.917280e-01, 'ffn_conv_b': 9.312595e-01, 'ffn_w_down': 1.813205e-01}


def _to_microbatches(a, axis):
    t = _jnp.moveaxis(a, axis, 0)
    t = t.reshape((N_MICROBATCH, t.shape[0] // N_MICROBATCH) + t.shape[1:])
    return _jnp.moveaxis(t, 1, axis + 1)


def setup_inputs(seed: int = 0) -> dict:
    inp = _fwd_setup_inputs(seed)
    key = _jax.random.fold_in(_jax.random.key(seed), 7919)
    shape, _ = _output_shape()
    out = dict(inp)
    out["loss_target"] = _jax.random.normal(_jax.random.fold_in(key, 0), shape, _jnp.float32)
    for i, name in enumerate(TWIN_WEIGHTS):
        w = inp[name].astype(_jnp.float32)
        if MOMENT_SCALE is None:
            s = _jnp.sqrt(_jnp.mean(_jnp.square(w)) + 1e-30)
        else:
            s = MOMENT_SCALE[name]
        km, kv = _jax.random.split(_jax.random.fold_in(key, i + 1))
        out[name] = w
        out["m_" + name] = s * _jax.random.normal(km, w.shape, _jnp.float32)
        out["v_" + name] = (s * s) * _jax.random.uniform(kv, w.shape, _jnp.float32, 0.5, 1.5)
    if N_MICROBATCH > 1:
        for name, axis in PER_EXAMPLE_BATCH_AXIS.items():
            out[name] = _to_microbatches(out[name], axis)
    return {'x': out['x'], 'positions': out['positions'], 'mix_norm': out['mix_norm'], 'ffn_norm': out['ffn_norm'], 'fox_w_in': out['fox_w_in'], 'fox_b_f': out['fox_b_f'], 'fox_q_gain': out['fox_q_gain'], 'fox_k_gain': out['fox_k_gain'], 'fox_w_out': out['fox_w_out'], 'mla_w_in': out['mla_w_in'], 'mla_q_a_gain': out['mla_q_a_gain'], 'mla_kv_a_gain': out['mla_kv_a_gain'], 'mla_w_q_b': out['mla_w_q_b'], 'mla_w_kv_b': out['mla_w_kv_b'], 'mla_q_gain': out['mla_q_gain'], 'mla_k_gain': out['mla_k_gain'], 'mla_w_out': out['mla_w_out'], 'sb_w_in': out['sb_w_in'], 'sb_q_gain': out['sb_q_gain'], 'sb_k_gain': out['sb_k_gain'], 'sb_w_out': out['sb_w_out'], 'sgu_w_in': out['sgu_w_in'], 'sgu_v_gain': out['sgu_v_gain'], 'sgu_w_s': out['sgu_w_s'], 'sgu_b_s': out['sgu_b_s'], 'sgu_w_out': out['sgu_w_out'], 'ffn_w_up': out['ffn_w_up'], 'ffn_conv_w': out['ffn_conv_w'], 'ffn_conv_b': out['ffn_conv_b'], 'ffn_w_down': out['ffn_w_down'], 'loss_target': out['loss_target'], 'm_mix_norm': out['m_mix_norm'], 'm_ffn_norm': out['m_ffn_norm'], 'm_fox_w_in': out['m_fox_w_in'], 'm_fox_b_f': out['m_fox_b_f'], 'm_fox_q_gain': out['m_fox_q_gain'], 'm_fox_k_gain': out['m_fox_k_gain'], 'm_fox_w_out': out['m_fox_w_out'], 'm_mla_w_in': out['m_mla_w_in'], 'm_mla_q_a_gain': out['m_mla_q_a_gain'], 'm_mla_kv_a_gain': out['m_mla_kv_a_gain'], 'm_mla_w_q_b': out['m_mla_w_q_b'], 'm_mla_w_kv_b': out['m_mla_w_kv_b'], 'm_mla_q_gain': out['m_mla_q_gain'], 'm_mla_k_gain': out['m_mla_k_gain'], 'm_mla_w_out': out['m_mla_w_out'], 'm_sb_w_in': out['m_sb_w_in'], 'm_sb_q_gain': out['m_sb_q_gain'], 'm_sb_k_gain': out['m_sb_k_gain'], 'm_sb_w_out': out['m_sb_w_out'], 'm_sgu_w_in': out['m_sgu_w_in'], 'm_sgu_v_gain': out['m_sgu_v_gain'], 'm_sgu_w_s': out['m_sgu_w_s'], 'm_sgu_b_s': out['m_sgu_b_s'], 'm_sgu_w_out': out['m_sgu_w_out'], 'm_ffn_w_up': out['m_ffn_w_up'], 'm_ffn_conv_w': out['m_ffn_conv_w'], 'm_ffn_conv_b': out['m_ffn_conv_b'], 'm_ffn_w_down': out['m_ffn_w_down'], 'v_mix_norm': out['v_mix_norm'], 'v_ffn_norm': out['v_ffn_norm'], 'v_fox_w_in': out['v_fox_w_in'], 'v_fox_b_f': out['v_fox_b_f'], 'v_fox_q_gain': out['v_fox_q_gain'], 'v_fox_k_gain': out['v_fox_k_gain'], 'v_fox_w_out': out['v_fox_w_out'], 'v_mla_w_in': out['v_mla_w_in'], 'v_mla_q_a_gain': out['v_mla_q_a_gain'], 'v_mla_kv_a_gain': out['v_mla_kv_a_gain'], 'v_mla_w_q_b': out['v_mla_w_q_b'], 'v_mla_w_kv_b': out['v_mla_w_kv_b'], 'v_mla_q_gain': out['v_mla_q_gain'], 'v_mla_k_gain': out['v_mla_k_gain'], 'v_mla_w_out': out['v_mla_w_out'], 'v_sb_w_in': out['v_sb_w_in'], 'v_sb_q_gain': out['v_sb_q_gain'], 'v_sb_k_gain': out['v_sb_k_gain'], 'v_sb_w_out': out['v_sb_w_out'], 'v_sgu_w_in': out['v_sgu_w_in'], 'v_sgu_v_gain': out['v_sgu_v_gain'], 'v_sgu_w_s': out['v_sgu_w_s'], 'v_sgu_b_s': out['v_sgu_b_s'], 'v_sgu_w_out': out['v_sgu_w_out'], 'v_ffn_w_up': out['v_ffn_w_up'], 'v_ffn_conv_w': out['v_ffn_conv_w'], 'v_ffn_conv_b': out['v_ffn_conv_b'], 'v_ffn_w_down': out['v_ffn_w_down']}


def _loss(weights, diff, rest, loss_target):
    with _jax.named_scope("forward"):
        args = {**rest, TWIN_DIFF_INPUT: diff, **{k: w.astype(_WEIGHT_DTYPES[k]) for k, w in weights.items()}}
        y = _forward(args)
    with _jax.named_scope("loss_head"):
        err = _jnp.square(y.astype(_jnp.float32) - loss_target)
        return 0.5 * _jnp.sum(_jnp.mean(err, axis=-1)) if err.ndim else 0.5 * err


def _adamw(w, g, m, v):
    m = ADAM_B1 * m + (1.0 - ADAM_B1) * g
    v = ADAM_B2 * v + (1.0 - ADAM_B2) * _jnp.square(g)
    m_hat = m / (1.0 - ADAM_B1 ** ADAM_STEP)
    v_hat = v / (1.0 - ADAM_B2 ** ADAM_STEP)
    delta = -ADAM_LR * (m_hat / (_jnp.sqrt(v_hat) + ADAM_EPS) + ADAM_WD * w)
    return delta, m, v


def reference(x, positions, mix_norm, ffn_norm, fox_w_in, fox_b_f, fox_q_gain, fox_k_gain, fox_w_out, mla_w_in, mla_q_a_gain, mla_kv_a_gain, mla_w_q_b, mla_w_kv_b, mla_q_gain, mla_k_gain, mla_w_out, sb_w_in, sb_q_gain, sb_k_gain, sb_w_out, sgu_w_in, sgu_v_gain, sgu_w_s, sgu_b_s, sgu_w_out, ffn_w_up, ffn_conv_w, ffn_conv_b, ffn_w_down, loss_target, m_mix_norm, m_ffn_norm, m_fox_w_in, m_fox_b_f, m_fox_q_gain, m_fox_k_gain, m_fox_w_out, m_mla_w_in, m_mla_q_a_gain, m_mla_kv_a_gain, m_mla_w_q_b, m_mla_w_kv_b, m_mla_q_gain, m_mla_k_gain, m_mla_w_out, m_sb_w_in, m_sb_q_gain, m_sb_k_gain, m_sb_w_out, m_sgu_w_in, m_sgu_v_gain, m_sgu_w_s, m_sgu_b_s, m_sgu_w_out, m_ffn_w_up, m_ffn_conv_w, m_ffn_conv_b, m_ffn_w_down, v_mix_norm, v_ffn_norm, v_fox_w_in, v_fox_b_f, v_fox_q_gain, v_fox_k_gain, v_fox_w_out, v_mla_w_in, v_mla_q_a_gain, v_mla_kv_a_gain, v_mla_w_q_b, v_mla_w_kv_b, v_mla_q_gain, v_mla_k_gain, v_mla_w_out, v_sb_w_in, v_sb_q_gain, v_sb_k_gain, v_sb_w_out, v_sgu_w_in, v_sgu_v_gain, v_sgu_w_s, v_sgu_b_s, v_sgu_w_out, v_ffn_w_up, v_ffn_conv_w, v_ffn_conv_b, v_ffn_w_down):
    given = dict(x=x, positions=positions, mix_norm=mix_norm, ffn_norm=ffn_norm, fox_w_in=fox_w_in, fox_b_f=fox_b_f, fox_q_gain=fox_q_gain, fox_k_gain=fox_k_gain, fox_w_out=fox_w_out, mla_w_in=mla_w_in, mla_q_a_gain=mla_q_a_gain, mla_kv_a_gain=mla_kv_a_gain, mla_w_q_b=mla_w_q_b, mla_w_kv_b=mla_w_kv_b, mla_q_gain=mla_q_gain, mla_k_gain=mla_k_gain, mla_w_out=mla_w_out, sb_w_in=sb_w_in, sb_q_gain=sb_q_gain, sb_k_gain=sb_k_gain, sb_w_out=sb_w_out, sgu_w_in=sgu_w_in, sgu_v_gain=sgu_v_gain, sgu_w_s=sgu_w_s, sgu_b_s=sgu_b_s, sgu_w_out=sgu_w_out, ffn_w_up=ffn_w_up, ffn_conv_w=ffn_conv_w, ffn_conv_b=ffn_conv_b, ffn_w_down=ffn_w_down, loss_target=loss_target, m_mix_norm=m_mix_norm, m_ffn_norm=m_ffn_norm, m_fox_w_in=m_fox_w_in, m_fox_b_f=m_fox_b_f, m_fox_q_gain=m_fox_q_gain, m_fox_k_gain=m_fox_k_gain, m_fox_w_out=m_fox_w_out, m_mla_w_in=m_mla_w_in, m_mla_q_a_gain=m_mla_q_a_gain, m_mla_kv_a_gain=m_mla_kv_a_gain, m_mla_w_q_b=m_mla_w_q_b, m_mla_w_kv_b=m_mla_w_kv_b, m_mla_q_gain=m_mla_q_gain, m_mla_k_gain=m_mla_k_gain, m_mla_w_out=m_mla_w_out, m_sb_w_in=m_sb_w_in, m_sb_q_gain=m_sb_q_gain, m_sb_k_gain=m_sb_k_gain, m_sb_w_out=m_sb_w_out, m_sgu_w_in=m_sgu_w_in, m_sgu_v_gain=m_sgu_v_gain, m_sgu_w_s=m_sgu_w_s, m_sgu_b_s=m_sgu_b_s, m_sgu_w_out=m_sgu_w_out, m_ffn_w_up=m_ffn_w_up, m_ffn_conv_w=m_ffn_conv_w, m_ffn_conv_b=m_ffn_conv_b, m_ffn_w_down=m_ffn_w_down, v_mix_norm=v_mix_norm, v_ffn_norm=v_ffn_norm, v_fox_w_in=v_fox_w_in, v_fox_b_f=v_fox_b_f, v_fox_q_gain=v_fox_q_gain, v_fox_k_gain=v_fox_k_gain, v_fox_w_out=v_fox_w_out, v_mla_w_in=v_mla_w_in, v_mla_q_a_gain=v_mla_q_a_gain, v_mla_kv_a_gain=v_mla_kv_a_gain, v_mla_w_q_b=v_mla_w_q_b, v_mla_w_kv_b=v_mla_w_kv_b, v_mla_q_gain=v_mla_q_gain, v_mla_k_gain=v_mla_k_gain, v_mla_w_out=v_mla_w_out, v_sb_w_in=v_sb_w_in, v_sb_q_gain=v_sb_q_gain, v_sb_k_gain=v_sb_k_gain, v_sb_w_out=v_sb_w_out, v_sgu_w_in=v_sgu_w_in, v_sgu_v_gain=v_sgu_v_gain, v_sgu_w_s=v_sgu_w_s, v_sgu_b_s=v_sgu_b_s, v_sgu_w_out=v_sgu_w_out, v_ffn_w_up=v_ffn_w_up, v_ffn_conv_w=v_ffn_conv_w, v_ffn_conv_b=v_ffn_conv_b, v_ffn_w_down=v_ffn_w_down)
    weights = {n: given[n] for n in TWIN_WEIGHTS}
    shared = {n: given[n] for n in SHARED_INPUTS}
    per_example = {n: given[n] for n in ['x', 'positions']}
    grad_fn = _jax.value_and_grad(_loss, argnums=(0, 1))

    def one_microbatch(ex, loss_target):
        ex = dict(ex)
        diff = ex.pop(TWIN_DIFF_INPUT)
        return grad_fn(weights, diff, {**shared, **ex}, loss_target)

    if N_MICROBATCH == 1:
        loss, (grad_w, grad_x) = one_microbatch(per_example, given["loss_target"])
    else:
        def body(carry, xs):
            loss_sum, grad_sum = carry
            l_k, (gw_k, gx_k) = one_microbatch(xs[0], xs[1])
            with _jax.named_scope("update"):
                return (loss_sum + l_k, _jax.tree.map(_jnp.add, grad_sum, gw_k)), gx_k

        init = (_jnp.zeros((), _jnp.float32), _jax.tree.map(_jnp.zeros_like, weights))
        (loss, grad_w), grad_x = _jax.lax.scan(body, init, (per_example, given["loss_target"]))
    with _jax.named_scope("update"):
        delta_w, new_m, new_v = {}, {}, {}
        for n in TWIN_WEIGHTS:
            delta_w[n], new_m[n], new_v[n] = _adamw(weights[n], grad_w[n], given["m_" + n], given["v_" + n])
    return (loss, grad_x, *[grad_w[n] for n in TWIN_WEIGHTS], *[delta_w[n] for n in TWIN_WEIGHTS],
            *[new_m[n] for n in TWIN_WEIGHTS], *[new_v[n] for n in TWIN_WEIGHTS])
```

```python
import functools

import jax
import jax.numpy as jnp
from jax import lax
from jax.experimental import pallas as pl
from jax.experimental.pallas import tpu as pltpu

F32 = jnp.float32
BF16 = jnp.bfloat16
LANES = 128
HEAD_DIM = 128
NORM_EPS = 1e-6
MLA_Q_RANK = 512
MLA_KV_RANK = 512
MLA_NOPE = 128
MLA_ROPE = 64
ROPE_THETA = 10000.0
SGU_CHUNK = 128
N_CHIPS = 4
ADAM_LR, ADAM_B1, ADAM_B2, ADAM_EPS, ADAM_WD, ADAM_STEP = 0.001, 0.9, 0.999, 1e-08, 0.01, 10
VMEM_LIMIT_BYTES = 56 * 1024 * 1024
MESH = pl.DeviceIdType.MESH
NEG_BIG = -1e30


def _params(*sem):
    return pltpu.CompilerParams(dimension_semantics=sem, vmem_limit_bytes=VMEM_LIMIT_BYTES)


def _div_block(n, target, mult=LANES):
    if n <= target:
        return n
    best = None
    for b in range(mult, target + 1, mult):
        if n % b == 0:
            best = b
    assert best is not None, (n, target, mult)
    return best


def _iota(shape, dim):
    return lax.broadcasted_iota(jnp.int32, shape, dim)


def _dot(a, b, ca, cb):
    return lax.dot_general(a, b, (((ca,), (cb,)), ((), ())), preferred_element_type=F32)


def _mm(a, b, *, name, ta=False, tb=False, b_sh=None, o_sh=False, res=None, out_dtype=F32,
        bm=1024, bn=1024, bk=512):
    m, k = (a.shape[1], a.shape[0]) if ta else a.shape
    if b_sh == 'n':
        n = b.shape[2] * N_CHIPS
        assert b.shape[1] == k and not tb
    elif b_sh == 'k':
        n = b.shape[1]
        assert b.shape[2] * N_CHIPS == k
    else:
        n = b.shape[0] if tb else b.shape[1]
        assert (b.shape[1] if tb else b.shape[0]) == k
    n_sh = n // N_CHIPS
    k_sh = k // N_CHIPS
    bm = _div_block(m, bm, 8 if not ta else LANES)
    bn = _div_block(n_sh if (b_sh == 'n' or o_sh) else n, bn)
    bk = _div_block(k_sh if b_sh == 'k' else k, bk)
    nbs = n_sh // bn if (b_sh == 'n' or o_sh) else 1
    nks = k_sh // bk if b_sh == 'k' else 1
    nk = k // bk

    a_spec = pl.BlockSpec((bk, bm), lambda i, j, q: (q, i)) if ta else pl.BlockSpec((bm, bk), lambda i, j, q: (i, q))
    if b_sh == 'n':
        b_spec = pl.BlockSpec((None, bk, bn), lambda i, j, q: (j // nbs, q, j % nbs))
    elif b_sh == 'k':
        b_spec = pl.BlockSpec((None, bn, bk), lambda i, j, q: (q // nks, j, q % nks))
    elif tb:
        b_spec = pl.BlockSpec((bn, bk), lambda i, j, q: (j, q))
    else:
        b_spec = pl.BlockSpec((bk, bn), lambda i, j, q: (q, j))
    if o_sh:
        o_spec = pl.BlockSpec((None, bm, bn), lambda i, j, q: (j // nbs, i, j % nbs))
        o_shape = jax.ShapeDtypeStruct((N_CHIPS, m, n_sh), out_dtype)
    else:
        o_spec = pl.BlockSpec((bm, bn), lambda i, j, q: (i, j))
        o_shape = jax.ShapeDtypeStruct((m, n), out_dtype)
    tb_eff = tb or b_sh == 'k'

    def body(a_ref, b_ref, *rest):
        if res is not None:
            r_ref, o_ref, acc = rest
        else:
            o_ref, acc = rest
        q = pl.program_id(2)

        @pl.when(q == 0)
        def _():
            acc[...] = jnp.zeros_like(acc)

        acc[...] += _dot(a_ref[...].astype(BF16), b_ref[...].astype(BF16), 0 if ta else 1, 1 if tb_eff else 0)

        @pl.when(q == nk - 1)
        def _():
            r = acc[...]
            if res is not None:
                r = r + r_ref[...].astype(F32)
            o_ref[...] = r.astype(out_dtype)

    ins = [a, b]
    in_specs = [a_spec, b_spec]
    if res is not None:
        assert not o_sh
        ins.append(res)
        in_specs.append(pl.BlockSpec((bm, bn), lambda i, j, q: (i, j)))
    return pl.pallas_call(
        body, name=name, grid=(m // bm, n // bn, nk), in_specs=in_specs, out_specs=o_spec, out_shape=o_shape,
        scratch_shapes=[pltpu.VMEM((bm, bn), F32)],
        compiler_params=_params("parallel", "parallel", "arbitrary"))(*ins)


def _rms_fwd(x, g, *, name, out_dtype=BF16, br=256):
    r, c = x.shape
    br = _div_block(r, br, 8)

    def body(x_ref, g_ref, o_ref):
        xv = x_ref[...].astype(F32)
        inv = lax.rsqrt(jnp.mean(xv * xv, axis=-1, keepdims=True) + NORM_EPS)
        o_ref[...] = (xv * inv * g_ref[...]).astype(out_dtype)

    return pl.pallas_call(
        body, name=name, grid=(r // br,),
        in_specs=[pl.BlockSpec((br, c), lambda i: (i, 0)), pl.BlockSpec((1, c), lambda i: (0, 0))],
        out_specs=pl.BlockSpec((br, c), lambda i: (i, 0)), out_shape=jax.ShapeDtypeStruct((r, c), out_dtype),
        compiler_params=_params("parallel"))(x, g)


def _rms_bwd_math(xv, gv, dyv, n):
    inv = lax.rsqrt(jnp.sum(xv * xv, axis=-1, keepdims=True) / n + NORM_EPS)
    xh = xv * inv
    dyg = dyv * gv
    dx = inv * (dyg - xh * (jnp.sum(dyg * xh, axis=-1, keepdims=True) / n))
    return dx, dyv * xh


def _rms_bwd(x, g, dy, *, name, res=None, br=256):
    r, c = x.shape
    br = _div_block(r, br, 8)

    def body(x_ref, g_ref, dy_ref, *rest):
        if res is not None:
            r_ref, dx_ref, dg_ref = rest
        else:
            dx_ref, dg_ref = rest
        dx, dgr = _rms_bwd_math(x_ref[...].astype(F32), g_ref[...], dy_ref[...].astype(F32), c)
        if res is not None:
            dx = dx + r_ref[...]
        dx_ref[...] = dx

        @pl.when(pl.program_id(0) == 0)
        def _():
            dg_ref[...] = jnp.zeros_like(dg_ref)

        dg_ref[...] += jnp.sum(dgr, axis=0, keepdims=True)

    row = pl.BlockSpec((br, c), lambda i: (i, 0))
    vec = pl.BlockSpec((1, c), lambda i: (0, 0))
    ins = [x, g, dy] + ([res] if res is not None else [])
    return pl.pallas_call(
        body, name=name, grid=(r // br,), in_specs=[row, vec, row] + ([row] if res is not None else []),
        out_specs=[row, vec], out_shape=[jax.ShapeDtypeStruct((r, c), F32), jax.ShapeDtypeStruct((1, c), F32)],
        compiler_params=_params("arbitrary"))(*ins)


def _loss(y, target, *, name="loss", br=256):
    r, c = y.shape
    br = _div_block(r, br, 8)

    def body(y_ref, t_ref, l_ref, dy_ref):
        d = y_ref[...] - t_ref[...]
        dy_ref[...] = d * (1.0 / c)

        @pl.when(pl.program_id(0) == 0)
        def _():
            l_ref[...] = jnp.zeros_like(l_ref)

        part = jnp.sum(d * d, axis=0, keepdims=True)
        l_ref[...] += (0.5 / c) * jnp.sum(part, axis=1, keepdims=True) * jnp.ones((1, LANES), F32)

    row = pl.BlockSpec((br, c), lambda i: (i, 0))
    return pl.pallas_call(
        body, name=name, grid=(r // br,), in_specs=[row, row],
        out_specs=[pl.BlockSpec((1, LANES), lambda i: (0, 0)), row],
        out_shape=[jax.ShapeDtypeStruct((1, LANES), F32), jax.ShapeDtypeStruct((r, c), F32)],
        compiler_params=_params("arbitrary"))(y, target)


def _split2(x):
    hi = x.astype(BF16)
    lo = (x - hi.astype(F32)).astype(BF16)
    return hi, lo


def _lane_scan(x, *, suffix):
    rows, n = x.shape
    nb = n // LANES
    a, b = _iota((LANES, LANES), 0), _iota((LANES, LANES), 1)
    tri = ((a > b) if suffix else (a < b)).astype(BF16)
    outs = [None] * nb
    run = jnp.zeros((rows, 1), F32)
    order = range(nb - 1, -1, -1) if suffix else range(nb)
    for blk in order:
        xb = x[:, blk * LANES:(blk + 1) * LANES]
        hi, lo = _split2(xb)
        outs[blk] = _dot(hi, tri, 1, 0) + _dot(lo, tri, 1, 0) + run
        run = run + jnp.sum(xb, axis=-1, keepdims=True)
    return jnp.concatenate(outs, axis=1)


def _softplus(z):
    return jnp.maximum(z, 0.0) + jnp.log(1.0 + jnp.exp(-jnp.abs(z)))


def _head_norm(x, g):
    xv = x.astype(F32)
    inv = lax.rsqrt(jnp.mean(xv * xv, axis=-1, keepdims=True) + NORM_EPS)
    return xv * inv * g


def _attn_weights(kind, qn, kn, scale, qi, bq, bias):
    s = _dot(qn, kn, 1, 1) * scale
    row = qi * bq + _iota(s.shape, 0)
    col = _iota(s.shape, 1)
    if kind == 'sb':
        strict = col < row
        sp = _softplus(s)
        after = _lane_scan(jnp.where(strict, -sp, 0.0), suffix=True)
        w = jnp.where(strict, jnp.exp(s - sp + after), 0.0)
        return w, (strict, s - sp)
    if bias is not None:
        s = s + bias
    s = jnp.where(col <= row, s, NEG_BIG)
    mx = jnp.max(s, axis=-1, keepdims=True)
    e = jnp.exp(s - mx)
    return e, jnp.sum(e, axis=-1, keepdims=True)


def _attn_fwd(kind, q, k, v, *, name, n_heads, dqk, qcol, kcol, vcol, scale, gains=None, cq=None, ck=None, bq=256):
    s_len = q.shape[0]
    bq = _div_block(s_len, bq, 8)
    norm, fox = gains is not None, cq is not None

    def body(*refs):
        refs = list(refs)
        q_ref, k_ref, v_ref = refs[:3]
        rest = refs[3:]
        g_ref = rest.pop(0) if norm else None
        cq_ref, ck_ref = (rest.pop(0), rest.pop(0)) if fox else (None, None)
        o_ref, = rest
        qi = pl.program_id(1)
        if norm:
            qn = _head_norm(q_ref[...], g_ref[0]).astype(BF16)
            kn = _head_norm(k_ref[...], g_ref[1]).astype(BF16)
        else:
            qn, kn = q_ref[...].astype(BF16), k_ref[...].astype(BF16)
        bias = (cq_ref[...] - ck_ref[...]) if fox else None
        w, aux = _attn_weights(kind, qn, kn, scale, qi, bq, bias)
        o = _dot(w.astype(BF16), v_ref[...].astype(BF16), 1, 0)
        if kind != 'sb':
            o = o / aux
        o_ref[...] = o.astype(BF16)

    in_specs = [pl.BlockSpec((bq, dqk), lambda h, i: (i, qcol(h))),
                pl.BlockSpec((s_len, dqk), lambda h, i: (0, kcol(h))),
                pl.BlockSpec((s_len, HEAD_DIM), lambda h, i: (0, vcol(h)))]
    ins = [q, k, v]
    if norm:
        in_specs.append(pl.BlockSpec((2, 1, dqk), lambda h, i: (0, 0, 0)))
        ins.append(gains)
    if fox:
        in_specs += [pl.BlockSpec((None, bq, 1), lambda h, i: (h, i, 0)), pl.BlockSpec((None, 1, s_len), lambda h, i: (h, 0, 0))]
        ins += [cq, ck]
    return pl.pallas_call(
        body, name=name, grid=(n_heads, s_len // bq), in_specs=in_specs,
        out_specs=pl.BlockSpec((bq, HEAD_DIM), lambda h, i: (i, h)),
        out_shape=jax.ShapeDtypeStruct((s_len, n_heads * HEAD_DIM), BF16),
        compiler_params=_params("parallel", "parallel"))(*ins)


def _attn_bwd(kind, q, k, v, o, do, *, name, n_heads, dqk, qcol, kcol, vcol, scale, gains=None, cq=None, ck=None,
              bq=256):
    s_len = q.shape[0]
    bq = _div_block(s_len, bq, 8)
    nq = s_len // bq
    norm, fox = gains is not None, cq is not None

    def body(*refs):
        refs = list(refs)
        q_ref, k_ref, v_ref, o_ref, do_ref = refs[:5]
        rest = refs[5:]
        g_ref = rest.pop(0) if norm else None
        cq_ref, ck_ref = (rest.pop(0), rest.pop(0)) if fox else (None, None)
        dq_ref, dk_ref, dv_ref = rest.pop(0), rest.pop(0), rest.pop(0)
        dg_ref = rest.pop(0) if norm else None
        dcq_ref, dck_ref = (rest.pop(0), rest.pop(0)) if fox else (None, None)
        dk_acc, dv_acc = rest
        h, qi = pl.program_id(0), pl.program_id(1)

        @pl.when(qi == 0)
        def _():
            dk_acc[...] = jnp.zeros_like(dk_acc)
            dv_acc[...] = jnp.zeros_like(dv_acc)
            if fox:
                dck_ref[...] = jnp.zeros_like(dck_ref)

        if norm:
            @pl.when((qi == 0) & (h == 0))
            def _():
                dg_ref[...] = jnp.zeros_like(dg_ref)

            qn = _head_norm(q_ref[...], g_ref[0]).astype(BF16)
            kn = _head_norm(k_ref[...], g_ref[1]).astype(BF16)
        else:
            qn, kn = q_ref[...].astype(BF16), k_ref[...].astype(BF16)
        vb = v_ref[...].astype(BF16)
        dob = do_ref[...].astype(BF16)
        bias = (cq_ref[...] - ck_ref[...]) if fox else None
        w, aux = _attn_weights(kind, qn, kn, scale, qi, bq, bias)
        dw = _dot(dob, vb, 1, 1)
        if kind == 'sb':
            strict, log_sig = aux
            g = dw * w
            cc = _lane_scan(g, suffix=False)
            sig = jnp.exp(log_sig)
            ds = jnp.where(strict, g * (1.0 - sig) - cc * sig, 0.0)
            pw = w
        else:
            pw = w / aux
            delta = jnp.sum(do_ref[...].astype(F32) * o_ref[...].astype(F32), axis=-1, keepdims=True)
            ds = pw * (dw - delta)
            if fox:
                dcq_ref[...] = jnp.sum(ds, axis=1, keepdims=True)
                dck_ref[...] -= jnp.sum(ds, axis=0, keepdims=True)
        dsb = (ds * scale).astype(BF16)
        dqn = _dot(dsb, kn, 1, 0)
        dk_acc[...] += _dot(dsb, qn, 0, 0)
        dv_acc[...] += _dot(pw.astype(BF16), dob, 0, 0)
        if norm:
            dq, dgr = _rms_bwd_math(q_ref[...].astype(F32), g_ref[0], dqn, dqk)
            dg_ref[0] += jnp.sum(dgr, axis=0, keepdims=True)
            dq_ref[...] = dq.astype(BF16)
        else:
            dq_ref[...] = dqn.astype(BF16)

        @pl.when(qi == nq - 1)
        def _():
            if norm:
                dk, dgr = _rms_bwd_math(k_ref[...].astype(F32), g_ref[1], dk_acc[...], dqk)
                dg_ref[1] += jnp.sum(dgr, axis=0, keepdims=True)
                dk_ref[...] = dk.astype(BF16)
            else:
                dk_ref[...] = dk_acc[...].astype(BF16)
            dv_ref[...] = dv_acc[...].astype(BF16)

    in_specs = [pl.BlockSpec((bq, dqk), lambda h, i: (i, qcol(h))),
                pl.BlockSpec((s_len, dqk), lambda h, i: (0, kcol(h))),
                pl.BlockSpec((s_len, HEAD_DIM), lambda h, i: (0, vcol(h))),
                pl.BlockSpec((bq, HEAD_DIM), lambda h, i: (i, h)),
                pl.BlockSpec((bq, HEAD_DIM), lambda h, i: (i, h))]
    ins = [q, k, v, o, do]
    out_specs = [pl.BlockSpec((bq, dqk), lambda h, i: (i, h)),
                 pl.BlockSpec((s_len, dqk), lambda h, i: (0, h)),
                 pl.BlockSpec((s_len, HEAD_DIM), lambda h, i: (0, h))]
    out_shape = [jax.ShapeDtypeStruct((s_len, n_heads * dqk), BF16), jax.ShapeDtypeStruct((s_len, n_heads * dqk), BF16),
                 jax.ShapeDtypeStruct((s_len, n_heads * HEAD_DIM), BF16)]
    if norm:
        in_specs.append(pl.BlockSpec((2, 1, dqk), lambda h, i: (0, 0, 0)))
        ins.append(gains)
        out_specs.append(pl.BlockSpec((2, 1, dqk), lambda h, i: (0, 0, 0)))
        out_shape.append(jax.ShapeDtypeStruct((2, 1, dqk), F32))
    if fox:
        in_specs += [pl.BlockSpec((None, bq, 1), lambda h, i: (h, i, 0)), pl.BlockSpec((None, 1, s_len), lambda h, i: (h, 0, 0))]
        ins += [cq, ck]
        out_specs += [pl.BlockSpec((None, bq, 1), lambda h, i: (h, i, 0)), pl.BlockSpec((None, 1, s_len), lambda h, i: (h, 0, 0))]
        out_shape += [jax.ShapeDtypeStruct((n_heads, s_len, 1), F32), jax.ShapeDtypeStruct((n_heads, 1, s_len), F32)]
    return pl.pallas_call(
        body, name=name, grid=(n_heads, nq), in_specs=in_specs, out_specs=out_specs, out_shape=out_shape,
        scratch_shapes=[pltpu.VMEM((s_len, dqk), F32), pltpu.VMEM((s_len, HEAD_DIM), F32)],
        compiler_params=_params("arbitrary", "arbitrary"))(*ins)


def _split3(x):
    hi = x.astype(BF16)
    r1 = x - hi.astype(F32)
    mid = r1.astype(BF16)
    lo = (r1 - mid.astype(F32)).astype(BF16)
    return hi, mid, lo


def _seq_scan(x, *, reverse):
    n = x.shape[0] // LANES
    a, b = _iota((LANES, LANES), 0), _iota((LANES, LANES), 1)
    tri = ((b >= a) if reverse else (b <= a)).astype(BF16)
    outs = [None] * n
    run = jnp.zeros((1, x.shape[1]), F32)
    for blk in (range(n - 1, -1, -1) if reverse else range(n)):
        xb = x[blk * LANES:(blk + 1) * LANES, :]
        hi, mid, lo = _split3(xb)
        outs[blk] = _dot(tri, hi, 1, 0) + _dot(tri, mid, 1, 0) + _dot(tri, lo, 1, 0) + run
        run = run + jnp.sum(xb, axis=0, keepdims=True)
    return jnp.concatenate(outs, axis=0)


def _fgate_fwd(qkvf, b_f, *, fcol, name):
    s_len = qkvf.shape[0]

    def body(f_ref, b_ref, cum_ref):
        z = f_ref[...] + b_ref[...]
        cum_ref[...] = _seq_scan(-_softplus(-z), reverse=False)

    return pl.pallas_call(
        body, name=name, grid=(1,),
        in_specs=[pl.BlockSpec((s_len, LANES), lambda i: (0, fcol)), pl.BlockSpec((1, LANES), lambda i: (0, 0))],
        out_specs=pl.BlockSpec((s_len, LANES), lambda i: (0, 0)), out_shape=jax.ShapeDtypeStruct((s_len, LANES), F32),
        compiler_params=_params("arbitrary"))(qkvf, b_f)


def _fgate_bwd(qkvf, b_f, dcum_a, dcum_b, *, fcol, n_heads, name):
    s_len = qkvf.shape[0]

    def body(f_ref, b_ref, da_ref, db_ref, dz_ref, dbias_ref):
        z = f_ref[...] + b_ref[...]
        dlog = _seq_scan(da_ref[...] + db_ref[...], reverse=True)
        dz = dlog * jnp.exp(-_softplus(z))
        dz = jnp.where(_iota(dz.shape, 1) < n_heads, dz, 0.0)
        dz_ref[...] = dz.astype(BF16)
        dbias_ref[...] = jnp.sum(dz, axis=0, keepdims=True)

    full = pl.BlockSpec((s_len, LANES), lambda i: (0, 0))
    vec = pl.BlockSpec((1, LANES), lambda i: (0, 0))
    return pl.pallas_call(
        body, name=name, grid=(1,),
        in_specs=[pl.BlockSpec((s_len, LANES), lambda i: (0, fcol)), vec, full, full],
        out_specs=[full, vec], out_shape=[jax.ShapeDtypeStruct((s_len, LANES), BF16), jax.ShapeDtypeStruct((1, LANES), F32)],
        compiler_params=_params("arbitrary"))(qkvf, b_f, dcum_a, dcum_b)


def _rope_swap(x):
    half = MLA_ROPE // 2
    lane = _iota(x.shape, 1)
    sw = jnp.where(lane < half, pltpu.roll(x, LANES - half, axis=1), pltpu.roll(x, half, axis=1))
    return jnp.where(lane < MLA_ROPE, sw, 0.0)


def _mla_prep_fwd(qp, kv, c, gq, gk, cos_t, sin_t, *, n_heads, name, bs=512):
    s_len = qp.shape[0]
    bs = _div_block(s_len, bs, 8)
    krope_col = (MLA_Q_RANK + MLA_KV_RANK) // LANES

    def body(qn_ref, qr_ref, kn_ref, kr_ref, gq_ref, gk_ref, cos_ref, sin_ref, qc_ref, kc_ref):
        cos_v, sin_v = cos_ref[...], sin_ref[...]

        def rope(x, g):
            xv = x.astype(F32)
            inv = lax.rsqrt(jnp.sum(xv * xv, axis=-1, keepdims=True) / MLA_ROPE + NORM_EPS)
            y = xv * inv * g
            return y * cos_v + _rope_swap(y) * sin_v

        qc_ref[:, :LANES] = _head_norm(qn_ref[...], gq_ref[0]).astype(BF16)
        qc_ref[:, LANES:] = rope(qr_ref[...], gq_ref[1]).astype(BF16)
        kc_ref[:, :LANES] = _head_norm(kn_ref[...], gk_ref[0]).astype(BF16)
        kc_ref[:, LANES:] = rope(kr_ref[...], gk_ref[1]).astype(BF16)

    blk = lambda f: pl.BlockSpec((bs, LANES), f)
    gspec = pl.BlockSpec((2, 1, LANES), lambda i, h: (0, 0, 0))
    tspec = pl.BlockSpec((bs, LANES), lambda i, h: (i, 0))
    ospec = pl.BlockSpec((bs, 2 * LANES), lambda i, h: (i, h))
    oshape = jax.ShapeDtypeStruct((s_len, n_heads * 2 * LANES), BF16)
    return pl.pallas_call(
        body, name=name, grid=(s_len // bs, n_heads),
        in_specs=[blk(lambda i, h: (i, h)), blk(lambda i, h: (i, n_heads + h)), blk(lambda i, h: (i, 2 * h)),
                  blk(lambda i, h: (i, krope_col)), gspec, gspec, tspec, tspec],
        out_specs=[ospec, ospec], out_shape=[oshape, oshape],
        compiler_params=_params("parallel", "parallel"))(qp, qp, kv, c, gq, gk, cos_t, sin_t)


def _mla_prep_bwd(qp, kv, c, gq, gk, cos_t, sin_t, dqc, dkc, dv, *, n_heads, name, bs=512):
    s_len = qp.shape[0]
    bs = _div_block(s_len, bs, 8)
    krope_col = (MLA_Q_RANK + MLA_KV_RANK) // LANES

    def body(qn_ref, qr_ref, kn_ref, kr_ref, gq_ref, gk_ref, cos_ref, sin_ref, dqc_ref, dkc_ref, dv_ref,
             dqn_ref, dqr_ref, dkv_ref, dkr_ref, dgq_ref, dgk_ref):
        i, h = pl.program_id(0), pl.program_id(1)
        cos_v, sin_v = cos_ref[...], sin_ref[...]

        @pl.when((i == 0) & (h == 0))
        def _():
            dgq_ref[...] = jnp.zeros_like(dgq_ref)
            dgk_ref[...] = jnp.zeros_like(dgk_ref)

        @pl.when(h == 0)
        def _():
            dkr_ref[...] = jnp.zeros_like(dkr_ref)

        def unrope(dy):
            dy = dy.astype(F32)
            return dy * cos_v + _rope_swap(dy * sin_v)

        dqn, dg = _rms_bwd_math(qn_ref[...].astype(F32), gq_ref[0], dqc_ref[:, :LANES].astype(F32), MLA_NOPE)
        dgq_ref[0] += jnp.sum(dg, axis=0, keepdims=True)
        dqn_ref[...] = dqn.astype(BF16)
        dqr, dg = _rms_bwd_math(qr_ref[...].astype(F32), gq_ref[1], unrope(dqc_ref[:, LANES:]), MLA_ROPE)
        dgq_ref[1] += jnp.sum(dg, axis=0, keepdims=True)
        dqr_ref[...] = dqr.astype(BF16)
        dkn, dg = _rms_bwd_math(kn_ref[...].astype(F32), gk_ref[0], dkc_ref[:, :LANES].astype(F32), MLA_NOPE)
        dgk_ref[0] += jnp.sum(dg, axis=0, keepdims=True)
        dkv_ref[:, :LANES] = dkn.astype(BF16)
        dkv_ref[:, LANES:] = dv_ref[...]
        dkr, dg = _rms_bwd_math(kr_ref[...].astype(F32), gk_ref[1], unrope(dkc_ref[:, LANES:]), MLA_ROPE)
        dgk_ref[1] += jnp.sum(dg, axis=0, keepdims=True)
        dkr_ref[...] += dkr

    blk = lambda f: pl.BlockSpec((bs, LANES), f)
    gspec = pl.BlockSpec((2, 1, LANES), lambda i, h: (0, 0, 0))
    tspec = pl.BlockSpec((bs, LANES), lambda i, h: (i, 0))
    cat = pl.BlockSpec((bs, 2 * LANES), lambda i, h: (i, h))
    head = blk(lambda i, h: (i, h))
    hshape = jax.ShapeDtypeStruct((s_len, n_heads * LANES), BF16)
    gshape = jax.ShapeDtypeStruct((2, 1, LANES), F32)
    return pl.pallas_call(
        body, name=name, grid=(s_len // bs, n_heads),
        in_specs=[head, blk(lambda i, h: (i, n_heads + h)), blk(lambda i, h: (i, 2 * h)),
                  blk(lambda i, h: (i, krope_col)), gspec, gspec, tspec, tspec, cat, cat, head],
        out_specs=[head, head, cat, tspec, gspec, gspec],
        out_shape=[hshape, hshape, jax.ShapeDtypeStruct((s_len, n_heads * 2 * LANES), BF16),
                   jax.ShapeDtypeStruct((s_len, LANES), F32), gshape, gshape],
        compiler_params=_params("arbitrary", "arbitrary"))(qp, qp, kv, c, gq, gk, cos_t, sin_t, dqc, dkc, dv)


def _mla_latent_fwd(c, ga, *, name, br=256):
    s_len = c.shape[0]
    br = _div_block(s_len, br, 8)

    def body(c_ref, g_ref, o_ref):
        for part in range(2):
            sl = slice(part * MLA_Q_RANK, (part + 1) * MLA_Q_RANK)
            o_ref[:, sl] = _head_norm(c_ref[:, sl], g_ref[:, sl]).astype(BF16)

    w = MLA_Q_RANK + MLA_KV_RANK
    return pl.pallas_call(
        body, name=name, grid=(s_len // br,),
        in_specs=[pl.BlockSpec((br, w), lambda i: (i, 0)), pl.BlockSpec((1, w), lambda i: (0, 0))],
        out_specs=pl.BlockSpec((br, w), lambda i: (i, 0)), out_shape=jax.ShapeDtypeStruct((s_len, w), BF16),
        compiler_params=_params("parallel"))(c, ga)


def _mla_latent_bwd(c, ga, dcn_q, dcn_kv, dk_rope, *, name, br=256):
    s_len, cw = c.shape
    br = _div_block(s_len, br, 8)
    w = MLA_Q_RANK + MLA_KV_RANK

    def body(c_ref, g_ref, dq_ref, dkv_ref, dkr_ref, dc_ref, dg_ref):
        @pl.when(pl.program_id(0) == 0)
        def _():
            dg_ref[...] = jnp.zeros_like(dg_ref)

        for part, d_ref in enumerate((dq_ref, dkv_ref)):
            sl = slice(part * MLA_Q_RANK, (part + 1) * MLA_Q_RANK)
            dx, dg = _rms_bwd_math(c_ref[:, sl].astype(F32), g_ref[:, sl], d_ref[...].astype(F32), MLA_Q_RANK)
            dc_ref[:, sl] = dx.astype(BF16)
            dg_ref[:, sl] += jnp.sum(dg, axis=0, keepdims=True)
        dc_ref[:, w:] = dkr_ref[...].astype(BF16)

    return pl.pallas_call(
        body, name=name, grid=(s_len // br,),
        in_specs=[pl.BlockSpec((br, w), lambda i: (i, 0)), pl.BlockSpec((1, w), lambda i: (0, 0)),
                  pl.BlockSpec((br, MLA_Q_RANK), lambda i: (i, 0)), pl.BlockSpec((br, MLA_KV_RANK), lambda i: (i, 0)),
                  pl.BlockSpec((br, LANES), lambda i: (i, 0))],
        out_specs=[pl.BlockSpec((br, cw), lambda i: (i, 0)), pl.BlockSpec((1, w), lambda i: (0, 0))],
        out_shape=[jax.ShapeDtypeStruct((s_len, cw), BF16), jax.ShapeDtypeStruct((1, w), F32)],
        compiler_params=_params("arbitrary"))(c, ga, dcn_q, dcn_kv, dk_rope)


_GELU_C = 0.7978845608028654


def _gelu(x):
    return 0.5 * x * (1.0 + jnp.tanh(_GELU_C * (x + 0.044715 * x * x * x)))


def _gelu_grad(x):
    t = jnp.tanh(_GELU_C * (x + 0.044715 * x * x * x))
    return 0.5 * (1.0 + t) + 0.5 * x * (1.0 - t * t) * _GELU_C * (1.0 + 3 * 0.044715 * x * x)


def _sgu_act_fwd(uv, vg, *, name, br=256):
    s_len, w2 = uv.shape
    w = w2 // 2
    br = _div_block(s_len, br, 8)

    def body(uv_ref, g_ref, u_ref, v_ref):
        u_ref[...] = _gelu(uv_ref[:, :w])
        v_ref[...] = _head_norm(_gelu(uv_ref[:, w:]), g_ref[...]).astype(BF16)

    row = lambda c: pl.BlockSpec((br, c), lambda i: (i, 0))
    return pl.pallas_call(
        body, name=name, grid=(s_len // br,), in_specs=[row(w2), pl.BlockSpec((1, w), lambda i: (0, 0))],
        out_specs=[row(w), row(w)], out_shape=[jax.ShapeDtypeStruct((s_len, w), F32), jax.ShapeDtypeStruct((s_len, w), BF16)],
        compiler_params=_params("parallel"))(uv, vg)


def _sgu_act_bwd(uv, vg, du, dvn, *, name, br=256):
    s_len, w2 = uv.shape
    w = w2 // 2
    br = _div_block(s_len, br, 8)

    def body(uv_ref, g_ref, du_ref, dvn_ref, duv_ref, dg_ref):
        @pl.when(pl.program_id(0) == 0)
        def _():
            dg_ref[...] = jnp.zeros_like(dg_ref)

        up, vp = uv_ref[:, :w], uv_ref[:, w:]
        duv_ref[:, :w] = (du_ref[...] * _gelu_grad(up)).astype(BF16)
        dva, dg = _rms_bwd_math(_gelu(vp), g_ref[...], dvn_ref[...], w)
        dg_ref[...] += jnp.sum(dg, axis=0, keepdims=True)
        duv_ref[:, w:] = (dva * _gelu_grad(vp)).astype(BF16)

    row = lambda c: pl.BlockSpec((br, c), lambda i: (i, 0))
    vec = pl.BlockSpec((1, w), lambda i: (0, 0))
    return pl.pallas_call(
        body, name=name, grid=(s_len // br,), in_specs=[row(w2), vec, row(w), row(w)], out_specs=[row(w2), vec],
        out_shape=[jax.ShapeDtypeStruct((s_len, w2), BF16), jax.ShapeDtypeStruct((1, w), F32)],
        compiler_params=_params("arbitrary"))(uv, vg, du, dvn)


def _tril_weights(ws_ref):
    t, s = _iota((SGU_CHUNK, SGU_CHUNK), 0), _iota((SGU_CHUNK, SGU_CHUNK), 1)
    keep = s <= t
    return jnp.where(keep, ws_ref[...], 0.0), keep


def _sgu_mix_fwd(u, vn, w_s, b_s, *, name):
    s_len, w = u.shape
    nc = s_len // SGU_CHUNK

    def body(u_ref, v_ref, ws_ref, b_ref, o_ref):
        wm = _tril_weights(ws_ref)[0].astype(BF16)
        for n in range(nc):
            rows = slice(n * SGU_CHUNK, (n + 1) * SGU_CHUNK)
            mixed = _dot(wm, v_ref[rows, :], 1, 0) + b_ref[...]
            o_ref[rows, :] = (u_ref[rows, :] * mixed).astype(BF16)

    col = pl.BlockSpec((s_len, LANES), lambda g: (0, g))
    return pl.pallas_call(
        body, name=name, grid=(w // LANES,),
        in_specs=[col, col, pl.BlockSpec((None, SGU_CHUNK, SGU_CHUNK), lambda g: (g, 0, 0)),
                  pl.BlockSpec((None, SGU_CHUNK, 1), lambda g: (g, 0, 0))],
        out_specs=col, out_shape=jax.ShapeDtypeStruct((s_len, w), BF16),
        compiler_params=_params("parallel"))(u, vn, w_s, b_s)


def _sgu_mix_bwd(u, vn, w_s, b_s, dgated, *, name):
    s_len, w = u.shape
    nc = s_len // SGU_CHUNK

    def body(u_ref, v_ref, ws_ref, b_ref, dg_ref, du_ref, dv_ref, dws_ref, dbs_ref):
        wf, keep = _tril_weights(ws_ref)
        wm = wf.astype(BF16)
        wmt = wf.T.astype(BF16)
        dws = jnp.zeros((SGU_CHUNK, SGU_CHUNK), F32)
        dbs = jnp.zeros((SGU_CHUNK, 1), F32)
        for n in range(nc):
            rows = slice(n * SGU_CHUNK, (n + 1) * SGU_CHUNK)
            vb = v_ref[rows, :]
            dgv = dg_ref[rows, :].astype(F32)
            mixed = _dot(wm, vb, 1, 0) + b_ref[...]
            du_ref[rows, :] = dgv * mixed
            dm = dgv * u_ref[rows, :]
            dmb = dm.astype(BF16)
            dws = dws + _dot(dmb, vb, 1, 1)
            dbs = dbs + jnp.sum(dm, axis=1, keepdims=True)
            dv_ref[rows, :] = _dot(wmt, dmb, 1, 0)
        dws_ref[...] = jnp.where(keep, dws, 0.0)
        dbs_ref[...] = dbs

    col = pl.BlockSpec((s_len, LANES), lambda g: (0, g))
    wspec = pl.BlockSpec((None, SGU_CHUNK, SGU_CHUNK), lambda g: (g, 0, 0))
    bspec = pl.BlockSpec((None, SGU_CHUNK, 1), lambda g: (g, 0, 0))
    return pl.pallas_call(
        body, name=name, grid=(w // LANES,), in_specs=[col, col, wspec, bspec, col],
        out_specs=[col, col, wspec, bspec],
        out_shape=[jax.ShapeDtypeStruct((s_len, w), F32), jax.ShapeDtypeStruct((s_len, w), F32),
                   jax.ShapeDtypeStruct(w_s.shape, F32), jax.ShapeDtypeStruct(b_s.shape, F32)],
        compiler_params=_params("parallel"))(u, vn, w_s, b_s, dgated)


def _shift_down(x, k):
    if k == 0:
        return x
    return jnp.where(_iota(x.shape, 0) >= k, pltpu.roll(x, k, axis=0), 0.0)


def _shift_up(x, k):
    if k == 0:
        return x
    n = x.shape[0]
    return jnp.where(_iota(x.shape, 0) < n - k, pltpu.roll(x, n - k, axis=0), 0.0)


def _conv(u, w_ref, b_ref):
    return b_ref[...] + w_ref[0:1, :] * _shift_down(u, 2) + w_ref[1:2, :] * _shift_down(u, 1) + w_ref[2:3, :] * u


def _sigmoid(x):
    return 1.0 / (1.0 + jnp.exp(-x))


def _glu_fwd(up, cw, cb, *, name, bc=256):
    s_len, f2 = up.shape
    f = f2 // 2
    bc = _div_block(f, bc)
    nf = f // bc

    def body(ug_ref, uv_ref, wg_ref, wv_ref, bg_ref, bv_ref, o_ref):
        yg = _conv(ug_ref[...], wg_ref, bg_ref)
        yv = _conv(uv_ref[...], wv_ref, bv_ref)
        o_ref[...] = (yg * _sigmoid(yg) * yv).astype(BF16)

    big = lambda off: pl.BlockSpec((s_len, bc), lambda j: (0, j + off))
    wsp = lambda off: pl.BlockSpec((3, bc), lambda j: (0, j + off))
    bsp = lambda off: pl.BlockSpec((1, bc), lambda j: (0, j + off))
    return pl.pallas_call(
        body, name=name, grid=(nf,), in_specs=[big(0), big(nf), wsp(0), wsp(nf), bsp(0), bsp(nf)],
        out_specs=pl.BlockSpec((s_len, bc), lambda j: (0, j)), out_shape=jax.ShapeDtypeStruct((s_len, f), BF16),
        compiler_params=_params("parallel"))(up, up, cw, cw, cb, cb)


def _glu_bwd(up, cw, cb, dact, *, name, bc=256):
    s_len, f2 = up.shape
    f = f2 // 2
    bc = _div_block(f, bc)
    nf = f // bc

    def body(ut_ref, up_ref, wt_ref, wp_ref, bt_ref, bp_ref, da_ref, du_ref, dw_ref, db_ref):
        is_gate = pl.program_id(0) < nf
        ut = ut_ref[...]
        yt = _conv(ut, wt_ref, bt_ref)
        yp = _conv(up_ref[...], wp_ref, bp_ref)
        da = da_ref[...].astype(F32)
        sg = _sigmoid(yt)
        d_gate = da * yp * (sg * (1.0 + yt * (1.0 - sg)))
        d_val = da * (yp * _sigmoid(yp))
        dy = jnp.where(is_gate, d_gate, d_val)
        db_ref[...] = jnp.sum(dy, axis=0, keepdims=True)
        dw_ref[0:1, :] = jnp.sum(dy * _shift_down(ut, 2), axis=0, keepdims=True)
        dw_ref[1:2, :] = jnp.sum(dy * _shift_down(ut, 1), axis=0, keepdims=True)
        dw_ref[2:3, :] = jnp.sum(dy * ut, axis=0, keepdims=True)
        du = wt_ref[2:3, :] * dy + wt_ref[1:2, :] * _shift_up(dy, 1) + wt_ref[0:1, :] * _shift_up(dy, 2)
        du_ref[...] = du.astype(BF16)

    this = lambda r: pl.BlockSpec((r, bc), lambda j: (0, j))
    partner = lambda r: pl.BlockSpec((r, bc), lambda j: (0, (j + nf) % (2 * nf)))
    return pl.pallas_call(
        body, name=name, grid=(2 * nf,),
        in_specs=[this(s_len), partner(s_len), this(3), partner(3), this(1), partner(1),
                  pl.BlockSpec((s_len, bc), lambda j: (0, j % nf))],
        out_specs=[this(s_len), this(3), this(1)],
        out_shape=[jax.ShapeDtypeStruct((s_len, f2), BF16), jax.ShapeDtypeStruct((3, f2), F32),
                   jax.ShapeDtypeStruct((1, f2), F32)],
        compiler_params=_params("parallel"))(up, up, cw, cw, cb, cb, dact)


def _as2d(a):
    return a.reshape(-1, a.shape[-1]) if a.ndim >= 2 else a.reshape(1, -1)


def _adamw(w, g, m, v, *, name, target_bytes=1 << 20):
    shape = w.shape
    w2, m2, v2 = _as2d(w), _as2d(m), _as2d(v)
    g2 = g.reshape(w2.shape)
    r, c = w2.shape
    br = r if r * c * 4 <= target_bytes else _div_block(r, max(8, target_bytes // (4 * c) // 8 * 8), 8)
    c1 = 1.0 - ADAM_B1 ** ADAM_STEP
    c2 = 1.0 - ADAM_B2 ** ADAM_STEP

    def body(w_ref, g_ref, m_ref, v_ref, d_ref, nm_ref, nv_ref):
        gv = g_ref[...]
        nm = ADAM_B1 * m_ref[...] + (1.0 - ADAM_B1) * gv
        nv = ADAM_B2 * v_ref[...] + (1.0 - ADAM_B2) * (gv * gv)
        nm_ref[...] = nm
        nv_ref[...] = nv
        d_ref[...] = -ADAM_LR * ((nm / c1) / (jnp.sqrt(nv / c2) + ADAM_EPS) + ADAM_WD * w_ref[...])

    spec = pl.BlockSpec((br, c), lambda i: (i, 0))
    sds = jax.ShapeDtypeStruct((r, c), F32)
    d, nm, nv = pl.pallas_call(
        body, name=name, grid=(r // br,), in_specs=[spec] * 4, out_specs=[spec] * 3, out_shape=[sds] * 3,
        compiler_params=_params("parallel"))(w2, g2, m2, v2)
    return d.reshape(shape), nm.reshape(shape), nv.reshape(shape)


def _add_halves(g, recv, c_idx, *, name, target_bytes=1 << 20):
    _, _, r, c = g.shape
    br = _div_block(r, max(16, target_bytes // (2 * c) // 16 * 16), 16)

    def body(c_ref, g_ref, r_ref, o_ref):
        o_ref[...] = (g_ref[...].astype(F32) + r_ref[...].astype(F32)).astype(BF16)

    return pl.pallas_call(
        body, name=name,
        grid_spec=pltpu.PrefetchScalarGridSpec(
            num_scalar_prefetch=1, grid=(N_CHIPS, r // br),
            in_specs=[pl.BlockSpec((None, None, br, c), lambda s, i, c_ref: (s, c_ref[0], i, 0)),
                      pl.BlockSpec((None, br, c), lambda s, i, c_ref: (s, i, 0))],
            out_specs=pl.BlockSpec((None, br, c), lambda s, i, c_ref: (s, i, 0))),
        out_shape=jax.ShapeDtypeStruct((N_CHIPS, r, c), BF16),
        compiler_params=_params("parallel", "parallel"))(c_idx, g, recv)


def _sum_chips(x, *, name, target_bytes=1 << 20):
    _, r, c = x.shape
    br = _div_block(r, max(16, target_bytes // (4 * c) // 16 * 16), 16)

    def body(x_ref, o_ref):
        acc = x_ref[0].astype(F32)
        for s in range(1, N_CHIPS):
            acc = acc + x_ref[s].astype(F32)
        o_ref[...] = acc

    return pl.pallas_call(
        body, name=name, grid=(r // br,), in_specs=[pl.BlockSpec((N_CHIPS, br, c), lambda i: (0, i, 0))],
        out_specs=pl.BlockSpec((br, c), lambda i: (i, 0)), out_shape=jax.ShapeDtypeStruct((r, c), F32),
        compiler_params=_params("parallel"))(x)


def _sum_devices(x, *, name):
    n, r, c = x.shape
    br = _div_block(r, 512, 8)

    def body(x_ref, o_ref):
        acc = x_ref[0]
        for s in range(1, n):
            acc = acc + x_ref[s]
        o_ref[...] = acc

    return pl.pallas_call(
        body, name=name, grid=(r // br,), in_specs=[pl.BlockSpec((n, br, c), lambda i: (0, i, 0))],
        out_specs=pl.BlockSpec((br, c), lambda i: (i, 0)), out_shape=jax.ShapeDtypeStruct((r, c), F32),
        compiler_params=_params("parallel"))(x)


_ANY = pl.BlockSpec(memory_space=pl.ANY)


def _place():
    x, y, c = lax.axis_index("x"), lax.axis_index("y"), lax.axis_index("c")
    other_chips = [(1 - x, y), (x, 1 - y), (1 - x, 1 - y)]
    return x, y, c, other_chips


def _all_gather(xs, *, name):
    n = len(xs)

    def body(*refs):
        x_refs, o_refs = refs[:n], refs[n:2 * n]
        send_sems, recv_sems, local_sems = refs[2 * n:]
        x, y, c, chips = _place()
        me = 2 * x + y
        sibling = (x, y, 1 - c)

        def remote(i, k, src, dst, to):
            return pltpu.make_async_remote_copy(src_ref=src, dst_ref=dst, send_sem=send_sems.at[i, k],
                                                recv_sem=recv_sems.at[i, k], device_id=to, device_id_type=MESH)

        pending = []
        for i in range(n):
            loc = pltpu.make_async_copy(x_refs[i], o_refs[i].at[me], local_sems.at[i])
            loc.start()
            pending.append(loc)
        sends = []
        for i in range(n):
            for j, (px, py) in enumerate(chips):
                cp = remote(i, j, x_refs[i].at[c], o_refs[i].at[me, c], (px, py, c))
                cp.start()
                sends.append(cp)
        for i in range(n):
            for j, (px, py) in enumerate(chips):
                landed = o_refs[i].at[2 * px + py, c]
                remote(i, j, landed, landed, (px, py, c)).wait_recv()
                fwd = remote(i, 3 + j, landed, landed, sibling)
                fwd.start()
                sends.append(fwd)
        for i in range(n):
            for j, (px, py) in enumerate(chips):
                landed = o_refs[i].at[2 * px + py, 1 - c]
                remote(i, 3 + j, landed, landed, sibling).wait_recv()
        for cp in sends:
            cp.wait_send()
        for loc in pending:
            loc.wait()

    return pl.pallas_call(
        body, name=name, in_specs=[_ANY] * n, out_specs=[_ANY] * n,
        out_shape=[jax.ShapeDtypeStruct((N_CHIPS,) + a.shape, a.dtype) for a in xs],
        scratch_shapes=[pltpu.SemaphoreType.DMA((n, 6)), pltpu.SemaphoreType.DMA((n, 6)), pltpu.SemaphoreType.DMA((n,))],
    )(*xs)


def _sibling_halves(gs, *, name):
    n = len(gs)

    def body(*refs):
        g_refs, o_refs = refs[:n], refs[n:2 * n]
        send_sems, recv_sems = refs[2 * n:]
        x, y, c, _ = _place()
        copies = []
        for i in range(n):
            for s in range(N_CHIPS):
                k = i * N_CHIPS + s
                cp = pltpu.make_async_remote_copy(src_ref=g_refs[i].at[s, 1 - c], dst_ref=o_refs[i].at[s],
                                                  send_sem=send_sems.at[k], recv_sem=recv_sems.at[k],
                                                  device_id=(x, y, 1 - c), device_id_type=MESH)
                cp.start()
                copies.append(cp)
        for cp in copies:
            cp.wait()

    return pl.pallas_call(
        body, name=name, in_specs=[_ANY] * n, out_specs=[_ANY] * n,
        out_shape=[jax.ShapeDtypeStruct((N_CHIPS,) + g.shape[2:], g.dtype) for g in gs],
        scratch_shapes=[pltpu.SemaphoreType.DMA((n * N_CHIPS,)), pltpu.SemaphoreType.DMA((n * N_CHIPS,))],
    )(*gs)


def _chip_scatter(ps, *, name):
    n = len(ps)

    def body(*refs):
        p_refs, o_refs = refs[:n], refs[n:2 * n]
        send_sems, recv_sems, local_sems = refs[2 * n:]
        x, y, c, chips = _place()
        me = 2 * x + y
        copies, locals_ = [], []
        for i in range(n):
            loc = pltpu.make_async_copy(p_refs[i].at[me], o_refs[i].at[me], local_sems.at[i])
            loc.start()
            locals_.append(loc)
            for j, (px, py) in enumerate(chips):
                cp = pltpu.make_async_remote_copy(src_ref=p_refs[i].at[2 * px + py], dst_ref=o_refs[i].at[me],
                                                  send_sem=send_sems.at[i, j], recv_sem=recv_sems.at[i, j],
                                                  device_id=(px, py, c), device_id_type=MESH)
                cp.start()
                copies.append((cp, i, j, px, py))
        for cp, i, j, px, py in copies:
            cp.wait_send()
            landed = o_refs[i].at[2 * px + py]
            pltpu.make_async_remote_copy(src_ref=landed, dst_ref=landed, send_sem=send_sems.at[i, j],
                                         recv_sem=recv_sems.at[i, j], device_id=(px, py, c),
                                         device_id_type=MESH).wait_recv()
        for loc in locals_:
            loc.wait()

    return pl.pallas_call(
        body, name=name, in_specs=[_ANY] * n, out_specs=[_ANY] * n,
        out_shape=[jax.ShapeDtypeStruct(p.shape, p.dtype) for p in ps],
        scratch_shapes=[pltpu.SemaphoreType.DMA((n, 3)), pltpu.SemaphoreType.DMA((n, 3)), pltpu.SemaphoreType.DMA((n,))],
    )(*ps)


def _sibling_share(rs, *, name):
    n = len(rs)

    def body(*refs):
        r_refs, o_refs = refs[:n], refs[n:2 * n]
        send_sems, recv_sems, local_sems = refs[2 * n:]
        x, y, c, _ = _place()
        copies = []
        for i in range(n):
            loc = pltpu.make_async_copy(r_refs[i], o_refs[i].at[c], local_sems.at[i])
            loc.start()
            cp = pltpu.make_async_remote_copy(src_ref=r_refs[i], dst_ref=o_refs[i].at[c], send_sem=send_sems.at[i],
                                              recv_sem=recv_sems.at[i], device_id=(x, y, 1 - c), device_id_type=MESH)
            cp.start()
            copies.append((loc, cp, i))
        for loc, cp, i in copies:
            cp.wait_send()
            theirs = o_refs[i].at[1 - c]
            pltpu.make_async_remote_copy(src_ref=theirs, dst_ref=theirs, send_sem=send_sems.at[i],
                                         recv_sem=recv_sems.at[i], device_id=(x, y, 1 - c),
                                         device_id_type=MESH).wait_recv()
            loc.wait()

    return pl.pallas_call(
        body, name=name, in_specs=[_ANY] * n, out_specs=[_ANY] * n,
        out_shape=[jax.ShapeDtypeStruct((2,) + r.shape, r.dtype) for r in rs],
        scratch_shapes=[pltpu.SemaphoreType.DMA((n,)), pltpu.SemaphoreType.DMA((n,)), pltpu.SemaphoreType.DMA((n,))],
    )(*rs)


def _broadcast_all(v, *, name):
    def body(v_ref, o_ref, send_sems, recv_sems, local_sem):
        x, y, c, _ = _place()
        me = 4 * x + 2 * y + c
        loc = pltpu.make_async_copy(v_ref, o_ref.at[me], local_sem)
        loc.start()
        copies = []
        for k in range(1, 8):
            dx, dy, dc = (k >> 2) & 1, (k >> 1) & 1, k & 1
            to = (1 - x if dx else x, 1 - y if dy else y, 1 - c if dc else c)
            cp = pltpu.make_async_remote_copy(src_ref=v_ref, dst_ref=o_ref.at[me], send_sem=send_sems.at[k - 1],
                                              recv_sem=recv_sems.at[k - 1], device_id=to, device_id_type=MESH)
            cp.start()
            copies.append((cp, k, to))
        for cp, k, to in copies:
            cp.wait_send()
            theirs = o_ref.at[4 * to[0] + 2 * to[1] + to[2]]
            pltpu.make_async_remote_copy(src_ref=theirs, dst_ref=theirs, send_sem=send_sems.at[k - 1],
                                         recv_sem=recv_sems.at[k - 1], device_id=to, device_id_type=MESH).wait_recv()
        loc.wait()

    return pl.pallas_call(
        body, name=name, in_specs=[_ANY], out_specs=_ANY,
        out_shape=jax.ShapeDtypeStruct((8,) + v.shape, v.dtype),
        scratch_shapes=[pltpu.SemaphoreType.DMA((7,)), pltpu.SemaphoreType.DMA((7,)), pltpu.SemaphoreType.DMA(())],
    )(v)


def _halves(a):
    return a.reshape(2, a.shape[0] // 2, a.shape[1])


def _gather_layer(shards, *, name):
    names = list(shards)
    outs = _all_gather([_halves(shards[k]) for k in names], name=name)
    return {k: o.reshape((N_CHIPS,) + shards[k].shape) for k, o in zip(names, outs)}


def _reduce_layer(grads, c_idx, *, name):
    names = list(grads)
    gs = [grads[k].reshape(N_CHIPS, 2, grads[k].shape[1] // 2, grads[k].shape[2]) for k in names]
    recv = _sibling_halves(gs, name=name + "_sib")
    ps = [_add_halves(g, r, c_idx, name=f"{name}_add2_{k}") for g, r, k in zip(gs, recv, names)]
    landed = _chip_scatter(ps, name=name + "_scatter")
    rs = [_sum_chips(p, name=f"{name}_sum4_{k}") for p, k in zip(landed, names)]
    both = _sibling_share(rs, name=name + "_share")
    return {k: b.reshape(grads[k].shape[1:]) for k, b in zip(names, both)}


def _pad_lanes(a, n=LANES):
    return jnp.pad(a, [(0, 0)] * (a.ndim - 1) + [(0, n - a.shape[-1])])


def _unshard_cols(g):
    return jnp.transpose(g, (1, 0, 2)).reshape(g.shape[1], -1)


def _shard_cols(w):
    k, n = w.shape
    return jnp.transpose(w.reshape(k, N_CHIPS, n // N_CHIPS), (1, 0, 2))


def _ffn_fwd(h, p, tag):
    b = _rms_fwd(h, p['ffn_norm'], name=f"{tag}_ffn_norm")
    up = _mm(b, p['ffn_w_up'], b_sh='n', name=f"{tag}_ffn_up", bn=1408)
    act = _glu_fwd(up, p['ffn_conv_w'], p['ffn_conv_b'], name=f"{tag}_ffn_glu")
    out = _mm(act, p['ffn_w_down'], res=h, name=f"{tag}_ffn_down", bk=704)
    return out, (h, b, up, act)


def _ffn_bwd(dh, saved, p, tag):
    h, b, up, act = saved
    dact = _mm(dh, p['ffn_w_down'], tb=True, out_dtype=BF16, name=f"{tag}_ffn_dact", bn=1408)
    dw_down = _mm(act, dh, ta=True, out_dtype=BF16, name=f"{tag}_ffn_dwdown", bm=1408)
    dup, dcw, dcb = _glu_bwd(up, p['ffn_conv_w'], p['ffn_conv_b'], dact, name=f"{tag}_ffn_dglu")
    dw_up = _mm(b, dup, ta=True, o_sh=True, out_dtype=BF16, name=f"{tag}_ffn_dwup", bn=1408)
    db = _mm(dup, p['ffn_w_up'], b_sh='k', name=f"{tag}_ffn_db", bk=1408)
    dh_in, dg = _rms_bwd(h, p['ffn_norm'], db, res=dh, name=f"{tag}_ffn_dnorm")
    big = {'ffn_w_up': dw_up, 'ffn_w_down': dw_down.reshape(N_CHIPS, -1, dw_down.shape[1])}
    small = {'ffn_norm': dg, 'ffn_conv_w': dcw, 'ffn_conv_b': dcb}
    return dh_in, big, small


def _qkv_attn_fwd(kind, h, p, tag, n_heads):
    a = _rms_fwd(h, p['mix_norm'], name=f"{tag}_norm")
    if kind == 'fox':
        qkv = _mm(a, p['w_in'], name=f"{tag}_qkv", bn=896)
        cum = _fgate_fwd(qkv, p['b_f'], fcol=3 * n_heads, name=f"{tag}_fgate")
        cum_t = cum[:, :n_heads].T
        cq, ck = cum_t[:, :, None], cum_t[:, None, :]
    else:
        qkv = _mm(a, p['w_in'], b_sh='n', name=f"{tag}_qkv", bn=768)
        cq = ck = None
    cols = dict(qcol=lambda hh: hh, kcol=lambda hh: n_heads + hh, vcol=lambda hh: 2 * n_heads + hh)
    o = _attn_fwd(kind, qkv, qkv, qkv, name=f"{tag}_attn", n_heads=n_heads, dqk=HEAD_DIM, scale=HEAD_DIM ** -0.5,
                  gains=p['qk_gain'], cq=cq, ck=ck, **cols)
    out = _mm(o, p['w_out'], res=h, name=f"{tag}_out")
    return out, (h, a, qkv, o, cq, ck)


def _qkv_attn_bwd(kind, dh, saved, p, tag, n_heads):
    h, a, qkv, o, cq, ck = saved
    do = _mm(dh, p['w_out'], tb=True, out_dtype=BF16, name=f"{tag}_do")
    dw_out = _mm(o, dh, ta=True, out_dtype=BF16, name=f"{tag}_dwout")
    cols = dict(qcol=lambda hh: hh, kcol=lambda hh: n_heads + hh, vcol=lambda hh: 2 * n_heads + hh)
    outs = _attn_bwd(kind, qkv, qkv, qkv, o, do, name=f"{tag}_dattn", n_heads=n_heads, dqk=HEAD_DIM,
                     scale=HEAD_DIM ** -0.5, gains=p['qk_gain'], cq=cq, ck=ck, **cols)
    dq, dk, dv, dgain = outs[:4]
    small = {'q_gain': dgain[0], 'k_gain': dgain[1]}
    if kind == 'fox':
        dcq, dck = outs[4:]
        dca = _pad_lanes(dcq[:, :, 0].T)
        dcb = _pad_lanes(dck[:, 0, :].T)
        dflog, dbf = _fgate_bwd(qkv, p['b_f'], dca, dcb, fcol=3 * n_heads, n_heads=n_heads, name=f"{tag}_dfgate")
        small['b_f'] = dbf[:, :n_heads]
        dqkv = jnp.concatenate([dq, dk, dv, dflog], axis=1)
        dw_in = _mm(a, dqkv, ta=True, out_dtype=BF16, name=f"{tag}_dwin", bn=896)
        da = _mm(dqkv, p['w_in'], tb=True, name=f"{tag}_da", bk=896)
        dw_in = _shard_cols(dw_in[:, :3 * n_heads * HEAD_DIM + n_heads])
    else:
        dqkv = jnp.concatenate([dq, dk, dv], axis=1)
        dw_in = _mm(a, dqkv, ta=True, o_sh=True, out_dtype=BF16, name=f"{tag}_dwin", bn=768)
        da = _mm(dqkv, p['w_in'], b_sh='k', name=f"{tag}_da", bk=768)
    dh_in, dg = _rms_bwd(h, p['mix_norm'], da, res=dh, name=f"{tag}_dnorm")
    small['mix_norm'] = dg
    big = {'w_in': dw_in, 'w_out': dw_out.reshape(N_CHIPS, -1, dw_out.shape[1])}
    return dh_in, big, small


def _mla_fwd(h, p, tag, n_heads):
    a = _rms_fwd(h, p['mix_norm'], name=f"{tag}_norm")
    c = _mm(a, p['w_in'], name=f"{tag}_latent", bn=1152)
    cn = _mla_latent_fwd(c, p['a_gain'], name=f"{tag}_latent_norm")
    qp = _mm(cn[:, :MLA_Q_RANK], p['w_q_b'], name=f"{tag}_q_up")
    kv = _mm(cn[:, MLA_Q_RANK:], p['w_kv_b'], b_sh='n', name=f"{tag}_kv_up")
    qc, kc = _mla_prep_fwd(qp, kv, c, p['gq'], p['gk'], p['cos'], p['sin'], n_heads=n_heads, name=f"{tag}_prep")
    cols = dict(qcol=lambda hh: hh, kcol=lambda hh: hh, vcol=lambda hh: 2 * hh + 1)
    scale = (MLA_NOPE + MLA_ROPE) ** -0.5
    o = _attn_fwd('mla', qc, kc, kv, name=f"{tag}_attn", n_heads=n_heads, dqk=2 * LANES, scale=scale, **cols)
    out = _mm(o, p['w_out'], res=h, name=f"{tag}_out")
    return out, (h, a, c, cn, qp, kv, qc, kc, o)


def _mla_bwd(dh, saved, p, tag, n_heads):
    h, a, c, cn, qp, kv, qc, kc, o = saved
    do = _mm(dh, p['w_out'], tb=True, out_dtype=BF16, name=f"{tag}_do")
    dw_out = _mm(o, dh, ta=True, out_dtype=BF16, name=f"{tag}_dwout")
    cols = dict(qcol=lambda hh: hh, kcol=lambda hh: hh, vcol=lambda hh: 2 * hh + 1)
    scale = (MLA_NOPE + MLA_ROPE) ** -0.5
    dqc, dkc, dv = _attn_bwd('mla', qc, kc, kv, o, do, name=f"{tag}_dattn", n_heads=n_heads, dqk=2 * LANES,
                             scale=scale, **cols)
    dqn, dqr, dkv, dkr, dgq, dgk = _mla_prep_bwd(qp, kv, c, p['gq'], p['gk'], p['cos'], p['sin'], dqc, dkc, dv,
                                                 n_heads=n_heads, name=f"{tag}_dprep")
    dqp = jnp.concatenate([dqn, dqr], axis=1)
    cn_q, cn_kv = cn[:, :MLA_Q_RANK], cn[:, MLA_Q_RANK:]
    dw_q_b = _mm(cn_q, dqp, ta=True, out_dtype=BF16, name=f"{tag}_dwqb", bm=512)
    dcn_q = _mm(dqp, p['w_q_b'], tb=True, out_dtype=BF16, name=f"{tag}_dcnq")
    dw_kv_b = _mm(cn_kv, dkv, ta=True, o_sh=True, out_dtype=BF16, name=f"{tag}_dwkvb", bm=512)
    dcn_kv = _mm(dkv, p['w_kv_b'], b_sh='k', out_dtype=BF16, name=f"{tag}_dcnkv")
    dc, dga = _mla_latent_bwd(c, p['a_gain'], dcn_q, dcn_kv, dkr, name=f"{tag}_dlatent")
    dw_in = _mm(a, dc, ta=True, out_dtype=BF16, name=f"{tag}_dwin", bn=1152)
    da = _mm(dc, p['w_in'], tb=True, name=f"{tag}_da", bk=1152)
    dh_in, dg = _rms_bwd(h, p['mix_norm'], da, res=dh, name=f"{tag}_dnorm")
    k_rank = dw_q_b.shape[0]
    nope = dw_q_b[:, :n_heads * LANES].reshape(k_rank, n_heads, LANES)
    rope = dw_q_b[:, n_heads * LANES:].reshape(k_rank, n_heads, LANES)[:, :, :MLA_ROPE]
    dw_q_b = jnp.concatenate([nope, rope], axis=2).reshape(k_rank, n_heads * (MLA_NOPE + MLA_ROPE))
    w_in_cols = MLA_Q_RANK + MLA_KV_RANK + MLA_ROPE
    big = {'w_in': dw_in[:, :w_in_cols].reshape(N_CHIPS, -1, w_in_cols), 'w_q_b': _shard_cols(dw_q_b),
           'w_kv_b': dw_kv_b, 'w_out': dw_out.reshape(N_CHIPS, -1, dw_out.shape[1])}
    small = {'mix_norm': dg, 'q_a_gain': dga[:, :MLA_Q_RANK], 'kv_a_gain': dga[:, MLA_Q_RANK:],
             'q_gain': jnp.concatenate([dgq[0], dgq[1][:, :MLA_ROPE]], axis=1),
             'k_gain': jnp.concatenate([dgk[0], dgk[1][:, :MLA_ROPE]], axis=1)}
    return dh_in, big, small


def _sgu_fwd(h, p, tag):
    a = _rms_fwd(h, p['mix_norm'], name=f"{tag}_norm")
    uv = _mm(a, p['w_in'], b_sh='n', name=f"{tag}_in")
    u, vn = _sgu_act_fwd(uv, p['v_gain'], name=f"{tag}_act")
    gated = _sgu_mix_fwd(u, vn, p['w_s'], p['b_s'], name=f"{tag}_mix")
    out = _mm(gated, p['w_out'], res=h, name=f"{tag}_out")
    return out, (h, a, uv, u, vn, gated)


def _sgu_bwd(dh, saved, p, tag):
    h, a, uv, u, vn, gated = saved
    dgated = _mm(dh, p['w_out'], tb=True, out_dtype=BF16, name=f"{tag}_dgated")
    dw_out = _mm(gated, dh, ta=True, out_dtype=BF16, name=f"{tag}_dwout")
    du, dvn, dws, dbs = _sgu_mix_bwd(u, vn, p['w_s'], p['b_s'], dgated, name=f"{tag}_dmix")
    duv, dvg = _sgu_act_bwd(uv, p['v_gain'], du, dvn, name=f"{tag}_dact")
    dw_in = _mm(a, duv, ta=True, o_sh=True, out_dtype=BF16, name=f"{tag}_dwin")
    da = _mm(duv, p['w_in'], b_sh='k', name=f"{tag}_da")
    dh_in, dg = _rms_bwd(h, p['mix_norm'], da, res=dh, name=f"{tag}_dnorm")
    big = {'w_in': dw_in, 'w_out': dw_out.reshape(N_CHIPS, -1, dw_out.shape[1])}
    small = {'mix_norm': dg, 'v_gain': dvg, 'w_s': dws, 'b_s': dbs[:, :, 0]}
    return dh_in, big, small


def _pack(parts):
    flat = jnp.concatenate([p.reshape(-1).astype(F32) for p in parts])
    rows = -(-flat.shape[0] // LANES)
    rows = -(-rows // 16) * 16
    return jnp.pad(flat, (0, rows * LANES - flat.shape[0])).reshape(rows, LANES)


def _unpack(packed, shapes):
    flat = packed.reshape(-1)
    out, off = [], 0
    for s in shapes:
        n = 1
        for d in s:
            n *= d
        out.append(flat[off:off + n].reshape(s))
        off += n
    return out


MIXERS = ('fox', 'mla', 'sb', 'sgu')
WEIGHT_NAMES = ['mix_norm', 'ffn_norm', 'fox_w_in', 'fox_b_f', 'fox_q_gain', 'fox_k_gain', 'fox_w_out', 'mla_w_in',
                'mla_q_a_gain', 'mla_kv_a_gain', 'mla_w_q_b', 'mla_w_kv_b', 'mla_q_gain', 'mla_k_gain', 'mla_w_out',
                'sb_w_in', 'sb_q_gain', 'sb_k_gain', 'sb_w_out', 'sgu_w_in', 'sgu_v_gain', 'sgu_w_s', 'sgu_b_s',
                'sgu_w_out', 'ffn_w_up', 'ffn_conv_w', 'ffn_conv_b', 'ffn_w_down']
SMALL_SHARDED = {'mla_q_a_gain': 1, 'mla_kv_a_gain': 1, 'sgu_v_gain': 1, 'ffn_conv_w': 2}
BIG = ['fox_w_in', 'fox_w_out', 'mla_w_in', 'mla_w_q_b', 'mla_w_kv_b', 'mla_w_out', 'sb_w_in', 'sb_w_out', 'sgu_w_in',
       'sgu_w_out', 'ffn_w_up', 'ffn_w_down']


def kernel(x, positions, mix_norm, ffn_norm, fox_w_in, fox_b_f, fox_q_gain, fox_k_gain, fox_w_out, mla_w_in, mla_q_a_gain, mla_kv_a_gain, mla_w_q_b, mla_w_kv_b, mla_q_gain, mla_k_gain, mla_w_out, sb_w_in, sb_q_gain, sb_k_gain, sb_w_out, sgu_w_in, sgu_v_gain, sgu_w_s, sgu_b_s, sgu_w_out, ffn_w_up, ffn_conv_w, ffn_conv_b, ffn_w_down, loss_target, m_mix_norm, m_ffn_norm, m_fox_w_in, m_fox_b_f, m_fox_q_gain, m_fox_k_gain, m_fox_w_out, m_mla_w_in, m_mla_q_a_gain, m_mla_kv_a_gain, m_mla_w_q_b, m_mla_w_kv_b, m_mla_q_gain, m_mla_k_gain, m_mla_w_out, m_sb_w_in, m_sb_q_gain, m_sb_k_gain, m_sb_w_out, m_sgu_w_in, m_sgu_v_gain, m_sgu_w_s, m_sgu_b_s, m_sgu_w_out, m_ffn_w_up, m_ffn_conv_w, m_ffn_conv_b, m_ffn_w_down, v_mix_norm, v_ffn_norm, v_fox_w_in, v_fox_b_f, v_fox_q_gain, v_fox_k_gain, v_fox_w_out, v_mla_w_in, v_mla_q_a_gain, v_mla_kv_a_gain, v_mla_w_q_b, v_mla_w_kv_b, v_mla_q_gain, v_mla_k_gain, v_mla_w_out, v_sb_w_in, v_sb_q_gain, v_sb_k_gain, v_sb_w_out, v_sgu_w_in, v_sgu_v_gain, v_sgu_w_s, v_sgu_b_s, v_sgu_w_out, v_ffn_w_up, v_ffn_conv_w, v_ffn_conv_b, v_ffn_w_down):
    args = dict(locals())
    W = {k: args[k] for k in WEIGHT_NAMES}
    M = {k: args['m_' + k] for k in WEIGHT_NAMES}
    V = {k: args['v_' + k] for k in WEIGHT_NAMES}
    depth = mix_norm.shape[0]
    s_len, d_model = x.shape[1], x.shape[2]
    n_heads = d_model // HEAD_DIM
    assert all(W[k].shape[0] == 1 for k in WEIGHT_NAMES if k.split('_')[0] in MIXERS), "one layer per mixer"
    xi, yi, ci = lax.axis_index("x"), lax.axis_index("y"), lax.axis_index("c")
    chip = 2 * xi + yi
    c_idx = jnp.reshape(ci, (1,)).astype(jnp.int32)

    small_local = _pack([W[k][0] if k != 'ffn_conv_w' else W[k] for k in SMALL_SHARDED])
    gathered = {}
    for i in range(depth):
        mixer = MIXERS[i % len(MIXERS)]
        shards = {k: W[k][0].astype(BF16) for k in BIG if k.startswith(mixer + '_')}
        shards['ffn_w_up'] = W['ffn_w_up'][i].astype(BF16)
        shards['ffn_w_down'] = W['ffn_w_down'][i].astype(BF16)
        if i == 0:
            shards['small'] = small_local
        gathered[i] = _gather_layer(shards, name=f"gather{i}")
    small_shapes = [W[k][0].shape if k != 'ffn_conv_w' else W[k].shape for k in SMALL_SHARDED]
    per_chip = [_unpack(gathered[0]['small'][s], small_shapes) for s in range(N_CHIPS)]
    full_small = {k: jnp.concatenate([per_chip[s][j] for s in range(N_CHIPS)], axis=-1)
                  for j, k in enumerate(SMALL_SHARDED)}

    pos = positions.reshape(s_len).astype(F32)
    inv_freq = ROPE_THETA ** (-jnp.arange(0, MLA_ROPE, 2, dtype=F32) / MLA_ROPE)
    ang = pos[:, None] * inv_freq
    cos_t = _pad_lanes(jnp.concatenate([jnp.cos(ang), jnp.cos(ang)], axis=1))
    sin_t = _pad_lanes(jnp.concatenate([-jnp.sin(ang), jnp.sin(ang)], axis=1))

    def layer_params(i):
        mixer = MIXERS[i % len(MIXERS)]
        g = gathered[i]
        p = {'mix_norm': mix_norm[i:i + 1], 'ffn_norm': ffn_norm[i:i + 1], 'ffn_w_up': g['ffn_w_up'],
             'ffn_w_down': g['ffn_w_down'].reshape(-1, d_model), 'ffn_conv_w': full_small['ffn_conv_w'][i],
             'ffn_conv_b': ffn_conv_b[i:i + 1]}
        rows = lambda w: w.reshape(-1, w.shape[-1])
        if mixer == 'fox':
            w = _unshard_cols(g['fox_w_in'])
            p['w_in'] = jnp.pad(w, ((0, 0), (0, (3 * n_heads + 1) * HEAD_DIM - w.shape[1])))
            p['b_f'] = _pad_lanes(fox_b_f)
            p['qk_gain'] = jnp.stack([fox_q_gain, fox_k_gain])
            p['w_out'] = rows(g['fox_w_out'])
        elif mixer == 'sb':
            p['w_in'] = g['sb_w_in']
            p['qk_gain'] = jnp.stack([sb_q_gain, sb_k_gain])
            p['w_out'] = rows(g['sb_w_out'])
        elif mixer == 'sgu':
            p['w_in'] = g['sgu_w_in']
            p['v_gain'] = full_small['sgu_v_gain'].reshape(1, -1)
            p['w_s'] = sgu_w_s[0]
            p['b_s'] = sgu_b_s[0][:, :, None]
            p['w_out'] = rows(g['sgu_w_out'])
        else:
            w = rows(g['mla_w_in'])
            p['w_in'] = jnp.pad(w, ((0, 0), (0, MLA_Q_RANK + MLA_KV_RANK + LANES - w.shape[1])))
            p['a_gain'] = jnp.concatenate([full_small['mla_q_a_gain'], full_small['mla_kv_a_gain']]).reshape(1, -1)
            wq = _unshard_cols(g['mla_w_q_b']).reshape(MLA_Q_RANK, n_heads, MLA_NOPE + MLA_ROPE)
            p['w_q_b'] = jnp.concatenate([wq[:, :, :MLA_NOPE].reshape(MLA_Q_RANK, -1),
                                          _pad_lanes(wq[:, :, MLA_NOPE:]).reshape(MLA_Q_RANK, -1)], axis=1)
            p['w_kv_b'] = g['mla_w_kv_b']
            p['gq'] = jnp.stack([mla_q_gain[:, :MLA_NOPE], _pad_lanes(mla_q_gain[:, MLA_NOPE:])])
            p['gk'] = jnp.stack([mla_k_gain[:, :MLA_NOPE], _pad_lanes(mla_k_gain[:, MLA_NOPE:])])
            p['cos'], p['sin'] = cos_t, sin_t
            p['w_out'] = rows(g['mla_w_out'])
        return mixer, p

    h = x.reshape(s_len, d_model)
    saved = []
    for i in range(depth):
        mixer, p = layer_params(i)
        tag = f"l{i}_{mixer}"
        if mixer in ('fox', 'sb'):
            h, sm = _qkv_attn_fwd(mixer, h, p, tag, n_heads)
        elif mixer == 'mla':
            h, sm = _mla_fwd(h, p, tag, n_heads)
        else:
            h, sm = _sgu_fwd(h, p, tag)
        h, sf = _ffn_fwd(h, p, f"l{i}")
        saved.append((mixer, p, sm, sf))
    loss_row, dh = _loss(h, loss_target.reshape(s_len, d_model))
    loss = lax.psum(loss_row[0, 0], ("x", "y", "c"))

    big_grads, small_grads = {}, {k: [None] * depth for k in ('mix_norm', 'ffn_norm', 'ffn_conv_w', 'ffn_conv_b')}
    for i in reversed(range(depth)):
        mixer, p, sm, sf = saved[i]
        tag = f"l{i}_{mixer}"
        dh, big_f, small_f = _ffn_bwd(dh, sf, p, f"l{i}")
        if mixer in ('fox', 'sb'):
            dh, big_m, small_m = _qkv_attn_bwd(mixer, dh, sm, p, tag, n_heads)
        elif mixer == 'mla':
            dh, big_m, small_m = _mla_bwd(dh, sm, p, tag, n_heads)
        else:
            dh, big_m, small_m = _sgu_bwd(dh, sm, p, tag)
        layer_big = {f"{mixer}_{k}": v for k, v in big_m.items()}
        layer_big.update(big_f)
        reduced = _reduce_layer(layer_big, c_idx, name=f"reduce{i}")
        for k, v in reduced.items():
            if k.startswith('ffn_'):
                big_grads.setdefault(k, [None] * depth)[i] = v
            else:
                big_grads[k] = v[None]
        for k, v in {**small_m, **small_f}.items():
            if k in small_grads:
                small_grads[k][i] = v
            else:
                small_grads[f"{mixer}_{k}"] = v
    grad_x = dh.reshape(x.shape)
    for k in ('ffn_w_up', 'ffn_w_down'):
        big_grads[k] = jnp.stack(big_grads[k])
    for k in ('mix_norm', 'ffn_norm', 'ffn_conv_b'):
        small_grads[k] = jnp.concatenate(small_grads[k], axis=0)
    small_grads['ffn_conv_w'] = jnp.stack(small_grads['ffn_conv_w'])

    small_names = [k for k in WEIGHT_NAMES if k not in BIG]
    full_shapes = {k: (W[k].shape[:-1] + (W[k].shape[-1] * N_CHIPS,) if k in SMALL_SHARDED else W[k].shape)
                   for k in small_names}
    packed = _pack([small_grads[k].reshape(full_shapes[k]) for k in small_names])
    summed = _sum_devices(_broadcast_all(packed, name="small_bcast"), name="small_sum")
    small_full = dict(zip(small_names, _unpack(summed, [full_shapes[k] for k in small_names])))
    grads = dict(big_grads)
    for k in small_names:
        g = small_full[k]
        if k in SMALL_SHARDED:
            n = W[k].shape[-1]
            g = lax.dynamic_slice_in_dim(g, chip * n, n, axis=g.ndim - 1)
        grads[k] = g
    grads = {k: grads[k].reshape(W[k].shape) for k in WEIGHT_NAMES}

    delta, new_m, new_v = {}, {}, {}
    for k in WEIGHT_NAMES:
        delta[k], new_m[k], new_v[k] = _adamw(W[k], grads[k], M[k], V[k], name=f"adamw_{k}")
    return (loss, grad_x, *[grads[k] for k in WEIGHT_NAMES], *[delta[k] for k in WEIGHT_NAMES],
            *[new_m[k] for k in WEIGHT_NAMES], *[new_v[k] for k in WEIGHT_NAMES])
```

```python
import functools

import jax
import jax.numpy as jnp
from jax import lax
from jax.experimental import pallas as pl
from jax.experimental.pallas import tpu as pltpu

F32 = jnp.float32
BF16 = jnp.bfloat16
LANES = 128
HEAD_DIM = 128
NORM_EPS = 1e-6
MLA_Q_RANK = 512
MLA_KV_RANK = 512
MLA_NOPE = 128
MLA_ROPE = 64
ROPE_THETA = 10000.0
SGU_CHUNK = 128
N_CHIPS = 4
ADAM_LR, ADAM_B1, ADAM_B2, ADAM_EPS, ADAM_WD, ADAM_STEP = 0.001, 0.9, 0.999, 1e-08, 0.01, 10
VMEM_LIMIT_BYTES = 56 * 1024 * 1024
MESH = pl.DeviceIdType.MESH
NEG_BIG = -1e30


def _params(*sem):
    return pltpu.CompilerParams(dimension_semantics=sem, vmem_limit_bytes=VMEM_LIMIT_BYTES)


def _div_block(n, target, mult=LANES):
    if n <= target:
        return n
    best = None
    for b in range(mult, target + 1, mult):
        if n % b == 0:
            best = b
    assert best is not None, (n, target, mult)
    return best


def _iota(shape, dim):
    return lax.broadcasted_iota(jnp.int32, shape, dim)


def _dot(a, b, ca, cb):
    return lax.dot_general(a, b, (((ca,), (cb,)), ((), ())), preferred_element_type=F32)


def _mm(a, b, *, name, ta=False, tb=False, b_sh=None, o_sh=False, res=None, out_dtype=F32,
        bm=1024, bn=1024, bk=512):
    m, k = (a.shape[1], a.shape[0]) if ta else a.shape
    if b_sh == 'n':
        n = b.shape[2] * N_CHIPS
        assert b.shape[1] == k and not tb
    elif b_sh == 'k':
        n = b.shape[1]
        assert b.shape[2] * N_CHIPS == k
    else:
        n = b.shape[0] if tb else b.shape[1]
        assert (b.shape[1] if tb else b.shape[0]) == k
    n_sh = n // N_CHIPS
    k_sh = k // N_CHIPS
    bm = _div_block(m, bm, 8 if not ta else LANES)
    bn = _div_block(n_sh if (b_sh == 'n' or o_sh) else n, bn)
    bk = _div_block(k_sh if b_sh == 'k' else k, bk)
    nbs = n_sh // bn if (b_sh == 'n' or o_sh) else 1
    nks = k_sh // bk if b_sh == 'k' else 1
    nk = k // bk

    a_spec = pl.BlockSpec((bk, bm), lambda i, j, q: (q, i)) if ta else pl.BlockSpec((bm, bk), lambda i, j, q: (i, q))
    if b_sh == 'n':
        b_spec = pl.BlockSpec((None, bk, bn), lambda i, j, q: (j // nbs, q, j % nbs))
    elif b_sh == 'k':
        b_spec = pl.BlockSpec((None, bn, bk), lambda i, j, q: (q // nks, j, q % nks))
    elif tb:
        b_spec = pl.BlockSpec((bn, bk), lambda i, j, q: (j, q))
    else:
        b_spec = pl.BlockSpec((bk, bn), lambda i, j, q: (q, j))
    if o_sh:
        o_spec = pl.BlockSpec((None, bm, bn), lambda i, j, q: (j // nbs, i, j % nbs))
        o_shape = jax.ShapeDtypeStruct((N_CHIPS, m, n_sh), out_dtype)
    else:
        o_spec = pl.BlockSpec((bm, bn), lambda i, j, q: (i, j))
        o_shape = jax.ShapeDtypeStruct((m, n), out_dtype)
    tb_eff = tb or b_sh == 'k'

    def body(a_ref, b_ref, *rest):
        if res is not None:
            r_ref, o_ref, acc = rest
        else:
            o_ref, acc = rest
        q = pl.program_id(2)

        @pl.when(q == 0)
        def _():
            acc[...] = jnp.zeros_like(acc)

        acc[...] += _dot(a_ref[...].astype(BF16), b_ref[...].astype(BF16), 0 if ta else 1, 1 if tb_eff else 0)

        @pl.when(q == nk - 1)
        def _():
            r = acc[...]
            if res is not None:
                r = r + r_ref[...].astype(F32)
            o_ref[...] = r.astype(out_dtype)

    ins = [a, b]
    in_specs = [a_spec, b_spec]
    if res is not None:
        assert not o_sh
        ins.append(res)
        in_specs.append(pl.BlockSpec((bm, bn), lambda i, j, q: (i, j)))
    return pl.pallas_call(
        body, name=name, grid=(m // bm, n // bn, nk), in_specs=in_specs, out_specs=o_spec, out_shape=o_shape,
        scratch_shapes=[pltpu.VMEM((bm, bn), F32)],
        compiler_params=_params("parallel", "parallel", "arbitrary"))(*ins)


def _rms_fwd(x, g, *, name, out_dtype=BF16, br=256):
    r, c = x.shape
    br = _div_block(r, br, 8)

    def body(x_ref, g_ref, o_ref):
        xv = x_ref[...].astype(F32)
        inv = lax.rsqrt(jnp.mean(xv * xv, axis=-1, keepdims=True) + NORM_EPS)
        o_ref[...] = (xv * inv * g_ref[...]).astype(out_dtype)

    return pl.pallas_call(
        body, name=name, grid=(r // br,),
        in_specs=[pl.BlockSpec((br, c), lambda i: (i, 0)), pl.BlockSpec((1, c), lambda i: (0, 0))],
        out_specs=pl.BlockSpec((br, c), lambda i: (i, 0)), out_shape=jax.ShapeDtypeStruct((r, c), out_dtype),
        compiler_params=_params("parallel"))(x, g)


def _rms_bwd_math(xv, gv, dyv, n):
    inv = lax.rsqrt(jnp.sum(xv * xv, axis=-1, keepdims=True) / n + NORM_EPS)
    xh = xv * inv
    dyg = dyv * gv
    dx = inv * (dyg - xh * (jnp.sum(dyg * xh, axis=-1, keepdims=True) / n))
    return dx, dyv * xh


def _rms_bwd(x, g, dy, *, name, res=None, br=256):
    r, c = x.shape
    br = _div_block(r, br, 8)

    def body(x_ref, g_ref, dy_ref, *rest):
        if res is not None:
            r_ref, dx_ref, dg_ref = rest
        else:
            dx_ref, dg_ref = rest
        dx, dgr = _rms_bwd_math(x_ref[...].astype(F32), g_ref[...], dy_ref[...].astype(F32), c)
        if res is not None:
            dx = dx + r_ref[...]
        dx_ref[...] = dx

        @pl.when(pl.program_id(0) == 0)
        def _():
            dg_ref[...] = jnp.zeros_like(dg_ref)

        dg_ref[...] += jnp.sum(dgr, axis=0, keepdims=True)

    row = pl.BlockSpec((br, c), lambda i: (i, 0))
    vec = pl.BlockSpec((1, c), lambda i: (0, 0))
    ins = [x, g, dy] + ([res] if res is not None else [])
    return pl.pallas_call(
        body, name=name, grid=(r // br,), in_specs=[row, vec, row] + ([row] if res is not None else []),
        out_specs=[row, vec], out_shape=[jax.ShapeDtypeStruct((r, c), F32), jax.ShapeDtypeStruct((1, c), F32)],
        compiler_params=_params("arbitrary"))(*ins)


def _loss(y, target, *, name="loss", br=256):
    r, c = y.shape
    br = _div_block(r, br, 8)

    def body(y_ref, t_ref, l_ref, dy_ref):
        d = y_ref[...] - t_ref[...]
        dy_ref[...] = d * (1.0 / c)

        @pl.when(pl.program_id(0) == 0)
        def _():
            l_ref[...] = jnp.zeros_like(l_ref)

        part = jnp.sum(d * d, axis=0, keepdims=True)
        l_ref[...] += (0.5 / c) * jnp.sum(part, axis=1, keepdims=True) * jnp.ones((1, LANES), F32)

    row = pl.BlockSpec((br, c), lambda i: (i, 0))
    return pl.pallas_call(
        body, name=name, grid=(r // br,), in_specs=[row, row],
        out_specs=[pl.BlockSpec((1, LANES), lambda i: (0, 0)), row],
        out_shape=[jax.ShapeDtypeStruct((1, LANES), F32), jax.ShapeDtypeStruct((r, c), F32)],
        compiler_params=_params("arbitrary"))(y, target)


def _split2(x):
    hi = x.astype(BF16)
    lo = (x - hi.astype(F32)).astype(BF16)
    return hi, lo


def _lane_scan(x, *, suffix):
    rows, n = x.shape
    nb = n // LANES
    a, b = _iota((LANES, LANES), 0), _iota((LANES, LANES), 1)
    tri = ((a > b) if suffix else (a < b)).astype(BF16)
    outs = [None] * nb
    run = jnp.zeros((rows, 1), F32)
    order = range(nb - 1, -1, -1) if suffix else range(nb)
    for blk in order:
        xb = x[:, blk * LANES:(blk + 1) * LANES]
        hi, lo = _split2(xb)
        outs[blk] = _dot(hi, tri, 1, 0) + _dot(lo, tri, 1, 0) + run
        run = run + jnp.sum(xb, axis=-1, keepdims=True)
    return jnp.concatenate(outs, axis=1)


def _softplus(z):
    return jnp.maximum(z, 0.0) + jnp.log(1.0 + jnp.exp(-jnp.abs(z)))


def _head_norm(x, g):
    xv = x.astype(F32)
    inv = lax.rsqrt(jnp.mean(xv * xv, axis=-1, keepdims=True) + NORM_EPS)
    return xv * inv * g


def _attn_weights(kind, qn, kn, scale, qi, bq, bias):
    s = _dot(qn, kn, 1, 1) * scale
    row = qi * bq + _iota(s.shape, 0)
    col = _iota(s.shape, 1)
    if kind == 'sb':
        strict = col < row
        sp = _softplus(s)
        after = _lane_scan(jnp.where(strict, -sp, 0.0), suffix=True)
        w = jnp.where(strict, jnp.exp(s - sp + after), 0.0)
        return w, (strict, s - sp)
    if bias is not None:
        s = s + bias
    s = jnp.where(col <= row, s, NEG_BIG)
    mx = jnp.max(s, axis=-1, keepdims=True)
    e = jnp.exp(s - mx)
    return e, jnp.sum(e, axis=-1, keepdims=True)


def _attn_fwd(kind, q, k, v, *, name, n_heads, dqk, qcol, kcol, vcol, scale, gains=None, cq=None, ck=None, bq=256):
    s_len = q.shape[0]
    bq = _div_block(s_len, bq, 8)
    norm, fox = gains is not None, cq is not None

    def body(*refs):
        refs = list(refs)
        q_ref, k_ref, v_ref = refs[:3]
        rest = refs[3:]
        g_ref = rest.pop(0) if norm else None
        cq_ref, ck_ref = (rest.pop(0), rest.pop(0)) if fox else (None, None)
        o_ref, = rest
        qi = pl.program_id(1)
        if norm:
            qn = _head_norm(q_ref[...], g_ref[0]).astype(BF16)
            kn = _head_norm(k_ref[...], g_ref[1]).astype(BF16)
        else:
            qn, kn = q_ref[...].astype(BF16), k_ref[...].astype(BF16)
        bias = (cq_ref[...] - ck_ref[...]) if fox else None
        w, aux = _attn_weights(kind, qn, kn, scale, qi, bq, bias)
        o = _dot(w.astype(BF16), v_ref[...].astype(BF16), 1, 0)
        if kind != 'sb':
            o = o / aux
        o_ref[...] = o.astype(BF16)

    in_specs = [pl.BlockSpec((bq, dqk), lambda h, i: (i, qcol(h))),
                pl.BlockSpec((s_len, dqk), lambda h, i: (0, kcol(h))),
                pl.BlockSpec((s_len, HEAD_DIM), lambda h, i: (0, vcol(h)))]
    ins = [q, k, v]
    if norm:
        in_specs.append(pl.BlockSpec((2, 1, dqk), lambda h, i: (0, 0, 0)))
        ins.append(gains)
    if fox:
        in_specs += [pl.BlockSpec((None, bq, 1), lambda h, i: (h, i, 0)), pl.BlockSpec((None, 1, s_len), lambda h, i: (h, 0, 0))]
        ins += [cq, ck]
    return pl.pallas_call(
        body, name=name, grid=(n_heads, s_len // bq), in_specs=in_specs,
        out_specs=pl.BlockSpec((bq, HEAD_DIM), lambda h, i: (i, h)),
        out_shape=jax.ShapeDtypeStruct((s_len, n_heads * HEAD_DIM), BF16),
        compiler_params=_params("parallel", "parallel"))(*ins)


def _attn_bwd(kind, q, k, v, o, do, *, name, n_heads, dqk, qcol, kcol, vcol, scale, gains=None, cq=None, ck=None,
              bq=256):
    s_len = q.shape[0]
    bq = _div_block(s_len, bq, 8)
    nq = s_len // bq
    norm, fox = gains is not None, cq is not None

    def body(*refs):
        refs = list(refs)
        q_ref, k_ref, v_ref, o_ref, do_ref = refs[:5]
        rest = refs[5:]
        g_ref = rest.pop(0) if norm else None
        cq_ref, ck_ref = (rest.pop(0), rest.pop(0)) if fox else (None, None)
        dq_ref, dk_ref, dv_ref = rest.pop(0), rest.pop(0), rest.pop(0)
        dg_ref = rest.pop(0) if norm else None
        dcq_ref, dck_ref = (rest.pop(0), rest.pop(0)) if fox else (None, None)
        dk_acc, dv_acc = rest
        h, qi = pl.program_id(0), pl.program_id(1)

        @pl.when(qi == 0)
        def _():
            dk_acc[...] = jnp.zeros_like(dk_acc)
            dv_acc[...] = jnp.zeros_like(dv_acc)
            if fox:
                dck_ref[...] = jnp.zeros_like(dck_ref)

        if norm:
            @pl.when((qi == 0) & (h == 0))
            def _():
                dg_ref[...] = jnp.zeros_like(dg_ref)

            qn = _head_norm(q_ref[...], g_ref[0]).astype(BF16)
            kn = _head_norm(k_ref[...], g_ref[1]).astype(BF16)
        else:
            qn, kn = q_ref[...].astype(BF16), k_ref[...].astype(BF16)
        vb = v_ref[...].astype(BF16)
        dob = do_ref[...].astype(BF16)
        bias = (cq_ref[...] - ck_ref[...]) if fox else None
        w, aux = _attn_weights(kind, qn, kn, scale, qi, bq, bias)
        dw = _dot(dob, vb, 1, 1)
        if kind == 'sb':
            strict, log_sig = aux
            g = dw * w
            cc = _lane_scan(g, suffix=False)
            sig = jnp.exp(log_sig)
            ds = jnp.where(strict, g * (1.0 - sig) - cc * sig, 0.0)
            pw = w
        else:
            pw = w / aux
            delta = jnp.sum(do_ref[...].astype(F32) * o_ref[...].astype(F32), axis=-1, keepdims=True)
            ds = pw * (dw - delta)
            if fox:
                dcq_ref[...] = jnp.sum(ds, axis=1, keepdims=True)
                dck_ref[...] -= jnp.sum(ds, axis=0, keepdims=True)
        dsb = (ds * scale).astype(BF16)
        dqn = _dot(dsb, kn, 1, 0)
        dk_acc[...] += _dot(dsb, qn, 0, 0)
        dv_acc[...] += _dot(pw.astype(BF16), dob, 0, 0)
        if norm:
            dq, dgr = _rms_bwd_math(q_ref[...].astype(F32), g_ref[0], dqn, dqk)
            dg_ref[0] += jnp.sum(dgr, axis=0, keepdims=True)
            dq_ref[...] = dq.astype(BF16)
        else:
            dq_ref[...] = dqn.astype(BF16)

        @pl.when(qi == nq - 1)
        def _():
            if norm:
                dk, dgr = _rms_bwd_math(k_ref[...].astype(F32), g_ref[1], dk_acc[...], dqk)
                dg_ref[1] += jnp.sum(dgr, axis=0, keepdims=True)
                dk_ref[...] = dk.astype(BF16)
            else:
                dk_ref[...] = dk_acc[...].astype(BF16)
            dv_ref[...] = dv_acc[...].astype(BF16)

    in_specs = [pl.BlockSpec((bq, dqk), lambda h, i: (i, qcol(h))),
                pl.BlockSpec((s_len, dqk), lambda h, i: (0, kcol(h))),
                pl.BlockSpec((s_len, HEAD_DIM), lambda h, i: (0, vcol(h))),
                pl.BlockSpec((bq, HEAD_DIM), lambda h, i: (i, h)),
                pl.BlockSpec((bq, HEAD_DIM), lambda h, i: (i, h))]
    ins = [q, k, v, o, do]
    out_specs = [pl.BlockSpec((bq, dqk), lambda h, i: (i, h)),
                 pl.BlockSpec((s_len, dqk), lambda h, i: (0, h)),
                 pl.BlockSpec((s_len, HEAD_DIM), lambda h, i: (0, h))]
    out_shape = [jax.ShapeDtypeStruct((s_len, n_heads * dqk), BF16), jax.ShapeDtypeStruct((s_len, n_heads * dqk), BF16),
                 jax.ShapeDtypeStruct((s_len, n_heads * HEAD_DIM), BF16)]
    if norm:
        in_specs.append(pl.BlockSpec((2, 1, dqk), lambda h, i: (0, 0, 0)))
        ins.append(gains)
        out_specs.append(pl.BlockSpec((2, 1, dqk), lambda h, i: (0, 0, 0)))
        out_shape.append(jax.ShapeDtypeStruct((2, 1, dqk), F32))
    if fox:
        in_specs += [pl.BlockSpec((None, bq, 1), lambda h, i: (h, i, 0)), pl.BlockSpec((None, 1, s_len), lambda h, i: (h, 0, 0))]
        ins += [cq, ck]
        out_specs += [pl.BlockSpec((None, bq, 1), lambda h, i: (h, i, 0)), pl.BlockSpec((None, 1, s_len), lambda h, i: (h, 0, 0))]
        out_shape += [jax.ShapeDtypeStruct((n_heads, s_len, 1), F32), jax.ShapeDtypeStruct((n_heads, 1, s_len), F32)]
    return pl.pallas_call(
        body, name=name, grid=(n_heads, nq), in_specs=in_specs, out_specs=out_specs, out_shape=out_shape,
        scratch_shapes=[pltpu.VMEM((s_len, dqk), F32), pltpu.VMEM((s_len, HEAD_DIM), F32)],
        compiler_params=_params("arbitrary", "arbitrary"))(*ins)


def _split3(x):
    hi = x.astype(BF16)
    r1 = x - hi.astype(F32)
    mid = r1.astype(BF16)
    lo = (r1 - mid.astype(F32)).astype(BF16)
    return hi, mid, lo


def _seq_scan(x, *, reverse):
    n = x.shape[0] // LANES
    a, b = _iota((LANES, LANES), 0), _iota((LANES, LANES), 1)
    tri = ((b >= a) if reverse else (b <= a)).astype(BF16)
    outs = [None] * n
    run = jnp.zeros((1, x.shape[1]), F32)
    for blk in (range(n - 1, -1, -1) if reverse else range(n)):
        xb = x[blk * LANES:(blk + 1) * LANES, :]
        hi, mid, lo = _split3(xb)
        outs[blk] = _dot(tri, hi, 1, 0) + _dot(tri, mid, 1, 0) + _dot(tri, lo, 1, 0) + run
        run = run + jnp.sum(xb, axis=0, keepdims=True)
    return jnp.concatenate(outs, axis=0)


def _fgate_fwd(qkvf, b_f, *, fcol, name):
    s_len = qkvf.shape[0]

    def body(f_ref, b_ref, cum_ref):
        z = f_ref[...] + b_ref[...]
        cum_ref[...] = _seq_scan(-_softplus(-z), reverse=False)

    return pl.pallas_call(
        body, name=name, grid=(1,),
        in_specs=[pl.BlockSpec((s_len, LANES), lambda i: (0, fcol)), pl.BlockSpec((1, LANES), lambda i: (0, 0))],
        out_specs=pl.BlockSpec((s_len, LANES), lambda i: (0, 0)), out_shape=jax.ShapeDtypeStruct((s_len, LANES), F32),
        compiler_params=_params("arbitrary"))(qkvf, b_f)


def _fgate_bwd(qkvf, b_f, dcum_a, dcum_b, *, fcol, n_heads, name):
    s_len = qkvf.shape[0]

    def body(f_ref, b_ref, da_ref, db_ref, dz_ref, dbias_ref):
        z = f_ref[...] + b_ref[...]
        dlog = _seq_scan(da_ref[...] + db_ref[...], reverse=True)
        dz = dlog * jnp.exp(-_softplus(z))
        dz = jnp.where(_iota(dz.shape, 1) < n_heads, dz, 0.0)
        dz_ref[...] = dz.astype(BF16)
        dbias_ref[...] = jnp.sum(dz, axis=0, keepdims=True)

    full = pl.BlockSpec((s_len, LANES), lambda i: (0, 0))
    vec = pl.BlockSpec((1, LANES), lambda i: (0, 0))
    return pl.pallas_call(
        body, name=name, grid=(1,),
        in_specs=[pl.BlockSpec((s_len, LANES), lambda i: (0, fcol)), vec, full, full],
        out_specs=[full, vec], out_shape=[jax.ShapeDtypeStruct((s_len, LANES), BF16), jax.ShapeDtypeStruct((1, LANES), F32)],
        compiler_params=_params("arbitrary"))(qkvf, b_f, dcum_a, dcum_b)


def _rope_swap(x):
    half = MLA_ROPE // 2
    lane = _iota(x.shape, 1)
    sw = jnp.where(lane < half, pltpu.roll(x, LANES - half, axis=1), pltpu.roll(x, half, axis=1))
    return jnp.where(lane < MLA_ROPE, sw, 0.0)


def _mla_prep_fwd(qp, kv, c, gq, gk, cos_t, sin_t, *, n_heads, name, bs=512):
    s_len = qp.shape[0]
    bs = _div_block(s_len, bs, 8)
    krope_col = (MLA_Q_RANK + MLA_KV_RANK) // LANES

    def body(qn_ref, qr_ref, kn_ref, kr_ref, gq_ref, gk_ref, cos_ref, sin_ref, qc_ref, kc_ref):
        cos_v, sin_v = cos_ref[...], sin_ref[...]

        def rope(x, g):
            xv = x.astype(F32)
            inv = lax.rsqrt(jnp.sum(xv * xv, axis=-1, keepdims=True) / MLA_ROPE + NORM_EPS)
            y = xv * inv * g
            return y * cos_v + _rope_swap(y) * sin_v

        qc_ref[:, :LANES] = _head_norm(qn_ref[...], gq_ref[0]).astype(BF16)
        qc_ref[:, LANES:] = rope(qr_ref[...], gq_ref[1]).astype(BF16)
        kc_ref[:, :LANES] = _head_norm(kn_ref[...], gk_ref[0]).astype(BF16)
        kc_ref[:, LANES:] = rope(kr_ref[...], gk_ref[1]).astype(BF16)

    blk = lambda f: pl.BlockSpec((bs, LANES), f)
    gspec = pl.BlockSpec((2, 1, LANES), lambda i, h: (0, 0, 0))
    tspec = pl.BlockSpec((bs, LANES), lambda i, h: (i, 0))
    ospec = pl.BlockSpec((bs, 2 * LANES), lambda i, h: (i, h))
    oshape = jax.ShapeDtypeStruct((s_len, n_heads * 2 * LANES), BF16)
    return pl.pallas_call(
        body, name=name, grid=(s_len // bs, n_heads),
        in_specs=[blk(lambda i, h: (i, h)), blk(lambda i, h: (i, n_heads + h)), blk(lambda i, h: (i, 2 * h)),
                  blk(lambda i, h: (i, krope_col)), gspec, gspec, tspec, tspec],
        out_specs=[ospec, ospec], out_shape=[oshape, oshape],
        compiler_params=_params("parallel", "parallel"))(qp, qp, kv, c, gq, gk, cos_t, sin_t)


def _mla_prep_bwd(qp, kv, c, gq, gk, cos_t, sin_t, dqc, dkc, dv, *, n_heads, name, bs=512):
    s_len = qp.shape[0]
    bs = _div_block(s_len, bs, 8)
    krope_col = (MLA_Q_RANK + MLA_KV_RANK) // LANES

    def body(qn_ref, qr_ref, kn_ref, kr_ref, gq_ref, gk_ref, cos_ref, sin_ref, dqc_ref, dkc_ref, dv_ref,
             dqn_ref, dqr_ref, dkv_ref, dkr_ref, dgq_ref, dgk_ref):
        i, h = pl.program_id(0), pl.program_id(1)
        cos_v, sin_v = cos_ref[...], sin_ref[...]

        @pl.when((i == 0) & (h == 0))
        def _():
            dgq_ref[...] = jnp.zeros_like(dgq_ref)
            dgk_ref[...] = jnp.zeros_like(dgk_ref)

        @pl.when(h == 0)
        def _():
            dkr_ref[...] = jnp.zeros_like(dkr_ref)

        def unrope(dy):
            dy = dy.astype(F32)
            return dy * cos_v + _rope_swap(dy * sin_v)

        dqn, dg = _rms_bwd_math(qn_ref[...].astype(F32), gq_ref[0], dqc_ref[:, :LANES].astype(F32), MLA_NOPE)
        dgq_ref[0] += jnp.sum(dg, axis=0, keepdims=True)
        dqn_ref[...] = dqn.astype(BF16)
        dqr, dg = _rms_bwd_math(qr_ref[...].astype(F32), gq_ref[1], unrope(dqc_ref[:, LANES:]), MLA_ROPE)
        dgq_ref[1] += jnp.sum(dg, axis=0, keepdims=True)
        dqr_ref[...] = dqr.astype(BF16)
        dkn, dg = _rms_bwd_math(kn_ref[...].astype(F32), gk_ref[0], dkc_ref[:, :LANES].astype(F32), MLA_NOPE)
        dgk_ref[0] += jnp.sum(dg, axis=0, keepdims=True)
        dkv_ref[:, :LANES] = dkn.astype(BF16)
        dkv_ref[:, LANES:] = dv_ref[...]
        dkr, dg = _rms_bwd_math(kr_ref[...].astype(F32), gk_ref[1], unrope(dkc_ref[:, LANES:]), MLA_ROPE)
        dgk_ref[1] += jnp.sum(dg, axis=0, keepdims=True)
        dkr_ref[...] += dkr

    blk = lambda f: pl.BlockSpec((bs, LANES), f)
    gspec = pl.BlockSpec((2, 1, LANES), lambda i, h: (0, 0, 0))
    tspec = pl.BlockSpec((bs, LANES), lambda i, h: (i, 0))
    cat = pl.BlockSpec((bs, 2 * LANES), lambda i, h: (i, h))
    head = blk(lambda i, h: (i, h))
    hshape = jax.ShapeDtypeStruct((s_len, n_heads * LANES), BF16)
    gshape = jax.ShapeDtypeStruct((2, 1, LANES), F32)
    return pl.pallas_call(
        body, name=name, grid=(s_len // bs, n_heads),
        in_specs=[head, blk(lambda i, h: (i, n_heads + h)), blk(lambda i, h: (i, 2 * h)),
                  blk(lambda i, h: (i, krope_col)), gspec, gspec, tspec, tspec, cat, cat, head],
        out_specs=[head, head, cat, tspec, gspec, gspec],
        out_shape=[hshape, hshape, jax.ShapeDtypeStruct((s_len, n_heads * 2 * LANES), BF16),
                   jax.ShapeDtypeStruct((s_len, LANES), F32), gshape, gshape],
        compiler_params=_params("arbitrary", "arbitrary"))(qp, qp, kv, c, gq, gk, cos_t, sin_t, dqc, dkc, dv)


def _mla_latent_fwd(c, ga, *, name, br=256):
    s_len = c.shape[0]
    br = _div_block(s_len, br, 8)

    def body(c_ref, g_ref, o_ref):
        for part in range(2):
            sl = slice(part * MLA_Q_RANK, (part + 1) * MLA_Q_RANK)
            o_ref[:, sl] = _head_norm(c_ref[:, sl], g_ref[:, sl]).astype(BF16)

    w = MLA_Q_RANK + MLA_KV_RANK
    return pl.pallas_call(
        body, name=name, grid=(s_len // br,),
        in_specs=[pl.BlockSpec((br, w), lambda i: (i, 0)), pl.BlockSpec((1, w), lambda i: (0, 0))],
        out_specs=pl.BlockSpec((br, w), lambda i: (i, 0)), out_shape=jax.ShapeDtypeStruct((s_len, w), BF16),
        compiler_params=_params("parallel"))(c, ga)


def _mla_latent_bwd(c, ga, dcn_q, dcn_kv, dk_rope, *, name, br=256):
    s_len, cw = c.shape
    br = _div_block(s_len, br, 8)
    w = MLA_Q_RANK + MLA_KV_RANK

    def body(c_ref, g_ref, dq_ref, dkv_ref, dkr_ref, dc_ref, dg_ref):
        @pl.when(pl.program_id(0) == 0)
        def _():
            dg_ref[...] = jnp.zeros_like(dg_ref)

        for part, d_ref in enumerate((dq_ref, dkv_ref)):
            sl = slice(part * MLA_Q_RANK, (part + 1) * MLA_Q_RANK)
            dx, dg = _rms_bwd_math(c_ref[:, sl].astype(F32), g_ref[:, sl], d_ref[...].astype(F32), MLA_Q_RANK)
            dc_ref[:, sl] = dx.astype(BF16)
            dg_ref[:, sl] += jnp.sum(dg, axis=0, keepdims=True)
        dc_ref[:, w:] = dkr_ref[...].astype(BF16)

    return pl.pallas_call(
        body, name=name, grid=(s_len // br,),
        in_specs=[pl.BlockSpec((br, w), lambda i: (i, 0)), pl.BlockSpec((1, w), lambda i: (0, 0)),
                  pl.BlockSpec((br, MLA_Q_RANK), lambda i: (i, 0)), pl.BlockSpec((br, MLA_KV_RANK), lambda i: (i, 0)),
                  pl.BlockSpec((br, LANES), lambda i: (i, 0))],
        out_specs=[pl.BlockSpec((br, cw), lambda i: (i, 0)), pl.BlockSpec((1, w), lambda i: (0, 0))],
        out_shape=[jax.ShapeDtypeStruct((s_len, cw), BF16), jax.ShapeDtypeStruct((1, w), F32)],
        compiler_params=_params("arbitrary"))(c, ga, dcn_q, dcn_kv, dk_rope)


_GELU_C = 0.7978845608028654


def _gelu(x):
    return 0.5 * x * (1.0 + jnp.tanh(_GELU_C * (x + 0.044715 * x * x * x)))


def _gelu_grad(x):
    t = jnp.tanh(_GELU_C * (x + 0.044715 * x * x * x))
    return 0.5 * (1.0 + t) + 0.5 * x * (1.0 - t * t) * _GELU_C * (1.0 + 3 * 0.044715 * x * x)


def _sgu_act_fwd(uv, vg, *, name, br=256):
    s_len, w2 = uv.shape
    w = w2 // 2
    br = _div_block(s_len, br, 8)

    def body(uv_ref, g_ref, u_ref, v_ref):
        u_ref[...] = _gelu(uv_ref[:, :w])
        v_ref[...] = _head_norm(_gelu(uv_ref[:, w:]), g_ref[...]).astype(BF16)

    row = lambda c: pl.BlockSpec((br, c), lambda i: (i, 0))
    return pl.pallas_call(
        body, name=name, grid=(s_len // br,), in_specs=[row(w2), pl.BlockSpec((1, w), lambda i: (0, 0))],
        out_specs=[row(w), row(w)], out_shape=[jax.ShapeDtypeStruct((s_len, w), F32), jax.ShapeDtypeStruct((s_len, w), BF16)],
        compiler_params=_params("parallel"))(uv, vg)


def _sgu_act_bwd(uv, vg, du, dvn, *, name, br=256):
    s_len, w2 = uv.shape
    w = w2 // 2
    br = _div_block(s_len, br, 8)

    def body(uv_ref, g_ref, du_ref, dvn_ref, duv_ref, dg_ref):
        @pl.when(pl.program_id(0) == 0)
        def _():
            dg_ref[...] = jnp.zeros_like(dg_ref)

        up, vp = uv_ref[:, :w], uv_ref[:, w:]
        duv_ref[:, :w] = (du_ref[...] * _gelu_grad(up)).astype(BF16)
        dva, dg = _rms_bwd_math(_gelu(vp), g_ref[...], dvn_ref[...], w)
        dg_ref[...] += jnp.sum(dg, axis=0, keepdims=True)
        duv_ref[:, w:] = (dva * _gelu_grad(vp)).astype(BF16)

    row = lambda c: pl.BlockSpec((br, c), lambda i: (i, 0))
    vec = pl.BlockSpec((1, w), lambda i: (0, 0))
    return pl.pallas_call(
        body, name=name, grid=(s_len // br,), in_specs=[row(w2), vec, row(w), row(w)], out_specs=[row(w2), vec],
        out_shape=[jax.ShapeDtypeStruct((s_len, w2), BF16), jax.ShapeDtypeStruct((1, w), F32)],
        compiler_params=_params("arbitrary"))(uv, vg, du, dvn)


def _tril_weights(ws_ref):
    t, s = _iota((SGU_CHUNK, SGU_CHUNK), 0), _iota((SGU_CHUNK, SGU_CHUNK), 1)
    keep = s <= t
    return jnp.where(keep, ws_ref[...], 0.0), keep


def _sgu_mix_fwd(u, vn, w_s, b_s, *, name):
    s_len, w = u.shape
    nc = s_len // SGU_CHUNK

    def body(u_ref, v_ref, ws_ref, b_ref, o_ref):
        wm = _tril_weights(ws_ref)[0].astype(BF16)
        for n in range(nc):
            rows = slice(n * SGU_CHUNK, (n + 1) * SGU_CHUNK)
            mixed = _dot(wm, v_ref[rows, :], 1, 0) + b_ref[...]
            o_ref[rows, :] = (u_ref[rows, :] * mixed).astype(BF16)

    col = pl.BlockSpec((s_len, LANES), lambda g: (0, g))
    return pl.pallas_call(
        body, name=name, grid=(w // LANES,),
        in_specs=[col, col, pl.BlockSpec((None, SGU_CHUNK, SGU_CHUNK), lambda g: (g, 0, 0)),
                  pl.BlockSpec((None, SGU_CHUNK, 1), lambda g: (g, 0, 0))],
        out_specs=col, out_shape=jax.ShapeDtypeStruct((s_len, w), BF16),
        compiler_params=_params("parallel"))(u, vn, w_s, b_s)


def _sgu_mix_bwd(u, vn, w_s, b_s, dgated, *, name):
    s_len, w = u.shape
    nc = s_len // SGU_CHUNK

    def body(u_ref, v_ref, ws_ref, b_ref, dg_ref, du_ref, dv_ref, dws_ref, dbs_ref):
        wf, keep = _tril_weights(ws_ref)
        wm = wf.astype(BF16)
        wmt = wf.T.astype(BF16)
        dws = jnp.zeros((SGU_CHUNK, SGU_CHUNK), F32)
        dbs = jnp.zeros((SGU_CHUNK, 1), F32)
        for n in range(nc):
            rows = slice(n * SGU_CHUNK, (n + 1) * SGU_CHUNK)
            vb = v_ref[rows, :]
            dgv = dg_ref[rows, :].astype(F32)
            mixed = _dot(wm, vb, 1, 0) + b_ref[...]
            du_ref[rows, :] = dgv * mixed
            dm = dgv * u_ref[rows, :]
            dmb = dm.astype(BF16)
            dws = dws + _dot(dmb, vb, 1, 1)
            dbs = dbs + jnp.sum(dm, axis=1, keepdims=True)
            dv_ref[rows, :] = _dot(wmt, dmb, 1, 0)
        dws_ref[...] = jnp.where(keep, dws, 0.0)
        dbs_ref[...] = dbs

    col = pl.BlockSpec((s_len, LANES), lambda g: (0, g))
    wspec = pl.BlockSpec((None, SGU_CHUNK, SGU_CHUNK), lambda g: (g, 0, 0))
    bspec = pl.BlockSpec((None, SGU_CHUNK, 1), lambda g: (g, 0, 0))
    return pl.pallas_call(
        body, name=name, grid=(w // LANES,), in_specs=[col, col, wspec, bspec, col],
        out_specs=[col, col, wspec, bspec],
        out_shape=[jax.ShapeDtypeStruct((s_len, w), F32), jax.ShapeDtypeStruct((s_len, w), F32),
                   jax.ShapeDtypeStruct(w_s.shape, F32), jax.ShapeDtypeStruct(b_s.shape, F32)],
        compiler_params=_params("parallel"))(u, vn, w_s, b_s, dgated)


def _shift_down(x, k):
    if k == 0:
        return x
    return jnp.where(_iota(x.shape, 0) >= k, pltpu.roll(x, k, axis=0), 0.0)


def _shift_up(x, k):
    if k == 0:
        return x
    n = x.shape[0]
    return jnp.where(_iota(x.shape, 0) < n - k, pltpu.roll(x, n - k, axis=0), 0.0)


def _conv(u, w_ref, b_ref):
    return b_ref[...] + w_ref[0:1, :] * _shift_down(u, 2) + w_ref[1:2, :] * _shift_down(u, 1) + w_ref[2:3, :] * u


def _sigmoid(x):
    return 1.0 / (1.0 + jnp.exp(-x))


def _glu_fwd(up, cw, cb, *, name, bc=256):
    s_len, f2 = up.shape
    f = f2 // 2
    bc = _div_block(f, bc)
    nf = f // bc

    def body(ug_ref, uv_ref, wg_ref, wv_ref, bg_ref, bv_ref, o_ref):
        yg = _conv(ug_ref[...], wg_ref, bg_ref)
        yv = _conv(uv_ref[...], wv_ref, bv_ref)
        o_ref[...] = (yg * _sigmoid(yg) * yv).astype(BF16)

    big = lambda off: pl.BlockSpec((s_len, bc), lambda j: (0, j + off))
    wsp = lambda off: pl.BlockSpec((3, bc), lambda j: (0, j + off))
    bsp = lambda off: pl.BlockSpec((1, bc), lambda j: (0, j + off))
    return pl.pallas_call(
        body, name=name, grid=(nf,), in_specs=[big(0), big(nf), wsp(0), wsp(nf), bsp(0), bsp(nf)],
        out_specs=pl.BlockSpec((s_len, bc), lambda j: (0, j)), out_shape=jax.ShapeDtypeStruct((s_len, f), BF16),
        compiler_params=_params("parallel"))(up, up, cw, cw, cb, cb)


def _glu_bwd(up, cw, cb, dact, *, name, bc=256):
    s_len, f2 = up.shape
    f = f2 // 2
    bc = _div_block(f, bc)
    nf = f // bc

    def body(ut_ref, up_ref, wt_ref, wp_ref, bt_ref, bp_ref, da_ref, du_ref, dw_ref, db_ref):
        is_gate = pl.program_id(0) < nf
        ut = ut_ref[...]
        yt = _conv(ut, wt_ref, bt_ref)
        yp = _conv(up_ref[...], wp_ref, bp_ref)
        da = da_ref[...].astype(F32)
        sg = _sigmoid(yt)
        d_gate = da * yp * (sg * (1.0 + yt * (1.0 - sg)))
        d_val = da * (yp * _sigmoid(yp))
        dy = jnp.where(is_gate, d_gate, d_val)
        db_ref[...] = jnp.sum(dy, axis=0, keepdims=True)
        dw_ref[0:1, :] = jnp.sum(dy * _shift_down(ut, 2), axis=0, keepdims=True)
        dw_ref[1:2, :] = jnp.sum(dy * _shift_down(ut, 1), axis=0, keepdims=True)
        dw_ref[2:3, :] = jnp.sum(dy * ut, axis=0, keepdims=True)
        du = wt_ref[2:3, :] * dy + wt_ref[1:2, :] * _shift_up(dy, 1) + wt_ref[0:1, :] * _shift_up(dy, 2)
        du_ref[...] = du.astype(BF16)

    this = lambda r: pl.BlockSpec((r, bc), lambda j: (0, j))
    partner = lambda r: pl.BlockSpec((r, bc), lambda j: (0, (j + nf) % (2 * nf)))
    return pl.pallas_call(
        body, name=name, grid=(2 * nf,),
        in_specs=[this(s_len), partner(s_len), this(3), partner(3), this(1), partner(1),
                  pl.BlockSpec((s_len, bc), lambda j: (0, j % nf))],
        out_specs=[this(s_len), this(3), this(1)],
        out_shape=[jax.ShapeDtypeStruct((s_len, f2), BF16), jax.ShapeDtypeStruct((3, f2), F32),
                   jax.ShapeDtypeStruct((1, f2), F32)],
        compiler_params=_params("parallel"))(up, up, cw, cw, cb, cb, dact)


def _as2d(a):
    return a.reshape(-1, a.shape[-1]) if a.ndim >= 2 else a.reshape(1, -1)


def _adamw(w, g, m, v, *, name, target_bytes=1 << 20):
    shape = w.shape
    w2, m2, v2 = _as2d(w), _as2d(m), _as2d(v)
    g2 = g.reshape(w2.shape)
    r, c = w2.shape
    br = r if r * c * 4 <= target_bytes else _div_block(r, max(8, target_bytes // (4 * c) // 8 * 8), 8)
    c1 = 1.0 - ADAM_B1 ** ADAM_STEP
    c2 = 1.0 - ADAM_B2 ** ADAM_STEP

    def body(w_ref, g_ref, m_ref, v_ref, d_ref, nm_ref, nv_ref):
        gv = g_ref[...]
        nm = ADAM_B1 * m_ref[...] + (1.0 - ADAM_B1) * gv
        nv = ADAM_B2 * v_ref[...] + (1.0 - ADAM_B2) * (gv * gv)
        nm_ref[...] = nm
        nv_ref[...] = nv
        d_ref[...] = -ADAM_LR * ((nm / c1) / (jnp.sqrt(nv / c2) + ADAM_EPS) + ADAM_WD * w_ref[...])

    spec = pl.BlockSpec((br, c), lambda i: (i, 0))
    sds = jax.ShapeDtypeStruct((r, c), F32)
    d, nm, nv = pl.pallas_call(
        body, name=name, grid=(r // br,), in_specs=[spec] * 4, out_specs=[spec] * 3, out_shape=[sds] * 3,
        compiler_params=_params("parallel"))(w2, g2, m2, v2)
    return d.reshape(shape), nm.reshape(shape), nv.reshape(shape)


def _add_halves(g, recv, c_idx, *, name, target_bytes=1 << 20):
    _, _, r, c = g.shape
    br = _div_block(r, max(16, target_bytes // (2 * c) // 16 * 16), 16)

    def body(c_ref, g_ref, r_ref, o_ref):
        o_ref[...] = (g_ref[...].astype(F32) + r_ref[...].astype(F32)).astype(BF16)

    return pl.pallas_call(
        body, name=name,
        grid_spec=pltpu.PrefetchScalarGridSpec(
            num_scalar_prefetch=1, grid=(N_CHIPS, r // br),
            in_specs=[pl.BlockSpec((None, None, br, c), lambda s, i, c_ref: (s, c_ref[0], i, 0)),
                      pl.BlockSpec((None, br, c), lambda s, i, c_ref: (s, i, 0))],
            out_specs=pl.BlockSpec((None, br, c), lambda s, i, c_ref: (s, i, 0))),
        out_shape=jax.ShapeDtypeStruct((N_CHIPS, r, c), BF16),
        compiler_params=_params("parallel", "parallel"))(c_idx, g, recv)


def _sum_chips(p, landed, place_idx, *, name, target_bytes=1 << 20):
    _, r, c = p.shape
    br = _div_block(r, max(16, target_bytes // (4 * c) // 16 * 16), 16)

    def body(idx_ref, p_ref, l1_ref, l2_ref, l3_ref, o_ref):
        o_ref[...] = ((p_ref[...].astype(F32) + l1_ref[...].astype(F32)) + l2_ref[...].astype(F32)) + l3_ref[...].astype(F32)

    slot = lambda k: pl.BlockSpec((None, br, c), lambda i, idx: ((idx[0] + k) % N_CHIPS, i, 0))
    return pl.pallas_call(
        body, name=name,
        grid_spec=pltpu.PrefetchScalarGridSpec(
            num_scalar_prefetch=1, grid=(r // br,), in_specs=[slot(0), slot(1), slot(2), slot(3)],
            out_specs=pl.BlockSpec((None, br, c), lambda i, idx: (idx[1], i, 0))),
        out_shape=jax.ShapeDtypeStruct((2, r, c), F32),
        compiler_params=_params("parallel"))(place_idx, p, landed, landed, landed)


def _place_shard(w, place_idx, *, dtype, name, target_bytes=1 << 20):
    r, c = w.shape
    hr = r // 2
    mult = 16 if dtype == BF16 else 8
    br = _div_block(hr, max(mult, target_bytes // (4 * c) // mult * mult), mult)
    nb = hr // br

    def body(idx_ref, w_ref, o_ref):
        o_ref[...] = w_ref[...].astype(dtype)

    return pl.pallas_call(
        body, name=name,
        grid_spec=pltpu.PrefetchScalarGridSpec(
            num_scalar_prefetch=1, grid=(2, nb), in_specs=[pl.BlockSpec((br, c), lambda h, i, idx: (h * nb + i, 0))],
            out_specs=pl.BlockSpec((None, None, br, c), lambda h, i, idx: (idx[0], h, i, 0))),
        out_shape=jax.ShapeDtypeStruct((N_CHIPS, 2, hr, c), dtype),
        compiler_params=_params("parallel", "parallel"))(place_idx, w)


def _sum_devices(x, *, name):
    n, r, c = x.shape
    br = _div_block(r, 512, 8)

    def body(x_ref, o_ref):
        acc = x_ref[0]
        for s in range(1, n):
            acc = acc + x_ref[s]
        o_ref[...] = acc

    return pl.pallas_call(
        body, name=name, grid=(r // br,), in_specs=[pl.BlockSpec((n, br, c), lambda i: (0, i, 0))],
        out_specs=pl.BlockSpec((br, c), lambda i: (i, 0)), out_shape=jax.ShapeDtypeStruct((r, c), F32),
        compiler_params=_params("parallel"))(x)


_ANY = pl.BlockSpec(memory_space=pl.ANY)


def _place():
    x, y, c = lax.axis_index("x"), lax.axis_index("y"), lax.axis_index("c")
    other_chips = [(1 - x, y), (x, 1 - y), (1 - x, 1 - y)]
    return x, y, c, other_chips


def _gather_ici(bufs, *, name):
    n = len(bufs)

    def body(*refs):
        b_refs = refs[n:2 * n]
        send_sems, recv_sems = refs[2 * n:]
        x, y, c, chips = _place()
        me = 2 * x + y
        sends = []
        for i in range(n):
            for j, (px, py) in enumerate(chips):
                cp = pltpu.make_async_remote_copy(src_ref=b_refs[i].at[me, c], dst_ref=b_refs[i].at[me, c],
                                                  send_sem=send_sems.at[i, j], recv_sem=recv_sems.at[i, j],
                                                  device_id=(px, py, c), device_id_type=MESH)
                cp.start()
                sends.append((cp, i, j, px, py))
        for cp, i, j, px, py in sends:
            landed = b_refs[i].at[2 * px + py, c]
            pltpu.make_async_remote_copy(src_ref=landed, dst_ref=landed, send_sem=send_sems.at[i, j],
                                         recv_sem=recv_sems.at[i, j], device_id=(px, py, c),
                                         device_id_type=MESH).wait_recv()
            cp.wait_send()

    return pl.pallas_call(
        body, name=name, in_specs=[_ANY] * n, out_specs=[_ANY] * n, input_output_aliases={i: i for i in range(n)},
        out_shape=[jax.ShapeDtypeStruct(a.shape, a.dtype) for a in bufs],
        scratch_shapes=[pltpu.SemaphoreType.DMA((n, 3)), pltpu.SemaphoreType.DMA((n, 3))],
    )(*bufs)


def _gather_d2d(bufs, *, name):
    n = len(bufs)

    def body(*refs):
        b_refs = refs[n:2 * n]
        send_sems, recv_sems = refs[2 * n:]
        x, y, c, chips = _place()
        sends = []
        for i in range(n):
            for j, (px, py) in enumerate(chips):
                mine = b_refs[i].at[2 * px + py, c]
                cp = pltpu.make_async_remote_copy(src_ref=mine, dst_ref=mine, send_sem=send_sems.at[i, j],
                                                  recv_sem=recv_sems.at[i, j], device_id=(x, y, 1 - c),
                                                  device_id_type=MESH)
                cp.start()
                sends.append((cp, i, j, px, py))
        for cp, i, j, px, py in sends:
            theirs = b_refs[i].at[2 * px + py, 1 - c]
            pltpu.make_async_remote_copy(src_ref=theirs, dst_ref=theirs, send_sem=send_sems.at[i, j],
                                         recv_sem=recv_sems.at[i, j], device_id=(x, y, 1 - c),
                                         device_id_type=MESH).wait_recv()
            cp.wait_send()

    return pl.pallas_call(
        body, name=name, in_specs=[_ANY] * n, out_specs=[_ANY] * n, input_output_aliases={i: i for i in range(n)},
        out_shape=[jax.ShapeDtypeStruct(a.shape, a.dtype) for a in bufs],
        scratch_shapes=[pltpu.SemaphoreType.DMA((n, 3)), pltpu.SemaphoreType.DMA((n, 3))],
    )(*bufs)


def _sibling_halves(gs, *, name):
    n = len(gs)

    def body(*refs):
        g_refs, o_refs = refs[:n], refs[n:2 * n]
        send_sems, recv_sems = refs[2 * n:]
        x, y, c, _ = _place()
        copies = []
        for i in range(n):
            for s in range(N_CHIPS):
                k = i * N_CHIPS + s
                cp = pltpu.make_async_remote_copy(src_ref=g_refs[i].at[s, 1 - c], dst_ref=o_refs[i].at[s],
                                                  send_sem=send_sems.at[k], recv_sem=recv_sems.at[k],
                                                  device_id=(x, y, 1 - c), device_id_type=MESH)
                cp.start()
                copies.append(cp)
        for cp in copies:
            cp.wait()

    return pl.pallas_call(
        body, name=name, in_specs=[_ANY] * n, out_specs=[_ANY] * n,
        out_shape=[jax.ShapeDtypeStruct((N_CHIPS,) + g.shape[2:], g.dtype) for g in gs],
        scratch_shapes=[pltpu.SemaphoreType.DMA((n * N_CHIPS,)), pltpu.SemaphoreType.DMA((n * N_CHIPS,))],
    )(*gs)


def _chip_scatter(ps, *, name):
    n = len(ps)

    def body(*refs):
        p_refs, o_refs = refs[:n], refs[n:2 * n]
        send_sems, recv_sems = refs[2 * n:]
        x, y, c, chips = _place()
        me = 2 * x + y
        copies = []
        for i in range(n):
            for j, (px, py) in enumerate(chips):
                cp = pltpu.make_async_remote_copy(src_ref=p_refs[i].at[2 * px + py], dst_ref=o_refs[i].at[me],
                                                  send_sem=send_sems.at[i, j], recv_sem=recv_sems.at[i, j],
                                                  device_id=(px, py, c), device_id_type=MESH)
                cp.start()
                copies.append((cp, i, j, px, py))
        for cp, i, j, px, py in copies:
            landed = o_refs[i].at[2 * px + py]
            pltpu.make_async_remote_copy(src_ref=landed, dst_ref=landed, send_sem=send_sems.at[i, j],
                                         recv_sem=recv_sems.at[i, j], device_id=(px, py, c),
                                         device_id_type=MESH).wait_recv()
            cp.wait_send()

    return pl.pallas_call(
        body, name=name, in_specs=[_ANY] * n, out_specs=[_ANY] * n,
        out_shape=[jax.ShapeDtypeStruct(p.shape, p.dtype) for p in ps],
        scratch_shapes=[pltpu.SemaphoreType.DMA((n, 3)), pltpu.SemaphoreType.DMA((n, 3))],
    )(*ps)


def _sibling_share(bufs, *, name):
    n = len(bufs)

    def body(*refs):
        b_refs = refs[n:2 * n]
        send_sems, recv_sems = refs[2 * n:]
        x, y, c, _ = _place()
        copies = []
        for i in range(n):
            cp = pltpu.make_async_remote_copy(src_ref=b_refs[i].at[c], dst_ref=b_refs[i].at[c], send_sem=send_sems.at[i],
                                              recv_sem=recv_sems.at[i], device_id=(x, y, 1 - c), device_id_type=MESH)
            cp.start()
            copies.append((cp, i))
        for cp, i in copies:
            theirs = b_refs[i].at[1 - c]
            pltpu.make_async_remote_copy(src_ref=theirs, dst_ref=theirs, send_sem=send_sems.at[i],
                                         recv_sem=recv_sems.at[i], device_id=(x, y, 1 - c),
                                         device_id_type=MESH).wait_recv()
            cp.wait_send()

    return pl.pallas_call(
        body, name=name, in_specs=[_ANY] * n, out_specs=[_ANY] * n, input_output_aliases={i: i for i in range(n)},
        out_shape=[jax.ShapeDtypeStruct(a.shape, a.dtype) for a in bufs],
        scratch_shapes=[pltpu.SemaphoreType.DMA((n,)), pltpu.SemaphoreType.DMA((n,))],
    )(*bufs)


def _broadcast_all(v, *, name):
    def body(v_ref, o_ref, send_sems, recv_sems, local_sem):
        x, y, c, _ = _place()
        me = 4 * x + 2 * y + c
        loc = pltpu.make_async_copy(v_ref, o_ref.at[me], local_sem)
        loc.start()
        copies = []
        for k in range(1, 8):
            dx, dy, dc = (k >> 2) & 1, (k >> 1) & 1, k & 1
            to = (1 - x if dx else x, 1 - y if dy else y, 1 - c if dc else c)
            cp = pltpu.make_async_remote_copy(src_ref=v_ref, dst_ref=o_ref.at[me], send_sem=send_sems.at[k - 1],
                                              recv_sem=recv_sems.at[k - 1], device_id=to, device_id_type=MESH)
            cp.start()
            copies.append((cp, k, to))
        for cp, k, to in copies:
            cp.wait_send()
            theirs = o_ref.at[4 * to[0] + 2 * to[1] + to[2]]
            pltpu.make_async_remote_copy(src_ref=theirs, dst_ref=theirs, send_sem=send_sems.at[k - 1],
                                         recv_sem=recv_sems.at[k - 1], device_id=to, device_id_type=MESH).wait_recv()
        loc.wait()

    return pl.pallas_call(
        body, name=name, in_specs=[_ANY], out_specs=_ANY,
        out_shape=jax.ShapeDtypeStruct((8,) + v.shape, v.dtype),
        scratch_shapes=[pltpu.SemaphoreType.DMA((7,)), pltpu.SemaphoreType.DMA((7,)), pltpu.SemaphoreType.DMA(())],
    )(v)


def _gather_layer(shards, place_idx, *, name):
    names = list(shards)
    bufs = [_place_shard(shards[k], place_idx, dtype=F32 if k == 'small' else BF16, name=f"{name}_place_{k}")
            for k in names]
    bufs = _gather_d2d(_gather_ici(bufs, name=name + "_ici"), name=name + "_d2d")
    return {k: o.reshape((N_CHIPS,) + shards[k].shape) for k, o in zip(names, bufs)}


def _reduce_layer(grads, place_idx, *, name):
    names = list(grads)
    gs = [grads[k].reshape(N_CHIPS, 2, grads[k].shape[1] // 2, grads[k].shape[2]) for k in names]
    recv = _sibling_halves(gs, name=name + "_sib")
    ps = [_add_halves(g, r, place_idx[1:], name=f"{name}_add2_{k}") for g, r, k in zip(gs, recv, names)]
    landed = _chip_scatter(ps, name=name + "_scatter")
    rs = [_sum_chips(p, l, place_idx, name=f"{name}_sum4_{k}") for p, l, k in zip(ps, landed, names)]
    both = _sibling_share(rs, name=name + "_share")
    return {k: b.reshape(grads[k].shape[1:]) for k, b in zip(names, both)}


def _pad_lanes(a, n=LANES):
    return jnp.pad(a, [(0, 0)] * (a.ndim - 1) + [(0, n - a.shape[-1])])


def _unshard_cols(g):
    return jnp.transpose(g, (1, 0, 2)).reshape(g.shape[1], -1)


def _shard_cols(w):
    k, n = w.shape
    return jnp.transpose(w.reshape(k, N_CHIPS, n // N_CHIPS), (1, 0, 2))


def _ffn_fwd(h, p, tag):
    b = _rms_fwd(h, p['ffn_norm'], name=f"{tag}_ffn_norm")
    up = _mm(b, p['ffn_w_up'], b_sh='n', name=f"{tag}_ffn_up", bn=1408)
    act = _glu_fwd(up, p['ffn_conv_w'], p['ffn_conv_b'], name=f"{tag}_ffn_glu")
    out = _mm(act, p['ffn_w_down'], res=h, name=f"{tag}_ffn_down", bk=704)
    return out, (h, b, up, act)


def _ffn_bwd(dh, saved, p, tag):
    h, b, up, act = saved
    dact = _mm(dh, p['ffn_w_down'], tb=True, out_dtype=BF16, name=f"{tag}_ffn_dact", bn=1408)
    dw_down = _mm(act, dh, ta=True, out_dtype=BF16, name=f"{tag}_ffn_dwdown", bm=1408)
    dup, dcw, dcb = _glu_bwd(up, p['ffn_conv_w'], p['ffn_conv_b'], dact, name=f"{tag}_ffn_dglu")
    dw_up = _mm(b, dup, ta=True, o_sh=True, out_dtype=BF16, name=f"{tag}_ffn_dwup", bn=1408)
    db = _mm(dup, p['ffn_w_up'], b_sh='k', name=f"{tag}_ffn_db", bk=1408)
    dh_in, dg = _rms_bwd(h, p['ffn_norm'], db, res=dh, name=f"{tag}_ffn_dnorm")
    big = {'ffn_w_up': dw_up, 'ffn_w_down': dw_down.reshape(N_CHIPS, -1, dw_down.shape[1])}
    small = {'ffn_norm': dg, 'ffn_conv_w': dcw, 'ffn_conv_b': dcb}
    return dh_in, big, small


def _qkv_attn_fwd(kind, h, p, tag, n_heads):
    a = _rms_fwd(h, p['mix_norm'], name=f"{tag}_norm")
    if kind == 'fox':
        qkv = _mm(a, p['w_in'], name=f"{tag}_qkv", bn=896)
        cum = _fgate_fwd(qkv, p['b_f'], fcol=3 * n_heads, name=f"{tag}_fgate")
        cum_t = cum[:, :n_heads].T
        cq, ck = cum_t[:, :, None], cum_t[:, None, :]
    else:
        qkv = _mm(a, p['w_in'], b_sh='n', name=f"{tag}_qkv", bn=768)
        cq = ck = None
    cols = dict(qcol=lambda hh: hh, kcol=lambda hh: n_heads + hh, vcol=lambda hh: 2 * n_heads + hh)
    o = _attn_fwd(kind, qkv, qkv, qkv, name=f"{tag}_attn", n_heads=n_heads, dqk=HEAD_DIM, scale=HEAD_DIM ** -0.5,
                  gains=p['qk_gain'], cq=cq, ck=ck, **cols)
    out = _mm(o, p['w_out'], res=h, name=f"{tag}_out")
    return out, (h, a, qkv, o, cq, ck)


def _qkv_attn_bwd(kind, dh, saved, p, tag, n_heads):
    h, a, qkv, o, cq, ck = saved
    do = _mm(dh, p['w_out'], tb=True, out_dtype=BF16, name=f"{tag}_do")
    dw_out = _mm(o, dh, ta=True, out_dtype=BF16, name=f"{tag}_dwout")
    cols = dict(qcol=lambda hh: hh, kcol=lambda hh: n_heads + hh, vcol=lambda hh: 2 * n_heads + hh)
    outs = _attn_bwd(kind, qkv, qkv, qkv, o, do, name=f"{tag}_dattn", n_heads=n_heads, dqk=HEAD_DIM,
                     scale=HEAD_DIM ** -0.5, gains=p['qk_gain'], cq=cq, ck=ck, **cols)
    dq, dk, dv, dgain = outs[:4]
    small = {'q_gain': dgain[0], 'k_gain': dgain[1]}
    if kind == 'fox':
        dcq, dck = outs[4:]
        dca = _pad_lanes(dcq[:, :, 0].T)
        dcb = _pad_lanes(dck[:, 0, :].T)
        dflog, dbf = _fgate_bwd(qkv, p['b_f'], dca, dcb, fcol=3 * n_heads, n_heads=n_heads, name=f"{tag}_dfgate")
        small['b_f'] = dbf[:, :n_heads]
        dqkv = jnp.concatenate([dq, dk, dv, dflog], axis=1)
        dw_in = _mm(a, dqkv, ta=True, out_dtype=BF16, name=f"{tag}_dwin", bn=896)
        da = _mm(dqkv, p['w_in'], tb=True, name=f"{tag}_da", bk=896)
        dw_in = _shard_cols(dw_in[:, :3 * n_heads * HEAD_DIM + n_heads])
    else:
        dqkv = jnp.concatenate([dq, dk, dv], axis=1)
        dw_in = _mm(a, dqkv, ta=True, o_sh=True, out_dtype=BF16, name=f"{tag}_dwin", bn=768)
        da = _mm(dqkv, p['w_in'], b_sh='k', name=f"{tag}_da", bk=768)
    dh_in, dg = _rms_bwd(h, p['mix_norm'], da, res=dh, name=f"{tag}_dnorm")
    small['mix_norm'] = dg
    big = {'w_in': dw_in, 'w_out': dw_out.reshape(N_CHIPS, -1, dw_out.shape[1])}
    return dh_in, big, small


def _mla_fwd(h, p, tag, n_heads):
    a = _rms_fwd(h, p['mix_norm'], name=f"{tag}_norm")
    c = _mm(a, p['w_in'], name=f"{tag}_latent", bn=1152)
    cn = _mla_latent_fwd(c, p['a_gain'], name=f"{tag}_latent_norm")
    qp = _mm(cn[:, :MLA_Q_RANK], p['w_q_b'], name=f"{tag}_q_up")
    kv = _mm(cn[:, MLA_Q_RANK:], p['w_kv_b'], b_sh='n', name=f"{tag}_kv_up")
    qc, kc = _mla_prep_fwd(qp, kv, c, p['gq'], p['gk'], p['cos'], p['sin'], n_heads=n_heads, name=f"{tag}_prep")
    cols = dict(qcol=lambda hh: hh, kcol=lambda hh: hh, vcol=lambda hh: 2 * hh + 1)
    scale = (MLA_NOPE + MLA_ROPE) ** -0.5
    o = _attn_fwd('mla', qc, kc, kv, name=f"{tag}_attn", n_heads=n_heads, dqk=2 * LANES, scale=scale, **cols)
    out = _mm(o, p['w_out'], res=h, name=f"{tag}_out")
    return out, (h, a, c, cn, qp, kv, qc, kc, o)


def _mla_bwd(dh, saved, p, tag, n_heads):
    h, a, c, cn, qp, kv, qc, kc, o = saved
    do = _mm(dh, p['w_out'], tb=True, out_dtype=BF16, name=f"{tag}_do")
    dw_out = _mm(o, dh, ta=True, out_dtype=BF16, name=f"{tag}_dwout")
    cols = dict(qcol=lambda hh: hh, kcol=lambda hh: hh, vcol=lambda hh: 2 * hh + 1)
    scale = (MLA_NOPE + MLA_ROPE) ** -0.5
    dqc, dkc, dv = _attn_bwd('mla', qc, kc, kv, o, do, name=f"{tag}_dattn", n_heads=n_heads, dqk=2 * LANES,
                             scale=scale, **cols)
    dqn, dqr, dkv, dkr, dgq, dgk = _mla_prep_bwd(qp, kv, c, p['gq'], p['gk'], p['cos'], p['sin'], dqc, dkc, dv,
                                                 n_heads=n_heads, name=f"{tag}_dprep")
    dqp = jnp.concatenate([dqn, dqr], axis=1)
    cn_q, cn_kv = cn[:, :MLA_Q_RANK], cn[:, MLA_Q_RANK:]
    dw_q_b = _mm(cn_q, dqp, ta=True, out_dtype=BF16, name=f"{tag}_dwqb", bm=512)
    dcn_q = _mm(dqp, p['w_q_b'], tb=True, out_dtype=BF16, name=f"{tag}_dcnq")
    dw_kv_b = _mm(cn_kv, dkv, ta=True, o_sh=True, out_dtype=BF16, name=f"{tag}_dwkvb", bm=512)
    dcn_kv = _mm(dkv, p['w_kv_b'], b_sh='k', out_dtype=BF16, name=f"{tag}_dcnkv")
    dc, dga = _mla_latent_bwd(c, p['a_gain'], dcn_q, dcn_kv, dkr, name=f"{tag}_dlatent")
    dw_in = _mm(a, dc, ta=True, out_dtype=BF16, name=f"{tag}_dwin", bn=1152)
    da = _mm(dc, p['w_in'], tb=True, name=f"{tag}_da", bk=1152)
    dh_in, dg = _rms_bwd(h, p['mix_norm'], da, res=dh, name=f"{tag}_dnorm")
    k_rank = dw_q_b.shape[0]
    nope = dw_q_b[:, :n_heads * LANES].reshape(k_rank, n_heads, LANES)
    rope = dw_q_b[:, n_heads * LANES:].reshape(k_rank, n_heads, LANES)[:, :, :MLA_ROPE]
    dw_q_b = jnp.concatenate([nope, rope], axis=2).reshape(k_rank, n_heads * (MLA_NOPE + MLA_ROPE))
    w_in_cols = MLA_Q_RANK + MLA_KV_RANK + MLA_ROPE
    big = {'w_in': dw_in[:, :w_in_cols].reshape(N_CHIPS, -1, w_in_cols), 'w_q_b': _shard_cols(dw_q_b),
           'w_kv_b': dw_kv_b, 'w_out': dw_out.reshape(N_CHIPS, -1, dw_out.shape[1])}
    small = {'mix_norm': dg, 'q_a_gain': dga[:, :MLA_Q_RANK], 'kv_a_gain': dga[:, MLA_Q_RANK:],
             'q_gain': jnp.concatenate([dgq[0], dgq[1][:, :MLA_ROPE]], axis=1),
             'k_gain': jnp.concatenate([dgk[0], dgk[1][:, :MLA_ROPE]], axis=1)}
    return dh_in, big, small


def _sgu_fwd(h, p, tag):
    a = _rms_fwd(h, p['mix_norm'], name=f"{tag}_norm")
    uv = _mm(a, p['w_in'], b_sh='n', name=f"{tag}_in")
    u, vn = _sgu_act_fwd(uv, p['v_gain'], name=f"{tag}_act")
    gated = _sgu_mix_fwd(u, vn, p['w_s'], p['b_s'], name=f"{tag}_mix")
    out = _mm(gated, p['w_out'], res=h, name=f"{tag}_out")
    return out, (h, a, uv, u, vn, gated)


def _sgu_bwd(dh, saved, p, tag):
    h, a, uv, u, vn, gated = saved
    dgated = _mm(dh, p['w_out'], tb=True, out_dtype=BF16, name=f"{tag}_dgated")
    dw_out = _mm(gated, dh, ta=True, out_dtype=BF16, name=f"{tag}_dwout")
    du, dvn, dws, dbs = _sgu_mix_bwd(u, vn, p['w_s'], p['b_s'], dgated, name=f"{tag}_dmix")
    duv, dvg = _sgu_act_bwd(uv, p['v_gain'], du, dvn, name=f"{tag}_dact")
    dw_in = _mm(a, duv, ta=True, o_sh=True, out_dtype=BF16, name=f"{tag}_dwin")
    da = _mm(duv, p['w_in'], b_sh='k', name=f"{tag}_da")
    dh_in, dg = _rms_bwd(h, p['mix_norm'], da, res=dh, name=f"{tag}_dnorm")
    big = {'w_in': dw_in, 'w_out': dw_out.reshape(N_CHIPS, -1, dw_out.shape[1])}
    small = {'mix_norm': dg, 'v_gain': dvg, 'w_s': dws, 'b_s': dbs[:, :, 0]}
    return dh_in, big, small


def _pack(parts):
    flat = jnp.concatenate([p.reshape(-1).astype(F32) for p in parts])
    rows = -(-flat.shape[0] // LANES)
    rows = -(-rows // 32) * 32
    return jnp.pad(flat, (0, rows * LANES - flat.shape[0])).reshape(rows, LANES)


def _unpack(packed, shapes):
    flat = packed.reshape(-1)
    out, off = [], 0
    for s in shapes:
        n = 1
        for d in s:
            n *= d
        out.append(flat[off:off + n].reshape(s))
        off += n
    return out


MIXERS = ('fox', 'mla', 'sb', 'sgu')
WEIGHT_NAMES = ['mix_norm', 'ffn_norm', 'fox_w_in', 'fox_b_f', 'fox_q_gain', 'fox_k_gain', 'fox_w_out', 'mla_w_in',
                'mla_q_a_gain', 'mla_kv_a_gain', 'mla_w_q_b', 'mla_w_kv_b', 'mla_q_gain', 'mla_k_gain', 'mla_w_out',
                'sb_w_in', 'sb_q_gain', 'sb_k_gain', 'sb_w_out', 'sgu_w_in', 'sgu_v_gain', 'sgu_w_s', 'sgu_b_s',
                'sgu_w_out', 'ffn_w_up', 'ffn_conv_w', 'ffn_conv_b', 'ffn_w_down']
SMALL_SHARDED = {'mla_q_a_gain': 1, 'mla_kv_a_gain': 1, 'sgu_v_gain': 1, 'ffn_conv_w': 2}
BIG = ['fox_w_in', 'fox_w_out', 'mla_w_in', 'mla_w_q_b', 'mla_w_kv_b', 'mla_w_out', 'sb_w_in', 'sb_w_out', 'sgu_w_in',
       'sgu_w_out', 'ffn_w_up', 'ffn_w_down']


def kernel(x, positions, mix_norm, ffn_norm, fox_w_in, fox_b_f, fox_q_gain, fox_k_gain, fox_w_out, mla_w_in, mla_q_a_gain, mla_kv_a_gain, mla_w_q_b, mla_w_kv_b, mla_q_gain, mla_k_gain, mla_w_out, sb_w_in, sb_q_gain, sb_k_gain, sb_w_out, sgu_w_in, sgu_v_gain, sgu_w_s, sgu_b_s, sgu_w_out, ffn_w_up, ffn_conv_w, ffn_conv_b, ffn_w_down, loss_target, m_mix_norm, m_ffn_norm, m_fox_w_in, m_fox_b_f, m_fox_q_gain, m_fox_k_gain, m_fox_w_out, m_mla_w_in, m_mla_q_a_gain, m_mla_kv_a_gain, m_mla_w_q_b, m_mla_w_kv_b, m_mla_q_gain, m_mla_k_gain, m_mla_w_out, m_sb_w_in, m_sb_q_gain, m_sb_k_gain, m_sb_w_out, m_sgu_w_in, m_sgu_v_gain, m_sgu_w_s, m_sgu_b_s, m_sgu_w_out, m_ffn_w_up, m_ffn_conv_w, m_ffn_conv_b, m_ffn_w_down, v_mix_norm, v_ffn_norm, v_fox_w_in, v_fox_b_f, v_fox_q_gain, v_fox_k_gain, v_fox_w_out, v_mla_w_in, v_mla_q_a_gain, v_mla_kv_a_gain, v_mla_w_q_b, v_mla_w_kv_b, v_mla_q_gain, v_mla_k_gain, v_mla_w_out, v_sb_w_in, v_sb_q_gain, v_sb_k_gain, v_sb_w_out, v_sgu_w_in, v_sgu_v_gain, v_sgu_w_s, v_sgu_b_s, v_sgu_w_out, v_ffn_w_up, v_ffn_conv_w, v_ffn_conv_b, v_ffn_w_down):
    args = dict(locals())
    W = {k: args[k] for k in WEIGHT_NAMES}
    M = {k: args['m_' + k] for k in WEIGHT_NAMES}
    V = {k: args['v_' + k] for k in WEIGHT_NAMES}
    depth = mix_norm.shape[0]
    s_len, d_model = x.shape[1], x.shape[2]
    n_heads = d_model // HEAD_DIM
    assert all(W[k].shape[0] == 1 for k in WEIGHT_NAMES if k.split('_')[0] in MIXERS), "one layer per mixer"
    xi, yi, ci = lax.axis_index("x"), lax.axis_index("y"), lax.axis_index("c")
    chip = 2 * xi + yi
    place_idx = jnp.stack([chip, ci]).astype(jnp.int32)

    small_local = _pack([W[k][0] if k != 'ffn_conv_w' else W[k] for k in SMALL_SHARDED])
    gathered = {}
    for i in range(depth):
        mixer = MIXERS[i % len(MIXERS)]
        shards = {k: W[k][0] for k in BIG if k.startswith(mixer + '_')}
        shards['ffn_w_up'] = W['ffn_w_up'][i]
        shards['ffn_w_down'] = W['ffn_w_down'][i]
        if i == 0:
            shards['small'] = small_local
        gathered[i] = _gather_layer(shards, place_idx, name=f"gather{i}")
    small_shapes = [W[k][0].shape if k != 'ffn_conv_w' else W[k].shape for k in SMALL_SHARDED]
    per_chip = [_unpack(gathered[0]['small'][s], small_shapes) for s in range(N_CHIPS)]
    full_small = {k: jnp.concatenate([per_chip[s][j] for s in range(N_CHIPS)], axis=-1)
                  for j, k in enumerate(SMALL_SHARDED)}

    pos = positions.reshape(s_len).astype(F32)
    inv_freq = ROPE_THETA ** (-jnp.arange(0, MLA_ROPE, 2, dtype=F32) / MLA_ROPE)
    ang = pos[:, None] * inv_freq
    cos_t = _pad_lanes(jnp.concatenate([jnp.cos(ang), jnp.cos(ang)], axis=1))
    sin_t = _pad_lanes(jnp.concatenate([-jnp.sin(ang), jnp.sin(ang)], axis=1))

    def layer_params(i):
        mixer = MIXERS[i % len(MIXERS)]
        g = gathered[i]
        p = {'mix_norm': mix_norm[i:i + 1], 'ffn_norm': ffn_norm[i:i + 1], 'ffn_w_up': g['ffn_w_up'],
             'ffn_w_down': g['ffn_w_down'].reshape(-1, d_model), 'ffn_conv_w': full_small['ffn_conv_w'][i],
             'ffn_conv_b': ffn_conv_b[i:i + 1]}
        rows = lambda w: w.reshape(-1, w.shape[-1])
        if mixer == 'fox':
            w = _unshard_cols(g['fox_w_in'])
            p['w_in'] = jnp.pad(w, ((0, 0), (0, (3 * n_heads + 1) * HEAD_DIM - w.shape[1])))
            p['b_f'] = _pad_lanes(fox_b_f)
            p['qk_gain'] = jnp.stack([fox_q_gain, fox_k_gain])
            p['w_out'] = rows(g['fox_w_out'])
        elif mixer == 'sb':
            p['w_in'] = g['sb_w_in']
            p['qk_gain'] = jnp.stack([sb_q_gain, sb_k_gain])
            p['w_out'] = rows(g['sb_w_out'])
        elif mixer == 'sgu':
            p['w_in'] = g['sgu_w_in']
            p['v_gain'] = full_small['sgu_v_gain'].reshape(1, -1)
            p['w_s'] = sgu_w_s[0]
            p['b_s'] = sgu_b_s[0][:, :, None]
            p['w_out'] = rows(g['sgu_w_out'])
        else:
            w = rows(g['mla_w_in'])
            p['w_in'] = jnp.pad(w, ((0, 0), (0, MLA_Q_RANK + MLA_KV_RANK + LANES - w.shape[1])))
            p['a_gain'] = jnp.concatenate([full_small['mla_q_a_gain'], full_small['mla_kv_a_gain']]).reshape(1, -1)
            wq = _unshard_cols(g['mla_w_q_b']).reshape(MLA_Q_RANK, n_heads, MLA_NOPE + MLA_ROPE)
            p['w_q_b'] = jnp.concatenate([wq[:, :, :MLA_NOPE].reshape(MLA_Q_RANK, -1),
                                          _pad_lanes(wq[:, :, MLA_NOPE:]).reshape(MLA_Q_RANK, -1)], axis=1)
            p['w_kv_b'] = g['mla_w_kv_b']
            p['gq'] = jnp.stack([mla_q_gain[:, :MLA_NOPE], _pad_lanes(mla_q_gain[:, MLA_NOPE:])])
            p['gk'] = jnp.stack([mla_k_gain[:, :MLA_NOPE], _pad_lanes(mla_k_gain[:, MLA_NOPE:])])
            p['cos'], p['sin'] = cos_t, sin_t
            p['w_out'] = rows(g['mla_w_out'])
        return mixer, p

    h = x.reshape(s_len, d_model)
    saved = []
    for i in range(depth):
        mixer, p = layer_params(i)
        tag = f"l{i}_{mixer}"
        if mixer in ('fox', 'sb'):
            h, sm = _qkv_attn_fwd(mixer, h, p, tag, n_heads)
        elif mixer == 'mla':
            h, sm = _mla_fwd(h, p, tag, n_heads)
        else:
            h, sm = _sgu_fwd(h, p, tag)
        h, sf = _ffn_fwd(h, p, f"l{i}")
        saved.append((mixer, p, sm, sf))
    loss_row, dh = _loss(h, loss_target.reshape(s_len, d_model))
    loss = lax.psum(loss_row[0, 0], ("x", "y", "c"))

    big_grads, small_grads = {}, {k: [None] * depth for k in ('mix_norm', 'ffn_norm', 'ffn_conv_w', 'ffn_conv_b')}
    for i in reversed(range(depth)):
        mixer, p, sm, sf = saved[i]
        tag = f"l{i}_{mixer}"
        dh, big_f, small_f = _ffn_bwd(dh, sf, p, f"l{i}")
        if mixer in ('fox', 'sb'):
            dh, big_m, small_m = _qkv_attn_bwd(mixer, dh, sm, p, tag, n_heads)
        elif mixer == 'mla':
            dh, big_m, small_m = _mla_bwd(dh, sm, p, tag, n_heads)
        else:
            dh, big_m, small_m = _sgu_bwd(dh, sm, p, tag)
        layer_big = {f"{mixer}_{k}": v for k, v in big_m.items()}
        layer_big.update(big_f)
        reduced = _reduce_layer(layer_big, place_idx, name=f"reduce{i}")
        for k, v in reduced.items():
            if k.startswith('ffn_'):
                big_grads.setdefault(k, [None] * depth)[i] = v
            else:
                big_grads[k] = v[None]
        for k, v in {**small_m, **small_f}.items():
            if k in small_grads:
                small_grads[k][i] = v
            else:
                small_grads[f"{mixer}_{k}"] = v
    grad_x = dh.reshape(x.shape)
    for k in ('ffn_w_up', 'ffn_w_down'):
        big_grads[k] = jnp.stack(big_grads[k])
    for k in ('mix_norm', 'ffn_norm', 'ffn_conv_b'):
        small_grads[k] = jnp.concatenate(small_grads[k], axis=0)
    small_grads['ffn_conv_w'] = jnp.stack(small_grads['ffn_conv_w'])

    small_names = [k for k in WEIGHT_NAMES if k not in BIG]
    full_shapes = {k: (W[k].shape[:-1] + (W[k].shape[-1] * N_CHIPS,) if k in SMALL_SHARDED else W[k].shape)
                   for k in small_names}
    packed = _pack([small_grads[k].reshape(full_shapes[k]) for k in small_names])
    summed = _sum_devices(_broadcast_all(packed, name="small_bcast"), name="small_sum")
    small_full = dict(zip(small_names, _unpack(summed, [full_shapes[k] for k in small_names])))
    grads = dict(big_grads)
    for k in small_names:
        g = small_full[k]
        if k in SMALL_SHARDED:
            n = W[k].shape[-1]
            g = lax.dynamic_slice_in_dim(g, chip * n, n, axis=g.ndim - 1)
        grads[k] = g
    grads = {k: grads[k].reshape(W[k].shape) for k in WEIGHT_NAMES}

    delta, new_m, new_v = {}, {}, {}
    for k in WEIGHT_NAMES:
        delta[k], new_m[k], new_v[k] = _adamw(W[k], grads[k], M[k], V[k], name=f"adamw_{k}")
    return (loss, grad_x, *[grads[k] for k in WEIGHT_NAMES], *[delta[k] for k in WEIGHT_NAMES],
            *[new_m[k] for k in WEIGHT_NAMES], *[new_v[k] for k in WEIGHT_NAMES])
```

```python
import functools

import jax
import jax.numpy as jnp
from jax import lax
from jax.experimental import pallas as pl
from jax.experimental.pallas import tpu as pltpu

F32 = jnp.float32
BF16 = jnp.bfloat16
LANES = 128
HEAD_DIM = 128
NORM_EPS = 1e-6
MLA_Q_RANK = 512
MLA_KV_RANK = 512
MLA_NOPE = 128
MLA_ROPE = 64
ROPE_THETA = 10000.0
SGU_CHUNK = 128
N_CHIPS = 4
ADAM_LR, ADAM_B1, ADAM_B2, ADAM_EPS, ADAM_WD, ADAM_STEP = 0.001, 0.9, 0.999, 1e-08, 0.01, 10
VMEM_LIMIT_BYTES = 56 * 1024 * 1024
MESH = pl.DeviceIdType.MESH
NEG_BIG = -1e30


def _params(*sem):
    return pltpu.CompilerParams(dimension_semantics=sem, vmem_limit_bytes=VMEM_LIMIT_BYTES)


def _div_block(n, target, mult=LANES):
    if n <= target:
        return n
    best = None
    for b in range(mult, target + 1, mult):
        if n % b == 0:
            best = b
    assert best is not None, (n, target, mult)
    return best


def _iota(shape, dim):
    return lax.broadcasted_iota(jnp.int32, shape, dim)


def _dot(a, b, ca, cb):
    return lax.dot_general(a, b, (((ca,), (cb,)), ((), ())), preferred_element_type=F32)


def _mm(a, b, *, name, ta=False, tb=False, b_sh=None, o_sh=False, res=None, out_dtype=F32,
        bm=1024, bn=1024, bk=512):
    m, k = (a.shape[1], a.shape[0]) if ta else a.shape
    if b_sh == 'n':
        n = b.shape[2] * N_CHIPS
        assert b.shape[1] == k and not tb
    elif b_sh == 'k':
        n = b.shape[1]
        assert b.shape[2] * N_CHIPS == k
    else:
        n = b.shape[0] if tb else b.shape[1]
        assert (b.shape[1] if tb else b.shape[0]) == k
    n_sh = n // N_CHIPS
    k_sh = k // N_CHIPS
    bm = _div_block(m, bm, 8 if not ta else LANES)
    bn = _div_block(n_sh if (b_sh == 'n' or o_sh) else n, bn)
    bk = _div_block(k_sh if b_sh == 'k' else k, bk)
    nbs = n_sh // bn if (b_sh == 'n' or o_sh) else 1
    nks = k_sh // bk if b_sh == 'k' else 1
    nk = k // bk

    a_spec = pl.BlockSpec((bk, bm), lambda i, j, q: (q, i)) if ta else pl.BlockSpec((bm, bk), lambda i, j, q: (i, q))
    if b_sh == 'n':
        b_spec = pl.BlockSpec((None, bk, bn), lambda i, j, q: (j // nbs, q, j % nbs))
    elif b_sh == 'k':
        b_spec = pl.BlockSpec((None, bn, bk), lambda i, j, q: (q // nks, j, q % nks))
    elif tb:
        b_spec = pl.BlockSpec((bn, bk), lambda i, j, q: (j, q))
    else:
        b_spec = pl.BlockSpec((bk, bn), lambda i, j, q: (q, j))
    if o_sh:
        o_spec = pl.BlockSpec((None, bm, bn), lambda i, j, q: (j // nbs, i, j % nbs))
        o_shape = jax.ShapeDtypeStruct((N_CHIPS, m, n_sh), out_dtype)
    else:
        o_spec = pl.BlockSpec((bm, bn), lambda i, j, q: (i, j))
        o_shape = jax.ShapeDtypeStruct((m, n), out_dtype)
    tb_eff = tb or b_sh == 'k'

    def body(a_ref, b_ref, *rest):
        if res is not None:
            r_ref, o_ref, acc = rest
        else:
            o_ref, acc = rest
        q = pl.program_id(2)

        @pl.when(q == 0)
        def _():
            acc[...] = jnp.zeros_like(acc)

        acc[...] += _dot(a_ref[...].astype(BF16), b_ref[...].astype(BF16), 0 if ta else 1, 1 if tb_eff else 0)

        @pl.when(q == nk - 1)
        def _():
            r = acc[...]
            if res is not None:
                r = r + r_ref[...].astype(F32)
            o_ref[...] = r.astype(out_dtype)

    ins = [a, b]
    in_specs = [a_spec, b_spec]
    if res is not None:
        assert not o_sh
        ins.append(res)
        in_specs.append(pl.BlockSpec((bm, bn), lambda i, j, q: (i, j)))
    return pl.pallas_call(
        body, name=name, grid=(m // bm, n // bn, nk), in_specs=in_specs, out_specs=o_spec, out_shape=o_shape,
        scratch_shapes=[pltpu.VMEM((bm, bn), F32)],
        compiler_params=_params("parallel", "parallel", "arbitrary"))(*ins)


def _rms_fwd(x, g, *, name, out_dtype=BF16, br=256):
    r, c = x.shape
    br = _div_block(r, br, 8)

    def body(x_ref, g_ref, o_ref):
        xv = x_ref[...].astype(F32)
        inv = lax.rsqrt(jnp.mean(xv * xv, axis=-1, keepdims=True) + NORM_EPS)
        o_ref[...] = (xv * inv * g_ref[...]).astype(out_dtype)

    return pl.pallas_call(
        body, name=name, grid=(r // br,),
        in_specs=[pl.BlockSpec((br, c), lambda i: (i, 0)), pl.BlockSpec((1, c), lambda i: (0, 0))],
        out_specs=pl.BlockSpec((br, c), lambda i: (i, 0)), out_shape=jax.ShapeDtypeStruct((r, c), out_dtype),
        compiler_params=_params("parallel"))(x, g)


def _rms_bwd_math(xv, gv, dyv, n):
    inv = lax.rsqrt(jnp.sum(xv * xv, axis=-1, keepdims=True) / n + NORM_EPS)
    xh = xv * inv
    dyg = dyv * gv
    dx = inv * (dyg - xh * (jnp.sum(dyg * xh, axis=-1, keepdims=True) / n))
    return dx, dyv * xh


def _rms_bwd(x, g, dy, *, name, res=None, br=256):
    r, c = x.shape
    br = _div_block(r, br, 8)

    def body(x_ref, g_ref, dy_ref, *rest):
        if res is not None:
            r_ref, dx_ref, dg_ref = rest
        else:
            dx_ref, dg_ref = rest
        dx, dgr = _rms_bwd_math(x_ref[...].astype(F32), g_ref[...], dy_ref[...].astype(F32), c)
        if res is not None:
            dx = dx + r_ref[...]
        dx_ref[...] = dx

        @pl.when(pl.program_id(0) == 0)
        def _():
            dg_ref[...] = jnp.zeros_like(dg_ref)

        dg_ref[...] += jnp.sum(dgr, axis=0, keepdims=True)

    row = pl.BlockSpec((br, c), lambda i: (i, 0))
    vec = pl.BlockSpec((1, c), lambda i: (0, 0))
    ins = [x, g, dy] + ([res] if res is not None else [])
    return pl.pallas_call(
        body, name=name, grid=(r // br,), in_specs=[row, vec, row] + ([row] if res is not None else []),
        out_specs=[row, vec], out_shape=[jax.ShapeDtypeStruct((r, c), F32), jax.ShapeDtypeStruct((1, c), F32)],
        compiler_params=_params("arbitrary"))(*ins)


def _loss(y, target, *, name="loss", br=256):
    r, c = y.shape
    br = _div_block(r, br, 8)

    def body(y_ref, t_ref, l_ref, dy_ref):
        d = y_ref[...] - t_ref[...]
        dy_ref[...] = d * (1.0 / c)

        @pl.when(pl.program_id(0) == 0)
        def _():
            l_ref[...] = jnp.zeros_like(l_ref)

        part = jnp.sum(d * d, axis=0, keepdims=True)
        l_ref[...] += (0.5 / c) * jnp.sum(part, axis=1, keepdims=True) * jnp.ones((1, LANES), F32)

    row = pl.BlockSpec((br, c), lambda i: (i, 0))
    return pl.pallas_call(
        body, name=name, grid=(r // br,), in_specs=[row, row],
        out_specs=[pl.BlockSpec((1, LANES), lambda i: (0, 0)), row],
        out_shape=[jax.ShapeDtypeStruct((1, LANES), F32), jax.ShapeDtypeStruct((r, c), F32)],
        compiler_params=_params("arbitrary"))(y, target)


def _split2(x):
    hi = x.astype(BF16)
    lo = (x - hi.astype(F32)).astype(BF16)
    return hi, lo


def _lane_scan(x, *, suffix):
    rows, n = x.shape
    nb = n // LANES
    a, b = _iota((LANES, LANES), 0), _iota((LANES, LANES), 1)
    tri = ((a > b) if suffix else (a < b)).astype(BF16)
    outs = [None] * nb
    run = jnp.zeros((rows, 1), F32)
    order = range(nb - 1, -1, -1) if suffix else range(nb)
    for blk in order:
        xb = x[:, blk * LANES:(blk + 1) * LANES]
        hi, lo = _split2(xb)
        outs[blk] = _dot(hi, tri, 1, 0) + _dot(lo, tri, 1, 0) + run
        run = run + jnp.sum(xb, axis=-1, keepdims=True)
    return jnp.concatenate(outs, axis=1)


def _softplus(z):
    return jnp.maximum(z, 0.0) + jnp.log(1.0 + jnp.exp(-jnp.abs(z)))


def _head_norm(x, g):
    xv = x.astype(F32)
    inv = lax.rsqrt(jnp.mean(xv * xv, axis=-1, keepdims=True) + NORM_EPS)
    return xv * inv * g


def _attn_weights(kind, qn, kn, scale, qi, bq, bias):
    s = _dot(qn, kn, 1, 1) * scale
    row = qi * bq + _iota(s.shape, 0)
    col = _iota(s.shape, 1)
    if kind == 'sb':
        strict = col < row
        sp = _softplus(s)
        after = _lane_scan(jnp.where(strict, -sp, 0.0), suffix=True)
        w = jnp.where(strict, jnp.exp(s - sp + after), 0.0)
        return w, (strict, s - sp)
    if bias is not None:
        s = s + bias
    s = jnp.where(col <= row, s, NEG_BIG)
    mx = jnp.max(s, axis=-1, keepdims=True)
    e = jnp.exp(s - mx)
    return e, jnp.sum(e, axis=-1, keepdims=True)


def _attn_fwd(kind, q, k, v, *, name, n_heads, dqk, qcol, kcol, vcol, scale, gains=None, cq=None, ck=None, bq=256):
    s_len = q.shape[0]
    bq = _div_block(s_len, bq, 8)
    norm, fox = gains is not None, cq is not None

    def body(*refs):
        refs = list(refs)
        q_ref, k_ref, v_ref = refs[:3]
        rest = refs[3:]
        g_ref = rest.pop(0) if norm else None
        cq_ref, ck_ref = (rest.pop(0), rest.pop(0)) if fox else (None, None)
        o_ref, = rest
        qi = pl.program_id(1)
        if norm:
            qn = _head_norm(q_ref[...], g_ref[0]).astype(BF16)
            kn = _head_norm(k_ref[...], g_ref[1]).astype(BF16)
        else:
            qn, kn = q_ref[...].astype(BF16), k_ref[...].astype(BF16)
        bias = (cq_ref[...] - ck_ref[...]) if fox else None
        w, aux = _attn_weights(kind, qn, kn, scale, qi, bq, bias)
        o = _dot(w.astype(BF16), v_ref[...].astype(BF16), 1, 0)
        if kind != 'sb':
            o = o / aux
        o_ref[...] = o.astype(BF16)

    in_specs = [pl.BlockSpec((bq, dqk), lambda h, i: (i, qcol(h))),
                pl.BlockSpec((s_len, dqk), lambda h, i: (0, kcol(h))),
                pl.BlockSpec((s_len, HEAD_DIM), lambda h, i: (0, vcol(h)))]
    ins = [q, k, v]
    if norm:
        in_specs.append(pl.BlockSpec((2, 1, dqk), lambda h, i: (0, 0, 0)))
        ins.append(gains)
    if fox:
        in_specs += [pl.BlockSpec((None, bq, 1), lambda h, i: (h, i, 0)), pl.BlockSpec((None, 1, s_len), lambda h, i: (h, 0, 0))]
        ins += [cq, ck]
    return pl.pallas_call(
        body, name=name, grid=(n_heads, s_len // bq), in_specs=in_specs,
        out_specs=pl.BlockSpec((bq, HEAD_DIM), lambda h, i: (i, h)),
        out_shape=jax.ShapeDtypeStruct((s_len, n_heads * HEAD_DIM), BF16),
        compiler_params=_params("parallel", "parallel"))(*ins)


def _attn_bwd(kind, q, k, v, o, do, *, name, n_heads, dqk, qcol, kcol, vcol, scale, gains=None, cq=None, ck=None,
              bq=256):
    s_len = q.shape[0]
    bq = _div_block(s_len, bq, 8)
    nq = s_len // bq
    norm, fox = gains is not None, cq is not None

    def body(*refs):
        refs = list(refs)
        q_ref, k_ref, v_ref, o_ref, do_ref = refs[:5]
        rest = refs[5:]
        g_ref = rest.pop(0) if norm else None
        cq_ref, ck_ref = (rest.pop(0), rest.pop(0)) if fox else (None, None)
        dq_ref, dk_ref, dv_ref = rest.pop(0), rest.pop(0), rest.pop(0)
        dg_ref = rest.pop(0) if norm else None
        dcq_ref, dck_ref = (rest.pop(0), rest.pop(0)) if fox else (None, None)
        dk_acc, dv_acc = rest
        h, qi = pl.program_id(0), pl.program_id(1)

        @pl.when(qi == 0)
        def _():
            dk_acc[...] = jnp.zeros_like(dk_acc)
            dv_acc[...] = jnp.zeros_like(dv_acc)
            if fox:
                dck_ref[...] = jnp.zeros_like(dck_ref)

        if norm:
            @pl.when((qi == 0) & (h == 0))
            def _():
                dg_ref[...] = jnp.zeros_like(dg_ref)

            qn = _head_norm(q_ref[...], g_ref[0]).astype(BF16)
            kn = _head_norm(k_ref[...], g_ref[1]).astype(BF16)
        else:
            qn, kn = q_ref[...].astype(BF16), k_ref[...].astype(BF16)
        vb = v_ref[...].astype(BF16)
        dob = do_ref[...].astype(BF16)
        bias = (cq_ref[...] - ck_ref[...]) if fox else None
        w, aux = _attn_weights(kind, qn, kn, scale, qi, bq, bias)
        dw = _dot(dob, vb, 1, 1)
        if kind == 'sb':
            strict, log_sig = aux
            g = dw * w
            cc = _lane_scan(g, suffix=False)
            sig = jnp.exp(log_sig)
            ds = jnp.where(strict, g * (1.0 - sig) - cc * sig, 0.0)
            pw = w
        else:
            pw = w / aux
            delta = jnp.sum(do_ref[...].astype(F32) * o_ref[...].astype(F32), axis=-1, keepdims=True)
            ds = pw * (dw - delta)
            if fox:
                dcq_ref[...] = jnp.sum(ds, axis=1, keepdims=True)
                dck_ref[...] -= jnp.sum(ds, axis=0, keepdims=True)
        dsb = (ds * scale).astype(BF16)
        dqn = _dot(dsb, kn, 1, 0)
        dk_acc[...] += _dot(dsb, qn, 0, 0)
        dv_acc[...] += _dot(pw.astype(BF16), dob, 0, 0)
        if norm:
            dq, dgr = _rms_bwd_math(q_ref[...].astype(F32), g_ref[0], dqn, dqk)
            dg_ref[0] += jnp.sum(dgr, axis=0, keepdims=True)
            dq_ref[...] = dq.astype(BF16)
        else:
            dq_ref[...] = dqn.astype(BF16)

        @pl.when(qi == nq - 1)
        def _():
            if norm:
                dk, dgr = _rms_bwd_math(k_ref[...].astype(F32), g_ref[1], dk_acc[...], dqk)
                dg_ref[1] += jnp.sum(dgr, axis=0, keepdims=True)
                dk_ref[...] = dk.astype(BF16)
            else:
                dk_ref[...] = dk_acc[...].astype(BF16)
            dv_ref[...] = dv_acc[...].astype(BF16)

    in_specs = [pl.BlockSpec((bq, dqk), lambda h, i: (i, qcol(h))),
                pl.BlockSpec((s_len, dqk), lambda h, i: (0, kcol(h))),
                pl.BlockSpec((s_len, HEAD_DIM), lambda h, i: (0, vcol(h))),
                pl.BlockSpec((bq, HEAD_DIM), lambda h, i: (i, h)),
                pl.BlockSpec((bq, HEAD_DIM), lambda h, i: (i, h))]
    ins = [q, k, v, o, do]
    out_specs = [pl.BlockSpec((bq, dqk), lambda h, i: (i, h)),
                 pl.BlockSpec((s_len, dqk), lambda h, i: (0, h)),
                 pl.BlockSpec((s_len, HEAD_DIM), lambda h, i: (0, h))]
    out_shape = [jax.ShapeDtypeStruct((s_len, n_heads * dqk), BF16), jax.ShapeDtypeStruct((s_len, n_heads * dqk), BF16),
                 jax.ShapeDtypeStruct((s_len, n_heads * HEAD_DIM), BF16)]
    if norm:
        in_specs.append(pl.BlockSpec((2, 1, dqk), lambda h, i: (0, 0, 0)))
        ins.append(gains)
        out_specs.append(pl.BlockSpec((2, 1, dqk), lambda h, i: (0, 0, 0)))
        out_shape.append(jax.ShapeDtypeStruct((2, 1, dqk), F32))
    if fox:
        in_specs += [pl.BlockSpec((None, bq, 1), lambda h, i: (h, i, 0)), pl.BlockSpec((None, 1, s_len), lambda h, i: (h, 0, 0))]
        ins += [cq, ck]
        out_specs += [pl.BlockSpec((None, bq, 1), lambda h, i: (h, i, 0)), pl.BlockSpec((None, 1, s_len), lambda h, i: (h, 0, 0))]
        out_shape += [jax.ShapeDtypeStruct((n_heads, s_len, 1), F32), jax.ShapeDtypeStruct((n_heads, 1, s_len), F32)]
    return pl.pallas_call(
        body, name=name, grid=(n_heads, nq), in_specs=in_specs, out_specs=out_specs, out_shape=out_shape,
        scratch_shapes=[pltpu.VMEM((s_len, dqk), F32), pltpu.VMEM((s_len, HEAD_DIM), F32)],
        compiler_params=_params("arbitrary", "arbitrary"))(*ins)


def _split3(x):
    hi = x.astype(BF16)
    r1 = x - hi.astype(F32)
    mid = r1.astype(BF16)
    lo = (r1 - mid.astype(F32)).astype(BF16)
    return hi, mid, lo


def _seq_scan(x, *, reverse):
    n = x.shape[0] // LANES
    a, b = _iota((LANES, LANES), 0), _iota((LANES, LANES), 1)
    tri = ((b >= a) if reverse else (b <= a)).astype(BF16)
    outs = [None] * n
    run = jnp.zeros((1, x.shape[1]), F32)
    for blk in (range(n - 1, -1, -1) if reverse else range(n)):
        xb = x[blk * LANES:(blk + 1) * LANES, :]
        hi, mid, lo = _split3(xb)
        outs[blk] = _dot(tri, hi, 1, 0) + _dot(tri, mid, 1, 0) + _dot(tri, lo, 1, 0) + run
        run = run + jnp.sum(xb, axis=0, keepdims=True)
    return jnp.concatenate(outs, axis=0)


def _fgate_fwd(qkvf, b_f, *, fcol, name):
    s_len = qkvf.shape[0]

    def body(f_ref, b_ref, cum_ref):
        z = f_ref[...] + b_ref[...]
        cum_ref[...] = _seq_scan(-_softplus(-z), reverse=False)

    return pl.pallas_call(
        body, name=name, grid=(1,),
        in_specs=[pl.BlockSpec((s_len, LANES), lambda i: (0, fcol)), pl.BlockSpec((1, LANES), lambda i: (0, 0))],
        out_specs=pl.BlockSpec((s_len, LANES), lambda i: (0, 0)), out_shape=jax.ShapeDtypeStruct((s_len, LANES), F32),
        compiler_params=_params("arbitrary"))(qkvf, b_f)


def _fgate_bwd(qkvf, b_f, dcum_a, dcum_b, *, fcol, n_heads, name):
    s_len = qkvf.shape[0]

    def body(f_ref, b_ref, da_ref, db_ref, dz_ref, dbias_ref):
        z = f_ref[...] + b_ref[...]
        dlog = _seq_scan(da_ref[...] + db_ref[...], reverse=True)
        dz = dlog * jnp.exp(-_softplus(z))
        dz = jnp.where(_iota(dz.shape, 1) < n_heads, dz, 0.0)
        dz_ref[...] = dz.astype(BF16)
        dbias_ref[...] = jnp.sum(dz, axis=0, keepdims=True)

    full = pl.BlockSpec((s_len, LANES), lambda i: (0, 0))
    vec = pl.BlockSpec((1, LANES), lambda i: (0, 0))
    return pl.pallas_call(
        body, name=name, grid=(1,),
        in_specs=[pl.BlockSpec((s_len, LANES), lambda i: (0, fcol)), vec, full, full],
        out_specs=[full, vec], out_shape=[jax.ShapeDtypeStruct((s_len, LANES), BF16), jax.ShapeDtypeStruct((1, LANES), F32)],
        compiler_params=_params("arbitrary"))(qkvf, b_f, dcum_a, dcum_b)


def _rope_swap(x):
    half = MLA_ROPE // 2
    lane = _iota(x.shape, 1)
    sw = jnp.where(lane < half, pltpu.roll(x, LANES - half, axis=1), pltpu.roll(x, half, axis=1))
    return jnp.where(lane < MLA_ROPE, sw, 0.0)


def _mla_prep_fwd(qp, kv, c, gq, gk, cos_t, sin_t, *, n_heads, name, bs=512):
    s_len = qp.shape[0]
    bs = _div_block(s_len, bs, 8)
    krope_col = (MLA_Q_RANK + MLA_KV_RANK) // LANES

    def body(qn_ref, qr_ref, kn_ref, kr_ref, gq_ref, gk_ref, cos_ref, sin_ref, qc_ref, kc_ref):
        cos_v, sin_v = cos_ref[...], sin_ref[...]

        def rope(x, g):
            xv = x.astype(F32)
            inv = lax.rsqrt(jnp.sum(xv * xv, axis=-1, keepdims=True) / MLA_ROPE + NORM_EPS)
            y = xv * inv * g
            return y * cos_v + _rope_swap(y) * sin_v

        qc_ref[:, :LANES] = _head_norm(qn_ref[...], gq_ref[0]).astype(BF16)
        qc_ref[:, LANES:] = rope(qr_ref[...], gq_ref[1]).astype(BF16)
        kc_ref[:, :LANES] = _head_norm(kn_ref[...], gk_ref[0]).astype(BF16)
        kc_ref[:, LANES:] = rope(kr_ref[...], gk_ref[1]).astype(BF16)

    blk = lambda f: pl.BlockSpec((bs, LANES), f)
    gspec = pl.BlockSpec((2, 1, LANES), lambda i, h: (0, 0, 0))
    tspec = pl.BlockSpec((bs, LANES), lambda i, h: (i, 0))
    ospec = pl.BlockSpec((bs, 2 * LANES), lambda i, h: (i, h))
    oshape = jax.ShapeDtypeStruct((s_len, n_heads * 2 * LANES), BF16)
    return pl.pallas_call(
        body, name=name, grid=(s_len // bs, n_heads),
        in_specs=[blk(lambda i, h: (i, h)), blk(lambda i, h: (i, n_heads + h)), blk(lambda i, h: (i, 2 * h)),
                  blk(lambda i, h: (i, krope_col)), gspec, gspec, tspec, tspec],
        out_specs=[ospec, ospec], out_shape=[oshape, oshape],
        compiler_params=_params("parallel", "parallel"))(qp, qp, kv, c, gq, gk, cos_t, sin_t)


def _mla_prep_bwd(qp, kv, c, gq, gk, cos_t, sin_t, dqc, dkc, dv, *, n_heads, name, bs=512):
    s_len = qp.shape[0]
    bs = _div_block(s_len, bs, 8)
    krope_col = (MLA_Q_RANK + MLA_KV_RANK) // LANES

    def body(qn_ref, qr_ref, kn_ref, kr_ref, gq_ref, gk_ref, cos_ref, sin_ref, dqc_ref, dkc_ref, dv_ref,
             dqn_ref, dqr_ref, dkv_ref, dkr_ref, dgq_ref, dgk_ref):
        i, h = pl.program_id(0), pl.program_id(1)
        cos_v, sin_v = cos_ref[...], sin_ref[...]

        @pl.when((i == 0) & (h == 0))
        def _():
            dgq_ref[...] = jnp.zeros_like(dgq_ref)
            dgk_ref[...] = jnp.zeros_like(dgk_ref)

        @pl.when(h == 0)
        def _():
            dkr_ref[...] = jnp.zeros_like(dkr_ref)

        def unrope(dy):
            dy = dy.astype(F32)
            return dy * cos_v + _rope_swap(dy * sin_v)

        dqn, dg = _rms_bwd_math(qn_ref[...].astype(F32), gq_ref[0], dqc_ref[:, :LANES].astype(F32), MLA_NOPE)
        dgq_ref[0] += jnp.sum(dg, axis=0, keepdims=True)
        dqn_ref[...] = dqn.astype(BF16)
        dqr, dg = _rms_bwd_math(qr_ref[...].astype(F32), gq_ref[1], unrope(dqc_ref[:, LANES:]), MLA_ROPE)
        dgq_ref[1] += jnp.sum(dg, axis=0, keepdims=True)
        dqr_ref[...] = dqr.astype(BF16)
        dkn, dg = _rms_bwd_math(kn_ref[...].astype(F32), gk_ref[0], dkc_ref[:, :LANES].astype(F32), MLA_NOPE)
        dgk_ref[0] += jnp.sum(dg, axis=0, keepdims=True)
        dkv_ref[:, :LANES] = dkn.astype(BF16)
        dkv_ref[:, LANES:] = dv_ref[...]
        dkr, dg = _rms_bwd_math(kr_ref[...].astype(F32), gk_ref[1], unrope(dkc_ref[:, LANES:]), MLA_ROPE)
        dgk_ref[1] += jnp.sum(dg, axis=0, keepdims=True)
        dkr_ref[...] += dkr

    blk = lambda f: pl.BlockSpec((bs, LANES), f)
    gspec = pl.BlockSpec((2, 1, LANES), lambda i, h: (0, 0, 0))
    tspec = pl.BlockSpec((bs, LANES), lambda i, h: (i, 0))
    cat = pl.BlockSpec((bs, 2 * LANES), lambda i, h: (i, h))
    head = blk(lambda i, h: (i, h))
    hshape = jax.ShapeDtypeStruct((s_len, n_heads * LANES), BF16)
    gshape = jax.ShapeDtypeStruct((2, 1, LANES), F32)
    return pl.pallas_call(
        body, name=name, grid=(s_len // bs, n_heads),
        in_specs=[head, blk(lambda i, h: (i, n_heads + h)), blk(lambda i, h: (i, 2 * h)),
                  blk(lambda i, h: (i, krope_col)), gspec, gspec, tspec, tspec, cat, cat, head],
        out_specs=[head, head, cat, tspec, gspec, gspec],
        out_shape=[hshape, hshape, jax.ShapeDtypeStruct((s_len, n_heads * 2 * LANES), BF16),
                   jax.ShapeDtypeStruct((s_len, LANES), F32), gshape, gshape],
        compiler_params=_params("arbitrary", "arbitrary"))(qp, qp, kv, c, gq, gk, cos_t, sin_t, dqc, dkc, dv)


def _mla_latent_fwd(c, ga, *, name, br=256):
    s_len = c.shape[0]
    br = _div_block(s_len, br, 8)

    def body(c_ref, g_ref, o_ref):
        for part in range(2):
            sl = slice(part * MLA_Q_RANK, (part + 1) * MLA_Q_RANK)
            o_ref[:, sl] = _head_norm(c_ref[:, sl], g_ref[:, sl]).astype(BF16)

    w = MLA_Q_RANK + MLA_KV_RANK
    return pl.pallas_call(
        body, name=name, grid=(s_len // br,),
        in_specs=[pl.BlockSpec((br, w), lambda i: (i, 0)), pl.BlockSpec((1, w), lambda i: (0, 0))],
        out_specs=pl.BlockSpec((br, w), lambda i: (i, 0)), out_shape=jax.ShapeDtypeStruct((s_len, w), BF16),
        compiler_params=_params("parallel"))(c, ga)


def _mla_latent_bwd(c, ga, dcn_q, dcn_kv, dk_rope, *, name, br=256):
    s_len, cw = c.shape
    br = _div_block(s_len, br, 8)
    w = MLA_Q_RANK + MLA_KV_RANK

    def body(c_ref, g_ref, dq_ref, dkv_ref, dkr_ref, dc_ref, dg_ref):
        @pl.when(pl.program_id(0) == 0)
        def _():
            dg_ref[...] = jnp.zeros_like(dg_ref)

        for part, d_ref in enumerate((dq_ref, dkv_ref)):
            sl = slice(part * MLA_Q_RANK, (part + 1) * MLA_Q_RANK)
            dx, dg = _rms_bwd_math(c_ref[:, sl].astype(F32), g_ref[:, sl], d_ref[...].astype(F32), MLA_Q_RANK)
            dc_ref[:, sl] = dx.astype(BF16)
            dg_ref[:, sl] += jnp.sum(dg, axis=0, keepdims=True)
        dc_ref[:, w:] = dkr_ref[...].astype(BF16)

    return pl.pallas_call(
        body, name=name, grid=(s_len // br,),
        in_specs=[pl.BlockSpec((br, w), lambda i: (i, 0)), pl.BlockSpec((1, w), lambda i: (0, 0)),
                  pl.BlockSpec((br, MLA_Q_RANK), lambda i: (i, 0)), pl.BlockSpec((br, MLA_KV_RANK), lambda i: (i, 0)),
                  pl.BlockSpec((br, LANES), lambda i: (i, 0))],
        out_specs=[pl.BlockSpec((br, cw), lambda i: (i, 0)), pl.BlockSpec((1, w), lambda i: (0, 0))],
        out_shape=[jax.ShapeDtypeStruct((s_len, cw), BF16), jax.ShapeDtypeStruct((1, w), F32)],
        compiler_params=_params("arbitrary"))(c, ga, dcn_q, dcn_kv, dk_rope)


_GELU_C = 0.7978845608028654


def _gelu(x):
    return 0.5 * x * (1.0 + jnp.tanh(_GELU_C * (x + 0.044715 * x * x * x)))


def _gelu_grad(x):
    t = jnp.tanh(_GELU_C * (x + 0.044715 * x * x * x))
    return 0.5 * (1.0 + t) + 0.5 * x * (1.0 - t * t) * _GELU_C * (1.0 + 3 * 0.044715 * x * x)


def _sgu_act_fwd(uv, vg, *, name, br=256):
    s_len, w2 = uv.shape
    w = w2 // 2
    br = _div_block(s_len, br, 8)

    def body(uv_ref, g_ref, u_ref, v_ref):
        u_ref[...] = _gelu(uv_ref[:, :w])
        v_ref[...] = _head_norm(_gelu(uv_ref[:, w:]), g_ref[...]).astype(BF16)

    row = lambda c: pl.BlockSpec((br, c), lambda i: (i, 0))
    return pl.pallas_call(
        body, name=name, grid=(s_len // br,), in_specs=[row(w2), pl.BlockSpec((1, w), lambda i: (0, 0))],
        out_specs=[row(w), row(w)], out_shape=[jax.ShapeDtypeStruct((s_len, w), F32), jax.ShapeDtypeStruct((s_len, w), BF16)],
        compiler_params=_params("parallel"))(uv, vg)


def _sgu_act_bwd(uv, vg, du, dvn, *, name, br=256):
    s_len, w2 = uv.shape
    w = w2 // 2
    br = _div_block(s_len, br, 8)

    def body(uv_ref, g_ref, du_ref, dvn_ref, duv_ref, dg_ref):
        @pl.when(pl.program_id(0) == 0)
        def _():
            dg_ref[...] = jnp.zeros_like(dg_ref)

        up, vp = uv_ref[:, :w], uv_ref[:, w:]
        duv_ref[:, :w] = (du_ref[...] * _gelu_grad(up)).astype(BF16)
        dva, dg = _rms_bwd_math(_gelu(vp), g_ref[...], dvn_ref[...], w)
        dg_ref[...] += jnp.sum(dg, axis=0, keepdims=True)
        duv_ref[:, w:] = (dva * _gelu_grad(vp)).astype(BF16)

    row = lambda c: pl.BlockSpec((br, c), lambda i: (i, 0))
    vec = pl.BlockSpec((1, w), lambda i: (0, 0))
    return pl.pallas_call(
        body, name=name, grid=(s_len // br,), in_specs=[row(w2), vec, row(w), row(w)], out_specs=[row(w2), vec],
        out_shape=[jax.ShapeDtypeStruct((s_len, w2), BF16), jax.ShapeDtypeStruct((1, w), F32)],
        compiler_params=_params("arbitrary"))(uv, vg, du, dvn)


def _tril_weights(ws_ref):
    t, s = _iota((SGU_CHUNK, SGU_CHUNK), 0), _iota((SGU_CHUNK, SGU_CHUNK), 1)
    keep = s <= t
    return jnp.where(keep, ws_ref[...], 0.0), keep


def _sgu_mix_fwd(u, vn, w_s, b_s, *, name):
    s_len, w = u.shape
    nc = s_len // SGU_CHUNK

    def body(u_ref, v_ref, ws_ref, b_ref, o_ref):
        wm = _tril_weights(ws_ref)[0].astype(BF16)
        for n in range(nc):
            rows = slice(n * SGU_CHUNK, (n + 1) * SGU_CHUNK)
            mixed = _dot(wm, v_ref[rows, :], 1, 0) + b_ref[...]
            o_ref[rows, :] = (u_ref[rows, :] * mixed).astype(BF16)

    col = pl.BlockSpec((s_len, LANES), lambda g: (0, g))
    return pl.pallas_call(
        body, name=name, grid=(w // LANES,),
        in_specs=[col, col, pl.BlockSpec((None, SGU_CHUNK, SGU_CHUNK), lambda g: (g, 0, 0)),
                  pl.BlockSpec((None, SGU_CHUNK, 1), lambda g: (g, 0, 0))],
        out_specs=col, out_shape=jax.ShapeDtypeStruct((s_len, w), BF16),
        compiler_params=_params("parallel"))(u, vn, w_s, b_s)


def _sgu_mix_bwd(u, vn, w_s, b_s, dgated, *, name):
    s_len, w = u.shape
    nc = s_len // SGU_CHUNK

    def body(u_ref, v_ref, ws_ref, b_ref, dg_ref, du_ref, dv_ref, dws_ref, dbs_ref):
        wf, keep = _tril_weights(ws_ref)
        wm = wf.astype(BF16)
        wmt = wf.T.astype(BF16)
        dws = jnp.zeros((SGU_CHUNK, SGU_CHUNK), F32)
        dbs = jnp.zeros((SGU_CHUNK, 1), F32)
        for n in range(nc):
            rows = slice(n * SGU_CHUNK, (n + 1) * SGU_CHUNK)
            vb = v_ref[rows, :]
            dgv = dg_ref[rows, :].astype(F32)
            mixed = _dot(wm, vb, 1, 0) + b_ref[...]
            du_ref[rows, :] = dgv * mixed
            dm = dgv * u_ref[rows, :]
            dmb = dm.astype(BF16)
            dws = dws + _dot(dmb, vb, 1, 1)
            dbs = dbs + jnp.sum(dm, axis=1, keepdims=True)
            dv_ref[rows, :] = _dot(wmt, dmb, 1, 0)
        dws_ref[...] = jnp.where(keep, dws, 0.0)
        dbs_ref[...] = dbs

    col = pl.BlockSpec((s_len, LANES), lambda g: (0, g))
    wspec = pl.BlockSpec((None, SGU_CHUNK, SGU_CHUNK), lambda g: (g, 0, 0))
    bspec = pl.BlockSpec((None, SGU_CHUNK, 1), lambda g: (g, 0, 0))
    return pl.pallas_call(
        body, name=name, grid=(w // LANES,), in_specs=[col, col, wspec, bspec, col],
        out_specs=[col, col, wspec, bspec],
        out_shape=[jax.ShapeDtypeStruct((s_len, w), F32), jax.ShapeDtypeStruct((s_len, w), F32),
                   jax.ShapeDtypeStruct(w_s.shape, F32), jax.ShapeDtypeStruct(b_s.shape, F32)],
        compiler_params=_params("parallel"))(u, vn, w_s, b_s, dgated)


def _shift_down(x, k):
    if k == 0:
        return x
    return jnp.where(_iota(x.shape, 0) >= k, pltpu.roll(x, k, axis=0), 0.0)


def _shift_up(x, k):
    if k == 0:
        return x
    n = x.shape[0]
    return jnp.where(_iota(x.shape, 0) < n - k, pltpu.roll(x, n - k, axis=0), 0.0)


def _conv(u, w_ref, b_ref):
    return b_ref[...] + w_ref[0:1, :] * _shift_down(u, 2) + w_ref[1:2, :] * _shift_down(u, 1) + w_ref[2:3, :] * u


def _sigmoid(x):
    return 1.0 / (1.0 + jnp.exp(-x))


def _glu_fwd(up, cw, cb, *, name, bc=256):
    s_len, f2 = up.shape
    f = f2 // 2
    bc = _div_block(f, bc)
    nf = f // bc

    def body(ug_ref, uv_ref, wg_ref, wv_ref, bg_ref, bv_ref, o_ref):
        yg = _conv(ug_ref[...], wg_ref, bg_ref)
        yv = _conv(uv_ref[...], wv_ref, bv_ref)
        o_ref[...] = (yg * _sigmoid(yg) * yv).astype(BF16)

    big = lambda off: pl.BlockSpec((s_len, bc), lambda j: (0, j + off))
    wsp = lambda off: pl.BlockSpec((3, bc), lambda j: (0, j + off))
    bsp = lambda off: pl.BlockSpec((1, bc), lambda j: (0, j + off))
    return pl.pallas_call(
        body, name=name, grid=(nf,), in_specs=[big(0), big(nf), wsp(0), wsp(nf), bsp(0), bsp(nf)],
        out_specs=pl.BlockSpec((s_len, bc), lambda j: (0, j)), out_shape=jax.ShapeDtypeStruct((s_len, f), BF16),
        compiler_params=_params("parallel"))(up, up, cw, cw, cb, cb)


def _glu_bwd(up, cw, cb, dact, *, name, bc=256):
    s_len, f2 = up.shape
    f = f2 // 2
    bc = _div_block(f, bc)
    nf = f // bc

    def body(ut_ref, up_ref, wt_ref, wp_ref, bt_ref, bp_ref, da_ref, du_ref, dw_ref, db_ref):
        is_gate = pl.program_id(0) < nf
        ut = ut_ref[...]
        yt = _conv(ut, wt_ref, bt_ref)
        yp = _conv(up_ref[...], wp_ref, bp_ref)
        da = da_ref[...].astype(F32)
        sg = _sigmoid(yt)
        d_gate = da * yp * (sg * (1.0 + yt * (1.0 - sg)))
        d_val = da * (yp * _sigmoid(yp))
        dy = jnp.where(is_gate, d_gate, d_val)
        db_ref[...] = jnp.sum(dy, axis=0, keepdims=True)
        dw_ref[0:1, :] = jnp.sum(dy * _shift_down(ut, 2), axis=0, keepdims=True)
        dw_ref[1:2, :] = jnp.sum(dy * _shift_down(ut, 1), axis=0, keepdims=True)
        dw_ref[2:3, :] = jnp.sum(dy * ut, axis=0, keepdims=True)
        du = wt_ref[2:3, :] * dy + wt_ref[1:2, :] * _shift_up(dy, 1) + wt_ref[0:1, :] * _shift_up(dy, 2)
        du_ref[...] = du.astype(BF16)

    this = lambda r: pl.BlockSpec((r, bc), lambda j: (0, j))
    partner = lambda r: pl.BlockSpec((r, bc), lambda j: (0, (j + nf) % (2 * nf)))
    return pl.pallas_call(
        body, name=name, grid=(2 * nf,),
        in_specs=[this(s_len), partner(s_len), this(3), partner(3), this(1), partner(1),
                  pl.BlockSpec((s_len, bc), lambda j: (0, j % nf))],
        out_specs=[this(s_len), this(3), this(1)],
        out_shape=[jax.ShapeDtypeStruct((s_len, f2), BF16), jax.ShapeDtypeStruct((3, f2), F32),
                   jax.ShapeDtypeStruct((1, f2), F32)],
        compiler_params=_params("parallel"))(up, up, cw, cw, cb, cb, dact)


def _as2d(a):
    return a.reshape(-1, a.shape[-1]) if a.ndim >= 2 else a.reshape(1, -1)


def _adamw(w, g, m, v, *, name, target_bytes=1 << 20):
    shape = w.shape
    w2, m2, v2 = _as2d(w), _as2d(m), _as2d(v)
    g2 = g.reshape(w2.shape)
    r, c = w2.shape
    br = r if r * c * 4 <= target_bytes else _div_block(r, max(8, target_bytes // (4 * c) // 8 * 8), 8)
    c1 = 1.0 - ADAM_B1 ** ADAM_STEP
    c2 = 1.0 - ADAM_B2 ** ADAM_STEP

    def body(w_ref, g_ref, m_ref, v_ref, d_ref, nm_ref, nv_ref):
        gv = g_ref[...]
        nm = ADAM_B1 * m_ref[...] + (1.0 - ADAM_B1) * gv
        nv = ADAM_B2 * v_ref[...] + (1.0 - ADAM_B2) * (gv * gv)
        nm_ref[...] = nm
        nv_ref[...] = nv
        d_ref[...] = -ADAM_LR * ((nm / c1) / (jnp.sqrt(nv / c2) + ADAM_EPS) + ADAM_WD * w_ref[...])

    spec = pl.BlockSpec((br, c), lambda i: (i, 0))
    sds = jax.ShapeDtypeStruct((r, c), F32)
    d, nm, nv = pl.pallas_call(
        body, name=name, grid=(r // br,), in_specs=[spec] * 4, out_specs=[spec] * 3, out_shape=[sds] * 3,
        compiler_params=_params("parallel"))(w2, g2, m2, v2)
    return d.reshape(shape), nm.reshape(shape), nv.reshape(shape)


def _add_halves(g, recv, place, *, name, target_bytes=1 << 20):
    _, _, r, c = g.shape
    br = _div_block(r, max(16, target_bytes // (2 * c) // 16 * 16), 16)

    def body(x_ref, y_ref, c_ref, g_ref, r_ref, o_ref):
        o_ref[...] = (g_ref[...].astype(F32) + r_ref[...].astype(F32)).astype(BF16)

    return pl.pallas_call(
        body, name=name,
        grid_spec=pltpu.PrefetchScalarGridSpec(
            num_scalar_prefetch=3, grid=(N_CHIPS, r // br),
            in_specs=[pl.BlockSpec((None, None, br, c), lambda s, i, xr, yr, cr: (s, cr[0], i, 0)),
                      pl.BlockSpec((None, br, c), lambda s, i, xr, yr, cr: (s, i, 0))],
            out_specs=pl.BlockSpec((None, br, c), lambda s, i, xr, yr, cr: (s, i, 0))),
        out_shape=jax.ShapeDtypeStruct((N_CHIPS, r, c), BF16),
        compiler_params=_params("parallel", "parallel"))(*place, g, recv)


def _sum_chips(p, landed, place, *, name, target_bytes=1 << 20):
    _, r, c = p.shape
    br = _div_block(r, max(16, target_bytes // (4 * c) // 16 * 16), 16)

    def body(x_ref, y_ref, c_ref, p_ref, l1_ref, l2_ref, l3_ref, o_ref):
        o_ref[...] = ((p_ref[...].astype(F32) + l1_ref[...].astype(F32)) + l2_ref[...].astype(F32)) + l3_ref[...].astype(F32)

    slot = lambda k: pl.BlockSpec((None, br, c), lambda i, xr, yr, cr: ((2 * xr[0] + yr[0] + k) % N_CHIPS, i, 0))
    return pl.pallas_call(
        body, name=name,
        grid_spec=pltpu.PrefetchScalarGridSpec(
            num_scalar_prefetch=3, grid=(r // br,), in_specs=[slot(0), slot(1), slot(2), slot(3)],
            out_specs=pl.BlockSpec((None, br, c), lambda i, xr, yr, cr: (cr[0], i, 0))),
        out_shape=jax.ShapeDtypeStruct((2, r, c), F32),
        compiler_params=_params("parallel"))(*place, p, landed, landed, landed)


def _place_shard(w, place, *, dtype, name, target_bytes=1 << 20):
    r, c = w.shape
    hr = r // 2
    mult = 16 if dtype == BF16 else 8
    br = _div_block(hr, max(mult, target_bytes // (4 * c) // mult * mult), mult)
    nb = hr // br

    def body(x_ref, y_ref, c_ref, w_ref, o_ref):
        o_ref[...] = w_ref[...].astype(dtype)

    return pl.pallas_call(
        body, name=name,
        grid_spec=pltpu.PrefetchScalarGridSpec(
            num_scalar_prefetch=3, grid=(2, nb),
            in_specs=[pl.BlockSpec((br, c), lambda h, i, xr, yr, cr: (h * nb + i, 0))],
            out_specs=pl.BlockSpec((None, None, br, c), lambda h, i, xr, yr, cr: (2 * xr[0] + yr[0], h, i, 0))),
        out_shape=jax.ShapeDtypeStruct((N_CHIPS, 2, hr, c), dtype),
        compiler_params=_params("parallel", "parallel"))(*place, w)


def _sum_devices(x, *, name):
    n, r, c = x.shape
    br = _div_block(r, 512, 8)

    def body(x_ref, o_ref):
        acc = x_ref[0]
        for s in range(1, n):
            acc = acc + x_ref[s]
        o_ref[...] = acc

    return pl.pallas_call(
        body, name=name, grid=(r // br,), in_specs=[pl.BlockSpec((n, br, c), lambda i: (0, i, 0))],
        out_specs=pl.BlockSpec((br, c), lambda i: (i, 0)), out_shape=jax.ShapeDtypeStruct((r, c), F32),
        compiler_params=_params("parallel"))(x)


_ANY = pl.BlockSpec(memory_space=pl.ANY)


def _place():
    x, y, c = lax.axis_index("x"), lax.axis_index("y"), lax.axis_index("c")
    other_chips = [(1 - x, y), (x, 1 - y), (1 - x, 1 - y)]
    return x, y, c, other_chips


_HBM = pl.BlockSpec(memory_space=pltpu.HBM)
_SEM = pl.BlockSpec(memory_space=pltpu.SEMAPHORE)
_EFFECT = pltpu.SideEffectType.DATAFLOW_SIDE_EFFECTING


def _in_hbm(a):
    return pltpu.with_memory_space_constraint(a, pltpu.HBM)


def _token_spec():
    return pl.BlockSpec(memory_space=pltpu.VMEM), jax.ShapeDtypeStruct((8, LANES), F32)


def _gather_ici_start(bufs, after, *, name):
    n = len(bufs)

    def body(*refs):
        b_refs = refs[:n]
        send_sems, recv_sems = refs[n + 1], refs[n + 2]
        token = refs[-1]
        x, y, c, chips = _place()
        me = 2 * x + y
        for i in range(n):
            for j, (px, py) in enumerate(chips):
                pltpu.make_async_remote_copy(src_ref=b_refs[i].at[me, c], dst_ref=b_refs[i].at[me, c],
                                             send_sem=send_sems.at[3 * i + j], recv_sem=recv_sems.at[3 * i + j],
                                             device_id=(px, py, c), device_id_type=MESH).start()
        token[...] = jnp.zeros_like(token)

    tspec, tshape = _token_spec()
    outs = pl.pallas_call(
        body, name=name, in_specs=[_HBM] * n + [_ANY], out_specs=(_SEM, _SEM, *[_HBM] * n, tspec),
        out_shape=(pltpu.SemaphoreType.DMA((3 * n,)), pltpu.SemaphoreType.DMA((3 * n,)),
                   *[pltpu.HBM(a.shape, a.dtype) for a in bufs], tshape),
        input_output_aliases={i: 2 + i for i in range(n)},
        compiler_params=pltpu.CompilerParams(has_side_effects=_EFFECT),
    )(*[_in_hbm(a) for a in bufs], after)
    return outs[0], outs[1], list(outs[2:2 + n]), outs[-1]


def _gather_ici_wait(send_sems, recv_sems, bufs, after, *, name):
    n = len(bufs)

    def body(*refs):
        b_refs = refs[:n]
        send_sems, recv_sems = refs[n], refs[n + 1]
        x, y, c, chips = _place()
        me = 2 * x + y
        for i in range(n):
            for j, (px, py) in enumerate(chips):
                cp = pltpu.make_async_remote_copy(src_ref=b_refs[i].at[me, c], dst_ref=b_refs[i].at[2 * px + py, c],
                                                  send_sem=send_sems.at[3 * i + j], recv_sem=recv_sems.at[3 * i + j],
                                                  device_id=(px, py, c), device_id_type=MESH)
                cp.wait_send()
                cp.wait_recv()

    outs = pl.pallas_call(
        body, name=name, in_specs=[_HBM] * n + [_SEM, _SEM, _ANY], out_specs=[_HBM] * n,
        out_shape=[pltpu.HBM(a.shape, a.dtype) for a in bufs], input_output_aliases={i: i for i in range(n)},
        compiler_params=pltpu.CompilerParams(has_side_effects=_EFFECT),
    )(*bufs, send_sems, recv_sems, after)
    return list(outs)


def _gather_d2d(bufs, *, name):
    n = len(bufs)

    def body(*refs):
        b_refs = refs[n:2 * n]
        send_sems, recv_sems = refs[2 * n:]
        x, y, c, chips = _place()
        sends = []
        for i in range(n):
            for j, (px, py) in enumerate(chips):
                mine = b_refs[i].at[2 * px + py, c]
                cp = pltpu.make_async_remote_copy(src_ref=mine, dst_ref=mine, send_sem=send_sems.at[3 * i + j],
                                                  recv_sem=recv_sems.at[3 * i + j], device_id=(x, y, 1 - c),
                                                  device_id_type=MESH)
                cp.start()
                sends.append((cp, i, j, px, py))
        for cp, i, j, px, py in sends:
            theirs = b_refs[i].at[2 * px + py, 1 - c]
            pltpu.make_async_remote_copy(src_ref=theirs, dst_ref=theirs, send_sem=send_sems.at[3 * i + j],
                                         recv_sem=recv_sems.at[3 * i + j], device_id=(x, y, 1 - c),
                                         device_id_type=MESH).wait_recv()
            cp.wait_send()

    return pl.pallas_call(
        body, name=name, in_specs=[_ANY] * n, out_specs=[_ANY] * n, input_output_aliases={i: i for i in range(n)},
        out_shape=[jax.ShapeDtypeStruct(a.shape, a.dtype) for a in bufs],
        scratch_shapes=[pltpu.SemaphoreType.DMA((3 * n,)), pltpu.SemaphoreType.DMA((3 * n,))],
    )(*bufs)


def _sibling_halves(gs, *, name):
    n = len(gs)

    def body(*refs):
        g_refs, o_refs = refs[:n], refs[n:2 * n]
        send_sems, recv_sems = refs[2 * n:]
        x, y, c, _ = _place()
        copies = []
        for i in range(n):
            for s in range(N_CHIPS):
                k = i * N_CHIPS + s
                cp = pltpu.make_async_remote_copy(src_ref=g_refs[i].at[s, 1 - c], dst_ref=o_refs[i].at[s],
                                                  send_sem=send_sems.at[k], recv_sem=recv_sems.at[k],
                                                  device_id=(x, y, 1 - c), device_id_type=MESH)
                cp.start()
                copies.append(cp)
        for cp in copies:
            cp.wait()

    return pl.pallas_call(
        body, name=name, in_specs=[_ANY] * n, out_specs=[_ANY] * n,
        out_shape=[jax.ShapeDtypeStruct((N_CHIPS,) + g.shape[2:], g.dtype) for g in gs],
        scratch_shapes=[pltpu.SemaphoreType.DMA((n * N_CHIPS,)), pltpu.SemaphoreType.DMA((n * N_CHIPS,))],
    )(*gs)


def _chip_scatter_start(ps, after, *, name):
    n = len(ps)

    def body(*refs):
        p_refs, l_refs = refs[:n], refs[n:2 * n]
        send_sems, recv_sems = refs[2 * n + 1], refs[2 * n + 2]
        token = refs[-1]
        x, y, c, chips = _place()
        me = 2 * x + y
        for i in range(n):
            for j, (px, py) in enumerate(chips):
                pltpu.make_async_remote_copy(src_ref=p_refs[i].at[2 * px + py], dst_ref=l_refs[i].at[me],
                                             send_sem=send_sems.at[3 * i + j], recv_sem=recv_sems.at[3 * i + j],
                                             device_id=(px, py, c), device_id_type=MESH).start()
        token[...] = jnp.zeros_like(token)

    tspec, tshape = _token_spec()
    lands = [_in_hbm(lax.empty(p.shape, p.dtype)) for p in ps]
    outs = pl.pallas_call(
        body, name=name, in_specs=[_HBM] * (2 * n) + [_ANY], out_specs=(_SEM, _SEM, *[_HBM] * (2 * n), tspec),
        out_shape=(pltpu.SemaphoreType.DMA((3 * n,)), pltpu.SemaphoreType.DMA((3 * n,)),
                   *[pltpu.HBM(p.shape, p.dtype) for p in ps], *[pltpu.HBM(p.shape, p.dtype) for p in ps], tshape),
        input_output_aliases={i: 2 + i for i in range(2 * n)},
        compiler_params=pltpu.CompilerParams(has_side_effects=_EFFECT),
    )(*[_in_hbm(p) for p in ps], *lands, after)
    return outs[0], outs[1], list(outs[2:2 + n]), list(outs[2 + n:2 + 2 * n]), outs[-1]


def _chip_scatter_wait(send_sems, recv_sems, ps, lands, after, *, name):
    n = len(ps)

    def body(*refs):
        p_refs, l_refs = refs[:n], refs[n:2 * n]
        send_sems, recv_sems = refs[2 * n], refs[2 * n + 1]
        x, y, c, chips = _place()
        for i in range(n):
            for j, (px, py) in enumerate(chips):
                cp = pltpu.make_async_remote_copy(src_ref=p_refs[i].at[2 * px + py], dst_ref=l_refs[i].at[2 * px + py],
                                                  send_sem=send_sems.at[3 * i + j], recv_sem=recv_sems.at[3 * i + j],
                                                  device_id=(px, py, c), device_id_type=MESH)
                cp.wait_send()
                cp.wait_recv()

    outs = pl.pallas_call(
        body, name=name, in_specs=[_HBM] * (2 * n) + [_SEM, _SEM, _ANY], out_specs=[_HBM] * (2 * n),
        out_shape=[pltpu.HBM(p.shape, p.dtype) for p in ps] * 2, input_output_aliases={i: i for i in range(2 * n)},
        compiler_params=pltpu.CompilerParams(has_side_effects=_EFFECT),
    )(*ps, *lands, send_sems, recv_sems, after)
    return list(outs[:n]), list(outs[n:])


def _sibling_share(bufs, *, name):
    n = len(bufs)

    def body(*refs):
        b_refs = refs[n:2 * n]
        send_sems, recv_sems = refs[2 * n:]
        x, y, c, _ = _place()
        copies = []
        for i in range(n):
            cp = pltpu.make_async_remote_copy(src_ref=b_refs[i].at[c], dst_ref=b_refs[i].at[c], send_sem=send_sems.at[i],
                                              recv_sem=recv_sems.at[i], device_id=(x, y, 1 - c), device_id_type=MESH)
            cp.start()
            copies.append((cp, i))
        for cp, i in copies:
            theirs = b_refs[i].at[1 - c]
            pltpu.make_async_remote_copy(src_ref=theirs, dst_ref=theirs, send_sem=send_sems.at[i],
                                         recv_sem=recv_sems.at[i], device_id=(x, y, 1 - c),
                                         device_id_type=MESH).wait_recv()
            cp.wait_send()

    return pl.pallas_call(
        body, name=name, in_specs=[_ANY] * n, out_specs=[_ANY] * n, input_output_aliases={i: i for i in range(n)},
        out_shape=[jax.ShapeDtypeStruct(a.shape, a.dtype) for a in bufs],
        scratch_shapes=[pltpu.SemaphoreType.DMA((n,)), pltpu.SemaphoreType.DMA((n,))],
    )(*bufs)


def _broadcast_all(v, *, name):
    def body(v_ref, o_ref, send_sems, recv_sems, local_sem):
        x, y, c, _ = _place()
        me = 4 * x + 2 * y + c
        loc = pltpu.make_async_copy(v_ref, o_ref.at[me], local_sem)
        loc.start()
        copies = []
        for k in range(1, 8):
            dx, dy, dc = (k >> 2) & 1, (k >> 1) & 1, k & 1
            to = (1 - x if dx else x, 1 - y if dy else y, 1 - c if dc else c)
            cp = pltpu.make_async_remote_copy(src_ref=v_ref, dst_ref=o_ref.at[me], send_sem=send_sems.at[k - 1],
                                              recv_sem=recv_sems.at[k - 1], device_id=to, device_id_type=MESH)
            cp.start()
            copies.append((cp, k, to))
        for cp, k, to in copies:
            cp.wait_send()
            theirs = o_ref.at[4 * to[0] + 2 * to[1] + to[2]]
            pltpu.make_async_remote_copy(src_ref=theirs, dst_ref=theirs, send_sem=send_sems.at[k - 1],
                                         recv_sem=recv_sems.at[k - 1], device_id=to, device_id_type=MESH).wait_recv()
        loc.wait()

    return pl.pallas_call(
        body, name=name, in_specs=[_ANY], out_specs=_ANY,
        out_shape=jax.ShapeDtypeStruct((8,) + v.shape, v.dtype),
        scratch_shapes=[pltpu.SemaphoreType.DMA((7,)), pltpu.SemaphoreType.DMA((7,)), pltpu.SemaphoreType.DMA(())],
    )(v)


def _gather_begin(shards, place, after, *, name):
    names = list(shards)
    bufs = [_place_shard(shards[k], place, dtype=F32 if k == 'small' else BF16, name=f"{name}_place_{k}") for k in names]
    send_sems, recv_sems, bufs, token = _gather_ici_start(bufs, after, name=name + "_ici_start")
    return (names, [shards[k].shape for k in names], send_sems, recv_sems, bufs), token


def _gather_end(state, after, *, name):
    names, shapes, send_sems, recv_sems, bufs = state
    bufs = _gather_ici_wait(send_sems, recv_sems, bufs, after, name=name + "_ici_wait")
    bufs = _gather_d2d(bufs, name=name + "_d2d")
    return {k: o.reshape((N_CHIPS,) + sh) for k, o, sh in zip(names, bufs, shapes)}


def _reduce_begin(grads, place, *, name):
    names = list(grads)
    gs = [grads[k].reshape(N_CHIPS, 2, grads[k].shape[1] // 2, grads[k].shape[2]) for k in names]
    recv = _sibling_halves(gs, name=name + "_sib")
    ps = [_add_halves(g, r, place, name=f"{name}_add2_{k}") for g, r, k in zip(gs, recv, names)]
    send_sems, recv_sems, ps, lands, token = _chip_scatter_start(ps, recv[0], name=name + "_scatter_start")
    return (names, [grads[k].shape[1:] for k in names], send_sems, recv_sems, ps, lands), token


def _reduce_end(state, place, after, *, name):
    names, shapes, send_sems, recv_sems, ps, lands = state
    ps, lands = _chip_scatter_wait(send_sems, recv_sems, ps, lands, after, name=name + "_scatter_wait")
    rs = [_sum_chips(p, l, place, name=f"{name}_sum4_{k}") for p, l, k in zip(ps, lands, names)]
    both = _sibling_share(rs, name=name + "_share")
    return {k: b.reshape(sh) for k, b, sh in zip(names, both, shapes)}


def _pad_lanes(a, n=LANES):
    return jnp.pad(a, [(0, 0)] * (a.ndim - 1) + [(0, n - a.shape[-1])])


def _unshard_cols(g):
    return jnp.transpose(g, (1, 0, 2)).reshape(g.shape[1], -1)


def _shard_cols(w):
    k, n = w.shape
    return jnp.transpose(w.reshape(k, N_CHIPS, n // N_CHIPS), (1, 0, 2))


def _ffn_fwd(h, p, tag):
    b = _rms_fwd(h, p['ffn_norm'], name=f"{tag}_ffn_norm")
    up = _mm(b, p['ffn_w_up'], b_sh='n', name=f"{tag}_ffn_up", bn=1408)
    act = _glu_fwd(up, p['ffn_conv_w'], p['ffn_conv_b'], name=f"{tag}_ffn_glu")
    out = _mm(act, p['ffn_w_down'], res=h, name=f"{tag}_ffn_down", bk=704)
    return out, (h, b, up, act)


def _ffn_bwd(dh, saved, p, tag):
    h, b, up, act = saved
    dact = _mm(dh, p['ffn_w_down'], tb=True, out_dtype=BF16, name=f"{tag}_ffn_dact", bn=1408)
    dw_down = _mm(act, dh, ta=True, out_dtype=BF16, name=f"{tag}_ffn_dwdown", bm=1408)
    dup, dcw, dcb = _glu_bwd(up, p['ffn_conv_w'], p['ffn_conv_b'], dact, name=f"{tag}_ffn_dglu")
    dw_up = _mm(b, dup, ta=True, o_sh=True, out_dtype=BF16, name=f"{tag}_ffn_dwup", bn=1408)
    db = _mm(dup, p['ffn_w_up'], b_sh='k', name=f"{tag}_ffn_db", bk=1408)
    dh_in, dg = _rms_bwd(h, p['ffn_norm'], db, res=dh, name=f"{tag}_ffn_dnorm")
    big = {'ffn_w_up': dw_up, 'ffn_w_down': dw_down.reshape(N_CHIPS, -1, dw_down.shape[1])}
    small = {'ffn_norm': dg, 'ffn_conv_w': dcw, 'ffn_conv_b': dcb}
    return dh_in, big, small


def _qkv_attn_fwd(kind, h, p, tag, n_heads):
    a = _rms_fwd(h, p['mix_norm'], name=f"{tag}_norm")
    if kind == 'fox':
        qkv = _mm(a, p['w_in'], name=f"{tag}_qkv", bn=896)
        cum = _fgate_fwd(qkv, p['b_f'], fcol=3 * n_heads, name=f"{tag}_fgate")
        cum_t = cum[:, :n_heads].T
        cq, ck = cum_t[:, :, None], cum_t[:, None, :]
    else:
        qkv = _mm(a, p['w_in'], b_sh='n', name=f"{tag}_qkv", bn=768)
        cq = ck = None
    cols = dict(qcol=lambda hh: hh, kcol=lambda hh: n_heads + hh, vcol=lambda hh: 2 * n_heads + hh)
    o = _attn_fwd(kind, qkv, qkv, qkv, name=f"{tag}_attn", n_heads=n_heads, dqk=HEAD_DIM, scale=HEAD_DIM ** -0.5,
                  gains=p['qk_gain'], cq=cq, ck=ck, **cols)
    out = _mm(o, p['w_out'], res=h, name=f"{tag}_out")
    return out, (h, a, qkv, o, cq, ck)


def _qkv_attn_bwd(kind, dh, saved, p, tag, n_heads):
    h, a, qkv, o, cq, ck = saved
    do = _mm(dh, p['w_out'], tb=True, out_dtype=BF16, name=f"{tag}_do")
    dw_out = _mm(o, dh, ta=True, out_dtype=BF16, name=f"{tag}_dwout")
    cols = dict(qcol=lambda hh: hh, kcol=lambda hh: n_heads + hh, vcol=lambda hh: 2 * n_heads + hh)
    outs = _attn_bwd(kind, qkv, qkv, qkv, o, do, name=f"{tag}_dattn", n_heads=n_heads, dqk=HEAD_DIM,
                     scale=HEAD_DIM ** -0.5, gains=p['qk_gain'], cq=cq, ck=ck, **cols)
    dq, dk, dv, dgain = outs[:4]
    small = {'q_gain': dgain[0], 'k_gain': dgain[1]}
    if kind == 'fox':
        dcq, dck = outs[4:]
        dca = _pad_lanes(dcq[:, :, 0].T)
        dcb = _pad_lanes(dck[:, 0, :].T)
        dflog, dbf = _fgate_bwd(qkv, p['b_f'], dca, dcb, fcol=3 * n_heads, n_heads=n_heads, name=f"{tag}_dfgate")
        small['b_f'] = dbf[:, :n_heads]
        dqkv = jnp.concatenate([dq, dk, dv, dflog], axis=1)
        dw_in = _mm(a, dqkv, ta=True, out_dtype=BF16, name=f"{tag}_dwin", bn=896)
        da = _mm(dqkv, p['w_in'], tb=True, name=f"{tag}_da", bk=896)
        dw_in = _shard_cols(dw_in[:, :3 * n_heads * HEAD_DIM + n_heads])
    else:
        dqkv = jnp.concatenate([dq, dk, dv], axis=1)
        dw_in = _mm(a, dqkv, ta=True, o_sh=True, out_dtype=BF16, name=f"{tag}_dwin", bn=768)
        da = _mm(dqkv, p['w_in'], b_sh='k', name=f"{tag}_da", bk=768)
    dh_in, dg = _rms_bwd(h, p['mix_norm'], da, res=dh, name=f"{tag}_dnorm")
    small['mix_norm'] = dg
    big = {'w_in': dw_in, 'w_out': dw_out.reshape(N_CHIPS, -1, dw_out.shape[1])}
    return dh_in, big, small


def _mla_fwd(h, p, tag, n_heads):
    a = _rms_fwd(h, p['mix_norm'], name=f"{tag}_norm")
    c = _mm(a, p['w_in'], name=f"{tag}_latent", bn=1152)
    cn = _mla_latent_fwd(c, p['a_gain'], name=f"{tag}_latent_norm")
    qp = _mm(cn[:, :MLA_Q_RANK], p['w_q_b'], name=f"{tag}_q_up")
    kv = _mm(cn[:, MLA_Q_RANK:], p['w_kv_b'], b_sh='n', name=f"{tag}_kv_up")
    qc, kc = _mla_prep_fwd(qp, kv, c, p['gq'], p['gk'], p['cos'], p['sin'], n_heads=n_heads, name=f"{tag}_prep")
    cols = dict(qcol=lambda hh: hh, kcol=lambda hh: hh, vcol=lambda hh: 2 * hh + 1)
    scale = (MLA_NOPE + MLA_ROPE) ** -0.5
    o = _attn_fwd('mla', qc, kc, kv, name=f"{tag}_attn", n_heads=n_heads, dqk=2 * LANES, scale=scale, **cols)
    out = _mm(o, p['w_out'], res=h, name=f"{tag}_out")
    return out, (h, a, c, cn, qp, kv, qc, kc, o)


def _mla_bwd(dh, saved, p, tag, n_heads):
    h, a, c, cn, qp, kv, qc, kc, o = saved
    do = _mm(dh, p['w_out'], tb=True, out_dtype=BF16, name=f"{tag}_do")
    dw_out = _mm(o, dh, ta=True, out_dtype=BF16, name=f"{tag}_dwout")
    cols = dict(qcol=lambda hh: hh, kcol=lambda hh: hh, vcol=lambda hh: 2 * hh + 1)
    scale = (MLA_NOPE + MLA_ROPE) ** -0.5
    dqc, dkc, dv = _attn_bwd('mla', qc, kc, kv, o, do, name=f"{tag}_dattn", n_heads=n_heads, dqk=2 * LANES,
                             scale=scale, **cols)
    dqn, dqr, dkv, dkr, dgq, dgk = _mla_prep_bwd(qp, kv, c, p['gq'], p['gk'], p['cos'], p['sin'], dqc, dkc, dv,
                                                 n_heads=n_heads, name=f"{tag}_dprep")
    dqp = jnp.concatenate([dqn, dqr], axis=1)
    cn_q, cn_kv = cn[:, :MLA_Q_RANK], cn[:, MLA_Q_RANK:]
    dw_q_b = _mm(cn_q, dqp, ta=True, out_dtype=BF16, name=f"{tag}_dwqb", bm=512)
    dcn_q = _mm(dqp, p['w_q_b'], tb=True, out_dtype=BF16, name=f"{tag}_dcnq")
    dw_kv_b = _mm(cn_kv, dkv, ta=True, o_sh=True, out_dtype=BF16, name=f"{tag}_dwkvb", bm=512)
    dcn_kv = _mm(dkv, p['w_kv_b'], b_sh='k', out_dtype=BF16, name=f"{tag}_dcnkv")
    dc, dga = _mla_latent_bwd(c, p['a_gain'], dcn_q, dcn_kv, dkr, name=f"{tag}_dlatent")
    dw_in = _mm(a, dc, ta=True, out_dtype=BF16, name=f"{tag}_dwin", bn=1152)
    da = _mm(dc, p['w_in'], tb=True, name=f"{tag}_da", bk=1152)
    dh_in, dg = _rms_bwd(h, p['mix_norm'], da, res=dh, name=f"{tag}_dnorm")
    k_rank = dw_q_b.shape[0]
    nope = dw_q_b[:, :n_heads * LANES].reshape(k_rank, n_heads, LANES)
    rope = dw_q_b[:, n_heads * LANES:].reshape(k_rank, n_heads, LANES)[:, :, :MLA_ROPE]
    dw_q_b = jnp.concatenate([nope, rope], axis=2).reshape(k_rank, n_heads * (MLA_NOPE + MLA_ROPE))
    w_in_cols = MLA_Q_RANK + MLA_KV_RANK + MLA_ROPE
    big = {'w_in': dw_in[:, :w_in_cols].reshape(N_CHIPS, -1, w_in_cols), 'w_q_b': _shard_cols(dw_q_b),
           'w_kv_b': dw_kv_b, 'w_out': dw_out.reshape(N_CHIPS, -1, dw_out.shape[1])}
    small = {'mix_norm': dg, 'q_a_gain': dga[:, :MLA_Q_RANK], 'kv_a_gain': dga[:, MLA_Q_RANK:],
             'q_gain': jnp.concatenate([dgq[0], dgq[1][:, :MLA_ROPE]], axis=1),
             'k_gain': jnp.concatenate([dgk[0], dgk[1][:, :MLA_ROPE]], axis=1)}
    return dh_in, big, small


def _sgu_fwd(h, p, tag):
    a = _rms_fwd(h, p['mix_norm'], name=f"{tag}_norm")
    uv = _mm(a, p['w_in'], b_sh='n', name=f"{tag}_in")
    u, vn = _sgu_act_fwd(uv, p['v_gain'], name=f"{tag}_act")
    gated = _sgu_mix_fwd(u, vn, p['w_s'], p['b_s'], name=f"{tag}_mix")
    out = _mm(gated, p['w_out'], res=h, name=f"{tag}_out")
    return out, (h, a, uv, u, vn, gated)


def _sgu_bwd(dh, saved, p, tag):
    h, a, uv, u, vn, gated = saved
    dgated = _mm(dh, p['w_out'], tb=True, out_dtype=BF16, name=f"{tag}_dgated")
    dw_out = _mm(gated, dh, ta=True, out_dtype=BF16, name=f"{tag}_dwout")
    du, dvn, dws, dbs = _sgu_mix_bwd(u, vn, p['w_s'], p['b_s'], dgated, name=f"{tag}_dmix")
    duv, dvg = _sgu_act_bwd(uv, p['v_gain'], du, dvn, name=f"{tag}_dact")
    dw_in = _mm(a, duv, ta=True, o_sh=True, out_dtype=BF16, name=f"{tag}_dwin")
    da = _mm(duv, p['w_in'], b_sh='k', name=f"{tag}_da")
    dh_in, dg = _rms_bwd(h, p['mix_norm'], da, res=dh, name=f"{tag}_dnorm")
    big = {'w_in': dw_in, 'w_out': dw_out.reshape(N_CHIPS, -1, dw_out.shape[1])}
    small = {'mix_norm': dg, 'v_gain': dvg, 'w_s': dws, 'b_s': dbs[:, :, 0]}
    return dh_in, big, small


def _pack(parts):
    flat = jnp.concatenate([p.reshape(-1).astype(F32) for p in parts])
    rows = -(-flat.shape[0] // LANES)
    rows = -(-rows // 32) * 32
    return jnp.pad(flat, (0, rows * LANES - flat.shape[0])).reshape(rows, LANES)


def _unpack(packed, shapes):
    flat = packed.reshape(-1)
    out, off = [], 0
    for s in shapes:
        n = 1
        for d in s:
            n *= d
        out.append(flat[off:off + n].reshape(s))
        off += n
    return out


MIXERS = ('fox', 'mla', 'sb', 'sgu')
WEIGHT_NAMES = ['mix_norm', 'ffn_norm', 'fox_w_in', 'fox_b_f', 'fox_q_gain', 'fox_k_gain', 'fox_w_out', 'mla_w_in',
                'mla_q_a_gain', 'mla_kv_a_gain', 'mla_w_q_b', 'mla_w_kv_b', 'mla_q_gain', 'mla_k_gain', 'mla_w_out',
                'sb_w_in', 'sb_q_gain', 'sb_k_gain', 'sb_w_out', 'sgu_w_in', 'sgu_v_gain', 'sgu_w_s', 'sgu_b_s',
                'sgu_w_out', 'ffn_w_up', 'ffn_conv_w', 'ffn_conv_b', 'ffn_w_down']
SMALL_SHARDED = {'mla_q_a_gain': 1, 'mla_kv_a_gain': 1, 'sgu_v_gain': 1, 'ffn_conv_w': 2}
BIG = ['fox_w_in', 'fox_w_out', 'mla_w_in', 'mla_w_q_b', 'mla_w_kv_b', 'mla_w_out', 'sb_w_in', 'sb_w_out', 'sgu_w_in',
       'sgu_w_out', 'ffn_w_up', 'ffn_w_down']


def kernel(x, positions, mix_norm, ffn_norm, fox_w_in, fox_b_f, fox_q_gain, fox_k_gain, fox_w_out, mla_w_in, mla_q_a_gain, mla_kv_a_gain, mla_w_q_b, mla_w_kv_b, mla_q_gain, mla_k_gain, mla_w_out, sb_w_in, sb_q_gain, sb_k_gain, sb_w_out, sgu_w_in, sgu_v_gain, sgu_w_s, sgu_b_s, sgu_w_out, ffn_w_up, ffn_conv_w, ffn_conv_b, ffn_w_down, loss_target, m_mix_norm, m_ffn_norm, m_fox_w_in, m_fox_b_f, m_fox_q_gain, m_fox_k_gain, m_fox_w_out, m_mla_w_in, m_mla_q_a_gain, m_mla_kv_a_gain, m_mla_w_q_b, m_mla_w_kv_b, m_mla_q_gain, m_mla_k_gain, m_mla_w_out, m_sb_w_in, m_sb_q_gain, m_sb_k_gain, m_sb_w_out, m_sgu_w_in, m_sgu_v_gain, m_sgu_w_s, m_sgu_b_s, m_sgu_w_out, m_ffn_w_up, m_ffn_conv_w, m_ffn_conv_b, m_ffn_w_down, v_mix_norm, v_ffn_norm, v_fox_w_in, v_fox_b_f, v_fox_q_gain, v_fox_k_gain, v_fox_w_out, v_mla_w_in, v_mla_q_a_gain, v_mla_kv_a_gain, v_mla_w_q_b, v_mla_w_kv_b, v_mla_q_gain, v_mla_k_gain, v_mla_w_out, v_sb_w_in, v_sb_q_gain, v_sb_k_gain, v_sb_w_out, v_sgu_w_in, v_sgu_v_gain, v_sgu_w_s, v_sgu_b_s, v_sgu_w_out, v_ffn_w_up, v_ffn_conv_w, v_ffn_conv_b, v_ffn_w_down):
    args = dict(locals())
    W = {k: args[k] for k in WEIGHT_NAMES}
    M = {k: args['m_' + k] for k in WEIGHT_NAMES}
    V = {k: args['v_' + k] for k in WEIGHT_NAMES}
    depth = mix_norm.shape[0]
    s_len, d_model = x.shape[1], x.shape[2]
    n_heads = d_model // HEAD_DIM
    assert all(W[k].shape[0] == 1 for k in WEIGHT_NAMES if k.split('_')[0] in MIXERS), "one layer per mixer"
    xi, yi, ci = lax.axis_index("x"), lax.axis_index("y"), lax.axis_index("c")
    chip = 2 * xi + yi
    place = tuple(jnp.reshape(v, (1,)).astype(jnp.int32) for v in (xi, yi, ci))

    small_local = _pack([W[k][0] if k != 'ffn_conv_w' else W[k] for k in SMALL_SHARDED])

    def layer_shards(i):
        mixer = MIXERS[i % len(MIXERS)]
        shards = {k: W[k][0] for k in BIG if k.startswith(mixer + '_')}
        shards['ffn_w_up'] = W['ffn_w_up'][i]
        shards['ffn_w_down'] = W['ffn_w_down'][i]
        if i == 0:
            shards['small'] = small_local
        return shards

    gathered = {}
    state, token = _gather_begin(layer_shards(0), place, mix_norm, name="gather0")
    gathered[0] = _gather_end(state, token, name="gather0")
    small_shapes = [W[k][0].shape if k != 'ffn_conv_w' else W[k].shape for k in SMALL_SHARDED]
    per_chip = [_unpack(gathered[0]['small'][s], small_shapes) for s in range(N_CHIPS)]
    full_small = {k: jnp.concatenate([per_chip[s][j] for s in range(N_CHIPS)], axis=-1)
                  for j, k in enumerate(SMALL_SHARDED)}

    pos = positions.reshape(s_len).astype(F32)
    inv_freq = ROPE_THETA ** (-jnp.arange(0, MLA_ROPE, 2, dtype=F32) / MLA_ROPE)
    ang = pos[:, None] * inv_freq
    cos_t = _pad_lanes(jnp.concatenate([jnp.cos(ang), jnp.cos(ang)], axis=1))
    sin_t = _pad_lanes(jnp.concatenate([-jnp.sin(ang), jnp.sin(ang)], axis=1))

    def layer_params(i):
        mixer = MIXERS[i % len(MIXERS)]
        g = gathered[i]
        p = {'mix_norm': mix_norm[i:i + 1], 'ffn_norm': ffn_norm[i:i + 1], 'ffn_w_up': g['ffn_w_up'],
             'ffn_w_down': g['ffn_w_down'].reshape(-1, d_model), 'ffn_conv_w': full_small['ffn_conv_w'][i],
             'ffn_conv_b': ffn_conv_b[i:i + 1]}
        rows = lambda w: w.reshape(-1, w.shape[-1])
        if mixer == 'fox':
            w = _unshard_cols(g['fox_w_in'])
            p['w_in'] = jnp.pad(w, ((0, 0), (0, (3 * n_heads + 1) * HEAD_DIM - w.shape[1])))
            p['b_f'] = _pad_lanes(fox_b_f)
            p['qk_gain'] = jnp.stack([fox_q_gain, fox_k_gain])
            p['w_out'] = rows(g['fox_w_out'])
        elif mixer == 'sb':
            p['w_in'] = g['sb_w_in']
            p['qk_gain'] = jnp.stack([sb_q_gain, sb_k_gain])
            p['w_out'] = rows(g['sb_w_out'])
        elif mixer == 'sgu':
            p['w_in'] = g['sgu_w_in']
            p['v_gain'] = full_small['sgu_v_gain'].reshape(1, -1)
            p['w_s'] = sgu_w_s[0]
            p['b_s'] = sgu_b_s[0][:, :, None]
            p['w_out'] = rows(g['sgu_w_out'])
        else:
            w = rows(g['mla_w_in'])
            p['w_in'] = jnp.pad(w, ((0, 0), (0, MLA_Q_RANK + MLA_KV_RANK + LANES - w.shape[1])))
            p['a_gain'] = jnp.concatenate([full_small['mla_q_a_gain'], full_small['mla_kv_a_gain']]).reshape(1, -1)
            wq = _unshard_cols(g['mla_w_q_b']).reshape(MLA_Q_RANK, n_heads, MLA_NOPE + MLA_ROPE)
            p['w_q_b'] = jnp.concatenate([wq[:, :, :MLA_NOPE].reshape(MLA_Q_RANK, -1),
                                          _pad_lanes(wq[:, :, MLA_NOPE:]).reshape(MLA_Q_RANK, -1)], axis=1)
            p['w_kv_b'] = g['mla_w_kv_b']
            p['gq'] = jnp.stack([mla_q_gain[:, :MLA_NOPE], _pad_lanes(mla_q_gain[:, MLA_NOPE:])])
            p['gk'] = jnp.stack([mla_k_gain[:, :MLA_NOPE], _pad_lanes(mla_k_gain[:, MLA_NOPE:])])
            p['cos'], p['sin'] = cos_t, sin_t
            p['w_out'] = rows(g['mla_w_out'])
        return mixer, p

    h = x.reshape(s_len, d_model)
    saved = []
    for i in range(depth):
        if i + 1 < depth:
            state, token = _gather_begin(layer_shards(i + 1), place, gathered[i]['ffn_w_down'], name=f"gather{i + 1}")
        mixer, p = layer_params(i)
        if i + 1 < depth:
            p['mix_norm'] = p['mix_norm'] + token[0:1, 0:1]
        tag = f"l{i}_{mixer}"
        if mixer in ('fox', 'sb'):
            h, sm = _qkv_attn_fwd(mixer, h, p, tag, n_heads)
        elif mixer == 'mla':
            h, sm = _mla_fwd(h, p, tag, n_heads)
        else:
            h, sm = _sgu_fwd(h, p, tag)
        h, sf = _ffn_fwd(h, p, f"l{i}")
        saved.append((mixer, p, sm, sf))
        if i + 1 < depth:
            gathered[i + 1] = _gather_end(state, h, name=f"gather{i + 1}")
    loss_row, dh = _loss(h, loss_target.reshape(s_len, d_model))
    loss = lax.psum(loss_row[0, 0], ("x", "y", "c"))

    big_grads, small_grads = {}, {k: [None] * depth for k in ('mix_norm', 'ffn_norm', 'ffn_conv_w', 'ffn_conv_b')}

    def keep(reduced, i):
        for k, v in reduced.items():
            if k.startswith('ffn_'):
                big_grads.setdefault(k, [None] * depth)[i] = v
            else:
                big_grads[k] = v[None]

    state, token = None, None
    for i in reversed(range(depth)):
        mixer, p, sm, sf = saved[i]
        tag = f"l{i}_{mixer}"
        if token is not None:
            dh = dh + token[0:1, 0:1]
        dh, big_f, small_f = _ffn_bwd(dh, sf, p, f"l{i}")
        if mixer in ('fox', 'sb'):
            dh, big_m, small_m = _qkv_attn_bwd(mixer, dh, sm, p, tag, n_heads)
        elif mixer == 'mla':
            dh, big_m, small_m = _mla_bwd(dh, sm, p, tag, n_heads)
        else:
            dh, big_m, small_m = _sgu_bwd(dh, sm, p, tag)
        if state is not None:
            keep(_reduce_end(state, place, dh, name=f"reduce{i + 1}"), i + 1)
        layer_big = {f"{mixer}_{k}": v for k, v in big_m.items()}
        layer_big.update(big_f)
        state, token = _reduce_begin(layer_big, place, name=f"reduce{i}")
        for k, v in {**small_m, **small_f}.items():
            if k in small_grads:
                small_grads[k][i] = v
            else:
                small_grads[f"{mixer}_{k}"] = v
    last_state = state
    grad_x = dh.reshape(x.shape)
    for k in ('mix_norm', 'ffn_norm', 'ffn_conv_b'):
        small_grads[k] = jnp.concatenate(small_grads[k], axis=0)
    small_grads['ffn_conv_w'] = jnp.stack(small_grads['ffn_conv_w'])

    small_names = [k for k in WEIGHT_NAMES if k not in BIG]
    full_shapes = {k: (W[k].shape[:-1] + (W[k].shape[-1] * N_CHIPS,) if k in SMALL_SHARDED else W[k].shape)
                   for k in small_names}
    packed = _pack([small_grads[k].reshape(full_shapes[k]) for k in small_names])
    summed = _sum_devices(_broadcast_all(packed, name="small_bcast"), name="small_sum")
    small_full = dict(zip(small_names, _unpack(summed, [full_shapes[k] for k in small_names])))
    keep(_reduce_end(last_state, place, summed, name="reduce0"), 0)
    for k in ('ffn_w_up', 'ffn_w_down'):
        big_grads[k] = jnp.stack(big_grads[k])
    grads = dict(big_grads)
    for k in small_names:
        g = small_full[k]
        if k in SMALL_SHARDED:
            n = W[k].shape[-1]
            g = lax.dynamic_slice_in_dim(g, chip * n, n, axis=g.ndim - 1)
        grads[k] = g
    grads = {k: grads[k].reshape(W[k].shape) for k in WEIGHT_NAMES}

    delta, new_m, new_v = {}, {}, {}
    for k in WEIGHT_NAMES:
        delta[k], new_m[k], new_v[k] = _adamw(W[k], grads[k], M[k], V[k], name=f"adamw_{k}")
    return (loss, grad_x, *[grads[k] for k in WEIGHT_NAMES], *[delta[k] for k in WEIGHT_NAMES],
            *[new_m[k] for k in WEIGHT_NAMES], *[new_v[k] for k in WEIGHT_NAMES])
```

```python
import functools

import jax
import jax.numpy as jnp
from jax import lax
from jax.experimental import pallas as pl
from jax.experimental.pallas import tpu as pltpu

F32 = jnp.float32
BF16 = jnp.bfloat16
LANES = 128
HEAD_DIM = 128
NORM_EPS = 1e-6
MLA_Q_RANK = 512
MLA_KV_RANK = 512
MLA_NOPE = 128
MLA_ROPE = 64
ROPE_THETA = 10000.0
SGU_CHUNK = 128
N_CHIPS = 4
ADAM_LR, ADAM_B1, ADAM_B2, ADAM_EPS, ADAM_WD, ADAM_STEP = 0.001, 0.9, 0.999, 1e-08, 0.01, 10
VMEM_LIMIT_BYTES = 56 * 1024 * 1024
MESH = pl.DeviceIdType.MESH
NEG_BIG = -1e30


def _params(*sem):
    return pltpu.CompilerParams(dimension_semantics=sem, vmem_limit_bytes=VMEM_LIMIT_BYTES)


def _div_block(n, target, mult=LANES):
    if n <= target:
        return n
    best = None
    for b in range(mult, target + 1, mult):
        if n % b == 0:
            best = b
    assert best is not None, (n, target, mult)
    return best


def _iota(shape, dim):
    return lax.broadcasted_iota(jnp.int32, shape, dim)


def _dot(a, b, ca, cb):
    return lax.dot_general(a, b, (((ca,), (cb,)), ((), ())), preferred_element_type=F32)


def _mm(a, b, *, name, ta=False, tb=False, a_sh=False, b_sh=None, o_sh=False, res=None, out_dtype=F32,
        bm=1024, bn=1024, bk=512):
    if a_sh:
        assert not ta
        m, k = a.shape[1], a.shape[0] * a.shape[2]
    else:
        m, k = (a.shape[1], a.shape[0]) if ta else a.shape
    if b_sh == 'n':
        n = b.shape[2] * b.shape[0]
        assert b.shape[1] == k and not tb
    elif b_sh == 'k':
        n = b.shape[1]
        assert b.shape[2] * N_CHIPS == k
    else:
        n = b.shape[0] if tb else b.shape[1]
        assert (b.shape[1] if tb else b.shape[0]) == k
    n_sh = n // N_CHIPS
    k_sh = k // N_CHIPS
    bm = _div_block(m, bm, 8 if not ta else LANES)
    bn_limit = n
    if b_sh == 'n':
        bn_limit = b.shape[2]
    if o_sh:
        bn_limit = min(bn_limit, n_sh)
    bn = _div_block(bn_limit, bn)
    assert (not o_sh or n_sh % bn == 0) and (b_sh != 'n' or b.shape[2] % bn == 0)
    bk_limit = k_sh if b_sh == 'k' else k
    if a_sh:
        bk_limit = min(bk_limit, a.shape[2])
    bk = _div_block(bk_limit, bk)
    assert (not a_sh or a.shape[2] % bk == 0) and (b_sh != 'k' or k_sh % bk == 0)
    nbo = n_sh // bn if o_sh else 1
    nbb = b.shape[2] // bn if b_sh == 'n' else 1
    nks = k_sh // bk if b_sh == 'k' else 1
    nka = a.shape[2] // bk if a_sh else 1
    nk = k // bk

    if a_sh:
        a_spec = pl.BlockSpec((None, bm, bk), lambda i, j, q: (q // nka, i, q % nka))
    elif ta:
        a_spec = pl.BlockSpec((bk, bm), lambda i, j, q: (q, i))
    else:
        a_spec = pl.BlockSpec((bm, bk), lambda i, j, q: (i, q))
    if b_sh == 'n':
        b_spec = pl.BlockSpec((None, bk, bn), lambda i, j, q: (j // nbb, q, j % nbb))
    elif b_sh == 'k':
        b_spec = pl.BlockSpec((None, bn, bk), lambda i, j, q: (q // nks, j, q % nks))
    elif tb:
        b_spec = pl.BlockSpec((bn, bk), lambda i, j, q: (j, q))
    else:
        b_spec = pl.BlockSpec((bk, bn), lambda i, j, q: (q, j))
    if o_sh:
        o_spec = pl.BlockSpec((None, bm, bn), lambda i, j, q: (j // nbo, i, j % nbo))
        o_shape = jax.ShapeDtypeStruct((N_CHIPS, m, n_sh), out_dtype)
    else:
        o_spec = pl.BlockSpec((bm, bn), lambda i, j, q: (i, j))
        o_shape = jax.ShapeDtypeStruct((m, n), out_dtype)
    tb_eff = tb or b_sh == 'k'

    def body(a_ref, b_ref, *rest):
        if res is not None:
            r_ref, o_ref, acc = rest
        else:
            o_ref, acc = rest
        q = pl.program_id(2)

        @pl.when(q == 0)
        def _():
            acc[...] = jnp.zeros_like(acc)

        acc[...] += _dot(a_ref[...].astype(BF16), b_ref[...].astype(BF16), 0 if ta else 1, 1 if tb_eff else 0)

        @pl.when(q == nk - 1)
        def _():
            r = acc[...]
            if res is not None:
                r = r + r_ref[...].astype(F32)
            o_ref[...] = r.astype(out_dtype)

    ins = [a, b]
    in_specs = [a_spec, b_spec]
    if res is not None:
        assert not o_sh
        ins.append(res)
        in_specs.append(pl.BlockSpec((bm, bn), lambda i, j, q: (i, j)))
    return pl.pallas_call(
        body, name=name, grid=(m // bm, n // bn, nk), in_specs=in_specs, out_specs=o_spec, out_shape=o_shape,
        scratch_shapes=[pltpu.VMEM((bm, bn), F32)],
        compiler_params=_params("parallel", "parallel", "arbitrary"))(*ins)


def _rms_fwd(x, g, *, name, out_dtype=BF16, br=256):
    r, c = x.shape
    br = _div_block(r, br, 8)

    def body(x_ref, g_ref, o_ref):
        xv = x_ref[...].astype(F32)
        inv = lax.rsqrt(jnp.mean(xv * xv, axis=-1, keepdims=True) + NORM_EPS)
        o_ref[...] = (xv * inv * g_ref[...]).astype(out_dtype)

    return pl.pallas_call(
        body, name=name, grid=(r // br,),
        in_specs=[pl.BlockSpec((br, c), lambda i: (i, 0)), pl.BlockSpec((1, c), lambda i: (0, 0))],
        out_specs=pl.BlockSpec((br, c), lambda i: (i, 0)), out_shape=jax.ShapeDtypeStruct((r, c), out_dtype),
        compiler_params=_params("parallel"))(x, g)


def _rms_bwd_math(xv, gv, dyv, n):
    inv = lax.rsqrt(jnp.sum(xv * xv, axis=-1, keepdims=True) / n + NORM_EPS)
    xh = xv * inv
    dyg = dyv * gv
    dx = inv * (dyg - xh * (jnp.sum(dyg * xh, axis=-1, keepdims=True) / n))
    return dx, dyv * xh


def _rms_bwd(x, g, dy, *, name, res=None, br=256):
    r, c = x.shape
    br = _div_block(r, br, 8)

    def body(x_ref, g_ref, dy_ref, *rest):
        if res is not None:
            r_ref, dx_ref, dg_ref = rest
        else:
            dx_ref, dg_ref = rest
        dx, dgr = _rms_bwd_math(x_ref[...].astype(F32), g_ref[...], dy_ref[...].astype(F32), c)
        if res is not None:
            dx = dx + r_ref[...]
        dx_ref[...] = dx

        @pl.when(pl.program_id(0) == 0)
        def _():
            dg_ref[...] = jnp.zeros_like(dg_ref)

        dg_ref[...] += jnp.sum(dgr, axis=0, keepdims=True)

    row = pl.BlockSpec((br, c), lambda i: (i, 0))
    vec = pl.BlockSpec((1, c), lambda i: (0, 0))
    ins = [x, g, dy] + ([res] if res is not None else [])
    return pl.pallas_call(
        body, name=name, grid=(r // br,), in_specs=[row, vec, row] + ([row] if res is not None else []),
        out_specs=[row, vec], out_shape=[jax.ShapeDtypeStruct((r, c), F32), jax.ShapeDtypeStruct((1, c), F32)],
        compiler_params=_params("arbitrary"))(*ins)


def _loss(y, target, *, name="loss", br=256):
    r, c = y.shape
    br = _div_block(r, br, 8)

    def body(y_ref, t_ref, l_ref, dy_ref):
        d = y_ref[...] - t_ref[...]
        dy_ref[...] = d * (1.0 / c)

        @pl.when(pl.program_id(0) == 0)
        def _():
            l_ref[...] = jnp.zeros_like(l_ref)

        part = jnp.sum(d * d, axis=0, keepdims=True)
        l_ref[...] += (0.5 / c) * jnp.sum(part, axis=1, keepdims=True) * jnp.ones((1, LANES), F32)

    row = pl.BlockSpec((br, c), lambda i: (i, 0))
    return pl.pallas_call(
        body, name=name, grid=(r // br,), in_specs=[row, row],
        out_specs=[pl.BlockSpec((1, LANES), lambda i: (0, 0)), row],
        out_shape=[jax.ShapeDtypeStruct((1, LANES), F32), jax.ShapeDtypeStruct((r, c), F32)],
        compiler_params=_params("arbitrary"))(y, target)


def _split2(x):
    hi = x.astype(BF16)
    lo = (x - hi.astype(F32)).astype(BF16)
    return hi, lo


def _lane_scan(x, *, suffix):
    rows, n = x.shape
    nb = n // LANES
    a, b = _iota((LANES, LANES), 0), _iota((LANES, LANES), 1)
    tri = ((a > b) if suffix else (a < b)).astype(BF16)
    outs = [None] * nb
    run = jnp.zeros((rows, 1), F32)
    order = range(nb - 1, -1, -1) if suffix else range(nb)
    for blk in order:
        xb = x[:, blk * LANES:(blk + 1) * LANES]
        hi, lo = _split2(xb)
        outs[blk] = _dot(hi, tri, 1, 0) + _dot(lo, tri, 1, 0) + run
        run = run + jnp.sum(xb, axis=-1, keepdims=True)
    return jnp.concatenate(outs, axis=1)


def _softplus(z):
    return jnp.maximum(z, 0.0) + jnp.log(1.0 + jnp.exp(-jnp.abs(z)))


def _head_norm(x, g):
    xv = x.astype(F32)
    inv = lax.rsqrt(jnp.mean(xv * xv, axis=-1, keepdims=True) + NORM_EPS)
    return xv * inv * g


def _attn_weights(kind, qn, kn, scale, qi, bq, bias):
    s = _dot(qn, kn, 1, 1) * scale
    row = qi * bq + _iota(s.shape, 0)
    col = _iota(s.shape, 1)
    if kind == 'sb':
        strict = col < row
        sp = _softplus(s)
        after = _lane_scan(jnp.where(strict, -sp, 0.0), suffix=True)
        w = jnp.where(strict, jnp.exp(s - sp + after), 0.0)
        return w, (strict, s - sp)
    if bias is not None:
        s = s + bias
    s = jnp.where(col <= row, s, NEG_BIG)
    mx = jnp.max(s, axis=-1, keepdims=True)
    e = jnp.exp(s - mx)
    return e, jnp.sum(e, axis=-1, keepdims=True)


def _attn_fwd(kind, q, k, v, *, name, n_heads, dqk, qcol, kcol, vcol, scale, gains=None, cq=None, ck=None, bq=256):
    s_len = q.shape[0]
    bq = _div_block(s_len, bq, 8)
    norm, fox = gains is not None, cq is not None

    def body(*refs):
        refs = list(refs)
        q_ref, k_ref, v_ref = refs[:3]
        rest = refs[3:]
        g_ref = rest.pop(0) if norm else None
        cq_ref, ck_ref = (rest.pop(0), rest.pop(0)) if fox else (None, None)
        o_ref, = rest
        qi = pl.program_id(1)

        def step(n_keys):
            if norm:
                qn = _head_norm(q_ref[...], g_ref[0]).astype(BF16)
                kn = _head_norm(k_ref[0:n_keys, :], g_ref[1]).astype(BF16)
            else:
                qn, kn = q_ref[...].astype(BF16), k_ref[0:n_keys, :].astype(BF16)
            bias = (cq_ref[...] - ck_ref[:, 0:n_keys]) if fox else None
            w, aux = _attn_weights(kind, qn, kn, scale, qi, bq, bias)
            o = _dot(w.astype(BF16), v_ref[0:n_keys, :].astype(BF16), 1, 0)
            if kind != 'sb':
                o = o / aux
            o_ref[...] = o.astype(BF16)

        for qv in range(s_len // bq):
            pl.when(qi == qv)(functools.partial(step, (qv + 1) * bq))

    in_specs = [pl.BlockSpec((bq, dqk), lambda h, i: (i, qcol(h))),
                pl.BlockSpec((s_len, dqk), lambda h, i: (0, kcol(h))),
                pl.BlockSpec((s_len, HEAD_DIM), lambda h, i: (0, vcol(h)))]
    ins = [q, k, v]
    if norm:
        in_specs.append(pl.BlockSpec((2, 1, dqk), lambda h, i: (0, 0, 0)))
        ins.append(gains)
    if fox:
        in_specs += [pl.BlockSpec((None, bq, 1), lambda h, i: (h, i, 0)), pl.BlockSpec((None, 1, s_len), lambda h, i: (h, 0, 0))]
        ins += [cq, ck]
    return pl.pallas_call(
        body, name=name, grid=(n_heads, s_len // bq), in_specs=in_specs,
        out_specs=pl.BlockSpec((bq, HEAD_DIM), lambda h, i: (i, h)),
        out_shape=jax.ShapeDtypeStruct((s_len, n_heads * HEAD_DIM), BF16),
        compiler_params=_params("parallel", "parallel"))(*ins)


def _attn_bwd(kind, q, k, v, o, do, *, name, n_heads, dqk, qcol, kcol, vcol, scale, gains=None, cq=None, ck=None,
              bq=256):
    s_len = q.shape[0]
    bq = _div_block(s_len, bq, 8)
    nq = s_len // bq
    norm, fox = gains is not None, cq is not None

    def body(*refs):
        refs = list(refs)
        q_ref, k_ref, v_ref, o_ref, do_ref = refs[:5]
        rest = refs[5:]
        g_ref = rest.pop(0) if norm else None
        cq_ref, ck_ref = (rest.pop(0), rest.pop(0)) if fox else (None, None)
        dq_ref, dk_ref, dv_ref = rest.pop(0), rest.pop(0), rest.pop(0)
        dg_ref = rest.pop(0) if norm else None
        dcq_ref, dck_ref = (rest.pop(0), rest.pop(0)) if fox else (None, None)
        dk_acc, dv_acc = rest
        h, qi = pl.program_id(0), pl.program_id(1)

        @pl.when(qi == 0)
        def _():
            dk_acc[...] = jnp.zeros_like(dk_acc)
            dv_acc[...] = jnp.zeros_like(dv_acc)
            if fox:
                dck_ref[...] = jnp.zeros_like(dck_ref)

        if norm:
            @pl.when((qi == 0) & (h == 0))
            def _():
                dg_ref[...] = jnp.zeros_like(dg_ref)


        def step(n_keys):
            if norm:
                qn = _head_norm(q_ref[...], g_ref[0]).astype(BF16)
                kn = _head_norm(k_ref[0:n_keys, :], g_ref[1]).astype(BF16)
            else:
                qn, kn = q_ref[...].astype(BF16), k_ref[0:n_keys, :].astype(BF16)
            vb = v_ref[0:n_keys, :].astype(BF16)
            dob = do_ref[...].astype(BF16)
            bias = (cq_ref[...] - ck_ref[:, 0:n_keys]) if fox else None
            w, aux = _attn_weights(kind, qn, kn, scale, qi, bq, bias)
            dw = _dot(dob, vb, 1, 1)
            if kind == 'sb':
                strict, log_sig = aux
                g = dw * w
                cc = _lane_scan(g, suffix=False)
                sig = jnp.exp(log_sig)
                ds = jnp.where(strict, g * (1.0 - sig) - cc * sig, 0.0)
                pw = w
            else:
                pw = w / aux
                delta = jnp.sum(do_ref[...].astype(F32) * o_ref[...].astype(F32), axis=-1, keepdims=True)
                ds = pw * (dw - delta)
                if fox:
                    dcq_ref[...] = jnp.sum(ds, axis=1, keepdims=True)
                    dck_ref[:, 0:n_keys] -= jnp.sum(ds, axis=0, keepdims=True)
            dsb = (ds * scale).astype(BF16)
            dqn = _dot(dsb, kn, 1, 0)
            dk_acc[0:n_keys, :] += _dot(dsb, qn, 0, 0)
            dv_acc[0:n_keys, :] += _dot(pw.astype(BF16), dob, 0, 0)
            if norm:
                dq, dgr = _rms_bwd_math(q_ref[...].astype(F32), g_ref[0], dqn, dqk)
                dg_ref[0] += jnp.sum(dgr, axis=0, keepdims=True)
                dq_ref[...] = dq.astype(BF16)
            else:
                dq_ref[...] = dqn.astype(BF16)

        for qv in range(nq):
            pl.when(qi == qv)(functools.partial(step, (qv + 1) * bq))

        @pl.when(qi == nq - 1)
        def _():
            if norm:
                dk, dgr = _rms_bwd_math(k_ref[...].astype(F32), g_ref[1], dk_acc[...], dqk)
                dg_ref[1] += jnp.sum(dgr, axis=0, keepdims=True)
                dk_ref[...] = dk.astype(BF16)
            else:
                dk_ref[...] = dk_acc[...].astype(BF16)
            dv_ref[...] = dv_acc[...].astype(BF16)

    in_specs = [pl.BlockSpec((bq, dqk), lambda h, i: (i, qcol(h))),
                pl.BlockSpec((s_len, dqk), lambda h, i: (0, kcol(h))),
                pl.BlockSpec((s_len, HEAD_DIM), lambda h, i: (0, vcol(h))),
                pl.BlockSpec((bq, HEAD_DIM), lambda h, i: (i, h)),
                pl.BlockSpec((bq, HEAD_DIM), lambda h, i: (i, h))]
    ins = [q, k, v, o, do]
    out_specs = [pl.BlockSpec((bq, dqk), lambda h, i: (i, h)),
                 pl.BlockSpec((s_len, dqk), lambda h, i: (0, h)),
                 pl.BlockSpec((s_len, HEAD_DIM), lambda h, i: (0, h))]
    out_shape = [jax.ShapeDtypeStruct((s_len, n_heads * dqk), BF16), jax.ShapeDtypeStruct((s_len, n_heads * dqk), BF16),
                 jax.ShapeDtypeStruct((s_len, n_heads * HEAD_DIM), BF16)]
    if norm:
        in_specs.append(pl.BlockSpec((2, 1, dqk), lambda h, i: (0, 0, 0)))
        ins.append(gains)
        out_specs.append(pl.BlockSpec((2, 1, dqk), lambda h, i: (0, 0, 0)))
        out_shape.append(jax.ShapeDtypeStruct((2, 1, dqk), F32))
    if fox:
        in_specs += [pl.BlockSpec((None, bq, 1), lambda h, i: (h, i, 0)), pl.BlockSpec((None, 1, s_len), lambda h, i: (h, 0, 0))]
        ins += [cq, ck]
        out_specs += [pl.BlockSpec((None, bq, 1), lambda h, i: (h, i, 0)), pl.BlockSpec((None, 1, s_len), lambda h, i: (h, 0, 0))]
        out_shape += [jax.ShapeDtypeStruct((n_heads, s_len, 1), F32), jax.ShapeDtypeStruct((n_heads, 1, s_len), F32)]
    return pl.pallas_call(
        body, name=name, grid=(n_heads, nq), in_specs=in_specs, out_specs=out_specs, out_shape=out_shape,
        scratch_shapes=[pltpu.VMEM((s_len, dqk), F32), pltpu.VMEM((s_len, HEAD_DIM), F32)],
        compiler_params=_params("arbitrary", "arbitrary"))(*ins)


def _split3(x):
    hi = x.astype(BF16)
    r1 = x - hi.astype(F32)
    mid = r1.astype(BF16)
    lo = (r1 - mid.astype(F32)).astype(BF16)
    return hi, mid, lo


def _seq_scan(x, *, reverse):
    n = x.shape[0] // LANES
    a, b = _iota((LANES, LANES), 0), _iota((LANES, LANES), 1)
    tri = ((b >= a) if reverse else (b <= a)).astype(BF16)
    outs = [None] * n
    run = jnp.zeros((1, x.shape[1]), F32)
    for blk in (range(n - 1, -1, -1) if reverse else range(n)):
        xb = x[blk * LANES:(blk + 1) * LANES, :]
        hi, mid, lo = _split3(xb)
        outs[blk] = _dot(tri, hi, 1, 0) + _dot(tri, mid, 1, 0) + _dot(tri, lo, 1, 0) + run
        run = run + jnp.sum(xb, axis=0, keepdims=True)
    return jnp.concatenate(outs, axis=0)


def _fgate_fwd(qkvf, b_f, *, fcol, name):
    s_len = qkvf.shape[0]

    def body(f_ref, b_ref, cum_ref):
        z = f_ref[...] + b_ref[...]
        cum_ref[...] = _seq_scan(-_softplus(-z), reverse=False)

    return pl.pallas_call(
        body, name=name, grid=(1,),
        in_specs=[pl.BlockSpec((s_len, LANES), lambda i: (0, fcol)), pl.BlockSpec((1, LANES), lambda i: (0, 0))],
        out_specs=pl.BlockSpec((s_len, LANES), lambda i: (0, 0)), out_shape=jax.ShapeDtypeStruct((s_len, LANES), F32),
        compiler_params=_params("arbitrary"))(qkvf, b_f)


def _fgate_bwd(qkvf, b_f, dcum_a, dcum_b, *, fcol, n_heads, name):
    s_len = qkvf.shape[0]

    def body(f_ref, b_ref, da_ref, db_ref, dz_ref, dbias_ref):
        z = f_ref[...] + b_ref[...]
        dlog = _seq_scan(da_ref[...] + db_ref[...], reverse=True)
        dz = dlog * jnp.exp(-_softplus(z))
        dz = jnp.where(_iota(dz.shape, 1) < n_heads, dz, 0.0)
        dz_ref[...] = dz.astype(BF16)
        dbias_ref[...] = jnp.sum(dz, axis=0, keepdims=True)

    full = pl.BlockSpec((s_len, LANES), lambda i: (0, 0))
    vec = pl.BlockSpec((1, LANES), lambda i: (0, 0))
    return pl.pallas_call(
        body, name=name, grid=(1,),
        in_specs=[pl.BlockSpec((s_len, LANES), lambda i: (0, fcol)), vec, full, full],
        out_specs=[full, vec], out_shape=[jax.ShapeDtypeStruct((s_len, LANES), BF16), jax.ShapeDtypeStruct((1, LANES), F32)],
        compiler_params=_params("arbitrary"))(qkvf, b_f, dcum_a, dcum_b)


def _rope_swap(x):
    half = MLA_ROPE // 2
    lane = _iota(x.shape, 1)
    sw = jnp.where(lane < half, pltpu.roll(x, LANES - half, axis=1), pltpu.roll(x, half, axis=1))
    return jnp.where(lane < MLA_ROPE, sw, 0.0)


def _mla_prep_fwd(qp, kv, c, gq, gk, cos_t, sin_t, *, n_heads, name, bs=512):
    s_len = qp.shape[0]
    bs = _div_block(s_len, bs, 8)
    krope_col = (MLA_Q_RANK + MLA_KV_RANK) // LANES

    def body(qn_ref, qr_ref, kn_ref, kr_ref, gq_ref, gk_ref, cos_ref, sin_ref, qc_ref, kc_ref):
        cos_v, sin_v = cos_ref[...], sin_ref[...]

        def rope(x, g):
            xv = x.astype(F32)
            inv = lax.rsqrt(jnp.sum(xv * xv, axis=-1, keepdims=True) / MLA_ROPE + NORM_EPS)
            y = xv * inv * g
            return y * cos_v + _rope_swap(y) * sin_v

        qc_ref[:, :LANES] = _head_norm(qn_ref[...], gq_ref[0]).astype(BF16)
        qc_ref[:, LANES:] = rope(qr_ref[...], gq_ref[1]).astype(BF16)
        kc_ref[:, :LANES] = _head_norm(kn_ref[...], gk_ref[0]).astype(BF16)
        kc_ref[:, LANES:] = rope(kr_ref[...], gk_ref[1]).astype(BF16)

    blk = lambda f: pl.BlockSpec((bs, LANES), f)
    gspec = pl.BlockSpec((2, 1, LANES), lambda i, h: (0, 0, 0))
    tspec = pl.BlockSpec((bs, LANES), lambda i, h: (i, 0))
    ospec = pl.BlockSpec((bs, 2 * LANES), lambda i, h: (i, h))
    oshape = jax.ShapeDtypeStruct((s_len, n_heads * 2 * LANES), BF16)
    return pl.pallas_call(
        body, name=name, grid=(s_len // bs, n_heads),
        in_specs=[blk(lambda i, h: (i, h)), blk(lambda i, h: (i, n_heads + h)), blk(lambda i, h: (i, 2 * h)),
                  blk(lambda i, h: (i, krope_col)), gspec, gspec, tspec, tspec],
        out_specs=[ospec, ospec], out_shape=[oshape, oshape],
        compiler_params=_params("parallel", "parallel"))(qp, qp, kv, c, gq, gk, cos_t, sin_t)


def _mla_prep_bwd(qp, kv, c, gq, gk, cos_t, sin_t, dqc, dkc, dv, *, n_heads, name, bs=512):
    s_len = qp.shape[0]
    bs = _div_block(s_len, bs, 8)
    krope_col = (MLA_Q_RANK + MLA_KV_RANK) // LANES

    def body(qn_ref, qr_ref, kn_ref, kr_ref, gq_ref, gk_ref, cos_ref, sin_ref, dqc_ref, dkc_ref, dv_ref,
             dqn_ref, dqr_ref, dkv_ref, dkr_ref, dgq_ref, dgk_ref):
        i, h = pl.program_id(0), pl.program_id(1)
        cos_v, sin_v = cos_ref[...], sin_ref[...]

        @pl.when((i == 0) & (h == 0))
        def _():
            dgq_ref[...] = jnp.zeros_like(dgq_ref)
            dgk_ref[...] = jnp.zeros_like(dgk_ref)

        @pl.when(h == 0)
        def _():
            dkr_ref[...] = jnp.zeros_like(dkr_ref)

        def unrope(dy):
            dy = dy.astype(F32)
            return dy * cos_v + _rope_swap(dy * sin_v)

        dqn, dg = _rms_bwd_math(qn_ref[...].astype(F32), gq_ref[0], dqc_ref[:, :LANES].astype(F32), MLA_NOPE)
        dgq_ref[0] += jnp.sum(dg, axis=0, keepdims=True)
        dqn_ref[...] = dqn.astype(BF16)
        dqr, dg = _rms_bwd_math(qr_ref[...].astype(F32), gq_ref[1], unrope(dqc_ref[:, LANES:]), MLA_ROPE)
        dgq_ref[1] += jnp.sum(dg, axis=0, keepdims=True)
        dqr_ref[...] = dqr.astype(BF16)
        dkn, dg = _rms_bwd_math(kn_ref[...].astype(F32), gk_ref[0], dkc_ref[:, :LANES].astype(F32), MLA_NOPE)
        dgk_ref[0] += jnp.sum(dg, axis=0, keepdims=True)
        dkv_ref[:, :LANES] = dkn.astype(BF16)
        dkv_ref[:, LANES:] = dv_ref[...]
        dkr, dg = _rms_bwd_math(kr_ref[...].astype(F32), gk_ref[1], unrope(dkc_ref[:, LANES:]), MLA_ROPE)
        dgk_ref[1] += jnp.sum(dg, axis=0, keepdims=True)
        dkr_ref[...] += dkr

    blk = lambda f: pl.BlockSpec((bs, LANES), f)
    gspec = pl.BlockSpec((2, 1, LANES), lambda i, h: (0, 0, 0))
    tspec = pl.BlockSpec((bs, LANES), lambda i, h: (i, 0))
    cat = pl.BlockSpec((bs, 2 * LANES), lambda i, h: (i, h))
    head = blk(lambda i, h: (i, h))
    hshape = jax.ShapeDtypeStruct((s_len, n_heads * LANES), BF16)
    gshape = jax.ShapeDtypeStruct((2, 1, LANES), F32)
    return pl.pallas_call(
        body, name=name, grid=(s_len // bs, n_heads),
        in_specs=[head, blk(lambda i, h: (i, n_heads + h)), blk(lambda i, h: (i, 2 * h)),
                  blk(lambda i, h: (i, krope_col)), gspec, gspec, tspec, tspec, cat, cat, head],
        out_specs=[head, head, cat, tspec, gspec, gspec],
        out_shape=[hshape, hshape, jax.ShapeDtypeStruct((s_len, n_heads * 2 * LANES), BF16),
                   jax.ShapeDtypeStruct((s_len, LANES), F32), gshape, gshape],
        compiler_params=_params("arbitrary", "arbitrary"))(qp, qp, kv, c, gq, gk, cos_t, sin_t, dqc, dkc, dv)


def _mla_latent_fwd(c, ga, *, name, br=256):
    s_len = c.shape[0]
    br = _div_block(s_len, br, 8)

    def body(c_ref, g_ref, o_ref):
        for part in range(2):
            sl = slice(part * MLA_Q_RANK, (part + 1) * MLA_Q_RANK)
            o_ref[:, sl] = _head_norm(c_ref[:, sl], g_ref[:, sl]).astype(BF16)

    w = MLA_Q_RANK + MLA_KV_RANK
    return pl.pallas_call(
        body, name=name, grid=(s_len // br,),
        in_specs=[pl.BlockSpec((br, w), lambda i: (i, 0)), pl.BlockSpec((1, w), lambda i: (0, 0))],
        out_specs=pl.BlockSpec((br, w), lambda i: (i, 0)), out_shape=jax.ShapeDtypeStruct((s_len, w), BF16),
        compiler_params=_params("parallel"))(c, ga)


def _mla_latent_bwd(c, ga, dcn_q, dcn_kv, dk_rope, *, name, br=256):
    s_len, cw = c.shape
    br = _div_block(s_len, br, 8)
    w = MLA_Q_RANK + MLA_KV_RANK

    def body(c_ref, g_ref, dq_ref, dkv_ref, dkr_ref, dc_ref, dg_ref):
        @pl.when(pl.program_id(0) == 0)
        def _():
            dg_ref[...] = jnp.zeros_like(dg_ref)

        for part, d_ref in enumerate((dq_ref, dkv_ref)):
            sl = slice(part * MLA_Q_RANK, (part + 1) * MLA_Q_RANK)
            dx, dg = _rms_bwd_math(c_ref[:, sl].astype(F32), g_ref[:, sl], d_ref[...].astype(F32), MLA_Q_RANK)
            dc_ref[:, sl] = dx.astype(BF16)
            dg_ref[:, sl] += jnp.sum(dg, axis=0, keepdims=True)
        dc_ref[:, w:] = dkr_ref[...].astype(BF16)

    return pl.pallas_call(
        body, name=name, grid=(s_len // br,),
        in_specs=[pl.BlockSpec((br, w), lambda i: (i, 0)), pl.BlockSpec((1, w), lambda i: (0, 0)),
                  pl.BlockSpec((br, MLA_Q_RANK), lambda i: (i, 0)), pl.BlockSpec((br, MLA_KV_RANK), lambda i: (i, 0)),
                  pl.BlockSpec((br, LANES), lambda i: (i, 0))],
        out_specs=[pl.BlockSpec((br, cw), lambda i: (i, 0)), pl.BlockSpec((1, w), lambda i: (0, 0))],
        out_shape=[jax.ShapeDtypeStruct((s_len, cw), BF16), jax.ShapeDtypeStruct((1, w), F32)],
        compiler_params=_params("arbitrary"))(c, ga, dcn_q, dcn_kv, dk_rope)


_GELU_C = 0.7978845608028654


def _gelu(x):
    return 0.5 * x * (1.0 + jnp.tanh(_GELU_C * (x + 0.044715 * x * x * x)))


def _gelu_grad(x):
    t = jnp.tanh(_GELU_C * (x + 0.044715 * x * x * x))
    return 0.5 * (1.0 + t) + 0.5 * x * (1.0 - t * t) * _GELU_C * (1.0 + 3 * 0.044715 * x * x)


def _sgu_act_fwd(uv, vg, *, name, br=256):
    s_len, w2 = uv.shape
    w = w2 // 2
    br = _div_block(s_len, br, 8)

    def body(uv_ref, g_ref, u_ref, v_ref):
        u_ref[...] = _gelu(uv_ref[:, :w])
        v_ref[...] = _head_norm(_gelu(uv_ref[:, w:]), g_ref[...]).astype(BF16)

    row = lambda c: pl.BlockSpec((br, c), lambda i: (i, 0))
    return pl.pallas_call(
        body, name=name, grid=(s_len // br,), in_specs=[row(w2), pl.BlockSpec((1, w), lambda i: (0, 0))],
        out_specs=[row(w), row(w)], out_shape=[jax.ShapeDtypeStruct((s_len, w), F32), jax.ShapeDtypeStruct((s_len, w), BF16)],
        compiler_params=_params("parallel"))(uv, vg)


def _sgu_act_bwd(uv, vg, du, dvn, *, name, br=256):
    s_len, w2 = uv.shape
    w = w2 // 2
    br = _div_block(s_len, br, 8)

    def body(uv_ref, g_ref, du_ref, dvn_ref, duv_ref, dg_ref):
        @pl.when(pl.program_id(0) == 0)
        def _():
            dg_ref[...] = jnp.zeros_like(dg_ref)

        up, vp = uv_ref[:, :w], uv_ref[:, w:]
        duv_ref[:, :w] = (du_ref[...] * _gelu_grad(up)).astype(BF16)
        dva, dg = _rms_bwd_math(_gelu(vp), g_ref[...], dvn_ref[...], w)
        dg_ref[...] += jnp.sum(dg, axis=0, keepdims=True)
        duv_ref[:, w:] = (dva * _gelu_grad(vp)).astype(BF16)

    row = lambda c: pl.BlockSpec((br, c), lambda i: (i, 0))
    vec = pl.BlockSpec((1, w), lambda i: (0, 0))
    return pl.pallas_call(
        body, name=name, grid=(s_len // br,), in_specs=[row(w2), vec, row(w), row(w)], out_specs=[row(w2), vec],
        out_shape=[jax.ShapeDtypeStruct((s_len, w2), BF16), jax.ShapeDtypeStruct((1, w), F32)],
        compiler_params=_params("arbitrary"))(uv, vg, du, dvn)


def _tril_weights(ws_ref):
    t, s = _iota((SGU_CHUNK, SGU_CHUNK), 0), _iota((SGU_CHUNK, SGU_CHUNK), 1)
    keep = s <= t
    return jnp.where(keep, ws_ref[...], 0.0), keep


def _sgu_mix_fwd(u, vn, w_s, b_s, *, name):
    s_len, w = u.shape
    nc = s_len // SGU_CHUNK

    def body(u_ref, v_ref, ws_ref, b_ref, o_ref):
        wm = _tril_weights(ws_ref)[0].astype(BF16)
        for n in range(nc):
            rows = slice(n * SGU_CHUNK, (n + 1) * SGU_CHUNK)
            mixed = _dot(wm, v_ref[rows, :], 1, 0) + b_ref[...]
            o_ref[rows, :] = (u_ref[rows, :] * mixed).astype(BF16)

    col = pl.BlockSpec((s_len, LANES), lambda g: (0, g))
    return pl.pallas_call(
        body, name=name, grid=(w // LANES,),
        in_specs=[col, col, pl.BlockSpec((None, SGU_CHUNK, SGU_CHUNK), lambda g: (g, 0, 0)),
                  pl.BlockSpec((None, SGU_CHUNK, 1), lambda g: (g, 0, 0))],
        out_specs=col, out_shape=jax.ShapeDtypeStruct((s_len, w), BF16),
        compiler_params=_params("parallel"))(u, vn, w_s, b_s)


def _sgu_mix_bwd(u, vn, w_s, b_s, dgated, *, name):
    s_len, w = u.shape
    nc = s_len // SGU_CHUNK

    def body(u_ref, v_ref, ws_ref, b_ref, dg_ref, du_ref, dv_ref, dws_ref, dbs_ref):
        wf, keep = _tril_weights(ws_ref)
        wm = wf.astype(BF16)
        wmt = wf.T.astype(BF16)
        dws = jnp.zeros((SGU_CHUNK, SGU_CHUNK), F32)
        dbs = jnp.zeros((SGU_CHUNK, 1), F32)
        for n in range(nc):
            rows = slice(n * SGU_CHUNK, (n + 1) * SGU_CHUNK)
            vb = v_ref[rows, :]
            dgv = dg_ref[rows, :].astype(F32)
            mixed = _dot(wm, vb, 1, 0) + b_ref[...]
            du_ref[rows, :] = dgv * mixed
            dm = dgv * u_ref[rows, :]
            dmb = dm.astype(BF16)
            dws = dws + _dot(dmb, vb, 1, 1)
            dbs = dbs + jnp.sum(dm, axis=1, keepdims=True)
            dv_ref[rows, :] = _dot(wmt, dmb, 1, 0)
        dws_ref[...] = jnp.where(keep, dws, 0.0)
        dbs_ref[...] = dbs

    col = pl.BlockSpec((s_len, LANES), lambda g: (0, g))
    wspec = pl.BlockSpec((None, SGU_CHUNK, SGU_CHUNK), lambda g: (g, 0, 0))
    bspec = pl.BlockSpec((None, SGU_CHUNK, 1), lambda g: (g, 0, 0))
    return pl.pallas_call(
        body, name=name, grid=(w // LANES,), in_specs=[col, col, wspec, bspec, col],
        out_specs=[col, col, wspec, bspec],
        out_shape=[jax.ShapeDtypeStruct((s_len, w), F32), jax.ShapeDtypeStruct((s_len, w), F32),
                   jax.ShapeDtypeStruct(w_s.shape, F32), jax.ShapeDtypeStruct(b_s.shape, F32)],
        compiler_params=_params("parallel"))(u, vn, w_s, b_s, dgated)


def _shift_down(x, k):
    if k == 0:
        return x
    return jnp.where(_iota(x.shape, 0) >= k, pltpu.roll(x, k, axis=0), 0.0)


def _shift_up(x, k):
    if k == 0:
        return x
    n = x.shape[0]
    return jnp.where(_iota(x.shape, 0) < n - k, pltpu.roll(x, n - k, axis=0), 0.0)


def _conv(u, w_ref, b_ref):
    return b_ref[...] + w_ref[0:1, :] * _shift_down(u, 2) + w_ref[1:2, :] * _shift_down(u, 1) + w_ref[2:3, :] * u


def _sigmoid(x):
    return 1.0 / (1.0 + jnp.exp(-x))


def _glu_fwd(up, cw, cb, *, name, bc=256):
    s_len, f2 = up.shape
    f = f2 // 2
    bc = _div_block(f, bc)
    nf = f // bc

    def body(ug_ref, uv_ref, wg_ref, wv_ref, bg_ref, bv_ref, o_ref):
        yg = _conv(ug_ref[...], wg_ref, bg_ref)
        yv = _conv(uv_ref[...], wv_ref, bv_ref)
        o_ref[...] = (yg * _sigmoid(yg) * yv).astype(BF16)

    big = lambda off: pl.BlockSpec((s_len, bc), lambda j: (0, j + off))
    wsp = lambda off: pl.BlockSpec((3, bc), lambda j: (0, j + off))
    bsp = lambda off: pl.BlockSpec((1, bc), lambda j: (0, j + off))
    return pl.pallas_call(
        body, name=name, grid=(nf,), in_specs=[big(0), big(nf), wsp(0), wsp(nf), bsp(0), bsp(nf)],
        out_specs=pl.BlockSpec((s_len, bc), lambda j: (0, j)), out_shape=jax.ShapeDtypeStruct((s_len, f), BF16),
        compiler_params=_params("parallel"))(up, up, cw, cw, cb, cb)


def _glu_bwd(up, cw, cb, dact, *, name, bc=256):
    s_len, f2 = up.shape
    f = f2 // 2
    bc = _div_block(f, bc)
    nf = f // bc

    def body(ug_ref, uv_ref, wg_ref, wv_ref, bg_ref, bv_ref, da_ref, du_ref, dw_ref, db_ref):
        ug, uv = ug_ref[...], uv_ref[...]
        yg = _conv(ug, wg_ref, bg_ref)
        yv = _conv(uv, wv_ref, bv_ref)
        da = da_ref[...].astype(F32)
        sg = _sigmoid(yg)
        planes = ((da * yv * (sg * (1.0 + yg * (1.0 - sg))), ug, wg_ref), (da * (yg * sg), uv, wv_ref))
        for plane, (dy, u, w_ref) in enumerate(planes):
            db_ref[plane] = jnp.sum(dy, axis=0, keepdims=True)
            dw_ref[plane, 0:1, :] = jnp.sum(dy * _shift_down(u, 2), axis=0, keepdims=True)
            dw_ref[plane, 1:2, :] = jnp.sum(dy * _shift_down(u, 1), axis=0, keepdims=True)
            dw_ref[plane, 2:3, :] = jnp.sum(dy * u, axis=0, keepdims=True)
            du = w_ref[2:3, :] * dy + w_ref[1:2, :] * _shift_up(dy, 1) + w_ref[0:1, :] * _shift_up(dy, 2)
            du_ref[plane] = du.astype(BF16)

    big = lambda off: pl.BlockSpec((s_len, bc), lambda j: (0, j + off))
    wsp = lambda off: pl.BlockSpec((3, bc), lambda j: (0, j + off))
    bsp = lambda off: pl.BlockSpec((1, bc), lambda j: (0, j + off))
    planes = lambda r: pl.BlockSpec((2, r, bc), lambda j: (0, 0, j))
    return pl.pallas_call(
        body, name=name, grid=(nf,),
        in_specs=[big(0), big(nf), wsp(0), wsp(nf), bsp(0), bsp(nf), pl.BlockSpec((s_len, bc), lambda j: (0, j))],
        out_specs=[planes(s_len), planes(3), planes(1)],
        out_shape=[jax.ShapeDtypeStruct((2, s_len, f), BF16), jax.ShapeDtypeStruct((2, 3, f), F32),
                   jax.ShapeDtypeStruct((2, 1, f), F32)],
        compiler_params=_params("parallel"))(up, up, cw, cw, cb, cb, dact)


def _as2d(a):
    return a.reshape(-1, a.shape[-1]) if a.ndim >= 2 else a.reshape(1, -1)


def _adamw(w, g, m, v, *, name, target_bytes=1 << 20):
    shape = w.shape
    w2, m2, v2 = _as2d(w), _as2d(m), _as2d(v)
    g2 = g.reshape(w2.shape)
    r, c = w2.shape
    br = r if r * c * 4 <= target_bytes else _div_block(r, max(8, target_bytes // (4 * c) // 8 * 8), 8)
    c1 = 1.0 - ADAM_B1 ** ADAM_STEP
    c2 = 1.0 - ADAM_B2 ** ADAM_STEP

    def body(w_ref, g_ref, m_ref, v_ref, d_ref, nm_ref, nv_ref):
        gv = g_ref[...]
        nm = ADAM_B1 * m_ref[...] + (1.0 - ADAM_B1) * gv
        nv = ADAM_B2 * v_ref[...] + (1.0 - ADAM_B2) * (gv * gv)
        nm_ref[...] = nm
        nv_ref[...] = nv
        d_ref[...] = -ADAM_LR * ((nm / c1) / (jnp.sqrt(nv / c2) + ADAM_EPS) + ADAM_WD * w_ref[...])

    spec = pl.BlockSpec((br, c), lambda i: (i, 0))
    sds = jax.ShapeDtypeStruct((r, c), F32)
    d, nm, nv = pl.pallas_call(
        body, name=name, grid=(r // br,), in_specs=[spec] * 4, out_specs=[spec] * 3, out_shape=[sds] * 3,
        compiler_params=_params("parallel"))(w2, g2, m2, v2)
    return d.reshape(shape), nm.reshape(shape), nv.reshape(shape)


def _add_halves(g, recv, place, *, name, target_bytes=1 << 20):
    _, _, r, c = g.shape
    br = _div_block(r, max(16, target_bytes // (2 * c) // 16 * 16), 16)

    def body(x_ref, y_ref, c_ref, g_ref, r_ref, o_ref):
        o_ref[...] = (g_ref[...].astype(F32) + r_ref[...].astype(F32)).astype(BF16)

    return pl.pallas_call(
        body, name=name,
        grid_spec=pltpu.PrefetchScalarGridSpec(
            num_scalar_prefetch=3, grid=(N_CHIPS, r // br),
            in_specs=[pl.BlockSpec((None, None, br, c), lambda s, i, xr, yr, cr: (s, cr[0], i, 0)),
                      pl.BlockSpec((None, br, c), lambda s, i, xr, yr, cr: (s, i, 0))],
            out_specs=pl.BlockSpec((None, br, c), lambda s, i, xr, yr, cr: (s, i, 0))),
        out_shape=jax.ShapeDtypeStruct((N_CHIPS, r, c), BF16),
        compiler_params=_params("parallel", "parallel"))(*place, g, recv)


def _sum_chips(p, landed, place, *, name, target_bytes=1 << 20):
    _, r, c = p.shape
    br = _div_block(r, max(16, target_bytes // (4 * c) // 16 * 16), 16)

    def body(x_ref, y_ref, c_ref, p_ref, l1_ref, l2_ref, l3_ref, o_ref):
        o_ref[...] = ((p_ref[...].astype(F32) + l1_ref[...].astype(F32)) + l2_ref[...].astype(F32)) + l3_ref[...].astype(F32)

    slot = lambda k: pl.BlockSpec((None, br, c), lambda i, xr, yr, cr: ((2 * xr[0] + yr[0] + k) % N_CHIPS, i, 0))
    return pl.pallas_call(
        body, name=name,
        grid_spec=pltpu.PrefetchScalarGridSpec(
            num_scalar_prefetch=3, grid=(r // br,), in_specs=[slot(0), slot(1), slot(2), slot(3)],
            out_specs=pl.BlockSpec((None, br, c), lambda i, xr, yr, cr: (cr[0], i, 0))),
        out_shape=jax.ShapeDtypeStruct((2, r, c), F32),
        compiler_params=_params("parallel"))(*place, p, landed, landed, landed)


def _place_shard(w, place, *, dtype, name, target_bytes=1 << 20):
    r, c = w.shape
    hr = r // 2
    mult = 16 if dtype == BF16 else 8
    br = _div_block(hr, max(mult, target_bytes // (4 * c) // mult * mult), mult)
    nb = hr // br

    def body(x_ref, y_ref, c_ref, w_ref, o_ref):
        o_ref[...] = w_ref[...].astype(dtype)

    return pl.pallas_call(
        body, name=name,
        grid_spec=pltpu.PrefetchScalarGridSpec(
            num_scalar_prefetch=3, grid=(2, nb),
            in_specs=[pl.BlockSpec((br, c), lambda h, i, xr, yr, cr: (h * nb + i, 0))],
            out_specs=pl.BlockSpec((None, None, br, c), lambda h, i, xr, yr, cr: (2 * xr[0] + yr[0], h, i, 0))),
        out_shape=jax.ShapeDtypeStruct((N_CHIPS, 2, hr, c), dtype),
        compiler_params=_params("parallel", "parallel"))(*place, w)


def _sum_devices(x, *, name):
    n, r, c = x.shape
    br = _div_block(r, 512, 8)

    def body(x_ref, o_ref):
        acc = x_ref[0]
        for s in range(1, n):
            acc = acc + x_ref[s]
        o_ref[...] = acc

    return pl.pallas_call(
        body, name=name, grid=(r // br,), in_specs=[pl.BlockSpec((n, br, c), lambda i: (0, i, 0))],
        out_specs=pl.BlockSpec((br, c), lambda i: (i, 0)), out_shape=jax.ShapeDtypeStruct((r, c), F32),
        compiler_params=_params("parallel"))(x)


_ANY = pl.BlockSpec(memory_space=pl.ANY)


def _place():
    x, y, c = lax.axis_index("x"), lax.axis_index("y"), lax.axis_index("c")
    other_chips = [(1 - x, y), (x, 1 - y), (1 - x, 1 - y)]
    return x, y, c, other_chips


_HBM = pl.BlockSpec(memory_space=pltpu.HBM)
_SEM = pl.BlockSpec(memory_space=pltpu.SEMAPHORE)
_EFFECT = pltpu.SideEffectType.DATAFLOW_SIDE_EFFECTING


def _in_hbm(a):
    return pltpu.with_memory_space_constraint(a, pltpu.HBM)


def _token_spec():
    return pl.BlockSpec(memory_space=pltpu.VMEM), jax.ShapeDtypeStruct((8, LANES), F32)


def _gather_ici_start(bufs, after, *, name):
    n = len(bufs)

    def body(*refs):
        b_refs = refs[:n]
        send_sems, recv_sems = refs[n + 1], refs[n + 2]
        token = refs[-1]
        x, y, c, chips = _place()
        me = 2 * x + y
        for i in range(n):
            for j, (px, py) in enumerate(chips):
                pltpu.make_async_remote_copy(src_ref=b_refs[i].at[me, c], dst_ref=b_refs[i].at[me, c],
                                             send_sem=send_sems.at[3 * i + j], recv_sem=recv_sems.at[3 * i + j],
                                             device_id=(px, py, c), device_id_type=MESH).start()
        token[...] = jnp.zeros_like(token)

    tspec, tshape = _token_spec()
    outs = pl.pallas_call(
        body, name=name, in_specs=[_HBM] * n + [_ANY], out_specs=(_SEM, _SEM, *[_HBM] * n, tspec),
        out_shape=(pltpu.SemaphoreType.DMA((3 * n,)), pltpu.SemaphoreType.DMA((3 * n,)),
                   *[pltpu.HBM(a.shape, a.dtype) for a in bufs], tshape),
        input_output_aliases={i: 2 + i for i in range(n)},
        compiler_params=pltpu.CompilerParams(has_side_effects=_EFFECT),
    )(*[_in_hbm(a) for a in bufs], after)
    return outs[0], outs[1], list(outs[2:2 + n]), outs[-1]


def _gather_ici_wait(send_sems, recv_sems, bufs, after, *, name):
    n = len(bufs)

    def body(*refs):
        b_refs = refs[:n]
        send_sems, recv_sems = refs[n], refs[n + 1]
        x, y, c, chips = _place()
        me = 2 * x + y
        for i in range(n):
            for j, (px, py) in enumerate(chips):
                cp = pltpu.make_async_remote_copy(src_ref=b_refs[i].at[me, c], dst_ref=b_refs[i].at[2 * px + py, c],
                                                  send_sem=send_sems.at[3 * i + j], recv_sem=recv_sems.at[3 * i + j],
                                                  device_id=(px, py, c), device_id_type=MESH)
                cp.wait_send()
                cp.wait_recv()

    outs = pl.pallas_call(
        body, name=name, in_specs=[_HBM] * n + [_SEM, _SEM, _ANY], out_specs=[_HBM] * n,
        out_shape=[pltpu.HBM(a.shape, a.dtype) for a in bufs], input_output_aliases={i: i for i in range(n)},
        compiler_params=pltpu.CompilerParams(has_side_effects=_EFFECT),
    )(*bufs, send_sems, recv_sems, after)
    return list(outs)


def _gather_d2d(bufs, *, name):
    n = len(bufs)

    def body(*refs):
        b_refs = refs[n:2 * n]
        send_sems, recv_sems = refs[2 * n:]
        x, y, c, chips = _place()
        sends = []
        for i in range(n):
            for j, (px, py) in enumerate(chips):
                mine = b_refs[i].at[2 * px + py, c]
                cp = pltpu.make_async_remote_copy(src_ref=mine, dst_ref=mine, send_sem=send_sems.at[3 * i + j],
                                                  recv_sem=recv_sems.at[3 * i + j], device_id=(x, y, 1 - c),
                                                  device_id_type=MESH)
                cp.start()
                sends.append((cp, i, j, px, py))
        for cp, i, j, px, py in sends:
            theirs = b_refs[i].at[2 * px + py, 1 - c]
            pltpu.make_async_remote_copy(src_ref=theirs, dst_ref=theirs, send_sem=send_sems.at[3 * i + j],
                                         recv_sem=recv_sems.at[3 * i + j], device_id=(x, y, 1 - c),
                                         device_id_type=MESH).wait_recv()
            cp.wait_send()

    return pl.pallas_call(
        body, name=name, in_specs=[_ANY] * n, out_specs=[_ANY] * n, input_output_aliases={i: i for i in range(n)},
        out_shape=[jax.ShapeDtypeStruct(a.shape, a.dtype) for a in bufs],
        scratch_shapes=[pltpu.SemaphoreType.DMA((3 * n,)), pltpu.SemaphoreType.DMA((3 * n,))],
    )(*bufs)


def _sibling_halves(gs, *, name):
    n = len(gs)

    def body(*refs):
        g_refs, o_refs = refs[:n], refs[n:2 * n]
        send_sems, recv_sems = refs[2 * n:]
        x, y, c, _ = _place()
        copies = []
        for i in range(n):
            for s in range(N_CHIPS):
                k = i * N_CHIPS + s
                cp = pltpu.make_async_remote_copy(src_ref=g_refs[i].at[s, 1 - c], dst_ref=o_refs[i].at[s],
                                                  send_sem=send_sems.at[k], recv_sem=recv_sems.at[k],
                                                  device_id=(x, y, 1 - c), device_id_type=MESH)
                cp.start()
                copies.append(cp)
        for cp in copies:
            cp.wait()

    return pl.pallas_call(
        body, name=name, in_specs=[_ANY] * n, out_specs=[_ANY] * n,
        out_shape=[jax.ShapeDtypeStruct((N_CHIPS,) + g.shape[2:], g.dtype) for g in gs],
        scratch_shapes=[pltpu.SemaphoreType.DMA((n * N_CHIPS,)), pltpu.SemaphoreType.DMA((n * N_CHIPS,))],
    )(*gs)


def _chip_scatter_start(ps, after, *, name):
    n = len(ps)

    def body(*refs):
        p_refs, l_refs = refs[:n], refs[n:2 * n]
        send_sems, recv_sems = refs[2 * n + 1], refs[2 * n + 2]
        token = refs[-1]
        x, y, c, chips = _place()
        me = 2 * x + y
        for i in range(n):
            for j, (px, py) in enumerate(chips):
                pltpu.make_async_remote_copy(src_ref=p_refs[i].at[2 * px + py], dst_ref=l_refs[i].at[me],
                                             send_sem=send_sems.at[3 * i + j], recv_sem=recv_sems.at[3 * i + j],
                                             device_id=(px, py, c), device_id_type=MESH).start()
        token[...] = jnp.zeros_like(token)

    tspec, tshape = _token_spec()
    lands = [_in_hbm(lax.empty(p.shape, p.dtype)) for p in ps]
    outs = pl.pallas_call(
        body, name=name, in_specs=[_HBM] * (2 * n) + [_ANY], out_specs=(_SEM, _SEM, *[_HBM] * (2 * n), tspec),
        out_shape=(pltpu.SemaphoreType.DMA((3 * n,)), pltpu.SemaphoreType.DMA((3 * n,)),
                   *[pltpu.HBM(p.shape, p.dtype) for p in ps], *[pltpu.HBM(p.shape, p.dtype) for p in ps], tshape),
        input_output_aliases={i: 2 + i for i in range(2 * n)},
        compiler_params=pltpu.CompilerParams(has_side_effects=_EFFECT),
    )(*[_in_hbm(p) for p in ps], *lands, after)
    return outs[0], outs[1], list(outs[2:2 + n]), list(outs[2 + n:2 + 2 * n]), outs[-1]


def _chip_scatter_wait(send_sems, recv_sems, ps, lands, after, *, name):
    n = len(ps)

    def body(*refs):
        p_refs, l_refs = refs[:n], refs[n:2 * n]
        send_sems, recv_sems = refs[2 * n], refs[2 * n + 1]
        x, y, c, chips = _place()
        for i in range(n):
            for j, (px, py) in enumerate(chips):
                cp = pltpu.make_async_remote_copy(src_ref=p_refs[i].at[2 * px + py], dst_ref=l_refs[i].at[2 * px + py],
                                                  send_sem=send_sems.at[3 * i + j], recv_sem=recv_sems.at[3 * i + j],
                                                  device_id=(px, py, c), device_id_type=MESH)
                cp.wait_send()
                cp.wait_recv()

    outs = pl.pallas_call(
        body, name=name, in_specs=[_HBM] * (2 * n) + [_SEM, _SEM, _ANY], out_specs=[_HBM] * (2 * n),
        out_shape=[pltpu.HBM(p.shape, p.dtype) for p in ps] * 2, input_output_aliases={i: i for i in range(2 * n)},
        compiler_params=pltpu.CompilerParams(has_side_effects=_EFFECT),
    )(*ps, *lands, send_sems, recv_sems, after)
    return list(outs[:n]), list(outs[n:])


def _sibling_share(bufs, *, name):
    n = len(bufs)

    def body(*refs):
        b_refs = refs[n:2 * n]
        send_sems, recv_sems = refs[2 * n:]
        x, y, c, _ = _place()
        copies = []
        for i in range(n):
            cp = pltpu.make_async_remote_copy(src_ref=b_refs[i].at[c], dst_ref=b_refs[i].at[c], send_sem=send_sems.at[i],
                                              recv_sem=recv_sems.at[i], device_id=(x, y, 1 - c), device_id_type=MESH)
            cp.start()
            copies.append((cp, i))
        for cp, i in copies:
            theirs = b_refs[i].at[1 - c]
            pltpu.make_async_remote_copy(src_ref=theirs, dst_ref=theirs, send_sem=send_sems.at[i],
                                         recv_sem=recv_sems.at[i], device_id=(x, y, 1 - c),
                                         device_id_type=MESH).wait_recv()
            cp.wait_send()

    return pl.pallas_call(
        body, name=name, in_specs=[_ANY] * n, out_specs=[_ANY] * n, input_output_aliases={i: i for i in range(n)},
        out_shape=[jax.ShapeDtypeStruct(a.shape, a.dtype) for a in bufs],
        scratch_shapes=[pltpu.SemaphoreType.DMA((n,)), pltpu.SemaphoreType.DMA((n,))],
    )(*bufs)


def _broadcast_all(v, *, name):
    def body(v_ref, o_ref, send_sems, recv_sems, local_sem):
        x, y, c, _ = _place()
        me = 4 * x + 2 * y + c
        loc = pltpu.make_async_copy(v_ref, o_ref.at[me], local_sem)
        loc.start()
        copies = []
        for k in range(1, 8):
            dx, dy, dc = (k >> 2) & 1, (k >> 1) & 1, k & 1
            to = (1 - x if dx else x, 1 - y if dy else y, 1 - c if dc else c)
            cp = pltpu.make_async_remote_copy(src_ref=v_ref, dst_ref=o_ref.at[me], send_sem=send_sems.at[k - 1],
                                              recv_sem=recv_sems.at[k - 1], device_id=to, device_id_type=MESH)
            cp.start()
            copies.append((cp, k, to))
        for cp, k, to in copies:
            cp.wait_send()
            theirs = o_ref.at[4 * to[0] + 2 * to[1] + to[2]]
            pltpu.make_async_remote_copy(src_ref=theirs, dst_ref=theirs, send_sem=send_sems.at[k - 1],
                                         recv_sem=recv_sems.at[k - 1], device_id=to, device_id_type=MESH).wait_recv()
        loc.wait()

    return pl.pallas_call(
        body, name=name, in_specs=[_ANY], out_specs=_ANY,
        out_shape=jax.ShapeDtypeStruct((8,) + v.shape, v.dtype),
        scratch_shapes=[pltpu.SemaphoreType.DMA((7,)), pltpu.SemaphoreType.DMA((7,)), pltpu.SemaphoreType.DMA(())],
    )(v)


def _gather_begin(shards, place, after, *, name):
    names = list(shards)
    bufs = [_place_shard(shards[k], place, dtype=F32 if k == 'small' else BF16, name=f"{name}_place_{k}") for k in names]
    send_sems, recv_sems, bufs, token = _gather_ici_start(bufs, after, name=name + "_ici_start")
    return (names, [shards[k].shape for k in names], send_sems, recv_sems, bufs), token


def _gather_end(state, after, *, name):
    names, shapes, send_sems, recv_sems, bufs = state
    bufs = _gather_ici_wait(send_sems, recv_sems, bufs, after, name=name + "_ici_wait")
    bufs = _gather_d2d(bufs, name=name + "_d2d")
    return {k: o.reshape((N_CHIPS,) + sh) for k, o, sh in zip(names, bufs, shapes)}


def _reduce_begin(grads, place, *, name):
    names = list(grads)
    gs = [grads[k].reshape(N_CHIPS, 2, grads[k].shape[1] // 2, grads[k].shape[2]) for k in names]
    recv = _sibling_halves(gs, name=name + "_sib")
    ps = [_add_halves(g, r, place, name=f"{name}_add2_{k}") for g, r, k in zip(gs, recv, names)]
    send_sems, recv_sems, ps, lands, token = _chip_scatter_start(ps, recv[0], name=name + "_scatter_start")
    return (names, [grads[k].shape[1:] for k in names], send_sems, recv_sems, ps, lands), token


def _reduce_end(state, place, after, *, name):
    names, shapes, send_sems, recv_sems, ps, lands = state
    ps, lands = _chip_scatter_wait(send_sems, recv_sems, ps, lands, after, name=name + "_scatter_wait")
    rs = [_sum_chips(p, l, place, name=f"{name}_sum4_{k}") for p, l, k in zip(ps, lands, names)]
    both = _sibling_share(rs, name=name + "_share")
    return {k: b.reshape(sh) for k, b, sh in zip(names, both, shapes)}


def _pad_lanes(a, n=LANES):
    return jnp.pad(a, [(0, 0)] * (a.ndim - 1) + [(0, n - a.shape[-1])])


def _unshard_cols(g):
    return jnp.transpose(g, (1, 0, 2)).reshape(g.shape[1], -1)


def _shard_cols(w):
    k, n = w.shape
    return jnp.transpose(w.reshape(k, N_CHIPS, n // N_CHIPS), (1, 0, 2))


def _ffn_fwd(h, p, tag):
    b = _rms_fwd(h, p['ffn_norm'], name=f"{tag}_ffn_norm")
    up = _mm(b, p['ffn_w_up'], b_sh='n', name=f"{tag}_ffn_up", bn=1408)
    act = _glu_fwd(up, p['ffn_conv_w'], p['ffn_conv_b'], name=f"{tag}_ffn_glu")
    out = _mm(act, p['ffn_w_down'], res=h, name=f"{tag}_ffn_down", bk=704)
    return out, (h, b, up, act)


def _ffn_bwd(dh, saved, p, tag):
    h, b, up, act = saved
    dact = _mm(dh, p['ffn_w_down'], tb=True, out_dtype=BF16, name=f"{tag}_ffn_dact", bn=1408)
    dw_down = _mm(act, dh, ta=True, out_dtype=BF16, name=f"{tag}_ffn_dwdown", bm=1408)
    dup, dcw, dcb = _glu_bwd(up, p['ffn_conv_w'], p['ffn_conv_b'], dact, name=f"{tag}_ffn_dglu")
    dw_up = _mm(b, dup, ta=True, b_sh='n', o_sh=True, out_dtype=BF16, name=f"{tag}_ffn_dwup", bn=1408)
    db = _mm(dup, p['ffn_w_up'], a_sh=True, b_sh='k', name=f"{tag}_ffn_db", bk=1408)
    dcw = jnp.transpose(dcw, (1, 0, 2)).reshape(dcw.shape[1], -1)
    dcb = dcb.reshape(1, -1)
    dh_in, dg = _rms_bwd(h, p['ffn_norm'], db, res=dh, name=f"{tag}_ffn_dnorm")
    big = {'ffn_w_up': dw_up, 'ffn_w_down': dw_down.reshape(N_CHIPS, -1, dw_down.shape[1])}
    small = {'ffn_norm': dg, 'ffn_conv_w': dcw, 'ffn_conv_b': dcb}
    return dh_in, big, small


def _qkv_attn_fwd(kind, h, p, tag, n_heads):
    a = _rms_fwd(h, p['mix_norm'], name=f"{tag}_norm")
    if kind == 'fox':
        qkv = _mm(a, p['w_in'], name=f"{tag}_qkv", bn=896)
        cum = _fgate_fwd(qkv, p['b_f'], fcol=3 * n_heads, name=f"{tag}_fgate")
        cum_t = cum[:, :n_heads].T
        cq, ck = cum_t[:, :, None], cum_t[:, None, :]
    else:
        qkv = _mm(a, p['w_in'], b_sh='n', name=f"{tag}_qkv", bn=768)
        cq = ck = None
    cols = dict(qcol=lambda hh: hh, kcol=lambda hh: n_heads + hh, vcol=lambda hh: 2 * n_heads + hh)
    o = _attn_fwd(kind, qkv, qkv, qkv, name=f"{tag}_attn", n_heads=n_heads, dqk=HEAD_DIM, scale=HEAD_DIM ** -0.5,
                  gains=p['qk_gain'], cq=cq, ck=ck, **cols)
    out = _mm(o, p['w_out'], res=h, name=f"{tag}_out")
    return out, (h, a, qkv, o, cq, ck)


def _qkv_attn_bwd(kind, dh, saved, p, tag, n_heads):
    h, a, qkv, o, cq, ck = saved
    do = _mm(dh, p['w_out'], tb=True, out_dtype=BF16, name=f"{tag}_do")
    dw_out = _mm(o, dh, ta=True, out_dtype=BF16, name=f"{tag}_dwout")
    cols = dict(qcol=lambda hh: hh, kcol=lambda hh: n_heads + hh, vcol=lambda hh: 2 * n_heads + hh)
    outs = _attn_bwd(kind, qkv, qkv, qkv, o, do, name=f"{tag}_dattn", n_heads=n_heads, dqk=HEAD_DIM,
                     scale=HEAD_DIM ** -0.5, gains=p['qk_gain'], cq=cq, ck=ck, **cols)
    dq, dk, dv, dgain = outs[:4]
    small = {'q_gain': dgain[0], 'k_gain': dgain[1]}
    if kind == 'fox':
        dcq, dck = outs[4:]
        dca = _pad_lanes(dcq[:, :, 0].T)
        dcb = _pad_lanes(dck[:, 0, :].T)
        dflog, dbf = _fgate_bwd(qkv, p['b_f'], dca, dcb, fcol=3 * n_heads, n_heads=n_heads, name=f"{tag}_dfgate")
        small['b_f'] = dbf[:, :n_heads]
        dqkv = jnp.concatenate([dq, dk, dv, dflog], axis=1)
        dw_in = _mm(a, dqkv, ta=True, out_dtype=BF16, name=f"{tag}_dwin", bn=896)
        da = _mm(dqkv, p['w_in'], tb=True, name=f"{tag}_da", bk=896)
        dw_in = _shard_cols(dw_in[:, :3 * n_heads * HEAD_DIM + n_heads])
    else:
        dqkv = jnp.concatenate([dq, dk, dv], axis=1)
        dw_in = _mm(a, dqkv, ta=True, o_sh=True, out_dtype=BF16, name=f"{tag}_dwin", bn=768)
        da = _mm(dqkv, p['w_in'], b_sh='k', name=f"{tag}_da", bk=768)
    dh_in, dg = _rms_bwd(h, p['mix_norm'], da, res=dh, name=f"{tag}_dnorm")
    small['mix_norm'] = dg
    big = {'w_in': dw_in, 'w_out': dw_out.reshape(N_CHIPS, -1, dw_out.shape[1])}
    return dh_in, big, small


def _mla_fwd(h, p, tag, n_heads):
    a = _rms_fwd(h, p['mix_norm'], name=f"{tag}_norm")
    c = _mm(a, p['w_in'], name=f"{tag}_latent", bn=1152)
    cn = _mla_latent_fwd(c, p['a_gain'], name=f"{tag}_latent_norm")
    qp = _mm(cn[:, :MLA_Q_RANK], p['w_q_b'], name=f"{tag}_q_up")
    kv = _mm(cn[:, MLA_Q_RANK:], p['w_kv_b'], b_sh='n', name=f"{tag}_kv_up")
    qc, kc = _mla_prep_fwd(qp, kv, c, p['gq'], p['gk'], p['cos'], p['sin'], n_heads=n_heads, name=f"{tag}_prep")
    cols = dict(qcol=lambda hh: hh, kcol=lambda hh: hh, vcol=lambda hh: 2 * hh + 1)
    scale = (MLA_NOPE + MLA_ROPE) ** -0.5
    o = _attn_fwd('mla', qc, kc, kv, name=f"{tag}_attn", n_heads=n_heads, dqk=2 * LANES, scale=scale, **cols)
    out = _mm(o, p['w_out'], res=h, name=f"{tag}_out")
    return out, (h, a, c, cn, qp, kv, qc, kc, o)


def _mla_bwd(dh, saved, p, tag, n_heads):
    h, a, c, cn, qp, kv, qc, kc, o = saved
    do = _mm(dh, p['w_out'], tb=True, out_dtype=BF16, name=f"{tag}_do")
    dw_out = _mm(o, dh, ta=True, out_dtype=BF16, name=f"{tag}_dwout")
    cols = dict(qcol=lambda hh: hh, kcol=lambda hh: hh, vcol=lambda hh: 2 * hh + 1)
    scale = (MLA_NOPE + MLA_ROPE) ** -0.5
    dqc, dkc, dv = _attn_bwd('mla', qc, kc, kv, o, do, name=f"{tag}_dattn", n_heads=n_heads, dqk=2 * LANES,
                             scale=scale, **cols)
    dqn, dqr, dkv, dkr, dgq, dgk = _mla_prep_bwd(qp, kv, c, p['gq'], p['gk'], p['cos'], p['sin'], dqc, dkc, dv,
                                                 n_heads=n_heads, name=f"{tag}_dprep")
    dqp = jnp.concatenate([dqn, dqr], axis=1)
    cn_q, cn_kv = cn[:, :MLA_Q_RANK], cn[:, MLA_Q_RANK:]
    dw_q_b = _mm(cn_q, dqp, ta=True, out_dtype=BF16, name=f"{tag}_dwqb", bm=512)
    dcn_q = _mm(dqp, p['w_q_b'], tb=True, out_dtype=BF16, name=f"{tag}_dcnq")
    dw_kv_b = _mm(cn_kv, dkv, ta=True, o_sh=True, out_dtype=BF16, name=f"{tag}_dwkvb", bm=512)
    dcn_kv = _mm(dkv, p['w_kv_b'], b_sh='k', out_dtype=BF16, name=f"{tag}_dcnkv")
    dc, dga = _mla_latent_bwd(c, p['a_gain'], dcn_q, dcn_kv, dkr, name=f"{tag}_dlatent")
    dw_in = _mm(a, dc, ta=True, out_dtype=BF16, name=f"{tag}_dwin", bn=1152)
    da = _mm(dc, p['w_in'], tb=True, name=f"{tag}_da", bk=1152)
    dh_in, dg = _rms_bwd(h, p['mix_norm'], da, res=dh, name=f"{tag}_dnorm")
    k_rank = dw_q_b.shape[0]
    nope = dw_q_b[:, :n_heads * LANES].reshape(k_rank, n_heads, LANES)
    rope = dw_q_b[:, n_heads * LANES:].reshape(k_rank, n_heads, LANES)[:, :, :MLA_ROPE]
    dw_q_b = jnp.concatenate([nope, rope], axis=2).reshape(k_rank, n_heads * (MLA_NOPE + MLA_ROPE))
    w_in_cols = MLA_Q_RANK + MLA_KV_RANK + MLA_ROPE
    big = {'w_in': dw_in[:, :w_in_cols].reshape(N_CHIPS, -1, w_in_cols), 'w_q_b': _shard_cols(dw_q_b),
           'w_kv_b': dw_kv_b, 'w_out': dw_out.reshape(N_CHIPS, -1, dw_out.shape[1])}
    small = {'mix_norm': dg, 'q_a_gain': dga[:, :MLA_Q_RANK], 'kv_a_gain': dga[:, MLA_Q_RANK:],
             'q_gain': jnp.concatenate([dgq[0], dgq[1][:, :MLA_ROPE]], axis=1),
             'k_gain': jnp.concatenate([dgk[0], dgk[1][:, :MLA_ROPE]], axis=1)}
    return dh_in, big, small


def _sgu_fwd(h, p, tag):
    a = _rms_fwd(h, p['mix_norm'], name=f"{tag}_norm")
    uv = _mm(a, p['w_in'], b_sh='n', name=f"{tag}_in")
    u, vn = _sgu_act_fwd(uv, p['v_gain'], name=f"{tag}_act")
    gated = _sgu_mix_fwd(u, vn, p['w_s'], p['b_s'], name=f"{tag}_mix")
    out = _mm(gated, p['w_out'], res=h, name=f"{tag}_out")
    return out, (h, a, uv, u, vn, gated)


def _sgu_bwd(dh, saved, p, tag):
    h, a, uv, u, vn, gated = saved
    dgated = _mm(dh, p['w_out'], tb=True, out_dtype=BF16, name=f"{tag}_dgated")
    dw_out = _mm(gated, dh, ta=True, out_dtype=BF16, name=f"{tag}_dwout")
    du, dvn, dws, dbs = _sgu_mix_bwd(u, vn, p['w_s'], p['b_s'], dgated, name=f"{tag}_dmix")
    duv, dvg = _sgu_act_bwd(uv, p['v_gain'], du, dvn, name=f"{tag}_dact")
    dw_in = _mm(a, duv, ta=True, o_sh=True, out_dtype=BF16, name=f"{tag}_dwin")
    da = _mm(duv, p['w_in'], b_sh='k', name=f"{tag}_da")
    dh_in, dg = _rms_bwd(h, p['mix_norm'], da, res=dh, name=f"{tag}_dnorm")
    big = {'w_in': dw_in, 'w_out': dw_out.reshape(N_CHIPS, -1, dw_out.shape[1])}
    small = {'mix_norm': dg, 'v_gain': dvg, 'w_s': dws, 'b_s': dbs[:, :, 0]}
    return dh_in, big, small


def _pack(parts):
    flat = jnp.concatenate([p.reshape(-1).astype(F32) for p in parts])
    rows = -(-flat.shape[0] // LANES)
    rows = -(-rows // 32) * 32
    return jnp.pad(flat, (0, rows * LANES - flat.shape[0])).reshape(rows, LANES)


def _unpack(packed, shapes):
    flat = packed.reshape(-1)
    out, off = [], 0
    for s in shapes:
        n = 1
        for d in s:
            n *= d
        out.append(flat[off:off + n].reshape(s))
        off += n
    return out


MIXERS = ('fox', 'mla', 'sb', 'sgu')
WEIGHT_NAMES = ['mix_norm', 'ffn_norm', 'fox_w_in', 'fox_b_f', 'fox_q_gain', 'fox_k_gain', 'fox_w_out', 'mla_w_in',
                'mla_q_a_gain', 'mla_kv_a_gain', 'mla_w_q_b', 'mla_w_kv_b', 'mla_q_gain', 'mla_k_gain', 'mla_w_out',
                'sb_w_in', 'sb_q_gain', 'sb_k_gain', 'sb_w_out', 'sgu_w_in', 'sgu_v_gain', 'sgu_w_s', 'sgu_b_s',
                'sgu_w_out', 'ffn_w_up', 'ffn_conv_w', 'ffn_conv_b', 'ffn_w_down']
SMALL_SHARDED = {'mla_q_a_gain': 1, 'mla_kv_a_gain': 1, 'sgu_v_gain': 1, 'ffn_conv_w': 2}
BIG = ['fox_w_in', 'fox_w_out', 'mla_w_in', 'mla_w_q_b', 'mla_w_kv_b', 'mla_w_out', 'sb_w_in', 'sb_w_out', 'sgu_w_in',
       'sgu_w_out', 'ffn_w_up', 'ffn_w_down']


def kernel(x, positions, mix_norm, ffn_norm, fox_w_in, fox_b_f, fox_q_gain, fox_k_gain, fox_w_out, mla_w_in, mla_q_a_gain, mla_kv_a_gain, mla_w_q_b, mla_w_kv_b, mla_q_gain, mla_k_gain, mla_w_out, sb_w_in, sb_q_gain, sb_k_gain, sb_w_out, sgu_w_in, sgu_v_gain, sgu_w_s, sgu_b_s, sgu_w_out, ffn_w_up, ffn_conv_w, ffn_conv_b, ffn_w_down, loss_target, m_mix_norm, m_ffn_norm, m_fox_w_in, m_fox_b_f, m_fox_q_gain, m_fox_k_gain, m_fox_w_out, m_mla_w_in, m_mla_q_a_gain, m_mla_kv_a_gain, m_mla_w_q_b, m_mla_w_kv_b, m_mla_q_gain, m_mla_k_gain, m_mla_w_out, m_sb_w_in, m_sb_q_gain, m_sb_k_gain, m_sb_w_out, m_sgu_w_in, m_sgu_v_gain, m_sgu_w_s, m_sgu_b_s, m_sgu_w_out, m_ffn_w_up, m_ffn_conv_w, m_ffn_conv_b, m_ffn_w_down, v_mix_norm, v_ffn_norm, v_fox_w_in, v_fox_b_f, v_fox_q_gain, v_fox_k_gain, v_fox_w_out, v_mla_w_in, v_mla_q_a_gain, v_mla_kv_a_gain, v_mla_w_q_b, v_mla_w_kv_b, v_mla_q_gain, v_mla_k_gain, v_mla_w_out, v_sb_w_in, v_sb_q_gain, v_sb_k_gain, v_sb_w_out, v_sgu_w_in, v_sgu_v_gain, v_sgu_w_s, v_sgu_b_s, v_sgu_w_out, v_ffn_w_up, v_ffn_conv_w, v_ffn_conv_b, v_ffn_w_down):
    args = dict(locals())
    W = {k: args[k] for k in WEIGHT_NAMES}
    M = {k: args['m_' + k] for k in WEIGHT_NAMES}
    V = {k: args['v_' + k] for k in WEIGHT_NAMES}
    depth = mix_norm.shape[0]
    s_len, d_model = x.shape[1], x.shape[2]
    n_heads = d_model // HEAD_DIM
    assert all(W[k].shape[0] == 1 for k in WEIGHT_NAMES if k.split('_')[0] in MIXERS), "one layer per mixer"
    xi, yi, ci = lax.axis_index("x"), lax.axis_index("y"), lax.axis_index("c")
    chip = 2 * xi + yi
    place = tuple(jnp.reshape(v, (1,)).astype(jnp.int32) for v in (xi, yi, ci))

    small_local = _pack([W[k][0] if k != 'ffn_conv_w' else W[k] for k in SMALL_SHARDED])

    def layer_shards(i):
        mixer = MIXERS[i % len(MIXERS)]
        shards = {k: W[k][0] for k in BIG if k.startswith(mixer + '_')}
        shards['ffn_w_up'] = W['ffn_w_up'][i]
        shards['ffn_w_down'] = W['ffn_w_down'][i]
        if i == 0:
            shards['small'] = small_local
        return shards

    gathered = {}
    state, token = _gather_begin(layer_shards(0), place, mix_norm, name="gather0")
    gathered[0] = _gather_end(state, token, name="gather0")
    small_shapes = [W[k][0].shape if k != 'ffn_conv_w' else W[k].shape for k in SMALL_SHARDED]
    per_chip = [_unpack(gathered[0]['small'][s], small_shapes) for s in range(N_CHIPS)]
    full_small = {k: jnp.concatenate([per_chip[s][j] for s in range(N_CHIPS)], axis=-1)
                  for j, k in enumerate(SMALL_SHARDED)}

    pos = positions.reshape(s_len).astype(F32)
    inv_freq = ROPE_THETA ** (-jnp.arange(0, MLA_ROPE, 2, dtype=F32) / MLA_ROPE)
    ang = pos[:, None] * inv_freq
    cos_t = _pad_lanes(jnp.concatenate([jnp.cos(ang), jnp.cos(ang)], axis=1))
    sin_t = _pad_lanes(jnp.concatenate([-jnp.sin(ang), jnp.sin(ang)], axis=1))

    def layer_params(i):
        mixer = MIXERS[i % len(MIXERS)]
        g = gathered[i]
        p = {'mix_norm': mix_norm[i:i + 1], 'ffn_norm': ffn_norm[i:i + 1], 'ffn_w_up': g['ffn_w_up'],
             'ffn_w_down': g['ffn_w_down'].reshape(-1, d_model), 'ffn_conv_w': full_small['ffn_conv_w'][i],
             'ffn_conv_b': ffn_conv_b[i:i + 1]}
        rows = lambda w: w.reshape(-1, w.shape[-1])
        if mixer == 'fox':
            w = _unshard_cols(g['fox_w_in'])
            p['w_in'] = jnp.pad(w, ((0, 0), (0, (3 * n_heads + 1) * HEAD_DIM - w.shape[1])))
            p['b_f'] = _pad_lanes(fox_b_f)
            p['qk_gain'] = jnp.stack([fox_q_gain, fox_k_gain])
            p['w_out'] = rows(g['fox_w_out'])
        elif mixer == 'sb':
            p['w_in'] = g['sb_w_in']
            p['qk_gain'] = jnp.stack([sb_q_gain, sb_k_gain])
            p['w_out'] = rows(g['sb_w_out'])
        elif mixer == 'sgu':
            p['w_in'] = g['sgu_w_in']
            p['v_gain'] = full_small['sgu_v_gain'].reshape(1, -1)
            p['w_s'] = sgu_w_s[0]
            p['b_s'] = sgu_b_s[0][:, :, None]
            p['w_out'] = rows(g['sgu_w_out'])
        else:
            w = rows(g['mla_w_in'])
            p['w_in'] = jnp.pad(w, ((0, 0), (0, MLA_Q_RANK + MLA_KV_RANK + LANES - w.shape[1])))
            p['a_gain'] = jnp.concatenate([full_small['mla_q_a_gain'], full_small['mla_kv_a_gain']]).reshape(1, -1)
            wq = _unshard_cols(g['mla_w_q_b']).reshape(MLA_Q_RANK, n_heads, MLA_NOPE + MLA_ROPE)
            p['w_q_b'] = jnp.concatenate([wq[:, :, :MLA_NOPE].reshape(MLA_Q_RANK, -1),
                                          _pad_lanes(wq[:, :, MLA_NOPE:]).reshape(MLA_Q_RANK, -1)], axis=1)
            p['w_kv_b'] = g['mla_w_kv_b']
            p['gq'] = jnp.stack([mla_q_gain[:, :MLA_NOPE], _pad_lanes(mla_q_gain[:, MLA_NOPE:])])
            p['gk'] = jnp.stack([mla_k_gain[:, :MLA_NOPE], _pad_lanes(mla_k_gain[:, MLA_NOPE:])])
            p['cos'], p['sin'] = cos_t, sin_t
            p['w_out'] = rows(g['mla_w_out'])
        return mixer, p

    h = x.reshape(s_len, d_model)
    saved = []
    for i in range(depth):
        if i + 1 < depth:
            state, token = _gather_begin(layer_shards(i + 1), place, gathered[i]['ffn_w_down'], name=f"gather{i + 1}")
        mixer, p = layer_params(i)
        if i + 1 < depth:
            p['mix_norm'] = p['mix_norm'] + token[0:1, 0:1]
        tag = f"l{i}_{mixer}"
        if mixer in ('fox', 'sb'):
            h, sm = _qkv_attn_fwd(mixer, h, p, tag, n_heads)
        elif mixer == 'mla':
            h, sm = _mla_fwd(h, p, tag, n_heads)
        else:
            h, sm = _sgu_fwd(h, p, tag)
        h, sf = _ffn_fwd(h, p, f"l{i}")
        saved.append((mixer, p, sm, sf))
        if i + 1 < depth:
            gathered[i + 1] = _gather_end(state, h, name=f"gather{i + 1}")
    loss_row, dh = _loss(h, loss_target.reshape(s_len, d_model))
    loss = lax.psum(loss_row[0, 0], ("x", "y", "c"))

    big_grads, small_grads = {}, {k: [None] * depth for k in ('mix_norm', 'ffn_norm', 'ffn_conv_w', 'ffn_conv_b')}

    def keep(reduced, i):
        for k, v in reduced.items():
            if k.startswith('ffn_'):
                big_grads.setdefault(k, [None] * depth)[i] = v
            else:
                big_grads[k] = v[None]

    state, token = None, None
    for i in reversed(range(depth)):
        mixer, p, sm, sf = saved[i]
        tag = f"l{i}_{mixer}"
        if token is not None:
            dh = dh + token[0:1, 0:1]
        dh, big_f, small_f = _ffn_bwd(dh, sf, p, f"l{i}")
        if mixer in ('fox', 'sb'):
            dh, big_m, small_m = _qkv_attn_bwd(mixer, dh, sm, p, tag, n_heads)
        elif mixer == 'mla':
            dh, big_m, small_m = _mla_bwd(dh, sm, p, tag, n_heads)
        else:
            dh, big_m, small_m = _sgu_bwd(dh, sm, p, tag)
        if state is not None:
            keep(_reduce_end(state, place, dh, name=f"reduce{i + 1}"), i + 1)
        layer_big = {f"{mixer}_{k}": v for k, v in big_m.items()}
        layer_big.update(big_f)
        state, token = _reduce_begin(layer_big, place, name=f"reduce{i}")
        for k, v in {**small_m, **small_f}.items():
            if k in small_grads:
                small_grads[k][i] = v
            else:
                small_grads[f"{mixer}_{k}"] = v
    last_state = state
    grad_x = dh.reshape(x.shape)
    for k in ('mix_norm', 'ffn_norm', 'ffn_conv_b'):
        small_grads[k] = jnp.concatenate(small_grads[k], axis=0)
    small_grads['ffn_conv_w'] = jnp.stack(small_grads['ffn_conv_w'])

    small_names = [k for k in WEIGHT_NAMES if k not in BIG]
    full_shapes = {k: (W[k].shape[:-1] + (W[k].shape[-1] * N_CHIPS,) if k in SMALL_SHARDED else W[k].shape)
                   for k in small_names}
    packed = _pack([small_grads[k].reshape(full_shapes[k]) for k in small_names])
    summed = _sum_devices(_broadcast_all(packed, name="small_bcast"), name="small_sum")
    small_full = dict(zip(small_names, _unpack(summed, [full_shapes[k] for k in small_names])))
    keep(_reduce_end(last_state, place, summed, name="reduce0"), 0)
    for k in ('ffn_w_up', 'ffn_w_down'):
        big_grads[k] = jnp.stack(big_grads[k])
    grads = dict(big_grads)
    for k in small_names:
        g = small_full[k]
        if k in SMALL_SHARDED:
            n = W[k].shape[-1]
            g = lax.dynamic_slice_in_dim(g, chip * n, n, axis=g.ndim - 1)
        grads[k] = g
    grads = {k: grads[k].reshape(W[k].shape) for k in WEIGHT_NAMES}

    delta, new_m, new_v = {}, {}, {}
    for k in WEIGHT_NAMES:
        delta[k], new_m[k], new_v[k] = _adamw(W[k], grads[k], M[k], V[k], name=f"adamw_{k}")
    return (loss, grad_x, *[grads[k] for k in WEIGHT_NAMES], *[delta[k] for k in WEIGHT_NAMES],
            *[new_m[k] for k in WEIGHT_NAMES], *[new_v[k] for k in WEIGHT_NAMES])
```

```python
import functools

import jax
import jax.numpy as jnp
from jax import lax
from jax.experimental import pallas as pl
from jax.experimental.pallas import tpu as pltpu

F32 = jnp.float32
BF16 = jnp.bfloat16
LANES = 128
HEAD_DIM = 128
NORM_EPS = 1e-6
MLA_Q_RANK = 512
MLA_KV_RANK = 512
MLA_NOPE = 128
MLA_ROPE = 64
ROPE_THETA = 10000.0
SGU_CHUNK = 128
N_CHIPS = 4
ADAM_LR, ADAM_B1, ADAM_B2, ADAM_EPS, ADAM_WD, ADAM_STEP = 0.001, 0.9, 0.999, 1e-08, 0.01, 10
VMEM_LIMIT_BYTES = 56 * 1024 * 1024
MESH = pl.DeviceIdType.MESH
NEG_BIG = -1e30


def _params(*sem):
    return pltpu.CompilerParams(dimension_semantics=sem, vmem_limit_bytes=VMEM_LIMIT_BYTES)


def _div_block(n, target, mult=LANES):
    if n <= target:
        return n
    best = None
    for b in range(mult, target + 1, mult):
        if n % b == 0:
            best = b
    assert best is not None, (n, target, mult)
    return best


def _iota(shape, dim):
    return lax.broadcasted_iota(jnp.int32, shape, dim)


def _dot(a, b, ca, cb):
    return lax.dot_general(a, b, (((ca,), (cb,)), ((), ())), preferred_element_type=F32)


def _mm(a, b, *, name, ta=False, tb=False, a_sh=False, b_sh=None, o_sh=False, res=None, out_dtype=F32,
        bm=1024, bn=1024, bk=512):
    if a_sh:
        assert not ta
        m, k = a.shape[1], a.shape[0] * a.shape[2]
    else:
        m, k = (a.shape[1], a.shape[0]) if ta else a.shape
    if b_sh == 'n':
        n = b.shape[2] * b.shape[0]
        assert b.shape[1] == k and not tb
    elif b_sh == 'k':
        n = b.shape[1]
        assert b.shape[2] * N_CHIPS == k
    else:
        n = b.shape[0] if tb else b.shape[1]
        assert (b.shape[1] if tb else b.shape[0]) == k
    n_sh = n // N_CHIPS
    k_sh = k // N_CHIPS
    bm = _div_block(m, bm, 8 if not ta else LANES)
    bn_limit = n
    if b_sh == 'n':
        bn_limit = b.shape[2]
    if o_sh:
        bn_limit = min(bn_limit, n_sh)
    bn = _div_block(bn_limit, bn)
    assert (not o_sh or n_sh % bn == 0) and (b_sh != 'n' or b.shape[2] % bn == 0)
    bk_limit = k_sh if b_sh == 'k' else k
    if a_sh:
        bk_limit = min(bk_limit, a.shape[2])
    bk = _div_block(bk_limit, bk)
    assert (not a_sh or a.shape[2] % bk == 0) and (b_sh != 'k' or k_sh % bk == 0)
    nbo = n_sh // bn if o_sh else 1
    nbb = b.shape[2] // bn if b_sh == 'n' else 1
    nks = k_sh // bk if b_sh == 'k' else 1
    nka = a.shape[2] // bk if a_sh else 1
    nk = k // bk

    if a_sh:
        a_spec = pl.BlockSpec((None, bm, bk), lambda i, j, q: (q // nka, i, q % nka))
    elif ta:
        a_spec = pl.BlockSpec((bk, bm), lambda i, j, q: (q, i))
    else:
        a_spec = pl.BlockSpec((bm, bk), lambda i, j, q: (i, q))
    if b_sh == 'n':
        b_spec = pl.BlockSpec((None, bk, bn), lambda i, j, q: (j // nbb, q, j % nbb))
    elif b_sh == 'k':
        b_spec = pl.BlockSpec((None, bn, bk), lambda i, j, q: (q // nks, j, q % nks))
    elif tb:
        b_spec = pl.BlockSpec((bn, bk), lambda i, j, q: (j, q))
    else:
        b_spec = pl.BlockSpec((bk, bn), lambda i, j, q: (q, j))
    if o_sh:
        o_spec = pl.BlockSpec((None, bm, bn), lambda i, j, q: (j // nbo, i, j % nbo))
        o_shape = jax.ShapeDtypeStruct((N_CHIPS, m, n_sh), out_dtype)
    else:
        o_spec = pl.BlockSpec((bm, bn), lambda i, j, q: (i, j))
        o_shape = jax.ShapeDtypeStruct((m, n), out_dtype)
    tb_eff = tb or b_sh == 'k'

    def body(a_ref, b_ref, *rest):
        if res is not None:
            r_ref, o_ref, acc = rest
        else:
            o_ref, acc = rest
        q = pl.program_id(2)

        @pl.when(q == 0)
        def _():
            acc[...] = jnp.zeros_like(acc)

        acc[...] += _dot(a_ref[...].astype(BF16), b_ref[...].astype(BF16), 0 if ta else 1, 1 if tb_eff else 0)

        @pl.when(q == nk - 1)
        def _():
            r = acc[...]
            if res is not None:
                r = r + r_ref[...].astype(F32)
            o_ref[...] = r.astype(out_dtype)

    ins = [a, b]
    in_specs = [a_spec, b_spec]
    if res is not None:
        assert not o_sh
        ins.append(res)
        in_specs.append(pl.BlockSpec((bm, bn), lambda i, j, q: (i, j)))
    return pl.pallas_call(
        body, name=name, grid=(m // bm, n // bn, nk), in_specs=in_specs, out_specs=o_spec, out_shape=o_shape,
        scratch_shapes=[pltpu.VMEM((bm, bn), F32)],
        compiler_params=_params("parallel", "parallel", "arbitrary"))(*ins)


def _rms_fwd(x, g, *, name, out_dtype=BF16, br=256):
    r, c = x.shape
    br = _div_block(r, br, 8)

    def body(x_ref, g_ref, o_ref):
        xv = x_ref[...].astype(F32)
        inv = lax.rsqrt(jnp.mean(xv * xv, axis=-1, keepdims=True) + NORM_EPS)
        o_ref[...] = (xv * inv * g_ref[...]).astype(out_dtype)

    return pl.pallas_call(
        body, name=name, grid=(r // br,),
        in_specs=[pl.BlockSpec((br, c), lambda i: (i, 0)), pl.BlockSpec((1, c), lambda i: (0, 0))],
        out_specs=pl.BlockSpec((br, c), lambda i: (i, 0)), out_shape=jax.ShapeDtypeStruct((r, c), out_dtype),
        compiler_params=_params("parallel"))(x, g)


def _rms_bwd_math(xv, gv, dyv, n):
    inv = lax.rsqrt(jnp.sum(xv * xv, axis=-1, keepdims=True) / n + NORM_EPS)
    xh = xv * inv
    dyg = dyv * gv
    dx = inv * (dyg - xh * (jnp.sum(dyg * xh, axis=-1, keepdims=True) / n))
    return dx, dyv * xh


def _rms_bwd(x, g, dy, *, name, res=None, br=256):
    r, c = x.shape
    br = _div_block(r, br, 8)

    def body(x_ref, g_ref, dy_ref, *rest):
        if res is not None:
            r_ref, dx_ref, dg_ref = rest
        else:
            dx_ref, dg_ref = rest
        dx, dgr = _rms_bwd_math(x_ref[...].astype(F32), g_ref[...], dy_ref[...].astype(F32), c)
        if res is not None:
            dx = dx + r_ref[...]
        dx_ref[...] = dx

        @pl.when(pl.program_id(0) == 0)
        def _():
            dg_ref[...] = jnp.zeros_like(dg_ref)

        dg_ref[...] += jnp.sum(dgr, axis=0, keepdims=True)

    row = pl.BlockSpec((br, c), lambda i: (i, 0))
    vec = pl.BlockSpec((1, c), lambda i: (0, 0))
    ins = [x, g, dy] + ([res] if res is not None else [])
    return pl.pallas_call(
        body, name=name, grid=(r // br,), in_specs=[row, vec, row] + ([row] if res is not None else []),
        out_specs=[row, vec], out_shape=[jax.ShapeDtypeStruct((r, c), F32), jax.ShapeDtypeStruct((1, c), F32)],
        compiler_params=_params("arbitrary"))(*ins)


def _loss(y, target, *, name="loss", br=256):
    r, c = y.shape
    br = _div_block(r, br, 8)

    def body(y_ref, t_ref, l_ref, dy_ref):
        d = y_ref[...] - t_ref[...]
        dy_ref[...] = d * (1.0 / c)

        @pl.when(pl.program_id(0) == 0)
        def _():
            l_ref[...] = jnp.zeros_like(l_ref)

        part = jnp.sum(d * d, axis=0, keepdims=True)
        l_ref[...] += (0.5 / c) * jnp.sum(part, axis=1, keepdims=True) * jnp.ones((1, LANES), F32)

    row = pl.BlockSpec((br, c), lambda i: (i, 0))
    return pl.pallas_call(
        body, name=name, grid=(r // br,), in_specs=[row, row],
        out_specs=[pl.BlockSpec((1, LANES), lambda i: (0, 0)), row],
        out_shape=[jax.ShapeDtypeStruct((1, LANES), F32), jax.ShapeDtypeStruct((r, c), F32)],
        compiler_params=_params("arbitrary"))(y, target)


def _split2(x):
    hi = x.astype(BF16)
    lo = (x - hi.astype(F32)).astype(BF16)
    return hi, lo


def _lane_scan(x, *, suffix):
    rows, n = x.shape
    nb = n // LANES
    a, b = _iota((LANES, LANES), 0), _iota((LANES, LANES), 1)
    tri = ((a > b) if suffix else (a < b)).astype(BF16)
    outs = [None] * nb
    run = jnp.zeros((rows, 1), F32)
    order = range(nb - 1, -1, -1) if suffix else range(nb)
    for blk in order:
        xb = x[:, blk * LANES:(blk + 1) * LANES]
        hi, lo = _split2(xb)
        outs[blk] = _dot(hi, tri, 1, 0) + _dot(lo, tri, 1, 0) + run
        run = run + jnp.sum(xb, axis=-1, keepdims=True)
    return jnp.concatenate(outs, axis=1)


def _softplus(z):
    return jnp.maximum(z, 0.0) + jnp.log(1.0 + jnp.exp(-jnp.abs(z)))


def _head_norm(x, g):
    xv = x.astype(F32)
    inv = lax.rsqrt(jnp.mean(xv * xv, axis=-1, keepdims=True) + NORM_EPS)
    return xv * inv * g


def _attn_weights(kind, qn, kn, scale, qi, bq, bias):
    s = _dot(qn, kn, 1, 1) * scale
    row = qi * bq + _iota(s.shape, 0)
    col = _iota(s.shape, 1)
    if kind == 'sb':
        strict = col < row
        sp = _softplus(s)
        after = _lane_scan(jnp.where(strict, -sp, 0.0), suffix=True)
        w = jnp.where(strict, jnp.exp(s - sp + after), 0.0)
        return w, (strict, s - sp)
    if bias is not None:
        s = s + bias
    s = jnp.where(col <= row, s, NEG_BIG)
    mx = jnp.max(s, axis=-1, keepdims=True)
    e = jnp.exp(s - mx)
    return e, jnp.sum(e, axis=-1, keepdims=True)


def _attn_fwd(kind, q, k, v, *, name, n_heads, dqk, qcol, kcol, vcol, scale, gains=None, cq=None, ck=None, bq=256):
    s_len = q.shape[0]
    bq = _div_block(s_len, bq, 8)
    norm, fox = gains is not None, cq is not None

    def body(*refs):
        refs = list(refs)
        q_ref, k_ref, v_ref = refs[:3]
        rest = refs[3:]
        g_ref = rest.pop(0) if norm else None
        cq_ref, ck_ref = (rest.pop(0), rest.pop(0)) if fox else (None, None)
        o_ref, = rest
        qi = pl.program_id(1)

        def step(n_keys):
            if norm:
                qn = _head_norm(q_ref[...], g_ref[0]).astype(BF16)
                kn = _head_norm(k_ref[0:n_keys, :], g_ref[1]).astype(BF16)
            else:
                qn, kn = q_ref[...].astype(BF16), k_ref[0:n_keys, :].astype(BF16)
            bias = (cq_ref[...] - ck_ref[:, 0:n_keys]) if fox else None
            w, aux = _attn_weights(kind, qn, kn, scale, qi, bq, bias)
            o = _dot(w.astype(BF16), v_ref[0:n_keys, :].astype(BF16), 1, 0)
            if kind != 'sb':
                o = o / aux
            o_ref[...] = o.astype(BF16)

        for qv in range(s_len // bq):
            pl.when(qi == qv)(functools.partial(step, (qv + 1) * bq))

    in_specs = [pl.BlockSpec((bq, dqk), lambda h, i: (i, qcol(h))),
                pl.BlockSpec((s_len, dqk), lambda h, i: (0, kcol(h))),
                pl.BlockSpec((s_len, HEAD_DIM), lambda h, i: (0, vcol(h)))]
    ins = [q, k, v]
    if norm:
        in_specs.append(pl.BlockSpec((2, 1, dqk), lambda h, i: (0, 0, 0)))
        ins.append(gains)
    if fox:
        in_specs += [pl.BlockSpec((None, bq, 1), lambda h, i: (h, i, 0)), pl.BlockSpec((None, 1, s_len), lambda h, i: (h, 0, 0))]
        ins += [cq, ck]
    return pl.pallas_call(
        body, name=name, grid=(n_heads, s_len // bq), in_specs=in_specs,
        out_specs=pl.BlockSpec((bq, HEAD_DIM), lambda h, i: (i, h)),
        out_shape=jax.ShapeDtypeStruct((s_len, n_heads * HEAD_DIM), BF16),
        compiler_params=_params("parallel", "parallel"))(*ins)


def _attn_bwd(kind, q, k, v, o, do, *, name, n_heads, dqk, qcol, kcol, vcol, scale, gains=None, cq=None, ck=None,
              bq=256):
    s_len = q.shape[0]
    bq = _div_block(s_len, bq, 8)
    nq = s_len // bq
    norm, fox = gains is not None, cq is not None

    def body(*refs):
        refs = list(refs)
        q_ref, k_ref, v_ref, o_ref, do_ref = refs[:5]
        rest = refs[5:]
        g_ref = rest.pop(0) if norm else None
        cq_ref, ck_ref = (rest.pop(0), rest.pop(0)) if fox else (None, None)
        dq_ref, dk_ref, dv_ref = rest.pop(0), rest.pop(0), rest.pop(0)
        dg_ref = rest.pop(0) if norm else None
        dcq_ref, dck_ref = (rest.pop(0), rest.pop(0)) if fox else (None, None)
        dk_acc, dv_acc = rest
        h, qi = pl.program_id(0), pl.program_id(1)

        @pl.when(qi == 0)
        def _():
            dk_acc[...] = jnp.zeros_like(dk_acc)
            dv_acc[...] = jnp.zeros_like(dv_acc)
            if fox:
                dck_ref[...] = jnp.zeros_like(dck_ref)

        if norm:
            @pl.when((qi == 0) & (h == 0))
            def _():
                dg_ref[...] = jnp.zeros_like(dg_ref)


        def step(n_keys):
            if norm:
                qn = _head_norm(q_ref[...], g_ref[0]).astype(BF16)
                kn = _head_norm(k_ref[0:n_keys, :], g_ref[1]).astype(BF16)
            else:
                qn, kn = q_ref[...].astype(BF16), k_ref[0:n_keys, :].astype(BF16)
            vb = v_ref[0:n_keys, :].astype(BF16)
            dob = do_ref[...].astype(BF16)
            bias = (cq_ref[...] - ck_ref[:, 0:n_keys]) if fox else None
            w, aux = _attn_weights(kind, qn, kn, scale, qi, bq, bias)
            dw = _dot(dob, vb, 1, 1)
            if kind == 'sb':
                strict, log_sig = aux
                g = dw * w
                cc = _lane_scan(g, suffix=False)
                sig = jnp.exp(log_sig)
                ds = jnp.where(strict, g * (1.0 - sig) - cc * sig, 0.0)
                pw = w
            else:
                pw = w / aux
                delta = jnp.sum(do_ref[...].astype(F32) * o_ref[...].astype(F32), axis=-1, keepdims=True)
                ds = pw * (dw - delta)
                if fox:
                    dcq_ref[...] = jnp.sum(ds, axis=1, keepdims=True)
                    dck_ref[:, 0:n_keys] -= jnp.sum(ds, axis=0, keepdims=True)
            dsb = (ds * scale).astype(BF16)
            dqn = _dot(dsb, kn, 1, 0)
            dk_acc[0:n_keys, :] += _dot(dsb, qn, 0, 0)
            dv_acc[0:n_keys, :] += _dot(pw.astype(BF16), dob, 0, 0)
            if norm:
                dq, dgr = _rms_bwd_math(q_ref[...].astype(F32), g_ref[0], dqn, dqk)
                dg_ref[0] += jnp.sum(dgr, axis=0, keepdims=True)
                dq_ref[...] = dq.astype(BF16)
            else:
                dq_ref[...] = dqn.astype(BF16)

        for qv in range(nq):
            pl.when(qi == qv)(functools.partial(step, (qv + 1) * bq))

        @pl.when(qi == nq - 1)
        def _():
            if norm:
                dk, dgr = _rms_bwd_math(k_ref[...].astype(F32), g_ref[1], dk_acc[...], dqk)
                dg_ref[1] += jnp.sum(dgr, axis=0, keepdims=True)
                dk_ref[...] = dk.astype(BF16)
            else:
                dk_ref[...] = dk_acc[...].astype(BF16)
            dv_ref[...] = dv_acc[...].astype(BF16)

    in_specs = [pl.BlockSpec((bq, dqk), lambda h, i: (i, qcol(h))),
                pl.BlockSpec((s_len, dqk), lambda h, i: (0, kcol(h))),
                pl.BlockSpec((s_len, HEAD_DIM), lambda h, i: (0, vcol(h))),
                pl.BlockSpec((bq, HEAD_DIM), lambda h, i: (i, h)),
                pl.BlockSpec((bq, HEAD_DIM), lambda h, i: (i, h))]
    ins = [q, k, v, o, do]
    out_specs = [pl.BlockSpec((bq, dqk), lambda h, i: (i, h)),
                 pl.BlockSpec((s_len, dqk), lambda h, i: (0, h)),
                 pl.BlockSpec((s_len, HEAD_DIM), lambda h, i: (0, h))]
    out_shape = [jax.ShapeDtypeStruct((s_len, n_heads * dqk), BF16), jax.ShapeDtypeStruct((s_len, n_heads * dqk), BF16),
                 jax.ShapeDtypeStruct((s_len, n_heads * HEAD_DIM), BF16)]
    if norm:
        in_specs.append(pl.BlockSpec((2, 1, dqk), lambda h, i: (0, 0, 0)))
        ins.append(gains)
        out_specs.append(pl.BlockSpec((2, 1, dqk), lambda h, i: (0, 0, 0)))
        out_shape.append(jax.ShapeDtypeStruct((2, 1, dqk), F32))
    if fox:
        in_specs += [pl.BlockSpec((None, bq, 1), lambda h, i: (h, i, 0)), pl.BlockSpec((None, 1, s_len), lambda h, i: (h, 0, 0))]
        ins += [cq, ck]
        out_specs += [pl.BlockSpec((None, bq, 1), lambda h, i: (h, i, 0)), pl.BlockSpec((None, 1, s_len), lambda h, i: (h, 0, 0))]
        out_shape += [jax.ShapeDtypeStruct((n_heads, s_len, 1), F32), jax.ShapeDtypeStruct((n_heads, 1, s_len), F32)]
    return pl.pallas_call(
        body, name=name, grid=(n_heads, nq), in_specs=in_specs, out_specs=out_specs, out_shape=out_shape,
        scratch_shapes=[pltpu.VMEM((s_len, dqk), F32), pltpu.VMEM((s_len, HEAD_DIM), F32)],
        compiler_params=_params("arbitrary", "arbitrary"))(*ins)


def _split3(x):
    hi = x.astype(BF16)
    r1 = x - hi.astype(F32)
    mid = r1.astype(BF16)
    lo = (r1 - mid.astype(F32)).astype(BF16)
    return hi, mid, lo


def _seq_scan(x, *, reverse):
    n = x.shape[0] // LANES
    a, b = _iota((LANES, LANES), 0), _iota((LANES, LANES), 1)
    tri = ((b >= a) if reverse else (b <= a)).astype(BF16)
    outs = [None] * n
    run = jnp.zeros((1, x.shape[1]), F32)
    for blk in (range(n - 1, -1, -1) if reverse else range(n)):
        xb = x[blk * LANES:(blk + 1) * LANES, :]
        hi, mid, lo = _split3(xb)
        outs[blk] = _dot(tri, hi, 1, 0) + _dot(tri, mid, 1, 0) + _dot(tri, lo, 1, 0) + run
        run = run + jnp.sum(xb, axis=0, keepdims=True)
    return jnp.concatenate(outs, axis=0)


def _fgate_fwd(qkvf, b_f, *, fcol, name):
    s_len = qkvf.shape[0]

    def body(f_ref, b_ref, cum_ref):
        z = f_ref[...] + b_ref[...]
        cum_ref[...] = _seq_scan(-_softplus(-z), reverse=False)

    return pl.pallas_call(
        body, name=name, grid=(1,),
        in_specs=[pl.BlockSpec((s_len, LANES), lambda i: (0, fcol)), pl.BlockSpec((1, LANES), lambda i: (0, 0))],
        out_specs=pl.BlockSpec((s_len, LANES), lambda i: (0, 0)), out_shape=jax.ShapeDtypeStruct((s_len, LANES), F32),
        compiler_params=_params("arbitrary"))(qkvf, b_f)


def _fgate_bwd(qkvf, b_f, dcum_a, dcum_b, *, fcol, n_heads, name):
    s_len = qkvf.shape[0]

    def body(f_ref, b_ref, da_ref, db_ref, dz_ref, dbias_ref):
        z = f_ref[...] + b_ref[...]
        dlog = _seq_scan(da_ref[...] + db_ref[...], reverse=True)
        dz = dlog * jnp.exp(-_softplus(z))
        dz = jnp.where(_iota(dz.shape, 1) < n_heads, dz, 0.0)
        dz_ref[...] = dz.astype(BF16)
        dbias_ref[...] = jnp.sum(dz, axis=0, keepdims=True)

    full = pl.BlockSpec((s_len, LANES), lambda i: (0, 0))
    vec = pl.BlockSpec((1, LANES), lambda i: (0, 0))
    return pl.pallas_call(
        body, name=name, grid=(1,),
        in_specs=[pl.BlockSpec((s_len, LANES), lambda i: (0, fcol)), vec, full, full],
        out_specs=[full, vec], out_shape=[jax.ShapeDtypeStruct((s_len, LANES), BF16), jax.ShapeDtypeStruct((1, LANES), F32)],
        compiler_params=_params("arbitrary"))(qkvf, b_f, dcum_a, dcum_b)


def _rope_swap(x):
    half = MLA_ROPE // 2
    lane = _iota(x.shape, 1)
    sw = jnp.where(lane < half, pltpu.roll(x, LANES - half, axis=1), pltpu.roll(x, half, axis=1))
    return jnp.where(lane < MLA_ROPE, sw, 0.0)


def _mla_prep_fwd(qp, kv, c, gq, gk, cos_t, sin_t, *, n_heads, name, bs=512):
    s_len = qp.shape[0]
    bs = _div_block(s_len, bs, 8)
    krope_col = (MLA_Q_RANK + MLA_KV_RANK) // LANES

    def body(qn_ref, qr_ref, kn_ref, kr_ref, gq_ref, gk_ref, cos_ref, sin_ref, qc_ref, kc_ref):
        cos_v, sin_v = cos_ref[...], sin_ref[...]

        def rope(x, g):
            xv = x.astype(F32)
            inv = lax.rsqrt(jnp.sum(xv * xv, axis=-1, keepdims=True) / MLA_ROPE + NORM_EPS)
            y = xv * inv * g
            return y * cos_v + _rope_swap(y) * sin_v

        qc_ref[:, :LANES] = _head_norm(qn_ref[...], gq_ref[0]).astype(BF16)
        qc_ref[:, LANES:] = rope(qr_ref[...], gq_ref[1]).astype(BF16)
        kc_ref[:, :LANES] = _head_norm(kn_ref[...], gk_ref[0]).astype(BF16)
        kc_ref[:, LANES:] = rope(kr_ref[...], gk_ref[1]).astype(BF16)

    blk = lambda f: pl.BlockSpec((bs, LANES), f)
    gspec = pl.BlockSpec((2, 1, LANES), lambda i, h: (0, 0, 0))
    tspec = pl.BlockSpec((bs, LANES), lambda i, h: (i, 0))
    ospec = pl.BlockSpec((bs, 2 * LANES), lambda i, h: (i, h))
    oshape = jax.ShapeDtypeStruct((s_len, n_heads * 2 * LANES), BF16)
    return pl.pallas_call(
        body, name=name, grid=(s_len // bs, n_heads),
        in_specs=[blk(lambda i, h: (i, h)), blk(lambda i, h: (i, n_heads + h)), blk(lambda i, h: (i, 2 * h)),
                  blk(lambda i, h: (i, krope_col)), gspec, gspec, tspec, tspec],
        out_specs=[ospec, ospec], out_shape=[oshape, oshape],
        compiler_params=_params("parallel", "parallel"))(qp, qp, kv, c, gq, gk, cos_t, sin_t)


def _mla_prep_bwd(qp, kv, c, gq, gk, cos_t, sin_t, dqc, dkc, dv, *, n_heads, name, bs=512):
    s_len = qp.shape[0]
    bs = _div_block(s_len, bs, 8)
    krope_col = (MLA_Q_RANK + MLA_KV_RANK) // LANES

    def body(qn_ref, qr_ref, kn_ref, kr_ref, gq_ref, gk_ref, cos_ref, sin_ref, dqc_ref, dkc_ref, dv_ref,
             dqn_ref, dqr_ref, dkv_ref, dkr_ref, dgq_ref, dgk_ref):
        i, h = pl.program_id(0), pl.program_id(1)
        cos_v, sin_v = cos_ref[...], sin_ref[...]

        @pl.when((i == 0) & (h == 0))
        def _():
            dgq_ref[...] = jnp.zeros_like(dgq_ref)
            dgk_ref[...] = jnp.zeros_like(dgk_ref)

        @pl.when(h == 0)
        def _():
            dkr_ref[...] = jnp.zeros_like(dkr_ref)

        def unrope(dy):
            dy = dy.astype(F32)
            return dy * cos_v + _rope_swap(dy * sin_v)

        dqn, dg = _rms_bwd_math(qn_ref[...].astype(F32), gq_ref[0], dqc_ref[:, :LANES].astype(F32), MLA_NOPE)
        dgq_ref[0] += jnp.sum(dg, axis=0, keepdims=True)
        dqn_ref[...] = dqn.astype(BF16)
        dqr, dg = _rms_bwd_math(qr_ref[...].astype(F32), gq_ref[1], unrope(dqc_ref[:, LANES:]), MLA_ROPE)
        dgq_ref[1] += jnp.sum(dg, axis=0, keepdims=True)
        dqr_ref[...] = dqr.astype(BF16)
        dkn, dg = _rms_bwd_math(kn_ref[...].astype(F32), gk_ref[0], dkc_ref[:, :LANES].astype(F32), MLA_NOPE)
        dgk_ref[0] += jnp.sum(dg, axis=0, keepdims=True)
        dkv_ref[:, :LANES] = dkn.astype(BF16)
        dkv_ref[:, LANES:] = dv_ref[...]
        dkr, dg = _rms_bwd_math(kr_ref[...].astype(F32), gk_ref[1], unrope(dkc_ref[:, LANES:]), MLA_ROPE)
        dgk_ref[1] += jnp.sum(dg, axis=0, keepdims=True)
        dkr_ref[...] += dkr

    blk = lambda f: pl.BlockSpec((bs, LANES), f)
    gspec = pl.BlockSpec((2, 1, LANES), lambda i, h: (0, 0, 0))
    tspec = pl.BlockSpec((bs, LANES), lambda i, h: (i, 0))
    cat = pl.BlockSpec((bs, 2 * LANES), lambda i, h: (i, h))
    head = blk(lambda i, h: (i, h))
    hshape = jax.ShapeDtypeStruct((s_len, n_heads * LANES), BF16)
    gshape = jax.ShapeDtypeStruct((2, 1, LANES), F32)
    return pl.pallas_call(
        body, name=name, grid=(s_len // bs, n_heads),
        in_specs=[head, blk(lambda i, h: (i, n_heads + h)), blk(lambda i, h: (i, 2 * h)),
                  blk(lambda i, h: (i, krope_col)), gspec, gspec, tspec, tspec, cat, cat, head],
        out_specs=[head, head, cat, tspec, gspec, gspec],
        out_shape=[hshape, hshape, jax.ShapeDtypeStruct((s_len, n_heads * 2 * LANES), BF16),
                   jax.ShapeDtypeStruct((s_len, LANES), F32), gshape, gshape],
        compiler_params=_params("arbitrary", "arbitrary"))(qp, qp, kv, c, gq, gk, cos_t, sin_t, dqc, dkc, dv)


def _mla_latent_fwd(c, ga, *, name, br=256):
    s_len = c.shape[0]
    br = _div_block(s_len, br, 8)

    def body(c_ref, g_ref, o_ref):
        for part in range(2):
            sl = slice(part * MLA_Q_RANK, (part + 1) * MLA_Q_RANK)
            o_ref[:, sl] = _head_norm(c_ref[:, sl], g_ref[:, sl]).astype(BF16)

    w = MLA_Q_RANK + MLA_KV_RANK
    return pl.pallas_call(
        body, name=name, grid=(s_len // br,),
        in_specs=[pl.BlockSpec((br, w), lambda i: (i, 0)), pl.BlockSpec((1, w), lambda i: (0, 0))],
        out_specs=pl.BlockSpec((br, w), lambda i: (i, 0)), out_shape=jax.ShapeDtypeStruct((s_len, w), BF16),
        compiler_params=_params("parallel"))(c, ga)


def _mla_latent_bwd(c, ga, dcn_q, dcn_kv, dk_rope, *, name, br=256):
    s_len, cw = c.shape
    br = _div_block(s_len, br, 8)
    w = MLA_Q_RANK + MLA_KV_RANK

    def body(c_ref, g_ref, dq_ref, dkv_ref, dkr_ref, dc_ref, dg_ref):
        @pl.when(pl.program_id(0) == 0)
        def _():
            dg_ref[...] = jnp.zeros_like(dg_ref)

        for part, d_ref in enumerate((dq_ref, dkv_ref)):
            sl = slice(part * MLA_Q_RANK, (part + 1) * MLA_Q_RANK)
            dx, dg = _rms_bwd_math(c_ref[:, sl].astype(F32), g_ref[:, sl], d_ref[...].astype(F32), MLA_Q_RANK)
            dc_ref[:, sl] = dx.astype(BF16)
            dg_ref[:, sl] += jnp.sum(dg, axis=0, keepdims=True)
        dc_ref[:, w:] = dkr_ref[...].astype(BF16)

    return pl.pallas_call(
        body, name=name, grid=(s_len // br,),
        in_specs=[pl.BlockSpec((br, w), lambda i: (i, 0)), pl.BlockSpec((1, w), lambda i: (0, 0)),
                  pl.BlockSpec((br, MLA_Q_RANK), lambda i: (i, 0)), pl.BlockSpec((br, MLA_KV_RANK), lambda i: (i, 0)),
                  pl.BlockSpec((br, LANES), lambda i: (i, 0))],
        out_specs=[pl.BlockSpec((br, cw), lambda i: (i, 0)), pl.BlockSpec((1, w), lambda i: (0, 0))],
        out_shape=[jax.ShapeDtypeStruct((s_len, cw), BF16), jax.ShapeDtypeStruct((1, w), F32)],
        compiler_params=_params("arbitrary"))(c, ga, dcn_q, dcn_kv, dk_rope)


_GELU_C = 0.7978845608028654


def _gelu(x):
    return 0.5 * x * (1.0 + jnp.tanh(_GELU_C * (x + 0.044715 * x * x * x)))


def _gelu_grad(x):
    t = jnp.tanh(_GELU_C * (x + 0.044715 * x * x * x))
    return 0.5 * (1.0 + t) + 0.5 * x * (1.0 - t * t) * _GELU_C * (1.0 + 3 * 0.044715 * x * x)


def _sgu_act_fwd(uv, vg, *, name, br=256):
    s_len, w2 = uv.shape
    w = w2 // 2
    br = _div_block(s_len, br, 8)

    def body(uv_ref, g_ref, u_ref, v_ref):
        u_ref[...] = _gelu(uv_ref[:, :w])
        v_ref[...] = _head_norm(_gelu(uv_ref[:, w:]), g_ref[...]).astype(BF16)

    row = lambda c: pl.BlockSpec((br, c), lambda i: (i, 0))
    return pl.pallas_call(
        body, name=name, grid=(s_len // br,), in_specs=[row(w2), pl.BlockSpec((1, w), lambda i: (0, 0))],
        out_specs=[row(w), row(w)], out_shape=[jax.ShapeDtypeStruct((s_len, w), F32), jax.ShapeDtypeStruct((s_len, w), BF16)],
        compiler_params=_params("parallel"))(uv, vg)


def _sgu_act_bwd(uv, vg, du, dvn, *, name, br=256):
    s_len, w2 = uv.shape
    w = w2 // 2
    br = _div_block(s_len, br, 8)

    def body(uv_ref, g_ref, du_ref, dvn_ref, duv_ref, dg_ref):
        @pl.when(pl.program_id(0) == 0)
        def _():
            dg_ref[...] = jnp.zeros_like(dg_ref)

        up, vp = uv_ref[:, :w], uv_ref[:, w:]
        duv_ref[:, :w] = (du_ref[...] * _gelu_grad(up)).astype(BF16)
        dva, dg = _rms_bwd_math(_gelu(vp), g_ref[...], dvn_ref[...], w)
        dg_ref[...] += jnp.sum(dg, axis=0, keepdims=True)
        duv_ref[:, w:] = (dva * _gelu_grad(vp)).astype(BF16)

    row = lambda c: pl.BlockSpec((br, c), lambda i: (i, 0))
    vec = pl.BlockSpec((1, w), lambda i: (0, 0))
    return pl.pallas_call(
        body, name=name, grid=(s_len // br,), in_specs=[row(w2), vec, row(w), row(w)], out_specs=[row(w2), vec],
        out_shape=[jax.ShapeDtypeStruct((s_len, w2), BF16), jax.ShapeDtypeStruct((1, w), F32)],
        compiler_params=_params("arbitrary"))(uv, vg, du, dvn)


def _tril_weights(ws_ref):
    t, s = _iota((SGU_CHUNK, SGU_CHUNK), 0), _iota((SGU_CHUNK, SGU_CHUNK), 1)
    keep = s <= t
    return jnp.where(keep, ws_ref[...], 0.0), keep


def _sgu_mix_fwd(u, vn, w_s, b_s, *, name):
    s_len, w = u.shape
    nc = s_len // SGU_CHUNK

    def body(u_ref, v_ref, ws_ref, b_ref, o_ref):
        wm = _tril_weights(ws_ref)[0].astype(BF16)
        for n in range(nc):
            rows = slice(n * SGU_CHUNK, (n + 1) * SGU_CHUNK)
            mixed = _dot(wm, v_ref[rows, :], 1, 0) + b_ref[...]
            o_ref[rows, :] = (u_ref[rows, :] * mixed).astype(BF16)

    col = pl.BlockSpec((s_len, LANES), lambda g: (0, g))
    return pl.pallas_call(
        body, name=name, grid=(w // LANES,),
        in_specs=[col, col, pl.BlockSpec((None, SGU_CHUNK, SGU_CHUNK), lambda g: (g, 0, 0)),
                  pl.BlockSpec((None, SGU_CHUNK, 1), lambda g: (g, 0, 0))],
        out_specs=col, out_shape=jax.ShapeDtypeStruct((s_len, w), BF16),
        compiler_params=_params("parallel"))(u, vn, w_s, b_s)


def _sgu_mix_bwd(u, vn, w_s, b_s, dgated, *, name):
    s_len, w = u.shape
    nc = s_len // SGU_CHUNK

    def body(u_ref, v_ref, ws_ref, b_ref, dg_ref, du_ref, dv_ref, dws_ref, dbs_ref):
        wf, keep = _tril_weights(ws_ref)
        wm = wf.astype(BF16)
        wmt = wf.T.astype(BF16)
        dws = jnp.zeros((SGU_CHUNK, SGU_CHUNK), F32)
        dbs = jnp.zeros((SGU_CHUNK, 1), F32)
        for n in range(nc):
            rows = slice(n * SGU_CHUNK, (n + 1) * SGU_CHUNK)
            vb = v_ref[rows, :]
            dgv = dg_ref[rows, :].astype(F32)
            mixed = _dot(wm, vb, 1, 0) + b_ref[...]
            du_ref[rows, :] = dgv * mixed
            dm = dgv * u_ref[rows, :]
            dmb = dm.astype(BF16)
            dws = dws + _dot(dmb, vb, 1, 1)
            dbs = dbs + jnp.sum(dm, axis=1, keepdims=True)
            dv_ref[rows, :] = _dot(wmt, dmb, 1, 0)
        dws_ref[...] = jnp.where(keep, dws, 0.0)
        dbs_ref[...] = dbs

    col = pl.BlockSpec((s_len, LANES), lambda g: (0, g))
    wspec = pl.BlockSpec((None, SGU_CHUNK, SGU_CHUNK), lambda g: (g, 0, 0))
    bspec = pl.BlockSpec((None, SGU_CHUNK, 1), lambda g: (g, 0, 0))
    return pl.pallas_call(
        body, name=name, grid=(w // LANES,), in_specs=[col, col, wspec, bspec, col],
        out_specs=[col, col, wspec, bspec],
        out_shape=[jax.ShapeDtypeStruct((s_len, w), F32), jax.ShapeDtypeStruct((s_len, w), F32),
                   jax.ShapeDtypeStruct(w_s.shape, F32), jax.ShapeDtypeStruct(b_s.shape, F32)],
        compiler_params=_params("parallel"))(u, vn, w_s, b_s, dgated)


def _shift_down(x, k):
    if k == 0:
        return x
    return jnp.where(_iota(x.shape, 0) >= k, pltpu.roll(x, k, axis=0), 0.0)


def _shift_up(x, k):
    if k == 0:
        return x
    n = x.shape[0]
    return jnp.where(_iota(x.shape, 0) < n - k, pltpu.roll(x, n - k, axis=0), 0.0)


def _conv(u, w_ref, b_ref):
    return b_ref[...] + w_ref[0:1, :] * _shift_down(u, 2) + w_ref[1:2, :] * _shift_down(u, 1) + w_ref[2:3, :] * u


def _sigmoid(x):
    return 1.0 / (1.0 + jnp.exp(-x))


def _glu_fwd(up, cw, cb, *, name, bc=256):
    s_len, f2 = up.shape
    f = f2 // 2
    bc = _div_block(f, bc)
    nf = f // bc

    def body(ug_ref, uv_ref, wg_ref, wv_ref, bg_ref, bv_ref, o_ref):
        yg = _conv(ug_ref[...], wg_ref, bg_ref)
        yv = _conv(uv_ref[...], wv_ref, bv_ref)
        o_ref[...] = (yg * _sigmoid(yg) * yv).astype(BF16)

    big = lambda off: pl.BlockSpec((s_len, bc), lambda j: (0, j + off))
    wsp = lambda off: pl.BlockSpec((3, bc), lambda j: (0, j + off))
    bsp = lambda off: pl.BlockSpec((1, bc), lambda j: (0, j + off))
    return pl.pallas_call(
        body, name=name, grid=(nf,), in_specs=[big(0), big(nf), wsp(0), wsp(nf), bsp(0), bsp(nf)],
        out_specs=pl.BlockSpec((s_len, bc), lambda j: (0, j)), out_shape=jax.ShapeDtypeStruct((s_len, f), BF16),
        compiler_params=_params("parallel"))(up, up, cw, cw, cb, cb)


def _glu_bwd(up, cw, cb, dact, *, name, bc=256):
    s_len, f2 = up.shape
    f = f2 // 2
    bc = _div_block(f, bc)
    nf = f // bc

    def body(ug_ref, uv_ref, wg_ref, wv_ref, bg_ref, bv_ref, da_ref, du_ref, dw_ref, db_ref):
        ug, uv = ug_ref[...], uv_ref[...]
        yg = _conv(ug, wg_ref, bg_ref)
        yv = _conv(uv, wv_ref, bv_ref)
        da = da_ref[...].astype(F32)
        sg = _sigmoid(yg)
        planes = ((da * yv * (sg * (1.0 + yg * (1.0 - sg))), ug, wg_ref), (da * (yg * sg), uv, wv_ref))
        for plane, (dy, u, w_ref) in enumerate(planes):
            db_ref[plane] = jnp.sum(dy, axis=0, keepdims=True)
            dw_ref[plane, 0:1, :] = jnp.sum(dy * _shift_down(u, 2), axis=0, keepdims=True)
            dw_ref[plane, 1:2, :] = jnp.sum(dy * _shift_down(u, 1), axis=0, keepdims=True)
            dw_ref[plane, 2:3, :] = jnp.sum(dy * u, axis=0, keepdims=True)
            du = w_ref[2:3, :] * dy + w_ref[1:2, :] * _shift_up(dy, 1) + w_ref[0:1, :] * _shift_up(dy, 2)
            du_ref[plane] = du.astype(BF16)

    big = lambda off: pl.BlockSpec((s_len, bc), lambda j: (0, j + off))
    wsp = lambda off: pl.BlockSpec((3, bc), lambda j: (0, j + off))
    bsp = lambda off: pl.BlockSpec((1, bc), lambda j: (0, j + off))
    planes = lambda r: pl.BlockSpec((2, r, bc), lambda j: (0, 0, j))
    return pl.pallas_call(
        body, name=name, grid=(nf,),
        in_specs=[big(0), big(nf), wsp(0), wsp(nf), bsp(0), bsp(nf), pl.BlockSpec((s_len, bc), lambda j: (0, j))],
        out_specs=[planes(s_len), planes(3), planes(1)],
        out_shape=[jax.ShapeDtypeStruct((2, s_len, f), BF16), jax.ShapeDtypeStruct((2, 3, f), F32),
                   jax.ShapeDtypeStruct((2, 1, f), F32)],
        compiler_params=_params("parallel"))(up, up, cw, cw, cb, cb, dact)


def _as2d(a):
    return a.reshape(-1, a.shape[-1]) if a.ndim >= 2 else a.reshape(1, -1)


def _adamw(w, g, m, v, *, name, target_bytes=1 << 20):
    shape = w.shape
    w2, m2, v2 = _as2d(w), _as2d(m), _as2d(v)
    g2 = g.reshape(w2.shape)
    r, c = w2.shape
    br = r if r * c * 4 <= target_bytes else _div_block(r, max(8, target_bytes // (4 * c) // 8 * 8), 8)
    c1 = 1.0 - ADAM_B1 ** ADAM_STEP
    c2 = 1.0 - ADAM_B2 ** ADAM_STEP

    def body(w_ref, g_ref, m_ref, v_ref, d_ref, nm_ref, nv_ref):
        gv = g_ref[...]
        nm = ADAM_B1 * m_ref[...] + (1.0 - ADAM_B1) * gv
        nv = ADAM_B2 * v_ref[...] + (1.0 - ADAM_B2) * (gv * gv)
        nm_ref[...] = nm
        nv_ref[...] = nv
        d_ref[...] = -ADAM_LR * ((nm / c1) / (jnp.sqrt(nv / c2) + ADAM_EPS) + ADAM_WD * w_ref[...])

    spec = pl.BlockSpec((br, c), lambda i: (i, 0))
    sds = jax.ShapeDtypeStruct((r, c), F32)
    d, nm, nv = pl.pallas_call(
        body, name=name, grid=(r // br,), in_specs=[spec] * 4, out_specs=[spec] * 3, out_shape=[sds] * 3,
        compiler_params=_params("parallel"))(w2, g2, m2, v2)
    return d.reshape(shape), nm.reshape(shape), nv.reshape(shape)


def _add_halves(g, recv, place, *, name, target_bytes=1 << 20):
    _, _, r, c = g.shape
    br = _div_block(r, max(16, target_bytes // (2 * c) // 16 * 16), 16)

    def body(x_ref, y_ref, c_ref, g_ref, r_ref, o_ref):
        o_ref[...] = (g_ref[...].astype(F32) + r_ref[...].astype(F32)).astype(BF16)

    return pl.pallas_call(
        body, name=name,
        grid_spec=pltpu.PrefetchScalarGridSpec(
            num_scalar_prefetch=3, grid=(N_CHIPS, r // br),
            in_specs=[pl.BlockSpec((None, None, br, c), lambda s, i, xr, yr, cr: (s, cr[0], i, 0)),
                      pl.BlockSpec((None, br, c), lambda s, i, xr, yr, cr: (s, i, 0))],
            out_specs=pl.BlockSpec((None, br, c), lambda s, i, xr, yr, cr: (s, i, 0))),
        out_shape=jax.ShapeDtypeStruct((N_CHIPS, r, c), BF16),
        compiler_params=_params("parallel", "parallel"))(*place, g, recv)


def _sum_chips(p, landed, place, *, name, target_bytes=1 << 20):
    _, r, c = p.shape
    br = _div_block(r, max(16, target_bytes // (4 * c) // 16 * 16), 16)

    def body(x_ref, y_ref, c_ref, p_ref, l1_ref, l2_ref, l3_ref, o_ref):
        o_ref[...] = ((p_ref[...].astype(F32) + l1_ref[...].astype(F32)) + l2_ref[...].astype(F32)) + l3_ref[...].astype(F32)

    slot = lambda k: pl.BlockSpec((None, br, c), lambda i, xr, yr, cr: ((2 * xr[0] + yr[0] + k) % N_CHIPS, i, 0))
    return pl.pallas_call(
        body, name=name,
        grid_spec=pltpu.PrefetchScalarGridSpec(
            num_scalar_prefetch=3, grid=(r // br,), in_specs=[slot(0), slot(1), slot(2), slot(3)],
            out_specs=pl.BlockSpec((None, br, c), lambda i, xr, yr, cr: (cr[0], i, 0))),
        out_shape=jax.ShapeDtypeStruct((2, r, c), F32),
        compiler_params=_params("parallel"))(*place, p, landed, landed, landed)


def _place_shard(w, place, *, dtype, name, target_bytes=1 << 20):
    r, c = w.shape
    hr = r // 2
    mult = 16 if dtype == BF16 else 8
    br = _div_block(hr, max(mult, target_bytes // (4 * c) // mult * mult), mult)
    nb = hr // br

    def body(x_ref, y_ref, c_ref, w_ref, o_ref):
        o_ref[...] = w_ref[...].astype(dtype)

    return pl.pallas_call(
        body, name=name,
        grid_spec=pltpu.PrefetchScalarGridSpec(
            num_scalar_prefetch=3, grid=(2, nb),
            in_specs=[pl.BlockSpec((br, c), lambda h, i, xr, yr, cr: (h * nb + i, 0))],
            out_specs=pl.BlockSpec((None, None, br, c), lambda h, i, xr, yr, cr: (2 * xr[0] + yr[0], h, i, 0))),
        out_shape=jax.ShapeDtypeStruct((N_CHIPS, 2, hr, c), dtype),
        compiler_params=_params("parallel", "parallel"))(*place, w)


def _sum_devices(x, *, name):
    n, r, c = x.shape
    br = _div_block(r, 512, 8)

    def body(x_ref, o_ref):
        acc = x_ref[0]
        for s in range(1, n):
            acc = acc + x_ref[s]
        o_ref[...] = acc

    return pl.pallas_call(
        body, name=name, grid=(r // br,), in_specs=[pl.BlockSpec((n, br, c), lambda i: (0, i, 0))],
        out_specs=pl.BlockSpec((br, c), lambda i: (i, 0)), out_shape=jax.ShapeDtypeStruct((r, c), F32),
        compiler_params=_params("parallel"))(x)


_ANY = pl.BlockSpec(memory_space=pl.ANY)


def _place():
    x, y, c = lax.axis_index("x"), lax.axis_index("y"), lax.axis_index("c")
    other_chips = [(1 - x, y), (x, 1 - y), (1 - x, 1 - y)]
    return x, y, c, other_chips


_HBM = pl.BlockSpec(memory_space=pltpu.HBM)
_SEM = pl.BlockSpec(memory_space=pltpu.SEMAPHORE)
_EFFECT = pltpu.SideEffectType.DATAFLOW_SIDE_EFFECTING


def _in_hbm(a):
    return pltpu.with_memory_space_constraint(a, pltpu.HBM)


def _token_spec():
    return pl.BlockSpec(memory_space=pltpu.VMEM), jax.ShapeDtypeStruct((8, LANES), F32)


def _gather_ici_start(bufs, after, *, name):
    n = len(bufs)

    def body(*refs):
        b_refs = refs[:n]
        send_sems, recv_sems = refs[n + 1], refs[n + 2]
        token = refs[-1]
        x, y, c, chips = _place()
        me = 2 * x + y
        for i in range(n):
            for j, (px, py) in enumerate(chips):
                pltpu.make_async_remote_copy(src_ref=b_refs[i].at[me, c], dst_ref=b_refs[i].at[me, c],
                                             send_sem=send_sems.at[3 * i + j], recv_sem=recv_sems.at[3 * i + j],
                                             device_id=(px, py, c), device_id_type=MESH).start()
        token[...] = jnp.zeros_like(token)

    tspec, tshape = _token_spec()
    outs = pl.pallas_call(
        body, name=name, in_specs=[_HBM] * n + [_ANY], out_specs=(_SEM, _SEM, *[_HBM] * n, tspec),
        out_shape=(pltpu.SemaphoreType.DMA((3 * n,)), pltpu.SemaphoreType.DMA((3 * n,)),
                   *[pltpu.HBM(a.shape, a.dtype) for a in bufs], tshape),
        input_output_aliases={i: 2 + i for i in range(n)},
        compiler_params=pltpu.CompilerParams(has_side_effects=_EFFECT),
    )(*[_in_hbm(a) for a in bufs], after)
    return outs[0], outs[1], list(outs[2:2 + n]), outs[-1]


def _gather_ici_wait(send_sems, recv_sems, bufs, after, *, name):
    n = len(bufs)

    def body(*refs):
        b_refs = refs[:n]
        send_sems, recv_sems = refs[n], refs[n + 1]
        x, y, c, chips = _place()
        me = 2 * x + y
        for i in range(n):
            for j, (px, py) in enumerate(chips):
                cp = pltpu.make_async_remote_copy(src_ref=b_refs[i].at[me, c], dst_ref=b_refs[i].at[2 * px + py, c],
                                                  send_sem=send_sems.at[3 * i + j], recv_sem=recv_sems.at[3 * i + j],
                                                  device_id=(px, py, c), device_id_type=MESH)
                cp.wait_send()
                cp.wait_recv()

    outs = pl.pallas_call(
        body, name=name, in_specs=[_HBM] * n + [_SEM, _SEM, _ANY], out_specs=[_HBM] * n,
        out_shape=[pltpu.HBM(a.shape, a.dtype) for a in bufs], input_output_aliases={i: i for i in range(n)},
        compiler_params=pltpu.CompilerParams(has_side_effects=_EFFECT),
    )(*bufs, send_sems, recv_sems, after)
    return list(outs)


def _gather_d2d(bufs, *, name):
    n = len(bufs)

    def body(*refs):
        b_refs = refs[n:2 * n]
        send_sems, recv_sems = refs[2 * n:]
        x, y, c, chips = _place()
        sends = []
        for i in range(n):
            for j, (px, py) in enumerate(chips):
                mine = b_refs[i].at[2 * px + py, c]
                cp = pltpu.make_async_remote_copy(src_ref=mine, dst_ref=mine, send_sem=send_sems.at[3 * i + j],
                                                  recv_sem=recv_sems.at[3 * i + j], device_id=(x, y, 1 - c),
                                                  device_id_type=MESH)
                cp.start()
                sends.append((cp, i, j, px, py))
        for cp, i, j, px, py in sends:
            theirs = b_refs[i].at[2 * px + py, 1 - c]
            pltpu.make_async_remote_copy(src_ref=theirs, dst_ref=theirs, send_sem=send_sems.at[3 * i + j],
                                         recv_sem=recv_sems.at[3 * i + j], device_id=(x, y, 1 - c),
                                         device_id_type=MESH).wait_recv()
            cp.wait_send()

    return pl.pallas_call(
        body, name=name, in_specs=[_ANY] * n, out_specs=[_ANY] * n, input_output_aliases={i: i for i in range(n)},
        out_shape=[jax.ShapeDtypeStruct(a.shape, a.dtype) for a in bufs],
        scratch_shapes=[pltpu.SemaphoreType.DMA((3 * n,)), pltpu.SemaphoreType.DMA((3 * n,))],
    )(*bufs)


def _sibling_halves(gs, *, name):
    n = len(gs)

    def body(*refs):
        g_refs, o_refs = refs[:n], refs[n:2 * n]
        send_sems, recv_sems = refs[2 * n:]
        x, y, c, _ = _place()
        copies = []
        for i in range(n):
            for s in range(N_CHIPS):
                k = i * N_CHIPS + s
                cp = pltpu.make_async_remote_copy(src_ref=g_refs[i].at[s, 1 - c], dst_ref=o_refs[i].at[s],
                                                  send_sem=send_sems.at[k], recv_sem=recv_sems.at[k],
                                                  device_id=(x, y, 1 - c), device_id_type=MESH)
                cp.start()
                copies.append(cp)
        for cp in copies:
            cp.wait()

    return pl.pallas_call(
        body, name=name, in_specs=[_ANY] * n, out_specs=[_ANY] * n,
        out_shape=[jax.ShapeDtypeStruct((N_CHIPS,) + g.shape[2:], g.dtype) for g in gs],
        scratch_shapes=[pltpu.SemaphoreType.DMA((n * N_CHIPS,)), pltpu.SemaphoreType.DMA((n * N_CHIPS,))],
    )(*gs)


def _chip_scatter_start(ps, after, *, name):
    n = len(ps)

    def body(*refs):
        p_refs, l_refs = refs[:n], refs[n:2 * n]
        send_sems, recv_sems = refs[2 * n + 1], refs[2 * n + 2]
        token = refs[-1]
        x, y, c, chips = _place()
        me = 2 * x + y
        for i in range(n):
            for j, (px, py) in enumerate(chips):
                pltpu.make_async_remote_copy(src_ref=p_refs[i].at[2 * px + py], dst_ref=l_refs[i].at[me],
                                             send_sem=send_sems.at[3 * i + j], recv_sem=recv_sems.at[3 * i + j],
                                             device_id=(px, py, c), device_id_type=MESH).start()
        token[...] = jnp.zeros_like(token)

    tspec, tshape = _token_spec()
    lands = [_in_hbm(lax.empty(p.shape, p.dtype)) for p in ps]
    outs = pl.pallas_call(
        body, name=name, in_specs=[_HBM] * (2 * n) + [_ANY], out_specs=(_SEM, _SEM, *[_HBM] * (2 * n), tspec),
        out_shape=(pltpu.SemaphoreType.DMA((3 * n,)), pltpu.SemaphoreType.DMA((3 * n,)),
                   *[pltpu.HBM(p.shape, p.dtype) for p in ps], *[pltpu.HBM(p.shape, p.dtype) for p in ps], tshape),
        input_output_aliases={i: 2 + i for i in range(2 * n)},
        compiler_params=pltpu.CompilerParams(has_side_effects=_EFFECT),
    )(*[_in_hbm(p) for p in ps], *lands, after)
    return outs[0], outs[1], list(outs[2:2 + n]), list(outs[2 + n:2 + 2 * n]), outs[-1]


def _chip_scatter_wait(send_sems, recv_sems, ps, lands, after, *, name):
    n = len(ps)

    def body(*refs):
        p_refs, l_refs = refs[:n], refs[n:2 * n]
        send_sems, recv_sems = refs[2 * n], refs[2 * n + 1]
        x, y, c, chips = _place()
        for i in range(n):
            for j, (px, py) in enumerate(chips):
                cp = pltpu.make_async_remote_copy(src_ref=p_refs[i].at[2 * px + py], dst_ref=l_refs[i].at[2 * px + py],
                                                  send_sem=send_sems.at[3 * i + j], recv_sem=recv_sems.at[3 * i + j],
                                                  device_id=(px, py, c), device_id_type=MESH)
                cp.wait_send()
                cp.wait_recv()

    outs = pl.pallas_call(
        body, name=name, in_specs=[_HBM] * (2 * n) + [_SEM, _SEM, _ANY], out_specs=[_HBM] * (2 * n),
        out_shape=[pltpu.HBM(p.shape, p.dtype) for p in ps] * 2, input_output_aliases={i: i for i in range(2 * n)},
        compiler_params=pltpu.CompilerParams(has_side_effects=_EFFECT),
    )(*ps, *lands, send_sems, recv_sems, after)
    return list(outs[:n]), list(outs[n:])


def _sibling_share(bufs, *, name):
    n = len(bufs)

    def body(*refs):
        b_refs = refs[n:2 * n]
        send_sems, recv_sems = refs[2 * n:]
        x, y, c, _ = _place()
        copies = []
        for i in range(n):
            cp = pltpu.make_async_remote_copy(src_ref=b_refs[i].at[c], dst_ref=b_refs[i].at[c], send_sem=send_sems.at[i],
                                              recv_sem=recv_sems.at[i], device_id=(x, y, 1 - c), device_id_type=MESH)
            cp.start()
            copies.append((cp, i))
        for cp, i in copies:
            theirs = b_refs[i].at[1 - c]
            pltpu.make_async_remote_copy(src_ref=theirs, dst_ref=theirs, send_sem=send_sems.at[i],
                                         recv_sem=recv_sems.at[i], device_id=(x, y, 1 - c),
                                         device_id_type=MESH).wait_recv()
            cp.wait_send()

    return pl.pallas_call(
        body, name=name, in_specs=[_ANY] * n, out_specs=[_ANY] * n, input_output_aliases={i: i for i in range(n)},
        out_shape=[jax.ShapeDtypeStruct(a.shape, a.dtype) for a in bufs],
        scratch_shapes=[pltpu.SemaphoreType.DMA((n,)), pltpu.SemaphoreType.DMA((n,))],
    )(*bufs)


def _broadcast_all(v, *, name):
    def body(v_ref, o_ref, send_sems, recv_sems, local_sem):
        x, y, c, _ = _place()
        me = 4 * x + 2 * y + c
        loc = pltpu.make_async_copy(v_ref, o_ref.at[me], local_sem)
        loc.start()
        copies = []
        for k in range(1, 8):
            dx, dy, dc = (k >> 2) & 1, (k >> 1) & 1, k & 1
            to = (1 - x if dx else x, 1 - y if dy else y, 1 - c if dc else c)
            cp = pltpu.make_async_remote_copy(src_ref=v_ref, dst_ref=o_ref.at[me], send_sem=send_sems.at[k - 1],
                                              recv_sem=recv_sems.at[k - 1], device_id=to, device_id_type=MESH)
            cp.start()
            copies.append((cp, k, to))
        for cp, k, to in copies:
            cp.wait_send()
            theirs = o_ref.at[4 * to[0] + 2 * to[1] + to[2]]
            pltpu.make_async_remote_copy(src_ref=theirs, dst_ref=theirs, send_sem=send_sems.at[k - 1],
                                         recv_sem=recv_sems.at[k - 1], device_id=to, device_id_type=MESH).wait_recv()
        loc.wait()

    return pl.pallas_call(
        body, name=name, in_specs=[_ANY], out_specs=_ANY,
        out_shape=jax.ShapeDtypeStruct((8,) + v.shape, v.dtype),
        scratch_shapes=[pltpu.SemaphoreType.DMA((7,)), pltpu.SemaphoreType.DMA((7,)), pltpu.SemaphoreType.DMA(())],
    )(v)


def _gather_begin(shards, place, after, *, name):
    names = list(shards)
    bufs = [_place_shard(shards[k], place, dtype=F32 if k == 'small' else BF16, name=f"{name}_place_{k}") for k in names]
    send_sems, recv_sems, bufs, token = _gather_ici_start(bufs, after, name=name + "_ici_start")
    return (names, [shards[k].shape for k in names], send_sems, recv_sems, bufs), token


def _gather_end(state, after, *, name):
    names, shapes, send_sems, recv_sems, bufs = state
    bufs = _gather_ici_wait(send_sems, recv_sems, bufs, after, name=name + "_ici_wait")
    bufs = _gather_d2d(bufs, name=name + "_d2d")
    return {k: o.reshape((N_CHIPS,) + sh) for k, o, sh in zip(names, bufs, shapes)}


def _reduce_begin(grads, place, *, name):
    names = list(grads)
    gs = [grads[k].reshape(N_CHIPS, 2, grads[k].shape[1] // 2, grads[k].shape[2]) for k in names]
    recv = _sibling_halves(gs, name=name + "_sib")
    ps = [_add_halves(g, r, place, name=f"{name}_add2_{k}") for g, r, k in zip(gs, recv, names)]
    send_sems, recv_sems, ps, lands, token = _chip_scatter_start(ps, recv[0], name=name + "_scatter_start")
    return (names, [grads[k].shape[1:] for k in names], send_sems, recv_sems, ps, lands), token


def _reduce_end(state, place, after, *, name):
    names, shapes, send_sems, recv_sems, ps, lands = state
    ps, lands = _chip_scatter_wait(send_sems, recv_sems, ps, lands, after, name=name + "_scatter_wait")
    rs = [_sum_chips(p, l, place, name=f"{name}_sum4_{k}") for p, l, k in zip(ps, lands, names)]
    both = _sibling_share(rs, name=name + "_share")
    return {k: b.reshape(sh) for k, b, sh in zip(names, both, shapes)}


def _pad_lanes(a, n=LANES):
    return jnp.pad(a, [(0, 0)] * (a.ndim - 1) + [(0, n - a.shape[-1])])


def _unshard_cols(g):
    return jnp.transpose(g, (1, 0, 2)).reshape(g.shape[1], -1)


def _shard_cols(w):
    k, n = w.shape
    return jnp.transpose(w.reshape(k, N_CHIPS, n // N_CHIPS), (1, 0, 2))


def _ffn_fwd(h, p, tag):
    b = _rms_fwd(h, p['ffn_norm'], name=f"{tag}_ffn_norm")
    up = _mm(b, p['ffn_w_up'], b_sh='n', name=f"{tag}_ffn_up", bn=1408)
    act = _glu_fwd(up, p['ffn_conv_w'], p['ffn_conv_b'], name=f"{tag}_ffn_glu")
    out = _mm(act, p['ffn_w_down'], res=h, name=f"{tag}_ffn_down", bk=704)
    return out, (h, b, up, act)


def _ffn_bwd(dh, saved, p, tag):
    h, b, up, act = saved
    dact = _mm(dh, p['ffn_w_down'], tb=True, out_dtype=BF16, name=f"{tag}_ffn_dact", bn=1408)
    dw_down = _mm(act, dh, ta=True, out_dtype=BF16, name=f"{tag}_ffn_dwdown", bm=1408)
    dup, dcw, dcb = _glu_bwd(up, p['ffn_conv_w'], p['ffn_conv_b'], dact, name=f"{tag}_ffn_dglu")
    dw_up = _mm(b, dup, ta=True, b_sh='n', o_sh=True, out_dtype=BF16, name=f"{tag}_ffn_dwup", bn=1408)
    db = _mm(dup, p['ffn_w_up'], a_sh=True, b_sh='k', name=f"{tag}_ffn_db", bk=1408)
    dcw = jnp.transpose(dcw, (1, 0, 2)).reshape(dcw.shape[1], -1)
    dcb = dcb.reshape(1, -1)
    dh_in, dg = _rms_bwd(h, p['ffn_norm'], db, res=dh, name=f"{tag}_ffn_dnorm")
    big = {'ffn_w_up': dw_up, 'ffn_w_down': dw_down.reshape(N_CHIPS, -1, dw_down.shape[1])}
    small = {'ffn_norm': dg, 'ffn_conv_w': dcw, 'ffn_conv_b': dcb}
    return dh_in, big, small


def _qkv_attn_fwd(kind, h, p, tag, n_heads):
    a = _rms_fwd(h, p['mix_norm'], name=f"{tag}_norm")
    if kind == 'fox':
        qkv = _mm(a, p['w_in'], name=f"{tag}_qkv", bn=896)
        cum = _fgate_fwd(qkv, p['b_f'], fcol=3 * n_heads, name=f"{tag}_fgate")
        cum_t = cum[:, :n_heads].T
        cq, ck = cum_t[:, :, None], cum_t[:, None, :]
    else:
        qkv = _mm(a, p['w_in'], b_sh='n', name=f"{tag}_qkv", bn=768)
        cq = ck = None
    cols = dict(qcol=lambda hh: hh, kcol=lambda hh: n_heads + hh, vcol=lambda hh: 2 * n_heads + hh)
    o = _attn_fwd(kind, qkv, qkv, qkv, name=f"{tag}_attn", n_heads=n_heads, dqk=HEAD_DIM, scale=HEAD_DIM ** -0.5,
                  gains=p['qk_gain'], cq=cq, ck=ck, **cols)
    out = _mm(o, p['w_out'], res=h, name=f"{tag}_out")
    return out, (h, a, qkv, o, cq, ck)


def _qkv_attn_bwd(kind, dh, saved, p, tag, n_heads):
    h, a, qkv, o, cq, ck = saved
    do = _mm(dh, p['w_out'], tb=True, out_dtype=BF16, name=f"{tag}_do")
    dw_out = _mm(o, dh, ta=True, out_dtype=BF16, name=f"{tag}_dwout")
    cols = dict(qcol=lambda hh: hh, kcol=lambda hh: n_heads + hh, vcol=lambda hh: 2 * n_heads + hh)
    outs = _attn_bwd(kind, qkv, qkv, qkv, o, do, name=f"{tag}_dattn", n_heads=n_heads, dqk=HEAD_DIM,
                     scale=HEAD_DIM ** -0.5, gains=p['qk_gain'], cq=cq, ck=ck, **cols)
    dq, dk, dv, dgain = outs[:4]
    small = {'q_gain': dgain[0], 'k_gain': dgain[1]}
    if kind == 'fox':
        dcq, dck = outs[4:]
        dca = _pad_lanes(dcq[:, :, 0].T)
        dcb = _pad_lanes(dck[:, 0, :].T)
        dflog, dbf = _fgate_bwd(qkv, p['b_f'], dca, dcb, fcol=3 * n_heads, n_heads=n_heads, name=f"{tag}_dfgate")
        small['b_f'] = dbf[:, :n_heads]
        dqkv = jnp.concatenate([dq, dk, dv, dflog], axis=1)
        dw_in = _mm(a, dqkv, ta=True, out_dtype=BF16, name=f"{tag}_dwin", bn=896)
        da = _mm(dqkv, p['w_in'], tb=True, name=f"{tag}_da", bk=896)
        dw_in = _shard_cols(dw_in[:, :3 * n_heads * HEAD_DIM + n_heads])
    else:
        dqkv = jnp.concatenate([dq, dk, dv], axis=1)
        dw_in = _mm(a, dqkv, ta=True, o_sh=True, out_dtype=BF16, name=f"{tag}_dwin", bn=768)
        da = _mm(dqkv, p['w_in'], b_sh='k', name=f"{tag}_da", bk=768)
    dh_in, dg = _rms_bwd(h, p['mix_norm'], da, res=dh, name=f"{tag}_dnorm")
    small['mix_norm'] = dg
    big = {'w_in': dw_in, 'w_out': dw_out.reshape(N_CHIPS, -1, dw_out.shape[1])}
    return dh_in, big, small


def _mla_fwd(h, p, tag, n_heads):
    a = _rms_fwd(h, p['mix_norm'], name=f"{tag}_norm")
    c = _mm(a, p['w_in'], name=f"{tag}_latent", bn=1152)
    cn = _mla_latent_fwd(c, p['a_gain'], name=f"{tag}_latent_norm")
    qp = _mm(cn[:, :MLA_Q_RANK], p['w_q_b'], name=f"{tag}_q_up")
    kv = _mm(cn[:, MLA_Q_RANK:], p['w_kv_b'], b_sh='n', name=f"{tag}_kv_up")
    qc, kc = _mla_prep_fwd(qp, kv, c, p['gq'], p['gk'], p['cos'], p['sin'], n_heads=n_heads, name=f"{tag}_prep")
    cols = dict(qcol=lambda hh: hh, kcol=lambda hh: hh, vcol=lambda hh: 2 * hh + 1)
    scale = (MLA_NOPE + MLA_ROPE) ** -0.5
    o = _attn_fwd('mla', qc, kc, kv, name=f"{tag}_attn", n_heads=n_heads, dqk=2 * LANES, scale=scale, **cols)
    out = _mm(o, p['w_out'], res=h, name=f"{tag}_out")
    return out, (h, a, c, cn, qp, kv, qc, kc, o)


def _mla_bwd(dh, saved, p, tag, n_heads):
    h, a, c, cn, qp, kv, qc, kc, o = saved
    do = _mm(dh, p['w_out'], tb=True, out_dtype=BF16, name=f"{tag}_do")
    dw_out = _mm(o, dh, ta=True, out_dtype=BF16, name=f"{tag}_dwout")
    cols = dict(qcol=lambda hh: hh, kcol=lambda hh: hh, vcol=lambda hh: 2 * hh + 1)
    scale = (MLA_NOPE + MLA_ROPE) ** -0.5
    dqc, dkc, dv = _attn_bwd('mla', qc, kc, kv, o, do, name=f"{tag}_dattn", n_heads=n_heads, dqk=2 * LANES,
                             scale=scale, **cols)
    dqn, dqr, dkv, dkr, dgq, dgk = _mla_prep_bwd(qp, kv, c, p['gq'], p['gk'], p['cos'], p['sin'], dqc, dkc, dv,
                                                 n_heads=n_heads, name=f"{tag}_dprep")
    dqp = jnp.concatenate([dqn, dqr], axis=1)
    cn_q, cn_kv = cn[:, :MLA_Q_RANK], cn[:, MLA_Q_RANK:]
    dw_q_b = _mm(cn_q, dqp, ta=True, out_dtype=BF16, name=f"{tag}_dwqb", bm=512)
    dcn_q = _mm(dqp, p['w_q_b'], tb=True, out_dtype=BF16, name=f"{tag}_dcnq")
    dw_kv_b = _mm(cn_kv, dkv, ta=True, o_sh=True, out_dtype=BF16, name=f"{tag}_dwkvb", bm=512)
    dcn_kv = _mm(dkv, p['w_kv_b'], b_sh='k', out_dtype=BF16, name=f"{tag}_dcnkv")
    dc, dga = _mla_latent_bwd(c, p['a_gain'], dcn_q, dcn_kv, dkr, name=f"{tag}_dlatent")
    dw_in = _mm(a, dc, ta=True, out_dtype=BF16, name=f"{tag}_dwin", bn=1152)
    da = _mm(dc, p['w_in'], tb=True, name=f"{tag}_da", bk=1152)
    dh_in, dg = _rms_bwd(h, p['mix_norm'], da, res=dh, name=f"{tag}_dnorm")
    k_rank = dw_q_b.shape[0]
    nope = dw_q_b[:, :n_heads * LANES].reshape(k_rank, n_heads, LANES)
    rope = dw_q_b[:, n_heads * LANES:].reshape(k_rank, n_heads, LANES)[:, :, :MLA_ROPE]
    dw_q_b = jnp.concatenate([nope, rope], axis=2).reshape(k_rank, n_heads * (MLA_NOPE + MLA_ROPE))
    w_in_cols = MLA_Q_RANK + MLA_KV_RANK + MLA_ROPE
    big = {'w_in': dw_in[:, :w_in_cols].reshape(N_CHIPS, -1, w_in_cols), 'w_q_b': _shard_cols(dw_q_b),
           'w_kv_b': dw_kv_b, 'w_out': dw_out.reshape(N_CHIPS, -1, dw_out.shape[1])}
    small = {'mix_norm': dg, 'q_a_gain': dga[:, :MLA_Q_RANK], 'kv_a_gain': dga[:, MLA_Q_RANK:],
             'q_gain': jnp.concatenate([dgq[0], dgq[1][:, :MLA_ROPE]], axis=1),
             'k_gain': jnp.concatenate([dgk[0], dgk[1][:, :MLA_ROPE]], axis=1)}
    return dh_in, big, small


def _sgu_fwd(h, p, tag):
    a = _rms_fwd(h, p['mix_norm'], name=f"{tag}_norm")
    uv = _mm(a, p['w_in'], b_sh='n', name=f"{tag}_in")
    u, vn = _sgu_act_fwd(uv, p['v_gain'], name=f"{tag}_act")
    gated = _sgu_mix_fwd(u, vn, p['w_s'], p['b_s'], name=f"{tag}_mix")
    out = _mm(gated, p['w_out'], res=h, name=f"{tag}_out")
    return out, (h, a, uv, u, vn, gated)


def _sgu_bwd(dh, saved, p, tag):
    h, a, uv, u, vn, gated = saved
    dgated = _mm(dh, p['w_out'], tb=True, out_dtype=BF16, name=f"{tag}_dgated")
    dw_out = _mm(gated, dh, ta=True, out_dtype=BF16, name=f"{tag}_dwout")
    du, dvn, dws, dbs = _sgu_mix_bwd(u, vn, p['w_s'], p['b_s'], dgated, name=f"{tag}_dmix")
    duv, dvg = _sgu_act_bwd(uv, p['v_gain'], du, dvn, name=f"{tag}_dact")
    dw_in = _mm(a, duv, ta=True, o_sh=True, out_dtype=BF16, name=f"{tag}_dwin")
    da = _mm(duv, p['w_in'], b_sh='k', name=f"{tag}_da")
    dh_in, dg = _rms_bwd(h, p['mix_norm'], da, res=dh, name=f"{tag}_dnorm")
    big = {'w_in': dw_in, 'w_out': dw_out.reshape(N_CHIPS, -1, dw_out.shape[1])}
    small = {'mix_norm': dg, 'v_gain': dvg, 'w_s': dws, 'b_s': dbs[:, :, 0]}
    return dh_in, big, small


def _pack(parts):
    flat = jnp.concatenate([p.reshape(-1).astype(F32) for p in parts])
    rows = -(-flat.shape[0] // LANES)
    rows = -(-rows // 32) * 32
    return jnp.pad(flat, (0, rows * LANES - flat.shape[0])).reshape(rows, LANES)


def _unpack(packed, shapes):
    flat = packed.reshape(-1)
    out, off = [], 0
    for s in shapes:
        n = 1
        for d in s:
            n *= d
        out.append(flat[off:off + n].reshape(s))
        off += n
    return out


MIXERS = ('fox', 'mla', 'sb', 'sgu')
WEIGHT_NAMES = ['mix_norm', 'ffn_norm', 'fox_w_in', 'fox_b_f', 'fox_q_gain', 'fox_k_gain', 'fox_w_out', 'mla_w_in',
                'mla_q_a_gain', 'mla_kv_a_gain', 'mla_w_q_b', 'mla_w_kv_b', 'mla_q_gain', 'mla_k_gain', 'mla_w_out',
                'sb_w_in', 'sb_q_gain', 'sb_k_gain', 'sb_w_out', 'sgu_w_in', 'sgu_v_gain', 'sgu_w_s', 'sgu_b_s',
                'sgu_w_out', 'ffn_w_up', 'ffn_conv_w', 'ffn_conv_b', 'ffn_w_down']
SMALL_SHARDED = {'mla_q_a_gain': 1, 'mla_kv_a_gain': 1, 'sgu_v_gain': 1, 'ffn_conv_w': 2}
BIG = ['fox_w_in', 'fox_w_out', 'mla_w_in', 'mla_w_q_b', 'mla_w_kv_b', 'mla_w_out', 'sb_w_in', 'sb_w_out', 'sgu_w_in',
       'sgu_w_out', 'ffn_w_up', 'ffn_w_down']


def kernel(x, positions, mix_norm, ffn_norm, fox_w_in, fox_b_f, fox_q_gain, fox_k_gain, fox_w_out, mla_w_in, mla_q_a_gain, mla_kv_a_gain, mla_w_q_b, mla_w_kv_b, mla_q_gain, mla_k_gain, mla_w_out, sb_w_in, sb_q_gain, sb_k_gain, sb_w_out, sgu_w_in, sgu_v_gain, sgu_w_s, sgu_b_s, sgu_w_out, ffn_w_up, ffn_conv_w, ffn_conv_b, ffn_w_down, loss_target, m_mix_norm, m_ffn_norm, m_fox_w_in, m_fox_b_f, m_fox_q_gain, m_fox_k_gain, m_fox_w_out, m_mla_w_in, m_mla_q_a_gain, m_mla_kv_a_gain, m_mla_w_q_b, m_mla_w_kv_b, m_mla_q_gain, m_mla_k_gain, m_mla_w_out, m_sb_w_in, m_sb_q_gain, m_sb_k_gain, m_sb_w_out, m_sgu_w_in, m_sgu_v_gain, m_sgu_w_s, m_sgu_b_s, m_sgu_w_out, m_ffn_w_up, m_ffn_conv_w, m_ffn_conv_b, m_ffn_w_down, v_mix_norm, v_ffn_norm, v_fox_w_in, v_fox_b_f, v_fox_q_gain, v_fox_k_gain, v_fox_w_out, v_mla_w_in, v_mla_q_a_gain, v_mla_kv_a_gain, v_mla_w_q_b, v_mla_w_kv_b, v_mla_q_gain, v_mla_k_gain, v_mla_w_out, v_sb_w_in, v_sb_q_gain, v_sb_k_gain, v_sb_w_out, v_sgu_w_in, v_sgu_v_gain, v_sgu_w_s, v_sgu_b_s, v_sgu_w_out, v_ffn_w_up, v_ffn_conv_w, v_ffn_conv_b, v_ffn_w_down):
    args = dict(locals())
    W = {k: args[k] for k in WEIGHT_NAMES}
    M = {k: args['m_' + k] for k in WEIGHT_NAMES}
    V = {k: args['v_' + k] for k in WEIGHT_NAMES}
    depth = mix_norm.shape[0]
    s_len, d_model = x.shape[1], x.shape[2]
    n_heads = d_model // HEAD_DIM
    assert all(W[k].shape[0] == 1 for k in WEIGHT_NAMES if k.split('_')[0] in MIXERS), "one layer per mixer"
    xi, yi, ci = lax.axis_index("x"), lax.axis_index("y"), lax.axis_index("c")
    chip = 2 * xi + yi
    place = tuple(jnp.reshape(v, (1,)).astype(jnp.int32) for v in (xi, yi, ci))

    small_local = _pack([W[k][0] if k != 'ffn_conv_w' else W[k] for k in SMALL_SHARDED])

    def piece_shards(i, part):
        if part == 'ffn':
            return {'ffn_w_up': W['ffn_w_up'][i], 'ffn_w_down': W['ffn_w_down'][i]}
        mixer = MIXERS[i % len(MIXERS)]
        shards = {k: W[k][0] for k in BIG if k.startswith(mixer + '_')}
        if i == 0:
            shards['small'] = small_local
        return shards

    pieces = [(i, part) for i in range(depth) for part in ('mixer', 'ffn')]
    gathered = {}
    state, token = _gather_begin(piece_shards(*pieces[0]), place, mix_norm, name="gather_l0_mixer")
    gathered[pieces[0]] = _gather_end(state, token, name="gather_l0_mixer")
    small_shapes = [W[k][0].shape if k != 'ffn_conv_w' else W[k].shape for k in SMALL_SHARDED]
    per_chip = [_unpack(gathered[pieces[0]]['small'][s], small_shapes) for s in range(N_CHIPS)]
    full_small = {k: jnp.concatenate([per_chip[s][j] for s in range(N_CHIPS)], axis=-1)
                  for j, k in enumerate(SMALL_SHARDED)}

    pos = positions.reshape(s_len).astype(F32)
    inv_freq = ROPE_THETA ** (-jnp.arange(0, MLA_ROPE, 2, dtype=F32) / MLA_ROPE)
    ang = pos[:, None] * inv_freq
    cos_t = _pad_lanes(jnp.concatenate([jnp.cos(ang), jnp.cos(ang)], axis=1))
    sin_t = _pad_lanes(jnp.concatenate([-jnp.sin(ang), jnp.sin(ang)], axis=1))

    def piece_params(i, part):
        mixer = MIXERS[i % len(MIXERS)]
        g = gathered[(i, part)]
        if part == 'ffn':
            return mixer, {'ffn_norm': ffn_norm[i:i + 1], 'ffn_w_up': g['ffn_w_up'],
                           'ffn_w_down': g['ffn_w_down'].reshape(-1, d_model),
                           'ffn_conv_w': full_small['ffn_conv_w'][i], 'ffn_conv_b': ffn_conv_b[i:i + 1]}
        p = {'mix_norm': mix_norm[i:i + 1]}
        rows = lambda w: w.reshape(-1, w.shape[-1])
        if mixer == 'fox':
            w = _unshard_cols(g['fox_w_in'])
            p['w_in'] = jnp.pad(w, ((0, 0), (0, (3 * n_heads + 1) * HEAD_DIM - w.shape[1])))
            p['b_f'] = _pad_lanes(fox_b_f)
            p['qk_gain'] = jnp.stack([fox_q_gain, fox_k_gain])
            p['w_out'] = rows(g['fox_w_out'])
        elif mixer == 'sb':
            p['w_in'] = g['sb_w_in']
            p['qk_gain'] = jnp.stack([sb_q_gain, sb_k_gain])
            p['w_out'] = rows(g['sb_w_out'])
        elif mixer == 'sgu':
            p['w_in'] = g['sgu_w_in']
            p['v_gain'] = full_small['sgu_v_gain'].reshape(1, -1)
            p['w_s'] = sgu_w_s[0]
            p['b_s'] = sgu_b_s[0][:, :, None]
            p['w_out'] = rows(g['sgu_w_out'])
        else:
            w = rows(g['mla_w_in'])
            p['w_in'] = jnp.pad(w, ((0, 0), (0, MLA_Q_RANK + MLA_KV_RANK + LANES - w.shape[1])))
            p['a_gain'] = jnp.concatenate([full_small['mla_q_a_gain'], full_small['mla_kv_a_gain']]).reshape(1, -1)
            wq = _unshard_cols(g['mla_w_q_b']).reshape(MLA_Q_RANK, n_heads, MLA_NOPE + MLA_ROPE)
            p['w_q_b'] = jnp.concatenate([wq[:, :, :MLA_NOPE].reshape(MLA_Q_RANK, -1),
                                          _pad_lanes(wq[:, :, MLA_NOPE:]).reshape(MLA_Q_RANK, -1)], axis=1)
            p['w_kv_b'] = g['mla_w_kv_b']
            p['gq'] = jnp.stack([mla_q_gain[:, :MLA_NOPE], _pad_lanes(mla_q_gain[:, MLA_NOPE:])])
            p['gk'] = jnp.stack([mla_k_gain[:, :MLA_NOPE], _pad_lanes(mla_k_gain[:, MLA_NOPE:])])
            p['cos'], p['sin'] = cos_t, sin_t
            p['w_out'] = rows(g['mla_w_out'])
        return mixer, p

    h = x.reshape(s_len, d_model)
    saved = []
    for n, (i, part) in enumerate(pieces):
        nxt = pieces[n + 1] if n + 1 < len(pieces) else None
        if nxt is not None:
            after = next(iter(gathered[(i, part)].values()))
            state, token = _gather_begin(piece_shards(*nxt), place, after, name=f"gather_l{nxt[0]}_{nxt[1]}")
        mixer, p = piece_params(i, part)
        if nxt is not None:
            gain = 'ffn_norm' if part == 'ffn' else 'mix_norm'
            p[gain] = p[gain] + token[0:1, 0:1]
        tag = f"l{i}_{mixer}"
        if part == 'ffn':
            h, sv = _ffn_fwd(h, p, f"l{i}")
        elif mixer in ('fox', 'sb'):
            h, sv = _qkv_attn_fwd(mixer, h, p, tag, n_heads)
        elif mixer == 'mla':
            h, sv = _mla_fwd(h, p, tag, n_heads)
        else:
            h, sv = _sgu_fwd(h, p, tag)
        saved.append((mixer, p, sv))
        if nxt is not None:
            gathered[nxt] = _gather_end(state, h, name=f"gather_l{nxt[0]}_{nxt[1]}")
    loss_row, dh = _loss(h, loss_target.reshape(s_len, d_model))
    loss = lax.psum(loss_row[0, 0], ("x", "y", "c"))

    big_grads, small_grads = {}, {k: [None] * depth for k in ('mix_norm', 'ffn_norm', 'ffn_conv_w', 'ffn_conv_b')}

    def keep(reduced, i):
        for k, v in reduced.items():
            if k.startswith('ffn_'):
                big_grads.setdefault(k, [None] * depth)[i] = v
            else:
                big_grads[k] = v[None]

    state, token, flying = None, None, None
    for n in reversed(range(len(pieces))):
        i, part = pieces[n]
        mixer, p, sv = saved[n]
        tag = f"l{i}_{mixer}"
        if token is not None:
            dh = dh + token[0:1, 0:1]
        if part == 'ffn':
            dh, big, small = _ffn_bwd(dh, sv, p, f"l{i}")
        else:
            if mixer in ('fox', 'sb'):
                dh, big, small = _qkv_attn_bwd(mixer, dh, sv, p, tag, n_heads)
            elif mixer == 'mla':
                dh, big, small = _mla_bwd(dh, sv, p, tag, n_heads)
            else:
                dh, big, small = _sgu_bwd(dh, sv, p, tag)
            big = {f"{mixer}_{k}": v for k, v in big.items()}
        if state is not None:
            keep(_reduce_end(state, place, dh, name=f"reduce_l{flying[0]}_{flying[1]}"), flying[0])
        state, token = _reduce_begin(big, place, name=f"reduce_l{i}_{part}")
        flying = (i, part)
        for k, v in small.items():
            if k in small_grads:
                small_grads[k][i] = v
            else:
                small_grads[f"{mixer}_{k}"] = v
    last_state = state
    grad_x = dh.reshape(x.shape)
    for k in ('mix_norm', 'ffn_norm', 'ffn_conv_b'):
        small_grads[k] = jnp.concatenate(small_grads[k], axis=0)
    small_grads['ffn_conv_w'] = jnp.stack(small_grads['ffn_conv_w'])

    small_names = [k for k in WEIGHT_NAMES if k not in BIG]
    full_shapes = {k: (W[k].shape[:-1] + (W[k].shape[-1] * N_CHIPS,) if k in SMALL_SHARDED else W[k].shape)
                   for k in small_names}
    packed = _pack([small_grads[k].reshape(full_shapes[k]) for k in small_names])
    summed = _sum_devices(_broadcast_all(packed, name="small_bcast"), name="small_sum")
    small_full = dict(zip(small_names, _unpack(summed, [full_shapes[k] for k in small_names])))
    keep(_reduce_end(last_state, place, summed, name="reduce_l0_mixer"), 0)
    for k in ('ffn_w_up', 'ffn_w_down'):
        big_grads[k] = jnp.stack(big_grads[k])
    grads = dict(big_grads)
    for k in small_names:
        g = small_full[k]
        if k in SMALL_SHARDED:
            n = W[k].shape[-1]
            g = lax.dynamic_slice_in_dim(g, chip * n, n, axis=g.ndim - 1)
        grads[k] = g
    grads = {k: grads[k].reshape(W[k].shape) for k in WEIGHT_NAMES}

    delta, new_m, new_v = {}, {}, {}
    for k in WEIGHT_NAMES:
        delta[k], new_m[k], new_v[k] = _adamw(W[k], grads[k], M[k], V[k], name=f"adamw_{k}")
    return (loss, grad_x, *[grads[k] for k in WEIGHT_NAMES], *[delta[k] for k in WEIGHT_NAMES],
            *[new_m[k] for k in WEIGHT_NAMES], *[new_v[k] for k in WEIGHT_NAMES])
```

```python
import functools

import jax
import jax.numpy as jnp
from jax import lax
from jax.experimental import pallas as pl
from jax.experimental.pallas import tpu as pltpu

F32 = jnp.float32
BF16 = jnp.bfloat16
LANES = 128
HEAD_DIM = 128
NORM_EPS = 1e-6
MLA_Q_RANK = 512
MLA_KV_RANK = 512
MLA_NOPE = 128
MLA_ROPE = 64
ROPE_THETA = 10000.0
SGU_CHUNK = 128
N_CHIPS = 4
ADAM_LR, ADAM_B1, ADAM_B2, ADAM_EPS, ADAM_WD, ADAM_STEP = 0.001, 0.9, 0.999, 1e-08, 0.01, 10
VMEM_LIMIT_BYTES = 56 * 1024 * 1024
MM_VMEM_BUDGET_BYTES = 36 * 1024 * 1024
MESH = pl.DeviceIdType.MESH
NEG_BIG = -1e30


def _params(*sem):
    return pltpu.CompilerParams(dimension_semantics=sem, vmem_limit_bytes=VMEM_LIMIT_BYTES)


def _div_block(n, target, mult=LANES):
    if n <= target:
        return n
    best = None
    for b in range(mult, target + 1, mult):
        if n % b == 0:
            best = b
    assert best is not None, (n, target, mult)
    return best


def _iota(shape, dim):
    return lax.broadcasted_iota(jnp.int32, shape, dim)


def _dot(a, b, ca, cb):
    return lax.dot_general(a, b, (((ca,), (cb,)), ((), ())), preferred_element_type=F32)


def _mm(a, b, *, name, ta=False, tb=False, a_sh=False, b_sh=None, o_sh=False, res=None, after=None, out_dtype=F32,
        bm=1024, bn=1024, bk=512):
    if a_sh:
        assert not ta
        m, k = a.shape[1], a.shape[0] * a.shape[2]
    else:
        m, k = (a.shape[1], a.shape[0]) if ta else a.shape
    if b_sh == 'n':
        n = b.shape[2] * b.shape[0]
        assert b.shape[1] == k and not tb
    elif b_sh == 'k':
        n = b.shape[1]
        assert b.shape[2] * N_CHIPS == k
    else:
        n = b.shape[0] if tb else b.shape[1]
        assert (b.shape[1] if tb else b.shape[0]) == k
    n_sh = n // N_CHIPS
    k_sh = k // N_CHIPS
    bm = _div_block(m, bm, 8 if not ta else LANES)
    bn_limit = n
    if b_sh == 'n':
        bn_limit = b.shape[2]
    if o_sh:
        bn_limit = min(bn_limit, n_sh)
    bn = _div_block(bn_limit, bn)
    assert (not o_sh or n_sh % bn == 0) and (b_sh != 'n' or b.shape[2] % bn == 0)
    bk_limit = k_sh if b_sh == 'k' else k
    if a_sh:
        bk_limit = min(bk_limit, a.shape[2])

    def footprint(kb):
        io = bm * kb * a.dtype.itemsize + kb * bn * b.dtype.itemsize + bm * bn * jnp.dtype(out_dtype).itemsize
        if res is not None:
            io += bm * bn * res.dtype.itemsize
        return 2 * io + (bm * bn * 4 if kb < k else 0)

    bk = max([kb for kb in range(LANES, bk_limit + 1, LANES)
              if bk_limit % kb == 0 and (footprint(kb) <= MM_VMEM_BUDGET_BYTES or kb <= bk)])
    assert (not a_sh or a.shape[2] % bk == 0) and (b_sh != 'k' or k_sh % bk == 0) and k % bk == 0
    nbo = n_sh // bn if o_sh else 1
    nbb = b.shape[2] // bn if b_sh == 'n' else 1
    nks = k_sh // bk if b_sh == 'k' else 1
    nka = a.shape[2] // bk if a_sh else 1
    nk = k // bk

    if a_sh:
        a_spec = pl.BlockSpec((None, bm, bk), lambda i, j, q: (q // nka, i, q % nka))
    elif ta:
        a_spec = pl.BlockSpec((bk, bm), lambda i, j, q: (q, i))
    else:
        a_spec = pl.BlockSpec((bm, bk), lambda i, j, q: (i, q))
    if b_sh == 'n':
        b_spec = pl.BlockSpec((None, bk, bn), lambda i, j, q: (j // nbb, q, j % nbb))
    elif b_sh == 'k':
        b_spec = pl.BlockSpec((None, bn, bk), lambda i, j, q: (q // nks, j, q % nks))
    elif tb:
        b_spec = pl.BlockSpec((bn, bk), lambda i, j, q: (j, q))
    else:
        b_spec = pl.BlockSpec((bk, bn), lambda i, j, q: (q, j))
    if o_sh:
        o_spec = pl.BlockSpec((None, bm, bn), lambda i, j, q: (j // nbo, i, j % nbo))
        o_shape = jax.ShapeDtypeStruct((N_CHIPS, m, n_sh), out_dtype)
    else:
        o_spec = pl.BlockSpec((bm, bn), lambda i, j, q: (i, j))
        o_shape = jax.ShapeDtypeStruct((m, n), out_dtype)
    tb_eff = tb or b_sh == 'k'

    def body(a_ref, b_ref, *rest):
        rest = list(rest)
        if after is not None:
            rest.pop(0)
        r_ref = rest.pop(0) if res is not None else None
        o_ref = rest.pop(0)
        part = _dot(a_ref[...].astype(BF16), b_ref[...].astype(BF16), 0 if ta else 1, 1 if tb_eff else 0)

        def finish(r):
            if res is not None:
                r = r + r_ref[...].astype(F32)
            o_ref[...] = r.astype(out_dtype)

        if nk == 1:
            finish(part)
            return
        acc, = rest
        q = pl.program_id(2)

        @pl.when(q == 0)
        def _():
            acc[...] = part

        @pl.when(q > 0)
        def _():
            acc[...] += part

        @pl.when(q == nk - 1)
        def _():
            finish(acc[...])

    ins = [a, b]
    in_specs = [a_spec, b_spec]
    if after is not None:
        ins.append(after)
        in_specs.append(pl.BlockSpec((8, LANES), lambda i, j, q: (0, 0)))
    if res is not None:
        assert not o_sh
        ins.append(res)
        in_specs.append(pl.BlockSpec((bm, bn), lambda i, j, q: (i, j)))
    return pl.pallas_call(
        body, name=name, grid=(m // bm, n // bn, nk), in_specs=in_specs, out_specs=o_spec, out_shape=o_shape,
        scratch_shapes=[pltpu.VMEM((bm, bn), F32)] if nk > 1 else [],
        compiler_params=_params("parallel", "parallel", "arbitrary"))(*ins)


def _rms_fwd(x, g, *, name, out_dtype=BF16, br=256):
    r, c = x.shape
    br = _div_block(r, br, 8)

    def body(x_ref, g_ref, o_ref):
        xv = x_ref[...].astype(F32)
        inv = lax.rsqrt(jnp.mean(xv * xv, axis=-1, keepdims=True) + NORM_EPS)
        o_ref[...] = (xv * inv * g_ref[...]).astype(out_dtype)

    return pl.pallas_call(
        body, name=name, grid=(r // br,),
        in_specs=[pl.BlockSpec((br, c), lambda i: (i, 0)), pl.BlockSpec((1, c), lambda i: (0, 0))],
        out_specs=pl.BlockSpec((br, c), lambda i: (i, 0)), out_shape=jax.ShapeDtypeStruct((r, c), out_dtype),
        compiler_params=_params("parallel"))(x, g)


def _rms_bwd_math(xv, gv, dyv, n):
    inv = lax.rsqrt(jnp.sum(xv * xv, axis=-1, keepdims=True) / n + NORM_EPS)
    xh = xv * inv
    dyg = dyv * gv
    dx = inv * (dyg - xh * (jnp.sum(dyg * xh, axis=-1, keepdims=True) / n))
    return dx, dyv * xh


def _rms_bwd(x, g, dy, *, name, res=None, br=256):
    r, c = x.shape
    br = _div_block(r, br, 8)

    def body(x_ref, g_ref, dy_ref, *rest):
        if res is not None:
            r_ref, dx_ref, dg_ref = rest
        else:
            dx_ref, dg_ref = rest
        dx, dgr = _rms_bwd_math(x_ref[...].astype(F32), g_ref[...], dy_ref[...].astype(F32), c)
        if res is not None:
            dx = dx + r_ref[...]
        dx_ref[...] = dx

        @pl.when(pl.program_id(0) == 0)
        def _():
            dg_ref[...] = jnp.zeros_like(dg_ref)

        dg_ref[...] += jnp.sum(dgr, axis=0, keepdims=True)

    row = pl.BlockSpec((br, c), lambda i: (i, 0))
    vec = pl.BlockSpec((1, c), lambda i: (0, 0))
    ins = [x, g, dy] + ([res] if res is not None else [])
    return pl.pallas_call(
        body, name=name, grid=(r // br,), in_specs=[row, vec, row] + ([row] if res is not None else []),
        out_specs=[row, vec], out_shape=[jax.ShapeDtypeStruct((r, c), F32), jax.ShapeDtypeStruct((1, c), F32)],
        compiler_params=_params("arbitrary"))(*ins)


def _loss(y, target, *, name="loss", br=256):
    r, c = y.shape
    br = _div_block(r, br, 8)

    def body(y_ref, t_ref, l_ref, dy_ref):
        d = y_ref[...] - t_ref[...]
        dy_ref[...] = d * (1.0 / c)

        @pl.when(pl.program_id(0) == 0)
        def _():
            l_ref[...] = jnp.zeros_like(l_ref)

        part = jnp.sum(d * d, axis=0, keepdims=True)
        l_ref[...] += (0.5 / c) * jnp.sum(part, axis=1, keepdims=True) * jnp.ones((1, LANES), F32)

    row = pl.BlockSpec((br, c), lambda i: (i, 0))
    return pl.pallas_call(
        body, name=name, grid=(r // br,), in_specs=[row, row],
        out_specs=[pl.BlockSpec((1, LANES), lambda i: (0, 0)), row],
        out_shape=[jax.ShapeDtypeStruct((1, LANES), F32), jax.ShapeDtypeStruct((r, c), F32)],
        compiler_params=_params("arbitrary"))(y, target)


def _split2(x):
    hi = x.astype(BF16)
    lo = (x - hi.astype(F32)).astype(BF16)
    return hi, lo


def _lane_scan(x, *, suffix):
    rows, n = x.shape
    nb = n // LANES
    a, b = _iota((LANES, LANES), 0), _iota((LANES, LANES), 1)
    tri = ((a > b) if suffix else (a < b)).astype(BF16)
    outs = [None] * nb
    run = jnp.zeros((rows, 1), F32)
    order = range(nb - 1, -1, -1) if suffix else range(nb)
    for blk in order:
        xb = x[:, blk * LANES:(blk + 1) * LANES]
        hi, lo = _split2(xb)
        outs[blk] = _dot(hi, tri, 1, 0) + _dot(lo, tri, 1, 0) + run
        run = run + jnp.sum(xb, axis=-1, keepdims=True)
    return jnp.concatenate(outs, axis=1)


def _softplus(z):
    return jnp.maximum(z, 0.0) + jnp.log(1.0 + jnp.exp(-jnp.abs(z)))


def _head_norm(x, g):
    xv = x.astype(F32)
    inv = lax.rsqrt(jnp.mean(xv * xv, axis=-1, keepdims=True) + NORM_EPS)
    return xv * inv * g


def _attn_weights(kind, qn, kn, scale, qi, bq, bias):
    s = _dot(qn, kn, 1, 1) * scale
    row = qi * bq + _iota(s.shape, 0)
    col = _iota(s.shape, 1)
    if kind == 'sb':
        strict = col < row
        sp = _softplus(s)
        after = _lane_scan(jnp.where(strict, -sp, 0.0), suffix=True)
        w = jnp.where(strict, jnp.exp(s - sp + after), 0.0)
        return w, (strict, s - sp)
    if bias is not None:
        s = s + bias
    s = jnp.where(col <= row, s, NEG_BIG)
    mx = jnp.max(s, axis=-1, keepdims=True)
    e = jnp.exp(s - mx)
    return e, jnp.sum(e, axis=-1, keepdims=True)


def _attn_fwd(kind, q, k, v, *, name, n_heads, dqk, qcol, kcol, vcol, scale, gains=None, cq=None, ck=None, bq=256):
    s_len = q.shape[0]
    bq = _div_block(s_len, bq, 8)
    norm, fox = gains is not None, cq is not None

    def body(*refs):
        refs = list(refs)
        q_ref, k_ref, v_ref = refs[:3]
        rest = refs[3:]
        g_ref = rest.pop(0) if norm else None
        cq_ref, ck_ref = (rest.pop(0), rest.pop(0)) if fox else (None, None)
        o_ref, = rest
        qi = pl.program_id(1)

        def step(n_keys):
            if norm:
                qn = _head_norm(q_ref[...], g_ref[0]).astype(BF16)
                kn = _head_norm(k_ref[0:n_keys, :], g_ref[1]).astype(BF16)
            else:
                qn, kn = q_ref[...].astype(BF16), k_ref[0:n_keys, :].astype(BF16)
            bias = (cq_ref[...] - ck_ref[:, 0:n_keys]) if fox else None
            w, aux = _attn_weights(kind, qn, kn, scale, qi, bq, bias)
            o = _dot(w.astype(BF16), v_ref[0:n_keys, :].astype(BF16), 1, 0)
            if kind != 'sb':
                o = o / aux
            o_ref[...] = o.astype(BF16)

        for qv in range(s_len // bq):
            pl.when(qi == qv)(functools.partial(step, (qv + 1) * bq))

    in_specs = [pl.BlockSpec((bq, dqk), lambda h, i: (i, qcol(h))),
                pl.BlockSpec((s_len, dqk), lambda h, i: (0, kcol(h))),
                pl.BlockSpec((s_len, HEAD_DIM), lambda h, i: (0, vcol(h)))]
    ins = [q, k, v]
    if norm:
        in_specs.append(pl.BlockSpec((2, 1, dqk), lambda h, i: (0, 0, 0)))
        ins.append(gains)
    if fox:
        in_specs += [pl.BlockSpec((None, bq, 1), lambda h, i: (h, i, 0)), pl.BlockSpec((None, 1, s_len), lambda h, i: (h, 0, 0))]
        ins += [cq, ck]
    return pl.pallas_call(
        body, name=name, grid=(n_heads, s_len // bq), in_specs=in_specs,
        out_specs=pl.BlockSpec((bq, HEAD_DIM), lambda h, i: (i, h)),
        out_shape=jax.ShapeDtypeStruct((s_len, n_heads * HEAD_DIM), BF16),
        compiler_params=_params("parallel", "parallel"))(*ins)


def _attn_bwd(kind, q, k, v, o, do, *, name, n_heads, dqk, qcol, kcol, vcol, scale, gains=None, cq=None, ck=None,
              bq=256):
    s_len = q.shape[0]
    bq = _div_block(s_len, bq, 8)
    nq = s_len // bq
    norm, fox = gains is not None, cq is not None

    def body(*refs):
        refs = list(refs)
        q_ref, k_ref, v_ref, o_ref, do_ref = refs[:5]
        rest = refs[5:]
        g_ref = rest.pop(0) if norm else None
        cq_ref, ck_ref = (rest.pop(0), rest.pop(0)) if fox else (None, None)
        dq_ref, dk_ref, dv_ref = rest.pop(0), rest.pop(0), rest.pop(0)
        dg_ref = rest.pop(0) if norm else None
        dcq_ref, dck_ref = (rest.pop(0), rest.pop(0)) if fox else (None, None)
        dk_acc, dv_acc = rest
        h, qi = pl.program_id(0), pl.program_id(1)

        @pl.when(qi == 0)
        def _():
            dk_acc[...] = jnp.zeros_like(dk_acc)
            dv_acc[...] = jnp.zeros_like(dv_acc)
            if fox:
                dck_ref[...] = jnp.zeros_like(dck_ref)

        if norm:
            @pl.when((qi == 0) & (h == 0))
            def _():
                dg_ref[...] = jnp.zeros_like(dg_ref)


        def step(n_keys):
            if norm:
                qn = _head_norm(q_ref[...], g_ref[0]).astype(BF16)
                kn = _head_norm(k_ref[0:n_keys, :], g_ref[1]).astype(BF16)
            else:
                qn, kn = q_ref[...].astype(BF16), k_ref[0:n_keys, :].astype(BF16)
            vb = v_ref[0:n_keys, :].astype(BF16)
            dob = do_ref[...].astype(BF16)
            bias = (cq_ref[...] - ck_ref[:, 0:n_keys]) if fox else None
            w, aux = _attn_weights(kind, qn, kn, scale, qi, bq, bias)
            dw = _dot(dob, vb, 1, 1)
            if kind == 'sb':
                strict, log_sig = aux
                g = dw * w
                cc = _lane_scan(g, suffix=False)
                sig = jnp.exp(log_sig)
                ds = jnp.where(strict, g * (1.0 - sig) - cc * sig, 0.0)
                pw = w
            else:
                pw = w / aux
                delta = jnp.sum(do_ref[...].astype(F32) * o_ref[...].astype(F32), axis=-1, keepdims=True)
                ds = pw * (dw - delta)
                if fox:
                    dcq_ref[...] = jnp.sum(ds, axis=1, keepdims=True)
                    dck_ref[:, 0:n_keys] -= jnp.sum(ds, axis=0, keepdims=True)
            dsb = (ds * scale).astype(BF16)
            dqn = _dot(dsb, kn, 1, 0)
            dk_acc[0:n_keys, :] += _dot(dsb, qn, 0, 0)
            dv_acc[0:n_keys, :] += _dot(pw.astype(BF16), dob, 0, 0)
            if norm:
                dq, dgr = _rms_bwd_math(q_ref[...].astype(F32), g_ref[0], dqn, dqk)
                dg_ref[0] += jnp.sum(dgr, axis=0, keepdims=True)
                dq_ref[...] = dq.astype(BF16)
            else:
                dq_ref[...] = dqn.astype(BF16)

        for qv in range(nq):
            pl.when(qi == qv)(functools.partial(step, (qv + 1) * bq))

        @pl.when(qi == nq - 1)
        def _():
            if norm:
                dk, dgr = _rms_bwd_math(k_ref[...].astype(F32), g_ref[1], dk_acc[...], dqk)
                dg_ref[1] += jnp.sum(dgr, axis=0, keepdims=True)
                dk_ref[...] = dk.astype(BF16)
            else:
                dk_ref[...] = dk_acc[...].astype(BF16)
            dv_ref[...] = dv_acc[...].astype(BF16)

    in_specs = [pl.BlockSpec((bq, dqk), lambda h, i: (i, qcol(h))),
                pl.BlockSpec((s_len, dqk), lambda h, i: (0, kcol(h))),
                pl.BlockSpec((s_len, HEAD_DIM), lambda h, i: (0, vcol(h))),
                pl.BlockSpec((bq, HEAD_DIM), lambda h, i: (i, h)),
                pl.BlockSpec((bq, HEAD_DIM), lambda h, i: (i, h))]
    ins = [q, k, v, o, do]
    out_specs = [pl.BlockSpec((bq, dqk), lambda h, i: (i, h)),
                 pl.BlockSpec((s_len, dqk), lambda h, i: (0, h)),
                 pl.BlockSpec((s_len, HEAD_DIM), lambda h, i: (0, h))]
    out_shape = [jax.ShapeDtypeStruct((s_len, n_heads * dqk), BF16), jax.ShapeDtypeStruct((s_len, n_heads * dqk), BF16),
                 jax.ShapeDtypeStruct((s_len, n_heads * HEAD_DIM), BF16)]
    if norm:
        in_specs.append(pl.BlockSpec((2, 1, dqk), lambda h, i: (0, 0, 0)))
        ins.append(gains)
        out_specs.append(pl.BlockSpec((2, 1, dqk), lambda h, i: (0, 0, 0)))
        out_shape.append(jax.ShapeDtypeStruct((2, 1, dqk), F32))
    if fox:
        in_specs += [pl.BlockSpec((None, bq, 1), lambda h, i: (h, i, 0)), pl.BlockSpec((None, 1, s_len), lambda h, i: (h, 0, 0))]
        ins += [cq, ck]
        out_specs += [pl.BlockSpec((None, bq, 1), lambda h, i: (h, i, 0)), pl.BlockSpec((None, 1, s_len), lambda h, i: (h, 0, 0))]
        out_shape += [jax.ShapeDtypeStruct((n_heads, s_len, 1), F32), jax.ShapeDtypeStruct((n_heads, 1, s_len), F32)]
    return pl.pallas_call(
        body, name=name, grid=(n_heads, nq), in_specs=in_specs, out_specs=out_specs, out_shape=out_shape,
        scratch_shapes=[pltpu.VMEM((s_len, dqk), F32), pltpu.VMEM((s_len, HEAD_DIM), F32)],
        compiler_params=_params("arbitrary", "arbitrary"))(*ins)


def _split3(x):
    hi = x.astype(BF16)
    r1 = x - hi.astype(F32)
    mid = r1.astype(BF16)
    lo = (r1 - mid.astype(F32)).astype(BF16)
    return hi, mid, lo


def _seq_scan(x, *, reverse):
    n = x.shape[0] // LANES
    a, b = _iota((LANES, LANES), 0), _iota((LANES, LANES), 1)
    tri = ((b >= a) if reverse else (b <= a)).astype(BF16)
    outs = [None] * n
    run = jnp.zeros((1, x.shape[1]), F32)
    for blk in (range(n - 1, -1, -1) if reverse else range(n)):
        xb = x[blk * LANES:(blk + 1) * LANES, :]
        hi, mid, lo = _split3(xb)
        outs[blk] = _dot(tri, hi, 1, 0) + _dot(tri, mid, 1, 0) + _dot(tri, lo, 1, 0) + run
        run = run + jnp.sum(xb, axis=0, keepdims=True)
    return jnp.concatenate(outs, axis=0)


def _fgate_fwd(qkvf, b_f, *, fcol, name):
    s_len = qkvf.shape[0]

    def body(f_ref, b_ref, cum_ref):
        z = f_ref[...] + b_ref[...]
        cum_ref[...] = _seq_scan(-_softplus(-z), reverse=False)

    return pl.pallas_call(
        body, name=name, grid=(1,),
        in_specs=[pl.BlockSpec((s_len, LANES), lambda i: (0, fcol)), pl.BlockSpec((1, LANES), lambda i: (0, 0))],
        out_specs=pl.BlockSpec((s_len, LANES), lambda i: (0, 0)), out_shape=jax.ShapeDtypeStruct((s_len, LANES), F32),
        compiler_params=_params("arbitrary"))(qkvf, b_f)


def _fgate_bwd(qkvf, b_f, dcum_a, dcum_b, *, fcol, n_heads, name):
    s_len = qkvf.shape[0]

    def body(f_ref, b_ref, da_ref, db_ref, dz_ref, dbias_ref):
        z = f_ref[...] + b_ref[...]
        dlog = _seq_scan(da_ref[...] + db_ref[...], reverse=True)
        dz = dlog * jnp.exp(-_softplus(z))
        dz = jnp.where(_iota(dz.shape, 1) < n_heads, dz, 0.0)
        dz_ref[...] = dz.astype(BF16)
        dbias_ref[...] = jnp.sum(dz, axis=0, keepdims=True)

    full = pl.BlockSpec((s_len, LANES), lambda i: (0, 0))
    vec = pl.BlockSpec((1, LANES), lambda i: (0, 0))
    return pl.pallas_call(
        body, name=name, grid=(1,),
        in_specs=[pl.BlockSpec((s_len, LANES), lambda i: (0, fcol)), vec, full, full],
        out_specs=[full, vec], out_shape=[jax.ShapeDtypeStruct((s_len, LANES), BF16), jax.ShapeDtypeStruct((1, LANES), F32)],
        compiler_params=_params("arbitrary"))(qkvf, b_f, dcum_a, dcum_b)


def _rope_swap(x):
    half = MLA_ROPE // 2
    lane = _iota(x.shape, 1)
    sw = jnp.where(lane < half, pltpu.roll(x, LANES - half, axis=1), pltpu.roll(x, half, axis=1))
    return jnp.where(lane < MLA_ROPE, sw, 0.0)


def _mla_prep_fwd(qp, kv, c, gq, gk, cos_t, sin_t, *, n_heads, name, bs=512):
    s_len = qp.shape[0]
    bs = _div_block(s_len, bs, 8)
    krope_col = (MLA_Q_RANK + MLA_KV_RANK) // LANES

    def body(qn_ref, qr_ref, kn_ref, kr_ref, gq_ref, gk_ref, cos_ref, sin_ref, qc_ref, kc_ref):
        cos_v, sin_v = cos_ref[...], sin_ref[...]

        def rope(x, g):
            xv = x.astype(F32)
            inv = lax.rsqrt(jnp.sum(xv * xv, axis=-1, keepdims=True) / MLA_ROPE + NORM_EPS)
            y = xv * inv * g
            return y * cos_v + _rope_swap(y) * sin_v

        qc_ref[:, :LANES] = _head_norm(qn_ref[...], gq_ref[0]).astype(BF16)
        qc_ref[:, LANES:] = rope(qr_ref[...], gq_ref[1]).astype(BF16)
        kc_ref[:, :LANES] = _head_norm(kn_ref[...], gk_ref[0]).astype(BF16)
        kc_ref[:, LANES:] = rope(kr_ref[...], gk_ref[1]).astype(BF16)

    blk = lambda f: pl.BlockSpec((bs, LANES), f)
    gspec = pl.BlockSpec((2, 1, LANES), lambda i, h: (0, 0, 0))
    tspec = pl.BlockSpec((bs, LANES), lambda i, h: (i, 0))
    ospec = pl.BlockSpec((bs, 2 * LANES), lambda i, h: (i, h))
    oshape = jax.ShapeDtypeStruct((s_len, n_heads * 2 * LANES), BF16)
    return pl.pallas_call(
        body, name=name, grid=(s_len // bs, n_heads),
        in_specs=[blk(lambda i, h: (i, h)), blk(lambda i, h: (i, n_heads + h)), blk(lambda i, h: (i, 2 * h)),
                  blk(lambda i, h: (i, krope_col)), gspec, gspec, tspec, tspec],
        out_specs=[ospec, ospec], out_shape=[oshape, oshape],
        compiler_params=_params("parallel", "parallel"))(qp, qp, kv, c, gq, gk, cos_t, sin_t)


def _mla_prep_bwd(qp, kv, c, gq, gk, cos_t, sin_t, dqc, dkc, dv, *, n_heads, name, bs=512):
    s_len = qp.shape[0]
    bs = _div_block(s_len, bs, 8)
    krope_col = (MLA_Q_RANK + MLA_KV_RANK) // LANES

    def body(qn_ref, qr_ref, kn_ref, kr_ref, gq_ref, gk_ref, cos_ref, sin_ref, dqc_ref, dkc_ref, dv_ref,
             dqn_ref, dqr_ref, dkv_ref, dkr_ref, dgq_ref, dgk_ref):
        i, h = pl.program_id(0), pl.program_id(1)
        cos_v, sin_v = cos_ref[...], sin_ref[...]

        @pl.when((i == 0) & (h == 0))
        def _():
            dgq_ref[...] = jnp.zeros_like(dgq_ref)
            dgk_ref[...] = jnp.zeros_like(dgk_ref)

        @pl.when(h == 0)
        def _():
            dkr_ref[...] = jnp.zeros_like(dkr_ref)

        def unrope(dy):
            dy = dy.astype(F32)
            return dy * cos_v + _rope_swap(dy * sin_v)

        dqn, dg = _rms_bwd_math(qn_ref[...].astype(F32), gq_ref[0], dqc_ref[:, :LANES].astype(F32), MLA_NOPE)
        dgq_ref[0] += jnp.sum(dg, axis=0, keepdims=True)
        dqn_ref[...] = dqn.astype(BF16)
        dqr, dg = _rms_bwd_math(qr_ref[...].astype(F32), gq_ref[1], unrope(dqc_ref[:, LANES:]), MLA_ROPE)
        dgq_ref[1] += jnp.sum(dg, axis=0, keepdims=True)
        dqr_ref[...] = dqr.astype(BF16)
        dkn, dg = _rms_bwd_math(kn_ref[...].astype(F32), gk_ref[0], dkc_ref[:, :LANES].astype(F32), MLA_NOPE)
        dgk_ref[0] += jnp.sum(dg, axis=0, keepdims=True)
        dkv_ref[:, :LANES] = dkn.astype(BF16)
        dkv_ref[:, LANES:] = dv_ref[...]
        dkr, dg = _rms_bwd_math(kr_ref[...].astype(F32), gk_ref[1], unrope(dkc_ref[:, LANES:]), MLA_ROPE)
        dgk_ref[1] += jnp.sum(dg, axis=0, keepdims=True)
        dkr_ref[...] += dkr

    blk = lambda f: pl.BlockSpec((bs, LANES), f)
    gspec = pl.BlockSpec((2, 1, LANES), lambda i, h: (0, 0, 0))
    tspec = pl.BlockSpec((bs, LANES), lambda i, h: (i, 0))
    cat = pl.BlockSpec((bs, 2 * LANES), lambda i, h: (i, h))
    head = blk(lambda i, h: (i, h))
    hshape = jax.ShapeDtypeStruct((s_len, n_heads * LANES), BF16)
    gshape = jax.ShapeDtypeStruct((2, 1, LANES), F32)
    return pl.pallas_call(
        body, name=name, grid=(s_len // bs, n_heads),
        in_specs=[head, blk(lambda i, h: (i, n_heads + h)), blk(lambda i, h: (i, 2 * h)),
                  blk(lambda i, h: (i, krope_col)), gspec, gspec, tspec, tspec, cat, cat, head],
        out_specs=[head, head, cat, tspec, gspec, gspec],
        out_shape=[hshape, hshape, jax.ShapeDtypeStruct((s_len, n_heads * 2 * LANES), BF16),
                   jax.ShapeDtypeStruct((s_len, LANES), F32), gshape, gshape],
        compiler_params=_params("arbitrary", "arbitrary"))(qp, qp, kv, c, gq, gk, cos_t, sin_t, dqc, dkc, dv)


def _mla_latent_fwd(c, ga, *, name, br=256):
    s_len = c.shape[0]
    br = _div_block(s_len, br, 8)

    def body(c_ref, g_ref, o_ref):
        for part in range(2):
            sl = slice(part * MLA_Q_RANK, (part + 1) * MLA_Q_RANK)
            o_ref[:, sl] = _head_norm(c_ref[:, sl], g_ref[:, sl]).astype(BF16)

    w = MLA_Q_RANK + MLA_KV_RANK
    return pl.pallas_call(
        body, name=name, grid=(s_len // br,),
        in_specs=[pl.BlockSpec((br, w), lambda i: (i, 0)), pl.BlockSpec((1, w), lambda i: (0, 0))],
        out_specs=pl.BlockSpec((br, w), lambda i: (i, 0)), out_shape=jax.ShapeDtypeStruct((s_len, w), BF16),
        compiler_params=_params("parallel"))(c, ga)


def _mla_latent_bwd(c, ga, dcn_q, dcn_kv, dk_rope, *, name, br=256):
    s_len, cw = c.shape
    br = _div_block(s_len, br, 8)
    w = MLA_Q_RANK + MLA_KV_RANK

    def body(c_ref, g_ref, dq_ref, dkv_ref, dkr_ref, dc_ref, dg_ref):
        @pl.when(pl.program_id(0) == 0)
        def _():
            dg_ref[...] = jnp.zeros_like(dg_ref)

        for part, d_ref in enumerate((dq_ref, dkv_ref)):
            sl = slice(part * MLA_Q_RANK, (part + 1) * MLA_Q_RANK)
            dx, dg = _rms_bwd_math(c_ref[:, sl].astype(F32), g_ref[:, sl], d_ref[...].astype(F32), MLA_Q_RANK)
            dc_ref[:, sl] = dx.astype(BF16)
            dg_ref[:, sl] += jnp.sum(dg, axis=0, keepdims=True)
        dc_ref[:, w:] = dkr_ref[...].astype(BF16)

    return pl.pallas_call(
        body, name=name, grid=(s_len // br,),
        in_specs=[pl.BlockSpec((br, w), lambda i: (i, 0)), pl.BlockSpec((1, w), lambda i: (0, 0)),
                  pl.BlockSpec((br, MLA_Q_RANK), lambda i: (i, 0)), pl.BlockSpec((br, MLA_KV_RANK), lambda i: (i, 0)),
                  pl.BlockSpec((br, LANES), lambda i: (i, 0))],
        out_specs=[pl.BlockSpec((br, cw), lambda i: (i, 0)), pl.BlockSpec((1, w), lambda i: (0, 0))],
        out_shape=[jax.ShapeDtypeStruct((s_len, cw), BF16), jax.ShapeDtypeStruct((1, w), F32)],
        compiler_params=_params("arbitrary"))(c, ga, dcn_q, dcn_kv, dk_rope)


_GELU_C = 0.7978845608028654


def _gelu(x):
    return 0.5 * x * (1.0 + jnp.tanh(_GELU_C * (x + 0.044715 * x * x * x)))


def _gelu_grad(x):
    t = jnp.tanh(_GELU_C * (x + 0.044715 * x * x * x))
    return 0.5 * (1.0 + t) + 0.5 * x * (1.0 - t * t) * _GELU_C * (1.0 + 3 * 0.044715 * x * x)


def _sgu_act_fwd(uv, vg, *, name, br=256):
    s_len, w2 = uv.shape
    w = w2 // 2
    br = _div_block(s_len, br, 8)

    def body(uv_ref, g_ref, u_ref, v_ref):
        u_ref[...] = _gelu(uv_ref[:, :w])
        v_ref[...] = _head_norm(_gelu(uv_ref[:, w:]), g_ref[...]).astype(BF16)

    row = lambda c: pl.BlockSpec((br, c), lambda i: (i, 0))
    return pl.pallas_call(
        body, name=name, grid=(s_len // br,), in_specs=[row(w2), pl.BlockSpec((1, w), lambda i: (0, 0))],
        out_specs=[row(w), row(w)], out_shape=[jax.ShapeDtypeStruct((s_len, w), F32), jax.ShapeDtypeStruct((s_len, w), BF16)],
        compiler_params=_params("parallel"))(uv, vg)


def _sgu_act_bwd(uv, vg, du, dvn, *, name, br=256):
    s_len, w2 = uv.shape
    w = w2 // 2
    br = _div_block(s_len, br, 8)

    def body(uv_ref, g_ref, du_ref, dvn_ref, duv_ref, dg_ref):
        @pl.when(pl.program_id(0) == 0)
        def _():
            dg_ref[...] = jnp.zeros_like(dg_ref)

        up, vp = uv_ref[:, :w], uv_ref[:, w:]
        duv_ref[:, :w] = (du_ref[...] * _gelu_grad(up)).astype(BF16)
        dva, dg = _rms_bwd_math(_gelu(vp), g_ref[...], dvn_ref[...], w)
        dg_ref[...] += jnp.sum(dg, axis=0, keepdims=True)
        duv_ref[:, w:] = (dva * _gelu_grad(vp)).astype(BF16)

    row = lambda c: pl.BlockSpec((br, c), lambda i: (i, 0))
    vec = pl.BlockSpec((1, w), lambda i: (0, 0))
    return pl.pallas_call(
        body, name=name, grid=(s_len // br,), in_specs=[row(w2), vec, row(w), row(w)], out_specs=[row(w2), vec],
        out_shape=[jax.ShapeDtypeStruct((s_len, w2), BF16), jax.ShapeDtypeStruct((1, w), F32)],
        compiler_params=_params("arbitrary"))(uv, vg, du, dvn)


def _tril_weights(ws_ref):
    t, s = _iota((SGU_CHUNK, SGU_CHUNK), 0), _iota((SGU_CHUNK, SGU_CHUNK), 1)
    keep = s <= t
    return jnp.where(keep, ws_ref[...], 0.0), keep


def _sgu_mix_fwd(u, vn, w_s, b_s, *, name):
    s_len, w = u.shape
    nc = s_len // SGU_CHUNK

    def body(u_ref, v_ref, ws_ref, b_ref, o_ref):
        wm = _tril_weights(ws_ref)[0].astype(BF16)
        for n in range(nc):
            rows = slice(n * SGU_CHUNK, (n + 1) * SGU_CHUNK)
            mixed = _dot(wm, v_ref[rows, :], 1, 0) + b_ref[...]
            o_ref[rows, :] = (u_ref[rows, :] * mixed).astype(BF16)

    col = pl.BlockSpec((s_len, LANES), lambda g: (0, g))
    return pl.pallas_call(
        body, name=name, grid=(w // LANES,),
        in_specs=[col, col, pl.BlockSpec((None, SGU_CHUNK, SGU_CHUNK), lambda g: (g, 0, 0)),
                  pl.BlockSpec((None, SGU_CHUNK, 1), lambda g: (g, 0, 0))],
        out_specs=col, out_shape=jax.ShapeDtypeStruct((s_len, w), BF16),
        compiler_params=_params("parallel"))(u, vn, w_s, b_s)


def _sgu_mix_bwd(u, vn, w_s, b_s, dgated, *, name):
    s_len, w = u.shape
    nc = s_len // SGU_CHUNK

    def body(u_ref, v_ref, ws_ref, b_ref, dg_ref, du_ref, dv_ref, dws_ref, dbs_ref):
        wf, keep = _tril_weights(ws_ref)
        wm = wf.astype(BF16)
        wmt = wf.T.astype(BF16)
        dws = jnp.zeros((SGU_CHUNK, SGU_CHUNK), F32)
        dbs = jnp.zeros((SGU_CHUNK, 1), F32)
        for n in range(nc):
            rows = slice(n * SGU_CHUNK, (n + 1) * SGU_CHUNK)
            vb = v_ref[rows, :]
            dgv = dg_ref[rows, :].astype(F32)
            mixed = _dot(wm, vb, 1, 0) + b_ref[...]
            du_ref[rows, :] = dgv * mixed
            dm = dgv * u_ref[rows, :]
            dmb = dm.astype(BF16)
            dws = dws + _dot(dmb, vb, 1, 1)
            dbs = dbs + jnp.sum(dm, axis=1, keepdims=True)
            dv_ref[rows, :] = _dot(wmt, dmb, 1, 0)
        dws_ref[...] = jnp.where(keep, dws, 0.0)
        dbs_ref[...] = dbs

    col = pl.BlockSpec((s_len, LANES), lambda g: (0, g))
    wspec = pl.BlockSpec((None, SGU_CHUNK, SGU_CHUNK), lambda g: (g, 0, 0))
    bspec = pl.BlockSpec((None, SGU_CHUNK, 1), lambda g: (g, 0, 0))
    return pl.pallas_call(
        body, name=name, grid=(w // LANES,), in_specs=[col, col, wspec, bspec, col],
        out_specs=[col, col, wspec, bspec],
        out_shape=[jax.ShapeDtypeStruct((s_len, w), F32), jax.ShapeDtypeStruct((s_len, w), F32),
                   jax.ShapeDtypeStruct(w_s.shape, F32), jax.ShapeDtypeStruct(b_s.shape, F32)],
        compiler_params=_params("parallel"))(u, vn, w_s, b_s, dgated)


def _shift_down(x, k):
    if k == 0:
        return x
    return jnp.where(_iota(x.shape, 0) >= k, pltpu.roll(x, k, axis=0), 0.0)


def _shift_up(x, k):
    if k == 0:
        return x
    n = x.shape[0]
    return jnp.where(_iota(x.shape, 0) < n - k, pltpu.roll(x, n - k, axis=0), 0.0)


def _conv(u, w_ref, b_ref):
    return b_ref[...] + w_ref[0:1, :] * _shift_down(u, 2) + w_ref[1:2, :] * _shift_down(u, 1) + w_ref[2:3, :] * u


def _sigmoid(x):
    return 0.5 * jnp.tanh(0.5 * x) + 0.5


def _glu_fwd(up, cw, cb, *, name, bc=256):
    s_len, f2 = up.shape
    f = f2 // 2
    bc = _div_block(f, bc)
    nf = f // bc

    def body(ug_ref, uv_ref, wg_ref, wv_ref, bg_ref, bv_ref, o_ref):
        yg = _conv(ug_ref[...], wg_ref, bg_ref)
        yv = _conv(uv_ref[...], wv_ref, bv_ref)
        o_ref[...] = (yg * _sigmoid(yg) * yv).astype(BF16)

    big = lambda off: pl.BlockSpec((s_len, bc), lambda j: (0, j + off))
    wsp = lambda off: pl.BlockSpec((3, bc), lambda j: (0, j + off))
    bsp = lambda off: pl.BlockSpec((1, bc), lambda j: (0, j + off))
    return pl.pallas_call(
        body, name=name, grid=(nf,), in_specs=[big(0), big(nf), wsp(0), wsp(nf), bsp(0), bsp(nf)],
        out_specs=pl.BlockSpec((s_len, bc), lambda j: (0, j)), out_shape=jax.ShapeDtypeStruct((s_len, f), BF16),
        compiler_params=_params("parallel"))(up, up, cw, cw, cb, cb)


def _glu_bwd(up, cw, cb, dact, *, name, bc=256):
    s_len, f2 = up.shape
    f = f2 // 2
    bc = _div_block(f, bc)
    nf = f // bc

    def body(ug_ref, uv_ref, wg_ref, wv_ref, bg_ref, bv_ref, da_ref, du_ref, dw_ref, db_ref):
        ug, uv = ug_ref[...], uv_ref[...]
        yg = _conv(ug, wg_ref, bg_ref)
        yv = _conv(uv, wv_ref, bv_ref)
        da = da_ref[...].astype(F32)
        sg = _sigmoid(yg)
        planes = ((da * yv * (sg * (1.0 + yg * (1.0 - sg))), ug, wg_ref), (da * (yg * sg), uv, wv_ref))
        for plane, (dy, u, w_ref) in enumerate(planes):
            db_ref[plane] = jnp.sum(dy, axis=0, keepdims=True)
            dw_ref[plane, 0:1, :] = jnp.sum(dy * _shift_down(u, 2), axis=0, keepdims=True)
            dw_ref[plane, 1:2, :] = jnp.sum(dy * _shift_down(u, 1), axis=0, keepdims=True)
            dw_ref[plane, 2:3, :] = jnp.sum(dy * u, axis=0, keepdims=True)
            du = w_ref[2:3, :] * dy + w_ref[1:2, :] * _shift_up(dy, 1) + w_ref[0:1, :] * _shift_up(dy, 2)
            du_ref[plane] = du.astype(BF16)

    big = lambda off: pl.BlockSpec((s_len, bc), lambda j: (0, j + off))
    wsp = lambda off: pl.BlockSpec((3, bc), lambda j: (0, j + off))
    bsp = lambda off: pl.BlockSpec((1, bc), lambda j: (0, j + off))
    planes = lambda r: pl.BlockSpec((2, r, bc), lambda j: (0, 0, j))
    return pl.pallas_call(
        body, name=name, grid=(nf,),
        in_specs=[big(0), big(nf), wsp(0), wsp(nf), bsp(0), bsp(nf), pl.BlockSpec((s_len, bc), lambda j: (0, j))],
        out_specs=[planes(s_len), planes(3), planes(1)],
        out_shape=[jax.ShapeDtypeStruct((2, s_len, f), BF16), jax.ShapeDtypeStruct((2, 3, f), F32),
                   jax.ShapeDtypeStruct((2, 1, f), F32)],
        compiler_params=_params("parallel"))(up, up, cw, cw, cb, cb, dact)


def _as2d(a):
    return a.reshape(-1, a.shape[-1]) if a.ndim >= 2 else a.reshape(1, -1)


def _adamw(w, g, m, v, *, name, target_bytes=1 << 20):
    shape = w.shape
    w2, m2, v2 = _as2d(w), _as2d(m), _as2d(v)
    g2 = g.reshape(w2.shape)
    r, c = w2.shape
    br = r if r * c * 4 <= target_bytes else _div_block(r, max(8, target_bytes // (4 * c) // 8 * 8), 8)
    c1 = 1.0 - ADAM_B1 ** ADAM_STEP
    c2 = 1.0 - ADAM_B2 ** ADAM_STEP

    def body(w_ref, g_ref, m_ref, v_ref, d_ref, nm_ref, nv_ref):
        gv = g_ref[...]
        nm = ADAM_B1 * m_ref[...] + (1.0 - ADAM_B1) * gv
        nv = ADAM_B2 * v_ref[...] + (1.0 - ADAM_B2) * (gv * gv)
        nm_ref[...] = nm
        nv_ref[...] = nv
        d_ref[...] = -ADAM_LR * ((nm / c1) / (jnp.sqrt(nv / c2) + ADAM_EPS) + ADAM_WD * w_ref[...])

    spec = pl.BlockSpec((br, c), lambda i: (i, 0))
    sds = jax.ShapeDtypeStruct((r, c), F32)
    d, nm, nv = pl.pallas_call(
        body, name=name, grid=(r // br,), in_specs=[spec] * 4, out_specs=[spec] * 3, out_shape=[sds] * 3,
        compiler_params=_params("parallel"))(w2, g2, m2, v2)
    return d.reshape(shape), nm.reshape(shape), nv.reshape(shape)


def _add_halves(g, recv, place, *, name, target_bytes=1 << 20):
    _, _, r, c = g.shape
    br = _div_block(r, max(16, target_bytes // (2 * c) // 16 * 16), 16)

    def body(x_ref, y_ref, c_ref, g_ref, r_ref, o_ref):
        o_ref[...] = (g_ref[...].astype(F32) + r_ref[...].astype(F32)).astype(BF16)

    return pl.pallas_call(
        body, name=name,
        grid_spec=pltpu.PrefetchScalarGridSpec(
            num_scalar_prefetch=3, grid=(N_CHIPS, r // br),
            in_specs=[pl.BlockSpec((None, None, br, c), lambda s, i, xr, yr, cr: (s, cr[0], i, 0)),
                      pl.BlockSpec((None, br, c), lambda s, i, xr, yr, cr: (s, i, 0))],
            out_specs=pl.BlockSpec((None, br, c), lambda s, i, xr, yr, cr: (s, i, 0))),
        out_shape=jax.ShapeDtypeStruct((N_CHIPS, r, c), BF16),
        compiler_params=_params("parallel", "parallel"))(*place, g, recv)


def _sum_chips(p, landed, place, *, name, target_bytes=1 << 20):
    _, r, c = p.shape
    br = _div_block(r, max(16, target_bytes // (4 * c) // 16 * 16), 16)

    def body(x_ref, y_ref, c_ref, p_ref, l1_ref, l2_ref, l3_ref, o_ref):
        o_ref[...] = ((p_ref[...].astype(F32) + l1_ref[...].astype(F32)) + l2_ref[...].astype(F32)) + l3_ref[...].astype(F32)

    slot = lambda k: pl.BlockSpec((None, br, c), lambda i, xr, yr, cr: ((2 * xr[0] + yr[0] + k) % N_CHIPS, i, 0))
    return pl.pallas_call(
        body, name=name,
        grid_spec=pltpu.PrefetchScalarGridSpec(
            num_scalar_prefetch=3, grid=(r // br,), in_specs=[slot(0), slot(1), slot(2), slot(3)],
            out_specs=pl.BlockSpec((None, br, c), lambda i, xr, yr, cr: (cr[0], i, 0))),
        out_shape=jax.ShapeDtypeStruct((2, r, c), F32),
        compiler_params=_params("parallel"))(*place, p, landed, landed, landed)


def _place_shard(w, place, *, dtype, name, layer=None, target_bytes=1 << 20):
    r, c = w.shape[-2:]
    hr = r // 2
    mult = 16 if dtype == BF16 else 8
    br = _div_block(hr, max(mult, target_bytes // (4 * c) // mult * mult), mult)
    nb = hr // br

    def body(x_ref, y_ref, c_ref, w_ref, o_ref):
        o_ref[...] = w_ref[...].astype(dtype)

    if layer is None:
        w_spec = pl.BlockSpec((br, c), lambda h, i, xr, yr, cr: (h * nb + i, 0))
    else:
        w_spec = pl.BlockSpec((None, br, c), lambda h, i, xr, yr, cr: (layer, h * nb + i, 0))
    return pl.pallas_call(
        body, name=name,
        grid_spec=pltpu.PrefetchScalarGridSpec(
            num_scalar_prefetch=3, grid=(2, nb), in_specs=[w_spec],
            out_specs=pl.BlockSpec((None, None, br, c), lambda h, i, xr, yr, cr: (2 * xr[0] + yr[0], h, i, 0))),
        out_shape=jax.ShapeDtypeStruct((N_CHIPS, 2, hr, c), dtype),
        compiler_params=_params("parallel", "parallel"))(*place, w)


def _sum_devices(x, *, name):
    n, r, c = x.shape
    br = _div_block(r, 512, 8)

    def body(x_ref, o_ref):
        acc = x_ref[0]
        for s in range(1, n):
            acc = acc + x_ref[s]
        o_ref[...] = acc

    return pl.pallas_call(
        body, name=name, grid=(r // br,), in_specs=[pl.BlockSpec((n, br, c), lambda i: (0, i, 0))],
        out_specs=pl.BlockSpec((br, c), lambda i: (i, 0)), out_shape=jax.ShapeDtypeStruct((r, c), F32),
        compiler_params=_params("parallel"))(x)


_ANY = pl.BlockSpec(memory_space=pl.ANY)


def _place():
    x, y, c = lax.axis_index("x"), lax.axis_index("y"), lax.axis_index("c")
    other_chips = [(1 - x, y), (x, 1 - y), (1 - x, 1 - y)]
    return x, y, c, other_chips


_HBM = pl.BlockSpec(memory_space=pltpu.HBM)
_SEM = pl.BlockSpec(memory_space=pltpu.SEMAPHORE)
_EFFECT = pltpu.SideEffectType.DATAFLOW_SIDE_EFFECTING


def _in_hbm(a):
    return pltpu.with_memory_space_constraint(a, pltpu.HBM)


def _token_spec():
    return pl.BlockSpec(memory_space=pltpu.VMEM), jax.ShapeDtypeStruct((8, LANES), F32)


def _gather_ici_start(bufs, after, *, name):
    n = len(bufs)

    def body(*refs):
        b_refs = refs[:n]
        send_sems, recv_sems = refs[n + 1], refs[n + 2]
        token = refs[-1]
        x, y, c, chips = _place()
        me = 2 * x + y
        for i in range(n):
            for j, (px, py) in enumerate(chips):
                pltpu.make_async_remote_copy(src_ref=b_refs[i].at[me, c], dst_ref=b_refs[i].at[me, c],
                                             send_sem=send_sems.at[3 * i + j], recv_sem=recv_sems.at[3 * i + j],
                                             device_id=(px, py, c), device_id_type=MESH).start()
        token[...] = jnp.zeros_like(token)

    tspec, tshape = _token_spec()
    outs = pl.pallas_call(
        body, name=name, in_specs=[_HBM] * n + [_ANY], out_specs=(_SEM, _SEM, *[_HBM] * n, tspec),
        out_shape=(pltpu.SemaphoreType.DMA((3 * n,)), pltpu.SemaphoreType.DMA((3 * n,)),
                   *[pltpu.HBM(a.shape, a.dtype) for a in bufs], tshape),
        input_output_aliases={i: 2 + i for i in range(n)},
        compiler_params=pltpu.CompilerParams(has_side_effects=_EFFECT),
    )(*[_in_hbm(a) for a in bufs], after)
    return outs[0], outs[1], list(outs[2:2 + n]), outs[-1]


def _gather_ici_wait(send_sems, recv_sems, bufs, after, *, name):
    n = len(bufs)

    def body(*refs):
        b_refs = refs[:n]
        send_sems, recv_sems = refs[n], refs[n + 1]
        x, y, c, chips = _place()
        me = 2 * x + y
        for i in range(n):
            for j, (px, py) in enumerate(chips):
                cp = pltpu.make_async_remote_copy(src_ref=b_refs[i].at[me, c], dst_ref=b_refs[i].at[2 * px + py, c],
                                                  send_sem=send_sems.at[3 * i + j], recv_sem=recv_sems.at[3 * i + j],
                                                  device_id=(px, py, c), device_id_type=MESH)
                cp.wait_send()
                cp.wait_recv()

    outs = pl.pallas_call(
        body, name=name, in_specs=[_HBM] * n + [_SEM, _SEM, _ANY], out_specs=[_HBM] * n,
        out_shape=[pltpu.HBM(a.shape, a.dtype) for a in bufs], input_output_aliases={i: i for i in range(n)},
        compiler_params=pltpu.CompilerParams(has_side_effects=_EFFECT),
    )(*bufs, send_sems, recv_sems, after)
    return list(outs)


def _gather_d2d(bufs, *, name):
    n = len(bufs)

    def body(*refs):
        b_refs = refs[n:2 * n]
        send_sems, recv_sems = refs[2 * n:]
        x, y, c, chips = _place()
        sends = []
        for i in range(n):
            for j, (px, py) in enumerate(chips):
                mine = b_refs[i].at[2 * px + py, c]
                cp = pltpu.make_async_remote_copy(src_ref=mine, dst_ref=mine, send_sem=send_sems.at[3 * i + j],
                                                  recv_sem=recv_sems.at[3 * i + j], device_id=(x, y, 1 - c),
                                                  device_id_type=MESH)
                cp.start()
                sends.append((cp, i, j, px, py))
        for cp, i, j, px, py in sends:
            theirs = b_refs[i].at[2 * px + py, 1 - c]
            pltpu.make_async_remote_copy(src_ref=theirs, dst_ref=theirs, send_sem=send_sems.at[3 * i + j],
                                         recv_sem=recv_sems.at[3 * i + j], device_id=(x, y, 1 - c),
                                         device_id_type=MESH).wait_recv()
            cp.wait_send()

    return pl.pallas_call(
        body, name=name, in_specs=[_ANY] * n, out_specs=[_ANY] * n, input_output_aliases={i: i for i in range(n)},
        out_shape=[jax.ShapeDtypeStruct(a.shape, a.dtype) for a in bufs],
        scratch_shapes=[pltpu.SemaphoreType.DMA((3 * n,)), pltpu.SemaphoreType.DMA((3 * n,))],
    )(*bufs)


def _sibling_halves(gs, *, name):
    n = len(gs)

    def body(*refs):
        g_refs, o_refs = refs[:n], refs[n:2 * n]
        send_sems, recv_sems = refs[2 * n:]
        x, y, c, _ = _place()
        copies = []
        for i in range(n):
            for s in range(N_CHIPS):
                k = i * N_CHIPS + s
                cp = pltpu.make_async_remote_copy(src_ref=g_refs[i].at[s, 1 - c], dst_ref=o_refs[i].at[s],
                                                  send_sem=send_sems.at[k], recv_sem=recv_sems.at[k],
                                                  device_id=(x, y, 1 - c), device_id_type=MESH)
                cp.start()
                copies.append(cp)
        for cp in copies:
            cp.wait()

    return pl.pallas_call(
        body, name=name, in_specs=[_ANY] * n, out_specs=[_ANY] * n,
        out_shape=[jax.ShapeDtypeStruct((N_CHIPS,) + g.shape[2:], g.dtype) for g in gs],
        scratch_shapes=[pltpu.SemaphoreType.DMA((n * N_CHIPS,)), pltpu.SemaphoreType.DMA((n * N_CHIPS,))],
    )(*gs)


def _chip_scatter_start(ps, after, *, name):
    n = len(ps)

    def body(*refs):
        p_refs, l_refs = refs[:n], refs[n:2 * n]
        send_sems, recv_sems = refs[2 * n + 1], refs[2 * n + 2]
        token = refs[-1]
        x, y, c, chips = _place()
        me = 2 * x + y
        for i in range(n):
            for j, (px, py) in enumerate(chips):
                pltpu.make_async_remote_copy(src_ref=p_refs[i].at[2 * px + py], dst_ref=l_refs[i].at[me],
                                             send_sem=send_sems.at[3 * i + j], recv_sem=recv_sems.at[3 * i + j],
                                             device_id=(px, py, c), device_id_type=MESH).start()
        token[...] = jnp.zeros_like(token)

    tspec, tshape = _token_spec()
    lands = [_in_hbm(lax.empty(p.shape, p.dtype)) for p in ps]
    outs = pl.pallas_call(
        body, name=name, in_specs=[_HBM] * (2 * n) + [_ANY], out_specs=(_SEM, _SEM, *[_HBM] * (2 * n), tspec),
        out_shape=(pltpu.SemaphoreType.DMA((3 * n,)), pltpu.SemaphoreType.DMA((3 * n,)),
                   *[pltpu.HBM(p.shape, p.dtype) for p in ps], *[pltpu.HBM(p.shape, p.dtype) for p in ps], tshape),
        input_output_aliases={i: 2 + i for i in range(2 * n)},
        compiler_params=pltpu.CompilerParams(has_side_effects=_EFFECT),
    )(*[_in_hbm(p) for p in ps], *lands, after)
    return outs[0], outs[1], list(outs[2:2 + n]), list(outs[2 + n:2 + 2 * n]), outs[-1]


def _chip_scatter_wait(send_sems, recv_sems, ps, lands, after, *, name):
    n = len(ps)

    def body(*refs):
        p_refs, l_refs = refs[:n], refs[n:2 * n]
        send_sems, recv_sems = refs[2 * n], refs[2 * n + 1]
        x, y, c, chips = _place()
        for i in range(n):
            for j, (px, py) in enumerate(chips):
                cp = pltpu.make_async_remote_copy(src_ref=p_refs[i].at[2 * px + py], dst_ref=l_refs[i].at[2 * px + py],
                                                  send_sem=send_sems.at[3 * i + j], recv_sem=recv_sems.at[3 * i + j],
                                                  device_id=(px, py, c), device_id_type=MESH)
                cp.wait_send()
                cp.wait_recv()

    outs = pl.pallas_call(
        body, name=name, in_specs=[_HBM] * (2 * n) + [_SEM, _SEM, _ANY], out_specs=[_HBM] * (2 * n),
        out_shape=[pltpu.HBM(p.shape, p.dtype) for p in ps] * 2, input_output_aliases={i: i for i in range(2 * n)},
        compiler_params=pltpu.CompilerParams(has_side_effects=_EFFECT),
    )(*ps, *lands, send_sems, recv_sems, after)
    return list(outs[:n]), list(outs[n:])


def _sibling_share(bufs, *, name):
    n = len(bufs)

    def body(*refs):
        b_refs = refs[n:2 * n]
        send_sems, recv_sems = refs[2 * n:]
        x, y, c, _ = _place()
        copies = []
        for i in range(n):
            cp = pltpu.make_async_remote_copy(src_ref=b_refs[i].at[c], dst_ref=b_refs[i].at[c], send_sem=send_sems.at[i],
                                              recv_sem=recv_sems.at[i], device_id=(x, y, 1 - c), device_id_type=MESH)
            cp.start()
            copies.append((cp, i))
        for cp, i in copies:
            theirs = b_refs[i].at[1 - c]
            pltpu.make_async_remote_copy(src_ref=theirs, dst_ref=theirs, send_sem=send_sems.at[i],
                                         recv_sem=recv_sems.at[i], device_id=(x, y, 1 - c),
                                         device_id_type=MESH).wait_recv()
            cp.wait_send()

    return pl.pallas_call(
        body, name=name, in_specs=[_ANY] * n, out_specs=[_ANY] * n, input_output_aliases={i: i for i in range(n)},
        out_shape=[jax.ShapeDtypeStruct(a.shape, a.dtype) for a in bufs],
        scratch_shapes=[pltpu.SemaphoreType.DMA((n,)), pltpu.SemaphoreType.DMA((n,))],
    )(*bufs)


def _broadcast_all(v, *, name):
    def body(v_ref, o_ref, send_sems, recv_sems, local_sem):
        x, y, c, _ = _place()
        me = 4 * x + 2 * y + c
        loc = pltpu.make_async_copy(v_ref, o_ref.at[me], local_sem)
        loc.start()
        copies = []
        for k in range(1, 8):
            dx, dy, dc = (k >> 2) & 1, (k >> 1) & 1, k & 1
            to = (1 - x if dx else x, 1 - y if dy else y, 1 - c if dc else c)
            cp = pltpu.make_async_remote_copy(src_ref=v_ref, dst_ref=o_ref.at[me], send_sem=send_sems.at[k - 1],
                                              recv_sem=recv_sems.at[k - 1], device_id=to, device_id_type=MESH)
            cp.start()
            copies.append((cp, k, to))
        for cp, k, to in copies:
            cp.wait_send()
            theirs = o_ref.at[4 * to[0] + 2 * to[1] + to[2]]
            pltpu.make_async_remote_copy(src_ref=theirs, dst_ref=theirs, send_sem=send_sems.at[k - 1],
                                         recv_sem=recv_sems.at[k - 1], device_id=to, device_id_type=MESH).wait_recv()
        loc.wait()

    return pl.pallas_call(
        body, name=name, in_specs=[_ANY], out_specs=_ANY,
        out_shape=jax.ShapeDtypeStruct((8,) + v.shape, v.dtype),
        scratch_shapes=[pltpu.SemaphoreType.DMA((7,)), pltpu.SemaphoreType.DMA((7,)), pltpu.SemaphoreType.DMA(())],
    )(v)


def _gather_begin(shards, place, after, *, name):
    names = list(shards)
    bufs, shapes = [], []
    for k in names:
        w, layer = shards[k] if isinstance(shards[k], tuple) else (shards[k], None)
        bufs.append(_place_shard(w, place, dtype=F32 if k == 'small' else BF16, layer=layer, name=f"{name}_place_{k}"))
        shapes.append(w.shape[-2:])
    send_sems, recv_sems, bufs, token = _gather_ici_start(bufs, after, name=name + "_ici_start")
    return (names, shapes, send_sems, recv_sems, bufs), token


def _gather_end(state, after, *, name):
    names, shapes, send_sems, recv_sems, bufs = state
    bufs = _gather_ici_wait(send_sems, recv_sems, bufs, after, name=name + "_ici_wait")
    bufs = _gather_d2d(bufs, name=name + "_d2d")
    return {k: o.reshape((N_CHIPS,) + sh) for k, o, sh in zip(names, bufs, shapes)}


def _reduce_begin(grads, place, *, name):
    names = list(grads)
    gs = [grads[k].reshape(N_CHIPS, 2, grads[k].shape[1] // 2, grads[k].shape[2]) for k in names]
    recv = _sibling_halves(gs, name=name + "_sib")
    ps = [_add_halves(g, r, place, name=f"{name}_add2_{k}") for g, r, k in zip(gs, recv, names)]
    send_sems, recv_sems, ps, lands, token = _chip_scatter_start(ps, recv[0], name=name + "_scatter_start")
    return (names, [grads[k].shape[1:] for k in names], send_sems, recv_sems, ps, lands), token


def _reduce_end(state, place, after, *, name):
    names, shapes, send_sems, recv_sems, ps, lands = state
    ps, lands = _chip_scatter_wait(send_sems, recv_sems, ps, lands, after, name=name + "_scatter_wait")
    rs = [_sum_chips(p, l, place, name=f"{name}_sum4_{k}") for p, l, k in zip(ps, lands, names)]
    both = _sibling_share(rs, name=name + "_share")
    return {k: b.reshape(sh) for k, b, sh in zip(names, both, shapes)}


def _pad_lanes(a, n=LANES):
    return jnp.pad(a, [(0, 0)] * (a.ndim - 1) + [(0, n - a.shape[-1])])


def _unshard_cols(g):
    return jnp.transpose(g, (1, 0, 2)).reshape(g.shape[1], -1)


def _shard_cols(w):
    k, n = w.shape
    return jnp.transpose(w.reshape(k, N_CHIPS, n // N_CHIPS), (1, 0, 2))


def _ffn_fwd(h, p, tag):
    b = _rms_fwd(h, p['ffn_norm'], name=f"{tag}_ffn_norm")
    up = _mm(b, p['ffn_w_up'], b_sh='n', name=f"{tag}_ffn_up", bn=1408)
    act = _glu_fwd(up, p['ffn_conv_w'], p['ffn_conv_b'], name=f"{tag}_ffn_glu")
    out = _mm(act, p['ffn_w_down'], res=h, name=f"{tag}_ffn_down", bk=704)
    return out, (h, b, up, act)


def _ffn_bwd(dh, saved, p, tag, after=None):
    h, b, up, act = saved
    dact = _mm(dh, p['ffn_w_down'], tb=True, after=after, out_dtype=BF16, name=f"{tag}_ffn_dact", bn=1408)
    dw_down = _mm(act, dh, ta=True, after=after, out_dtype=BF16, name=f"{tag}_ffn_dwdown", bm=1408)
    dup, dcw, dcb = _glu_bwd(up, p['ffn_conv_w'], p['ffn_conv_b'], dact, name=f"{tag}_ffn_dglu")
    dw_up = _mm(b, dup, ta=True, b_sh='n', o_sh=True, out_dtype=BF16, name=f"{tag}_ffn_dwup", bn=1408)
    db = _mm(dup, p['ffn_w_up'], a_sh=True, b_sh='k', name=f"{tag}_ffn_db", bk=1408)
    dcw = jnp.transpose(dcw, (1, 0, 2)).reshape(dcw.shape[1], -1)
    dcb = dcb.reshape(1, -1)
    dh_in, dg = _rms_bwd(h, p['ffn_norm'], db, res=dh, name=f"{tag}_ffn_dnorm")
    big = {'ffn_w_up': dw_up, 'ffn_w_down': dw_down.reshape(N_CHIPS, -1, dw_down.shape[1])}
    small = {'ffn_norm': dg, 'ffn_conv_w': dcw, 'ffn_conv_b': dcb}
    return dh_in, big, small


def _qkv_attn_fwd(kind, h, p, tag, n_heads):
    a = _rms_fwd(h, p['mix_norm'], name=f"{tag}_norm")
    if kind == 'fox':
        qkv = _mm(a, p['w_in'], name=f"{tag}_qkv", bn=896)
        cum = _fgate_fwd(qkv, p['b_f'], fcol=3 * n_heads, name=f"{tag}_fgate")
        cum_t = cum[:, :n_heads].T
        cq, ck = cum_t[:, :, None], cum_t[:, None, :]
    else:
        qkv = _mm(a, p['w_in'], b_sh='n', name=f"{tag}_qkv", bn=768)
        cq = ck = None
    cols = dict(qcol=lambda hh: hh, kcol=lambda hh: n_heads + hh, vcol=lambda hh: 2 * n_heads + hh)
    o = _attn_fwd(kind, qkv, qkv, qkv, name=f"{tag}_attn", n_heads=n_heads, dqk=HEAD_DIM, scale=HEAD_DIM ** -0.5,
                  gains=p['qk_gain'], cq=cq, ck=ck, **cols)
    out = _mm(o, p['w_out'], res=h, name=f"{tag}_out")
    return out, (h, a, qkv, o, cq, ck)


def _qkv_attn_bwd(kind, dh, saved, p, tag, n_heads, after=None):
    h, a, qkv, o, cq, ck = saved
    do = _mm(dh, p['w_out'], tb=True, after=after, out_dtype=BF16, name=f"{tag}_do")
    dw_out = _mm(o, dh, ta=True, after=after, out_dtype=BF16, name=f"{tag}_dwout")
    cols = dict(qcol=lambda hh: hh, kcol=lambda hh: n_heads + hh, vcol=lambda hh: 2 * n_heads + hh)
    outs = _attn_bwd(kind, qkv, qkv, qkv, o, do, name=f"{tag}_dattn", n_heads=n_heads, dqk=HEAD_DIM,
                     scale=HEAD_DIM ** -0.5, gains=p['qk_gain'], cq=cq, ck=ck, **cols)
    dq, dk, dv, dgain = outs[:4]
    small = {'q_gain': dgain[0], 'k_gain': dgain[1]}
    if kind == 'fox':
        dcq, dck = outs[4:]
        dca = _pad_lanes(dcq[:, :, 0].T)
        dcb = _pad_lanes(dck[:, 0, :].T)
        dflog, dbf = _fgate_bwd(qkv, p['b_f'], dca, dcb, fcol=3 * n_heads, n_heads=n_heads, name=f"{tag}_dfgate")
        small['b_f'] = dbf[:, :n_heads]
        dqkv = jnp.concatenate([dq, dk, dv, dflog], axis=1)
        dw_in = _mm(a, dqkv, ta=True, out_dtype=BF16, name=f"{tag}_dwin", bn=896)
        da = _mm(dqkv, p['w_in'], tb=True, name=f"{tag}_da", bk=896)
        dw_in = _shard_cols(dw_in[:, :3 * n_heads * HEAD_DIM + n_heads])
    else:
        dqkv = jnp.concatenate([dq, dk, dv], axis=1)
        dw_in = _mm(a, dqkv, ta=True, o_sh=True, out_dtype=BF16, name=f"{tag}_dwin", bn=768)
        da = _mm(dqkv, p['w_in'], b_sh='k', name=f"{tag}_da", bk=768)
    dh_in, dg = _rms_bwd(h, p['mix_norm'], da, res=dh, name=f"{tag}_dnorm")
    small['mix_norm'] = dg
    big = {'w_in': dw_in, 'w_out': dw_out.reshape(N_CHIPS, -1, dw_out.shape[1])}
    return dh_in, big, small


def _mla_fwd(h, p, tag, n_heads):
    a = _rms_fwd(h, p['mix_norm'], name=f"{tag}_norm")
    c = _mm(a, p['w_in'], name=f"{tag}_latent", bn=1152)
    cn = _mla_latent_fwd(c, p['a_gain'], name=f"{tag}_latent_norm")
    qp = _mm(cn[:, :MLA_Q_RANK], p['w_q_b'], name=f"{tag}_q_up")
    kv = _mm(cn[:, MLA_Q_RANK:], p['w_kv_b'], b_sh='n', name=f"{tag}_kv_up")
    qc, kc = _mla_prep_fwd(qp, kv, c, p['gq'], p['gk'], p['cos'], p['sin'], n_heads=n_heads, name=f"{tag}_prep")
    cols = dict(qcol=lambda hh: hh, kcol=lambda hh: hh, vcol=lambda hh: 2 * hh + 1)
    scale = (MLA_NOPE + MLA_ROPE) ** -0.5
    o = _attn_fwd('mla', qc, kc, kv, name=f"{tag}_attn", n_heads=n_heads, dqk=2 * LANES, scale=scale, **cols)
    out = _mm(o, p['w_out'], res=h, name=f"{tag}_out")
    return out, (h, a, c, cn, qp, kv, qc, kc, o)


def _mla_bwd(dh, saved, p, tag, n_heads, after=None):
    h, a, c, cn, qp, kv, qc, kc, o = saved
    do = _mm(dh, p['w_out'], tb=True, after=after, out_dtype=BF16, name=f"{tag}_do")
    dw_out = _mm(o, dh, ta=True, after=after, out_dtype=BF16, name=f"{tag}_dwout")
    cols = dict(qcol=lambda hh: hh, kcol=lambda hh: hh, vcol=lambda hh: 2 * hh + 1)
    scale = (MLA_NOPE + MLA_ROPE) ** -0.5
    dqc, dkc, dv = _attn_bwd('mla', qc, kc, kv, o, do, name=f"{tag}_dattn", n_heads=n_heads, dqk=2 * LANES,
                             scale=scale, **cols)
    dqn, dqr, dkv, dkr, dgq, dgk = _mla_prep_bwd(qp, kv, c, p['gq'], p['gk'], p['cos'], p['sin'], dqc, dkc, dv,
                                                 n_heads=n_heads, name=f"{tag}_dprep")
    dqp = jnp.concatenate([dqn, dqr], axis=1)
    cn_q, cn_kv = cn[:, :MLA_Q_RANK], cn[:, MLA_Q_RANK:]
    dw_q_b = _mm(cn_q, dqp, ta=True, out_dtype=BF16, name=f"{tag}_dwqb", bm=512)
    dcn_q = _mm(dqp, p['w_q_b'], tb=True, out_dtype=BF16, name=f"{tag}_dcnq")
    dw_kv_b = _mm(cn_kv, dkv, ta=True, o_sh=True, out_dtype=BF16, name=f"{tag}_dwkvb", bm=512)
    dcn_kv = _mm(dkv, p['w_kv_b'], b_sh='k', out_dtype=BF16, name=f"{tag}_dcnkv")
    dc, dga = _mla_latent_bwd(c, p['a_gain'], dcn_q, dcn_kv, dkr, name=f"{tag}_dlatent")
    dw_in = _mm(a, dc, ta=True, out_dtype=BF16, name=f"{tag}_dwin", bn=1152)
    da = _mm(dc, p['w_in'], tb=True, name=f"{tag}_da", bk=1152)
    dh_in, dg = _rms_bwd(h, p['mix_norm'], da, res=dh, name=f"{tag}_dnorm")
    k_rank = dw_q_b.shape[0]
    nope = dw_q_b[:, :n_heads * LANES].reshape(k_rank, n_heads, LANES)
    rope = dw_q_b[:, n_heads * LANES:].reshape(k_rank, n_heads, LANES)[:, :, :MLA_ROPE]
    dw_q_b = jnp.concatenate([nope, rope], axis=2).reshape(k_rank, n_heads * (MLA_NOPE + MLA_ROPE))
    w_in_cols = MLA_Q_RANK + MLA_KV_RANK + MLA_ROPE
    big = {'w_in': dw_in[:, :w_in_cols].reshape(N_CHIPS, -1, w_in_cols), 'w_q_b': _shard_cols(dw_q_b),
           'w_kv_b': dw_kv_b, 'w_out': dw_out.reshape(N_CHIPS, -1, dw_out.shape[1])}
    small = {'mix_norm': dg, 'q_a_gain': dga[:, :MLA_Q_RANK], 'kv_a_gain': dga[:, MLA_Q_RANK:],
             'q_gain': jnp.concatenate([dgq[0], dgq[1][:, :MLA_ROPE]], axis=1),
             'k_gain': jnp.concatenate([dgk[0], dgk[1][:, :MLA_ROPE]], axis=1)}
    return dh_in, big, small


def _sgu_fwd(h, p, tag):
    a = _rms_fwd(h, p['mix_norm'], name=f"{tag}_norm")
    uv = _mm(a, p['w_in'], b_sh='n', name=f"{tag}_in")
    u, vn = _sgu_act_fwd(uv, p['v_gain'], name=f"{tag}_act")
    gated = _sgu_mix_fwd(u, vn, p['w_s'], p['b_s'], name=f"{tag}_mix")
    out = _mm(gated, p['w_out'], res=h, name=f"{tag}_out")
    return out, (h, a, uv, u, vn, gated)


def _sgu_bwd(dh, saved, p, tag, after=None):
    h, a, uv, u, vn, gated = saved
    dgated = _mm(dh, p['w_out'], tb=True, after=after, out_dtype=BF16, name=f"{tag}_dgated")
    dw_out = _mm(gated, dh, ta=True, after=after, out_dtype=BF16, name=f"{tag}_dwout")
    du, dvn, dws, dbs = _sgu_mix_bwd(u, vn, p['w_s'], p['b_s'], dgated, name=f"{tag}_dmix")
    duv, dvg = _sgu_act_bwd(uv, p['v_gain'], du, dvn, name=f"{tag}_dact")
    dw_in = _mm(a, duv, ta=True, o_sh=True, out_dtype=BF16, name=f"{tag}_dwin")
    da = _mm(duv, p['w_in'], b_sh='k', name=f"{tag}_da")
    dh_in, dg = _rms_bwd(h, p['mix_norm'], da, res=dh, name=f"{tag}_dnorm")
    big = {'w_in': dw_in, 'w_out': dw_out.reshape(N_CHIPS, -1, dw_out.shape[1])}
    small = {'mix_norm': dg, 'v_gain': dvg, 'w_s': dws, 'b_s': dbs[:, :, 0]}
    return dh_in, big, small


def _pack(parts):
    flat = jnp.concatenate([p.reshape(-1).astype(F32) for p in parts])
    rows = -(-flat.shape[0] // LANES)
    rows = -(-rows // 32) * 32
    return jnp.pad(flat, (0, rows * LANES - flat.shape[0])).reshape(rows, LANES)


def _unpack(packed, shapes):
    flat = packed.reshape(-1)
    out, off = [], 0
    for s in shapes:
        n = 1
        for d in s:
            n *= d
        out.append(flat[off:off + n].reshape(s))
        off += n
    return out


MIXERS = ('fox', 'mla', 'sb', 'sgu')
WEIGHT_NAMES = ['mix_norm', 'ffn_norm', 'fox_w_in', 'fox_b_f', 'fox_q_gain', 'fox_k_gain', 'fox_w_out', 'mla_w_in',
                'mla_q_a_gain', 'mla_kv_a_gain', 'mla_w_q_b', 'mla_w_kv_b', 'mla_q_gain', 'mla_k_gain', 'mla_w_out',
                'sb_w_in', 'sb_q_gain', 'sb_k_gain', 'sb_w_out', 'sgu_w_in', 'sgu_v_gain', 'sgu_w_s', 'sgu_b_s',
                'sgu_w_out', 'ffn_w_up', 'ffn_conv_w', 'ffn_conv_b', 'ffn_w_down']
SMALL_SHARDED = {'mla_q_a_gain': 1, 'mla_kv_a_gain': 1, 'sgu_v_gain': 1, 'ffn_conv_w': 2}
BIG = ['fox_w_in', 'fox_w_out', 'mla_w_in', 'mla_w_q_b', 'mla_w_kv_b', 'mla_w_out', 'sb_w_in', 'sb_w_out', 'sgu_w_in',
       'sgu_w_out', 'ffn_w_up', 'ffn_w_down']


def kernel(x, positions, mix_norm, ffn_norm, fox_w_in, fox_b_f, fox_q_gain, fox_k_gain, fox_w_out, mla_w_in, mla_q_a_gain, mla_kv_a_gain, mla_w_q_b, mla_w_kv_b, mla_q_gain, mla_k_gain, mla_w_out, sb_w_in, sb_q_gain, sb_k_gain, sb_w_out, sgu_w_in, sgu_v_gain, sgu_w_s, sgu_b_s, sgu_w_out, ffn_w_up, ffn_conv_w, ffn_conv_b, ffn_w_down, loss_target, m_mix_norm, m_ffn_norm, m_fox_w_in, m_fox_b_f, m_fox_q_gain, m_fox_k_gain, m_fox_w_out, m_mla_w_in, m_mla_q_a_gain, m_mla_kv_a_gain, m_mla_w_q_b, m_mla_w_kv_b, m_mla_q_gain, m_mla_k_gain, m_mla_w_out, m_sb_w_in, m_sb_q_gain, m_sb_k_gain, m_sb_w_out, m_sgu_w_in, m_sgu_v_gain, m_sgu_w_s, m_sgu_b_s, m_sgu_w_out, m_ffn_w_up, m_ffn_conv_w, m_ffn_conv_b, m_ffn_w_down, v_mix_norm, v_ffn_norm, v_fox_w_in, v_fox_b_f, v_fox_q_gain, v_fox_k_gain, v_fox_w_out, v_mla_w_in, v_mla_q_a_gain, v_mla_kv_a_gain, v_mla_w_q_b, v_mla_w_kv_b, v_mla_q_gain, v_mla_k_gain, v_mla_w_out, v_sb_w_in, v_sb_q_gain, v_sb_k_gain, v_sb_w_out, v_sgu_w_in, v_sgu_v_gain, v_sgu_w_s, v_sgu_b_s, v_sgu_w_out, v_ffn_w_up, v_ffn_conv_w, v_ffn_conv_b, v_ffn_w_down):
    args = dict(locals())
    W = {k: args[k] for k in WEIGHT_NAMES}
    M = {k: args['m_' + k] for k in WEIGHT_NAMES}
    V = {k: args['v_' + k] for k in WEIGHT_NAMES}
    depth = mix_norm.shape[0]
    s_len, d_model = x.shape[1], x.shape[2]
    n_heads = d_model // HEAD_DIM
    assert all(W[k].shape[0] == 1 for k in WEIGHT_NAMES if k.split('_')[0] in MIXERS), "one layer per mixer"
    xi, yi, ci = lax.axis_index("x"), lax.axis_index("y"), lax.axis_index("c")
    chip = 2 * xi + yi
    place = tuple(jnp.reshape(v, (1,)).astype(jnp.int32) for v in (xi, yi, ci))

    small_local = _pack([W[k][0] if k != 'ffn_conv_w' else W[k] for k in SMALL_SHARDED])

    def piece_shards(i, part):
        if part == 'ffn':
            return {'ffn_w_up': (W['ffn_w_up'], i), 'ffn_w_down': (W['ffn_w_down'], i)}
        mixer = MIXERS[i % len(MIXERS)]
        shards = {k: W[k][0] for k in BIG if k.startswith(mixer + '_')}
        if i == 0:
            shards['small'] = small_local
        return shards

    pieces = [(i, part) for i in range(depth) for part in ('mixer', 'ffn')]
    gathered = {}
    pname = lambda pc: f"gather_l{pc[0]}_{pc[1]}"
    states = {}
    states[pieces[0]], token = _gather_begin(piece_shards(*pieces[0]), place, mix_norm, name=pname(pieces[0]))
    gathered[pieces[0]] = _gather_end(states.pop(pieces[0]), token, name=pname(pieces[0]))
    first_done = next(iter(gathered[pieces[0]].values()))
    states[pieces[1]], token = _gather_begin(piece_shards(*pieces[1]), place, first_done, name=pname(pieces[1]))
    tokens = [token]
    small_shapes = [W[k][0].shape if k != 'ffn_conv_w' else W[k].shape for k in SMALL_SHARDED]
    per_chip = [_unpack(gathered[pieces[0]]['small'][s], small_shapes) for s in range(N_CHIPS)]
    full_small = {k: jnp.concatenate([per_chip[s][j] for s in range(N_CHIPS)], axis=-1)
                  for j, k in enumerate(SMALL_SHARDED)}

    pos = positions.reshape(s_len).astype(F32)
    inv_freq = ROPE_THETA ** (-jnp.arange(0, MLA_ROPE, 2, dtype=F32) / MLA_ROPE)
    ang = pos[:, None] * inv_freq
    cos_t = _pad_lanes(jnp.concatenate([jnp.cos(ang), jnp.cos(ang)], axis=1))
    sin_t = _pad_lanes(jnp.concatenate([-jnp.sin(ang), jnp.sin(ang)], axis=1))

    def piece_params(i, part):
        mixer = MIXERS[i % len(MIXERS)]
        g = gathered[(i, part)]
        if part == 'ffn':
            return mixer, {'ffn_norm': ffn_norm[i:i + 1], 'ffn_w_up': g['ffn_w_up'],
                           'ffn_w_down': g['ffn_w_down'].reshape(-1, d_model),
                           'ffn_conv_w': full_small['ffn_conv_w'][i], 'ffn_conv_b': ffn_conv_b[i:i + 1]}
        p = {'mix_norm': mix_norm[i:i + 1]}
        rows = lambda w: w.reshape(-1, w.shape[-1])
        if mixer == 'fox':
            w = _unshard_cols(g['fox_w_in'])
            p['w_in'] = jnp.pad(w, ((0, 0), (0, (3 * n_heads + 1) * HEAD_DIM - w.shape[1])))
            p['b_f'] = _pad_lanes(fox_b_f)
            p['qk_gain'] = jnp.stack([fox_q_gain, fox_k_gain])
            p['w_out'] = rows(g['fox_w_out'])
        elif mixer == 'sb':
            p['w_in'] = g['sb_w_in']
            p['qk_gain'] = jnp.stack([sb_q_gain, sb_k_gain])
            p['w_out'] = rows(g['sb_w_out'])
        elif mixer == 'sgu':
            p['w_in'] = g['sgu_w_in']
            p['v_gain'] = full_small['sgu_v_gain'].reshape(1, -1)
            p['w_s'] = sgu_w_s[0]
            p['b_s'] = sgu_b_s[0][:, :, None]
            p['w_out'] = rows(g['sgu_w_out'])
        else:
            w = rows(g['mla_w_in'])
            p['w_in'] = jnp.pad(w, ((0, 0), (0, MLA_Q_RANK + MLA_KV_RANK + LANES - w.shape[1])))
            p['a_gain'] = jnp.concatenate([full_small['mla_q_a_gain'], full_small['mla_kv_a_gain']]).reshape(1, -1)
            wq = _unshard_cols(g['mla_w_q_b']).reshape(MLA_Q_RANK, n_heads, MLA_NOPE + MLA_ROPE)
            p['w_q_b'] = jnp.concatenate([wq[:, :, :MLA_NOPE].reshape(MLA_Q_RANK, -1),
                                          _pad_lanes(wq[:, :, MLA_NOPE:]).reshape(MLA_Q_RANK, -1)], axis=1)
            p['w_kv_b'] = g['mla_w_kv_b']
            p['gq'] = jnp.stack([mla_q_gain[:, :MLA_NOPE], _pad_lanes(mla_q_gain[:, MLA_NOPE:])])
            p['gk'] = jnp.stack([mla_k_gain[:, :MLA_NOPE], _pad_lanes(mla_k_gain[:, MLA_NOPE:])])
            p['cos'], p['sin'] = cos_t, sin_t
            p['w_out'] = rows(g['mla_w_out'])
        return mixer, p

    h = x.reshape(s_len, d_model)
    saved = []
    for n, (i, part) in enumerate(pieces):
        nxt = pieces[n + 1] if n + 1 < len(pieces) else None
        ahead = pieces[n + 2] if n + 2 < len(pieces) else None
        if ahead is not None:
            after = next(iter(gathered[(i, part)].values()))
            states[ahead], token = _gather_begin(piece_shards(*ahead), place, after, name=pname(ahead))
            tokens.append(token)
        mixer, p = piece_params(i, part)
        gain = 'ffn_norm' if part == 'ffn' else 'mix_norm'
        for token in tokens:
            p[gain] = p[gain] + token[0:1, 0:1]
        tokens = []
        tag = f"l{i}_{mixer}"
        if part == 'ffn':
            h, sv = _ffn_fwd(h, p, f"l{i}")
        elif mixer in ('fox', 'sb'):
            h, sv = _qkv_attn_fwd(mixer, h, p, tag, n_heads)
        elif mixer == 'mla':
            h, sv = _mla_fwd(h, p, tag, n_heads)
        else:
            h, sv = _sgu_fwd(h, p, tag)
        saved.append((mixer, p, sv))
        if nxt is not None:
            gathered[nxt] = _gather_end(states.pop(nxt), h, name=pname(nxt))
    loss_row, dh = _loss(h, loss_target.reshape(s_len, d_model))
    loss = lax.psum(loss_row[0, 0], ("x", "y", "c"))

    big_grads, small_grads = {}, {k: [None] * depth for k in ('mix_norm', 'ffn_norm', 'ffn_conv_w', 'ffn_conv_b')}

    def keep(reduced, i):
        for k, v in reduced.items():
            if k.startswith('ffn_'):
                big_grads.setdefault(k, [None] * depth)[i] = v
            else:
                big_grads[k] = v[None]

    state, token, flying = None, None, None
    for n in reversed(range(len(pieces))):
        i, part = pieces[n]
        mixer, p, sv = saved[n]
        tag = f"l{i}_{mixer}"
        if part == 'ffn':
            dh, big, small = _ffn_bwd(dh, sv, p, f"l{i}", after=token)
        else:
            if mixer in ('fox', 'sb'):
                dh, big, small = _qkv_attn_bwd(mixer, dh, sv, p, tag, n_heads, after=token)
            elif mixer == 'mla':
                dh, big, small = _mla_bwd(dh, sv, p, tag, n_heads, after=token)
            else:
                dh, big, small = _sgu_bwd(dh, sv, p, tag, after=token)
            big = {f"{mixer}_{k}": v for k, v in big.items()}
        if state is not None:
            keep(_reduce_end(state, place, dh, name=f"reduce_l{flying[0]}_{flying[1]}"), flying[0])
        state, token = _reduce_begin(big, place, name=f"reduce_l{i}_{part}")
        flying = (i, part)
        for k, v in small.items():
            if k in small_grads:
                small_grads[k][i] = v
            else:
                small_grads[f"{mixer}_{k}"] = v
    last_state = state
    grad_x = dh.reshape(x.shape)
    for k in ('mix_norm', 'ffn_norm', 'ffn_conv_b'):
        small_grads[k] = jnp.concatenate(small_grads[k], axis=0)
    small_grads['ffn_conv_w'] = jnp.stack(small_grads['ffn_conv_w'])

    small_names = [k for k in WEIGHT_NAMES if k not in BIG]
    full_shapes = {k: (W[k].shape[:-1] + (W[k].shape[-1] * N_CHIPS,) if k in SMALL_SHARDED else W[k].shape)
                   for k in small_names}
    packed = _pack([small_grads[k].reshape(full_shapes[k]) for k in small_names])
    summed = _sum_devices(_broadcast_all(packed, name="small_bcast"), name="small_sum")
    small_full = dict(zip(small_names, _unpack(summed, [full_shapes[k] for k in small_names])))
    keep(_reduce_end(last_state, place, summed, name="reduce_l0_mixer"), 0)
    for k in ('ffn_w_up', 'ffn_w_down'):
        big_grads[k] = jnp.stack(big_grads[k])
    grads = dict(big_grads)
    for k in small_names:
        g = small_full[k]
        if k in SMALL_SHARDED:
            n = W[k].shape[-1]
            g = lax.dynamic_slice_in_dim(g, chip * n, n, axis=g.ndim - 1)
        grads[k] = g
    grads = {k: grads[k].reshape(W[k].shape) for k in WEIGHT_NAMES}

    delta, new_m, new_v = {}, {}, {}
    for k in WEIGHT_NAMES:
        delta[k], new_m[k], new_v[k] = _adamw(W[k], grads[k], M[k], V[k], name=f"adamw_{k}")
    return (loss, grad_x, *[grads[k] for k in WEIGHT_NAMES], *[delta[k] for k in WEIGHT_NAMES],
            *[new_m[k] for k in WEIGHT_NAMES], *[new_v[k] for k in WEIGHT_NAMES])
```

```python
import functools

import jax
import jax.numpy as jnp
from jax import lax
from jax.experimental import pallas as pl
from jax.experimental.pallas import tpu as pltpu

F32 = jnp.float32
BF16 = jnp.bfloat16
LANES = 128
HEAD_DIM = 128
NORM_EPS = 1e-6
MLA_Q_RANK = 512
MLA_KV_RANK = 512
MLA_NOPE = 128
MLA_ROPE = 64
ROPE_THETA = 10000.0
SGU_CHUNK = 128
N_CHIPS = 4
ADAM_LR, ADAM_B1, ADAM_B2, ADAM_EPS, ADAM_WD, ADAM_STEP = 0.001, 0.9, 0.999, 1e-08, 0.01, 10
VMEM_LIMIT_BYTES = 56 * 1024 * 1024
MM_VMEM_BUDGET_BYTES = 36 * 1024 * 1024
SHARD_BLOCK_BYTES = 4 * 1024 * 1024
MESH = pl.DeviceIdType.MESH
NEG_BIG = -1e30


def _params(*sem):
    return pltpu.CompilerParams(dimension_semantics=sem, vmem_limit_bytes=VMEM_LIMIT_BYTES)


def _div_block(n, target, mult=LANES):
    if n <= target:
        return n
    best = None
    for b in range(mult, target + 1, mult):
        if n % b == 0:
            best = b
    assert best is not None, (n, target, mult)
    return best


def _iota(shape, dim):
    return lax.broadcasted_iota(jnp.int32, shape, dim)


def _dot(a, b, ca, cb):
    return lax.dot_general(a, b, (((ca,), (cb,)), ((), ())), preferred_element_type=F32)


def _mm(a, b, *, name, ta=False, tb=False, a_sh=False, b_sh=None, o_sh=False, res=None, after=None, out_dtype=F32,
        bm=1024, bn=1024, bk=512):
    if a_sh:
        assert not ta
        m, k = a.shape[1], a.shape[0] * a.shape[2]
    else:
        m, k = (a.shape[1], a.shape[0]) if ta else a.shape
    if b_sh == 'n':
        n = b.shape[2] * b.shape[0]
        assert b.shape[1] == k and not tb
    elif b_sh == 'k':
        n = b.shape[1]
        assert b.shape[2] * N_CHIPS == k
    else:
        n = b.shape[0] if tb else b.shape[1]
        assert (b.shape[1] if tb else b.shape[0]) == k
    n_sh = n // N_CHIPS
    k_sh = k // N_CHIPS
    bm = _div_block(m, bm, 8 if not ta else LANES)
    bn_limit = n
    if b_sh == 'n':
        bn_limit = b.shape[2]
    if o_sh:
        bn_limit = min(bn_limit, n_sh)
    bn = _div_block(bn_limit, bn)
    assert (not o_sh or n_sh % bn == 0) and (b_sh != 'n' or b.shape[2] % bn == 0)
    bk_limit = k_sh if b_sh == 'k' else k
    if a_sh:
        bk_limit = min(bk_limit, a.shape[2])

    def footprint(kb):
        io = bm * kb * a.dtype.itemsize + kb * bn * b.dtype.itemsize + bm * bn * jnp.dtype(out_dtype).itemsize
        if res is not None:
            io += bm * bn * res.dtype.itemsize
        return 2 * io + (bm * bn * 4 if kb < k else 0)

    bk = max([kb for kb in range(LANES, bk_limit + 1, LANES)
              if bk_limit % kb == 0 and (footprint(kb) <= MM_VMEM_BUDGET_BYTES or kb <= bk)])
    assert (not a_sh or a.shape[2] % bk == 0) and (b_sh != 'k' or k_sh % bk == 0) and k % bk == 0
    nbo = n_sh // bn if o_sh else 1
    nbb = b.shape[2] // bn if b_sh == 'n' else 1
    nks = k_sh // bk if b_sh == 'k' else 1
    nka = a.shape[2] // bk if a_sh else 1
    nk = k // bk

    if a_sh:
        a_spec = pl.BlockSpec((None, bm, bk), lambda i, j, q: (q // nka, i, q % nka))
    elif ta:
        a_spec = pl.BlockSpec((bk, bm), lambda i, j, q: (q, i))
    else:
        a_spec = pl.BlockSpec((bm, bk), lambda i, j, q: (i, q))
    if b_sh == 'n':
        b_spec = pl.BlockSpec((None, bk, bn), lambda i, j, q: (j // nbb, q, j % nbb))
    elif b_sh == 'k':
        b_spec = pl.BlockSpec((None, bn, bk), lambda i, j, q: (q // nks, j, q % nks))
    elif tb:
        b_spec = pl.BlockSpec((bn, bk), lambda i, j, q: (j, q))
    else:
        b_spec = pl.BlockSpec((bk, bn), lambda i, j, q: (q, j))
    if o_sh:
        o_spec = pl.BlockSpec((None, bm, bn), lambda i, j, q: (j // nbo, i, j % nbo))
        o_shape = jax.ShapeDtypeStruct((N_CHIPS, m, n_sh), out_dtype)
    else:
        o_spec = pl.BlockSpec((bm, bn), lambda i, j, q: (i, j))
        o_shape = jax.ShapeDtypeStruct((m, n), out_dtype)
    tb_eff = tb or b_sh == 'k'

    def body(a_ref, b_ref, *rest):
        rest = list(rest)
        if after is not None:
            rest.pop(0)
        r_ref = rest.pop(0) if res is not None else None
        o_ref = rest.pop(0)
        part = _dot(a_ref[...].astype(BF16), b_ref[...].astype(BF16), 0 if ta else 1, 1 if tb_eff else 0)

        def finish(r):
            if res is not None:
                r = r + r_ref[...].astype(F32)
            o_ref[...] = r.astype(out_dtype)

        if nk == 1:
            finish(part)
            return
        acc, = rest
        q = pl.program_id(2)

        @pl.when(q == 0)
        def _():
            acc[...] = part

        @pl.when(q > 0)
        def _():
            acc[...] += part

        @pl.when(q == nk - 1)
        def _():
            finish(acc[...])

    ins = [a, b]
    in_specs = [a_spec, b_spec]
    if after is not None:
        ins.append(after)
        in_specs.append(pl.BlockSpec((8, LANES), lambda i, j, q: (0, 0)))
    if res is not None:
        assert not o_sh
        ins.append(res)
        in_specs.append(pl.BlockSpec((bm, bn), lambda i, j, q: (i, j)))
    return pl.pallas_call(
        body, name=name, grid=(m // bm, n // bn, nk), in_specs=in_specs, out_specs=o_spec, out_shape=o_shape,
        scratch_shapes=[pltpu.VMEM((bm, bn), F32)] if nk > 1 else [],
        compiler_params=_params("parallel", "parallel", "arbitrary"))(*ins)


def _rms_fwd(x, g, *, name, out_dtype=BF16, br=256):
    r, c = x.shape
    br = _div_block(r, br, 8)

    def body(x_ref, g_ref, o_ref):
        xv = x_ref[...].astype(F32)
        inv = lax.rsqrt(jnp.mean(xv * xv, axis=-1, keepdims=True) + NORM_EPS)
        o_ref[...] = (xv * inv * g_ref[...]).astype(out_dtype)

    return pl.pallas_call(
        body, name=name, grid=(r // br,),
        in_specs=[pl.BlockSpec((br, c), lambda i: (i, 0)), pl.BlockSpec((1, c), lambda i: (0, 0))],
        out_specs=pl.BlockSpec((br, c), lambda i: (i, 0)), out_shape=jax.ShapeDtypeStruct((r, c), out_dtype),
        compiler_params=_params("parallel"))(x, g)


def _rms_bwd_math(xv, gv, dyv, n):
    inv = lax.rsqrt(jnp.sum(xv * xv, axis=-1, keepdims=True) / n + NORM_EPS)
    xh = xv * inv
    dyg = dyv * gv
    dx = inv * (dyg - xh * (jnp.sum(dyg * xh, axis=-1, keepdims=True) / n))
    return dx, dyv * xh


def _rms_bwd(x, g, dy, *, name, res=None, br=256):
    r, c = x.shape
    br = _div_block(r, br, 8)

    def body(x_ref, g_ref, dy_ref, *rest):
        if res is not None:
            r_ref, dx_ref, dg_ref = rest
        else:
            dx_ref, dg_ref = rest
        dx, dgr = _rms_bwd_math(x_ref[...].astype(F32), g_ref[...], dy_ref[...].astype(F32), c)
        if res is not None:
            dx = dx + r_ref[...]
        dx_ref[...] = dx

        @pl.when(pl.program_id(0) == 0)
        def _():
            dg_ref[...] = jnp.zeros_like(dg_ref)

        dg_ref[...] += jnp.sum(dgr, axis=0, keepdims=True)

    row = pl.BlockSpec((br, c), lambda i: (i, 0))
    vec = pl.BlockSpec((1, c), lambda i: (0, 0))
    ins = [x, g, dy] + ([res] if res is not None else [])
    return pl.pallas_call(
        body, name=name, grid=(r // br,), in_specs=[row, vec, row] + ([row] if res is not None else []),
        out_specs=[row, vec], out_shape=[jax.ShapeDtypeStruct((r, c), F32), jax.ShapeDtypeStruct((1, c), F32)],
        compiler_params=_params("arbitrary"))(*ins)


def _loss(y, target, *, name="loss", br=256):
    r, c = y.shape
    br = _div_block(r, br, 8)

    def body(y_ref, t_ref, l_ref, dy_ref):
        d = y_ref[...] - t_ref[...]
        dy_ref[...] = d * (1.0 / c)

        @pl.when(pl.program_id(0) == 0)
        def _():
            l_ref[...] = jnp.zeros_like(l_ref)

        part = jnp.sum(d * d, axis=0, keepdims=True)
        l_ref[...] += (0.5 / c) * jnp.sum(part, axis=1, keepdims=True) * jnp.ones((1, LANES), F32)

    row = pl.BlockSpec((br, c), lambda i: (i, 0))
    return pl.pallas_call(
        body, name=name, grid=(r // br,), in_specs=[row, row],
        out_specs=[pl.BlockSpec((1, LANES), lambda i: (0, 0)), row],
        out_shape=[jax.ShapeDtypeStruct((1, LANES), F32), jax.ShapeDtypeStruct((r, c), F32)],
        compiler_params=_params("arbitrary"))(y, target)


def _split2(x):
    hi = x.astype(BF16)
    lo = (x - hi.astype(F32)).astype(BF16)
    return hi, lo


def _lane_scan(x, *, suffix):
    rows, n = x.shape
    nb = n // LANES
    a, b = _iota((LANES, LANES), 0), _iota((LANES, LANES), 1)
    tri = ((a > b) if suffix else (a < b)).astype(BF16)
    outs = [None] * nb
    run = jnp.zeros((rows, 1), F32)
    order = range(nb - 1, -1, -1) if suffix else range(nb)
    for blk in order:
        xb = x[:, blk * LANES:(blk + 1) * LANES]
        hi, lo = _split2(xb)
        outs[blk] = _dot(hi, tri, 1, 0) + _dot(lo, tri, 1, 0) + run
        run = run + jnp.sum(xb, axis=-1, keepdims=True)
    return jnp.concatenate(outs, axis=1)


def _softplus(z):
    return jnp.maximum(z, 0.0) + jnp.log(1.0 + jnp.exp(-jnp.abs(z)))


def _head_norm(x, g):
    xv = x.astype(F32)
    inv = lax.rsqrt(jnp.mean(xv * xv, axis=-1, keepdims=True) + NORM_EPS)
    return xv * inv * g


def _attn_weights(kind, qn, kn, scale, qi, bq, bias):
    s = _dot(qn, kn, 1, 1) * scale
    row = qi * bq + _iota(s.shape, 0)
    col = _iota(s.shape, 1)
    if kind == 'sb':
        strict = col < row
        sp = _softplus(s)
        after = _lane_scan(jnp.where(strict, -sp, 0.0), suffix=True)
        w = jnp.where(strict, jnp.exp(s - sp + after), 0.0)
        return w, (strict, s - sp)
    if bias is not None:
        s = s + bias
    s = jnp.where(col <= row, s, NEG_BIG)
    mx = jnp.max(s, axis=-1, keepdims=True)
    e = jnp.exp(s - mx)
    return e, jnp.sum(e, axis=-1, keepdims=True)


def _attn_fwd(kind, q, k, v, *, name, n_heads, dqk, qcol, kcol, vcol, scale, gains=None, cq=None, ck=None, bq=256):
    s_len = q.shape[0]
    bq = _div_block(s_len, bq, 8)
    norm, fox = gains is not None, cq is not None

    def body(*refs):
        refs = list(refs)
        q_ref, k_ref, v_ref = refs[:3]
        rest = refs[3:]
        g_ref = rest.pop(0) if norm else None
        cq_ref, ck_ref = (rest.pop(0), rest.pop(0)) if fox else (None, None)
        o_ref, = rest
        qi = pl.program_id(1)

        def step(n_keys):
            if norm:
                qn = _head_norm(q_ref[...], g_ref[0]).astype(BF16)
                kn = _head_norm(k_ref[0:n_keys, :], g_ref[1]).astype(BF16)
            else:
                qn, kn = q_ref[...].astype(BF16), k_ref[0:n_keys, :].astype(BF16)
            bias = (cq_ref[...] - ck_ref[:, 0:n_keys]) if fox else None
            w, aux = _attn_weights(kind, qn, kn, scale, qi, bq, bias)
            o = _dot(w.astype(BF16), v_ref[0:n_keys, :].astype(BF16), 1, 0)
            if kind != 'sb':
                o = o / aux
            o_ref[...] = o.astype(BF16)

        for qv in range(s_len // bq):
            pl.when(qi == qv)(functools.partial(step, (qv + 1) * bq))

    in_specs = [pl.BlockSpec((bq, dqk), lambda h, i: (i, qcol(h))),
                pl.BlockSpec((s_len, dqk), lambda h, i: (0, kcol(h))),
                pl.BlockSpec((s_len, HEAD_DIM), lambda h, i: (0, vcol(h)))]
    ins = [q, k, v]
    if norm:
        in_specs.append(pl.BlockSpec((2, 1, dqk), lambda h, i: (0, 0, 0)))
        ins.append(gains)
    if fox:
        in_specs += [pl.BlockSpec((None, bq, 1), lambda h, i: (h, i, 0)), pl.BlockSpec((None, 1, s_len), lambda h, i: (h, 0, 0))]
        ins += [cq, ck]
    return pl.pallas_call(
        body, name=name, grid=(n_heads, s_len // bq), in_specs=in_specs,
        out_specs=pl.BlockSpec((bq, HEAD_DIM), lambda h, i: (i, h)),
        out_shape=jax.ShapeDtypeStruct((s_len, n_heads * HEAD_DIM), BF16),
        compiler_params=_params("parallel", "parallel"))(*ins)


def _attn_bwd(kind, q, k, v, o, do, *, name, n_heads, dqk, qcol, kcol, vcol, scale, gains=None, cq=None, ck=None,
              bq=256):
    s_len = q.shape[0]
    bq = _div_block(s_len, bq, 8)
    nq = s_len // bq
    norm, fox = gains is not None, cq is not None

    def body(*refs):
        refs = list(refs)
        q_ref, k_ref, v_ref, o_ref, do_ref = refs[:5]
        rest = refs[5:]
        g_ref = rest.pop(0) if norm else None
        cq_ref, ck_ref = (rest.pop(0), rest.pop(0)) if fox else (None, None)
        dq_ref, dk_ref, dv_ref = rest.pop(0), rest.pop(0), rest.pop(0)
        dg_ref = rest.pop(0) if norm else None
        dcq_ref, dck_ref = (rest.pop(0), rest.pop(0)) if fox else (None, None)
        dk_acc, dv_acc = rest
        h, qi = pl.program_id(0), pl.program_id(1)

        @pl.when(qi == 0)
        def _():
            dk_acc[...] = jnp.zeros_like(dk_acc)
            dv_acc[...] = jnp.zeros_like(dv_acc)
            if fox:
                dck_ref[...] = jnp.zeros_like(dck_ref)

        if norm:
            @pl.when((qi == 0) & (h == 0))
            def _():
                dg_ref[...] = jnp.zeros_like(dg_ref)


        def step(n_keys):
            if norm:
                qn = _head_norm(q_ref[...], g_ref[0]).astype(BF16)
                kn = _head_norm(k_ref[0:n_keys, :], g_ref[1]).astype(BF16)
            else:
                qn, kn = q_ref[...].astype(BF16), k_ref[0:n_keys, :].astype(BF16)
            vb = v_ref[0:n_keys, :].astype(BF16)
            dob = do_ref[...].astype(BF16)
            bias = (cq_ref[...] - ck_ref[:, 0:n_keys]) if fox else None
            w, aux = _attn_weights(kind, qn, kn, scale, qi, bq, bias)
            dw = _dot(dob, vb, 1, 1)
            if kind == 'sb':
                strict, log_sig = aux
                g = dw * w
                cc = _lane_scan(g, suffix=False)
                sig = jnp.exp(log_sig)
                ds = jnp.where(strict, g * (1.0 - sig) - cc * sig, 0.0)
                pw = w
            else:
                pw = w / aux
                delta = jnp.sum(do_ref[...].astype(F32) * o_ref[...].astype(F32), axis=-1, keepdims=True)
                ds = pw * (dw - delta)
                if fox:
                    dcq_ref[...] = jnp.sum(ds, axis=1, keepdims=True)
                    dck_ref[:, 0:n_keys] -= jnp.sum(ds, axis=0, keepdims=True)
            dsb = (ds * scale).astype(BF16)
            dqn = _dot(dsb, kn, 1, 0)
            dk_acc[0:n_keys, :] += _dot(dsb, qn, 0, 0)
            dv_acc[0:n_keys, :] += _dot(pw.astype(BF16), dob, 0, 0)
            if norm:
                dq, dgr = _rms_bwd_math(q_ref[...].astype(F32), g_ref[0], dqn, dqk)
                dg_ref[0] += jnp.sum(dgr, axis=0, keepdims=True)
                dq_ref[...] = dq.astype(BF16)
            else:
                dq_ref[...] = dqn.astype(BF16)

        for qv in range(nq):
            pl.when(qi == qv)(functools.partial(step, (qv + 1) * bq))

        @pl.when(qi == nq - 1)
        def _():
            if norm:
                dk, dgr = _rms_bwd_math(k_ref[...].astype(F32), g_ref[1], dk_acc[...], dqk)
                dg_ref[1] += jnp.sum(dgr, axis=0, keepdims=True)
                dk_ref[...] = dk.astype(BF16)
            else:
                dk_ref[...] = dk_acc[...].astype(BF16)
            dv_ref[...] = dv_acc[...].astype(BF16)

    in_specs = [pl.BlockSpec((bq, dqk), lambda h, i: (i, qcol(h))),
                pl.BlockSpec((s_len, dqk), lambda h, i: (0, kcol(h))),
                pl.BlockSpec((s_len, HEAD_DIM), lambda h, i: (0, vcol(h))),
                pl.BlockSpec((bq, HEAD_DIM), lambda h, i: (i, h)),
                pl.BlockSpec((bq, HEAD_DIM), lambda h, i: (i, h))]
    ins = [q, k, v, o, do]
    out_specs = [pl.BlockSpec((bq, dqk), lambda h, i: (i, h)),
                 pl.BlockSpec((s_len, dqk), lambda h, i: (0, h)),
                 pl.BlockSpec((s_len, HEAD_DIM), lambda h, i: (0, h))]
    out_shape = [jax.ShapeDtypeStruct((s_len, n_heads * dqk), BF16), jax.ShapeDtypeStruct((s_len, n_heads * dqk), BF16),
                 jax.ShapeDtypeStruct((s_len, n_heads * HEAD_DIM), BF16)]
    if norm:
        in_specs.append(pl.BlockSpec((2, 1, dqk), lambda h, i: (0, 0, 0)))
        ins.append(gains)
        out_specs.append(pl.BlockSpec((2, 1, dqk), lambda h, i: (0, 0, 0)))
        out_shape.append(jax.ShapeDtypeStruct((2, 1, dqk), F32))
    if fox:
        in_specs += [pl.BlockSpec((None, bq, 1), lambda h, i: (h, i, 0)), pl.BlockSpec((None, 1, s_len), lambda h, i: (h, 0, 0))]
        ins += [cq, ck]
        out_specs += [pl.BlockSpec((None, bq, 1), lambda h, i: (h, i, 0)), pl.BlockSpec((None, 1, s_len), lambda h, i: (h, 0, 0))]
        out_shape += [jax.ShapeDtypeStruct((n_heads, s_len, 1), F32), jax.ShapeDtypeStruct((n_heads, 1, s_len), F32)]
    return pl.pallas_call(
        body, name=name, grid=(n_heads, nq), in_specs=in_specs, out_specs=out_specs, out_shape=out_shape,
        scratch_shapes=[pltpu.VMEM((s_len, dqk), F32), pltpu.VMEM((s_len, HEAD_DIM), F32)],
        compiler_params=_params("arbitrary", "arbitrary"))(*ins)


def _split3(x):
    hi = x.astype(BF16)
    r1 = x - hi.astype(F32)
    mid = r1.astype(BF16)
    lo = (r1 - mid.astype(F32)).astype(BF16)
    return hi, mid, lo


def _seq_scan(x, *, reverse):
    n = x.shape[0] // LANES
    a, b = _iota((LANES, LANES), 0), _iota((LANES, LANES), 1)
    tri = ((b >= a) if reverse else (b <= a)).astype(BF16)
    outs = [None] * n
    run = jnp.zeros((1, x.shape[1]), F32)
    for blk in (range(n - 1, -1, -1) if reverse else range(n)):
        xb = x[blk * LANES:(blk + 1) * LANES, :]
        hi, mid, lo = _split3(xb)
        outs[blk] = _dot(tri, hi, 1, 0) + _dot(tri, mid, 1, 0) + _dot(tri, lo, 1, 0) + run
        run = run + jnp.sum(xb, axis=0, keepdims=True)
    return jnp.concatenate(outs, axis=0)


def _fgate_fwd(qkvf, b_f, *, fcol, name):
    s_len = qkvf.shape[0]

    def body(f_ref, b_ref, cum_ref):
        z = f_ref[...] + b_ref[...]
        cum_ref[...] = _seq_scan(-_softplus(-z), reverse=False)

    return pl.pallas_call(
        body, name=name, grid=(1,),
        in_specs=[pl.BlockSpec((s_len, LANES), lambda i: (0, fcol)), pl.BlockSpec((1, LANES), lambda i: (0, 0))],
        out_specs=pl.BlockSpec((s_len, LANES), lambda i: (0, 0)), out_shape=jax.ShapeDtypeStruct((s_len, LANES), F32),
        compiler_params=_params("arbitrary"))(qkvf, b_f)


def _fgate_bwd(qkvf, b_f, dcum_a, dcum_b, *, fcol, n_heads, name):
    s_len = qkvf.shape[0]

    def body(f_ref, b_ref, da_ref, db_ref, dz_ref, dbias_ref):
        z = f_ref[...] + b_ref[...]
        dlog = _seq_scan(da_ref[...] + db_ref[...], reverse=True)
        dz = dlog * jnp.exp(-_softplus(z))
        dz = jnp.where(_iota(dz.shape, 1) < n_heads, dz, 0.0)
        dz_ref[...] = dz.astype(BF16)
        dbias_ref[...] = jnp.sum(dz, axis=0, keepdims=True)

    full = pl.BlockSpec((s_len, LANES), lambda i: (0, 0))
    vec = pl.BlockSpec((1, LANES), lambda i: (0, 0))
    return pl.pallas_call(
        body, name=name, grid=(1,),
        in_specs=[pl.BlockSpec((s_len, LANES), lambda i: (0, fcol)), vec, full, full],
        out_specs=[full, vec], out_shape=[jax.ShapeDtypeStruct((s_len, LANES), BF16), jax.ShapeDtypeStruct((1, LANES), F32)],
        compiler_params=_params("arbitrary"))(qkvf, b_f, dcum_a, dcum_b)


def _rope_swap(x):
    half = MLA_ROPE // 2
    lane = _iota(x.shape, 1)
    sw = jnp.where(lane < half, pltpu.roll(x, LANES - half, axis=1), pltpu.roll(x, half, axis=1))
    return jnp.where(lane < MLA_ROPE, sw, 0.0)


def _mla_prep_fwd(qp, kv, c, gq, gk, cos_t, sin_t, *, n_heads, name, bs=512):
    s_len = qp.shape[0]
    bs = _div_block(s_len, bs, 8)
    krope_col = (MLA_Q_RANK + MLA_KV_RANK) // LANES

    def body(qn_ref, qr_ref, kn_ref, kr_ref, gq_ref, gk_ref, cos_ref, sin_ref, qc_ref, kc_ref):
        cos_v, sin_v = cos_ref[...], sin_ref[...]

        def rope(x, g):
            xv = x.astype(F32)
            inv = lax.rsqrt(jnp.sum(xv * xv, axis=-1, keepdims=True) / MLA_ROPE + NORM_EPS)
            y = xv * inv * g
            return y * cos_v + _rope_swap(y) * sin_v

        qc_ref[:, :LANES] = _head_norm(qn_ref[...], gq_ref[0]).astype(BF16)
        qc_ref[:, LANES:] = rope(qr_ref[...], gq_ref[1]).astype(BF16)
        kc_ref[:, :LANES] = _head_norm(kn_ref[...], gk_ref[0]).astype(BF16)
        kc_ref[:, LANES:] = rope(kr_ref[...], gk_ref[1]).astype(BF16)

    blk = lambda f: pl.BlockSpec((bs, LANES), f)
    gspec = pl.BlockSpec((2, 1, LANES), lambda i, h: (0, 0, 0))
    tspec = pl.BlockSpec((bs, LANES), lambda i, h: (i, 0))
    ospec = pl.BlockSpec((bs, 2 * LANES), lambda i, h: (i, h))
    oshape = jax.ShapeDtypeStruct((s_len, n_heads * 2 * LANES), BF16)
    return pl.pallas_call(
        body, name=name, grid=(s_len // bs, n_heads),
        in_specs=[blk(lambda i, h: (i, h)), blk(lambda i, h: (i, n_heads + h)), blk(lambda i, h: (i, 2 * h)),
                  blk(lambda i, h: (i, krope_col)), gspec, gspec, tspec, tspec],
        out_specs=[ospec, ospec], out_shape=[oshape, oshape],
        compiler_params=_params("parallel", "parallel"))(qp, qp, kv, c, gq, gk, cos_t, sin_t)


def _mla_prep_bwd(qp, kv, c, gq, gk, cos_t, sin_t, dqc, dkc, dv, *, n_heads, name, bs=512):
    s_len = qp.shape[0]
    bs = _div_block(s_len, bs, 8)
    krope_col = (MLA_Q_RANK + MLA_KV_RANK) // LANES

    def body(qn_ref, qr_ref, kn_ref, kr_ref, gq_ref, gk_ref, cos_ref, sin_ref, dqc_ref, dkc_ref, dv_ref,
             dqn_ref, dqr_ref, dkv_ref, dkr_ref, dgq_ref, dgk_ref):
        i, h = pl.program_id(0), pl.program_id(1)
        cos_v, sin_v = cos_ref[...], sin_ref[...]

        @pl.when((i == 0) & (h == 0))
        def _():
            dgq_ref[...] = jnp.zeros_like(dgq_ref)
            dgk_ref[...] = jnp.zeros_like(dgk_ref)

        @pl.when(h == 0)
        def _():
            dkr_ref[...] = jnp.zeros_like(dkr_ref)

        def unrope(dy):
            dy = dy.astype(F32)
            return dy * cos_v + _rope_swap(dy * sin_v)

        dqn, dg = _rms_bwd_math(qn_ref[...].astype(F32), gq_ref[0], dqc_ref[:, :LANES].astype(F32), MLA_NOPE)
        dgq_ref[0] += jnp.sum(dg, axis=0, keepdims=True)
        dqn_ref[...] = dqn.astype(BF16)
        dqr, dg = _rms_bwd_math(qr_ref[...].astype(F32), gq_ref[1], unrope(dqc_ref[:, LANES:]), MLA_ROPE)
        dgq_ref[1] += jnp.sum(dg, axis=0, keepdims=True)
        dqr_ref[...] = dqr.astype(BF16)
        dkn, dg = _rms_bwd_math(kn_ref[...].astype(F32), gk_ref[0], dkc_ref[:, :LANES].astype(F32), MLA_NOPE)
        dgk_ref[0] += jnp.sum(dg, axis=0, keepdims=True)
        dkv_ref[:, :LANES] = dkn.astype(BF16)
        dkv_ref[:, LANES:] = dv_ref[...]
        dkr, dg = _rms_bwd_math(kr_ref[...].astype(F32), gk_ref[1], unrope(dkc_ref[:, LANES:]), MLA_ROPE)
        dgk_ref[1] += jnp.sum(dg, axis=0, keepdims=True)
        dkr_ref[...] += dkr

    blk = lambda f: pl.BlockSpec((bs, LANES), f)
    gspec = pl.BlockSpec((2, 1, LANES), lambda i, h: (0, 0, 0))
    tspec = pl.BlockSpec((bs, LANES), lambda i, h: (i, 0))
    cat = pl.BlockSpec((bs, 2 * LANES), lambda i, h: (i, h))
    head = blk(lambda i, h: (i, h))
    hshape = jax.ShapeDtypeStruct((s_len, n_heads * LANES), BF16)
    gshape = jax.ShapeDtypeStruct((2, 1, LANES), F32)
    return pl.pallas_call(
        body, name=name, grid=(s_len // bs, n_heads),
        in_specs=[head, blk(lambda i, h: (i, n_heads + h)), blk(lambda i, h: (i, 2 * h)),
                  blk(lambda i, h: (i, krope_col)), gspec, gspec, tspec, tspec, cat, cat, head],
        out_specs=[head, head, cat, tspec, gspec, gspec],
        out_shape=[hshape, hshape, jax.ShapeDtypeStruct((s_len, n_heads * 2 * LANES), BF16),
                   jax.ShapeDtypeStruct((s_len, LANES), F32), gshape, gshape],
        compiler_params=_params("arbitrary", "arbitrary"))(qp, qp, kv, c, gq, gk, cos_t, sin_t, dqc, dkc, dv)


def _mla_latent_fwd(c, ga, *, name, br=256):
    s_len = c.shape[0]
    br = _div_block(s_len, br, 8)

    def body(c_ref, g_ref, o_ref):
        for part in range(2):
            sl = slice(part * MLA_Q_RANK, (part + 1) * MLA_Q_RANK)
            o_ref[:, sl] = _head_norm(c_ref[:, sl], g_ref[:, sl]).astype(BF16)

    w = MLA_Q_RANK + MLA_KV_RANK
    return pl.pallas_call(
        body, name=name, grid=(s_len // br,),
        in_specs=[pl.BlockSpec((br, w), lambda i: (i, 0)), pl.BlockSpec((1, w), lambda i: (0, 0))],
        out_specs=pl.BlockSpec((br, w), lambda i: (i, 0)), out_shape=jax.ShapeDtypeStruct((s_len, w), BF16),
        compiler_params=_params("parallel"))(c, ga)


def _mla_latent_bwd(c, ga, dcn_q, dcn_kv, dk_rope, *, name, br=256):
    s_len, cw = c.shape
    br = _div_block(s_len, br, 8)
    w = MLA_Q_RANK + MLA_KV_RANK

    def body(c_ref, g_ref, dq_ref, dkv_ref, dkr_ref, dc_ref, dg_ref):
        @pl.when(pl.program_id(0) == 0)
        def _():
            dg_ref[...] = jnp.zeros_like(dg_ref)

        for part, d_ref in enumerate((dq_ref, dkv_ref)):
            sl = slice(part * MLA_Q_RANK, (part + 1) * MLA_Q_RANK)
            dx, dg = _rms_bwd_math(c_ref[:, sl].astype(F32), g_ref[:, sl], d_ref[...].astype(F32), MLA_Q_RANK)
            dc_ref[:, sl] = dx.astype(BF16)
            dg_ref[:, sl] += jnp.sum(dg, axis=0, keepdims=True)
        dc_ref[:, w:] = dkr_ref[...].astype(BF16)

    return pl.pallas_call(
        body, name=name, grid=(s_len // br,),
        in_specs=[pl.BlockSpec((br, w), lambda i: (i, 0)), pl.BlockSpec((1, w), lambda i: (0, 0)),
                  pl.BlockSpec((br, MLA_Q_RANK), lambda i: (i, 0)), pl.BlockSpec((br, MLA_KV_RANK), lambda i: (i, 0)),
                  pl.BlockSpec((br, LANES), lambda i: (i, 0))],
        out_specs=[pl.BlockSpec((br, cw), lambda i: (i, 0)), pl.BlockSpec((1, w), lambda i: (0, 0))],
        out_shape=[jax.ShapeDtypeStruct((s_len, cw), BF16), jax.ShapeDtypeStruct((1, w), F32)],
        compiler_params=_params("arbitrary"))(c, ga, dcn_q, dcn_kv, dk_rope)


_GELU_C = 0.7978845608028654


def _gelu(x):
    return 0.5 * x * (1.0 + jnp.tanh(_GELU_C * (x + 0.044715 * x * x * x)))


def _gelu_grad(x):
    t = jnp.tanh(_GELU_C * (x + 0.044715 * x * x * x))
    return 0.5 * (1.0 + t) + 0.5 * x * (1.0 - t * t) * _GELU_C * (1.0 + 3 * 0.044715 * x * x)


def _sgu_act_fwd(uv, vg, *, name, br=256):
    s_len, w2 = uv.shape
    w = w2 // 2
    br = _div_block(s_len, br, 8)

    def body(uv_ref, g_ref, u_ref, v_ref):
        u_ref[...] = _gelu(uv_ref[:, :w])
        v_ref[...] = _head_norm(_gelu(uv_ref[:, w:]), g_ref[...]).astype(BF16)

    row = lambda c: pl.BlockSpec((br, c), lambda i: (i, 0))
    return pl.pallas_call(
        body, name=name, grid=(s_len // br,), in_specs=[row(w2), pl.BlockSpec((1, w), lambda i: (0, 0))],
        out_specs=[row(w), row(w)], out_shape=[jax.ShapeDtypeStruct((s_len, w), F32), jax.ShapeDtypeStruct((s_len, w), BF16)],
        compiler_params=_params("parallel"))(uv, vg)


def _sgu_act_bwd(uv, vg, du, dvn, *, name, br=256):
    s_len, w2 = uv.shape
    w = w2 // 2
    br = _div_block(s_len, br, 8)

    def body(uv_ref, g_ref, du_ref, dvn_ref, duv_ref, dg_ref):
        @pl.when(pl.program_id(0) == 0)
        def _():
            dg_ref[...] = jnp.zeros_like(dg_ref)

        up, vp = uv_ref[:, :w], uv_ref[:, w:]
        duv_ref[:, :w] = (du_ref[...] * _gelu_grad(up)).astype(BF16)
        dva, dg = _rms_bwd_math(_gelu(vp), g_ref[...], dvn_ref[...], w)
        dg_ref[...] += jnp.sum(dg, axis=0, keepdims=True)
        duv_ref[:, w:] = (dva * _gelu_grad(vp)).astype(BF16)

    row = lambda c: pl.BlockSpec((br, c), lambda i: (i, 0))
    vec = pl.BlockSpec((1, w), lambda i: (0, 0))
    return pl.pallas_call(
        body, name=name, grid=(s_len // br,), in_specs=[row(w2), vec, row(w), row(w)], out_specs=[row(w2), vec],
        out_shape=[jax.ShapeDtypeStruct((s_len, w2), BF16), jax.ShapeDtypeStruct((1, w), F32)],
        compiler_params=_params("arbitrary"))(uv, vg, du, dvn)


def _tril_weights(ws_ref):
    t, s = _iota((SGU_CHUNK, SGU_CHUNK), 0), _iota((SGU_CHUNK, SGU_CHUNK), 1)
    keep = s <= t
    return jnp.where(keep, ws_ref[...], 0.0), keep


def _sgu_mix_fwd(u, vn, w_s, b_s, *, name):
    s_len, w = u.shape
    nc = s_len // SGU_CHUNK

    def body(u_ref, v_ref, ws_ref, b_ref, o_ref):
        wm = _tril_weights(ws_ref)[0].astype(BF16)
        for n in range(nc):
            rows = slice(n * SGU_CHUNK, (n + 1) * SGU_CHUNK)
            mixed = _dot(wm, v_ref[rows, :], 1, 0) + b_ref[...]
            o_ref[rows, :] = (u_ref[rows, :] * mixed).astype(BF16)

    col = pl.BlockSpec((s_len, LANES), lambda g: (0, g))
    return pl.pallas_call(
        body, name=name, grid=(w // LANES,),
        in_specs=[col, col, pl.BlockSpec((None, SGU_CHUNK, SGU_CHUNK), lambda g: (g, 0, 0)),
                  pl.BlockSpec((None, SGU_CHUNK, 1), lambda g: (g, 0, 0))],
        out_specs=col, out_shape=jax.ShapeDtypeStruct((s_len, w), BF16),
        compiler_params=_params("parallel"))(u, vn, w_s, b_s)


def _sgu_mix_bwd(u, vn, w_s, b_s, dgated, *, name):
    s_len, w = u.shape
    nc = s_len // SGU_CHUNK

    def body(u_ref, v_ref, ws_ref, b_ref, dg_ref, du_ref, dv_ref, dws_ref, dbs_ref):
        wf, keep = _tril_weights(ws_ref)
        wm = wf.astype(BF16)
        wmt = wf.T.astype(BF16)
        dws = jnp.zeros((SGU_CHUNK, SGU_CHUNK), F32)
        dbs = jnp.zeros((SGU_CHUNK, 1), F32)
        for n in range(nc):
            rows = slice(n * SGU_CHUNK, (n + 1) * SGU_CHUNK)
            vb = v_ref[rows, :]
            dgv = dg_ref[rows, :].astype(F32)
            mixed = _dot(wm, vb, 1, 0) + b_ref[...]
            du_ref[rows, :] = dgv * mixed
            dm = dgv * u_ref[rows, :]
            dmb = dm.astype(BF16)
            dws = dws + _dot(dmb, vb, 1, 1)
            dbs = dbs + jnp.sum(dm, axis=1, keepdims=True)
            dv_ref[rows, :] = _dot(wmt, dmb, 1, 0)
        dws_ref[...] = jnp.where(keep, dws, 0.0)
        dbs_ref[...] = dbs

    col = pl.BlockSpec((s_len, LANES), lambda g: (0, g))
    wspec = pl.BlockSpec((None, SGU_CHUNK, SGU_CHUNK), lambda g: (g, 0, 0))
    bspec = pl.BlockSpec((None, SGU_CHUNK, 1), lambda g: (g, 0, 0))
    return pl.pallas_call(
        body, name=name, grid=(w // LANES,), in_specs=[col, col, wspec, bspec, col],
        out_specs=[col, col, wspec, bspec],
        out_shape=[jax.ShapeDtypeStruct((s_len, w), F32), jax.ShapeDtypeStruct((s_len, w), F32),
                   jax.ShapeDtypeStruct(w_s.shape, F32), jax.ShapeDtypeStruct(b_s.shape, F32)],
        compiler_params=_params("parallel"))(u, vn, w_s, b_s, dgated)


def _shift_down(x, k):
    if k == 0:
        return x
    return jnp.where(_iota(x.shape, 0) >= k, pltpu.roll(x, k, axis=0), 0.0)


def _shift_up(x, k):
    if k == 0:
        return x
    n = x.shape[0]
    return jnp.where(_iota(x.shape, 0) < n - k, pltpu.roll(x, n - k, axis=0), 0.0)


def _conv(u, w_ref, b_ref):
    return b_ref[...] + w_ref[0:1, :] * _shift_down(u, 2) + w_ref[1:2, :] * _shift_down(u, 1) + w_ref[2:3, :] * u


def _sigmoid(x):
    return 0.5 * jnp.tanh(0.5 * x) + 0.5


def _glu_fwd(up, cw, cb, *, name, bc=256):
    s_len, f2 = up.shape
    f = f2 // 2
    bc = _div_block(f, bc)
    nf = f // bc

    def body(ug_ref, uv_ref, wg_ref, wv_ref, bg_ref, bv_ref, o_ref):
        yg = _conv(ug_ref[...], wg_ref, bg_ref)
        yv = _conv(uv_ref[...], wv_ref, bv_ref)
        o_ref[...] = (yg * _sigmoid(yg) * yv).astype(BF16)

    big = lambda off: pl.BlockSpec((s_len, bc), lambda j: (0, j + off))
    wsp = lambda off: pl.BlockSpec((3, bc), lambda j: (0, j + off))
    bsp = lambda off: pl.BlockSpec((1, bc), lambda j: (0, j + off))
    return pl.pallas_call(
        body, name=name, grid=(nf,), in_specs=[big(0), big(nf), wsp(0), wsp(nf), bsp(0), bsp(nf)],
        out_specs=pl.BlockSpec((s_len, bc), lambda j: (0, j)), out_shape=jax.ShapeDtypeStruct((s_len, f), BF16),
        compiler_params=_params("parallel"))(up, up, cw, cw, cb, cb)


def _glu_bwd(up, cw, cb, dact, *, name, bc=256):
    s_len, f2 = up.shape
    f = f2 // 2
    bc = _div_block(f, bc)
    nf = f // bc

    def body(ug_ref, uv_ref, wg_ref, wv_ref, bg_ref, bv_ref, da_ref, du_ref, dw_ref, db_ref):
        ug, uv = ug_ref[...], uv_ref[...]
        yg = _conv(ug, wg_ref, bg_ref)
        yv = _conv(uv, wv_ref, bv_ref)
        da = da_ref[...].astype(F32)
        sg = _sigmoid(yg)
        planes = ((da * yv * (sg * (1.0 + yg * (1.0 - sg))), ug, wg_ref), (da * (yg * sg), uv, wv_ref))
        for plane, (dy, u, w_ref) in enumerate(planes):
            dy1, dy2 = _shift_up(dy, 1), _shift_up(dy, 2)
            db_ref[plane] = jnp.sum(dy, axis=0, keepdims=True)
            dw_ref[plane, 0:1, :] = jnp.sum(dy2 * u, axis=0, keepdims=True)
            dw_ref[plane, 1:2, :] = jnp.sum(dy1 * u, axis=0, keepdims=True)
            dw_ref[plane, 2:3, :] = jnp.sum(dy * u, axis=0, keepdims=True)
            du_ref[plane] = (w_ref[2:3, :] * dy + w_ref[1:2, :] * dy1 + w_ref[0:1, :] * dy2).astype(BF16)

    big = lambda off: pl.BlockSpec((s_len, bc), lambda j: (0, j + off))
    wsp = lambda off: pl.BlockSpec((3, bc), lambda j: (0, j + off))
    bsp = lambda off: pl.BlockSpec((1, bc), lambda j: (0, j + off))
    planes = lambda r: pl.BlockSpec((2, r, bc), lambda j: (0, 0, j))
    return pl.pallas_call(
        body, name=name, grid=(nf,),
        in_specs=[big(0), big(nf), wsp(0), wsp(nf), bsp(0), bsp(nf), pl.BlockSpec((s_len, bc), lambda j: (0, j))],
        out_specs=[planes(s_len), planes(3), planes(1)],
        out_shape=[jax.ShapeDtypeStruct((2, s_len, f), BF16), jax.ShapeDtypeStruct((2, 3, f), F32),
                   jax.ShapeDtypeStruct((2, 1, f), F32)],
        compiler_params=_params("parallel"))(up, up, cw, cw, cb, cb, dact)


def _as2d(a):
    return a.reshape(-1, a.shape[-1]) if a.ndim >= 2 else a.reshape(1, -1)


def _adamw(w, g, m, v, *, name, target_bytes=1 << 20):
    shape = w.shape
    w2, m2, v2 = _as2d(w), _as2d(m), _as2d(v)
    g2 = g.reshape(w2.shape)
    r, c = w2.shape
    br = r if r * c * 4 <= target_bytes else _div_block(r, max(8, target_bytes // (4 * c) // 8 * 8), 8)
    c1 = 1.0 - ADAM_B1 ** ADAM_STEP
    c2 = 1.0 - ADAM_B2 ** ADAM_STEP

    def body(w_ref, g_ref, m_ref, v_ref, d_ref, nm_ref, nv_ref):
        gv = g_ref[...]
        nm = ADAM_B1 * m_ref[...] + (1.0 - ADAM_B1) * gv
        nv = ADAM_B2 * v_ref[...] + (1.0 - ADAM_B2) * (gv * gv)
        nm_ref[...] = nm
        nv_ref[...] = nv
        d_ref[...] = -ADAM_LR * ((nm / c1) / (jnp.sqrt(nv / c2) + ADAM_EPS) + ADAM_WD * w_ref[...])

    spec = pl.BlockSpec((br, c), lambda i: (i, 0))
    sds = jax.ShapeDtypeStruct((r, c), F32)
    d, nm, nv = pl.pallas_call(
        body, name=name, grid=(r // br,), in_specs=[spec] * 4, out_specs=[spec] * 3, out_shape=[sds] * 3,
        compiler_params=_params("parallel"))(w2, g2, m2, v2)
    return d.reshape(shape), nm.reshape(shape), nv.reshape(shape)


def _add_halves(g, recv, place, *, name, target_bytes=SHARD_BLOCK_BYTES):
    _, _, r, c = g.shape
    br = _div_block(r, max(16, target_bytes // (2 * c) // 16 * 16), 16)

    def body(x_ref, y_ref, c_ref, g_ref, r_ref, o_ref):
        o_ref[...] = (g_ref[...].astype(F32) + r_ref[...].astype(F32)).astype(BF16)

    return pl.pallas_call(
        body, name=name,
        grid_spec=pltpu.PrefetchScalarGridSpec(
            num_scalar_prefetch=3, grid=(N_CHIPS, r // br),
            in_specs=[pl.BlockSpec((None, None, br, c), lambda s, i, xr, yr, cr: (s, cr[0], i, 0)),
                      pl.BlockSpec((None, br, c), lambda s, i, xr, yr, cr: (s, i, 0))],
            out_specs=pl.BlockSpec((None, br, c), lambda s, i, xr, yr, cr: (s, i, 0))),
        out_shape=jax.ShapeDtypeStruct((N_CHIPS, r, c), BF16),
        compiler_params=_params("parallel", "parallel"))(*place, g, recv)


def _sum_chips(p, landed, place, *, name, target_bytes=SHARD_BLOCK_BYTES):
    _, r, c = p.shape
    br = _div_block(r, max(16, target_bytes // (4 * c) // 16 * 16), 16)

    def body(x_ref, y_ref, c_ref, p_ref, l1_ref, l2_ref, l3_ref, o_ref):
        o_ref[...] = ((p_ref[...].astype(F32) + l1_ref[...].astype(F32)) + l2_ref[...].astype(F32)) + l3_ref[...].astype(F32)

    slot = lambda k: pl.BlockSpec((None, br, c), lambda i, xr, yr, cr: ((2 * xr[0] + yr[0] + k) % N_CHIPS, i, 0))
    return pl.pallas_call(
        body, name=name,
        grid_spec=pltpu.PrefetchScalarGridSpec(
            num_scalar_prefetch=3, grid=(r // br,), in_specs=[slot(0), slot(1), slot(2), slot(3)],
            out_specs=pl.BlockSpec((None, br, c), lambda i, xr, yr, cr: (cr[0], i, 0))),
        out_shape=jax.ShapeDtypeStruct((2, r, c), F32),
        compiler_params=_params("parallel"))(*place, p, landed, landed, landed)


def _place_shard(w, place, *, dtype, name, layer=None, target_bytes=SHARD_BLOCK_BYTES):
    r, c = w.shape[-2:]
    hr = r // 2
    mult = 16 if dtype == BF16 else 8
    br = _div_block(hr, max(mult, target_bytes // (4 * c) // mult * mult), mult)
    nb = hr // br

    def body(x_ref, y_ref, c_ref, w_ref, o_ref):
        o_ref[...] = w_ref[...].astype(dtype)

    if layer is None:
        w_spec = pl.BlockSpec((br, c), lambda h, i, xr, yr, cr: (h * nb + i, 0))
    else:
        w_spec = pl.BlockSpec((None, br, c), lambda h, i, xr, yr, cr: (layer, h * nb + i, 0))
    return pl.pallas_call(
        body, name=name,
        grid_spec=pltpu.PrefetchScalarGridSpec(
            num_scalar_prefetch=3, grid=(2, nb), in_specs=[w_spec],
            out_specs=pl.BlockSpec((None, None, br, c), lambda h, i, xr, yr, cr: (2 * xr[0] + yr[0], h, i, 0))),
        out_shape=jax.ShapeDtypeStruct((N_CHIPS, 2, hr, c), dtype),
        compiler_params=_params("parallel", "parallel"))(*place, w)


def _sum_devices(x, *, name):
    n, r, c = x.shape
    br = _div_block(r, 512, 8)

    def body(x_ref, o_ref):
        acc = x_ref[0]
        for s in range(1, n):
            acc = acc + x_ref[s]
        o_ref[...] = acc

    return pl.pallas_call(
        body, name=name, grid=(r // br,), in_specs=[pl.BlockSpec((n, br, c), lambda i: (0, i, 0))],
        out_specs=pl.BlockSpec((br, c), lambda i: (i, 0)), out_shape=jax.ShapeDtypeStruct((r, c), F32),
        compiler_params=_params("parallel"))(x)


_ANY = pl.BlockSpec(memory_space=pl.ANY)


def _place():
    x, y, c = lax.axis_index("x"), lax.axis_index("y"), lax.axis_index("c")
    other_chips = [(1 - x, y), (x, 1 - y), (1 - x, 1 - y)]
    return x, y, c, other_chips


_HBM = pl.BlockSpec(memory_space=pltpu.HBM)
_SEM = pl.BlockSpec(memory_space=pltpu.SEMAPHORE)
_EFFECT = pltpu.SideEffectType.DATAFLOW_SIDE_EFFECTING


def _in_hbm(a):
    return pltpu.with_memory_space_constraint(a, pltpu.HBM)


def _token_spec():
    return pl.BlockSpec(memory_space=pltpu.VMEM), jax.ShapeDtypeStruct((8, LANES), F32)


def _gather_ici_start(bufs, after, *, name):
    n = len(bufs)

    def body(*refs):
        b_refs = refs[:n]
        send_sems, recv_sems = refs[n + 1], refs[n + 2]
        token = refs[-1]
        x, y, c, chips = _place()
        me = 2 * x + y
        for i in range(n):
            for j, (px, py) in enumerate(chips):
                pltpu.make_async_remote_copy(src_ref=b_refs[i].at[me, c], dst_ref=b_refs[i].at[me, c],
                                             send_sem=send_sems.at[3 * i + j], recv_sem=recv_sems.at[3 * i + j],
                                             device_id=(px, py, c), device_id_type=MESH).start()
        token[...] = jnp.zeros_like(token)

    tspec, tshape = _token_spec()
    outs = pl.pallas_call(
        body, name=name, in_specs=[_HBM] * n + [_ANY], out_specs=(_SEM, _SEM, *[_HBM] * n, tspec),
        out_shape=(pltpu.SemaphoreType.DMA((3 * n,)), pltpu.SemaphoreType.DMA((3 * n,)),
                   *[pltpu.HBM(a.shape, a.dtype) for a in bufs], tshape),
        input_output_aliases={i: 2 + i for i in range(n)},
        compiler_params=pltpu.CompilerParams(has_side_effects=_EFFECT),
    )(*[_in_hbm(a) for a in bufs], after)
    return outs[0], outs[1], list(outs[2:2 + n]), outs[-1]


def _gather_ici_wait(send_sems, recv_sems, bufs, after, *, name):
    n = len(bufs)

    def body(*refs):
        b_refs = refs[:n]
        send_sems, recv_sems = refs[n], refs[n + 1]
        x, y, c, chips = _place()
        me = 2 * x + y
        for i in range(n):
            for j, (px, py) in enumerate(chips):
                cp = pltpu.make_async_remote_copy(src_ref=b_refs[i].at[me, c], dst_ref=b_refs[i].at[2 * px + py, c],
                                                  send_sem=send_sems.at[3 * i + j], recv_sem=recv_sems.at[3 * i + j],
                                                  device_id=(px, py, c), device_id_type=MESH)
                cp.wait_send()
                cp.wait_recv()

    outs = pl.pallas_call(
        body, name=name, in_specs=[_HBM] * n + [_SEM, _SEM, _ANY], out_specs=[_HBM] * n,
        out_shape=[pltpu.HBM(a.shape, a.dtype) for a in bufs], input_output_aliases={i: i for i in range(n)},
        compiler_params=pltpu.CompilerParams(has_side_effects=_EFFECT),
    )(*bufs, send_sems, recv_sems, after)
    return list(outs)


def _gather_d2d(bufs, *, name):
    n = len(bufs)

    def body(*refs):
        b_refs = refs[n:2 * n]
        send_sems, recv_sems = refs[2 * n:]
        x, y, c, chips = _place()
        sends = []
        for i in range(n):
            for j, (px, py) in enumerate(chips):
                mine = b_refs[i].at[2 * px + py, c]
                cp = pltpu.make_async_remote_copy(src_ref=mine, dst_ref=mine, send_sem=send_sems.at[3 * i + j],
                                                  recv_sem=recv_sems.at[3 * i + j], device_id=(x, y, 1 - c),
                                                  device_id_type=MESH)
                cp.start()
                sends.append((cp, i, j, px, py))
        for cp, i, j, px, py in sends:
            theirs = b_refs[i].at[2 * px + py, 1 - c]
            pltpu.make_async_remote_copy(src_ref=theirs, dst_ref=theirs, send_sem=send_sems.at[3 * i + j],
                                         recv_sem=recv_sems.at[3 * i + j], device_id=(x, y, 1 - c),
                                         device_id_type=MESH).wait_recv()
            cp.wait_send()

    return pl.pallas_call(
        body, name=name, in_specs=[_ANY] * n, out_specs=[_ANY] * n, input_output_aliases={i: i for i in range(n)},
        out_shape=[jax.ShapeDtypeStruct(a.shape, a.dtype) for a in bufs],
        scratch_shapes=[pltpu.SemaphoreType.DMA((3 * n,)), pltpu.SemaphoreType.DMA((3 * n,))],
    )(*bufs)


def _sibling_halves(gs, *, name):
    n = len(gs)

    def body(*refs):
        g_refs, o_refs = refs[:n], refs[n:2 * n]
        send_sems, recv_sems = refs[2 * n:]
        x, y, c, _ = _place()
        copies = []
        for i in range(n):
            for s in range(N_CHIPS):
                k = i * N_CHIPS + s
                cp = pltpu.make_async_remote_copy(src_ref=g_refs[i].at[s, 1 - c], dst_ref=o_refs[i].at[s],
                                                  send_sem=send_sems.at[k], recv_sem=recv_sems.at[k],
                                                  device_id=(x, y, 1 - c), device_id_type=MESH)
                cp.start()
                copies.append(cp)
        for cp in copies:
            cp.wait()

    return pl.pallas_call(
        body, name=name, in_specs=[_ANY] * n, out_specs=[_ANY] * n,
        out_shape=[jax.ShapeDtypeStruct((N_CHIPS,) + g.shape[2:], g.dtype) for g in gs],
        scratch_shapes=[pltpu.SemaphoreType.DMA((n * N_CHIPS,)), pltpu.SemaphoreType.DMA((n * N_CHIPS,))],
    )(*gs)


def _chip_scatter_start(ps, after, *, name):
    n = len(ps)

    def body(*refs):
        p_refs, l_refs = refs[:n], refs[n:2 * n]
        send_sems, recv_sems = refs[2 * n + 1], refs[2 * n + 2]
        token = refs[-1]
        x, y, c, chips = _place()
        me = 2 * x + y
        for i in range(n):
            for j, (px, py) in enumerate(chips):
                pltpu.make_async_remote_copy(src_ref=p_refs[i].at[2 * px + py], dst_ref=l_refs[i].at[me],
                                             send_sem=send_sems.at[3 * i + j], recv_sem=recv_sems.at[3 * i + j],
                                             device_id=(px, py, c), device_id_type=MESH).start()
        token[...] = jnp.zeros_like(token)

    tspec, tshape = _token_spec()
    lands = [_in_hbm(lax.empty(p.shape, p.dtype)) for p in ps]
    outs = pl.pallas_call(
        body, name=name, in_specs=[_HBM] * (2 * n) + [_ANY], out_specs=(_SEM, _SEM, *[_HBM] * (2 * n), tspec),
        out_shape=(pltpu.SemaphoreType.DMA((3 * n,)), pltpu.SemaphoreType.DMA((3 * n,)),
                   *[pltpu.HBM(p.shape, p.dtype) for p in ps], *[pltpu.HBM(p.shape, p.dtype) for p in ps], tshape),
        input_output_aliases={i: 2 + i for i in range(2 * n)},
        compiler_params=pltpu.CompilerParams(has_side_effects=_EFFECT),
    )(*[_in_hbm(p) for p in ps], *lands, after)
    return outs[0], outs[1], list(outs[2:2 + n]), list(outs[2 + n:2 + 2 * n]), outs[-1]


def _chip_scatter_wait(send_sems, recv_sems, ps, lands, after, *, name):
    n = len(ps)

    def body(*refs):
        p_refs, l_refs = refs[:n], refs[n:2 * n]
        send_sems, recv_sems = refs[2 * n], refs[2 * n + 1]
        x, y, c, chips = _place()
        for i in range(n):
            for j, (px, py) in enumerate(chips):
                cp = pltpu.make_async_remote_copy(src_ref=p_refs[i].at[2 * px + py], dst_ref=l_refs[i].at[2 * px + py],
                                                  send_sem=send_sems.at[3 * i + j], recv_sem=recv_sems.at[3 * i + j],
                                                  device_id=(px, py, c), device_id_type=MESH)
                cp.wait_send()
                cp.wait_recv()

    outs = pl.pallas_call(
        body, name=name, in_specs=[_HBM] * (2 * n) + [_SEM, _SEM, _ANY], out_specs=[_HBM] * (2 * n),
        out_shape=[pltpu.HBM(p.shape, p.dtype) for p in ps] * 2, input_output_aliases={i: i for i in range(2 * n)},
        compiler_params=pltpu.CompilerParams(has_side_effects=_EFFECT),
    )(*ps, *lands, send_sems, recv_sems, after)
    return list(outs[:n]), list(outs[n:])


def _sibling_share(bufs, *, name):
    n = len(bufs)

    def body(*refs):
        b_refs = refs[n:2 * n]
        send_sems, recv_sems = refs[2 * n:]
        x, y, c, _ = _place()
        copies = []
        for i in range(n):
            cp = pltpu.make_async_remote_copy(src_ref=b_refs[i].at[c], dst_ref=b_refs[i].at[c], send_sem=send_sems.at[i],
                                              recv_sem=recv_sems.at[i], device_id=(x, y, 1 - c), device_id_type=MESH)
            cp.start()
            copies.append((cp, i))
        for cp, i in copies:
            theirs = b_refs[i].at[1 - c]
            pltpu.make_async_remote_copy(src_ref=theirs, dst_ref=theirs, send_sem=send_sems.at[i],
                                         recv_sem=recv_sems.at[i], device_id=(x, y, 1 - c),
                                         device_id_type=MESH).wait_recv()
            cp.wait_send()

    return pl.pallas_call(
        body, name=name, in_specs=[_ANY] * n, out_specs=[_ANY] * n, input_output_aliases={i: i for i in range(n)},
        out_shape=[jax.ShapeDtypeStruct(a.shape, a.dtype) for a in bufs],
        scratch_shapes=[pltpu.SemaphoreType.DMA((n,)), pltpu.SemaphoreType.DMA((n,))],
    )(*bufs)


def _broadcast_all(v, *, name):
    def body(v_ref, o_ref, send_sems, recv_sems, local_sem):
        x, y, c, _ = _place()
        me = 4 * x + 2 * y + c
        loc = pltpu.make_async_copy(v_ref, o_ref.at[me], local_sem)
        loc.start()
        copies = []
        for k in range(1, 8):
            dx, dy, dc = (k >> 2) & 1, (k >> 1) & 1, k & 1
            to = (1 - x if dx else x, 1 - y if dy else y, 1 - c if dc else c)
            cp = pltpu.make_async_remote_copy(src_ref=v_ref, dst_ref=o_ref.at[me], send_sem=send_sems.at[k - 1],
                                              recv_sem=recv_sems.at[k - 1], device_id=to, device_id_type=MESH)
            cp.start()
            copies.append((cp, k, to))
        for cp, k, to in copies:
            cp.wait_send()
            theirs = o_ref.at[4 * to[0] + 2 * to[1] + to[2]]
            pltpu.make_async_remote_copy(src_ref=theirs, dst_ref=theirs, send_sem=send_sems.at[k - 1],
                                         recv_sem=recv_sems.at[k - 1], device_id=to, device_id_type=MESH).wait_recv()
        loc.wait()

    return pl.pallas_call(
        body, name=name, in_specs=[_ANY], out_specs=_ANY,
        out_shape=jax.ShapeDtypeStruct((8,) + v.shape, v.dtype),
        scratch_shapes=[pltpu.SemaphoreType.DMA((7,)), pltpu.SemaphoreType.DMA((7,)), pltpu.SemaphoreType.DMA(())],
    )(v)


def _gather_place(shards, place, *, name):
    names = list(shards)
    bufs, shapes = [], []
    for k in names:
        w, layer = shards[k] if isinstance(shards[k], tuple) else (shards[k], None)
        bufs.append(_place_shard(w, place, dtype=F32 if k == 'small' else BF16, layer=layer, name=f"{name}_place_{k}"))
        shapes.append(w.shape[-2:])
    return names, shapes, bufs


def _gather_begin(placed, after, *, name):
    names, shapes, bufs = placed
    send_sems, recv_sems, bufs, token = _gather_ici_start(bufs, after, name=name + "_ici_start")
    return (names, shapes, send_sems, recv_sems, bufs), token


def _gather_end(state, after, *, name):
    names, shapes, send_sems, recv_sems, bufs = state
    bufs = _gather_ici_wait(send_sems, recv_sems, bufs, after, name=name + "_ici_wait")
    bufs = _gather_d2d(bufs, name=name + "_d2d")
    return {k: o.reshape((N_CHIPS,) + sh) for k, o, sh in zip(names, bufs, shapes)}


def _reduce_begin(grads, place, *, name):
    names = list(grads)
    gs = [grads[k].reshape(N_CHIPS, 2, grads[k].shape[1] // 2, grads[k].shape[2]) for k in names]
    recv = _sibling_halves(gs, name=name + "_sib")
    ps = [_add_halves(g, r, place, name=f"{name}_add2_{k}") for g, r, k in zip(gs, recv, names)]
    send_sems, recv_sems, ps, lands, token = _chip_scatter_start(ps, recv[0], name=name + "_scatter_start")
    return (names, [grads[k].shape[1:] for k in names], send_sems, recv_sems, ps, lands), token


def _reduce_end(state, place, after, *, name):
    names, shapes, send_sems, recv_sems, ps, lands = state
    ps, lands = _chip_scatter_wait(send_sems, recv_sems, ps, lands, after, name=name + "_scatter_wait")
    rs = [_sum_chips(p, l, place, name=f"{name}_sum4_{k}") for p, l, k in zip(ps, lands, names)]
    both = _sibling_share(rs, name=name + "_share")
    return {k: b.reshape(sh) for k, b, sh in zip(names, both, shapes)}


def _pad_lanes(a, n=LANES):
    return jnp.pad(a, [(0, 0)] * (a.ndim - 1) + [(0, n - a.shape[-1])])


def _unshard_cols(g):
    return jnp.transpose(g, (1, 0, 2)).reshape(g.shape[1], -1)


def _shard_cols(w):
    k, n = w.shape
    return jnp.transpose(w.reshape(k, N_CHIPS, n // N_CHIPS), (1, 0, 2))


def _ffn_fwd(h, p, tag):
    b = _rms_fwd(h, p['ffn_norm'], name=f"{tag}_ffn_norm")
    up = _mm(b, p['ffn_w_up'], b_sh='n', name=f"{tag}_ffn_up", bn=1408)
    act = _glu_fwd(up, p['ffn_conv_w'], p['ffn_conv_b'], name=f"{tag}_ffn_glu")
    out = _mm(act, p['ffn_w_down'], res=h, name=f"{tag}_ffn_down", bk=704)
    return out, (h, b, up, act)


def _ffn_bwd(dh, saved, p, tag, after=None):
    h, b, up, act = saved
    dact = _mm(dh, p['ffn_w_down'], tb=True, after=after, out_dtype=BF16, name=f"{tag}_ffn_dact", bn=1408)
    dw_down = _mm(act, dh, ta=True, after=after, out_dtype=BF16, name=f"{tag}_ffn_dwdown", bm=1408)
    dup, dcw, dcb = _glu_bwd(up, p['ffn_conv_w'], p['ffn_conv_b'], dact, name=f"{tag}_ffn_dglu")
    dw_up = _mm(b, dup, ta=True, b_sh='n', o_sh=True, out_dtype=BF16, name=f"{tag}_ffn_dwup", bn=1408)
    db = _mm(dup, p['ffn_w_up'], a_sh=True, b_sh='k', name=f"{tag}_ffn_db", bk=1408)
    dcw = jnp.transpose(dcw, (1, 0, 2)).reshape(dcw.shape[1], -1)
    dcb = dcb.reshape(1, -1)
    dh_in, dg = _rms_bwd(h, p['ffn_norm'], db, res=dh, name=f"{tag}_ffn_dnorm")
    big = {'ffn_w_up': dw_up, 'ffn_w_down': dw_down.reshape(N_CHIPS, -1, dw_down.shape[1])}
    small = {'ffn_norm': dg, 'ffn_conv_w': dcw, 'ffn_conv_b': dcb}
    return dh_in, big, small


def _qkv_attn_fwd(kind, h, p, tag, n_heads):
    a = _rms_fwd(h, p['mix_norm'], name=f"{tag}_norm")
    if kind == 'fox':
        qkv = _mm(a, p['w_in'], name=f"{tag}_qkv", bn=896)
        cum = _fgate_fwd(qkv, p['b_f'], fcol=3 * n_heads, name=f"{tag}_fgate")
        cum_t = cum[:, :n_heads].T
        cq, ck = cum_t[:, :, None], cum_t[:, None, :]
    else:
        qkv = _mm(a, p['w_in'], b_sh='n', name=f"{tag}_qkv", bn=768)
        cq = ck = None
    cols = dict(qcol=lambda hh: hh, kcol=lambda hh: n_heads + hh, vcol=lambda hh: 2 * n_heads + hh)
    o = _attn_fwd(kind, qkv, qkv, qkv, name=f"{tag}_attn", n_heads=n_heads, dqk=HEAD_DIM, scale=HEAD_DIM ** -0.5,
                  gains=p['qk_gain'], cq=cq, ck=ck, **cols)
    out = _mm(o, p['w_out'], res=h, name=f"{tag}_out")
    return out, (h, a, qkv, o, cq, ck)


def _qkv_attn_bwd(kind, dh, saved, p, tag, n_heads, after=None):
    h, a, qkv, o, cq, ck = saved
    do = _mm(dh, p['w_out'], tb=True, after=after, out_dtype=BF16, name=f"{tag}_do")
    dw_out = _mm(o, dh, ta=True, after=after, out_dtype=BF16, name=f"{tag}_dwout")
    cols = dict(qcol=lambda hh: hh, kcol=lambda hh: n_heads + hh, vcol=lambda hh: 2 * n_heads + hh)
    outs = _attn_bwd(kind, qkv, qkv, qkv, o, do, name=f"{tag}_dattn", n_heads=n_heads, dqk=HEAD_DIM,
                     scale=HEAD_DIM ** -0.5, gains=p['qk_gain'], cq=cq, ck=ck, **cols)
    dq, dk, dv, dgain = outs[:4]
    small = {'q_gain': dgain[0], 'k_gain': dgain[1]}
    if kind == 'fox':
        dcq, dck = outs[4:]
        dca = _pad_lanes(dcq[:, :, 0].T)
        dcb = _pad_lanes(dck[:, 0, :].T)
        dflog, dbf = _fgate_bwd(qkv, p['b_f'], dca, dcb, fcol=3 * n_heads, n_heads=n_heads, name=f"{tag}_dfgate")
        small['b_f'] = dbf[:, :n_heads]
        dqkv = jnp.concatenate([dq, dk, dv, dflog], axis=1)
        dw_in = _mm(a, dqkv, ta=True, out_dtype=BF16, name=f"{tag}_dwin", bn=896)
        da = _mm(dqkv, p['w_in'], tb=True, name=f"{tag}_da", bk=896)
        dw_in = _shard_cols(dw_in[:, :3 * n_heads * HEAD_DIM + n_heads])
    else:
        dqkv = jnp.concatenate([dq, dk, dv], axis=1)
        dw_in = _mm(a, dqkv, ta=True, o_sh=True, out_dtype=BF16, name=f"{tag}_dwin", bn=768)
        da = _mm(dqkv, p['w_in'], b_sh='k', name=f"{tag}_da", bk=768)
    dh_in, dg = _rms_bwd(h, p['mix_norm'], da, res=dh, name=f"{tag}_dnorm")
    small['mix_norm'] = dg
    big = {'w_in': dw_in, 'w_out': dw_out.reshape(N_CHIPS, -1, dw_out.shape[1])}
    return dh_in, big, small


def _mla_fwd(h, p, tag, n_heads):
    a = _rms_fwd(h, p['mix_norm'], name=f"{tag}_norm")
    c = _mm(a, p['w_in'], name=f"{tag}_latent", bn=1152)
    cn = _mla_latent_fwd(c, p['a_gain'], name=f"{tag}_latent_norm")
    qp = _mm(cn[:, :MLA_Q_RANK], p['w_q_b'], name=f"{tag}_q_up")
    kv = _mm(cn[:, MLA_Q_RANK:], p['w_kv_b'], b_sh='n', name=f"{tag}_kv_up")
    qc, kc = _mla_prep_fwd(qp, kv, c, p['gq'], p['gk'], p['cos'], p['sin'], n_heads=n_heads, name=f"{tag}_prep")
    cols = dict(qcol=lambda hh: hh, kcol=lambda hh: hh, vcol=lambda hh: 2 * hh + 1)
    scale = (MLA_NOPE + MLA_ROPE) ** -0.5
    o = _attn_fwd('mla', qc, kc, kv, name=f"{tag}_attn", n_heads=n_heads, dqk=2 * LANES, scale=scale, **cols)
    out = _mm(o, p['w_out'], res=h, name=f"{tag}_out")
    return out, (h, a, c, cn, qp, kv, qc, kc, o)


def _mla_bwd(dh, saved, p, tag, n_heads, after=None):
    h, a, c, cn, qp, kv, qc, kc, o = saved
    do = _mm(dh, p['w_out'], tb=True, after=after, out_dtype=BF16, name=f"{tag}_do")
    dw_out = _mm(o, dh, ta=True, after=after, out_dtype=BF16, name=f"{tag}_dwout")
    cols = dict(qcol=lambda hh: hh, kcol=lambda hh: hh, vcol=lambda hh: 2 * hh + 1)
    scale = (MLA_NOPE + MLA_ROPE) ** -0.5
    dqc, dkc, dv = _attn_bwd('mla', qc, kc, kv, o, do, name=f"{tag}_dattn", n_heads=n_heads, dqk=2 * LANES,
                             scale=scale, **cols)
    dqn, dqr, dkv, dkr, dgq, dgk = _mla_prep_bwd(qp, kv, c, p['gq'], p['gk'], p['cos'], p['sin'], dqc, dkc, dv,
                                                 n_heads=n_heads, name=f"{tag}_dprep")
    dqp = jnp.concatenate([dqn, dqr], axis=1)
    cn_q, cn_kv = cn[:, :MLA_Q_RANK], cn[:, MLA_Q_RANK:]
    dw_q_b = _mm(cn_q, dqp, ta=True, out_dtype=BF16, name=f"{tag}_dwqb", bm=512)
    dcn_q = _mm(dqp, p['w_q_b'], tb=True, out_dtype=BF16, name=f"{tag}_dcnq")
    dw_kv_b = _mm(cn_kv, dkv, ta=True, o_sh=True, out_dtype=BF16, name=f"{tag}_dwkvb", bm=512)
    dcn_kv = _mm(dkv, p['w_kv_b'], b_sh='k', out_dtype=BF16, name=f"{tag}_dcnkv")
    dc, dga = _mla_latent_bwd(c, p['a_gain'], dcn_q, dcn_kv, dkr, name=f"{tag}_dlatent")
    dw_in = _mm(a, dc, ta=True, out_dtype=BF16, name=f"{tag}_dwin", bn=1152)
    da = _mm(dc, p['w_in'], tb=True, name=f"{tag}_da", bk=1152)
    dh_in, dg = _rms_bwd(h, p['mix_norm'], da, res=dh, name=f"{tag}_dnorm")
    k_rank = dw_q_b.shape[0]
    nope = dw_q_b[:, :n_heads * LANES].reshape(k_rank, n_heads, LANES)
    rope = dw_q_b[:, n_heads * LANES:].reshape(k_rank, n_heads, LANES)[:, :, :MLA_ROPE]
    dw_q_b = jnp.concatenate([nope, rope], axis=2).reshape(k_rank, n_heads * (MLA_NOPE + MLA_ROPE))
    w_in_cols = MLA_Q_RANK + MLA_KV_RANK + MLA_ROPE
    big = {'w_in': dw_in[:, :w_in_cols].reshape(N_CHIPS, -1, w_in_cols), 'w_q_b': _shard_cols(dw_q_b),
           'w_kv_b': dw_kv_b, 'w_out': dw_out.reshape(N_CHIPS, -1, dw_out.shape[1])}
    small = {'mix_norm': dg, 'q_a_gain': dga[:, :MLA_Q_RANK], 'kv_a_gain': dga[:, MLA_Q_RANK:],
             'q_gain': jnp.concatenate([dgq[0], dgq[1][:, :MLA_ROPE]], axis=1),
             'k_gain': jnp.concatenate([dgk[0], dgk[1][:, :MLA_ROPE]], axis=1)}
    return dh_in, big, small


def _sgu_fwd(h, p, tag):
    a = _rms_fwd(h, p['mix_norm'], name=f"{tag}_norm")
    uv = _mm(a, p['w_in'], b_sh='n', name=f"{tag}_in")
    u, vn = _sgu_act_fwd(uv, p['v_gain'], name=f"{tag}_act")
    gated = _sgu_mix_fwd(u, vn, p['w_s'], p['b_s'], name=f"{tag}_mix")
    out = _mm(gated, p['w_out'], res=h, name=f"{tag}_out")
    return out, (h, a, uv, u, vn, gated)


def _sgu_bwd(dh, saved, p, tag, after=None):
    h, a, uv, u, vn, gated = saved
    dgated = _mm(dh, p['w_out'], tb=True, after=after, out_dtype=BF16, name=f"{tag}_dgated")
    dw_out = _mm(gated, dh, ta=True, after=after, out_dtype=BF16, name=f"{tag}_dwout")
    du, dvn, dws, dbs = _sgu_mix_bwd(u, vn, p['w_s'], p['b_s'], dgated, name=f"{tag}_dmix")
    duv, dvg = _sgu_act_bwd(uv, p['v_gain'], du, dvn, name=f"{tag}_dact")
    dw_in = _mm(a, duv, ta=True, o_sh=True, out_dtype=BF16, name=f"{tag}_dwin")
    da = _mm(duv, p['w_in'], b_sh='k', name=f"{tag}_da")
    dh_in, dg = _rms_bwd(h, p['mix_norm'], da, res=dh, name=f"{tag}_dnorm")
    big = {'w_in': dw_in, 'w_out': dw_out.reshape(N_CHIPS, -1, dw_out.shape[1])}
    small = {'mix_norm': dg, 'v_gain': dvg, 'w_s': dws, 'b_s': dbs[:, :, 0]}
    return dh_in, big, small


def _pack(parts):
    flat = jnp.concatenate([p.reshape(-1).astype(F32) for p in parts])
    rows = -(-flat.shape[0] // LANES)
    rows = -(-rows // 32) * 32
    return jnp.pad(flat, (0, rows * LANES - flat.shape[0])).reshape(rows, LANES)


def _unpack(packed, shapes):
    flat = packed.reshape(-1)
    out, off = [], 0
    for s in shapes:
        n = 1
        for d in s:
            n *= d
        out.append(flat[off:off + n].reshape(s))
        off += n
    return out


MIXERS = ('fox', 'mla', 'sb', 'sgu')
WEIGHT_NAMES = ['mix_norm', 'ffn_norm', 'fox_w_in', 'fox_b_f', 'fox_q_gain', 'fox_k_gain', 'fox_w_out', 'mla_w_in',
                'mla_q_a_gain', 'mla_kv_a_gain', 'mla_w_q_b', 'mla_w_kv_b', 'mla_q_gain', 'mla_k_gain', 'mla_w_out',
                'sb_w_in', 'sb_q_gain', 'sb_k_gain', 'sb_w_out', 'sgu_w_in', 'sgu_v_gain', 'sgu_w_s', 'sgu_b_s',
                'sgu_w_out', 'ffn_w_up', 'ffn_conv_w', 'ffn_conv_b', 'ffn_w_down']
SMALL_SHARDED = {'mla_q_a_gain': 1, 'mla_kv_a_gain': 1, 'sgu_v_gain': 1, 'ffn_conv_w': 2}
BIG = ['fox_w_in', 'fox_w_out', 'mla_w_in', 'mla_w_q_b', 'mla_w_kv_b', 'mla_w_out', 'sb_w_in', 'sb_w_out', 'sgu_w_in',
       'sgu_w_out', 'ffn_w_up', 'ffn_w_down']


def kernel(x, positions, mix_norm, ffn_norm, fox_w_in, fox_b_f, fox_q_gain, fox_k_gain, fox_w_out, mla_w_in, mla_q_a_gain, mla_kv_a_gain, mla_w_q_b, mla_w_kv_b, mla_q_gain, mla_k_gain, mla_w_out, sb_w_in, sb_q_gain, sb_k_gain, sb_w_out, sgu_w_in, sgu_v_gain, sgu_w_s, sgu_b_s, sgu_w_out, ffn_w_up, ffn_conv_w, ffn_conv_b, ffn_w_down, loss_target, m_mix_norm, m_ffn_norm, m_fox_w_in, m_fox_b_f, m_fox_q_gain, m_fox_k_gain, m_fox_w_out, m_mla_w_in, m_mla_q_a_gain, m_mla_kv_a_gain, m_mla_w_q_b, m_mla_w_kv_b, m_mla_q_gain, m_mla_k_gain, m_mla_w_out, m_sb_w_in, m_sb_q_gain, m_sb_k_gain, m_sb_w_out, m_sgu_w_in, m_sgu_v_gain, m_sgu_w_s, m_sgu_b_s, m_sgu_w_out, m_ffn_w_up, m_ffn_conv_w, m_ffn_conv_b, m_ffn_w_down, v_mix_norm, v_ffn_norm, v_fox_w_in, v_fox_b_f, v_fox_q_gain, v_fox_k_gain, v_fox_w_out, v_mla_w_in, v_mla_q_a_gain, v_mla_kv_a_gain, v_mla_w_q_b, v_mla_w_kv_b, v_mla_q_gain, v_mla_k_gain, v_mla_w_out, v_sb_w_in, v_sb_q_gain, v_sb_k_gain, v_sb_w_out, v_sgu_w_in, v_sgu_v_gain, v_sgu_w_s, v_sgu_b_s, v_sgu_w_out, v_ffn_w_up, v_ffn_conv_w, v_ffn_conv_b, v_ffn_w_down):
    args = dict(locals())
    W = {k: args[k] for k in WEIGHT_NAMES}
    M = {k: args['m_' + k] for k in WEIGHT_NAMES}
    V = {k: args['v_' + k] for k in WEIGHT_NAMES}
    depth = mix_norm.shape[0]
    s_len, d_model = x.shape[1], x.shape[2]
    n_heads = d_model // HEAD_DIM
    assert all(W[k].shape[0] == 1 for k in WEIGHT_NAMES if k.split('_')[0] in MIXERS), "one layer per mixer"
    xi, yi, ci = lax.axis_index("x"), lax.axis_index("y"), lax.axis_index("c")
    chip = 2 * xi + yi
    place = tuple(jnp.reshape(v, (1,)).astype(jnp.int32) for v in (xi, yi, ci))

    small_local = _pack([W[k][0] if k != 'ffn_conv_w' else W[k] for k in SMALL_SHARDED])

    def piece_shards(i, part):
        if part == 'ffn':
            return {'ffn_w_up': (W['ffn_w_up'], i), 'ffn_w_down': (W['ffn_w_down'], i)}
        mixer = MIXERS[i % len(MIXERS)]
        shards = {k: W[k][0] for k in BIG if k.startswith(mixer + '_')}
        if i == 0:
            shards['small'] = small_local
        return shards

    pieces = [(i, part) for i in range(depth) for part in ('mixer', 'ffn')]
    gathered = {}
    pname = lambda pc: f"gather_l{pc[0]}_{pc[1]}"
    states = {}
    placed = {pieces[0]: _gather_place(piece_shards(*pieces[0]), place, name=pname(pieces[0]))}
    states[pieces[0]], token = _gather_begin(placed[pieces[0]], mix_norm, name=pname(pieces[0]))
    for pc in pieces[1:]:
        placed[pc] = _gather_place(piece_shards(*pc), place, name=pname(pc))
    gathered[pieces[0]] = _gather_end(states.pop(pieces[0]), placed[pieces[-1]][2][-1], name=pname(pieces[0]))
    first_done = next(iter(gathered[pieces[0]].values()))
    states[pieces[1]], token = _gather_begin(placed[pieces[1]], first_done, name=pname(pieces[1]))
    tokens = [token]
    small_shapes = [W[k][0].shape if k != 'ffn_conv_w' else W[k].shape for k in SMALL_SHARDED]
    per_chip = [_unpack(gathered[pieces[0]]['small'][s], small_shapes) for s in range(N_CHIPS)]
    full_small = {k: jnp.concatenate([per_chip[s][j] for s in range(N_CHIPS)], axis=-1)
                  for j, k in enumerate(SMALL_SHARDED)}

    pos = positions.reshape(s_len).astype(F32)
    inv_freq = ROPE_THETA ** (-jnp.arange(0, MLA_ROPE, 2, dtype=F32) / MLA_ROPE)
    ang = pos[:, None] * inv_freq
    cos_t = _pad_lanes(jnp.concatenate([jnp.cos(ang), jnp.cos(ang)], axis=1))
    sin_t = _pad_lanes(jnp.concatenate([-jnp.sin(ang), jnp.sin(ang)], axis=1))

    def piece_params(i, part):
        mixer = MIXERS[i % len(MIXERS)]
        g = gathered[(i, part)]
        if part == 'ffn':
            return mixer, {'ffn_norm': ffn_norm[i:i + 1], 'ffn_w_up': g['ffn_w_up'],
                           'ffn_w_down': g['ffn_w_down'].reshape(-1, d_model),
                           'ffn_conv_w': full_small['ffn_conv_w'][i], 'ffn_conv_b': ffn_conv_b[i:i + 1]}
        p = {'mix_norm': mix_norm[i:i + 1]}
        rows = lambda w: w.reshape(-1, w.shape[-1])
        if mixer == 'fox':
            w = _unshard_cols(g['fox_w_in'])
            p['w_in'] = jnp.pad(w, ((0, 0), (0, (3 * n_heads + 1) * HEAD_DIM - w.shape[1])))
            p['b_f'] = _pad_lanes(fox_b_f)
            p['qk_gain'] = jnp.stack([fox_q_gain, fox_k_gain])
            p['w_out'] = rows(g['fox_w_out'])
        elif mixer == 'sb':
            p['w_in'] = g['sb_w_in']
            p['qk_gain'] = jnp.stack([sb_q_gain, sb_k_gain])
            p['w_out'] = rows(g['sb_w_out'])
        elif mixer == 'sgu':
            p['w_in'] = g['sgu_w_in']
            p['v_gain'] = full_small['sgu_v_gain'].reshape(1, -1)
            p['w_s'] = sgu_w_s[0]
            p['b_s'] = sgu_b_s[0][:, :, None]
            p['w_out'] = rows(g['sgu_w_out'])
        else:
            w = rows(g['mla_w_in'])
            p['w_in'] = jnp.pad(w, ((0, 0), (0, MLA_Q_RANK + MLA_KV_RANK + LANES - w.shape[1])))
            p['a_gain'] = jnp.concatenate([full_small['mla_q_a_gain'], full_small['mla_kv_a_gain']]).reshape(1, -1)
            wq = _unshard_cols(g['mla_w_q_b']).reshape(MLA_Q_RANK, n_heads, MLA_NOPE + MLA_ROPE)
            p['w_q_b'] = jnp.concatenate([wq[:, :, :MLA_NOPE].reshape(MLA_Q_RANK, -1),
                                          _pad_lanes(wq[:, :, MLA_NOPE:]).reshape(MLA_Q_RANK, -1)], axis=1)
            p['w_kv_b'] = g['mla_w_kv_b']
            p['gq'] = jnp.stack([mla_q_gain[:, :MLA_NOPE], _pad_lanes(mla_q_gain[:, MLA_NOPE:])])
            p['gk'] = jnp.stack([mla_k_gain[:, :MLA_NOPE], _pad_lanes(mla_k_gain[:, MLA_NOPE:])])
            p['cos'], p['sin'] = cos_t, sin_t
            p['w_out'] = rows(g['mla_w_out'])
        return mixer, p

    h = x.reshape(s_len, d_model)
    saved = []
    for n, (i, part) in enumerate(pieces):
        nxt = pieces[n + 1] if n + 1 < len(pieces) else None
        ahead = pieces[n + 2] if n + 2 < len(pieces) else None
        if ahead is not None:
            after = next(iter(gathered[(i, part)].values()))
            states[ahead], token = _gather_begin(placed[ahead], after, name=pname(ahead))
            tokens.append(token)
        mixer, p = piece_params(i, part)
        gain = 'ffn_norm' if part == 'ffn' else 'mix_norm'
        for token in tokens:
            p[gain] = p[gain] + token[0:1, 0:1]
        tokens = []
        tag = f"l{i}_{mixer}"
        if part == 'ffn':
            h, sv = _ffn_fwd(h, p, f"l{i}")
        elif mixer in ('fox', 'sb'):
            h, sv = _qkv_attn_fwd(mixer, h, p, tag, n_heads)
        elif mixer == 'mla':
            h, sv = _mla_fwd(h, p, tag, n_heads)
        else:
            h, sv = _sgu_fwd(h, p, tag)
        saved.append((mixer, p, sv))
        if nxt is not None:
            gathered[nxt] = _gather_end(states.pop(nxt), h, name=pname(nxt))
    loss_row, dh = _loss(h, loss_target.reshape(s_len, d_model))
    loss = lax.psum(loss_row[0, 0], ("x", "y", "c"))

    big_grads, small_grads = {}, {k: [None] * depth for k in ('mix_norm', 'ffn_norm', 'ffn_conv_w', 'ffn_conv_b')}

    def keep(reduced, i):
        for k, v in reduced.items():
            if k.startswith('ffn_'):
                big_grads.setdefault(k, [None] * depth)[i] = v
            else:
                big_grads[k] = v[None]

    state, token, flying = None, None, None
    for n in reversed(range(len(pieces))):
        i, part = pieces[n]
        mixer, p, sv = saved[n]
        tag = f"l{i}_{mixer}"
        if part == 'ffn':
            dh, big, small = _ffn_bwd(dh, sv, p, f"l{i}", after=token)
        else:
            if mixer in ('fox', 'sb'):
                dh, big, small = _qkv_attn_bwd(mixer, dh, sv, p, tag, n_heads, after=token)
            elif mixer == 'mla':
                dh, big, small = _mla_bwd(dh, sv, p, tag, n_heads, after=token)
            else:
                dh, big, small = _sgu_bwd(dh, sv, p, tag, after=token)
            big = {f"{mixer}_{k}": v for k, v in big.items()}
        if state is not None:
            keep(_reduce_end(state, place, dh, name=f"reduce_l{flying[0]}_{flying[1]}"), flying[0])
        state, token = _reduce_begin(big, place, name=f"reduce_l{i}_{part}")
        flying = (i, part)
        for k, v in small.items():
            if k in small_grads:
                small_grads[k][i] = v
            else:
                small_grads[f"{mixer}_{k}"] = v
    last_state = state
    grad_x = dh.reshape(x.shape)
    for k in ('mix_norm', 'ffn_norm', 'ffn_conv_b'):
        small_grads[k] = jnp.concatenate(small_grads[k], axis=0)
    small_grads['ffn_conv_w'] = jnp.stack(small_grads['ffn_conv_w'])

    small_names = [k for k in WEIGHT_NAMES if k not in BIG]
    full_shapes = {k: (W[k].shape[:-1] + (W[k].shape[-1] * N_CHIPS,) if k in SMALL_SHARDED else W[k].shape)
                   for k in small_names}
    packed = _pack([small_grads[k].reshape(full_shapes[k]) for k in small_names])
    summed = _sum_devices(_broadcast_all(packed, name="small_bcast"), name="small_sum")
    small_full = dict(zip(small_names, _unpack(summed, [full_shapes[k] for k in small_names])))
    for k in ('ffn_w_up', 'ffn_w_down'):
        big_grads[k] = jnp.stack(big_grads[k])
    grads = dict(big_grads)
    for k in small_names:
        g = small_full[k]
        if k in SMALL_SHARDED:
            n = W[k].shape[-1]
            g = lax.dynamic_slice_in_dim(g, chip * n, n, axis=g.ndim - 1)
        grads[k] = g

    delta, new_m, new_v = {}, {}, {}

    def update(k):
        grads[k] = grads[k].reshape(W[k].shape)
        delta[k], new_m[k], new_v[k] = _adamw(W[k], grads[k], M[k], V[k], name=f"adamw_{k}")

    last_names = [k for k in BIG if k.startswith(MIXERS[0] + '_')]
    for k in WEIGHT_NAMES:
        if k not in last_names:
            update(k)
    keep(_reduce_end(last_state, place, delta['ffn_w_up'], name="reduce_l0_mixer"), 0)
    for k in last_names:
        grads[k] = big_grads[k]
        update(k)
    return (loss, grad_x, *[grads[k] for k in WEIGHT_NAMES], *[delta[k] for k in WEIGHT_NAMES],
            *[new_m[k] for k in WEIGHT_NAMES], *[new_v[k] for k in WEIGHT_NAMES])
```

```python
import functools

import jax
import jax.numpy as jnp
from jax import lax
from jax.experimental import pallas as pl
from jax.experimental.pallas import tpu as pltpu

F32 = jnp.float32
BF16 = jnp.bfloat16
LANES = 128
HEAD_DIM = 128
NORM_EPS = 1e-6
MLA_Q_RANK = 512
MLA_KV_RANK = 512
MLA_NOPE = 128
MLA_ROPE = 64
ROPE_THETA = 10000.0
SGU_CHUNK = 128
N_CHIPS = 4
ADAM_LR, ADAM_B1, ADAM_B2, ADAM_EPS, ADAM_WD, ADAM_STEP = 0.001, 0.9, 0.999, 1e-08, 0.01, 10
VMEM_LIMIT_BYTES = 56 * 1024 * 1024
MM_VMEM_BUDGET_BYTES = 36 * 1024 * 1024
SHARD_BLOCK_BYTES = 4 * 1024 * 1024
MESH = pl.DeviceIdType.MESH
NEG_BIG = -1e30


def _params(*sem):
    return pltpu.CompilerParams(dimension_semantics=sem, vmem_limit_bytes=VMEM_LIMIT_BYTES)


def _div_block(n, target, mult=LANES):
    if n <= target:
        return n
    best = None
    for b in range(mult, target + 1, mult):
        if n % b == 0:
            best = b
    assert best is not None, (n, target, mult)
    return best


def _after_spec(after):
    if after is None:
        return []
    return [pl.BlockSpec((8, LANES), lambda *idx: (0, 0))]


def _after_arg(after):
    return [] if after is None else [after]


def _iota(shape, dim):
    return lax.broadcasted_iota(jnp.int32, shape, dim)


def _dot(a, b, ca, cb):
    return lax.dot_general(a, b, (((ca,), (cb,)), ((), ())), preferred_element_type=F32)


def _mm(a, b, *, name, ta=False, tb=False, a_sh=False, b_sh=None, o_sh=False, res=None, after=None, out_dtype=F32,
        bm=1024, bn=1024, bk=512):
    if a_sh:
        assert not ta
        m, k = a.shape[1], a.shape[0] * a.shape[2]
    else:
        m, k = (a.shape[1], a.shape[0]) if ta else a.shape
    if b_sh == 'n':
        n = b.shape[2] * b.shape[0]
        assert b.shape[1] == k and not tb
    elif b_sh == 'k':
        n = b.shape[1]
        assert b.shape[2] * N_CHIPS == k
    else:
        n = b.shape[0] if tb else b.shape[1]
        assert (b.shape[1] if tb else b.shape[0]) == k
    n_sh = n // N_CHIPS
    k_sh = k // N_CHIPS
    bm = _div_block(m, bm, 8 if not ta else LANES)
    bn_limit = n
    if b_sh == 'n':
        bn_limit = b.shape[2]
    if o_sh:
        bn_limit = min(bn_limit, n_sh)
    bn = _div_block(bn_limit, bn)
    assert (not o_sh or n_sh % bn == 0) and (b_sh != 'n' or b.shape[2] % bn == 0)
    bk_limit = k_sh if b_sh == 'k' else k
    if a_sh:
        bk_limit = min(bk_limit, a.shape[2])

    def footprint(kb):
        io = bm * kb * a.dtype.itemsize + kb * bn * b.dtype.itemsize + bm * bn * jnp.dtype(out_dtype).itemsize
        if res is not None:
            io += bm * bn * res.dtype.itemsize
        return 2 * io + (bm * bn * 4 if kb < k else 0)

    bk = max([kb for kb in range(LANES, bk_limit + 1, LANES)
              if bk_limit % kb == 0 and (footprint(kb) <= MM_VMEM_BUDGET_BYTES or kb <= bk)])
    assert (not a_sh or a.shape[2] % bk == 0) and (b_sh != 'k' or k_sh % bk == 0) and k % bk == 0
    nbo = n_sh // bn if o_sh else 1
    nbb = b.shape[2] // bn if b_sh == 'n' else 1
    nks = k_sh // bk if b_sh == 'k' else 1
    nka = a.shape[2] // bk if a_sh else 1
    nk = k // bk

    if a_sh:
        a_spec = pl.BlockSpec((None, bm, bk), lambda i, j, q: (q // nka, i, q % nka))
    elif ta:
        a_spec = pl.BlockSpec((bk, bm), lambda i, j, q: (q, i))
    else:
        a_spec = pl.BlockSpec((bm, bk), lambda i, j, q: (i, q))
    if b_sh == 'n':
        b_spec = pl.BlockSpec((None, bk, bn), lambda i, j, q: (j // nbb, q, j % nbb))
    elif b_sh == 'k':
        b_spec = pl.BlockSpec((None, bn, bk), lambda i, j, q: (q // nks, j, q % nks))
    elif tb:
        b_spec = pl.BlockSpec((bn, bk), lambda i, j, q: (j, q))
    else:
        b_spec = pl.BlockSpec((bk, bn), lambda i, j, q: (q, j))
    if o_sh:
        o_spec = pl.BlockSpec((None, bm, bn), lambda i, j, q: (j // nbo, i, j % nbo))
        o_shape = jax.ShapeDtypeStruct((N_CHIPS, m, n_sh), out_dtype)
    else:
        o_spec = pl.BlockSpec((bm, bn), lambda i, j, q: (i, j))
        o_shape = jax.ShapeDtypeStruct((m, n), out_dtype)
    tb_eff = tb or b_sh == 'k'

    def body(a_ref, b_ref, *rest):
        rest = list(rest)
        if after is not None:
            rest.pop(0)
        r_ref = rest.pop(0) if res is not None else None
        o_ref = rest.pop(0)
        part = _dot(a_ref[...].astype(BF16), b_ref[...].astype(BF16), 0 if ta else 1, 1 if tb_eff else 0)

        def finish(r):
            if res is not None:
                r = r + r_ref[...].astype(F32)
            o_ref[...] = r.astype(out_dtype)

        if nk == 1:
            finish(part)
            return
        acc, = rest
        q = pl.program_id(2)

        @pl.when(q == 0)
        def _():
            acc[...] = part

        @pl.when(q > 0)
        def _():
            acc[...] += part

        @pl.when(q == nk - 1)
        def _():
            finish(acc[...])

    ins = [a, b]
    in_specs = [a_spec, b_spec]
    if after is not None:
        ins.append(after)
        in_specs.append(pl.BlockSpec((8, LANES), lambda i, j, q: (0, 0)))
    if res is not None:
        assert not o_sh
        ins.append(res)
        in_specs.append(pl.BlockSpec((bm, bn), lambda i, j, q: (i, j)))
    return pl.pallas_call(
        body, name=name, grid=(m // bm, n // bn, nk), in_specs=in_specs, out_specs=o_spec, out_shape=o_shape,
        scratch_shapes=[pltpu.VMEM((bm, bn), F32)] if nk > 1 else [],
        compiler_params=_params("parallel", "parallel", "arbitrary"))(*ins)


def _rms_fwd(x, g, *, name, out_dtype=BF16, br=256):
    r, c = x.shape
    br = _div_block(r, br, 8)

    def body(x_ref, g_ref, o_ref):
        xv = x_ref[...].astype(F32)
        inv = lax.rsqrt(jnp.mean(xv * xv, axis=-1, keepdims=True) + NORM_EPS)
        o_ref[...] = (xv * inv * g_ref[...]).astype(out_dtype)

    return pl.pallas_call(
        body, name=name, grid=(r // br,),
        in_specs=[pl.BlockSpec((br, c), lambda i: (i, 0)), pl.BlockSpec((1, c), lambda i: (0, 0))],
        out_specs=pl.BlockSpec((br, c), lambda i: (i, 0)), out_shape=jax.ShapeDtypeStruct((r, c), out_dtype),
        compiler_params=_params("parallel"))(x, g)


def _rms_bwd_math(xv, gv, dyv, n):
    inv = lax.rsqrt(jnp.sum(xv * xv, axis=-1, keepdims=True) / n + NORM_EPS)
    xh = xv * inv
    dyg = dyv * gv
    dx = inv * (dyg - xh * (jnp.sum(dyg * xh, axis=-1, keepdims=True) / n))
    return dx, dyv * xh


def _rms_bwd(x, g, dy, *, name, res=None, br=256):
    r, c = x.shape
    br = _div_block(r, br, 8)

    def body(x_ref, g_ref, dy_ref, *rest):
        if res is not None:
            r_ref, dx_ref, dg_ref = rest
        else:
            dx_ref, dg_ref = rest
        dx, dgr = _rms_bwd_math(x_ref[...].astype(F32), g_ref[...], dy_ref[...].astype(F32), c)
        if res is not None:
            dx = dx + r_ref[...]
        dx_ref[...] = dx

        @pl.when(pl.program_id(0) == 0)
        def _():
            dg_ref[...] = jnp.zeros_like(dg_ref)

        dg_ref[...] += jnp.sum(dgr, axis=0, keepdims=True)

    row = pl.BlockSpec((br, c), lambda i: (i, 0))
    vec = pl.BlockSpec((1, c), lambda i: (0, 0))
    ins = [x, g, dy] + ([res] if res is not None else [])
    return pl.pallas_call(
        body, name=name, grid=(r // br,), in_specs=[row, vec, row] + ([row] if res is not None else []),
        out_specs=[row, vec], out_shape=[jax.ShapeDtypeStruct((r, c), F32), jax.ShapeDtypeStruct((1, c), F32)],
        compiler_params=_params("arbitrary"))(*ins)


def _loss(y, target, *, name="loss", br=256):
    r, c = y.shape
    br = _div_block(r, br, 8)

    def body(y_ref, t_ref, l_ref, dy_ref):
        d = y_ref[...] - t_ref[...]
        dy_ref[...] = d * (1.0 / c)

        @pl.when(pl.program_id(0) == 0)
        def _():
            l_ref[...] = jnp.zeros_like(l_ref)

        part = jnp.sum(d * d, axis=0, keepdims=True)
        l_ref[...] += (0.5 / c) * jnp.sum(part, axis=1, keepdims=True) * jnp.ones((1, LANES), F32)

    row = pl.BlockSpec((br, c), lambda i: (i, 0))
    return pl.pallas_call(
        body, name=name, grid=(r // br,), in_specs=[row, row],
        out_specs=[pl.BlockSpec((1, LANES), lambda i: (0, 0)), row],
        out_shape=[jax.ShapeDtypeStruct((1, LANES), F32), jax.ShapeDtypeStruct((r, c), F32)],
        compiler_params=_params("arbitrary"))(y, target)


def _split2(x):
    hi = x.astype(BF16)
    lo = (x - hi.astype(F32)).astype(BF16)
    return hi, lo


def _lane_scan(x, *, suffix):
    rows, n = x.shape
    nb = n // LANES
    a, b = _iota((LANES, LANES), 0), _iota((LANES, LANES), 1)
    tri = ((a > b) if suffix else (a < b)).astype(BF16)
    outs = [None] * nb
    run = jnp.zeros((rows, 1), F32)
    order = range(nb - 1, -1, -1) if suffix else range(nb)
    for blk in order:
        xb = x[:, blk * LANES:(blk + 1) * LANES]
        hi, lo = _split2(xb)
        outs[blk] = _dot(hi, tri, 1, 0) + _dot(lo, tri, 1, 0) + run
        run = run + jnp.sum(xb, axis=-1, keepdims=True)
    return jnp.concatenate(outs, axis=1)


def _softplus(z):
    return jnp.maximum(z, 0.0) + jnp.log(1.0 + jnp.exp(-jnp.abs(z)))


def _head_norm(x, g):
    xv = x.astype(F32)
    inv = lax.rsqrt(jnp.mean(xv * xv, axis=-1, keepdims=True) + NORM_EPS)
    return xv * inv * g


def _attn_weights(kind, qn, kn, scale, qi, bq, bias):
    s = _dot(qn, kn, 1, 1) * scale
    row = qi * bq + _iota(s.shape, 0)
    col = _iota(s.shape, 1)
    if kind == 'sb':
        strict = col < row
        sp = _softplus(s)
        after = _lane_scan(jnp.where(strict, -sp, 0.0), suffix=True)
        w = jnp.where(strict, jnp.exp(s - sp + after), 0.0)
        return w, (strict, s - sp)
    if bias is not None:
        s = s + bias
    s = jnp.where(col <= row, s, NEG_BIG)
    mx = jnp.max(s, axis=-1, keepdims=True)
    e = jnp.exp(s - mx)
    return e, jnp.sum(e, axis=-1, keepdims=True)


def _attn_fwd(kind, q, k, v, *, name, n_heads, dqk, qcol, kcol, vcol, scale, gains=None, cq=None, ck=None, bq=256):
    s_len = q.shape[0]
    bq = _div_block(s_len, bq, 8)
    norm, fox = gains is not None, cq is not None

    def body(*refs):
        refs = list(refs)
        q_ref, k_ref, v_ref = refs[:3]
        rest = refs[3:]
        g_ref = rest.pop(0) if norm else None
        cq_ref, ck_ref = (rest.pop(0), rest.pop(0)) if fox else (None, None)
        o_ref, = rest
        qi = pl.program_id(1)

        def step(n_keys):
            if norm:
                qn = _head_norm(q_ref[...], g_ref[0]).astype(BF16)
                kn = _head_norm(k_ref[0:n_keys, :], g_ref[1]).astype(BF16)
            else:
                qn, kn = q_ref[...].astype(BF16), k_ref[0:n_keys, :].astype(BF16)
            bias = (cq_ref[...] - ck_ref[:, 0:n_keys]) if fox else None
            w, aux = _attn_weights(kind, qn, kn, scale, qi, bq, bias)
            o = _dot(w.astype(BF16), v_ref[0:n_keys, :].astype(BF16), 1, 0)
            if kind != 'sb':
                o = o / aux
            o_ref[...] = o.astype(BF16)

        for qv in range(s_len // bq):
            pl.when(qi == qv)(functools.partial(step, (qv + 1) * bq))

    in_specs = [pl.BlockSpec((bq, dqk), lambda h, i: (i, qcol(h))),
                pl.BlockSpec((s_len, dqk), lambda h, i: (0, kcol(h))),
                pl.BlockSpec((s_len, HEAD_DIM), lambda h, i: (0, vcol(h)))]
    ins = [q, k, v]
    if norm:
        in_specs.append(pl.BlockSpec((2, 1, dqk), lambda h, i: (0, 0, 0)))
        ins.append(gains)
    if fox:
        in_specs += [pl.BlockSpec((None, bq, 1), lambda h, i: (h, i, 0)), pl.BlockSpec((None, 1, s_len), lambda h, i: (h, 0, 0))]
        ins += [cq, ck]
    return pl.pallas_call(
        body, name=name, grid=(n_heads, s_len // bq), in_specs=in_specs,
        out_specs=pl.BlockSpec((bq, HEAD_DIM), lambda h, i: (i, h)),
        out_shape=jax.ShapeDtypeStruct((s_len, n_heads * HEAD_DIM), BF16),
        compiler_params=_params("parallel", "parallel"))(*ins)


def _attn_bwd(kind, q, k, v, o, do, *, name, n_heads, dqk, qcol, kcol, vcol, scale, gains=None, cq=None, ck=None,
              bq=256):
    s_len = q.shape[0]
    bq = _div_block(s_len, bq, 8)
    nq = s_len // bq
    norm, fox = gains is not None, cq is not None

    def body(*refs):
        refs = list(refs)
        q_ref, k_ref, v_ref, o_ref, do_ref = refs[:5]
        rest = refs[5:]
        g_ref = rest.pop(0) if norm else None
        cq_ref, ck_ref = (rest.pop(0), rest.pop(0)) if fox else (None, None)
        dq_ref, dk_ref, dv_ref = rest.pop(0), rest.pop(0), rest.pop(0)
        dg_ref = rest.pop(0) if norm else None
        dcq_ref, dck_ref = (rest.pop(0), rest.pop(0)) if fox else (None, None)
        dk_acc, dv_acc = rest
        h, qi = pl.program_id(0), pl.program_id(1)

        @pl.when(qi == 0)
        def _():
            dk_acc[...] = jnp.zeros_like(dk_acc)
            dv_acc[...] = jnp.zeros_like(dv_acc)
            if fox:
                dck_ref[...] = jnp.zeros_like(dck_ref)

        if norm:
            @pl.when((qi == 0) & (h == 0))
            def _():
                dg_ref[...] = jnp.zeros_like(dg_ref)


        def step(n_keys):
            if norm:
                qn = _head_norm(q_ref[...], g_ref[0]).astype(BF16)
                kn = _head_norm(k_ref[0:n_keys, :], g_ref[1]).astype(BF16)
            else:
                qn, kn = q_ref[...].astype(BF16), k_ref[0:n_keys, :].astype(BF16)
            vb = v_ref[0:n_keys, :].astype(BF16)
            dob = do_ref[...].astype(BF16)
            bias = (cq_ref[...] - ck_ref[:, 0:n_keys]) if fox else None
            w, aux = _attn_weights(kind, qn, kn, scale, qi, bq, bias)
            dw = _dot(dob, vb, 1, 1)
            if kind == 'sb':
                strict, log_sig = aux
                g = dw * w
                cc = _lane_scan(g, suffix=False)
                sig = jnp.exp(log_sig)
                ds = jnp.where(strict, g * (1.0 - sig) - cc * sig, 0.0)
                pw = w
            else:
                pw = w / aux
                delta = jnp.sum(do_ref[...].astype(F32) * o_ref[...].astype(F32), axis=-1, keepdims=True)
                ds = pw * (dw - delta)
                if fox:
                    dcq_ref[...] = jnp.sum(ds, axis=1, keepdims=True)
                    dck_ref[:, 0:n_keys] -= jnp.sum(ds, axis=0, keepdims=True)
            dsb = (ds * scale).astype(BF16)
            dqn = _dot(dsb, kn, 1, 0)
            dk_acc[0:n_keys, :] += _dot(dsb, qn, 0, 0)
            dv_acc[0:n_keys, :] += _dot(pw.astype(BF16), dob, 0, 0)
            if norm:
                dq, dgr = _rms_bwd_math(q_ref[...].astype(F32), g_ref[0], dqn, dqk)
                dg_ref[0] += jnp.sum(dgr, axis=0, keepdims=True)
                dq_ref[...] = dq.astype(BF16)
            else:
                dq_ref[...] = dqn.astype(BF16)

        for qv in range(nq):
            pl.when(qi == qv)(functools.partial(step, (qv + 1) * bq))

        @pl.when(qi == nq - 1)
        def _():
            if norm:
                dk, dgr = _rms_bwd_math(k_ref[...].astype(F32), g_ref[1], dk_acc[...], dqk)
                dg_ref[1] += jnp.sum(dgr, axis=0, keepdims=True)
                dk_ref[...] = dk.astype(BF16)
            else:
                dk_ref[...] = dk_acc[...].astype(BF16)
            dv_ref[...] = dv_acc[...].astype(BF16)

    in_specs = [pl.BlockSpec((bq, dqk), lambda h, i: (i, qcol(h))),
                pl.BlockSpec((s_len, dqk), lambda h, i: (0, kcol(h))),
                pl.BlockSpec((s_len, HEAD_DIM), lambda h, i: (0, vcol(h))),
                pl.BlockSpec((bq, HEAD_DIM), lambda h, i: (i, h)),
                pl.BlockSpec((bq, HEAD_DIM), lambda h, i: (i, h))]
    ins = [q, k, v, o, do]
    out_specs = [pl.BlockSpec((bq, dqk), lambda h, i: (i, h)),
                 pl.BlockSpec((s_len, dqk), lambda h, i: (0, h)),
                 pl.BlockSpec((s_len, HEAD_DIM), lambda h, i: (0, h))]
    out_shape = [jax.ShapeDtypeStruct((s_len, n_heads * dqk), BF16), jax.ShapeDtypeStruct((s_len, n_heads * dqk), BF16),
                 jax.ShapeDtypeStruct((s_len, n_heads * HEAD_DIM), BF16)]
    if norm:
        in_specs.append(pl.BlockSpec((2, 1, dqk), lambda h, i: (0, 0, 0)))
        ins.append(gains)
        out_specs.append(pl.BlockSpec((2, 1, dqk), lambda h, i: (0, 0, 0)))
        out_shape.append(jax.ShapeDtypeStruct((2, 1, dqk), F32))
    if fox:
        in_specs += [pl.BlockSpec((None, bq, 1), lambda h, i: (h, i, 0)), pl.BlockSpec((None, 1, s_len), lambda h, i: (h, 0, 0))]
        ins += [cq, ck]
        out_specs += [pl.BlockSpec((None, bq, 1), lambda h, i: (h, i, 0)), pl.BlockSpec((None, 1, s_len), lambda h, i: (h, 0, 0))]
        out_shape += [jax.ShapeDtypeStruct((n_heads, s_len, 1), F32), jax.ShapeDtypeStruct((n_heads, 1, s_len), F32)]
    return pl.pallas_call(
        body, name=name, grid=(n_heads, nq), in_specs=in_specs, out_specs=out_specs, out_shape=out_shape,
        scratch_shapes=[pltpu.VMEM((s_len, dqk), F32), pltpu.VMEM((s_len, HEAD_DIM), F32)],
        compiler_params=_params("arbitrary", "arbitrary"))(*ins)


def _split3(x):
    hi = x.astype(BF16)
    r1 = x - hi.astype(F32)
    mid = r1.astype(BF16)
    lo = (r1 - mid.astype(F32)).astype(BF16)
    return hi, mid, lo


def _seq_scan(x, *, reverse):
    n = x.shape[0] // LANES
    a, b = _iota((LANES, LANES), 0), _iota((LANES, LANES), 1)
    tri = ((b >= a) if reverse else (b <= a)).astype(BF16)
    outs = [None] * n
    run = jnp.zeros((1, x.shape[1]), F32)
    for blk in (range(n - 1, -1, -1) if reverse else range(n)):
        xb = x[blk * LANES:(blk + 1) * LANES, :]
        hi, mid, lo = _split3(xb)
        outs[blk] = _dot(tri, hi, 1, 0) + _dot(tri, mid, 1, 0) + _dot(tri, lo, 1, 0) + run
        run = run + jnp.sum(xb, axis=0, keepdims=True)
    return jnp.concatenate(outs, axis=0)


def _fgate_fwd(qkvf, b_f, *, fcol, name):
    s_len = qkvf.shape[0]

    def body(f_ref, b_ref, cum_ref):
        z = f_ref[...] + b_ref[...]
        cum_ref[...] = _seq_scan(-_softplus(-z), reverse=False)

    return pl.pallas_call(
        body, name=name, grid=(1,),
        in_specs=[pl.BlockSpec((s_len, LANES), lambda i: (0, fcol)), pl.BlockSpec((1, LANES), lambda i: (0, 0))],
        out_specs=pl.BlockSpec((s_len, LANES), lambda i: (0, 0)), out_shape=jax.ShapeDtypeStruct((s_len, LANES), F32),
        compiler_params=_params("arbitrary"))(qkvf, b_f)


def _fgate_bwd(qkvf, b_f, dcum_a, dcum_b, *, fcol, n_heads, name):
    s_len = qkvf.shape[0]

    def body(f_ref, b_ref, da_ref, db_ref, dz_ref, dbias_ref):
        z = f_ref[...] + b_ref[...]
        dlog = _seq_scan(da_ref[...] + db_ref[...], reverse=True)
        dz = dlog * jnp.exp(-_softplus(z))
        dz = jnp.where(_iota(dz.shape, 1) < n_heads, dz, 0.0)
        dz_ref[...] = dz.astype(BF16)
        dbias_ref[...] = jnp.sum(dz, axis=0, keepdims=True)

    full = pl.BlockSpec((s_len, LANES), lambda i: (0, 0))
    vec = pl.BlockSpec((1, LANES), lambda i: (0, 0))
    return pl.pallas_call(
        body, name=name, grid=(1,),
        in_specs=[pl.BlockSpec((s_len, LANES), lambda i: (0, fcol)), vec, full, full],
        out_specs=[full, vec], out_shape=[jax.ShapeDtypeStruct((s_len, LANES), BF16), jax.ShapeDtypeStruct((1, LANES), F32)],
        compiler_params=_params("arbitrary"))(qkvf, b_f, dcum_a, dcum_b)


def _rope_swap(x):
    half = MLA_ROPE // 2
    lane = _iota(x.shape, 1)
    sw = jnp.where(lane < half, pltpu.roll(x, LANES - half, axis=1), pltpu.roll(x, half, axis=1))
    return jnp.where(lane < MLA_ROPE, sw, 0.0)


def _mla_prep_fwd(qp, kv, c, gq, gk, cos_t, sin_t, *, n_heads, name, bs=512):
    s_len = qp.shape[0]
    bs = _div_block(s_len, bs, 8)
    krope_col = (MLA_Q_RANK + MLA_KV_RANK) // LANES

    def body(qn_ref, qr_ref, kn_ref, kr_ref, gq_ref, gk_ref, cos_ref, sin_ref, qc_ref, kc_ref):
        cos_v, sin_v = cos_ref[...], sin_ref[...]

        def rope(x, g):
            xv = x.astype(F32)
            inv = lax.rsqrt(jnp.sum(xv * xv, axis=-1, keepdims=True) / MLA_ROPE + NORM_EPS)
            y = xv * inv * g
            return y * cos_v + _rope_swap(y) * sin_v

        qc_ref[:, :LANES] = _head_norm(qn_ref[...], gq_ref[0]).astype(BF16)
        qc_ref[:, LANES:] = rope(qr_ref[...], gq_ref[1]).astype(BF16)
        kc_ref[:, :LANES] = _head_norm(kn_ref[...], gk_ref[0]).astype(BF16)
        kc_ref[:, LANES:] = rope(kr_ref[...], gk_ref[1]).astype(BF16)

    blk = lambda f: pl.BlockSpec((bs, LANES), f)
    gspec = pl.BlockSpec((2, 1, LANES), lambda i, h: (0, 0, 0))
    tspec = pl.BlockSpec((bs, LANES), lambda i, h: (i, 0))
    ospec = pl.BlockSpec((bs, 2 * LANES), lambda i, h: (i, h))
    oshape = jax.ShapeDtypeStruct((s_len, n_heads * 2 * LANES), BF16)
    return pl.pallas_call(
        body, name=name, grid=(s_len // bs, n_heads),
        in_specs=[blk(lambda i, h: (i, h)), blk(lambda i, h: (i, n_heads + h)), blk(lambda i, h: (i, 2 * h)),
                  blk(lambda i, h: (i, krope_col)), gspec, gspec, tspec, tspec],
        out_specs=[ospec, ospec], out_shape=[oshape, oshape],
        compiler_params=_params("parallel", "parallel"))(qp, qp, kv, c, gq, gk, cos_t, sin_t)


def _mla_prep_bwd(qp, kv, c, gq, gk, cos_t, sin_t, dqc, dkc, dv, *, n_heads, name, bs=512):
    s_len = qp.shape[0]
    bs = _div_block(s_len, bs, 8)
    krope_col = (MLA_Q_RANK + MLA_KV_RANK) // LANES

    def body(qn_ref, qr_ref, kn_ref, kr_ref, gq_ref, gk_ref, cos_ref, sin_ref, dqc_ref, dkc_ref, dv_ref,
             dqn_ref, dqr_ref, dkv_ref, dkr_ref, dgq_ref, dgk_ref):
        i, h = pl.program_id(0), pl.program_id(1)
        cos_v, sin_v = cos_ref[...], sin_ref[...]

        @pl.when((i == 0) & (h == 0))
        def _():
            dgq_ref[...] = jnp.zeros_like(dgq_ref)
            dgk_ref[...] = jnp.zeros_like(dgk_ref)

        @pl.when(h == 0)
        def _():
            dkr_ref[...] = jnp.zeros_like(dkr_ref)

        def unrope(dy):
            dy = dy.astype(F32)
            return dy * cos_v + _rope_swap(dy * sin_v)

        dqn, dg = _rms_bwd_math(qn_ref[...].astype(F32), gq_ref[0], dqc_ref[:, :LANES].astype(F32), MLA_NOPE)
        dgq_ref[0] += jnp.sum(dg, axis=0, keepdims=True)
        dqn_ref[...] = dqn.astype(BF16)
        dqr, dg = _rms_bwd_math(qr_ref[...].astype(F32), gq_ref[1], unrope(dqc_ref[:, LANES:]), MLA_ROPE)
        dgq_ref[1] += jnp.sum(dg, axis=0, keepdims=True)
        dqr_ref[...] = dqr.astype(BF16)
        dkn, dg = _rms_bwd_math(kn_ref[...].astype(F32), gk_ref[0], dkc_ref[:, :LANES].astype(F32), MLA_NOPE)
        dgk_ref[0] += jnp.sum(dg, axis=0, keepdims=True)
        dkv_ref[:, :LANES] = dkn.astype(BF16)
        dkv_ref[:, LANES:] = dv_ref[...]
        dkr, dg = _rms_bwd_math(kr_ref[...].astype(F32), gk_ref[1], unrope(dkc_ref[:, LANES:]), MLA_ROPE)
        dgk_ref[1] += jnp.sum(dg, axis=0, keepdims=True)
        dkr_ref[...] += dkr

    blk = lambda f: pl.BlockSpec((bs, LANES), f)
    gspec = pl.BlockSpec((2, 1, LANES), lambda i, h: (0, 0, 0))
    tspec = pl.BlockSpec((bs, LANES), lambda i, h: (i, 0))
    cat = pl.BlockSpec((bs, 2 * LANES), lambda i, h: (i, h))
    head = blk(lambda i, h: (i, h))
    hshape = jax.ShapeDtypeStruct((s_len, n_heads * LANES), BF16)
    gshape = jax.ShapeDtypeStruct((2, 1, LANES), F32)
    return pl.pallas_call(
        body, name=name, grid=(s_len // bs, n_heads),
        in_specs=[head, blk(lambda i, h: (i, n_heads + h)), blk(lambda i, h: (i, 2 * h)),
                  blk(lambda i, h: (i, krope_col)), gspec, gspec, tspec, tspec, cat, cat, head],
        out_specs=[head, head, cat, tspec, gspec, gspec],
        out_shape=[hshape, hshape, jax.ShapeDtypeStruct((s_len, n_heads * 2 * LANES), BF16),
                   jax.ShapeDtypeStruct((s_len, LANES), F32), gshape, gshape],
        compiler_params=_params("arbitrary", "arbitrary"))(qp, qp, kv, c, gq, gk, cos_t, sin_t, dqc, dkc, dv)


def _mla_latent_fwd(c, ga, *, name, br=256):
    s_len = c.shape[0]
    br = _div_block(s_len, br, 8)

    def body(c_ref, g_ref, o_ref):
        for part in range(2):
            sl = slice(part * MLA_Q_RANK, (part + 1) * MLA_Q_RANK)
            o_ref[:, sl] = _head_norm(c_ref[:, sl], g_ref[:, sl]).astype(BF16)

    w = MLA_Q_RANK + MLA_KV_RANK
    return pl.pallas_call(
        body, name=name, grid=(s_len // br,),
        in_specs=[pl.BlockSpec((br, w), lambda i: (i, 0)), pl.BlockSpec((1, w), lambda i: (0, 0))],
        out_specs=pl.BlockSpec((br, w), lambda i: (i, 0)), out_shape=jax.ShapeDtypeStruct((s_len, w), BF16),
        compiler_params=_params("parallel"))(c, ga)


def _mla_latent_bwd(c, ga, dcn_q, dcn_kv, dk_rope, *, name, br=256):
    s_len, cw = c.shape
    br = _div_block(s_len, br, 8)
    w = MLA_Q_RANK + MLA_KV_RANK

    def body(c_ref, g_ref, dq_ref, dkv_ref, dkr_ref, dc_ref, dg_ref):
        @pl.when(pl.program_id(0) == 0)
        def _():
            dg_ref[...] = jnp.zeros_like(dg_ref)

        for part, d_ref in enumerate((dq_ref, dkv_ref)):
            sl = slice(part * MLA_Q_RANK, (part + 1) * MLA_Q_RANK)
            dx, dg = _rms_bwd_math(c_ref[:, sl].astype(F32), g_ref[:, sl], d_ref[...].astype(F32), MLA_Q_RANK)
            dc_ref[:, sl] = dx.astype(BF16)
            dg_ref[:, sl] += jnp.sum(dg, axis=0, keepdims=True)
        dc_ref[:, w:] = dkr_ref[...].astype(BF16)

    return pl.pallas_call(
        body, name=name, grid=(s_len // br,),
        in_specs=[pl.BlockSpec((br, w), lambda i: (i, 0)), pl.BlockSpec((1, w), lambda i: (0, 0)),
                  pl.BlockSpec((br, MLA_Q_RANK), lambda i: (i, 0)), pl.BlockSpec((br, MLA_KV_RANK), lambda i: (i, 0)),
                  pl.BlockSpec((br, LANES), lambda i: (i, 0))],
        out_specs=[pl.BlockSpec((br, cw), lambda i: (i, 0)), pl.BlockSpec((1, w), lambda i: (0, 0))],
        out_shape=[jax.ShapeDtypeStruct((s_len, cw), BF16), jax.ShapeDtypeStruct((1, w), F32)],
        compiler_params=_params("arbitrary"))(c, ga, dcn_q, dcn_kv, dk_rope)


_GELU_C = 0.7978845608028654


def _gelu(x):
    return 0.5 * x * (1.0 + jnp.tanh(_GELU_C * (x + 0.044715 * x * x * x)))


def _gelu_grad(x):
    t = jnp.tanh(_GELU_C * (x + 0.044715 * x * x * x))
    return 0.5 * (1.0 + t) + 0.5 * x * (1.0 - t * t) * _GELU_C * (1.0 + 3 * 0.044715 * x * x)


def _sgu_act_fwd(uv, vg, *, name, br=256):
    s_len, w2 = uv.shape
    w = w2 // 2
    br = _div_block(s_len, br, 8)

    def body(uv_ref, g_ref, u_ref, v_ref):
        u_ref[...] = _gelu(uv_ref[:, :w])
        v_ref[...] = _head_norm(_gelu(uv_ref[:, w:]), g_ref[...]).astype(BF16)

    row = lambda c: pl.BlockSpec((br, c), lambda i: (i, 0))
    return pl.pallas_call(
        body, name=name, grid=(s_len // br,), in_specs=[row(w2), pl.BlockSpec((1, w), lambda i: (0, 0))],
        out_specs=[row(w), row(w)], out_shape=[jax.ShapeDtypeStruct((s_len, w), F32), jax.ShapeDtypeStruct((s_len, w), BF16)],
        compiler_params=_params("parallel"))(uv, vg)


def _sgu_act_bwd(uv, vg, du, dvn, *, name, br=256):
    s_len, w2 = uv.shape
    w = w2 // 2
    br = _div_block(s_len, br, 8)

    def body(uv_ref, g_ref, du_ref, dvn_ref, duv_ref, dg_ref):
        @pl.when(pl.program_id(0) == 0)
        def _():
            dg_ref[...] = jnp.zeros_like(dg_ref)

        up, vp = uv_ref[:, :w], uv_ref[:, w:]
        duv_ref[:, :w] = (du_ref[...] * _gelu_grad(up)).astype(BF16)
        dva, dg = _rms_bwd_math(_gelu(vp), g_ref[...], dvn_ref[...], w)
        dg_ref[...] += jnp.sum(dg, axis=0, keepdims=True)
        duv_ref[:, w:] = (dva * _gelu_grad(vp)).astype(BF16)

    row = lambda c: pl.BlockSpec((br, c), lambda i: (i, 0))
    vec = pl.BlockSpec((1, w), lambda i: (0, 0))
    return pl.pallas_call(
        body, name=name, grid=(s_len // br,), in_specs=[row(w2), vec, row(w), row(w)], out_specs=[row(w2), vec],
        out_shape=[jax.ShapeDtypeStruct((s_len, w2), BF16), jax.ShapeDtypeStruct((1, w), F32)],
        compiler_params=_params("arbitrary"))(uv, vg, du, dvn)


def _tril_weights(ws_ref):
    t, s = _iota((SGU_CHUNK, SGU_CHUNK), 0), _iota((SGU_CHUNK, SGU_CHUNK), 1)
    keep = s <= t
    return jnp.where(keep, ws_ref[...], 0.0), keep


def _sgu_mix_fwd(u, vn, w_s, b_s, *, name):
    s_len, w = u.shape
    nc = s_len // SGU_CHUNK

    def body(u_ref, v_ref, ws_ref, b_ref, o_ref):
        wm = _tril_weights(ws_ref)[0].astype(BF16)
        for n in range(nc):
            rows = slice(n * SGU_CHUNK, (n + 1) * SGU_CHUNK)
            mixed = _dot(wm, v_ref[rows, :], 1, 0) + b_ref[...]
            o_ref[rows, :] = (u_ref[rows, :] * mixed).astype(BF16)

    col = pl.BlockSpec((s_len, LANES), lambda g: (0, g))
    return pl.pallas_call(
        body, name=name, grid=(w // LANES,),
        in_specs=[col, col, pl.BlockSpec((None, SGU_CHUNK, SGU_CHUNK), lambda g: (g, 0, 0)),
                  pl.BlockSpec((None, SGU_CHUNK, 1), lambda g: (g, 0, 0))],
        out_specs=col, out_shape=jax.ShapeDtypeStruct((s_len, w), BF16),
        compiler_params=_params("parallel"))(u, vn, w_s, b_s)


def _sgu_mix_bwd(u, vn, w_s, b_s, dgated, *, name):
    s_len, w = u.shape
    nc = s_len // SGU_CHUNK

    def body(u_ref, v_ref, ws_ref, b_ref, dg_ref, du_ref, dv_ref, dws_ref, dbs_ref):
        wf, keep = _tril_weights(ws_ref)
        wm = wf.astype(BF16)
        wmt = wf.T.astype(BF16)
        dws = jnp.zeros((SGU_CHUNK, SGU_CHUNK), F32)
        dbs = jnp.zeros((SGU_CHUNK, 1), F32)
        for n in range(nc):
            rows = slice(n * SGU_CHUNK, (n + 1) * SGU_CHUNK)
            vb = v_ref[rows, :]
            dgv = dg_ref[rows, :].astype(F32)
            mixed = _dot(wm, vb, 1, 0) + b_ref[...]
            du_ref[rows, :] = dgv * mixed
            dm = dgv * u_ref[rows, :]
            dmb = dm.astype(BF16)
            dws = dws + _dot(dmb, vb, 1, 1)
            dbs = dbs + jnp.sum(dm, axis=1, keepdims=True)
            dv_ref[rows, :] = _dot(wmt, dmb, 1, 0)
        dws_ref[...] = jnp.where(keep, dws, 0.0)
        dbs_ref[...] = dbs

    col = pl.BlockSpec((s_len, LANES), lambda g: (0, g))
    wspec = pl.BlockSpec((None, SGU_CHUNK, SGU_CHUNK), lambda g: (g, 0, 0))
    bspec = pl.BlockSpec((None, SGU_CHUNK, 1), lambda g: (g, 0, 0))
    return pl.pallas_call(
        body, name=name, grid=(w // LANES,), in_specs=[col, col, wspec, bspec, col],
        out_specs=[col, col, wspec, bspec],
        out_shape=[jax.ShapeDtypeStruct((s_len, w), F32), jax.ShapeDtypeStruct((s_len, w), F32),
                   jax.ShapeDtypeStruct(w_s.shape, F32), jax.ShapeDtypeStruct(b_s.shape, F32)],
        compiler_params=_params("parallel"))(u, vn, w_s, b_s, dgated)


def _shift_down(x, k):
    if k == 0:
        return x
    return jnp.where(_iota(x.shape, 0) >= k, pltpu.roll(x, k, axis=0), 0.0)


def _shift_up(x, k):
    if k == 0:
        return x
    n = x.shape[0]
    return jnp.where(_iota(x.shape, 0) < n - k, pltpu.roll(x, n - k, axis=0), 0.0)


def _conv(u, w_ref, b_ref):
    return b_ref[...] + w_ref[0:1, :] * _shift_down(u, 2) + w_ref[1:2, :] * _shift_down(u, 1) + w_ref[2:3, :] * u


def _sigmoid(x):
    return 0.5 * jnp.tanh(0.5 * x) + 0.5


def _glu_fwd(up, cw, cb, *, name, bc=256):
    s_len, f2 = up.shape
    f = f2 // 2
    bc = _div_block(f, bc)
    nf = f // bc

    def body(ug_ref, uv_ref, wg_ref, wv_ref, bg_ref, bv_ref, o_ref):
        yg = _conv(ug_ref[...], wg_ref, bg_ref)
        yv = _conv(uv_ref[...], wv_ref, bv_ref)
        o_ref[...] = (yg * _sigmoid(yg) * yv).astype(BF16)

    big = lambda off: pl.BlockSpec((s_len, bc), lambda j: (0, j + off))
    wsp = lambda off: pl.BlockSpec((3, bc), lambda j: (0, j + off))
    bsp = lambda off: pl.BlockSpec((1, bc), lambda j: (0, j + off))
    return pl.pallas_call(
        body, name=name, grid=(nf,), in_specs=[big(0), big(nf), wsp(0), wsp(nf), bsp(0), bsp(nf)],
        out_specs=pl.BlockSpec((s_len, bc), lambda j: (0, j)), out_shape=jax.ShapeDtypeStruct((s_len, f), BF16),
        compiler_params=_params("parallel"))(up, up, cw, cw, cb, cb)


def _glu_bwd(up, cw, cb, dact, *, name, bc=256):
    s_len, f2 = up.shape
    f = f2 // 2
    bc = _div_block(f, bc)
    nf = f // bc

    def body(ug_ref, uv_ref, wg_ref, wv_ref, bg_ref, bv_ref, da_ref, du_ref, dw_ref, db_ref):
        ug, uv = ug_ref[...], uv_ref[...]
        yg = _conv(ug, wg_ref, bg_ref)
        yv = _conv(uv, wv_ref, bv_ref)
        da = da_ref[...].astype(F32)
        sg = _sigmoid(yg)
        planes = ((da * yv * (sg * (1.0 + yg * (1.0 - sg))), ug, wg_ref), (da * (yg * sg), uv, wv_ref))
        for plane, (dy, u, w_ref) in enumerate(planes):
            dy1, dy2 = _shift_up(dy, 1), _shift_up(dy, 2)
            db_ref[plane] = jnp.sum(dy, axis=0, keepdims=True)
            dw_ref[plane, 0:1, :] = jnp.sum(dy2 * u, axis=0, keepdims=True)
            dw_ref[plane, 1:2, :] = jnp.sum(dy1 * u, axis=0, keepdims=True)
            dw_ref[plane, 2:3, :] = jnp.sum(dy * u, axis=0, keepdims=True)
            du_ref[plane] = (w_ref[2:3, :] * dy + w_ref[1:2, :] * dy1 + w_ref[0:1, :] * dy2).astype(BF16)

    big = lambda off: pl.BlockSpec((s_len, bc), lambda j: (0, j + off))
    wsp = lambda off: pl.BlockSpec((3, bc), lambda j: (0, j + off))
    bsp = lambda off: pl.BlockSpec((1, bc), lambda j: (0, j + off))
    planes = lambda r: pl.BlockSpec((2, r, bc), lambda j: (0, 0, j))
    return pl.pallas_call(
        body, name=name, grid=(nf,),
        in_specs=[big(0), big(nf), wsp(0), wsp(nf), bsp(0), bsp(nf), pl.BlockSpec((s_len, bc), lambda j: (0, j))],
        out_specs=[planes(s_len), planes(3), planes(1)],
        out_shape=[jax.ShapeDtypeStruct((2, s_len, f), BF16), jax.ShapeDtypeStruct((2, 3, f), F32),
                   jax.ShapeDtypeStruct((2, 1, f), F32)],
        compiler_params=_params("parallel"))(up, up, cw, cw, cb, cb, dact)


def _as2d(a):
    return a.reshape(-1, a.shape[-1]) if a.ndim >= 2 else a.reshape(1, -1)


def _adamw(w, g, m, v, *, name, after=None, target_bytes=1 << 20):
    shape = w.shape
    w2, m2, v2 = _as2d(w), _as2d(m), _as2d(v)
    g2 = g.reshape(w2.shape)
    r, c = w2.shape
    br = r if r * c * 4 <= target_bytes else _div_block(r, max(8, target_bytes // (4 * c) // 8 * 8), 8)
    c1 = 1.0 - ADAM_B1 ** ADAM_STEP
    c2 = 1.0 - ADAM_B2 ** ADAM_STEP

    def body(w_ref, g_ref, m_ref, v_ref, *rest):
        d_ref, nm_ref, nv_ref = rest[-3:]
        gv = g_ref[...]
        nm = ADAM_B1 * m_ref[...] + (1.0 - ADAM_B1) * gv
        nv = ADAM_B2 * v_ref[...] + (1.0 - ADAM_B2) * (gv * gv)
        nm_ref[...] = nm
        nv_ref[...] = nv
        d_ref[...] = -ADAM_LR * ((nm / c1) / (jnp.sqrt(nv / c2) + ADAM_EPS) + ADAM_WD * w_ref[...])

    spec = pl.BlockSpec((br, c), lambda i: (i, 0))
    sds = jax.ShapeDtypeStruct((r, c), F32)
    d, nm, nv = pl.pallas_call(
        body, name=name, grid=(r // br,), in_specs=[spec] * 4 + _after_spec(after), out_specs=[spec] * 3,
        out_shape=[sds] * 3, compiler_params=_params("parallel"))(w2, g2, m2, v2, *_after_arg(after))
    return d.reshape(shape), nm.reshape(shape), nv.reshape(shape)


def _add_halves(g, recv, place, *, name, target_bytes=SHARD_BLOCK_BYTES):
    _, _, r, c = g.shape
    br = _div_block(r, max(16, target_bytes // (2 * c) // 16 * 16), 16)

    def body(x_ref, y_ref, c_ref, g_ref, r_ref, o_ref):
        o_ref[...] = (g_ref[...].astype(F32) + r_ref[...].astype(F32)).astype(BF16)

    return pl.pallas_call(
        body, name=name,
        grid_spec=pltpu.PrefetchScalarGridSpec(
            num_scalar_prefetch=3, grid=(N_CHIPS, r // br),
            in_specs=[pl.BlockSpec((None, None, br, c), lambda s, i, xr, yr, cr: (s, cr[0], i, 0)),
                      pl.BlockSpec((None, br, c), lambda s, i, xr, yr, cr: (s, i, 0))],
            out_specs=pl.BlockSpec((None, br, c), lambda s, i, xr, yr, cr: (s, i, 0))),
        out_shape=jax.ShapeDtypeStruct((N_CHIPS, r, c), BF16),
        compiler_params=_params("parallel", "parallel"))(*place, g, recv)


def _sum_chips(p, landed, place, *, name, target_bytes=SHARD_BLOCK_BYTES):
    _, r, c = p.shape
    br = _div_block(r, max(16, target_bytes // (4 * c) // 16 * 16), 16)

    def body(x_ref, y_ref, c_ref, p_ref, l1_ref, l2_ref, l3_ref, o_ref):
        o_ref[...] = ((p_ref[...].astype(F32) + l1_ref[...].astype(F32)) + l2_ref[...].astype(F32)) + l3_ref[...].astype(F32)

    slot = lambda k: pl.BlockSpec((None, br, c), lambda i, xr, yr, cr: ((2 * xr[0] + yr[0] + k) % N_CHIPS, i, 0))
    return pl.pallas_call(
        body, name=name,
        grid_spec=pltpu.PrefetchScalarGridSpec(
            num_scalar_prefetch=3, grid=(r // br,), in_specs=[slot(0), slot(1), slot(2), slot(3)],
            out_specs=pl.BlockSpec((None, br, c), lambda i, xr, yr, cr: (cr[0], i, 0))),
        out_shape=jax.ShapeDtypeStruct((2, r, c), F32),
        compiler_params=_params("parallel"))(*place, p, landed, landed, landed)


def _place_shard(w, place, *, dtype, name, layer=None, after=None, target_bytes=SHARD_BLOCK_BYTES):
    r, c = w.shape[-2:]
    hr = r // 2
    mult = 16 if dtype == BF16 else 8
    br = _div_block(hr, max(mult, target_bytes // (4 * c) // mult * mult), mult)
    nb = hr // br

    def body(x_ref, y_ref, c_ref, w_ref, *rest):
        rest[-1][...] = w_ref[...].astype(dtype)

    if layer is None:
        w_spec = pl.BlockSpec((br, c), lambda h, i, xr, yr, cr: (h * nb + i, 0))
    else:
        w_spec = pl.BlockSpec((None, br, c), lambda h, i, xr, yr, cr: (layer, h * nb + i, 0))
    return pl.pallas_call(
        body, name=name,
        grid_spec=pltpu.PrefetchScalarGridSpec(
            num_scalar_prefetch=3, grid=(2, nb), in_specs=[w_spec] + _after_spec(after),
            out_specs=pl.BlockSpec((None, None, br, c), lambda h, i, xr, yr, cr: (2 * xr[0] + yr[0], h, i, 0))),
        out_shape=jax.ShapeDtypeStruct((N_CHIPS, 2, hr, c), dtype),
        compiler_params=_params("parallel", "parallel"))(*place, w, *_after_arg(after))


def _sum_devices(x, *, name):
    n, r, c = x.shape
    br = _div_block(r, 512, 8)

    def body(x_ref, o_ref):
        acc = x_ref[0]
        for s in range(1, n):
            acc = acc + x_ref[s]
        o_ref[...] = acc

    return pl.pallas_call(
        body, name=name, grid=(r // br,), in_specs=[pl.BlockSpec((n, br, c), lambda i: (0, i, 0))],
        out_specs=pl.BlockSpec((br, c), lambda i: (i, 0)), out_shape=jax.ShapeDtypeStruct((r, c), F32),
        compiler_params=_params("parallel"))(x)


_ANY = pl.BlockSpec(memory_space=pl.ANY)


def _place():
    x, y, c = lax.axis_index("x"), lax.axis_index("y"), lax.axis_index("c")
    other_chips = [(1 - x, y), (x, 1 - y), (1 - x, 1 - y)]
    return x, y, c, other_chips


_HBM = pl.BlockSpec(memory_space=pltpu.HBM)
_SEM = pl.BlockSpec(memory_space=pltpu.SEMAPHORE)
_EFFECT = pltpu.SideEffectType.DATAFLOW_SIDE_EFFECTING


def _in_hbm(a):
    return pltpu.with_memory_space_constraint(a, pltpu.HBM)


def _token_spec():
    return pl.BlockSpec(memory_space=pltpu.VMEM), jax.ShapeDtypeStruct((8, LANES), F32)


def _gather_ici_start(bufs, after, *, name):
    n = len(bufs)

    def body(*refs):
        b_refs = refs[:n]
        send_sems, recv_sems = refs[n + 1], refs[n + 2]
        token = refs[-1]
        x, y, c, chips = _place()
        me = 2 * x + y
        for i in range(n):
            for j, (px, py) in enumerate(chips):
                pltpu.make_async_remote_copy(src_ref=b_refs[i].at[me, c], dst_ref=b_refs[i].at[me, c],
                                             send_sem=send_sems.at[3 * i + j], recv_sem=recv_sems.at[3 * i + j],
                                             device_id=(px, py, c), device_id_type=MESH).start()
        token[...] = jnp.zeros_like(token)

    tspec, tshape = _token_spec()
    outs = pl.pallas_call(
        body, name=name, in_specs=[_HBM] * n + [_ANY], out_specs=(_SEM, _SEM, *[_HBM] * n, tspec),
        out_shape=(pltpu.SemaphoreType.DMA((3 * n,)), pltpu.SemaphoreType.DMA((3 * n,)),
                   *[pltpu.HBM(a.shape, a.dtype) for a in bufs], tshape),
        input_output_aliases={i: 2 + i for i in range(n)},
        compiler_params=pltpu.CompilerParams(has_side_effects=_EFFECT),
    )(*[_in_hbm(a) for a in bufs], after)
    return outs[0], outs[1], list(outs[2:2 + n]), outs[-1]


def _gather_ici_wait(send_sems, recv_sems, bufs, after, *, name):
    n = len(bufs)

    def body(*refs):
        b_refs = refs[:n]
        send_sems, recv_sems = refs[n], refs[n + 1]
        x, y, c, chips = _place()
        me = 2 * x + y
        for i in range(n):
            for j, (px, py) in enumerate(chips):
                cp = pltpu.make_async_remote_copy(src_ref=b_refs[i].at[me, c], dst_ref=b_refs[i].at[2 * px + py, c],
                                                  send_sem=send_sems.at[3 * i + j], recv_sem=recv_sems.at[3 * i + j],
                                                  device_id=(px, py, c), device_id_type=MESH)
                cp.wait_send()
                cp.wait_recv()

    outs = pl.pallas_call(
        body, name=name, in_specs=[_HBM] * n + [_SEM, _SEM, _ANY], out_specs=[_HBM] * n,
        out_shape=[pltpu.HBM(a.shape, a.dtype) for a in bufs], input_output_aliases={i: i for i in range(n)},
        compiler_params=pltpu.CompilerParams(has_side_effects=_EFFECT),
    )(*bufs, send_sems, recv_sems, after)
    return list(outs)


def _gather_d2d(bufs, *, name):
    n = len(bufs)

    def body(*refs):
        b_refs = refs[n:2 * n]
        send_sems, recv_sems = refs[2 * n:]
        x, y, c, chips = _place()
        sends = []
        for i in range(n):
            for j, (px, py) in enumerate(chips):
                mine = b_refs[i].at[2 * px + py, c]
                cp = pltpu.make_async_remote_copy(src_ref=mine, dst_ref=mine, send_sem=send_sems.at[3 * i + j],
                                                  recv_sem=recv_sems.at[3 * i + j], device_id=(x, y, 1 - c),
                                                  device_id_type=MESH)
                cp.start()
                sends.append((cp, i, j, px, py))
        for cp, i, j, px, py in sends:
            theirs = b_refs[i].at[2 * px + py, 1 - c]
            pltpu.make_async_remote_copy(src_ref=theirs, dst_ref=theirs, send_sem=send_sems.at[3 * i + j],
                                         recv_sem=recv_sems.at[3 * i + j], device_id=(x, y, 1 - c),
                                         device_id_type=MESH).wait_recv()
            cp.wait_send()

    return pl.pallas_call(
        body, name=name, in_specs=[_ANY] * n, out_specs=[_ANY] * n, input_output_aliases={i: i for i in range(n)},
        out_shape=[jax.ShapeDtypeStruct(a.shape, a.dtype) for a in bufs],
        scratch_shapes=[pltpu.SemaphoreType.DMA((3 * n,)), pltpu.SemaphoreType.DMA((3 * n,))],
    )(*bufs)


def _sibling_halves(gs, *, name):
    n = len(gs)

    def body(*refs):
        g_refs, o_refs = refs[:n], refs[n:2 * n]
        send_sems, recv_sems = refs[2 * n:]
        x, y, c, _ = _place()
        copies = []
        for i in range(n):
            for s in range(N_CHIPS):
                k = i * N_CHIPS + s
                cp = pltpu.make_async_remote_copy(src_ref=g_refs[i].at[s, 1 - c], dst_ref=o_refs[i].at[s],
                                                  send_sem=send_sems.at[k], recv_sem=recv_sems.at[k],
                                                  device_id=(x, y, 1 - c), device_id_type=MESH)
                cp.start()
                copies.append(cp)
        for cp in copies:
            cp.wait()

    return pl.pallas_call(
        body, name=name, in_specs=[_ANY] * n, out_specs=[_ANY] * n,
        out_shape=[jax.ShapeDtypeStruct((N_CHIPS,) + g.shape[2:], g.dtype) for g in gs],
        scratch_shapes=[pltpu.SemaphoreType.DMA((n * N_CHIPS,)), pltpu.SemaphoreType.DMA((n * N_CHIPS,))],
    )(*gs)


def _chip_scatter_start(ps, after, *, name):
    n = len(ps)

    def body(*refs):
        p_refs, l_refs = refs[:n], refs[n:2 * n]
        send_sems, recv_sems = refs[2 * n + 1], refs[2 * n + 2]
        token = refs[-1]
        x, y, c, chips = _place()
        me = 2 * x + y
        for i in range(n):
            for j, (px, py) in enumerate(chips):
                pltpu.make_async_remote_copy(src_ref=p_refs[i].at[2 * px + py], dst_ref=l_refs[i].at[me],
                                             send_sem=send_sems.at[3 * i + j], recv_sem=recv_sems.at[3 * i + j],
                                             device_id=(px, py, c), device_id_type=MESH).start()
        token[...] = jnp.zeros_like(token)

    tspec, tshape = _token_spec()
    lands = [_in_hbm(lax.empty(p.shape, p.dtype)) for p in ps]
    outs = pl.pallas_call(
        body, name=name, in_specs=[_HBM] * (2 * n) + [_ANY], out_specs=(_SEM, _SEM, *[_HBM] * (2 * n), tspec),
        out_shape=(pltpu.SemaphoreType.DMA((3 * n,)), pltpu.SemaphoreType.DMA((3 * n,)),
                   *[pltpu.HBM(p.shape, p.dtype) for p in ps], *[pltpu.HBM(p.shape, p.dtype) for p in ps], tshape),
        input_output_aliases={i: 2 + i for i in range(2 * n)},
        compiler_params=pltpu.CompilerParams(has_side_effects=_EFFECT),
    )(*[_in_hbm(p) for p in ps], *lands, after)
    return outs[0], outs[1], list(outs[2:2 + n]), list(outs[2 + n:2 + 2 * n]), outs[-1]


def _chip_scatter_wait(send_sems, recv_sems, ps, lands, after, *, name):
    n = len(ps)

    def body(*refs):
        p_refs, l_refs = refs[:n], refs[n:2 * n]
        send_sems, recv_sems = refs[2 * n], refs[2 * n + 1]
        x, y, c, chips = _place()
        for i in range(n):
            for j, (px, py) in enumerate(chips):
                cp = pltpu.make_async_remote_copy(src_ref=p_refs[i].at[2 * px + py], dst_ref=l_refs[i].at[2 * px + py],
                                                  send_sem=send_sems.at[3 * i + j], recv_sem=recv_sems.at[3 * i + j],
                                                  device_id=(px, py, c), device_id_type=MESH)
                cp.wait_send()
                cp.wait_recv()

    outs = pl.pallas_call(
        body, name=name, in_specs=[_HBM] * (2 * n) + [_SEM, _SEM, _ANY], out_specs=[_HBM] * (2 * n),
        out_shape=[pltpu.HBM(p.shape, p.dtype) for p in ps] * 2, input_output_aliases={i: i for i in range(2 * n)},
        compiler_params=pltpu.CompilerParams(has_side_effects=_EFFECT),
    )(*ps, *lands, send_sems, recv_sems, after)
    return list(outs[:n]), list(outs[n:])


def _sibling_share(bufs, *, name):
    n = len(bufs)

    def body(*refs):
        b_refs = refs[n:2 * n]
        send_sems, recv_sems = refs[2 * n:]
        x, y, c, _ = _place()
        copies = []
        for i in range(n):
            cp = pltpu.make_async_remote_copy(src_ref=b_refs[i].at[c], dst_ref=b_refs[i].at[c], send_sem=send_sems.at[i],
                                              recv_sem=recv_sems.at[i], device_id=(x, y, 1 - c), device_id_type=MESH)
            cp.start()
            copies.append((cp, i))
        for cp, i in copies:
            theirs = b_refs[i].at[1 - c]
            pltpu.make_async_remote_copy(src_ref=theirs, dst_ref=theirs, send_sem=send_sems.at[i],
                                         recv_sem=recv_sems.at[i], device_id=(x, y, 1 - c),
                                         device_id_type=MESH).wait_recv()
            cp.wait_send()

    return pl.pallas_call(
        body, name=name, in_specs=[_ANY] * n, out_specs=[_ANY] * n, input_output_aliases={i: i for i in range(n)},
        out_shape=[jax.ShapeDtypeStruct(a.shape, a.dtype) for a in bufs],
        scratch_shapes=[pltpu.SemaphoreType.DMA((n,)), pltpu.SemaphoreType.DMA((n,))],
    )(*bufs)


def _broadcast_all(v, after, *, name):
    def body(v_ref, after_ref, o_ref, send_sems, recv_sems, local_sem):
        x, y, c, _ = _place()
        me = 4 * x + 2 * y + c
        loc = pltpu.make_async_copy(v_ref, o_ref.at[me], local_sem)
        loc.start()
        copies = []
        for k in range(1, 8):
            dx, dy, dc = (k >> 2) & 1, (k >> 1) & 1, k & 1
            to = (1 - x if dx else x, 1 - y if dy else y, 1 - c if dc else c)
            cp = pltpu.make_async_remote_copy(src_ref=v_ref, dst_ref=o_ref.at[me], send_sem=send_sems.at[k - 1],
                                              recv_sem=recv_sems.at[k - 1], device_id=to, device_id_type=MESH)
            cp.start()
            copies.append((cp, k, to))
        for cp, k, to in copies:
            cp.wait_send()
            theirs = o_ref.at[4 * to[0] + 2 * to[1] + to[2]]
            pltpu.make_async_remote_copy(src_ref=theirs, dst_ref=theirs, send_sem=send_sems.at[k - 1],
                                         recv_sem=recv_sems.at[k - 1], device_id=to, device_id_type=MESH).wait_recv()
        loc.wait()

    return pl.pallas_call(
        body, name=name, in_specs=[_ANY, _ANY], out_specs=_ANY,
        out_shape=jax.ShapeDtypeStruct((8,) + v.shape, v.dtype),
        scratch_shapes=[pltpu.SemaphoreType.DMA((7,)), pltpu.SemaphoreType.DMA((7,)), pltpu.SemaphoreType.DMA(())],
    )(v, after)


def _gather_place(shards, place, *, name, after=None):
    names = list(shards)
    bufs, shapes = [], []
    for k in names:
        w, layer = shards[k] if isinstance(shards[k], tuple) else (shards[k], None)
        bufs.append(_place_shard(w, place, dtype=F32 if k == 'small' else BF16, layer=layer, after=after,
                                 name=f"{name}_place_{k}"))
        shapes.append(w.shape[-2:])
    return names, shapes, bufs


def _gather_begin(placed, after, *, name):
    names, shapes, bufs = placed
    send_sems, recv_sems, bufs, token = _gather_ici_start(bufs, after, name=name + "_ici_start")
    return (names, shapes, send_sems, recv_sems, bufs), token


def _gather_end(state, after, *, name):
    names, shapes, send_sems, recv_sems, bufs = state
    bufs = _gather_ici_wait(send_sems, recv_sems, bufs, after, name=name + "_ici_wait")
    bufs = _gather_d2d(bufs, name=name + "_d2d")
    return {k: o.reshape((N_CHIPS,) + sh) for k, o, sh in zip(names, bufs, shapes)}


def _reduce_begin(grads, place, *, name):
    names = list(grads)
    gs = [grads[k].reshape(N_CHIPS, 2, grads[k].shape[1] // 2, grads[k].shape[2]) for k in names]
    recv = _sibling_halves(gs, name=name + "_sib")
    ps = [_add_halves(g, r, place, name=f"{name}_add2_{k}") for g, r, k in zip(gs, recv, names)]
    send_sems, recv_sems, ps, lands, token = _chip_scatter_start(ps, recv[0], name=name + "_scatter_start")
    return (names, [grads[k].shape[1:] for k in names], send_sems, recv_sems, ps, lands), token


def _reduce_end(state, place, after, *, name):
    names, shapes, send_sems, recv_sems, ps, lands = state
    ps, lands = _chip_scatter_wait(send_sems, recv_sems, ps, lands, after, name=name + "_scatter_wait")
    rs = [_sum_chips(p, l, place, name=f"{name}_sum4_{k}") for p, l, k in zip(ps, lands, names)]
    both = _sibling_share(rs, name=name + "_share")
    return {k: b.reshape(sh) for k, b, sh in zip(names, both, shapes)}


def _pad_lanes(a, n=LANES):
    return jnp.pad(a, [(0, 0)] * (a.ndim - 1) + [(0, n - a.shape[-1])])


def _unshard_cols(g):
    return jnp.transpose(g, (1, 0, 2)).reshape(g.shape[1], -1)


def _shard_cols(w):
    k, n = w.shape
    return jnp.transpose(w.reshape(k, N_CHIPS, n // N_CHIPS), (1, 0, 2))


def _ffn_fwd(h, p, tag):
    b = _rms_fwd(h, p['ffn_norm'], name=f"{tag}_ffn_norm")
    up = _mm(b, p['ffn_w_up'], b_sh='n', name=f"{tag}_ffn_up", bn=1408)
    act = _glu_fwd(up, p['ffn_conv_w'], p['ffn_conv_b'], name=f"{tag}_ffn_glu")
    out = _mm(act, p['ffn_w_down'], res=h, name=f"{tag}_ffn_down", bk=704)
    return out, (h, b, up, act)


def _ffn_bwd(dh, saved, p, tag, after=None):
    h, b, up, act = saved
    dact = _mm(dh, p['ffn_w_down'], tb=True, after=after, out_dtype=BF16, name=f"{tag}_ffn_dact", bn=1408)
    dw_down = _mm(act, dh, ta=True, after=after, out_dtype=BF16, name=f"{tag}_ffn_dwdown", bm=1408)
    dup, dcw, dcb = _glu_bwd(up, p['ffn_conv_w'], p['ffn_conv_b'], dact, name=f"{tag}_ffn_dglu")
    dw_up = _mm(b, dup, ta=True, b_sh='n', o_sh=True, out_dtype=BF16, name=f"{tag}_ffn_dwup", bn=1408)
    db = _mm(dup, p['ffn_w_up'], a_sh=True, b_sh='k', name=f"{tag}_ffn_db", bk=1408)
    dcw = jnp.transpose(dcw, (1, 0, 2)).reshape(dcw.shape[1], -1)
    dcb = dcb.reshape(1, -1)
    dh_in, dg = _rms_bwd(h, p['ffn_norm'], db, res=dh, name=f"{tag}_ffn_dnorm")
    big = {'ffn_w_up': dw_up, 'ffn_w_down': dw_down.reshape(N_CHIPS, -1, dw_down.shape[1])}
    small = {'ffn_norm': dg, 'ffn_conv_w': dcw, 'ffn_conv_b': dcb}
    return dh_in, big, small


def _qkv_attn_fwd(kind, h, p, tag, n_heads):
    a = _rms_fwd(h, p['mix_norm'], name=f"{tag}_norm")
    if kind == 'fox':
        qkv = _mm(a, p['w_in'], name=f"{tag}_qkv", bn=896)
        cum = _fgate_fwd(qkv, p['b_f'], fcol=3 * n_heads, name=f"{tag}_fgate")
        cum_t = cum[:, :n_heads].T
        cq, ck = cum_t[:, :, None], cum_t[:, None, :]
    else:
        qkv = _mm(a, p['w_in'], b_sh='n', name=f"{tag}_qkv", bn=768)
        cq = ck = None
    cols = dict(qcol=lambda hh: hh, kcol=lambda hh: n_heads + hh, vcol=lambda hh: 2 * n_heads + hh)
    o = _attn_fwd(kind, qkv, qkv, qkv, name=f"{tag}_attn", n_heads=n_heads, dqk=HEAD_DIM, scale=HEAD_DIM ** -0.5,
                  gains=p['qk_gain'], cq=cq, ck=ck, **cols)
    out = _mm(o, p['w_out'], res=h, name=f"{tag}_out")
    return out, (h, a, qkv, o, cq, ck)


def _qkv_attn_bwd(kind, dh, saved, p, tag, n_heads, after=None):
    h, a, qkv, o, cq, ck = saved
    do = _mm(dh, p['w_out'], tb=True, after=after, out_dtype=BF16, name=f"{tag}_do")
    dw_out = _mm(o, dh, ta=True, after=after, out_dtype=BF16, name=f"{tag}_dwout")
    cols = dict(qcol=lambda hh: hh, kcol=lambda hh: n_heads + hh, vcol=lambda hh: 2 * n_heads + hh)
    outs = _attn_bwd(kind, qkv, qkv, qkv, o, do, name=f"{tag}_dattn", n_heads=n_heads, dqk=HEAD_DIM,
                     scale=HEAD_DIM ** -0.5, gains=p['qk_gain'], cq=cq, ck=ck, **cols)
    dq, dk, dv, dgain = outs[:4]
    small = {'q_gain': dgain[0], 'k_gain': dgain[1]}
    if kind == 'fox':
        dcq, dck = outs[4:]
        dca = _pad_lanes(dcq[:, :, 0].T)
        dcb = _pad_lanes(dck[:, 0, :].T)
        dflog, dbf = _fgate_bwd(qkv, p['b_f'], dca, dcb, fcol=3 * n_heads, n_heads=n_heads, name=f"{tag}_dfgate")
        small['b_f'] = dbf[:, :n_heads]
        dqkv = jnp.concatenate([dq, dk, dv, dflog], axis=1)
        dw_in = _mm(a, dqkv, ta=True, out_dtype=BF16, name=f"{tag}_dwin", bn=896)
        da = _mm(dqkv, p['w_in'], tb=True, name=f"{tag}_da", bk=896)
        dw_in = _shard_cols(dw_in[:, :3 * n_heads * HEAD_DIM + n_heads])
    else:
        dqkv = jnp.concatenate([dq, dk, dv], axis=1)
        dw_in = _mm(a, dqkv, ta=True, o_sh=True, out_dtype=BF16, name=f"{tag}_dwin", bn=768)
        da = _mm(dqkv, p['w_in'], b_sh='k', name=f"{tag}_da", bk=768)
    dh_in, dg = _rms_bwd(h, p['mix_norm'], da, res=dh, name=f"{tag}_dnorm")
    small['mix_norm'] = dg
    big = {'w_in': dw_in, 'w_out': dw_out.reshape(N_CHIPS, -1, dw_out.shape[1])}
    return dh_in, big, small


def _mla_fwd(h, p, tag, n_heads):
    a = _rms_fwd(h, p['mix_norm'], name=f"{tag}_norm")
    c = _mm(a, p['w_in'], name=f"{tag}_latent", bn=1152)
    cn = _mla_latent_fwd(c, p['a_gain'], name=f"{tag}_latent_norm")
    qp = _mm(cn[:, :MLA_Q_RANK], p['w_q_b'], name=f"{tag}_q_up")
    kv = _mm(cn[:, MLA_Q_RANK:], p['w_kv_b'], b_sh='n', name=f"{tag}_kv_up")
    qc, kc = _mla_prep_fwd(qp, kv, c, p['gq'], p['gk'], p['cos'], p['sin'], n_heads=n_heads, name=f"{tag}_prep")
    cols = dict(qcol=lambda hh: hh, kcol=lambda hh: hh, vcol=lambda hh: 2 * hh + 1)
    scale = (MLA_NOPE + MLA_ROPE) ** -0.5
    o = _attn_fwd('mla', qc, kc, kv, name=f"{tag}_attn", n_heads=n_heads, dqk=2 * LANES, scale=scale, **cols)
    out = _mm(o, p['w_out'], res=h, name=f"{tag}_out")
    return out, (h, a, c, cn, qp, kv, qc, kc, o)


def _mla_bwd(dh, saved, p, tag, n_heads, after=None):
    h, a, c, cn, qp, kv, qc, kc, o = saved
    do = _mm(dh, p['w_out'], tb=True, after=after, out_dtype=BF16, name=f"{tag}_do")
    dw_out = _mm(o, dh, ta=True, after=after, out_dtype=BF16, name=f"{tag}_dwout")
    cols = dict(qcol=lambda hh: hh, kcol=lambda hh: hh, vcol=lambda hh: 2 * hh + 1)
    scale = (MLA_NOPE + MLA_ROPE) ** -0.5
    dqc, dkc, dv = _attn_bwd('mla', qc, kc, kv, o, do, name=f"{tag}_dattn", n_heads=n_heads, dqk=2 * LANES,
                             scale=scale, **cols)
    dqn, dqr, dkv, dkr, dgq, dgk = _mla_prep_bwd(qp, kv, c, p['gq'], p['gk'], p['cos'], p['sin'], dqc, dkc, dv,
                                                 n_heads=n_heads, name=f"{tag}_dprep")
    dqp = jnp.concatenate([dqn, dqr], axis=1)
    cn_q, cn_kv = cn[:, :MLA_Q_RANK], cn[:, MLA_Q_RANK:]
    dw_q_b = _mm(cn_q, dqp, ta=True, out_dtype=BF16, name=f"{tag}_dwqb", bm=512)
    dcn_q = _mm(dqp, p['w_q_b'], tb=True, out_dtype=BF16, name=f"{tag}_dcnq")
    dw_kv_b = _mm(cn_kv, dkv, ta=True, o_sh=True, out_dtype=BF16, name=f"{tag}_dwkvb", bm=512)
    dcn_kv = _mm(dkv, p['w_kv_b'], b_sh='k', out_dtype=BF16, name=f"{tag}_dcnkv")
    dc, dga = _mla_latent_bwd(c, p['a_gain'], dcn_q, dcn_kv, dkr, name=f"{tag}_dlatent")
    dw_in = _mm(a, dc, ta=True, out_dtype=BF16, name=f"{tag}_dwin", bn=1152)
    da = _mm(dc, p['w_in'], tb=True, name=f"{tag}_da", bk=1152)
    dh_in, dg = _rms_bwd(h, p['mix_norm'], da, res=dh, name=f"{tag}_dnorm")
    k_rank = dw_q_b.shape[0]
    nope = dw_q_b[:, :n_heads * LANES].reshape(k_rank, n_heads, LANES)
    rope = dw_q_b[:, n_heads * LANES:].reshape(k_rank, n_heads, LANES)[:, :, :MLA_ROPE]
    dw_q_b = jnp.concatenate([nope, rope], axis=2).reshape(k_rank, n_heads * (MLA_NOPE + MLA_ROPE))
    w_in_cols = MLA_Q_RANK + MLA_KV_RANK + MLA_ROPE
    big = {'w_in': dw_in[:, :w_in_cols].reshape(N_CHIPS, -1, w_in_cols), 'w_q_b': _shard_cols(dw_q_b),
           'w_kv_b': dw_kv_b, 'w_out': dw_out.reshape(N_CHIPS, -1, dw_out.shape[1])}
    small = {'mix_norm': dg, 'q_a_gain': dga[:, :MLA_Q_RANK], 'kv_a_gain': dga[:, MLA_Q_RANK:],
             'q_gain': jnp.concatenate([dgq[0], dgq[1][:, :MLA_ROPE]], axis=1),
             'k_gain': jnp.concatenate([dgk[0], dgk[1][:, :MLA_ROPE]], axis=1)}
    return dh_in, big, small


def _sgu_fwd(h, p, tag):
    a = _rms_fwd(h, p['mix_norm'], name=f"{tag}_norm")
    uv = _mm(a, p['w_in'], b_sh='n', name=f"{tag}_in")
    u, vn = _sgu_act_fwd(uv, p['v_gain'], name=f"{tag}_act")
    gated = _sgu_mix_fwd(u, vn, p['w_s'], p['b_s'], name=f"{tag}_mix")
    out = _mm(gated, p['w_out'], res=h, name=f"{tag}_out")
    return out, (h, a, uv, u, vn, gated)


def _sgu_bwd(dh, saved, p, tag, after=None):
    h, a, uv, u, vn, gated = saved
    dgated = _mm(dh, p['w_out'], tb=True, after=after, out_dtype=BF16, name=f"{tag}_dgated")
    dw_out = _mm(gated, dh, ta=True, after=after, out_dtype=BF16, name=f"{tag}_dwout")
    du, dvn, dws, dbs = _sgu_mix_bwd(u, vn, p['w_s'], p['b_s'], dgated, name=f"{tag}_dmix")
    duv, dvg = _sgu_act_bwd(uv, p['v_gain'], du, dvn, name=f"{tag}_dact")
    dw_in = _mm(a, duv, ta=True, o_sh=True, out_dtype=BF16, name=f"{tag}_dwin")
    da = _mm(duv, p['w_in'], b_sh='k', name=f"{tag}_da")
    dh_in, dg = _rms_bwd(h, p['mix_norm'], da, res=dh, name=f"{tag}_dnorm")
    big = {'w_in': dw_in, 'w_out': dw_out.reshape(N_CHIPS, -1, dw_out.shape[1])}
    small = {'mix_norm': dg, 'v_gain': dvg, 'w_s': dws, 'b_s': dbs[:, :, 0]}
    return dh_in, big, small


def _pack(parts):
    flat = jnp.concatenate([p.reshape(-1).astype(F32) for p in parts])
    rows = -(-flat.shape[0] // LANES)
    rows = -(-rows // 32) * 32
    return jnp.pad(flat, (0, rows * LANES - flat.shape[0])).reshape(rows, LANES)


def _unpack(packed, shapes):
    flat = packed.reshape(-1)
    out, off = [], 0
    for s in shapes:
        n = 1
        for d in s:
            n *= d
        out.append(flat[off:off + n].reshape(s))
        off += n
    return out


MIXERS = ('fox', 'mla', 'sb', 'sgu')
WEIGHT_NAMES = ['mix_norm', 'ffn_norm', 'fox_w_in', 'fox_b_f', 'fox_q_gain', 'fox_k_gain', 'fox_w_out', 'mla_w_in',
                'mla_q_a_gain', 'mla_kv_a_gain', 'mla_w_q_b', 'mla_w_kv_b', 'mla_q_gain', 'mla_k_gain', 'mla_w_out',
                'sb_w_in', 'sb_q_gain', 'sb_k_gain', 'sb_w_out', 'sgu_w_in', 'sgu_v_gain', 'sgu_w_s', 'sgu_b_s',
                'sgu_w_out', 'ffn_w_up', 'ffn_conv_w', 'ffn_conv_b', 'ffn_w_down']
SMALL_SHARDED = {'mla_q_a_gain': 1, 'mla_kv_a_gain': 1, 'sgu_v_gain': 1, 'ffn_conv_w': 2}
BIG = ['fox_w_in', 'fox_w_out', 'mla_w_in', 'mla_w_q_b', 'mla_w_kv_b', 'mla_w_out', 'sb_w_in', 'sb_w_out', 'sgu_w_in',
       'sgu_w_out', 'ffn_w_up', 'ffn_w_down']


def kernel(x, positions, mix_norm, ffn_norm, fox_w_in, fox_b_f, fox_q_gain, fox_k_gain, fox_w_out, mla_w_in, mla_q_a_gain, mla_kv_a_gain, mla_w_q_b, mla_w_kv_b, mla_q_gain, mla_k_gain, mla_w_out, sb_w_in, sb_q_gain, sb_k_gain, sb_w_out, sgu_w_in, sgu_v_gain, sgu_w_s, sgu_b_s, sgu_w_out, ffn_w_up, ffn_conv_w, ffn_conv_b, ffn_w_down, loss_target, m_mix_norm, m_ffn_norm, m_fox_w_in, m_fox_b_f, m_fox_q_gain, m_fox_k_gain, m_fox_w_out, m_mla_w_in, m_mla_q_a_gain, m_mla_kv_a_gain, m_mla_w_q_b, m_mla_w_kv_b, m_mla_q_gain, m_mla_k_gain, m_mla_w_out, m_sb_w_in, m_sb_q_gain, m_sb_k_gain, m_sb_w_out, m_sgu_w_in, m_sgu_v_gain, m_sgu_w_s, m_sgu_b_s, m_sgu_w_out, m_ffn_w_up, m_ffn_conv_w, m_ffn_conv_b, m_ffn_w_down, v_mix_norm, v_ffn_norm, v_fox_w_in, v_fox_b_f, v_fox_q_gain, v_fox_k_gain, v_fox_w_out, v_mla_w_in, v_mla_q_a_gain, v_mla_kv_a_gain, v_mla_w_q_b, v_mla_w_kv_b, v_mla_q_gain, v_mla_k_gain, v_mla_w_out, v_sb_w_in, v_sb_q_gain, v_sb_k_gain, v_sb_w_out, v_sgu_w_in, v_sgu_v_gain, v_sgu_w_s, v_sgu_b_s, v_sgu_w_out, v_ffn_w_up, v_ffn_conv_w, v_ffn_conv_b, v_ffn_w_down):
    args = dict(locals())
    W = {k: args[k] for k in WEIGHT_NAMES}
    M = {k: args['m_' + k] for k in WEIGHT_NAMES}
    V = {k: args['v_' + k] for k in WEIGHT_NAMES}
    depth = mix_norm.shape[0]
    s_len, d_model = x.shape[1], x.shape[2]
    n_heads = d_model // HEAD_DIM
    assert all(W[k].shape[0] == 1 for k in WEIGHT_NAMES if k.split('_')[0] in MIXERS), "one layer per mixer"
    xi, yi, ci = lax.axis_index("x"), lax.axis_index("y"), lax.axis_index("c")
    chip = 2 * xi + yi
    place = tuple(jnp.reshape(v, (1,)).astype(jnp.int32) for v in (xi, yi, ci))

    small_local = _pack([W[k][0] if k != 'ffn_conv_w' else W[k] for k in SMALL_SHARDED])

    def piece_shards(i, part):
        if part == 'ffn':
            return {'ffn_w_up': (W['ffn_w_up'], i), 'ffn_w_down': (W['ffn_w_down'], i)}
        mixer = MIXERS[i % len(MIXERS)]
        shards = {k: W[k][0] for k in BIG if k.startswith(mixer + '_')}
        if i == 0:
            shards['small'] = small_local
        return shards

    pieces = [(i, part) for i in range(depth) for part in ('mixer', 'ffn')]
    gathered = {}
    pname = lambda pc: f"gather_l{pc[0]}_{pc[1]}"
    states = {}
    placed = {pieces[0]: _gather_place(piece_shards(*pieces[0]), place, name=pname(pieces[0]))}
    states[pieces[0]], token = _gather_begin(placed[pieces[0]], mix_norm, name=pname(pieces[0]))
    for pc in pieces[1:]:
        placed[pc] = _gather_place(piece_shards(*pc), place, name=pname(pc), after=token)
    gathered[pieces[0]] = _gather_end(states.pop(pieces[0]), placed[pieces[-1]][2][-1], name=pname(pieces[0]))
    first_done = next(iter(gathered[pieces[0]].values()))
    states[pieces[1]], token = _gather_begin(placed[pieces[1]], first_done, name=pname(pieces[1]))
    tokens = [token]
    small_shapes = [W[k][0].shape if k != 'ffn_conv_w' else W[k].shape for k in SMALL_SHARDED]
    per_chip = [_unpack(gathered[pieces[0]]['small'][s], small_shapes) for s in range(N_CHIPS)]
    full_small = {k: jnp.concatenate([per_chip[s][j] for s in range(N_CHIPS)], axis=-1)
                  for j, k in enumerate(SMALL_SHARDED)}

    pos = positions.reshape(s_len).astype(F32)
    inv_freq = ROPE_THETA ** (-jnp.arange(0, MLA_ROPE, 2, dtype=F32) / MLA_ROPE)
    ang = pos[:, None] * inv_freq
    cos_t = _pad_lanes(jnp.concatenate([jnp.cos(ang), jnp.cos(ang)], axis=1))
    sin_t = _pad_lanes(jnp.concatenate([-jnp.sin(ang), jnp.sin(ang)], axis=1))

    def piece_params(i, part):
        mixer = MIXERS[i % len(MIXERS)]
        g = gathered[(i, part)]
        if part == 'ffn':
            return mixer, {'ffn_norm': ffn_norm[i:i + 1], 'ffn_w_up': g['ffn_w_up'],
                           'ffn_w_down': g['ffn_w_down'].reshape(-1, d_model),
                           'ffn_conv_w': full_small['ffn_conv_w'][i], 'ffn_conv_b': ffn_conv_b[i:i + 1]}
        p = {'mix_norm': mix_norm[i:i + 1]}
        rows = lambda w: w.reshape(-1, w.shape[-1])
        if mixer == 'fox':
            w = _unshard_cols(g['fox_w_in'])
            p['w_in'] = jnp.pad(w, ((0, 0), (0, (3 * n_heads + 1) * HEAD_DIM - w.shape[1])))
            p['b_f'] = _pad_lanes(fox_b_f)
            p['qk_gain'] = jnp.stack([fox_q_gain, fox_k_gain])
            p['w_out'] = rows(g['fox_w_out'])
        elif mixer == 'sb':
            p['w_in'] = g['sb_w_in']
            p['qk_gain'] = jnp.stack([sb_q_gain, sb_k_gain])
            p['w_out'] = rows(g['sb_w_out'])
        elif mixer == 'sgu':
            p['w_in'] = g['sgu_w_in']
            p['v_gain'] = full_small['sgu_v_gain'].reshape(1, -1)
            p['w_s'] = sgu_w_s[0]
            p['b_s'] = sgu_b_s[0][:, :, None]
            p['w_out'] = rows(g['sgu_w_out'])
        else:
            w = rows(g['mla_w_in'])
            p['w_in'] = jnp.pad(w, ((0, 0), (0, MLA_Q_RANK + MLA_KV_RANK + LANES - w.shape[1])))
            p['a_gain'] = jnp.concatenate([full_small['mla_q_a_gain'], full_small['mla_kv_a_gain']]).reshape(1, -1)
            wq = _unshard_cols(g['mla_w_q_b']).reshape(MLA_Q_RANK, n_heads, MLA_NOPE + MLA_ROPE)
            p['w_q_b'] = jnp.concatenate([wq[:, :, :MLA_NOPE].reshape(MLA_Q_RANK, -1),
                                          _pad_lanes(wq[:, :, MLA_NOPE:]).reshape(MLA_Q_RANK, -1)], axis=1)
            p['w_kv_b'] = g['mla_w_kv_b']
            p['gq'] = jnp.stack([mla_q_gain[:, :MLA_NOPE], _pad_lanes(mla_q_gain[:, MLA_NOPE:])])
            p['gk'] = jnp.stack([mla_k_gain[:, :MLA_NOPE], _pad_lanes(mla_k_gain[:, MLA_NOPE:])])
            p['cos'], p['sin'] = cos_t, sin_t
            p['w_out'] = rows(g['mla_w_out'])
        return mixer, p

    h = x.reshape(s_len, d_model)
    saved = []
    for n, (i, part) in enumerate(pieces):
        nxt = pieces[n + 1] if n + 1 < len(pieces) else None
        ahead = pieces[n + 2] if n + 2 < len(pieces) else None
        if ahead is not None:
            after = next(iter(gathered[(i, part)].values()))
            states[ahead], token = _gather_begin(placed[ahead], after, name=pname(ahead))
            tokens.append(token)
        mixer, p = piece_params(i, part)
        gain = 'ffn_norm' if part == 'ffn' else 'mix_norm'
        for token in tokens:
            p[gain] = p[gain] + token[0:1, 0:1]
        tokens = []
        tag = f"l{i}_{mixer}"
        if part == 'ffn':
            h, sv = _ffn_fwd(h, p, f"l{i}")
        elif mixer in ('fox', 'sb'):
            h, sv = _qkv_attn_fwd(mixer, h, p, tag, n_heads)
        elif mixer == 'mla':
            h, sv = _mla_fwd(h, p, tag, n_heads)
        else:
            h, sv = _sgu_fwd(h, p, tag)
        saved.append((mixer, p, sv))
        if nxt is not None:
            gathered[nxt] = _gather_end(states.pop(nxt), h, name=pname(nxt))
    loss_row, dh = _loss(h, loss_target.reshape(s_len, d_model))
    loss = lax.psum(loss_row[0, 0], ("x", "y", "c"))

    big_grads, small_grads = {}, {k: [None] * depth for k in ('mix_norm', 'ffn_norm', 'ffn_conv_w', 'ffn_conv_b')}

    def keep(reduced, i):
        for k, v in reduced.items():
            if k.startswith('ffn_'):
                big_grads.setdefault(k, [None] * depth)[i] = v
            else:
                big_grads[k] = v[None]

    state, token, flying = None, None, None
    for n in reversed(range(len(pieces))):
        i, part = pieces[n]
        mixer, p, sv = saved[n]
        tag = f"l{i}_{mixer}"
        if part == 'ffn':
            dh, big, small = _ffn_bwd(dh, sv, p, f"l{i}", after=token)
        else:
            if mixer in ('fox', 'sb'):
                dh, big, small = _qkv_attn_bwd(mixer, dh, sv, p, tag, n_heads, after=token)
            elif mixer == 'mla':
                dh, big, small = _mla_bwd(dh, sv, p, tag, n_heads, after=token)
            else:
                dh, big, small = _sgu_bwd(dh, sv, p, tag, after=token)
            big = {f"{mixer}_{k}": v for k, v in big.items()}
        if state is not None:
            keep(_reduce_end(state, place, dh, name=f"reduce_l{flying[0]}_{flying[1]}"), flying[0])
        state, token = _reduce_begin(big, place, name=f"reduce_l{i}_{part}")
        flying = (i, part)
        for k, v in small.items():
            if k in small_grads:
                small_grads[k][i] = v
            else:
                small_grads[f"{mixer}_{k}"] = v
    last_state = state
    grad_x = dh.reshape(x.shape)
    for k in ('mix_norm', 'ffn_norm', 'ffn_conv_b'):
        small_grads[k] = jnp.concatenate(small_grads[k], axis=0)
    small_grads['ffn_conv_w'] = jnp.stack(small_grads['ffn_conv_w'])

    small_names = [k for k in WEIGHT_NAMES if k not in BIG]
    full_shapes = {k: (W[k].shape[:-1] + (W[k].shape[-1] * N_CHIPS,) if k in SMALL_SHARDED else W[k].shape)
                   for k in small_names}
    packed = _pack([small_grads[k].reshape(full_shapes[k]) for k in small_names])
    last_token = token
    summed = _sum_devices(_broadcast_all(packed, last_token, name="small_bcast"), name="small_sum")
    small_full = dict(zip(small_names, _unpack(summed, [full_shapes[k] for k in small_names])))
    for k in ('ffn_w_up', 'ffn_w_down'):
        big_grads[k] = jnp.stack(big_grads[k])
    grads = dict(big_grads)
    for k in small_names:
        g = small_full[k]
        if k in SMALL_SHARDED:
            n = W[k].shape[-1]
            g = lax.dynamic_slice_in_dim(g, chip * n, n, axis=g.ndim - 1)
        grads[k] = g

    delta, new_m, new_v = {}, {}, {}

    def update(k):
        grads[k] = grads[k].reshape(W[k].shape)
        delta[k], new_m[k], new_v[k] = _adamw(W[k], grads[k], M[k], V[k], after=last_token, name=f"adamw_{k}")

    last_names = [k for k in BIG if k.startswith(MIXERS[0] + '_')]
    for k in WEIGHT_NAMES:
        if k not in last_names:
            update(k)
    keep(_reduce_end(last_state, place, delta['ffn_w_up'], name="reduce_l0_mixer"), 0)
    for k in last_names:
        grads[k] = big_grads[k]
        update(k)
    return (loss, grad_x, *[grads[k] for k in WEIGHT_NAMES], *[delta[k] for k in WEIGHT_NAMES],
            *[new_m[k] for k in WEIGHT_NAMES], *[new_v[k] for k in WEIGHT_NAMES])
```

```python
import functools

import jax
import jax.numpy as jnp
from jax import lax
from jax.experimental import pallas as pl
from jax.experimental.pallas import tpu as pltpu

F32 = jnp.float32
BF16 = jnp.bfloat16
LANES = 128
HEAD_DIM = 128
NORM_EPS = 1e-6
MLA_Q_RANK = 512
MLA_KV_RANK = 512
MLA_NOPE = 128
MLA_ROPE = 64
ROPE_THETA = 10000.0
SGU_CHUNK = 128
N_CHIPS = 4
ADAM_LR, ADAM_B1, ADAM_B2, ADAM_EPS, ADAM_WD, ADAM_STEP = 0.001, 0.9, 0.999, 1e-08, 0.01, 10
VMEM_LIMIT_BYTES = 56 * 1024 * 1024
MM_VMEM_BUDGET_BYTES = 36 * 1024 * 1024
SHARD_BLOCK_BYTES = 4 * 1024 * 1024
MESH = pl.DeviceIdType.MESH
NEG_BIG = -1e30


def _params(*sem):
    return pltpu.CompilerParams(dimension_semantics=sem, vmem_limit_bytes=VMEM_LIMIT_BYTES)


def _div_block(n, target, mult=LANES):
    if n <= target:
        return n
    best = None
    for b in range(mult, target + 1, mult):
        if n % b == 0:
            best = b
    assert best is not None, (n, target, mult)
    return best


def _after_spec(after):
    if after is None:
        return []
    return [pl.BlockSpec(memory_space=pl.ANY)]


def _after_arg(after):
    return [] if after is None else [after]


def _iota(shape, dim):
    return lax.broadcasted_iota(jnp.int32, shape, dim)


def _dot(a, b, ca, cb):
    return lax.dot_general(a, b, (((ca,), (cb,)), ((), ())), preferred_element_type=F32)


def _mm(a, b, *, name, ta=False, tb=False, a_sh=False, b_sh=None, o_sh=False, res=None, after=None, out_dtype=F32,
        bm=1024, bn=1024, bk=512):
    if a_sh:
        assert not ta
        m, k = a.shape[1], a.shape[0] * a.shape[2]
    else:
        m, k = (a.shape[1], a.shape[0]) if ta else a.shape
    if b_sh == 'n':
        n = b.shape[2] * b.shape[0]
        assert b.shape[1] == k and not tb
    elif b_sh == 'k':
        n = b.shape[1]
        assert b.shape[2] * N_CHIPS == k
    else:
        n = b.shape[0] if tb else b.shape[1]
        assert (b.shape[1] if tb else b.shape[0]) == k
    n_sh = n // N_CHIPS
    k_sh = k // N_CHIPS
    bm = _div_block(m, bm, 8 if not ta else LANES)
    bn_limit = n
    if b_sh == 'n':
        bn_limit = b.shape[2]
    if o_sh:
        bn_limit = min(bn_limit, n_sh)
    bn = _div_block(bn_limit, bn)
    assert (not o_sh or n_sh % bn == 0) and (b_sh != 'n' or b.shape[2] % bn == 0)
    bk_limit = k_sh if b_sh == 'k' else k
    if a_sh:
        bk_limit = min(bk_limit, a.shape[2])

    def footprint(kb):
        io = bm * kb * a.dtype.itemsize + kb * bn * b.dtype.itemsize + bm * bn * jnp.dtype(out_dtype).itemsize
        if res is not None:
            io += bm * bn * res.dtype.itemsize
        return 2 * io + (bm * bn * 4 if kb < k else 0)

    bk = max([kb for kb in range(LANES, bk_limit + 1, LANES)
              if bk_limit % kb == 0 and (footprint(kb) <= MM_VMEM_BUDGET_BYTES or kb <= bk)])
    assert (not a_sh or a.shape[2] % bk == 0) and (b_sh != 'k' or k_sh % bk == 0) and k % bk == 0
    nbo = n_sh // bn if o_sh else 1
    nbb = b.shape[2] // bn if b_sh == 'n' else 1
    nks = k_sh // bk if b_sh == 'k' else 1
    nka = a.shape[2] // bk if a_sh else 1
    nk = k // bk

    if a_sh:
        a_spec = pl.BlockSpec((None, bm, bk), lambda i, j, q: (q // nka, i, q % nka))
    elif ta:
        a_spec = pl.BlockSpec((bk, bm), lambda i, j, q: (q, i))
    else:
        a_spec = pl.BlockSpec((bm, bk), lambda i, j, q: (i, q))
    if b_sh == 'n':
        b_spec = pl.BlockSpec((None, bk, bn), lambda i, j, q: (j // nbb, q, j % nbb))
    elif b_sh == 'k':
        b_spec = pl.BlockSpec((None, bn, bk), lambda i, j, q: (q // nks, j, q % nks))
    elif tb:
        b_spec = pl.BlockSpec((bn, bk), lambda i, j, q: (j, q))
    else:
        b_spec = pl.BlockSpec((bk, bn), lambda i, j, q: (q, j))
    if o_sh:
        o_spec = pl.BlockSpec((None, bm, bn), lambda i, j, q: (j // nbo, i, j % nbo))
        o_shape = jax.ShapeDtypeStruct((N_CHIPS, m, n_sh), out_dtype)
    else:
        o_spec = pl.BlockSpec((bm, bn), lambda i, j, q: (i, j))
        o_shape = jax.ShapeDtypeStruct((m, n), out_dtype)
    tb_eff = tb or b_sh == 'k'

    def body(a_ref, b_ref, *rest):
        rest = list(rest)
        if after is not None:
            rest.pop(0)
        r_ref = rest.pop(0) if res is not None else None
        o_ref = rest.pop(0)
        part = _dot(a_ref[...].astype(BF16), b_ref[...].astype(BF16), 0 if ta else 1, 1 if tb_eff else 0)

        def finish(r):
            if res is not None:
                r = r + r_ref[...].astype(F32)
            o_ref[...] = r.astype(out_dtype)

        if nk == 1:
            finish(part)
            return
        acc, = rest
        q = pl.program_id(2)

        @pl.when(q == 0)
        def _():
            acc[...] = part

        @pl.when(q > 0)
        def _():
            acc[...] += part

        @pl.when(q == nk - 1)
        def _():
            finish(acc[...])

    ins = [a, b]
    in_specs = [a_spec, b_spec]
    if after is not None:
        ins.append(after)
        in_specs.append(pl.BlockSpec((8, LANES), lambda i, j, q: (0, 0)))
    if res is not None:
        assert not o_sh
        ins.append(res)
        in_specs.append(pl.BlockSpec((bm, bn), lambda i, j, q: (i, j)))
    return pl.pallas_call(
        body, name=name, grid=(m // bm, n // bn, nk), in_specs=in_specs, out_specs=o_spec, out_shape=o_shape,
        scratch_shapes=[pltpu.VMEM((bm, bn), F32)] if nk > 1 else [],
        compiler_params=_params("parallel", "parallel", "arbitrary"))(*ins)


def _rms_fwd(x, g, *, name, out_dtype=BF16, br=256):
    r, c = x.shape
    br = _div_block(r, br, 8)

    def body(x_ref, g_ref, o_ref):
        xv = x_ref[...].astype(F32)
        inv = lax.rsqrt(jnp.mean(xv * xv, axis=-1, keepdims=True) + NORM_EPS)
        o_ref[...] = (xv * inv * g_ref[...]).astype(out_dtype)

    return pl.pallas_call(
        body, name=name, grid=(r // br,),
        in_specs=[pl.BlockSpec((br, c), lambda i: (i, 0)), pl.BlockSpec((1, c), lambda i: (0, 0))],
        out_specs=pl.BlockSpec((br, c), lambda i: (i, 0)), out_shape=jax.ShapeDtypeStruct((r, c), out_dtype),
        compiler_params=_params("parallel"))(x, g)


def _rms_bwd_math(xv, gv, dyv, n):
    inv = lax.rsqrt(jnp.sum(xv * xv, axis=-1, keepdims=True) / n + NORM_EPS)
    xh = xv * inv
    dyg = dyv * gv
    dx = inv * (dyg - xh * (jnp.sum(dyg * xh, axis=-1, keepdims=True) / n))
    return dx, dyv * xh


def _rms_bwd(x, g, dy, *, name, res=None, br=256):
    r, c = x.shape
    br = _div_block(r, br, 8)

    def body(x_ref, g_ref, dy_ref, *rest):
        if res is not None:
            r_ref, dx_ref, dg_ref = rest
        else:
            dx_ref, dg_ref = rest
        dx, dgr = _rms_bwd_math(x_ref[...].astype(F32), g_ref[...], dy_ref[...].astype(F32), c)
        if res is not None:
            dx = dx + r_ref[...]
        dx_ref[...] = dx

        @pl.when(pl.program_id(0) == 0)
        def _():
            dg_ref[...] = jnp.zeros_like(dg_ref)

        dg_ref[...] += jnp.sum(dgr, axis=0, keepdims=True)

    row = pl.BlockSpec((br, c), lambda i: (i, 0))
    vec = pl.BlockSpec((1, c), lambda i: (0, 0))
    ins = [x, g, dy] + ([res] if res is not None else [])
    return pl.pallas_call(
        body, name=name, grid=(r // br,), in_specs=[row, vec, row] + ([row] if res is not None else []),
        out_specs=[row, vec], out_shape=[jax.ShapeDtypeStruct((r, c), F32), jax.ShapeDtypeStruct((1, c), F32)],
        compiler_params=_params("arbitrary"))(*ins)


def _loss(y, target, *, name="loss", br=256):
    r, c = y.shape
    br = _div_block(r, br, 8)

    def body(y_ref, t_ref, l_ref, dy_ref):
        d = y_ref[...] - t_ref[...]
        dy_ref[...] = d * (1.0 / c)

        @pl.when(pl.program_id(0) == 0)
        def _():
            l_ref[...] = jnp.zeros_like(l_ref)

        part = jnp.sum(d * d, axis=0, keepdims=True)
        l_ref[...] += (0.5 / c) * jnp.sum(part, axis=1, keepdims=True) * jnp.ones((1, LANES), F32)

    row = pl.BlockSpec((br, c), lambda i: (i, 0))
    return pl.pallas_call(
        body, name=name, grid=(r // br,), in_specs=[row, row],
        out_specs=[pl.BlockSpec((1, LANES), lambda i: (0, 0)), row],
        out_shape=[jax.ShapeDtypeStruct((1, LANES), F32), jax.ShapeDtypeStruct((r, c), F32)],
        compiler_params=_params("arbitrary"))(y, target)


def _split2(x):
    hi = x.astype(BF16)
    lo = (x - hi.astype(F32)).astype(BF16)
    return hi, lo


def _lane_scan(x, *, suffix):
    rows, n = x.shape
    nb = n // LANES
    a, b = _iota((LANES, LANES), 0), _iota((LANES, LANES), 1)
    tri = ((a > b) if suffix else (a < b)).astype(BF16)
    outs = [None] * nb
    run = jnp.zeros((rows, 1), F32)
    order = range(nb - 1, -1, -1) if suffix else range(nb)
    for blk in order:
        xb = x[:, blk * LANES:(blk + 1) * LANES]
        hi, lo = _split2(xb)
        outs[blk] = _dot(hi, tri, 1, 0) + _dot(lo, tri, 1, 0) + run
        run = run + jnp.sum(xb, axis=-1, keepdims=True)
    return jnp.concatenate(outs, axis=1)


def _softplus(z):
    return jnp.maximum(z, 0.0) + jnp.log(1.0 + jnp.exp(-jnp.abs(z)))


def _head_norm(x, g):
    xv = x.astype(F32)
    inv = lax.rsqrt(jnp.mean(xv * xv, axis=-1, keepdims=True) + NORM_EPS)
    return xv * inv * g


def _attn_weights(kind, qn, kn, scale, qi, bq, bias):
    s = _dot(qn, kn, 1, 1) * scale
    row = qi * bq + _iota(s.shape, 0)
    col = _iota(s.shape, 1)
    if kind == 'sb':
        strict = col < row
        sp = _softplus(s)
        after = _lane_scan(jnp.where(strict, -sp, 0.0), suffix=True)
        w = jnp.where(strict, jnp.exp(s - sp + after), 0.0)
        return w, (strict, s - sp)
    if bias is not None:
        s = s + bias
    s = jnp.where(col <= row, s, NEG_BIG)
    mx = jnp.max(s, axis=-1, keepdims=True)
    e = jnp.exp(s - mx)
    return e, jnp.sum(e, axis=-1, keepdims=True)


def _attn_fwd(kind, q, k, v, *, name, n_heads, dqk, qcol, kcol, vcol, scale, gains=None, cq=None, ck=None, bq=256):
    s_len = q.shape[0]
    bq = _div_block(s_len, bq, 8)
    norm, fox = gains is not None, cq is not None

    def body(*refs):
        refs = list(refs)
        q_ref, k_ref, v_ref = refs[:3]
        rest = refs[3:]
        g_ref = rest.pop(0) if norm else None
        cq_ref, ck_ref = (rest.pop(0), rest.pop(0)) if fox else (None, None)
        o_ref, = rest
        qi = pl.program_id(1)

        def step(n_keys):
            if norm:
                qn = _head_norm(q_ref[...], g_ref[0]).astype(BF16)
                kn = _head_norm(k_ref[0:n_keys, :], g_ref[1]).astype(BF16)
            else:
                qn, kn = q_ref[...].astype(BF16), k_ref[0:n_keys, :].astype(BF16)
            bias = (cq_ref[...] - ck_ref[:, 0:n_keys]) if fox else None
            w, aux = _attn_weights(kind, qn, kn, scale, qi, bq, bias)
            o = _dot(w.astype(BF16), v_ref[0:n_keys, :].astype(BF16), 1, 0)
            if kind != 'sb':
                o = o / aux
            o_ref[...] = o.astype(BF16)

        for qv in range(s_len // bq):
            pl.when(qi == qv)(functools.partial(step, (qv + 1) * bq))

    in_specs = [pl.BlockSpec((bq, dqk), lambda h, i: (i, qcol(h))),
                pl.BlockSpec((s_len, dqk), lambda h, i: (0, kcol(h))),
                pl.BlockSpec((s_len, HEAD_DIM), lambda h, i: (0, vcol(h)))]
    ins = [q, k, v]
    if norm:
        in_specs.append(pl.BlockSpec((2, 1, dqk), lambda h, i: (0, 0, 0)))
        ins.append(gains)
    if fox:
        in_specs += [pl.BlockSpec((None, bq, 1), lambda h, i: (h, i, 0)), pl.BlockSpec((None, 1, s_len), lambda h, i: (h, 0, 0))]
        ins += [cq, ck]
    return pl.pallas_call(
        body, name=name, grid=(n_heads, s_len // bq), in_specs=in_specs,
        out_specs=pl.BlockSpec((bq, HEAD_DIM), lambda h, i: (i, h)),
        out_shape=jax.ShapeDtypeStruct((s_len, n_heads * HEAD_DIM), BF16),
        compiler_params=_params("parallel", "parallel"))(*ins)


def _attn_bwd(kind, q, k, v, o, do, *, name, n_heads, dqk, qcol, kcol, vcol, scale, gains=None, cq=None, ck=None,
              bq=256):
    s_len = q.shape[0]
    bq = _div_block(s_len, bq, 8)
    nq = s_len // bq
    norm, fox = gains is not None, cq is not None

    def body(*refs):
        refs = list(refs)
        q_ref, k_ref, v_ref, o_ref, do_ref = refs[:5]
        rest = refs[5:]
        g_ref = rest.pop(0) if norm else None
        cq_ref, ck_ref = (rest.pop(0), rest.pop(0)) if fox else (None, None)
        dq_ref, dk_ref, dv_ref = rest.pop(0), rest.pop(0), rest.pop(0)
        dg_ref = rest.pop(0) if norm else None
        dcq_ref, dck_ref = (rest.pop(0), rest.pop(0)) if fox else (None, None)
        dk_acc, dv_acc = rest
        h, qi = pl.program_id(0), pl.program_id(1)

        @pl.when(qi == 0)
        def _():
            dk_acc[...] = jnp.zeros_like(dk_acc)
            dv_acc[...] = jnp.zeros_like(dv_acc)
            if fox:
                dck_ref[...] = jnp.zeros_like(dck_ref)

        if norm:
            @pl.when((qi == 0) & (h == 0))
            def _():
                dg_ref[...] = jnp.zeros_like(dg_ref)


        def step(n_keys):
            if norm:
                qn = _head_norm(q_ref[...], g_ref[0]).astype(BF16)
                kn = _head_norm(k_ref[0:n_keys, :], g_ref[1]).astype(BF16)
            else:
                qn, kn = q_ref[...].astype(BF16), k_ref[0:n_keys, :].astype(BF16)
            vb = v_ref[0:n_keys, :].astype(BF16)
            dob = do_ref[...].astype(BF16)
            bias = (cq_ref[...] - ck_ref[:, 0:n_keys]) if fox else None
            w, aux = _attn_weights(kind, qn, kn, scale, qi, bq, bias)
            dw = _dot(dob, vb, 1, 1)
            if kind == 'sb':
                strict, log_sig = aux
                g = dw * w
                cc = _lane_scan(g, suffix=False)
                sig = jnp.exp(log_sig)
                ds = jnp.where(strict, g * (1.0 - sig) - cc * sig, 0.0)
                pw = w
            else:
                pw = w / aux
                delta = jnp.sum(do_ref[...].astype(F32) * o_ref[...].astype(F32), axis=-1, keepdims=True)
                ds = pw * (dw - delta)
                if fox:
                    dcq_ref[...] = jnp.sum(ds, axis=1, keepdims=True)
                    dck_ref[:, 0:n_keys] -= jnp.sum(ds, axis=0, keepdims=True)
            dsb = (ds * scale).astype(BF16)
            dqn = _dot(dsb, kn, 1, 0)
            dk_acc[0:n_keys, :] += _dot(dsb, qn, 0, 0)
            dv_acc[0:n_keys, :] += _dot(pw.astype(BF16), dob, 0, 0)
            if norm:
                dq, dgr = _rms_bwd_math(q_ref[...].astype(F32), g_ref[0], dqn, dqk)
                dg_ref[0] += jnp.sum(dgr, axis=0, keepdims=True)
                dq_ref[...] = dq.astype(BF16)
            else:
                dq_ref[...] = dqn.astype(BF16)

        for qv in range(nq):
            pl.when(qi == qv)(functools.partial(step, (qv + 1) * bq))

        @pl.when(qi == nq - 1)
        def _():
            if norm:
                dk, dgr = _rms_bwd_math(k_ref[...].astype(F32), g_ref[1], dk_acc[...], dqk)
                dg_ref[1] += jnp.sum(dgr, axis=0, keepdims=True)
                dk_ref[...] = dk.astype(BF16)
            else:
                dk_ref[...] = dk_acc[...].astype(BF16)
            dv_ref[...] = dv_acc[...].astype(BF16)

    in_specs = [pl.BlockSpec((bq, dqk), lambda h, i: (i, qcol(h))),
                pl.BlockSpec((s_len, dqk), lambda h, i: (0, kcol(h))),
                pl.BlockSpec((s_len, HEAD_DIM), lambda h, i: (0, vcol(h))),
                pl.BlockSpec((bq, HEAD_DIM), lambda h, i: (i, h)),
                pl.BlockSpec((bq, HEAD_DIM), lambda h, i: (i, h))]
    ins = [q, k, v, o, do]
    out_specs = [pl.BlockSpec((bq, dqk), lambda h, i: (i, h)),
                 pl.BlockSpec((s_len, dqk), lambda h, i: (0, h)),
                 pl.BlockSpec((s_len, HEAD_DIM), lambda h, i: (0, h))]
    out_shape = [jax.ShapeDtypeStruct((s_len, n_heads * dqk), BF16), jax.ShapeDtypeStruct((s_len, n_heads * dqk), BF16),
                 jax.ShapeDtypeStruct((s_len, n_heads * HEAD_DIM), BF16)]
    if norm:
        in_specs.append(pl.BlockSpec((2, 1, dqk), lambda h, i: (0, 0, 0)))
        ins.append(gains)
        out_specs.append(pl.BlockSpec((2, 1, dqk), lambda h, i: (0, 0, 0)))
        out_shape.append(jax.ShapeDtypeStruct((2, 1, dqk), F32))
    if fox:
        in_specs += [pl.BlockSpec((None, bq, 1), lambda h, i: (h, i, 0)), pl.BlockSpec((None, 1, s_len), lambda h, i: (h, 0, 0))]
        ins += [cq, ck]
        out_specs += [pl.BlockSpec((None, bq, 1), lambda h, i: (h, i, 0)), pl.BlockSpec((None, 1, s_len), lambda h, i: (h, 0, 0))]
        out_shape += [jax.ShapeDtypeStruct((n_heads, s_len, 1), F32), jax.ShapeDtypeStruct((n_heads, 1, s_len), F32)]
    return pl.pallas_call(
        body, name=name, grid=(n_heads, nq), in_specs=in_specs, out_specs=out_specs, out_shape=out_shape,
        scratch_shapes=[pltpu.VMEM((s_len, dqk), F32), pltpu.VMEM((s_len, HEAD_DIM), F32)],
        compiler_params=_params("arbitrary", "arbitrary"))(*ins)


def _split3(x):
    hi = x.astype(BF16)
    r1 = x - hi.astype(F32)
    mid = r1.astype(BF16)
    lo = (r1 - mid.astype(F32)).astype(BF16)
    return hi, mid, lo


def _seq_scan(x, *, reverse):
    n = x.shape[0] // LANES
    a, b = _iota((LANES, LANES), 0), _iota((LANES, LANES), 1)
    tri = ((b >= a) if reverse else (b <= a)).astype(BF16)
    outs = [None] * n
    run = jnp.zeros((1, x.shape[1]), F32)
    for blk in (range(n - 1, -1, -1) if reverse else range(n)):
        xb = x[blk * LANES:(blk + 1) * LANES, :]
        hi, mid, lo = _split3(xb)
        outs[blk] = _dot(tri, hi, 1, 0) + _dot(tri, mid, 1, 0) + _dot(tri, lo, 1, 0) + run
        run = run + jnp.sum(xb, axis=0, keepdims=True)
    return jnp.concatenate(outs, axis=0)


def _fgate_fwd(qkvf, b_f, *, fcol, name):
    s_len = qkvf.shape[0]

    def body(f_ref, b_ref, cum_ref):
        z = f_ref[...] + b_ref[...]
        cum_ref[...] = _seq_scan(-_softplus(-z), reverse=False)

    return pl.pallas_call(
        body, name=name, grid=(1,),
        in_specs=[pl.BlockSpec((s_len, LANES), lambda i: (0, fcol)), pl.BlockSpec((1, LANES), lambda i: (0, 0))],
        out_specs=pl.BlockSpec((s_len, LANES), lambda i: (0, 0)), out_shape=jax.ShapeDtypeStruct((s_len, LANES), F32),
        compiler_params=_params("arbitrary"))(qkvf, b_f)


def _fgate_bwd(qkvf, b_f, dcum_a, dcum_b, *, fcol, n_heads, name):
    s_len = qkvf.shape[0]

    def body(f_ref, b_ref, da_ref, db_ref, dz_ref, dbias_ref):
        z = f_ref[...] + b_ref[...]
        dlog = _seq_scan(da_ref[...] + db_ref[...], reverse=True)
        dz = dlog * jnp.exp(-_softplus(z))
        dz = jnp.where(_iota(dz.shape, 1) < n_heads, dz, 0.0)
        dz_ref[...] = dz.astype(BF16)
        dbias_ref[...] = jnp.sum(dz, axis=0, keepdims=True)

    full = pl.BlockSpec((s_len, LANES), lambda i: (0, 0))
    vec = pl.BlockSpec((1, LANES), lambda i: (0, 0))
    return pl.pallas_call(
        body, name=name, grid=(1,),
        in_specs=[pl.BlockSpec((s_len, LANES), lambda i: (0, fcol)), vec, full, full],
        out_specs=[full, vec], out_shape=[jax.ShapeDtypeStruct((s_len, LANES), BF16), jax.ShapeDtypeStruct((1, LANES), F32)],
        compiler_params=_params("arbitrary"))(qkvf, b_f, dcum_a, dcum_b)


def _rope_swap(x):
    half = MLA_ROPE // 2
    lane = _iota(x.shape, 1)
    sw = jnp.where(lane < half, pltpu.roll(x, LANES - half, axis=1), pltpu.roll(x, half, axis=1))
    return jnp.where(lane < MLA_ROPE, sw, 0.0)


def _mla_prep_fwd(qp, kv, c, gq, gk, cos_t, sin_t, *, n_heads, name, bs=512):
    s_len = qp.shape[0]
    bs = _div_block(s_len, bs, 8)
    krope_col = (MLA_Q_RANK + MLA_KV_RANK) // LANES

    def body(qn_ref, qr_ref, kn_ref, kr_ref, gq_ref, gk_ref, cos_ref, sin_ref, qc_ref, kc_ref):
        cos_v, sin_v = cos_ref[...], sin_ref[...]

        def rope(x, g):
            xv = x.astype(F32)
            inv = lax.rsqrt(jnp.sum(xv * xv, axis=-1, keepdims=True) / MLA_ROPE + NORM_EPS)
            y = xv * inv * g
            return y * cos_v + _rope_swap(y) * sin_v

        qc_ref[:, :LANES] = _head_norm(qn_ref[...], gq_ref[0]).astype(BF16)
        qc_ref[:, LANES:] = rope(qr_ref[...], gq_ref[1]).astype(BF16)
        kc_ref[:, :LANES] = _head_norm(kn_ref[...], gk_ref[0]).astype(BF16)
        kc_ref[:, LANES:] = rope(kr_ref[...], gk_ref[1]).astype(BF16)

    blk = lambda f: pl.BlockSpec((bs, LANES), f)
    gspec = pl.BlockSpec((2, 1, LANES), lambda i, h: (0, 0, 0))
    tspec = pl.BlockSpec((bs, LANES), lambda i, h: (i, 0))
    ospec = pl.BlockSpec((bs, 2 * LANES), lambda i, h: (i, h))
    oshape = jax.ShapeDtypeStruct((s_len, n_heads * 2 * LANES), BF16)
    return pl.pallas_call(
        body, name=name, grid=(s_len // bs, n_heads),
        in_specs=[blk(lambda i, h: (i, h)), blk(lambda i, h: (i, n_heads + h)), blk(lambda i, h: (i, 2 * h)),
                  blk(lambda i, h: (i, krope_col)), gspec, gspec, tspec, tspec],
        out_specs=[ospec, ospec], out_shape=[oshape, oshape],
        compiler_params=_params("parallel", "parallel"))(qp, qp, kv, c, gq, gk, cos_t, sin_t)


def _mla_prep_bwd(qp, kv, c, gq, gk, cos_t, sin_t, dqc, dkc, dv, *, n_heads, name, bs=512):
    s_len = qp.shape[0]
    bs = _div_block(s_len, bs, 8)
    krope_col = (MLA_Q_RANK + MLA_KV_RANK) // LANES

    def body(qn_ref, qr_ref, kn_ref, kr_ref, gq_ref, gk_ref, cos_ref, sin_ref, dqc_ref, dkc_ref, dv_ref,
             dqn_ref, dqr_ref, dkv_ref, dkr_ref, dgq_ref, dgk_ref):
        i, h = pl.program_id(0), pl.program_id(1)
        cos_v, sin_v = cos_ref[...], sin_ref[...]

        @pl.when((i == 0) & (h == 0))
        def _():
            dgq_ref[...] = jnp.zeros_like(dgq_ref)
            dgk_ref[...] = jnp.zeros_like(dgk_ref)

        @pl.when(h == 0)
        def _():
            dkr_ref[...] = jnp.zeros_like(dkr_ref)

        def unrope(dy):
            dy = dy.astype(F32)
            return dy * cos_v + _rope_swap(dy * sin_v)

        dqn, dg = _rms_bwd_math(qn_ref[...].astype(F32), gq_ref[0], dqc_ref[:, :LANES].astype(F32), MLA_NOPE)
        dgq_ref[0] += jnp.sum(dg, axis=0, keepdims=True)
        dqn_ref[...] = dqn.astype(BF16)
        dqr, dg = _rms_bwd_math(qr_ref[...].astype(F32), gq_ref[1], unrope(dqc_ref[:, LANES:]), MLA_ROPE)
        dgq_ref[1] += jnp.sum(dg, axis=0, keepdims=True)
        dqr_ref[...] = dqr.astype(BF16)
        dkn, dg = _rms_bwd_math(kn_ref[...].astype(F32), gk_ref[0], dkc_ref[:, :LANES].astype(F32), MLA_NOPE)
        dgk_ref[0] += jnp.sum(dg, axis=0, keepdims=True)
        dkv_ref[:, :LANES] = dkn.astype(BF16)
        dkv_ref[:, LANES:] = dv_ref[...]
        dkr, dg = _rms_bwd_math(kr_ref[...].astype(F32), gk_ref[1], unrope(dkc_ref[:, LANES:]), MLA_ROPE)
        dgk_ref[1] += jnp.sum(dg, axis=0, keepdims=True)
        dkr_ref[...] += dkr

    blk = lambda f: pl.BlockSpec((bs, LANES), f)
    gspec = pl.BlockSpec((2, 1, LANES), lambda i, h: (0, 0, 0))
    tspec = pl.BlockSpec((bs, LANES), lambda i, h: (i, 0))
    cat = pl.BlockSpec((bs, 2 * LANES), lambda i, h: (i, h))
    head = blk(lambda i, h: (i, h))
    hshape = jax.ShapeDtypeStruct((s_len, n_heads * LANES), BF16)
    gshape = jax.ShapeDtypeStruct((2, 1, LANES), F32)
    return pl.pallas_call(
        body, name=name, grid=(s_len // bs, n_heads),
        in_specs=[head, blk(lambda i, h: (i, n_heads + h)), blk(lambda i, h: (i, 2 * h)),
                  blk(lambda i, h: (i, krope_col)), gspec, gspec, tspec, tspec, cat, cat, head],
        out_specs=[head, head, cat, tspec, gspec, gspec],
        out_shape=[hshape, hshape, jax.ShapeDtypeStruct((s_len, n_heads * 2 * LANES), BF16),
                   jax.ShapeDtypeStruct((s_len, LANES), F32), gshape, gshape],
        compiler_params=_params("arbitrary", "arbitrary"))(qp, qp, kv, c, gq, gk, cos_t, sin_t, dqc, dkc, dv)


def _mla_latent_fwd(c, ga, *, name, br=256):
    s_len = c.shape[0]
    br = _div_block(s_len, br, 8)

    def body(c_ref, g_ref, o_ref):
        for part in range(2):
            sl = slice(part * MLA_Q_RANK, (part + 1) * MLA_Q_RANK)
            o_ref[:, sl] = _head_norm(c_ref[:, sl], g_ref[:, sl]).astype(BF16)

    w = MLA_Q_RANK + MLA_KV_RANK
    return pl.pallas_call(
        body, name=name, grid=(s_len // br,),
        in_specs=[pl.BlockSpec((br, w), lambda i: (i, 0)), pl.BlockSpec((1, w), lambda i: (0, 0))],
        out_specs=pl.BlockSpec((br, w), lambda i: (i, 0)), out_shape=jax.ShapeDtypeStruct((s_len, w), BF16),
        compiler_params=_params("parallel"))(c, ga)


def _mla_latent_bwd(c, ga, dcn_q, dcn_kv, dk_rope, *, name, br=256):
    s_len, cw = c.shape
    br = _div_block(s_len, br, 8)
    w = MLA_Q_RANK + MLA_KV_RANK

    def body(c_ref, g_ref, dq_ref, dkv_ref, dkr_ref, dc_ref, dg_ref):
        @pl.when(pl.program_id(0) == 0)
        def _():
            dg_ref[...] = jnp.zeros_like(dg_ref)

        for part, d_ref in enumerate((dq_ref, dkv_ref)):
            sl = slice(part * MLA_Q_RANK, (part + 1) * MLA_Q_RANK)
            dx, dg = _rms_bwd_math(c_ref[:, sl].astype(F32), g_ref[:, sl], d_ref[...].astype(F32), MLA_Q_RANK)
            dc_ref[:, sl] = dx.astype(BF16)
            dg_ref[:, sl] += jnp.sum(dg, axis=0, keepdims=True)
        dc_ref[:, w:] = dkr_ref[...].astype(BF16)

    return pl.pallas_call(
        body, name=name, grid=(s_len // br,),
        in_specs=[pl.BlockSpec((br, w), lambda i: (i, 0)), pl.BlockSpec((1, w), lambda i: (0, 0)),
                  pl.BlockSpec((br, MLA_Q_RANK), lambda i: (i, 0)), pl.BlockSpec((br, MLA_KV_RANK), lambda i: (i, 0)),
                  pl.BlockSpec((br, LANES), lambda i: (i, 0))],
        out_specs=[pl.BlockSpec((br, cw), lambda i: (i, 0)), pl.BlockSpec((1, w), lambda i: (0, 0))],
        out_shape=[jax.ShapeDtypeStruct((s_len, cw), BF16), jax.ShapeDtypeStruct((1, w), F32)],
        compiler_params=_params("arbitrary"))(c, ga, dcn_q, dcn_kv, dk_rope)


_GELU_C = 0.7978845608028654


def _gelu(x):
    return 0.5 * x * (1.0 + jnp.tanh(_GELU_C * (x + 0.044715 * x * x * x)))


def _gelu_grad(x):
    t = jnp.tanh(_GELU_C * (x + 0.044715 * x * x * x))
    return 0.5 * (1.0 + t) + 0.5 * x * (1.0 - t * t) * _GELU_C * (1.0 + 3 * 0.044715 * x * x)


def _sgu_act_fwd(uv, vg, *, name, br=256):
    s_len, w2 = uv.shape
    w = w2 // 2
    br = _div_block(s_len, br, 8)

    def body(uv_ref, g_ref, u_ref, v_ref):
        u_ref[...] = _gelu(uv_ref[:, :w])
        v_ref[...] = _head_norm(_gelu(uv_ref[:, w:]), g_ref[...]).astype(BF16)

    row = lambda c: pl.BlockSpec((br, c), lambda i: (i, 0))
    return pl.pallas_call(
        body, name=name, grid=(s_len // br,), in_specs=[row(w2), pl.BlockSpec((1, w), lambda i: (0, 0))],
        out_specs=[row(w), row(w)], out_shape=[jax.ShapeDtypeStruct((s_len, w), F32), jax.ShapeDtypeStruct((s_len, w), BF16)],
        compiler_params=_params("parallel"))(uv, vg)


def _sgu_act_bwd(uv, vg, du, dvn, *, name, br=256):
    s_len, w2 = uv.shape
    w = w2 // 2
    br = _div_block(s_len, br, 8)

    def body(uv_ref, g_ref, du_ref, dvn_ref, duv_ref, dg_ref):
        @pl.when(pl.program_id(0) == 0)
        def _():
            dg_ref[...] = jnp.zeros_like(dg_ref)

        up, vp = uv_ref[:, :w], uv_ref[:, w:]
        duv_ref[:, :w] = (du_ref[...] * _gelu_grad(up)).astype(BF16)
        dva, dg = _rms_bwd_math(_gelu(vp), g_ref[...], dvn_ref[...], w)
        dg_ref[...] += jnp.sum(dg, axis=0, keepdims=True)
        duv_ref[:, w:] = (dva * _gelu_grad(vp)).astype(BF16)

    row = lambda c: pl.BlockSpec((br, c), lambda i: (i, 0))
    vec = pl.BlockSpec((1, w), lambda i: (0, 0))
    return pl.pallas_call(
        body, name=name, grid=(s_len // br,), in_specs=[row(w2), vec, row(w), row(w)], out_specs=[row(w2), vec],
        out_shape=[jax.ShapeDtypeStruct((s_len, w2), BF16), jax.ShapeDtypeStruct((1, w), F32)],
        compiler_params=_params("arbitrary"))(uv, vg, du, dvn)


def _tril_weights(ws_ref):
    t, s = _iota((SGU_CHUNK, SGU_CHUNK), 0), _iota((SGU_CHUNK, SGU_CHUNK), 1)
    keep = s <= t
    return jnp.where(keep, ws_ref[...], 0.0), keep


def _sgu_mix_fwd(u, vn, w_s, b_s, *, name):
    s_len, w = u.shape
    nc = s_len // SGU_CHUNK

    def body(u_ref, v_ref, ws_ref, b_ref, o_ref):
        wm = _tril_weights(ws_ref)[0].astype(BF16)
        for n in range(nc):
            rows = slice(n * SGU_CHUNK, (n + 1) * SGU_CHUNK)
            mixed = _dot(wm, v_ref[rows, :], 1, 0) + b_ref[...]
            o_ref[rows, :] = (u_ref[rows, :] * mixed).astype(BF16)

    col = pl.BlockSpec((s_len, LANES), lambda g: (0, g))
    return pl.pallas_call(
        body, name=name, grid=(w // LANES,),
        in_specs=[col, col, pl.BlockSpec((None, SGU_CHUNK, SGU_CHUNK), lambda g: (g, 0, 0)),
                  pl.BlockSpec((None, SGU_CHUNK, 1), lambda g: (g, 0, 0))],
        out_specs=col, out_shape=jax.ShapeDtypeStruct((s_len, w), BF16),
        compiler_params=_params("parallel"))(u, vn, w_s, b_s)


def _sgu_mix_bwd(u, vn, w_s, b_s, dgated, *, name):
    s_len, w = u.shape
    nc = s_len // SGU_CHUNK

    def body(u_ref, v_ref, ws_ref, b_ref, dg_ref, du_ref, dv_ref, dws_ref, dbs_ref):
        wf, keep = _tril_weights(ws_ref)
        wm = wf.astype(BF16)
        wmt = wf.T.astype(BF16)
        dws = jnp.zeros((SGU_CHUNK, SGU_CHUNK), F32)
        dbs = jnp.zeros((SGU_CHUNK, 1), F32)
        for n in range(nc):
            rows = slice(n * SGU_CHUNK, (n + 1) * SGU_CHUNK)
            vb = v_ref[rows, :]
            dgv = dg_ref[rows, :].astype(F32)
            mixed = _dot(wm, vb, 1, 0) + b_ref[...]
            du_ref[rows, :] = dgv * mixed
            dm = dgv * u_ref[rows, :]
            dmb = dm.astype(BF16)
            dws = dws + _dot(dmb, vb, 1, 1)
            dbs = dbs + jnp.sum(dm, axis=1, keepdims=True)
            dv_ref[rows, :] = _dot(wmt, dmb, 1, 0)
        dws_ref[...] = jnp.where(keep, dws, 0.0)
        dbs_ref[...] = dbs

    col = pl.BlockSpec((s_len, LANES), lambda g: (0, g))
    wspec = pl.BlockSpec((None, SGU_CHUNK, SGU_CHUNK), lambda g: (g, 0, 0))
    bspec = pl.BlockSpec((None, SGU_CHUNK, 1), lambda g: (g, 0, 0))
    return pl.pallas_call(
        body, name=name, grid=(w // LANES,), in_specs=[col, col, wspec, bspec, col],
        out_specs=[col, col, wspec, bspec],
        out_shape=[jax.ShapeDtypeStruct((s_len, w), F32), jax.ShapeDtypeStruct((s_len, w), F32),
                   jax.ShapeDtypeStruct(w_s.shape, F32), jax.ShapeDtypeStruct(b_s.shape, F32)],
        compiler_params=_params("parallel"))(u, vn, w_s, b_s, dgated)


def _shift_down(x, k):
    if k == 0:
        return x
    return jnp.where(_iota(x.shape, 0) >= k, pltpu.roll(x, k, axis=0), 0.0)


def _shift_up(x, k):
    if k == 0:
        return x
    n = x.shape[0]
    return jnp.where(_iota(x.shape, 0) < n - k, pltpu.roll(x, n - k, axis=0), 0.0)


def _conv(u, w_ref, b_ref):
    return b_ref[...] + w_ref[0:1, :] * _shift_down(u, 2) + w_ref[1:2, :] * _shift_down(u, 1) + w_ref[2:3, :] * u


def _sigmoid(x):
    return 0.5 * jnp.tanh(0.5 * x) + 0.5


def _glu_fwd(up, cw, cb, *, name, bc=256):
    s_len, f2 = up.shape
    f = f2 // 2
    bc = _div_block(f, bc)
    nf = f // bc

    def body(ug_ref, uv_ref, wg_ref, wv_ref, bg_ref, bv_ref, o_ref):
        yg = _conv(ug_ref[...], wg_ref, bg_ref)
        yv = _conv(uv_ref[...], wv_ref, bv_ref)
        o_ref[...] = (yg * _sigmoid(yg) * yv).astype(BF16)

    big = lambda off: pl.BlockSpec((s_len, bc), lambda j: (0, j + off))
    wsp = lambda off: pl.BlockSpec((3, bc), lambda j: (0, j + off))
    bsp = lambda off: pl.BlockSpec((1, bc), lambda j: (0, j + off))
    return pl.pallas_call(
        body, name=name, grid=(nf,), in_specs=[big(0), big(nf), wsp(0), wsp(nf), bsp(0), bsp(nf)],
        out_specs=pl.BlockSpec((s_len, bc), lambda j: (0, j)), out_shape=jax.ShapeDtypeStruct((s_len, f), BF16),
        compiler_params=_params("parallel"))(up, up, cw, cw, cb, cb)


def _glu_bwd(up, cw, cb, dact, *, name, bc=256):
    s_len, f2 = up.shape
    f = f2 // 2
    bc = _div_block(f, bc)
    nf = f // bc

    def body(ug_ref, uv_ref, wg_ref, wv_ref, bg_ref, bv_ref, da_ref, du_ref, dw_ref, db_ref):
        ug, uv = ug_ref[...], uv_ref[...]
        yg = _conv(ug, wg_ref, bg_ref)
        yv = _conv(uv, wv_ref, bv_ref)
        da = da_ref[...].astype(F32)
        sg = _sigmoid(yg)
        planes = ((da * yv * (sg * (1.0 + yg * (1.0 - sg))), ug, wg_ref), (da * (yg * sg), uv, wv_ref))
        for plane, (dy, u, w_ref) in enumerate(planes):
            dy1, dy2 = _shift_up(dy, 1), _shift_up(dy, 2)
            db_ref[plane] = jnp.sum(dy, axis=0, keepdims=True)
            dw_ref[plane, 0:1, :] = jnp.sum(dy2 * u, axis=0, keepdims=True)
            dw_ref[plane, 1:2, :] = jnp.sum(dy1 * u, axis=0, keepdims=True)
            dw_ref[plane, 2:3, :] = jnp.sum(dy * u, axis=0, keepdims=True)
            du_ref[plane] = (w_ref[2:3, :] * dy + w_ref[1:2, :] * dy1 + w_ref[0:1, :] * dy2).astype(BF16)

    big = lambda off: pl.BlockSpec((s_len, bc), lambda j: (0, j + off))
    wsp = lambda off: pl.BlockSpec((3, bc), lambda j: (0, j + off))
    bsp = lambda off: pl.BlockSpec((1, bc), lambda j: (0, j + off))
    planes = lambda r: pl.BlockSpec((2, r, bc), lambda j: (0, 0, j))
    return pl.pallas_call(
        body, name=name, grid=(nf,),
        in_specs=[big(0), big(nf), wsp(0), wsp(nf), bsp(0), bsp(nf), pl.BlockSpec((s_len, bc), lambda j: (0, j))],
        out_specs=[planes(s_len), planes(3), planes(1)],
        out_shape=[jax.ShapeDtypeStruct((2, s_len, f), BF16), jax.ShapeDtypeStruct((2, 3, f), F32),
                   jax.ShapeDtypeStruct((2, 1, f), F32)],
        compiler_params=_params("parallel"))(up, up, cw, cw, cb, cb, dact)


def _as2d(a):
    return a.reshape(-1, a.shape[-1]) if a.ndim >= 2 else a.reshape(1, -1)


def _adamw(w, g, m, v, *, name, after=None, target_bytes=1 << 20):
    shape = w.shape
    w2, m2, v2 = _as2d(w), _as2d(m), _as2d(v)
    g2 = g.reshape(w2.shape)
    r, c = w2.shape
    br = r if r * c * 4 <= target_bytes else _div_block(r, max(8, target_bytes // (4 * c) // 8 * 8), 8)
    c1 = 1.0 - ADAM_B1 ** ADAM_STEP
    c2 = 1.0 - ADAM_B2 ** ADAM_STEP

    def body(w_ref, g_ref, m_ref, v_ref, *rest):
        d_ref, nm_ref, nv_ref = rest[-3:]
        gv = g_ref[...]
        nm = ADAM_B1 * m_ref[...] + (1.0 - ADAM_B1) * gv
        nv = ADAM_B2 * v_ref[...] + (1.0 - ADAM_B2) * (gv * gv)
        nm_ref[...] = nm
        nv_ref[...] = nv
        d_ref[...] = -ADAM_LR * ((nm / c1) / (jnp.sqrt(nv / c2) + ADAM_EPS) + ADAM_WD * w_ref[...])

    spec = pl.BlockSpec((br, c), lambda i: (i, 0))
    sds = jax.ShapeDtypeStruct((r, c), F32)
    d, nm, nv = pl.pallas_call(
        body, name=name, grid=(r // br,), in_specs=[spec] * 4 + _after_spec(after), out_specs=[spec] * 3,
        out_shape=[sds] * 3, compiler_params=_params("parallel"))(w2, g2, m2, v2, *_after_arg(after))
    return d.reshape(shape), nm.reshape(shape), nv.reshape(shape)


def _add_halves(g, recv, place, *, name, target_bytes=SHARD_BLOCK_BYTES):
    _, _, r, c = g.shape
    br = _div_block(r, max(16, target_bytes // (2 * c) // 16 * 16), 16)

    def body(x_ref, y_ref, c_ref, g_ref, r_ref, o_ref):
        o_ref[...] = (g_ref[...].astype(F32) + r_ref[...].astype(F32)).astype(BF16)

    return pl.pallas_call(
        body, name=name,
        grid_spec=pltpu.PrefetchScalarGridSpec(
            num_scalar_prefetch=3, grid=(N_CHIPS, r // br),
            in_specs=[pl.BlockSpec((None, None, br, c), lambda s, i, xr, yr, cr: (s, cr[0], i, 0)),
                      pl.BlockSpec((None, br, c), lambda s, i, xr, yr, cr: (s, i, 0))],
            out_specs=pl.BlockSpec((None, br, c), lambda s, i, xr, yr, cr: (s, i, 0))),
        out_shape=jax.ShapeDtypeStruct((N_CHIPS, r, c), BF16),
        compiler_params=_params("parallel", "parallel"))(*place, g, recv)


def _sum_chips(p, landed, place, *, name, target_bytes=SHARD_BLOCK_BYTES):
    _, r, c = p.shape
    br = _div_block(r, max(16, target_bytes // (4 * c) // 16 * 16), 16)

    def body(x_ref, y_ref, c_ref, p_ref, l1_ref, l2_ref, l3_ref, o_ref):
        o_ref[...] = ((p_ref[...].astype(F32) + l1_ref[...].astype(F32)) + l2_ref[...].astype(F32)) + l3_ref[...].astype(F32)

    slot = lambda k: pl.BlockSpec((None, br, c), lambda i, xr, yr, cr: ((2 * xr[0] + yr[0] + k) % N_CHIPS, i, 0))
    return pl.pallas_call(
        body, name=name,
        grid_spec=pltpu.PrefetchScalarGridSpec(
            num_scalar_prefetch=3, grid=(r // br,), in_specs=[slot(0), slot(1), slot(2), slot(3)],
            out_specs=pl.BlockSpec((None, br, c), lambda i, xr, yr, cr: (cr[0], i, 0))),
        out_shape=jax.ShapeDtypeStruct((2, r, c), F32),
        compiler_params=_params("parallel"))(*place, p, landed, landed, landed)


def _place_shard(w, place, *, dtype, name, layer=None, after=None, target_bytes=SHARD_BLOCK_BYTES):
    r, c = w.shape[-2:]
    hr = r // 2
    mult = 16 if dtype == BF16 else 8
    br = _div_block(hr, max(mult, target_bytes // (4 * c) // mult * mult), mult)
    nb = hr // br

    def body(x_ref, y_ref, c_ref, w_ref, *rest):
        rest[-1][...] = w_ref[...].astype(dtype)

    if layer is None:
        w_spec = pl.BlockSpec((br, c), lambda h, i, xr, yr, cr: (h * nb + i, 0))
    else:
        w_spec = pl.BlockSpec((None, br, c), lambda h, i, xr, yr, cr: (layer, h * nb + i, 0))
    return pl.pallas_call(
        body, name=name,
        grid_spec=pltpu.PrefetchScalarGridSpec(
            num_scalar_prefetch=3, grid=(2, nb), in_specs=[w_spec] + _after_spec(after),
            out_specs=pl.BlockSpec((None, None, br, c), lambda h, i, xr, yr, cr: (2 * xr[0] + yr[0], h, i, 0))),
        out_shape=jax.ShapeDtypeStruct((N_CHIPS, 2, hr, c), dtype),
        compiler_params=_params("parallel", "parallel"))(*place, w, *_after_arg(after))


def _sum_devices(x, *, name):
    n, r, c = x.shape
    br = _div_block(r, 512, 8)

    def body(x_ref, o_ref):
        acc = x_ref[0]
        for s in range(1, n):
            acc = acc + x_ref[s]
        o_ref[...] = acc

    return pl.pallas_call(
        body, name=name, grid=(r // br,), in_specs=[pl.BlockSpec((n, br, c), lambda i: (0, i, 0))],
        out_specs=pl.BlockSpec((br, c), lambda i: (i, 0)), out_shape=jax.ShapeDtypeStruct((r, c), F32),
        compiler_params=_params("parallel"))(x)


_ANY = pl.BlockSpec(memory_space=pl.ANY)


def _place():
    x, y, c = lax.axis_index("x"), lax.axis_index("y"), lax.axis_index("c")
    other_chips = [(1 - x, y), (x, 1 - y), (1 - x, 1 - y)]
    return x, y, c, other_chips


_HBM = pl.BlockSpec(memory_space=pltpu.HBM)
_SEM = pl.BlockSpec(memory_space=pltpu.SEMAPHORE)
_EFFECT = pltpu.SideEffectType.DATAFLOW_SIDE_EFFECTING


def _in_hbm(a):
    return pltpu.with_memory_space_constraint(a, pltpu.HBM)


def _token_spec():
    return pl.BlockSpec(memory_space=pltpu.VMEM), jax.ShapeDtypeStruct((8, LANES), F32)


def _gather_ici_start(bufs, after, *, name):
    n = len(bufs)

    def body(*refs):
        b_refs = refs[:n]
        send_sems, recv_sems = refs[n + 1], refs[n + 2]
        token = refs[-1]
        x, y, c, chips = _place()
        me = 2 * x + y
        for i in range(n):
            for j, (px, py) in enumerate(chips):
                pltpu.make_async_remote_copy(src_ref=b_refs[i].at[me, c], dst_ref=b_refs[i].at[me, c],
                                             send_sem=send_sems.at[3 * i + j], recv_sem=recv_sems.at[3 * i + j],
                                             device_id=(px, py, c), device_id_type=MESH).start()
        token[...] = jnp.zeros_like(token)

    tspec, tshape = _token_spec()
    outs = pl.pallas_call(
        body, name=name, in_specs=[_HBM] * n + [_ANY], out_specs=(_SEM, _SEM, *[_HBM] * n, tspec),
        out_shape=(pltpu.SemaphoreType.DMA((3 * n,)), pltpu.SemaphoreType.DMA((3 * n,)),
                   *[pltpu.HBM(a.shape, a.dtype) for a in bufs], tshape),
        input_output_aliases={i: 2 + i for i in range(n)},
        compiler_params=pltpu.CompilerParams(has_side_effects=_EFFECT),
    )(*[_in_hbm(a) for a in bufs], after)
    return outs[0], outs[1], list(outs[2:2 + n]), outs[-1]


def _gather_ici_wait(send_sems, recv_sems, bufs, after, *, name):
    n = len(bufs)

    def body(*refs):
        b_refs = refs[:n]
        send_sems, recv_sems = refs[n], refs[n + 1]
        x, y, c, chips = _place()
        me = 2 * x + y
        for i in range(n):
            for j, (px, py) in enumerate(chips):
                cp = pltpu.make_async_remote_copy(src_ref=b_refs[i].at[me, c], dst_ref=b_refs[i].at[2 * px + py, c],
                                                  send_sem=send_sems.at[3 * i + j], recv_sem=recv_sems.at[3 * i + j],
                                                  device_id=(px, py, c), device_id_type=MESH)
                cp.wait_send()
                cp.wait_recv()

    outs = pl.pallas_call(
        body, name=name, in_specs=[_HBM] * n + [_SEM, _SEM, _ANY], out_specs=[_HBM] * n,
        out_shape=[pltpu.HBM(a.shape, a.dtype) for a in bufs], input_output_aliases={i: i for i in range(n)},
        compiler_params=pltpu.CompilerParams(has_side_effects=_EFFECT),
    )(*bufs, send_sems, recv_sems, after)
    return list(outs)


def _gather_d2d(bufs, *, name):
    n = len(bufs)

    def body(*refs):
        b_refs = refs[n:2 * n]
        send_sems, recv_sems = refs[2 * n:]
        x, y, c, chips = _place()
        sends = []
        for i in range(n):
            for j, (px, py) in enumerate(chips):
                mine = b_refs[i].at[2 * px + py, c]
                cp = pltpu.make_async_remote_copy(src_ref=mine, dst_ref=mine, send_sem=send_sems.at[3 * i + j],
                                                  recv_sem=recv_sems.at[3 * i + j], device_id=(x, y, 1 - c),
                                                  device_id_type=MESH)
                cp.start()
                sends.append((cp, i, j, px, py))
        for cp, i, j, px, py in sends:
            theirs = b_refs[i].at[2 * px + py, 1 - c]
            pltpu.make_async_remote_copy(src_ref=theirs, dst_ref=theirs, send_sem=send_sems.at[3 * i + j],
                                         recv_sem=recv_sems.at[3 * i + j], device_id=(x, y, 1 - c),
                                         device_id_type=MESH).wait_recv()
            cp.wait_send()

    return pl.pallas_call(
        body, name=name, in_specs=[_ANY] * n, out_specs=[_ANY] * n, input_output_aliases={i: i for i in range(n)},
        out_shape=[jax.ShapeDtypeStruct(a.shape, a.dtype) for a in bufs],
        scratch_shapes=[pltpu.SemaphoreType.DMA((3 * n,)), pltpu.SemaphoreType.DMA((3 * n,))],
    )(*bufs)


def _sibling_halves_start(gs, after, *, name):
    n = len(gs)

    def body(*refs):
        g_refs, r_refs = refs[:n], refs[n:2 * n]
        send_sems, recv_sems = refs[2 * n + 1], refs[2 * n + 2]
        token = refs[-1]
        x, y, c, _ = _place()
        for i in range(n):
            for s in range(N_CHIPS):
                k = i * N_CHIPS + s
                pltpu.make_async_remote_copy(src_ref=g_refs[i].at[s, 1 - c], dst_ref=r_refs[i].at[s],
                                             send_sem=send_sems.at[k], recv_sem=recv_sems.at[k],
                                             device_id=(x, y, 1 - c), device_id_type=MESH).start()
        token[...] = jnp.zeros_like(token)

    tspec, tshape = _token_spec()
    rshapes = [(N_CHIPS,) + g.shape[2:] for g in gs]
    recvs = [_in_hbm(lax.empty(sh, g.dtype)) for sh, g in zip(rshapes, gs)]
    outs = pl.pallas_call(
        body, name=name, in_specs=[_HBM] * (2 * n) + [_ANY], out_specs=(_SEM, _SEM, *[_HBM] * (2 * n), tspec),
        out_shape=(pltpu.SemaphoreType.DMA((N_CHIPS * n,)), pltpu.SemaphoreType.DMA((N_CHIPS * n,)),
                   *[pltpu.HBM(g.shape, g.dtype) for g in gs], *[pltpu.HBM(sh, g.dtype) for sh, g in zip(rshapes, gs)],
                   tshape),
        input_output_aliases={i: 2 + i for i in range(2 * n)},
        compiler_params=pltpu.CompilerParams(has_side_effects=_EFFECT),
    )(*[_in_hbm(g) for g in gs], *recvs, after)
    return outs[0], outs[1], list(outs[2:2 + n]), list(outs[2 + n:2 + 2 * n]), outs[-1]


def _sibling_halves_wait(send_sems, recv_sems, gs, recvs, after, *, name):
    n = len(gs)

    def body(*refs):
        g_refs, r_refs = refs[:n], refs[n:2 * n]
        send_sems, recv_sems = refs[2 * n], refs[2 * n + 1]
        x, y, c, _ = _place()
        for i in range(n):
            for s in range(N_CHIPS):
                k = i * N_CHIPS + s
                cp = pltpu.make_async_remote_copy(src_ref=g_refs[i].at[s, 1 - c], dst_ref=r_refs[i].at[s],
                                                  send_sem=send_sems.at[k], recv_sem=recv_sems.at[k],
                                                  device_id=(x, y, 1 - c), device_id_type=MESH)
                cp.wait_send()
                cp.wait_recv()

    outs = pl.pallas_call(
        body, name=name, in_specs=[_HBM] * (2 * n) + [_SEM, _SEM, _ANY], out_specs=[_HBM] * (2 * n),
        out_shape=[pltpu.HBM(a.shape, a.dtype) for a in list(gs) + list(recvs)],
        input_output_aliases={i: i for i in range(2 * n)},
        compiler_params=pltpu.CompilerParams(has_side_effects=_EFFECT),
    )(*gs, *recvs, send_sems, recv_sems, after)
    return list(outs[:n]), list(outs[n:])


def _chip_scatter_start(ps, after, *, name):
    n = len(ps)

    def body(*refs):
        p_refs, l_refs = refs[:n], refs[n:2 * n]
        send_sems, recv_sems = refs[2 * n + 1], refs[2 * n + 2]
        token = refs[-1]
        x, y, c, chips = _place()
        me = 2 * x + y
        for i in range(n):
            for j, (px, py) in enumerate(chips):
                pltpu.make_async_remote_copy(src_ref=p_refs[i].at[2 * px + py], dst_ref=l_refs[i].at[me],
                                             send_sem=send_sems.at[3 * i + j], recv_sem=recv_sems.at[3 * i + j],
                                             device_id=(px, py, c), device_id_type=MESH).start()
        token[...] = jnp.zeros_like(token)

    tspec, tshape = _token_spec()
    lands = [_in_hbm(lax.empty(p.shape, p.dtype)) for p in ps]
    outs = pl.pallas_call(
        body, name=name, in_specs=[_HBM] * (2 * n) + [_ANY], out_specs=(_SEM, _SEM, *[_HBM] * (2 * n), tspec),
        out_shape=(pltpu.SemaphoreType.DMA((3 * n,)), pltpu.SemaphoreType.DMA((3 * n,)),
                   *[pltpu.HBM(p.shape, p.dtype) for p in ps], *[pltpu.HBM(p.shape, p.dtype) for p in ps], tshape),
        input_output_aliases={i: 2 + i for i in range(2 * n)},
        compiler_params=pltpu.CompilerParams(has_side_effects=_EFFECT),
    )(*[_in_hbm(p) for p in ps], *lands, after)
    return outs[0], outs[1], list(outs[2:2 + n]), list(outs[2 + n:2 + 2 * n]), outs[-1]


def _chip_scatter_wait(send_sems, recv_sems, ps, lands, after, *, name):
    n = len(ps)

    def body(*refs):
        p_refs, l_refs = refs[:n], refs[n:2 * n]
        send_sems, recv_sems = refs[2 * n], refs[2 * n + 1]
        x, y, c, chips = _place()
        for i in range(n):
            for j, (px, py) in enumerate(chips):
                cp = pltpu.make_async_remote_copy(src_ref=p_refs[i].at[2 * px + py], dst_ref=l_refs[i].at[2 * px + py],
                                                  send_sem=send_sems.at[3 * i + j], recv_sem=recv_sems.at[3 * i + j],
                                                  device_id=(px, py, c), device_id_type=MESH)
                cp.wait_send()
                cp.wait_recv()

    outs = pl.pallas_call(
        body, name=name, in_specs=[_HBM] * (2 * n) + [_SEM, _SEM, _ANY], out_specs=[_HBM] * (2 * n),
        out_shape=[pltpu.HBM(p.shape, p.dtype) for p in ps] * 2, input_output_aliases={i: i for i in range(2 * n)},
        compiler_params=pltpu.CompilerParams(has_side_effects=_EFFECT),
    )(*ps, *lands, send_sems, recv_sems, after)
    return list(outs[:n]), list(outs[n:])


def _sibling_share(bufs, *, name):
    n = len(bufs)

    def body(*refs):
        b_refs = refs[n:2 * n]
        send_sems, recv_sems = refs[2 * n:]
        x, y, c, _ = _place()
        copies = []
        for i in range(n):
            cp = pltpu.make_async_remote_copy(src_ref=b_refs[i].at[c], dst_ref=b_refs[i].at[c], send_sem=send_sems.at[i],
                                              recv_sem=recv_sems.at[i], device_id=(x, y, 1 - c), device_id_type=MESH)
            cp.start()
            copies.append((cp, i))
        for cp, i in copies:
            theirs = b_refs[i].at[1 - c]
            pltpu.make_async_remote_copy(src_ref=theirs, dst_ref=theirs, send_sem=send_sems.at[i],
                                         recv_sem=recv_sems.at[i], device_id=(x, y, 1 - c),
                                         device_id_type=MESH).wait_recv()
            cp.wait_send()

    return pl.pallas_call(
        body, name=name, in_specs=[_ANY] * n, out_specs=[_ANY] * n, input_output_aliases={i: i for i in range(n)},
        out_shape=[jax.ShapeDtypeStruct(a.shape, a.dtype) for a in bufs],
        scratch_shapes=[pltpu.SemaphoreType.DMA((n,)), pltpu.SemaphoreType.DMA((n,))],
    )(*bufs)


def _broadcast_all(v, after, *, name):
    def body(v_ref, after_ref, o_ref, send_sems, recv_sems, local_sem):
        x, y, c, _ = _place()
        me = 4 * x + 2 * y + c
        loc = pltpu.make_async_copy(v_ref, o_ref.at[me], local_sem)
        loc.start()
        copies = []
        for k in range(1, 8):
            dx, dy, dc = (k >> 2) & 1, (k >> 1) & 1, k & 1
            to = (1 - x if dx else x, 1 - y if dy else y, 1 - c if dc else c)
            cp = pltpu.make_async_remote_copy(src_ref=v_ref, dst_ref=o_ref.at[me], send_sem=send_sems.at[k - 1],
                                              recv_sem=recv_sems.at[k - 1], device_id=to, device_id_type=MESH)
            cp.start()
            copies.append((cp, k, to))
        for cp, k, to in copies:
            cp.wait_send()
            theirs = o_ref.at[4 * to[0] + 2 * to[1] + to[2]]
            pltpu.make_async_remote_copy(src_ref=theirs, dst_ref=theirs, send_sem=send_sems.at[k - 1],
                                         recv_sem=recv_sems.at[k - 1], device_id=to, device_id_type=MESH).wait_recv()
        loc.wait()

    return pl.pallas_call(
        body, name=name, in_specs=[_ANY, _ANY], out_specs=_ANY,
        out_shape=jax.ShapeDtypeStruct((8,) + v.shape, v.dtype),
        scratch_shapes=[pltpu.SemaphoreType.DMA((7,)), pltpu.SemaphoreType.DMA((7,)), pltpu.SemaphoreType.DMA(())],
    )(v, after)


def _gather_place(shards, place, *, name, after=None):
    names = list(shards)
    bufs, shapes = [], []
    for k in names:
        w, layer = shards[k] if isinstance(shards[k], tuple) else (shards[k], None)
        bufs.append(_place_shard(w, place, dtype=F32 if k == 'small' else BF16, layer=layer, after=after,
                                 name=f"{name}_place_{k}"))
        after = bufs[-1] if after is not None else None
        shapes.append(w.shape[-2:])
    return names, shapes, bufs


def _gather_begin(placed, after, *, name):
    names, shapes, bufs = placed
    send_sems, recv_sems, bufs, token = _gather_ici_start(bufs, after, name=name + "_ici_start")
    return (names, shapes, send_sems, recv_sems, bufs), token


def _gather_end(state, after, *, name):
    names, shapes, send_sems, recv_sems, bufs = state
    bufs = _gather_ici_wait(send_sems, recv_sems, bufs, after, name=name + "_ici_wait")
    bufs = _gather_d2d(bufs, name=name + "_d2d")
    return {k: o.reshape((N_CHIPS,) + sh) for k, o, sh in zip(names, bufs, shapes)}


def _reduce_sibling_start(grads, after, *, name):
    names = list(grads)
    gs = [grads[k].reshape(N_CHIPS, 2, grads[k].shape[1] // 2, grads[k].shape[2]) for k in names]
    send_sems, recv_sems, gs, recvs, token = _sibling_halves_start(gs, after, name=name + "_sib_start")
    return (names, [grads[k].shape[1:] for k in names], send_sems, recv_sems, gs, recvs), token


def _reduce_begin(state, place, after, *, name):
    names, shapes, send_sems, recv_sems, gs, recvs = state
    gs, recvs = _sibling_halves_wait(send_sems, recv_sems, gs, recvs, after, name=name + "_sib_wait")
    ps = [_add_halves(g, r, place, name=f"{name}_add2_{k}") for g, r, k in zip(gs, recvs, names)]
    send_sems, recv_sems, ps, lands, token = _chip_scatter_start(ps, recvs[0], name=name + "_scatter_start")
    return (names, shapes, send_sems, recv_sems, ps, lands), token


def _reduce_end(state, place, after, *, name):
    names, shapes, send_sems, recv_sems, ps, lands = state
    ps, lands = _chip_scatter_wait(send_sems, recv_sems, ps, lands, after, name=name + "_scatter_wait")
    rs = [_sum_chips(p, l, place, name=f"{name}_sum4_{k}") for p, l, k in zip(ps, lands, names)]
    both = _sibling_share(rs, name=name + "_share")
    return {k: b.reshape(sh) for k, b, sh in zip(names, both, shapes)}


def _pad_lanes(a, n=LANES):
    return jnp.pad(a, [(0, 0)] * (a.ndim - 1) + [(0, n - a.shape[-1])])


def _unshard_cols(g):
    return jnp.transpose(g, (1, 0, 2)).reshape(g.shape[1], -1)


def _shard_cols(w):
    k, n = w.shape
    return jnp.transpose(w.reshape(k, N_CHIPS, n // N_CHIPS), (1, 0, 2))


def _ffn_fwd(h, p, tag):
    b = _rms_fwd(h, p['ffn_norm'], name=f"{tag}_ffn_norm")
    up = _mm(b, p['ffn_w_up'], b_sh='n', name=f"{tag}_ffn_up", bn=1408)
    act = _glu_fwd(up, p['ffn_conv_w'], p['ffn_conv_b'], name=f"{tag}_ffn_glu")
    out = _mm(act, p['ffn_w_down'], res=h, name=f"{tag}_ffn_down", bk=704)
    return out, (h, b, up, act)


def _ffn_bwd(dh, saved, p, tag, after, on_big):
    h, b, up, act = saved
    dact = _mm(dh, p['ffn_w_down'], tb=True, after=after, out_dtype=BF16, name=f"{tag}_ffn_dact", bn=1408)
    dw_down = _mm(act, dh, ta=True, after=after, out_dtype=BF16, name=f"{tag}_ffn_dwdown", bm=1408)
    dup, dcw, dcb = _glu_bwd(up, p['ffn_conv_w'], p['ffn_conv_b'], dact, name=f"{tag}_ffn_dglu")
    dw_up = _mm(b, dup, ta=True, b_sh='n', o_sh=True, out_dtype=BF16, name=f"{tag}_ffn_dwup", bn=1408)
    sent = on_big({'ffn_w_up': dw_up, 'ffn_w_down': dw_down.reshape(N_CHIPS, -1, dw_down.shape[1])})
    db = _mm(dup, p['ffn_w_up'], a_sh=True, b_sh='k', after=sent, name=f"{tag}_ffn_db", bk=1408)
    dcw = jnp.transpose(dcw, (1, 0, 2)).reshape(dcw.shape[1], -1)
    dcb = dcb.reshape(1, -1)
    dh_in, dg = _rms_bwd(h, p['ffn_norm'], db, res=dh, name=f"{tag}_ffn_dnorm")
    small = {'ffn_norm': dg, 'ffn_conv_w': dcw, 'ffn_conv_b': dcb}
    return dh_in, small


def _qkv_attn_fwd(kind, h, p, tag, n_heads):
    a = _rms_fwd(h, p['mix_norm'], name=f"{tag}_norm")
    if kind == 'fox':
        qkv = _mm(a, p['w_in'], name=f"{tag}_qkv", bn=896)
        cum = _fgate_fwd(qkv, p['b_f'], fcol=3 * n_heads, name=f"{tag}_fgate")
        cum_t = cum[:, :n_heads].T
        cq, ck = cum_t[:, :, None], cum_t[:, None, :]
    else:
        qkv = _mm(a, p['w_in'], b_sh='n', name=f"{tag}_qkv", bn=768)
        cq = ck = None
    cols = dict(qcol=lambda hh: hh, kcol=lambda hh: n_heads + hh, vcol=lambda hh: 2 * n_heads + hh)
    o = _attn_fwd(kind, qkv, qkv, qkv, name=f"{tag}_attn", n_heads=n_heads, dqk=HEAD_DIM, scale=HEAD_DIM ** -0.5,
                  gains=p['qk_gain'], cq=cq, ck=ck, **cols)
    out = _mm(o, p['w_out'], res=h, name=f"{tag}_out")
    return out, (h, a, qkv, o, cq, ck)


def _qkv_attn_bwd(kind, dh, saved, p, tag, n_heads, after, on_big):
    h, a, qkv, o, cq, ck = saved
    do = _mm(dh, p['w_out'], tb=True, after=after, out_dtype=BF16, name=f"{tag}_do")
    dw_out = _mm(o, dh, ta=True, after=after, out_dtype=BF16, name=f"{tag}_dwout")
    cols = dict(qcol=lambda hh: hh, kcol=lambda hh: n_heads + hh, vcol=lambda hh: 2 * n_heads + hh)
    outs = _attn_bwd(kind, qkv, qkv, qkv, o, do, name=f"{tag}_dattn", n_heads=n_heads, dqk=HEAD_DIM,
                     scale=HEAD_DIM ** -0.5, gains=p['qk_gain'], cq=cq, ck=ck, **cols)
    dq, dk, dv, dgain = outs[:4]
    small = {'q_gain': dgain[0], 'k_gain': dgain[1]}
    if kind == 'fox':
        dcq, dck = outs[4:]
        dca = _pad_lanes(dcq[:, :, 0].T)
        dcb = _pad_lanes(dck[:, 0, :].T)
        dflog, dbf = _fgate_bwd(qkv, p['b_f'], dca, dcb, fcol=3 * n_heads, n_heads=n_heads, name=f"{tag}_dfgate")
        small['b_f'] = dbf[:, :n_heads]
        dqkv = jnp.concatenate([dq, dk, dv, dflog], axis=1)
        dw_in = _mm(a, dqkv, ta=True, out_dtype=BF16, name=f"{tag}_dwin", bn=896)
        dw_in = _shard_cols(dw_in[:, :3 * n_heads * HEAD_DIM + n_heads])
    else:
        dqkv = jnp.concatenate([dq, dk, dv], axis=1)
        dw_in = _mm(a, dqkv, ta=True, o_sh=True, out_dtype=BF16, name=f"{tag}_dwin", bn=768)
    sent = on_big({'w_in': dw_in, 'w_out': dw_out.reshape(N_CHIPS, -1, dw_out.shape[1])})
    if kind == 'fox':
        da = _mm(dqkv, p['w_in'], tb=True, after=sent, name=f"{tag}_da", bk=896)
    else:
        da = _mm(dqkv, p['w_in'], b_sh='k', after=sent, name=f"{tag}_da", bk=768)
    dh_in, dg = _rms_bwd(h, p['mix_norm'], da, res=dh, name=f"{tag}_dnorm")
    small['mix_norm'] = dg
    return dh_in, small


def _mla_fwd(h, p, tag, n_heads):
    a = _rms_fwd(h, p['mix_norm'], name=f"{tag}_norm")
    c = _mm(a, p['w_in'], name=f"{tag}_latent", bn=1152)
    cn = _mla_latent_fwd(c, p['a_gain'], name=f"{tag}_latent_norm")
    qp = _mm(cn[:, :MLA_Q_RANK], p['w_q_b'], name=f"{tag}_q_up")
    kv = _mm(cn[:, MLA_Q_RANK:], p['w_kv_b'], b_sh='n', name=f"{tag}_kv_up")
    qc, kc = _mla_prep_fwd(qp, kv, c, p['gq'], p['gk'], p['cos'], p['sin'], n_heads=n_heads, name=f"{tag}_prep")
    cols = dict(qcol=lambda hh: hh, kcol=lambda hh: hh, vcol=lambda hh: 2 * hh + 1)
    scale = (MLA_NOPE + MLA_ROPE) ** -0.5
    o = _attn_fwd('mla', qc, kc, kv, name=f"{tag}_attn", n_heads=n_heads, dqk=2 * LANES, scale=scale, **cols)
    out = _mm(o, p['w_out'], res=h, name=f"{tag}_out")
    return out, (h, a, c, cn, qp, kv, qc, kc, o)


def _mla_bwd(dh, saved, p, tag, n_heads, after, on_big):
    h, a, c, cn, qp, kv, qc, kc, o = saved
    do = _mm(dh, p['w_out'], tb=True, after=after, out_dtype=BF16, name=f"{tag}_do")
    dw_out = _mm(o, dh, ta=True, after=after, out_dtype=BF16, name=f"{tag}_dwout")
    cols = dict(qcol=lambda hh: hh, kcol=lambda hh: hh, vcol=lambda hh: 2 * hh + 1)
    scale = (MLA_NOPE + MLA_ROPE) ** -0.5
    dqc, dkc, dv = _attn_bwd('mla', qc, kc, kv, o, do, name=f"{tag}_dattn", n_heads=n_heads, dqk=2 * LANES,
                             scale=scale, **cols)
    dqn, dqr, dkv, dkr, dgq, dgk = _mla_prep_bwd(qp, kv, c, p['gq'], p['gk'], p['cos'], p['sin'], dqc, dkc, dv,
                                                 n_heads=n_heads, name=f"{tag}_dprep")
    dqp = jnp.concatenate([dqn, dqr], axis=1)
    cn_q, cn_kv = cn[:, :MLA_Q_RANK], cn[:, MLA_Q_RANK:]
    dw_q_b = _mm(cn_q, dqp, ta=True, out_dtype=BF16, name=f"{tag}_dwqb", bm=512)
    dcn_q = _mm(dqp, p['w_q_b'], tb=True, out_dtype=BF16, name=f"{tag}_dcnq")
    dw_kv_b = _mm(cn_kv, dkv, ta=True, o_sh=True, out_dtype=BF16, name=f"{tag}_dwkvb", bm=512)
    dcn_kv = _mm(dkv, p['w_kv_b'], b_sh='k', out_dtype=BF16, name=f"{tag}_dcnkv")
    dc, dga = _mla_latent_bwd(c, p['a_gain'], dcn_q, dcn_kv, dkr, name=f"{tag}_dlatent")
    dw_in = _mm(a, dc, ta=True, out_dtype=BF16, name=f"{tag}_dwin", bn=1152)
    k_rank = dw_q_b.shape[0]
    nope = dw_q_b[:, :n_heads * LANES].reshape(k_rank, n_heads, LANES)
    rope = dw_q_b[:, n_heads * LANES:].reshape(k_rank, n_heads, LANES)[:, :, :MLA_ROPE]
    dw_q_b = jnp.concatenate([nope, rope], axis=2).reshape(k_rank, n_heads * (MLA_NOPE + MLA_ROPE))
    w_in_cols = MLA_Q_RANK + MLA_KV_RANK + MLA_ROPE
    sent = on_big({'w_in': dw_in[:, :w_in_cols].reshape(N_CHIPS, -1, w_in_cols), 'w_q_b': _shard_cols(dw_q_b),
                   'w_kv_b': dw_kv_b, 'w_out': dw_out.reshape(N_CHIPS, -1, dw_out.shape[1])})
    da = _mm(dc, p['w_in'], tb=True, after=sent, name=f"{tag}_da", bk=1152)
    dh_in, dg = _rms_bwd(h, p['mix_norm'], da, res=dh, name=f"{tag}_dnorm")
    small = {'mix_norm': dg, 'q_a_gain': dga[:, :MLA_Q_RANK], 'kv_a_gain': dga[:, MLA_Q_RANK:],
             'q_gain': jnp.concatenate([dgq[0], dgq[1][:, :MLA_ROPE]], axis=1),
             'k_gain': jnp.concatenate([dgk[0], dgk[1][:, :MLA_ROPE]], axis=1)}
    return dh_in, small


def _sgu_fwd(h, p, tag):
    a = _rms_fwd(h, p['mix_norm'], name=f"{tag}_norm")
    uv = _mm(a, p['w_in'], b_sh='n', name=f"{tag}_in")
    u, vn = _sgu_act_fwd(uv, p['v_gain'], name=f"{tag}_act")
    gated = _sgu_mix_fwd(u, vn, p['w_s'], p['b_s'], name=f"{tag}_mix")
    out = _mm(gated, p['w_out'], res=h, name=f"{tag}_out")
    return out, (h, a, uv, u, vn, gated)


def _sgu_bwd(dh, saved, p, tag, after, on_big):
    h, a, uv, u, vn, gated = saved
    dgated = _mm(dh, p['w_out'], tb=True, after=after, out_dtype=BF16, name=f"{tag}_dgated")
    dw_out = _mm(gated, dh, ta=True, after=after, out_dtype=BF16, name=f"{tag}_dwout")
    du, dvn, dws, dbs = _sgu_mix_bwd(u, vn, p['w_s'], p['b_s'], dgated, name=f"{tag}_dmix")
    duv, dvg = _sgu_act_bwd(uv, p['v_gain'], du, dvn, name=f"{tag}_dact")
    dw_in = _mm(a, duv, ta=True, o_sh=True, out_dtype=BF16, name=f"{tag}_dwin")
    sent = on_big({'w_in': dw_in, 'w_out': dw_out.reshape(N_CHIPS, -1, dw_out.shape[1])})
    da = _mm(duv, p['w_in'], b_sh='k', after=sent, name=f"{tag}_da")
    dh_in, dg = _rms_bwd(h, p['mix_norm'], da, res=dh, name=f"{tag}_dnorm")
    small = {'mix_norm': dg, 'v_gain': dvg, 'w_s': dws, 'b_s': dbs[:, :, 0]}
    return dh_in, small


def _pack(parts):
    flat = jnp.concatenate([p.reshape(-1).astype(F32) for p in parts])
    rows = -(-flat.shape[0] // LANES)
    rows = -(-rows // 32) * 32
    return jnp.pad(flat, (0, rows * LANES - flat.shape[0])).reshape(rows, LANES)


def _unpack(packed, shapes):
    flat = packed.reshape(-1)
    out, off = [], 0
    for s in shapes:
        n = 1
        for d in s:
            n *= d
        out.append(flat[off:off + n].reshape(s))
        off += n
    return out


MIXERS = ('fox', 'mla', 'sb', 'sgu')
WEIGHT_NAMES = ['mix_norm', 'ffn_norm', 'fox_w_in', 'fox_b_f', 'fox_q_gain', 'fox_k_gain', 'fox_w_out', 'mla_w_in',
                'mla_q_a_gain', 'mla_kv_a_gain', 'mla_w_q_b', 'mla_w_kv_b', 'mla_q_gain', 'mla_k_gain', 'mla_w_out',
                'sb_w_in', 'sb_q_gain', 'sb_k_gain', 'sb_w_out', 'sgu_w_in', 'sgu_v_gain', 'sgu_w_s', 'sgu_b_s',
                'sgu_w_out', 'ffn_w_up', 'ffn_conv_w', 'ffn_conv_b', 'ffn_w_down']
SMALL_SHARDED = {'mla_q_a_gain': 1, 'mla_kv_a_gain': 1, 'sgu_v_gain': 1, 'ffn_conv_w': 2}
BIG = ['fox_w_in', 'fox_w_out', 'mla_w_in', 'mla_w_q_b', 'mla_w_kv_b', 'mla_w_out', 'sb_w_in', 'sb_w_out', 'sgu_w_in',
       'sgu_w_out', 'ffn_w_up', 'ffn_w_down']


def kernel(x, positions, mix_norm, ffn_norm, fox_w_in, fox_b_f, fox_q_gain, fox_k_gain, fox_w_out, mla_w_in, mla_q_a_gain, mla_kv_a_gain, mla_w_q_b, mla_w_kv_b, mla_q_gain, mla_k_gain, mla_w_out, sb_w_in, sb_q_gain, sb_k_gain, sb_w_out, sgu_w_in, sgu_v_gain, sgu_w_s, sgu_b_s, sgu_w_out, ffn_w_up, ffn_conv_w, ffn_conv_b, ffn_w_down, loss_target, m_mix_norm, m_ffn_norm, m_fox_w_in, m_fox_b_f, m_fox_q_gain, m_fox_k_gain, m_fox_w_out, m_mla_w_in, m_mla_q_a_gain, m_mla_kv_a_gain, m_mla_w_q_b, m_mla_w_kv_b, m_mla_q_gain, m_mla_k_gain, m_mla_w_out, m_sb_w_in, m_sb_q_gain, m_sb_k_gain, m_sb_w_out, m_sgu_w_in, m_sgu_v_gain, m_sgu_w_s, m_sgu_b_s, m_sgu_w_out, m_ffn_w_up, m_ffn_conv_w, m_ffn_conv_b, m_ffn_w_down, v_mix_norm, v_ffn_norm, v_fox_w_in, v_fox_b_f, v_fox_q_gain, v_fox_k_gain, v_fox_w_out, v_mla_w_in, v_mla_q_a_gain, v_mla_kv_a_gain, v_mla_w_q_b, v_mla_w_kv_b, v_mla_q_gain, v_mla_k_gain, v_mla_w_out, v_sb_w_in, v_sb_q_gain, v_sb_k_gain, v_sb_w_out, v_sgu_w_in, v_sgu_v_gain, v_sgu_w_s, v_sgu_b_s, v_sgu_w_out, v_ffn_w_up, v_ffn_conv_w, v_ffn_conv_b, v_ffn_w_down):
    args = dict(locals())
    W = {k: args[k] for k in WEIGHT_NAMES}
    M = {k: args['m_' + k] for k in WEIGHT_NAMES}
    V = {k: args['v_' + k] for k in WEIGHT_NAMES}
    depth = mix_norm.shape[0]
    s_len, d_model = x.shape[1], x.shape[2]
    n_heads = d_model // HEAD_DIM
    assert all(W[k].shape[0] == 1 for k in WEIGHT_NAMES if k.split('_')[0] in MIXERS), "one layer per mixer"
    xi, yi, ci = lax.axis_index("x"), lax.axis_index("y"), lax.axis_index("c")
    chip = 2 * xi + yi
    place = tuple(jnp.reshape(v, (1,)).astype(jnp.int32) for v in (xi, yi, ci))

    small_local = _pack([W[k][0] if k != 'ffn_conv_w' else W[k] for k in SMALL_SHARDED])

    def piece_shards(i, part):
        if part == 'ffn':
            return {'ffn_w_up': (W['ffn_w_up'], i), 'ffn_w_down': (W['ffn_w_down'], i)}
        mixer = MIXERS[i % len(MIXERS)]
        shards = {k: W[k][0] for k in BIG if k.startswith(mixer + '_')}
        if i == 0:
            shards['small'] = small_local
        return shards

    pieces = [(i, part) for i in range(depth) for part in ('mixer', 'ffn')]
    gathered = {}
    pname = lambda pc: f"gather_l{pc[0]}_{pc[1]}"
    states = {}
    placed = {pieces[0]: _gather_place(piece_shards(*pieces[0]), place, name=pname(pieces[0]))}
    states[pieces[0]], token = _gather_begin(placed[pieces[0]], mix_norm, name=pname(pieces[0]))
    prev = token
    for pc in pieces[1:]:
        placed[pc] = _gather_place(piece_shards(*pc), place, name=pname(pc), after=prev)
        prev = placed[pc][2][-1]
    gathered[pieces[0]] = _gather_end(states.pop(pieces[0]), placed[pieces[-1]][2][-1], name=pname(pieces[0]))
    first_done = next(iter(gathered[pieces[0]].values()))
    states[pieces[1]], token = _gather_begin(placed[pieces[1]], first_done, name=pname(pieces[1]))
    tokens = [token]
    small_shapes = [W[k][0].shape if k != 'ffn_conv_w' else W[k].shape for k in SMALL_SHARDED]
    per_chip = [_unpack(gathered[pieces[0]]['small'][s], small_shapes) for s in range(N_CHIPS)]
    full_small = {k: jnp.concatenate([per_chip[s][j] for s in range(N_CHIPS)], axis=-1)
                  for j, k in enumerate(SMALL_SHARDED)}

    pos = positions.reshape(s_len).astype(F32)
    inv_freq = ROPE_THETA ** (-jnp.arange(0, MLA_ROPE, 2, dtype=F32) / MLA_ROPE)
    ang = pos[:, None] * inv_freq
    cos_t = _pad_lanes(jnp.concatenate([jnp.cos(ang), jnp.cos(ang)], axis=1))
    sin_t = _pad_lanes(jnp.concatenate([-jnp.sin(ang), jnp.sin(ang)], axis=1))

    def piece_params(i, part):
        mixer = MIXERS[i % len(MIXERS)]
        g = gathered[(i, part)]
        if part == 'ffn':
            return mixer, {'ffn_norm': ffn_norm[i:i + 1], 'ffn_w_up': g['ffn_w_up'],
                           'ffn_w_down': g['ffn_w_down'].reshape(-1, d_model),
                           'ffn_conv_w': full_small['ffn_conv_w'][i], 'ffn_conv_b': ffn_conv_b[i:i + 1]}
        p = {'mix_norm': mix_norm[i:i + 1]}
        rows = lambda w: w.reshape(-1, w.shape[-1])
        if mixer == 'fox':
            w = _unshard_cols(g['fox_w_in'])
            p['w_in'] = jnp.pad(w, ((0, 0), (0, (3 * n_heads + 1) * HEAD_DIM - w.shape[1])))
            p['b_f'] = _pad_lanes(fox_b_f)
            p['qk_gain'] = jnp.stack([fox_q_gain, fox_k_gain])
            p['w_out'] = rows(g['fox_w_out'])
        elif mixer == 'sb':
            p['w_in'] = g['sb_w_in']
            p['qk_gain'] = jnp.stack([sb_q_gain, sb_k_gain])
            p['w_out'] = rows(g['sb_w_out'])
        elif mixer == 'sgu':
            p['w_in'] = g['sgu_w_in']
            p['v_gain'] = full_small['sgu_v_gain'].reshape(1, -1)
            p['w_s'] = sgu_w_s[0]
            p['b_s'] = sgu_b_s[0][:, :, None]
            p['w_out'] = rows(g['sgu_w_out'])
        else:
            w = rows(g['mla_w_in'])
            p['w_in'] = jnp.pad(w, ((0, 0), (0, MLA_Q_RANK + MLA_KV_RANK + LANES - w.shape[1])))
            p['a_gain'] = jnp.concatenate([full_small['mla_q_a_gain'], full_small['mla_kv_a_gain']]).reshape(1, -1)
            wq = _unshard_cols(g['mla_w_q_b']).reshape(MLA_Q_RANK, n_heads, MLA_NOPE + MLA_ROPE)
            p['w_q_b'] = jnp.concatenate([wq[:, :, :MLA_NOPE].reshape(MLA_Q_RANK, -1),
                                          _pad_lanes(wq[:, :, MLA_NOPE:]).reshape(MLA_Q_RANK, -1)], axis=1)
            p['w_kv_b'] = g['mla_w_kv_b']
            p['gq'] = jnp.stack([mla_q_gain[:, :MLA_NOPE], _pad_lanes(mla_q_gain[:, MLA_NOPE:])])
            p['gk'] = jnp.stack([mla_k_gain[:, :MLA_NOPE], _pad_lanes(mla_k_gain[:, MLA_NOPE:])])
            p['cos'], p['sin'] = cos_t, sin_t
            p['w_out'] = rows(g['mla_w_out'])
        return mixer, p

    h = x.reshape(s_len, d_model)
    saved = []
    for n, (i, part) in enumerate(pieces):
        nxt = pieces[n + 1] if n + 1 < len(pieces) else None
        ahead = pieces[n + 2] if n + 2 < len(pieces) else None
        if ahead is not None:
            after = next(iter(gathered[(i, part)].values()))
            states[ahead], token = _gather_begin(placed[ahead], after, name=pname(ahead))
            tokens.append(token)
        mixer, p = piece_params(i, part)
        gain = 'ffn_norm' if part == 'ffn' else 'mix_norm'
        for token in tokens:
            p[gain] = p[gain] + token[0:1, 0:1]
        tokens = []
        tag = f"l{i}_{mixer}"
        if part == 'ffn':
            h, sv = _ffn_fwd(h, p, f"l{i}")
        elif mixer in ('fox', 'sb'):
            h, sv = _qkv_attn_fwd(mixer, h, p, tag, n_heads)
        elif mixer == 'mla':
            h, sv = _mla_fwd(h, p, tag, n_heads)
        else:
            h, sv = _sgu_fwd(h, p, tag)
        saved.append((mixer, p, sv))
        if nxt is not None:
            gathered[nxt] = _gather_end(states.pop(nxt), h, name=pname(nxt))
    loss_row, dh = _loss(h, loss_target.reshape(s_len, d_model))
    loss = lax.psum(loss_row[0, 0], ("x", "y", "c"))

    big_grads, small_grads = {}, {k: [None] * depth for k in ('mix_norm', 'ffn_norm', 'ffn_conv_w', 'ffn_conv_b')}

    def keep(reduced, i):
        for k, v in reduced.items():
            if k.startswith('ffn_'):
                big_grads.setdefault(k, [None] * depth)[i] = v
            else:
                big_grads[k] = v[None]

    state, token, flying = None, None, None
    for n in reversed(range(len(pieces))):
        i, part = pieces[n]
        mixer, p, sv = saved[n]
        tag = f"l{i}_{mixer}"
        rname = f"reduce_l{i}_{part}"
        started = []

        def on_big(big, prefix=('' if part == 'ffn' else mixer + '_'), rname=rname, started=started):
            st, tok = _reduce_sibling_start({prefix + k: v for k, v in big.items()}, place[0], name=rname)
            started.append(st)
            return tok

        if part == 'ffn':
            dh, small = _ffn_bwd(dh, sv, p, f"l{i}", token, on_big)
        elif mixer in ('fox', 'sb'):
            dh, small = _qkv_attn_bwd(mixer, dh, sv, p, tag, n_heads, token, on_big)
        elif mixer == 'mla':
            dh, small = _mla_bwd(dh, sv, p, tag, n_heads, token, on_big)
        else:
            dh, small = _sgu_bwd(dh, sv, p, tag, token, on_big)
        if state is not None:
            keep(_reduce_end(state, place, dh, name=f"reduce_l{flying[0]}_{flying[1]}"), flying[0])
        state, token = _reduce_begin(started[0], place, dh, name=rname)
        flying = (i, part)
        for k, v in small.items():
            if k in small_grads:
                small_grads[k][i] = v
            else:
                small_grads[f"{mixer}_{k}"] = v
    last_state = state
    grad_x = dh.reshape(x.shape)
    for k in ('mix_norm', 'ffn_norm', 'ffn_conv_b'):
        small_grads[k] = jnp.concatenate(small_grads[k], axis=0)
    small_grads['ffn_conv_w'] = jnp.stack(small_grads['ffn_conv_w'])

    small_names = [k for k in WEIGHT_NAMES if k not in BIG]
    full_shapes = {k: (W[k].shape[:-1] + (W[k].shape[-1] * N_CHIPS,) if k in SMALL_SHARDED else W[k].shape)
                   for k in small_names}
    packed = _pack([small_grads[k].reshape(full_shapes[k]) for k in small_names])
    last_token = token
    summed = _sum_devices(_broadcast_all(packed, last_token, name="small_bcast"), name="small_sum")
    small_full = dict(zip(small_names, _unpack(summed, [full_shapes[k] for k in small_names])))
    for k in ('ffn_w_up', 'ffn_w_down'):
        big_grads[k] = jnp.stack(big_grads[k])
    grads = dict(big_grads)
    for k in small_names:
        g = small_full[k]
        if k in SMALL_SHARDED:
            n = W[k].shape[-1]
            g = lax.dynamic_slice_in_dim(g, chip * n, n, axis=g.ndim - 1)
        grads[k] = g

    delta, new_m, new_v = {}, {}, {}

    def update(k):
        grads[k] = grads[k].reshape(W[k].shape)
        delta[k], new_m[k], new_v[k] = _adamw(W[k], grads[k], M[k], V[k], after=last_token, name=f"adamw_{k}")

    last_names = [k for k in BIG if k.startswith(MIXERS[0] + '_')]
    for k in WEIGHT_NAMES:
        if k not in last_names:
            update(k)
    keep(_reduce_end(last_state, place, delta['ffn_w_up'], name="reduce_l0_mixer"), 0)
    for k in last_names:
        grads[k] = big_grads[k]
        update(k)
    return (loss, grad_x, *[grads[k] for k in WEIGHT_NAMES], *[delta[k] for k in WEIGHT_NAMES],
            *[new_m[k] for k in WEIGHT_NAMES], *[new_v[k] for k in WEIGHT_NAMES])
```

```python
import functools

import jax
import jax.numpy as jnp
from jax import lax
from jax.experimental import pallas as pl
from jax.experimental.pallas import tpu as pltpu

F32 = jnp.float32
BF16 = jnp.bfloat16
LANES = 128
HEAD_DIM = 128
NORM_EPS = 1e-6
MLA_Q_RANK = 512
MLA_KV_RANK = 512
MLA_NOPE = 128
MLA_ROPE = 64
ROPE_THETA = 10000.0
SGU_CHUNK = 128
N_CHIPS = 4
ADAM_LR, ADAM_B1, ADAM_B2, ADAM_EPS, ADAM_WD, ADAM_STEP = 0.001, 0.9, 0.999, 1e-08, 0.01, 10
VMEM_LIMIT_BYTES = 56 * 1024 * 1024
MM_VMEM_BUDGET_BYTES = 36 * 1024 * 1024
SHARD_BLOCK_BYTES = 4 * 1024 * 1024
MESH = pl.DeviceIdType.MESH
NEG_BIG = -1e30


def _params(*sem):
    return pltpu.CompilerParams(dimension_semantics=sem, vmem_limit_bytes=VMEM_LIMIT_BYTES)


def _div_block(n, target, mult=LANES):
    if n <= target:
        return n
    best = None
    for b in range(mult, target + 1, mult):
        if n % b == 0:
            best = b
    assert best is not None, (n, target, mult)
    return best


def _after_spec(after):
    if after is None:
        return []
    return [pl.BlockSpec(memory_space=pl.ANY)]


def _after_arg(after):
    return [] if after is None else [after]


def _iota(shape, dim):
    return lax.broadcasted_iota(jnp.int32, shape, dim)


def _dot(a, b, ca, cb):
    return lax.dot_general(a, b, (((ca,), (cb,)), ((), ())), preferred_element_type=F32)


def _mm(a, b, *, name, ta=False, tb=False, a_sh=False, b_sh=None, o_sh=False, res=None, after=None, out_dtype=F32,
        bm=1024, bn=1024, bk=512):
    if a_sh:
        assert not ta
        m, k = a.shape[1], a.shape[0] * a.shape[2]
    else:
        m, k = (a.shape[1], a.shape[0]) if ta else a.shape
    if b_sh == 'n':
        n = b.shape[2] * b.shape[0]
        assert b.shape[1] == k and not tb
    elif b_sh == 'k':
        n = b.shape[1]
        assert b.shape[2] * N_CHIPS == k
    else:
        n = b.shape[0] if tb else b.shape[1]
        assert (b.shape[1] if tb else b.shape[0]) == k
    n_sh = n // N_CHIPS
    k_sh = k // N_CHIPS
    bm = _div_block(m, bm, 8 if not ta else LANES)
    bn_limit = n
    if b_sh == 'n':
        bn_limit = b.shape[2]
    if o_sh:
        bn_limit = min(bn_limit, n_sh)
    bn = _div_block(bn_limit, bn)
    assert (not o_sh or n_sh % bn == 0) and (b_sh != 'n' or b.shape[2] % bn == 0)
    bk_limit = k_sh if b_sh == 'k' else k
    if a_sh:
        bk_limit = min(bk_limit, a.shape[2])

    def footprint(kb):
        io = bm * kb * a.dtype.itemsize + kb * bn * b.dtype.itemsize + bm * bn * jnp.dtype(out_dtype).itemsize
        if res is not None:
            io += bm * bn * res.dtype.itemsize
        return 2 * io + (bm * bn * 4 if kb < k else 0)

    bk = max([kb for kb in range(LANES, bk_limit + 1, LANES)
              if bk_limit % kb == 0 and (footprint(kb) <= MM_VMEM_BUDGET_BYTES or kb <= bk)])
    assert (not a_sh or a.shape[2] % bk == 0) and (b_sh != 'k' or k_sh % bk == 0) and k % bk == 0
    nbo = n_sh // bn if o_sh else 1
    nbb = b.shape[2] // bn if b_sh == 'n' else 1
    nks = k_sh // bk if b_sh == 'k' else 1
    nka = a.shape[2] // bk if a_sh else 1
    nk = k // bk

    if a_sh:
        a_spec = pl.BlockSpec((None, bm, bk), lambda i, j, q: (q // nka, i, q % nka))
    elif ta:
        a_spec = pl.BlockSpec((bk, bm), lambda i, j, q: (q, i))
    else:
        a_spec = pl.BlockSpec((bm, bk), lambda i, j, q: (i, q))
    if b_sh == 'n':
        b_spec = pl.BlockSpec((None, bk, bn), lambda i, j, q: (j // nbb, q, j % nbb))
    elif b_sh == 'k':
        b_spec = pl.BlockSpec((None, bn, bk), lambda i, j, q: (q // nks, j, q % nks))
    elif tb:
        b_spec = pl.BlockSpec((bn, bk), lambda i, j, q: (j, q))
    else:
        b_spec = pl.BlockSpec((bk, bn), lambda i, j, q: (q, j))
    if o_sh:
        o_spec = pl.BlockSpec((None, bm, bn), lambda i, j, q: (j // nbo, i, j % nbo))
        o_shape = jax.ShapeDtypeStruct((N_CHIPS, m, n_sh), out_dtype)
    else:
        o_spec = pl.BlockSpec((bm, bn), lambda i, j, q: (i, j))
        o_shape = jax.ShapeDtypeStruct((m, n), out_dtype)
    tb_eff = tb or b_sh == 'k'

    def body(a_ref, b_ref, *rest):
        rest = list(rest)
        if after is not None:
            rest.pop(0)
        r_ref = rest.pop(0) if res is not None else None
        o_ref = rest.pop(0)
        part = _dot(a_ref[...].astype(BF16), b_ref[...].astype(BF16), 0 if ta else 1, 1 if tb_eff else 0)

        def finish(r):
            if res is not None:
                r = r + r_ref[...].astype(F32)
            o_ref[...] = r.astype(out_dtype)

        if nk == 1:
            finish(part)
            return
        acc, = rest
        q = pl.program_id(2)

        @pl.when(q == 0)
        def _():
            acc[...] = part

        @pl.when(q > 0)
        def _():
            acc[...] += part

        @pl.when(q == nk - 1)
        def _():
            finish(acc[...])

    ins = [a, b]
    in_specs = [a_spec, b_spec]
    if after is not None:
        ins.append(after)
        in_specs.append(pl.BlockSpec((8, LANES), lambda i, j, q: (0, 0)))
    if res is not None:
        assert not o_sh
        ins.append(res)
        in_specs.append(pl.BlockSpec((bm, bn), lambda i, j, q: (i, j)))
    return pl.pallas_call(
        body, name=name, grid=(m // bm, n // bn, nk), in_specs=in_specs, out_specs=o_spec, out_shape=o_shape,
        scratch_shapes=[pltpu.VMEM((bm, bn), F32)] if nk > 1 else [],
        compiler_params=_params("parallel", "parallel", "arbitrary"))(*ins)


def _rms_fwd(x, g, *, name, out_dtype=BF16, br=256):
    r, c = x.shape
    br = _div_block(r, br, 8)

    def body(x_ref, g_ref, o_ref):
        xv = x_ref[...].astype(F32)
        inv = lax.rsqrt(jnp.mean(xv * xv, axis=-1, keepdims=True) + NORM_EPS)
        o_ref[...] = (xv * inv * g_ref[...]).astype(out_dtype)

    return pl.pallas_call(
        body, name=name, grid=(r // br,),
        in_specs=[pl.BlockSpec((br, c), lambda i: (i, 0)), pl.BlockSpec((1, c), lambda i: (0, 0))],
        out_specs=pl.BlockSpec((br, c), lambda i: (i, 0)), out_shape=jax.ShapeDtypeStruct((r, c), out_dtype),
        compiler_params=_params("parallel"))(x, g)


def _rms_bwd_math(xv, gv, dyv, n):
    inv = lax.rsqrt(jnp.sum(xv * xv, axis=-1, keepdims=True) / n + NORM_EPS)
    xh = xv * inv
    dyg = dyv * gv
    dx = inv * (dyg - xh * (jnp.sum(dyg * xh, axis=-1, keepdims=True) / n))
    return dx, dyv * xh


def _rms_bwd(x, g, dy, *, name, res=None, br=256):
    r, c = x.shape
    br = _div_block(r, br, 8)

    def body(x_ref, g_ref, dy_ref, *rest):
        if res is not None:
            r_ref, dx_ref, dg_ref = rest
        else:
            dx_ref, dg_ref = rest
        dx, dgr = _rms_bwd_math(x_ref[...].astype(F32), g_ref[...], dy_ref[...].astype(F32), c)
        if res is not None:
            dx = dx + r_ref[...]
        dx_ref[...] = dx

        @pl.when(pl.program_id(0) == 0)
        def _():
            dg_ref[...] = jnp.zeros_like(dg_ref)

        dg_ref[...] += jnp.sum(dgr, axis=0, keepdims=True)

    row = pl.BlockSpec((br, c), lambda i: (i, 0))
    vec = pl.BlockSpec((1, c), lambda i: (0, 0))
    ins = [x, g, dy] + ([res] if res is not None else [])
    return pl.pallas_call(
        body, name=name, grid=(r // br,), in_specs=[row, vec, row] + ([row] if res is not None else []),
        out_specs=[row, vec], out_shape=[jax.ShapeDtypeStruct((r, c), F32), jax.ShapeDtypeStruct((1, c), F32)],
        compiler_params=_params("arbitrary"))(*ins)


def _loss(y, target, *, name="loss", br=256):
    r, c = y.shape
    br = _div_block(r, br, 8)

    def body(y_ref, t_ref, l_ref, dy_ref):
        d = y_ref[...] - t_ref[...]
        dy_ref[...] = d * (1.0 / c)

        @pl.when(pl.program_id(0) == 0)
        def _():
            l_ref[...] = jnp.zeros_like(l_ref)

        part = jnp.sum(d * d, axis=0, keepdims=True)
        l_ref[...] += (0.5 / c) * jnp.sum(part, axis=1, keepdims=True) * jnp.ones((1, LANES), F32)

    row = pl.BlockSpec((br, c), lambda i: (i, 0))
    return pl.pallas_call(
        body, name=name, grid=(r // br,), in_specs=[row, row],
        out_specs=[pl.BlockSpec((1, LANES), lambda i: (0, 0)), row],
        out_shape=[jax.ShapeDtypeStruct((1, LANES), F32), jax.ShapeDtypeStruct((r, c), F32)],
        compiler_params=_params("arbitrary"))(y, target)


def _split2(x):
    hi = x.astype(BF16)
    lo = (x - hi.astype(F32)).astype(BF16)
    return hi, lo


def _lane_scan(x, *, suffix):
    rows, n = x.shape
    nb = n // LANES
    a, b = _iota((LANES, LANES), 0), _iota((LANES, LANES), 1)
    tri = ((a > b) if suffix else (a < b)).astype(BF16)
    outs = [None] * nb
    run = jnp.zeros((rows, 1), F32)
    order = range(nb - 1, -1, -1) if suffix else range(nb)
    for blk in order:
        xb = x[:, blk * LANES:(blk + 1) * LANES]
        hi, lo = _split2(xb)
        outs[blk] = _dot(hi, tri, 1, 0) + _dot(lo, tri, 1, 0) + run
        run = run + jnp.sum(xb, axis=-1, keepdims=True)
    return jnp.concatenate(outs, axis=1)


def _softplus(z):
    return jnp.maximum(z, 0.0) + jnp.log(1.0 + jnp.exp(-jnp.abs(z)))


def _head_norm(x, g):
    xv = x.astype(F32)
    inv = lax.rsqrt(jnp.mean(xv * xv, axis=-1, keepdims=True) + NORM_EPS)
    return xv * inv * g


def _attn_weights(kind, qn, kn, scale, qi, bq, bias):
    s = _dot(qn, kn, 1, 1) * scale
    row = qi * bq + _iota(s.shape, 0)
    col = _iota(s.shape, 1)
    if kind == 'sb':
        strict = col < row
        sp = _softplus(s)
        after = _lane_scan(jnp.where(strict, -sp, 0.0), suffix=True)
        w = jnp.where(strict, jnp.exp(s - sp + after), 0.0)
        return w, (strict, s - sp)
    if bias is not None:
        s = s + bias
    s = jnp.where(col <= row, s, NEG_BIG)
    mx = jnp.max(s, axis=-1, keepdims=True)
    e = jnp.exp(s - mx)
    return e, jnp.sum(e, axis=-1, keepdims=True)


def _attn_fwd(kind, q, k, v, *, name, n_heads, dqk, qcol, kcol, vcol, scale, gains=None, cq=None, ck=None, bq=256):
    s_len = q.shape[0]
    bq = _div_block(s_len, bq, 8)
    norm, fox = gains is not None, cq is not None

    def body(*refs):
        refs = list(refs)
        q_ref, k_ref, v_ref = refs[:3]
        rest = refs[3:]
        g_ref = rest.pop(0) if norm else None
        cq_ref, ck_ref = (rest.pop(0), rest.pop(0)) if fox else (None, None)
        o_ref, = rest
        qi = pl.program_id(1)

        def step(n_keys):
            if norm:
                qn = _head_norm(q_ref[...], g_ref[0]).astype(BF16)
                kn = _head_norm(k_ref[0:n_keys, :], g_ref[1]).astype(BF16)
            else:
                qn, kn = q_ref[...].astype(BF16), k_ref[0:n_keys, :].astype(BF16)
            bias = (cq_ref[...] - ck_ref[:, 0:n_keys]) if fox else None
            w, aux = _attn_weights(kind, qn, kn, scale, qi, bq, bias)
            o = _dot(w.astype(BF16), v_ref[0:n_keys, :].astype(BF16), 1, 0)
            if kind != 'sb':
                o = o / aux
            o_ref[...] = o.astype(BF16)

        for qv in range(s_len // bq):
            pl.when(qi == qv)(functools.partial(step, (qv + 1) * bq))

    in_specs = [pl.BlockSpec((bq, dqk), lambda h, i: (i, qcol(h))),
                pl.BlockSpec((s_len, dqk), lambda h, i: (0, kcol(h))),
                pl.BlockSpec((s_len, HEAD_DIM), lambda h, i: (0, vcol(h)))]
    ins = [q, k, v]
    if norm:
        in_specs.append(pl.BlockSpec((2, 1, dqk), lambda h, i: (0, 0, 0)))
        ins.append(gains)
    if fox:
        in_specs += [pl.BlockSpec((None, bq, 1), lambda h, i: (h, i, 0)), pl.BlockSpec((None, 1, s_len), lambda h, i: (h, 0, 0))]
        ins += [cq, ck]
    return pl.pallas_call(
        body, name=name, grid=(n_heads, s_len // bq), in_specs=in_specs,
        out_specs=pl.BlockSpec((bq, HEAD_DIM), lambda h, i: (i, h)),
        out_shape=jax.ShapeDtypeStruct((s_len, n_heads * HEAD_DIM), BF16),
        compiler_params=_params("parallel", "parallel"))(*ins)


def _attn_bwd(kind, q, k, v, o, do, *, name, n_heads, dqk, qcol, kcol, vcol, scale, gains=None, cq=None, ck=None,
              bq=256):
    s_len = q.shape[0]
    bq = _div_block(s_len, bq, 8)
    nq = s_len // bq
    norm, fox = gains is not None, cq is not None

    def body(*refs):
        refs = list(refs)
        q_ref, k_ref, v_ref, o_ref, do_ref = refs[:5]
        rest = refs[5:]
        g_ref = rest.pop(0) if norm else None
        cq_ref, ck_ref = (rest.pop(0), rest.pop(0)) if fox else (None, None)
        dq_ref, dk_ref, dv_ref = rest.pop(0), rest.pop(0), rest.pop(0)
        dg_ref = rest.pop(0) if norm else None
        dcq_ref, dck_ref = (rest.pop(0), rest.pop(0)) if fox else (None, None)
        dk_acc, dv_acc = rest
        h, qi = pl.program_id(0), pl.program_id(1)

        @pl.when(qi == 0)
        def _():
            dk_acc[...] = jnp.zeros_like(dk_acc)
            dv_acc[...] = jnp.zeros_like(dv_acc)
            if fox:
                dck_ref[...] = jnp.zeros_like(dck_ref)

        if norm:
            @pl.when((qi == 0) & (h == 0))
            def _():
                dg_ref[...] = jnp.zeros_like(dg_ref)


        def step(n_keys):
            if norm:
                qn = _head_norm(q_ref[...], g_ref[0]).astype(BF16)
                kn = _head_norm(k_ref[0:n_keys, :], g_ref[1]).astype(BF16)
            else:
                qn, kn = q_ref[...].astype(BF16), k_ref[0:n_keys, :].astype(BF16)
            vb = v_ref[0:n_keys, :].astype(BF16)
            dob = do_ref[...].astype(BF16)
            bias = (cq_ref[...] - ck_ref[:, 0:n_keys]) if fox else None
            w, aux = _attn_weights(kind, qn, kn, scale, qi, bq, bias)
            dw = _dot(dob, vb, 1, 1)
            if kind == 'sb':
                strict, log_sig = aux
                g = dw * w
                cc = _lane_scan(g, suffix=False)
                sig = jnp.exp(log_sig)
                ds = jnp.where(strict, g * (1.0 - sig) - cc * sig, 0.0)
                pw = w
            else:
                pw = w / aux
                delta = jnp.sum(do_ref[...].astype(F32) * o_ref[...].astype(F32), axis=-1, keepdims=True)
                ds = pw * (dw - delta)
                if fox:
                    dcq_ref[...] = jnp.sum(ds, axis=1, keepdims=True)
                    dck_ref[:, 0:n_keys] -= jnp.sum(ds, axis=0, keepdims=True)
            dsb = (ds * scale).astype(BF16)
            dqn = _dot(dsb, kn, 1, 0)
            dk_acc[0:n_keys, :] += _dot(dsb, qn, 0, 0)
            dv_acc[0:n_keys, :] += _dot(pw.astype(BF16), dob, 0, 0)
            if norm:
                dq, dgr = _rms_bwd_math(q_ref[...].astype(F32), g_ref[0], dqn, dqk)
                dg_ref[0] += jnp.sum(dgr, axis=0, keepdims=True)
                dq_ref[...] = dq.astype(BF16)
            else:
                dq_ref[...] = dqn.astype(BF16)

        for qv in range(nq):
            pl.when(qi == qv)(functools.partial(step, (qv + 1) * bq))

        @pl.when(qi == nq - 1)
        def _():
            if norm:
                dk, dgr = _rms_bwd_math(k_ref[...].astype(F32), g_ref[1], dk_acc[...], dqk)
                dg_ref[1] += jnp.sum(dgr, axis=0, keepdims=True)
                dk_ref[...] = dk.astype(BF16)
            else:
                dk_ref[...] = dk_acc[...].astype(BF16)
            dv_ref[...] = dv_acc[...].astype(BF16)

    in_specs = [pl.BlockSpec((bq, dqk), lambda h, i: (i, qcol(h))),
                pl.BlockSpec((s_len, dqk), lambda h, i: (0, kcol(h))),
                pl.BlockSpec((s_len, HEAD_DIM), lambda h, i: (0, vcol(h))),
                pl.BlockSpec((bq, HEAD_DIM), lambda h, i: (i, h)),
                pl.BlockSpec((bq, HEAD_DIM), lambda h, i: (i, h))]
    ins = [q, k, v, o, do]
    out_specs = [pl.BlockSpec((bq, dqk), lambda h, i: (i, h)),
                 pl.BlockSpec((s_len, dqk), lambda h, i: (0, h)),
                 pl.BlockSpec((s_len, HEAD_DIM), lambda h, i: (0, h))]
    out_shape = [jax.ShapeDtypeStruct((s_len, n_heads * dqk), BF16), jax.ShapeDtypeStruct((s_len, n_heads * dqk), BF16),
                 jax.ShapeDtypeStruct((s_len, n_heads * HEAD_DIM), BF16)]
    if norm:
        in_specs.append(pl.BlockSpec((2, 1, dqk), lambda h, i: (0, 0, 0)))
        ins.append(gains)
        out_specs.append(pl.BlockSpec((2, 1, dqk), lambda h, i: (0, 0, 0)))
        out_shape.append(jax.ShapeDtypeStruct((2, 1, dqk), F32))
    if fox:
        in_specs += [pl.BlockSpec((None, bq, 1), lambda h, i: (h, i, 0)), pl.BlockSpec((None, 1, s_len), lambda h, i: (h, 0, 0))]
        ins += [cq, ck]
        out_specs += [pl.BlockSpec((None, bq, 1), lambda h, i: (h, i, 0)), pl.BlockSpec((None, 1, s_len), lambda h, i: (h, 0, 0))]
        out_shape += [jax.ShapeDtypeStruct((n_heads, s_len, 1), F32), jax.ShapeDtypeStruct((n_heads, 1, s_len), F32)]
    return pl.pallas_call(
        body, name=name, grid=(n_heads, nq), in_specs=in_specs, out_specs=out_specs, out_shape=out_shape,
        scratch_shapes=[pltpu.VMEM((s_len, dqk), F32), pltpu.VMEM((s_len, HEAD_DIM), F32)],
        compiler_params=_params("arbitrary", "arbitrary"))(*ins)


def _split3(x):
    hi = x.astype(BF16)
    r1 = x - hi.astype(F32)
    mid = r1.astype(BF16)
    lo = (r1 - mid.astype(F32)).astype(BF16)
    return hi, mid, lo


def _seq_scan(x, *, reverse):
    n = x.shape[0] // LANES
    a, b = _iota((LANES, LANES), 0), _iota((LANES, LANES), 1)
    tri = ((b >= a) if reverse else (b <= a)).astype(BF16)
    outs = [None] * n
    run = jnp.zeros((1, x.shape[1]), F32)
    for blk in (range(n - 1, -1, -1) if reverse else range(n)):
        xb = x[blk * LANES:(blk + 1) * LANES, :]
        hi, mid, lo = _split3(xb)
        outs[blk] = _dot(tri, hi, 1, 0) + _dot(tri, mid, 1, 0) + _dot(tri, lo, 1, 0) + run
        run = run + jnp.sum(xb, axis=0, keepdims=True)
    return jnp.concatenate(outs, axis=0)


def _fgate_fwd(qkvf, b_f, *, fcol, name):
    s_len = qkvf.shape[0]

    def body(f_ref, b_ref, cum_ref):
        z = f_ref[...] + b_ref[...]
        cum_ref[...] = _seq_scan(-_softplus(-z), reverse=False)

    return pl.pallas_call(
        body, name=name, grid=(1,),
        in_specs=[pl.BlockSpec((s_len, LANES), lambda i: (0, fcol)), pl.BlockSpec((1, LANES), lambda i: (0, 0))],
        out_specs=pl.BlockSpec((s_len, LANES), lambda i: (0, 0)), out_shape=jax.ShapeDtypeStruct((s_len, LANES), F32),
        compiler_params=_params("arbitrary"))(qkvf, b_f)


def _fgate_bwd(qkvf, b_f, dcum_a, dcum_b, *, fcol, n_heads, name):
    s_len = qkvf.shape[0]

    def body(f_ref, b_ref, da_ref, db_ref, dz_ref, dbias_ref):
        z = f_ref[...] + b_ref[...]
        dlog = _seq_scan(da_ref[...] + db_ref[...], reverse=True)
        dz = dlog * jnp.exp(-_softplus(z))
        dz = jnp.where(_iota(dz.shape, 1) < n_heads, dz, 0.0)
        dz_ref[...] = dz.astype(BF16)
        dbias_ref[...] = jnp.sum(dz, axis=0, keepdims=True)

    full = pl.BlockSpec((s_len, LANES), lambda i: (0, 0))
    vec = pl.BlockSpec((1, LANES), lambda i: (0, 0))
    return pl.pallas_call(
        body, name=name, grid=(1,),
        in_specs=[pl.BlockSpec((s_len, LANES), lambda i: (0, fcol)), vec, full, full],
        out_specs=[full, vec], out_shape=[jax.ShapeDtypeStruct((s_len, LANES), BF16), jax.ShapeDtypeStruct((1, LANES), F32)],
        compiler_params=_params("arbitrary"))(qkvf, b_f, dcum_a, dcum_b)


def _rope_swap(x):
    half = MLA_ROPE // 2
    lane = _iota(x.shape, 1)
    sw = jnp.where(lane < half, pltpu.roll(x, LANES - half, axis=1), pltpu.roll(x, half, axis=1))
    return jnp.where(lane < MLA_ROPE, sw, 0.0)


def _mla_prep_fwd(qp, kv, c, gq, gk, cos_t, sin_t, *, n_heads, name, bs=512):
    s_len = qp.shape[0]
    bs = _div_block(s_len, bs, 8)
    krope_col = (MLA_Q_RANK + MLA_KV_RANK) // LANES

    def body(qn_ref, qr_ref, kn_ref, kr_ref, gq_ref, gk_ref, cos_ref, sin_ref, qc_ref, kc_ref):
        cos_v, sin_v = cos_ref[...], sin_ref[...]

        def rope(x, g):
            xv = x.astype(F32)
            inv = lax.rsqrt(jnp.sum(xv * xv, axis=-1, keepdims=True) / MLA_ROPE + NORM_EPS)
            y = xv * inv * g
            return y * cos_v + _rope_swap(y) * sin_v

        qc_ref[:, :LANES] = _head_norm(qn_ref[...], gq_ref[0]).astype(BF16)
        qc_ref[:, LANES:] = rope(qr_ref[...], gq_ref[1]).astype(BF16)
        kc_ref[:, :LANES] = _head_norm(kn_ref[...], gk_ref[0]).astype(BF16)
        kc_ref[:, LANES:] = rope(kr_ref[...], gk_ref[1]).astype(BF16)

    blk = lambda f: pl.BlockSpec((bs, LANES), f)
    gspec = pl.BlockSpec((2, 1, LANES), lambda i, h: (0, 0, 0))
    tspec = pl.BlockSpec((bs, LANES), lambda i, h: (i, 0))
    ospec = pl.BlockSpec((bs, 2 * LANES), lambda i, h: (i, h))
    oshape = jax.ShapeDtypeStruct((s_len, n_heads * 2 * LANES), BF16)
    return pl.pallas_call(
        body, name=name, grid=(s_len // bs, n_heads),
        in_specs=[blk(lambda i, h: (i, h)), blk(lambda i, h: (i, n_heads + h)), blk(lambda i, h: (i, 2 * h)),
                  blk(lambda i, h: (i, krope_col)), gspec, gspec, tspec, tspec],
        out_specs=[ospec, ospec], out_shape=[oshape, oshape],
        compiler_params=_params("parallel", "parallel"))(qp, qp, kv, c, gq, gk, cos_t, sin_t)


def _mla_prep_bwd(qp, kv, c, gq, gk, cos_t, sin_t, dqc, dkc, dv, *, n_heads, name, bs=512):
    s_len = qp.shape[0]
    bs = _div_block(s_len, bs, 8)
    krope_col = (MLA_Q_RANK + MLA_KV_RANK) // LANES

    def body(qn_ref, qr_ref, kn_ref, kr_ref, gq_ref, gk_ref, cos_ref, sin_ref, dqc_ref, dkc_ref, dv_ref,
             dqn_ref, dqr_ref, dkv_ref, dkr_ref, dgq_ref, dgk_ref):
        i, h = pl.program_id(0), pl.program_id(1)
        cos_v, sin_v = cos_ref[...], sin_ref[...]

        @pl.when((i == 0) & (h == 0))
        def _():
            dgq_ref[...] = jnp.zeros_like(dgq_ref)
            dgk_ref[...] = jnp.zeros_like(dgk_ref)

        @pl.when(h == 0)
        def _():
            dkr_ref[...] = jnp.zeros_like(dkr_ref)

        def unrope(dy):
            dy = dy.astype(F32)
            return dy * cos_v + _rope_swap(dy * sin_v)

        dqn, dg = _rms_bwd_math(qn_ref[...].astype(F32), gq_ref[0], dqc_ref[:, :LANES].astype(F32), MLA_NOPE)
        dgq_ref[0] += jnp.sum(dg, axis=0, keepdims=True)
        dqn_ref[...] = dqn.astype(BF16)
        dqr, dg = _rms_bwd_math(qr_ref[...].astype(F32), gq_ref[1], unrope(dqc_ref[:, LANES:]), MLA_ROPE)
        dgq_ref[1] += jnp.sum(dg, axis=0, keepdims=True)
        dqr_ref[...] = dqr.astype(BF16)
        dkn, dg = _rms_bwd_math(kn_ref[...].astype(F32), gk_ref[0], dkc_ref[:, :LANES].astype(F32), MLA_NOPE)
        dgk_ref[0] += jnp.sum(dg, axis=0, keepdims=True)
        dkv_ref[:, :LANES] = dkn.astype(BF16)
        dkv_ref[:, LANES:] = dv_ref[...]
        dkr, dg = _rms_bwd_math(kr_ref[...].astype(F32), gk_ref[1], unrope(dkc_ref[:, LANES:]), MLA_ROPE)
        dgk_ref[1] += jnp.sum(dg, axis=0, keepdims=True)
        dkr_ref[...] += dkr

    blk = lambda f: pl.BlockSpec((bs, LANES), f)
    gspec = pl.BlockSpec((2, 1, LANES), lambda i, h: (0, 0, 0))
    tspec = pl.BlockSpec((bs, LANES), lambda i, h: (i, 0))
    cat = pl.BlockSpec((bs, 2 * LANES), lambda i, h: (i, h))
    head = blk(lambda i, h: (i, h))
    hshape = jax.ShapeDtypeStruct((s_len, n_heads * LANES), BF16)
    gshape = jax.ShapeDtypeStruct((2, 1, LANES), F32)
    return pl.pallas_call(
        body, name=name, grid=(s_len // bs, n_heads),
        in_specs=[head, blk(lambda i, h: (i, n_heads + h)), blk(lambda i, h: (i, 2 * h)),
                  blk(lambda i, h: (i, krope_col)), gspec, gspec, tspec, tspec, cat, cat, head],
        out_specs=[head, head, cat, tspec, gspec, gspec],
        out_shape=[hshape, hshape, jax.ShapeDtypeStruct((s_len, n_heads * 2 * LANES), BF16),
                   jax.ShapeDtypeStruct((s_len, LANES), F32), gshape, gshape],
        compiler_params=_params("arbitrary", "arbitrary"))(qp, qp, kv, c, gq, gk, cos_t, sin_t, dqc, dkc, dv)


def _mla_latent_fwd(c, ga, *, name, br=256):
    s_len = c.shape[0]
    br = _div_block(s_len, br, 8)

    def body(c_ref, g_ref, o_ref):
        for part in range(2):
            sl = slice(part * MLA_Q_RANK, (part + 1) * MLA_Q_RANK)
            o_ref[:, sl] = _head_norm(c_ref[:, sl], g_ref[:, sl]).astype(BF16)

    w = MLA_Q_RANK + MLA_KV_RANK
    return pl.pallas_call(
        body, name=name, grid=(s_len // br,),
        in_specs=[pl.BlockSpec((br, w), lambda i: (i, 0)), pl.BlockSpec((1, w), lambda i: (0, 0))],
        out_specs=pl.BlockSpec((br, w), lambda i: (i, 0)), out_shape=jax.ShapeDtypeStruct((s_len, w), BF16),
        compiler_params=_params("parallel"))(c, ga)


def _mla_latent_bwd(c, ga, dcn_q, dcn_kv, dk_rope, *, name, br=256):
    s_len, cw = c.shape
    br = _div_block(s_len, br, 8)
    w = MLA_Q_RANK + MLA_KV_RANK

    def body(c_ref, g_ref, dq_ref, dkv_ref, dkr_ref, dc_ref, dg_ref):
        @pl.when(pl.program_id(0) == 0)
        def _():
            dg_ref[...] = jnp.zeros_like(dg_ref)

        for part, d_ref in enumerate((dq_ref, dkv_ref)):
            sl = slice(part * MLA_Q_RANK, (part + 1) * MLA_Q_RANK)
            dx, dg = _rms_bwd_math(c_ref[:, sl].astype(F32), g_ref[:, sl], d_ref[...].astype(F32), MLA_Q_RANK)
            dc_ref[:, sl] = dx.astype(BF16)
            dg_ref[:, sl] += jnp.sum(dg, axis=0, keepdims=True)
        dc_ref[:, w:] = dkr_ref[...].astype(BF16)

    return pl.pallas_call(
        body, name=name, grid=(s_len // br,),
        in_specs=[pl.BlockSpec((br, w), lambda i: (i, 0)), pl.BlockSpec((1, w), lambda i: (0, 0)),
                  pl.BlockSpec((br, MLA_Q_RANK), lambda i: (i, 0)), pl.BlockSpec((br, MLA_KV_RANK), lambda i: (i, 0)),
                  pl.BlockSpec((br, LANES), lambda i: (i, 0))],
        out_specs=[pl.BlockSpec((br, cw), lambda i: (i, 0)), pl.BlockSpec((1, w), lambda i: (0, 0))],
        out_shape=[jax.ShapeDtypeStruct((s_len, cw), BF16), jax.ShapeDtypeStruct((1, w), F32)],
        compiler_params=_params("arbitrary"))(c, ga, dcn_q, dcn_kv, dk_rope)


_GELU_C = 0.7978845608028654


def _gelu(x):
    return 0.5 * x * (1.0 + jnp.tanh(_GELU_C * (x + 0.044715 * x * x * x)))


def _gelu_grad(x):
    t = jnp.tanh(_GELU_C * (x + 0.044715 * x * x * x))
    return 0.5 * (1.0 + t) + 0.5 * x * (1.0 - t * t) * _GELU_C * (1.0 + 3 * 0.044715 * x * x)


def _sgu_act_fwd(uv, vg, *, name, br=256):
    s_len, w2 = uv.shape
    w = w2 // 2
    br = _div_block(s_len, br, 8)

    def body(uv_ref, g_ref, u_ref, v_ref):
        u_ref[...] = _gelu(uv_ref[:, :w])
        v_ref[...] = _head_norm(_gelu(uv_ref[:, w:]), g_ref[...]).astype(BF16)

    row = lambda c: pl.BlockSpec((br, c), lambda i: (i, 0))
    return pl.pallas_call(
        body, name=name, grid=(s_len // br,), in_specs=[row(w2), pl.BlockSpec((1, w), lambda i: (0, 0))],
        out_specs=[row(w), row(w)], out_shape=[jax.ShapeDtypeStruct((s_len, w), F32), jax.ShapeDtypeStruct((s_len, w), BF16)],
        compiler_params=_params("parallel"))(uv, vg)


def _sgu_act_bwd(uv, vg, du, dvn, *, name, br=256):
    s_len, w2 = uv.shape
    w = w2 // 2
    br = _div_block(s_len, br, 8)

    def body(uv_ref, g_ref, du_ref, dvn_ref, duv_ref, dg_ref):
        @pl.when(pl.program_id(0) == 0)
        def _():
            dg_ref[...] = jnp.zeros_like(dg_ref)

        up, vp = uv_ref[:, :w], uv_ref[:, w:]
        duv_ref[:, :w] = (du_ref[...] * _gelu_grad(up)).astype(BF16)
        dva, dg = _rms_bwd_math(_gelu(vp), g_ref[...], dvn_ref[...], w)
        dg_ref[...] += jnp.sum(dg, axis=0, keepdims=True)
        duv_ref[:, w:] = (dva * _gelu_grad(vp)).astype(BF16)

    row = lambda c: pl.BlockSpec((br, c), lambda i: (i, 0))
    vec = pl.BlockSpec((1, w), lambda i: (0, 0))
    return pl.pallas_call(
        body, name=name, grid=(s_len // br,), in_specs=[row(w2), vec, row(w), row(w)], out_specs=[row(w2), vec],
        out_shape=[jax.ShapeDtypeStruct((s_len, w2), BF16), jax.ShapeDtypeStruct((1, w), F32)],
        compiler_params=_params("arbitrary"))(uv, vg, du, dvn)


def _tril_weights(ws_ref):
    t, s = _iota((SGU_CHUNK, SGU_CHUNK), 0), _iota((SGU_CHUNK, SGU_CHUNK), 1)
    keep = s <= t
    return jnp.where(keep, ws_ref[...], 0.0), keep


def _sgu_mix_fwd(u, vn, w_s, b_s, *, name):
    s_len, w = u.shape
    nc = s_len // SGU_CHUNK

    def body(u_ref, v_ref, ws_ref, b_ref, o_ref):
        wm = _tril_weights(ws_ref)[0].astype(BF16)
        for n in range(nc):
            rows = slice(n * SGU_CHUNK, (n + 1) * SGU_CHUNK)
            mixed = _dot(wm, v_ref[rows, :], 1, 0) + b_ref[...]
            o_ref[rows, :] = (u_ref[rows, :] * mixed).astype(BF16)

    col = pl.BlockSpec((s_len, LANES), lambda g: (0, g))
    return pl.pallas_call(
        body, name=name, grid=(w // LANES,),
        in_specs=[col, col, pl.BlockSpec((None, SGU_CHUNK, SGU_CHUNK), lambda g: (g, 0, 0)),
                  pl.BlockSpec((None, SGU_CHUNK, 1), lambda g: (g, 0, 0))],
        out_specs=col, out_shape=jax.ShapeDtypeStruct((s_len, w), BF16),
        compiler_params=_params("parallel"))(u, vn, w_s, b_s)


def _sgu_mix_bwd(u, vn, w_s, b_s, dgated, *, name):
    s_len, w = u.shape
    nc = s_len // SGU_CHUNK

    def body(u_ref, v_ref, ws_ref, b_ref, dg_ref, du_ref, dv_ref, dws_ref, dbs_ref):
        wf, keep = _tril_weights(ws_ref)
        wm = wf.astype(BF16)
        wmt = wf.T.astype(BF16)
        dws = jnp.zeros((SGU_CHUNK, SGU_CHUNK), F32)
        dbs = jnp.zeros((SGU_CHUNK, 1), F32)
        for n in range(nc):
            rows = slice(n * SGU_CHUNK, (n + 1) * SGU_CHUNK)
            vb = v_ref[rows, :]
            dgv = dg_ref[rows, :].astype(F32)
            mixed = _dot(wm, vb, 1, 0) + b_ref[...]
            du_ref[rows, :] = dgv * mixed
            dm = dgv * u_ref[rows, :]
            dmb = dm.astype(BF16)
            dws = dws + _dot(dmb, vb, 1, 1)
            dbs = dbs + jnp.sum(dm, axis=1, keepdims=True)
            dv_ref[rows, :] = _dot(wmt, dmb, 1, 0)
        dws_ref[...] = jnp.where(keep, dws, 0.0)
        dbs_ref[...] = dbs

    col = pl.BlockSpec((s_len, LANES), lambda g: (0, g))
    wspec = pl.BlockSpec((None, SGU_CHUNK, SGU_CHUNK), lambda g: (g, 0, 0))
    bspec = pl.BlockSpec((None, SGU_CHUNK, 1), lambda g: (g, 0, 0))
    return pl.pallas_call(
        body, name=name, grid=(w // LANES,), in_specs=[col, col, wspec, bspec, col],
        out_specs=[col, col, wspec, bspec],
        out_shape=[jax.ShapeDtypeStruct((s_len, w), F32), jax.ShapeDtypeStruct((s_len, w), F32),
                   jax.ShapeDtypeStruct(w_s.shape, F32), jax.ShapeDtypeStruct(b_s.shape, F32)],
        compiler_params=_params("parallel"))(u, vn, w_s, b_s, dgated)


def _shift_down(x, k):
    if k == 0:
        return x
    return jnp.where(_iota(x.shape, 0) >= k, pltpu.roll(x, k, axis=0), 0.0)


def _shift_up(x, k):
    if k == 0:
        return x
    n = x.shape[0]
    return jnp.where(_iota(x.shape, 0) < n - k, pltpu.roll(x, n - k, axis=0), 0.0)


def _conv(u, w_ref, b_ref):
    return b_ref[...] + w_ref[0:1, :] * _shift_down(u, 2) + w_ref[1:2, :] * _shift_down(u, 1) + w_ref[2:3, :] * u


def _sigmoid(x):
    return 0.5 * jnp.tanh(0.5 * x) + 0.5


def _glu_fwd(up, cw, cb, *, name, bc=256):
    s_len, f2 = up.shape
    f = f2 // 2
    bc = _div_block(f, bc)
    nf = f // bc

    def body(ug_ref, uv_ref, wg_ref, wv_ref, bg_ref, bv_ref, o_ref):
        yg = _conv(ug_ref[...], wg_ref, bg_ref)
        yv = _conv(uv_ref[...], wv_ref, bv_ref)
        o_ref[...] = (yg * _sigmoid(yg) * yv).astype(BF16)

    big = lambda off: pl.BlockSpec((s_len, bc), lambda j: (0, j + off))
    wsp = lambda off: pl.BlockSpec((3, bc), lambda j: (0, j + off))
    bsp = lambda off: pl.BlockSpec((1, bc), lambda j: (0, j + off))
    return pl.pallas_call(
        body, name=name, grid=(nf,), in_specs=[big(0), big(nf), wsp(0), wsp(nf), bsp(0), bsp(nf)],
        out_specs=pl.BlockSpec((s_len, bc), lambda j: (0, j)), out_shape=jax.ShapeDtypeStruct((s_len, f), BF16),
        compiler_params=_params("parallel"))(up, up, cw, cw, cb, cb)


def _glu_bwd(up, cw, cb, dact, *, name, bc=256):
    s_len, f2 = up.shape
    f = f2 // 2
    bc = _div_block(f, bc)
    nf = f // bc

    def body(ug_ref, uv_ref, wg_ref, wv_ref, bg_ref, bv_ref, da_ref, du_ref, dw_ref, db_ref):
        ug, uv = ug_ref[...], uv_ref[...]
        yg = _conv(ug, wg_ref, bg_ref)
        yv = _conv(uv, wv_ref, bv_ref)
        da = da_ref[...].astype(F32)
        sg = _sigmoid(yg)
        planes = ((da * yv * (sg * (1.0 + yg * (1.0 - sg))), ug, wg_ref), (da * (yg * sg), uv, wv_ref))
        for plane, (dy, u, w_ref) in enumerate(planes):
            dy1, dy2 = _shift_up(dy, 1), _shift_up(dy, 2)
            db_ref[plane] = jnp.sum(dy, axis=0, keepdims=True)
            dw_ref[plane, 0:1, :] = jnp.sum(dy2 * u, axis=0, keepdims=True)
            dw_ref[plane, 1:2, :] = jnp.sum(dy1 * u, axis=0, keepdims=True)
            dw_ref[plane, 2:3, :] = jnp.sum(dy * u, axis=0, keepdims=True)
            du_ref[plane] = (w_ref[2:3, :] * dy + w_ref[1:2, :] * dy1 + w_ref[0:1, :] * dy2).astype(BF16)

    big = lambda off: pl.BlockSpec((s_len, bc), lambda j: (0, j + off))
    wsp = lambda off: pl.BlockSpec((3, bc), lambda j: (0, j + off))
    bsp = lambda off: pl.BlockSpec((1, bc), lambda j: (0, j + off))
    planes = lambda r: pl.BlockSpec((2, r, bc), lambda j: (0, 0, j))
    return pl.pallas_call(
        body, name=name, grid=(nf,),
        in_specs=[big(0), big(nf), wsp(0), wsp(nf), bsp(0), bsp(nf), pl.BlockSpec((s_len, bc), lambda j: (0, j))],
        out_specs=[planes(s_len), planes(3), planes(1)],
        out_shape=[jax.ShapeDtypeStruct((2, s_len, f), BF16), jax.ShapeDtypeStruct((2, 3, f), F32),
                   jax.ShapeDtypeStruct((2, 1, f), F32)],
        compiler_params=_params("parallel"))(up, up, cw, cw, cb, cb, dact)


def _as2d(a):
    return a.reshape(-1, a.shape[-1]) if a.ndim >= 2 else a.reshape(1, -1)


def _adamw(w, g, m, v, *, name, after=None, target_bytes=1 << 20):
    shape = w.shape
    w2, m2, v2 = _as2d(w), _as2d(m), _as2d(v)
    g2 = g.reshape(w2.shape)
    r, c = w2.shape
    br = r if r * c * 4 <= target_bytes else _div_block(r, max(8, target_bytes // (4 * c) // 8 * 8), 8)
    c1 = 1.0 - ADAM_B1 ** ADAM_STEP
    c2 = 1.0 - ADAM_B2 ** ADAM_STEP

    def body(w_ref, g_ref, m_ref, v_ref, *rest):
        d_ref, nm_ref, nv_ref = rest[-3:]
        gv = g_ref[...]
        nm = ADAM_B1 * m_ref[...] + (1.0 - ADAM_B1) * gv
        nv = ADAM_B2 * v_ref[...] + (1.0 - ADAM_B2) * (gv * gv)
        nm_ref[...] = nm
        nv_ref[...] = nv
        d_ref[...] = -ADAM_LR * ((nm / c1) / (jnp.sqrt(nv / c2) + ADAM_EPS) + ADAM_WD * w_ref[...])

    spec = pl.BlockSpec((br, c), lambda i: (i, 0))
    sds = jax.ShapeDtypeStruct((r, c), F32)
    d, nm, nv = pl.pallas_call(
        body, name=name, grid=(r // br,), in_specs=[spec] * 4 + _after_spec(after), out_specs=[spec] * 3,
        out_shape=[sds] * 3, compiler_params=_params("parallel"))(w2, g2, m2, v2, *_after_arg(after))
    return d.reshape(shape), nm.reshape(shape), nv.reshape(shape)


def _add_halves(g, recv, place, *, name, target_bytes=SHARD_BLOCK_BYTES):
    _, _, r, c = g.shape
    br = _div_block(r, max(16, target_bytes // (2 * c) // 16 * 16), 16)

    def body(x_ref, y_ref, c_ref, g_ref, r_ref, o_ref):
        o_ref[...] = (g_ref[...].astype(F32) + r_ref[...].astype(F32)).astype(BF16)

    return pl.pallas_call(
        body, name=name,
        grid_spec=pltpu.PrefetchScalarGridSpec(
            num_scalar_prefetch=3, grid=(N_CHIPS, r // br),
            in_specs=[pl.BlockSpec((None, None, br, c), lambda s, i, xr, yr, cr: (s, cr[0], i, 0)),
                      pl.BlockSpec((None, br, c), lambda s, i, xr, yr, cr: (s, i, 0))],
            out_specs=pl.BlockSpec((None, br, c), lambda s, i, xr, yr, cr: (s, i, 0))),
        out_shape=jax.ShapeDtypeStruct((N_CHIPS, r, c), BF16),
        compiler_params=_params("parallel", "parallel"))(*place, g, recv)


def _sum_chips(p, landed, place, *, name, target_bytes=SHARD_BLOCK_BYTES):
    _, r, c = p.shape
    br = _div_block(r, max(16, target_bytes // (4 * c) // 16 * 16), 16)

    def body(x_ref, y_ref, c_ref, p_ref, l1_ref, l2_ref, l3_ref, o_ref):
        o_ref[...] = ((p_ref[...].astype(F32) + l1_ref[...].astype(F32)) + l2_ref[...].astype(F32)) + l3_ref[...].astype(F32)

    slot = lambda k: pl.BlockSpec((None, br, c), lambda i, xr, yr, cr: ((2 * xr[0] + yr[0] + k) % N_CHIPS, i, 0))
    return pl.pallas_call(
        body, name=name,
        grid_spec=pltpu.PrefetchScalarGridSpec(
            num_scalar_prefetch=3, grid=(r // br,), in_specs=[slot(0), slot(1), slot(2), slot(3)],
            out_specs=pl.BlockSpec((None, br, c), lambda i, xr, yr, cr: (cr[0], i, 0))),
        out_shape=jax.ShapeDtypeStruct((2, r, c), F32),
        compiler_params=_params("parallel"))(*place, p, landed, landed, landed)


def _place_shard(w, place, *, dtype, name, layer=None, after=None, target_bytes=SHARD_BLOCK_BYTES):
    r, c = w.shape[-2:]
    hr = r // 2
    mult = 16 if dtype == BF16 else 8
    br = _div_block(hr, max(mult, target_bytes // (4 * c) // mult * mult), mult)
    nb = hr // br

    def body(x_ref, y_ref, c_ref, w_ref, *rest):
        rest[-1][...] = w_ref[...].astype(dtype)

    if layer is None:
        w_spec = pl.BlockSpec((br, c), lambda h, i, xr, yr, cr: (h * nb + i, 0))
    else:
        w_spec = pl.BlockSpec((None, br, c), lambda h, i, xr, yr, cr: (layer, h * nb + i, 0))
    return pl.pallas_call(
        body, name=name,
        grid_spec=pltpu.PrefetchScalarGridSpec(
            num_scalar_prefetch=3, grid=(2, nb), in_specs=[w_spec] + _after_spec(after),
            out_specs=pl.BlockSpec((None, None, br, c), lambda h, i, xr, yr, cr: (2 * xr[0] + yr[0], h, i, 0))),
        out_shape=jax.ShapeDtypeStruct((N_CHIPS, 2, hr, c), dtype),
        compiler_params=_params("parallel", "parallel"))(*place, w, *_after_arg(after))


def _sum_devices(x, *, name):
    n, r, c = x.shape
    br = _div_block(r, 512, 8)

    def body(x_ref, o_ref):
        acc = x_ref[0]
        for s in range(1, n):
            acc = acc + x_ref[s]
        o_ref[...] = acc

    return pl.pallas_call(
        body, name=name, grid=(r // br,), in_specs=[pl.BlockSpec((n, br, c), lambda i: (0, i, 0))],
        out_specs=pl.BlockSpec((br, c), lambda i: (i, 0)), out_shape=jax.ShapeDtypeStruct((r, c), F32),
        compiler_params=_params("parallel"))(x)


_ANY = pl.BlockSpec(memory_space=pl.ANY)


def _place():
    x, y, c = lax.axis_index("x"), lax.axis_index("y"), lax.axis_index("c")
    other_chips = [(1 - x, y), (x, 1 - y), (1 - x, 1 - y)]
    return x, y, c, other_chips


_HBM = pl.BlockSpec(memory_space=pltpu.HBM)
_SEM = pl.BlockSpec(memory_space=pltpu.SEMAPHORE)
_EFFECT = pltpu.SideEffectType.DATAFLOW_SIDE_EFFECTING


def _in_hbm(a):
    return pltpu.with_memory_space_constraint(a, pltpu.HBM)


def _token_spec():
    return pl.BlockSpec(memory_space=pltpu.VMEM), jax.ShapeDtypeStruct((8, LANES), F32)


def _gather_ici_start(bufs, after, *, name):
    n = len(bufs)

    def body(*refs):
        b_refs = refs[:n]
        send_sems, recv_sems = refs[n + 1], refs[n + 2]
        token = refs[-1]
        x, y, c, chips = _place()
        me = 2 * x + y
        for i in range(n):
            for j, (px, py) in enumerate(chips):
                pltpu.make_async_remote_copy(src_ref=b_refs[i].at[me, c], dst_ref=b_refs[i].at[me, c],
                                             send_sem=send_sems.at[3 * i + j], recv_sem=recv_sems.at[3 * i + j],
                                             device_id=(px, py, c), device_id_type=MESH).start()
        token[...] = jnp.zeros_like(token)

    tspec, tshape = _token_spec()
    outs = pl.pallas_call(
        body, name=name, in_specs=[_HBM] * n + [_ANY], out_specs=(_SEM, _SEM, *[_HBM] * n, tspec),
        out_shape=(pltpu.SemaphoreType.DMA((3 * n,)), pltpu.SemaphoreType.DMA((3 * n,)),
                   *[pltpu.HBM(a.shape, a.dtype) for a in bufs], tshape),
        input_output_aliases={i: 2 + i for i in range(n)},
        compiler_params=pltpu.CompilerParams(has_side_effects=_EFFECT),
    )(*[_in_hbm(a) for a in bufs], after)
    return outs[0], outs[1], list(outs[2:2 + n]), outs[-1]


def _gather_ici_wait(send_sems, recv_sems, bufs, after, *, name):
    n = len(bufs)

    def body(*refs):
        b_refs = refs[:n]
        send_sems, recv_sems = refs[n], refs[n + 1]
        x, y, c, chips = _place()
        me = 2 * x + y
        for i in range(n):
            for j, (px, py) in enumerate(chips):
                cp = pltpu.make_async_remote_copy(src_ref=b_refs[i].at[me, c], dst_ref=b_refs[i].at[2 * px + py, c],
                                                  send_sem=send_sems.at[3 * i + j], recv_sem=recv_sems.at[3 * i + j],
                                                  device_id=(px, py, c), device_id_type=MESH)
                cp.wait_send()
                cp.wait_recv()

    outs = pl.pallas_call(
        body, name=name, in_specs=[_HBM] * n + [_SEM, _SEM, _ANY], out_specs=[_HBM] * n,
        out_shape=[pltpu.HBM(a.shape, a.dtype) for a in bufs], input_output_aliases={i: i for i in range(n)},
        compiler_params=pltpu.CompilerParams(has_side_effects=_EFFECT),
    )(*bufs, send_sems, recv_sems, after)
    return list(outs)


def _gather_d2d(bufs, *, name):
    n = len(bufs)

    def body(*refs):
        b_refs = refs[n:2 * n]
        send_sems, recv_sems = refs[2 * n:]
        x, y, c, chips = _place()
        sends = []
        for i in range(n):
            for j, (px, py) in enumerate(chips):
                mine = b_refs[i].at[2 * px + py, c]
                cp = pltpu.make_async_remote_copy(src_ref=mine, dst_ref=mine, send_sem=send_sems.at[3 * i + j],
                                                  recv_sem=recv_sems.at[3 * i + j], device_id=(x, y, 1 - c),
                                                  device_id_type=MESH)
                cp.start()
                sends.append((cp, i, j, px, py))
        for cp, i, j, px, py in sends:
            theirs = b_refs[i].at[2 * px + py, 1 - c]
            pltpu.make_async_remote_copy(src_ref=theirs, dst_ref=theirs, send_sem=send_sems.at[3 * i + j],
                                         recv_sem=recv_sems.at[3 * i + j], device_id=(x, y, 1 - c),
                                         device_id_type=MESH).wait_recv()
            cp.wait_send()

    return pl.pallas_call(
        body, name=name, in_specs=[_ANY] * n, out_specs=[_ANY] * n, input_output_aliases={i: i for i in range(n)},
        out_shape=[jax.ShapeDtypeStruct(a.shape, a.dtype) for a in bufs],
        scratch_shapes=[pltpu.SemaphoreType.DMA((3 * n,)), pltpu.SemaphoreType.DMA((3 * n,))],
    )(*bufs)


def _sibling_halves_start(gs, after, *, name):
    n = len(gs)

    def body(*refs):
        g_refs, r_refs = refs[:n], refs[n:2 * n]
        send_sems, recv_sems = refs[2 * n + 1], refs[2 * n + 2]
        token = refs[-1]
        x, y, c, _ = _place()
        for i in range(n):
            for s in range(N_CHIPS):
                k = i * N_CHIPS + s
                pltpu.make_async_remote_copy(src_ref=g_refs[i].at[s, 1 - c], dst_ref=r_refs[i].at[s],
                                             send_sem=send_sems.at[k], recv_sem=recv_sems.at[k],
                                             device_id=(x, y, 1 - c), device_id_type=MESH).start()
        token[...] = jnp.zeros_like(token)

    tspec, tshape = _token_spec()
    rshapes = [(N_CHIPS,) + g.shape[2:] for g in gs]
    recvs = [_in_hbm(lax.empty(sh, g.dtype)) for sh, g in zip(rshapes, gs)]
    outs = pl.pallas_call(
        body, name=name, in_specs=[_HBM] * (2 * n) + [_ANY], out_specs=(_SEM, _SEM, *[_HBM] * (2 * n), tspec),
        out_shape=(pltpu.SemaphoreType.DMA((N_CHIPS * n,)), pltpu.SemaphoreType.DMA((N_CHIPS * n,)),
                   *[pltpu.HBM(g.shape, g.dtype) for g in gs], *[pltpu.HBM(sh, g.dtype) for sh, g in zip(rshapes, gs)],
                   tshape),
        input_output_aliases={i: 2 + i for i in range(2 * n)},
        compiler_params=pltpu.CompilerParams(has_side_effects=_EFFECT),
    )(*[_in_hbm(g) for g in gs], *recvs, after)
    return outs[0], outs[1], list(outs[2:2 + n]), list(outs[2 + n:2 + 2 * n]), outs[-1]


def _sibling_halves_wait(send_sems, recv_sems, gs, recvs, after, *, name):
    n = len(gs)

    def body(*refs):
        g_refs, r_refs = refs[:n], refs[n:2 * n]
        send_sems, recv_sems = refs[2 * n], refs[2 * n + 1]
        x, y, c, _ = _place()
        for i in range(n):
            for s in range(N_CHIPS):
                k = i * N_CHIPS + s
                cp = pltpu.make_async_remote_copy(src_ref=g_refs[i].at[s, 1 - c], dst_ref=r_refs[i].at[s],
                                                  send_sem=send_sems.at[k], recv_sem=recv_sems.at[k],
                                                  device_id=(x, y, 1 - c), device_id_type=MESH)
                cp.wait_send()
                cp.wait_recv()

    outs = pl.pallas_call(
        body, name=name, in_specs=[_HBM] * (2 * n) + [_SEM, _SEM, _ANY], out_specs=[_HBM] * (2 * n),
        out_shape=[pltpu.HBM(a.shape, a.dtype) for a in list(gs) + list(recvs)],
        input_output_aliases={i: i for i in range(2 * n)},
        compiler_params=pltpu.CompilerParams(has_side_effects=_EFFECT),
    )(*gs, *recvs, send_sems, recv_sems, after)
    return list(outs[:n]), list(outs[n:])


def _chip_scatter_start(ps, after, *, name):
    n = len(ps)

    def body(*refs):
        p_refs, l_refs = refs[:n], refs[n:2 * n]
        send_sems, recv_sems = refs[2 * n + 1], refs[2 * n + 2]
        token = refs[-1]
        x, y, c, chips = _place()
        me = 2 * x + y
        for i in range(n):
            for j, (px, py) in enumerate(chips):
                pltpu.make_async_remote_copy(src_ref=p_refs[i].at[2 * px + py], dst_ref=l_refs[i].at[me],
                                             send_sem=send_sems.at[3 * i + j], recv_sem=recv_sems.at[3 * i + j],
                                             device_id=(px, py, c), device_id_type=MESH).start()
        token[...] = jnp.zeros_like(token)

    tspec, tshape = _token_spec()
    lands = [_in_hbm(lax.empty(p.shape, p.dtype)) for p in ps]
    outs = pl.pallas_call(
        body, name=name, in_specs=[_HBM] * (2 * n) + [_ANY], out_specs=(_SEM, _SEM, *[_HBM] * (2 * n), tspec),
        out_shape=(pltpu.SemaphoreType.DMA((3 * n,)), pltpu.SemaphoreType.DMA((3 * n,)),
                   *[pltpu.HBM(p.shape, p.dtype) for p in ps], *[pltpu.HBM(p.shape, p.dtype) for p in ps], tshape),
        input_output_aliases={i: 2 + i for i in range(2 * n)},
        compiler_params=pltpu.CompilerParams(has_side_effects=_EFFECT),
    )(*[_in_hbm(p) for p in ps], *lands, after)
    return outs[0], outs[1], list(outs[2:2 + n]), list(outs[2 + n:2 + 2 * n]), outs[-1]


def _chip_scatter_wait(send_sems, recv_sems, ps, lands, after, *, name):
    n = len(ps)

    def body(*refs):
        p_refs, l_refs = refs[:n], refs[n:2 * n]
        send_sems, recv_sems = refs[2 * n], refs[2 * n + 1]
        x, y, c, chips = _place()
        for i in range(n):
            for j, (px, py) in enumerate(chips):
                cp = pltpu.make_async_remote_copy(src_ref=p_refs[i].at[2 * px + py], dst_ref=l_refs[i].at[2 * px + py],
                                                  send_sem=send_sems.at[3 * i + j], recv_sem=recv_sems.at[3 * i + j],
                                                  device_id=(px, py, c), device_id_type=MESH)
                cp.wait_send()
                cp.wait_recv()

    outs = pl.pallas_call(
        body, name=name, in_specs=[_HBM] * (2 * n) + [_SEM, _SEM, _ANY], out_specs=[_HBM] * (2 * n),
        out_shape=[pltpu.HBM(p.shape, p.dtype) for p in ps] * 2, input_output_aliases={i: i for i in range(2 * n)},
        compiler_params=pltpu.CompilerParams(has_side_effects=_EFFECT),
    )(*ps, *lands, send_sems, recv_sems, after)
    return list(outs[:n]), list(outs[n:])


def _sibling_share_start(bufs, after, *, name):
    n = len(bufs)

    def body(*refs):
        b_refs = refs[:n]
        send_sems, recv_sems = refs[n + 1], refs[n + 2]
        token = refs[-1]
        x, y, c, _ = _place()
        for i in range(n):
            pltpu.make_async_remote_copy(src_ref=b_refs[i].at[c], dst_ref=b_refs[i].at[c], send_sem=send_sems.at[i],
                                         recv_sem=recv_sems.at[i], device_id=(x, y, 1 - c), device_id_type=MESH).start()
        token[...] = jnp.zeros_like(token)

    tspec, tshape = _token_spec()
    outs = pl.pallas_call(
        body, name=name, in_specs=[_HBM] * n + [_ANY], out_specs=(_SEM, _SEM, *[_HBM] * n, tspec),
        out_shape=(pltpu.SemaphoreType.DMA((n,)), pltpu.SemaphoreType.DMA((n,)),
                   *[pltpu.HBM(a.shape, a.dtype) for a in bufs], tshape),
        input_output_aliases={i: 2 + i for i in range(n)},
        compiler_params=pltpu.CompilerParams(has_side_effects=_EFFECT),
    )(*[_in_hbm(a) for a in bufs], after)
    return outs[0], outs[1], list(outs[2:2 + n]), outs[-1]


def _sibling_share_wait(send_sems, recv_sems, bufs, after, *, name):
    n = len(bufs)

    def body(*refs):
        b_refs = refs[:n]
        send_sems, recv_sems = refs[n], refs[n + 1]
        x, y, c, _ = _place()
        for i in range(n):
            cp = pltpu.make_async_remote_copy(src_ref=b_refs[i].at[c], dst_ref=b_refs[i].at[1 - c],
                                              send_sem=send_sems.at[i], recv_sem=recv_sems.at[i],
                                              device_id=(x, y, 1 - c), device_id_type=MESH)
            cp.wait_send()
            cp.wait_recv()

    outs = pl.pallas_call(
        body, name=name, in_specs=[_HBM] * n + [_SEM, _SEM, _ANY], out_specs=[_HBM] * n,
        out_shape=[pltpu.HBM(a.shape, a.dtype) for a in bufs], input_output_aliases={i: i for i in range(n)},
        compiler_params=pltpu.CompilerParams(has_side_effects=_EFFECT),
    )(*bufs, send_sems, recv_sems, after)
    return list(outs)


def _broadcast_all(v, after, *, name):
    def body(v_ref, after_ref, o_ref, send_sems, recv_sems, local_sem):
        x, y, c, _ = _place()
        me = 4 * x + 2 * y + c
        loc = pltpu.make_async_copy(v_ref, o_ref.at[me], local_sem)
        loc.start()
        copies = []
        for k in range(1, 8):
            dx, dy, dc = (k >> 2) & 1, (k >> 1) & 1, k & 1
            to = (1 - x if dx else x, 1 - y if dy else y, 1 - c if dc else c)
            cp = pltpu.make_async_remote_copy(src_ref=v_ref, dst_ref=o_ref.at[me], send_sem=send_sems.at[k - 1],
                                              recv_sem=recv_sems.at[k - 1], device_id=to, device_id_type=MESH)
            cp.start()
            copies.append((cp, k, to))
        for cp, k, to in copies:
            cp.wait_send()
            theirs = o_ref.at[4 * to[0] + 2 * to[1] + to[2]]
            pltpu.make_async_remote_copy(src_ref=theirs, dst_ref=theirs, send_sem=send_sems.at[k - 1],
                                         recv_sem=recv_sems.at[k - 1], device_id=to, device_id_type=MESH).wait_recv()
        loc.wait()

    return pl.pallas_call(
        body, name=name, in_specs=[_ANY, _ANY], out_specs=_ANY,
        out_shape=jax.ShapeDtypeStruct((8,) + v.shape, v.dtype),
        scratch_shapes=[pltpu.SemaphoreType.DMA((7,)), pltpu.SemaphoreType.DMA((7,)), pltpu.SemaphoreType.DMA(())],
    )(v, after)


def _gather_place(shards, place, *, name, after=None):
    names = list(shards)
    bufs, shapes = [], []
    for k in names:
        w, layer = shards[k] if isinstance(shards[k], tuple) else (shards[k], None)
        bufs.append(_place_shard(w, place, dtype=F32 if k == 'small' else BF16, layer=layer, after=after,
                                 name=f"{name}_place_{k}"))
        after = bufs[-1] if after is not None else None
        shapes.append(w.shape[-2:])
    return names, shapes, bufs


def _gather_begin(placed, after, *, name):
    names, shapes, bufs = placed
    send_sems, recv_sems, bufs, token = _gather_ici_start(bufs, after, name=name + "_ici_start")
    return (names, shapes, send_sems, recv_sems, bufs), token


def _gather_end(state, after, *, name):
    names, shapes, send_sems, recv_sems, bufs = state
    bufs = _gather_ici_wait(send_sems, recv_sems, bufs, after, name=name + "_ici_wait")
    bufs = _gather_d2d(bufs, name=name + "_d2d")
    return {k: o.reshape((N_CHIPS,) + sh) for k, o, sh in zip(names, bufs, shapes)}


def _reduce_sibling_start(grads, after, *, name):
    names = list(grads)
    gs = [grads[k].reshape(N_CHIPS, 2, grads[k].shape[1] // 2, grads[k].shape[2]) for k in names]
    send_sems, recv_sems, gs, recvs, token = _sibling_halves_start(gs, after, name=name + "_sib_start")
    return (names, [grads[k].shape[1:] for k in names], send_sems, recv_sems, gs, recvs), token


def _reduce_begin(state, place, after, *, name):
    names, shapes, send_sems, recv_sems, gs, recvs = state
    gs, recvs = _sibling_halves_wait(send_sems, recv_sems, gs, recvs, after, name=name + "_sib_wait")
    ps = [_add_halves(g, r, place, name=f"{name}_add2_{k}") for g, r, k in zip(gs, recvs, names)]
    send_sems, recv_sems, ps, lands, token = _chip_scatter_start(ps, recvs[0], name=name + "_scatter_start")
    return (names, shapes, send_sems, recv_sems, ps, lands), token


def _reduce_end(state, place, after, *, name):
    names, shapes, send_sems, recv_sems, ps, lands = state
    ps, lands = _chip_scatter_wait(send_sems, recv_sems, ps, lands, after, name=name + "_scatter_wait")
    rs = [_sum_chips(p, l, place, name=f"{name}_sum4_{k}") for p, l, k in zip(ps, lands, names)]
    send_sems, recv_sems, rs, _ = _sibling_share_start(rs, lands[0], name=name + "_share_start")
    return names, shapes, send_sems, recv_sems, rs


def _reduce_finish(state, after, *, name):
    names, shapes, send_sems, recv_sems, rs = state
    both = _sibling_share_wait(send_sems, recv_sems, rs, after, name=name + "_share_wait")
    return {k: b.reshape(sh) for k, b, sh in zip(names, both, shapes)}


def _pad_lanes(a, n=LANES):
    return jnp.pad(a, [(0, 0)] * (a.ndim - 1) + [(0, n - a.shape[-1])])


def _unshard_cols(g):
    return jnp.transpose(g, (1, 0, 2)).reshape(g.shape[1], -1)


def _shard_cols(w):
    k, n = w.shape
    return jnp.transpose(w.reshape(k, N_CHIPS, n // N_CHIPS), (1, 0, 2))


def _ffn_fwd(h, p, tag):
    b = _rms_fwd(h, p['ffn_norm'], name=f"{tag}_ffn_norm")
    up = _mm(b, p['ffn_w_up'], b_sh='n', name=f"{tag}_ffn_up", bn=1408)
    act = _glu_fwd(up, p['ffn_conv_w'], p['ffn_conv_b'], name=f"{tag}_ffn_glu")
    out = _mm(act, p['ffn_w_down'], res=h, name=f"{tag}_ffn_down", bk=704)
    return out, (h, b, up, act)


def _ffn_bwd(dh, saved, p, tag, after, on_big):
    h, b, up, act = saved
    dact = _mm(dh, p['ffn_w_down'], tb=True, after=after, out_dtype=BF16, name=f"{tag}_ffn_dact", bn=1408)
    dw_down = _mm(act, dh, ta=True, after=after, out_dtype=BF16, name=f"{tag}_ffn_dwdown", bm=1408)
    dup, dcw, dcb = _glu_bwd(up, p['ffn_conv_w'], p['ffn_conv_b'], dact, name=f"{tag}_ffn_dglu")
    dw_up = _mm(b, dup, ta=True, b_sh='n', o_sh=True, out_dtype=BF16, name=f"{tag}_ffn_dwup", bn=1408)
    sent = on_big({'ffn_w_up': dw_up, 'ffn_w_down': dw_down.reshape(N_CHIPS, -1, dw_down.shape[1])})
    db = _mm(dup, p['ffn_w_up'], a_sh=True, b_sh='k', after=sent, name=f"{tag}_ffn_db", bk=1408)
    dcw = jnp.transpose(dcw, (1, 0, 2)).reshape(dcw.shape[1], -1)
    dcb = dcb.reshape(1, -1)
    dh_in, dg = _rms_bwd(h, p['ffn_norm'], db, res=dh, name=f"{tag}_ffn_dnorm")
    small = {'ffn_norm': dg, 'ffn_conv_w': dcw, 'ffn_conv_b': dcb}
    return dh_in, small


def _qkv_attn_fwd(kind, h, p, tag, n_heads):
    a = _rms_fwd(h, p['mix_norm'], name=f"{tag}_norm")
    if kind == 'fox':
        qkv = _mm(a, p['w_in'], name=f"{tag}_qkv", bn=896)
        cum = _fgate_fwd(qkv, p['b_f'], fcol=3 * n_heads, name=f"{tag}_fgate")
        cum_t = cum[:, :n_heads].T
        cq, ck = cum_t[:, :, None], cum_t[:, None, :]
    else:
        qkv = _mm(a, p['w_in'], b_sh='n', name=f"{tag}_qkv", bn=768)
        cq = ck = None
    cols = dict(qcol=lambda hh: hh, kcol=lambda hh: n_heads + hh, vcol=lambda hh: 2 * n_heads + hh)
    o = _attn_fwd(kind, qkv, qkv, qkv, name=f"{tag}_attn", n_heads=n_heads, dqk=HEAD_DIM, scale=HEAD_DIM ** -0.5,
                  gains=p['qk_gain'], cq=cq, ck=ck, **cols)
    out = _mm(o, p['w_out'], res=h, name=f"{tag}_out")
    return out, (h, a, qkv, o, cq, ck)


def _qkv_attn_bwd(kind, dh, saved, p, tag, n_heads, after, on_big):
    h, a, qkv, o, cq, ck = saved
    do = _mm(dh, p['w_out'], tb=True, after=after, out_dtype=BF16, name=f"{tag}_do")
    dw_out = _mm(o, dh, ta=True, after=after, out_dtype=BF16, name=f"{tag}_dwout")
    cols = dict(qcol=lambda hh: hh, kcol=lambda hh: n_heads + hh, vcol=lambda hh: 2 * n_heads + hh)
    outs = _attn_bwd(kind, qkv, qkv, qkv, o, do, name=f"{tag}_dattn", n_heads=n_heads, dqk=HEAD_DIM,
                     scale=HEAD_DIM ** -0.5, gains=p['qk_gain'], cq=cq, ck=ck, **cols)
    dq, dk, dv, dgain = outs[:4]
    small = {'q_gain': dgain[0], 'k_gain': dgain[1]}
    if kind == 'fox':
        dcq, dck = outs[4:]
        dca = _pad_lanes(dcq[:, :, 0].T)
        dcb = _pad_lanes(dck[:, 0, :].T)
        dflog, dbf = _fgate_bwd(qkv, p['b_f'], dca, dcb, fcol=3 * n_heads, n_heads=n_heads, name=f"{tag}_dfgate")
        small['b_f'] = dbf[:, :n_heads]
        dqkv = jnp.concatenate([dq, dk, dv, dflog], axis=1)
        dw_in = _mm(a, dqkv, ta=True, out_dtype=BF16, name=f"{tag}_dwin", bn=896)
        dw_in = _shard_cols(dw_in[:, :3 * n_heads * HEAD_DIM + n_heads])
    else:
        dqkv = jnp.concatenate([dq, dk, dv], axis=1)
        dw_in = _mm(a, dqkv, ta=True, o_sh=True, out_dtype=BF16, name=f"{tag}_dwin", bn=768)
    sent = on_big({'w_in': dw_in, 'w_out': dw_out.reshape(N_CHIPS, -1, dw_out.shape[1])})
    if kind == 'fox':
        da = _mm(dqkv, p['w_in'], tb=True, after=sent, name=f"{tag}_da", bk=896)
    else:
        da = _mm(dqkv, p['w_in'], b_sh='k', after=sent, name=f"{tag}_da", bk=768)
    dh_in, dg = _rms_bwd(h, p['mix_norm'], da, res=dh, name=f"{tag}_dnorm")
    small['mix_norm'] = dg
    return dh_in, small


def _mla_fwd(h, p, tag, n_heads):
    a = _rms_fwd(h, p['mix_norm'], name=f"{tag}_norm")
    c = _mm(a, p['w_in'], name=f"{tag}_latent", bn=1152)
    cn = _mla_latent_fwd(c, p['a_gain'], name=f"{tag}_latent_norm")
    qp = _mm(cn[:, :MLA_Q_RANK], p['w_q_b'], name=f"{tag}_q_up")
    kv = _mm(cn[:, MLA_Q_RANK:], p['w_kv_b'], b_sh='n', name=f"{tag}_kv_up")
    qc, kc = _mla_prep_fwd(qp, kv, c, p['gq'], p['gk'], p['cos'], p['sin'], n_heads=n_heads, name=f"{tag}_prep")
    cols = dict(qcol=lambda hh: hh, kcol=lambda hh: hh, vcol=lambda hh: 2 * hh + 1)
    scale = (MLA_NOPE + MLA_ROPE) ** -0.5
    o = _attn_fwd('mla', qc, kc, kv, name=f"{tag}_attn", n_heads=n_heads, dqk=2 * LANES, scale=scale, **cols)
    out = _mm(o, p['w_out'], res=h, name=f"{tag}_out")
    return out, (h, a, c, cn, qp, kv, qc, kc, o)


def _mla_bwd(dh, saved, p, tag, n_heads, after, on_big):
    h, a, c, cn, qp, kv, qc, kc, o = saved
    do = _mm(dh, p['w_out'], tb=True, after=after, out_dtype=BF16, name=f"{tag}_do")
    dw_out = _mm(o, dh, ta=True, after=after, out_dtype=BF16, name=f"{tag}_dwout")
    cols = dict(qcol=lambda hh: hh, kcol=lambda hh: hh, vcol=lambda hh: 2 * hh + 1)
    scale = (MLA_NOPE + MLA_ROPE) ** -0.5
    dqc, dkc, dv = _attn_bwd('mla', qc, kc, kv, o, do, name=f"{tag}_dattn", n_heads=n_heads, dqk=2 * LANES,
                             scale=scale, **cols)
    dqn, dqr, dkv, dkr, dgq, dgk = _mla_prep_bwd(qp, kv, c, p['gq'], p['gk'], p['cos'], p['sin'], dqc, dkc, dv,
                                                 n_heads=n_heads, name=f"{tag}_dprep")
    dqp = jnp.concatenate([dqn, dqr], axis=1)
    cn_q, cn_kv = cn[:, :MLA_Q_RANK], cn[:, MLA_Q_RANK:]
    dw_q_b = _mm(cn_q, dqp, ta=True, out_dtype=BF16, name=f"{tag}_dwqb", bm=512)
    dcn_q = _mm(dqp, p['w_q_b'], tb=True, out_dtype=BF16, name=f"{tag}_dcnq")
    dw_kv_b = _mm(cn_kv, dkv, ta=True, o_sh=True, out_dtype=BF16, name=f"{tag}_dwkvb", bm=512)
    dcn_kv = _mm(dkv, p['w_kv_b'], b_sh='k', out_dtype=BF16, name=f"{tag}_dcnkv")
    dc, dga = _mla_latent_bwd(c, p['a_gain'], dcn_q, dcn_kv, dkr, name=f"{tag}_dlatent")
    dw_in = _mm(a, dc, ta=True, out_dtype=BF16, name=f"{tag}_dwin", bn=1152)
    k_rank = dw_q_b.shape[0]
    nope = dw_q_b[:, :n_heads * LANES].reshape(k_rank, n_heads, LANES)
    rope = dw_q_b[:, n_heads * LANES:].reshape(k_rank, n_heads, LANES)[:, :, :MLA_ROPE]
    dw_q_b = jnp.concatenate([nope, rope], axis=2).reshape(k_rank, n_heads * (MLA_NOPE + MLA_ROPE))
    w_in_cols = MLA_Q_RANK + MLA_KV_RANK + MLA_ROPE
    sent = on_big({'w_in': dw_in[:, :w_in_cols].reshape(N_CHIPS, -1, w_in_cols), 'w_q_b': _shard_cols(dw_q_b),
                   'w_kv_b': dw_kv_b, 'w_out': dw_out.reshape(N_CHIPS, -1, dw_out.shape[1])})
    da = _mm(dc, p['w_in'], tb=True, after=sent, name=f"{tag}_da", bk=1152)
    dh_in, dg = _rms_bwd(h, p['mix_norm'], da, res=dh, name=f"{tag}_dnorm")
    small = {'mix_norm': dg, 'q_a_gain': dga[:, :MLA_Q_RANK], 'kv_a_gain': dga[:, MLA_Q_RANK:],
             'q_gain': jnp.concatenate([dgq[0], dgq[1][:, :MLA_ROPE]], axis=1),
             'k_gain': jnp.concatenate([dgk[0], dgk[1][:, :MLA_ROPE]], axis=1)}
    return dh_in, small


def _sgu_fwd(h, p, tag):
    a = _rms_fwd(h, p['mix_norm'], name=f"{tag}_norm")
    uv = _mm(a, p['w_in'], b_sh='n', name=f"{tag}_in")
    u, vn = _sgu_act_fwd(uv, p['v_gain'], name=f"{tag}_act")
    gated = _sgu_mix_fwd(u, vn, p['w_s'], p['b_s'], name=f"{tag}_mix")
    out = _mm(gated, p['w_out'], res=h, name=f"{tag}_out")
    return out, (h, a, uv, u, vn, gated)


def _sgu_bwd(dh, saved, p, tag, after, on_big):
    h, a, uv, u, vn, gated = saved
    dgated = _mm(dh, p['w_out'], tb=True, after=after, out_dtype=BF16, name=f"{tag}_dgated")
    dw_out = _mm(gated, dh, ta=True, after=after, out_dtype=BF16, name=f"{tag}_dwout")
    du, dvn, dws, dbs = _sgu_mix_bwd(u, vn, p['w_s'], p['b_s'], dgated, name=f"{tag}_dmix")
    duv, dvg = _sgu_act_bwd(uv, p['v_gain'], du, dvn, name=f"{tag}_dact")
    dw_in = _mm(a, duv, ta=True, o_sh=True, out_dtype=BF16, name=f"{tag}_dwin")
    sent = on_big({'w_in': dw_in, 'w_out': dw_out.reshape(N_CHIPS, -1, dw_out.shape[1])})
    da = _mm(duv, p['w_in'], b_sh='k', after=sent, name=f"{tag}_da")
    dh_in, dg = _rms_bwd(h, p['mix_norm'], da, res=dh, name=f"{tag}_dnorm")
    small = {'mix_norm': dg, 'v_gain': dvg, 'w_s': dws, 'b_s': dbs[:, :, 0]}
    return dh_in, small


def _pack(parts):
    flat = jnp.concatenate([p.reshape(-1).astype(F32) for p in parts])
    rows = -(-flat.shape[0] // LANES)
    rows = -(-rows // 32) * 32
    return jnp.pad(flat, (0, rows * LANES - flat.shape[0])).reshape(rows, LANES)


def _unpack(packed, shapes):
    flat = packed.reshape(-1)
    out, off = [], 0
    for s in shapes:
        n = 1
        for d in s:
            n *= d
        out.append(flat[off:off + n].reshape(s))
        off += n
    return out


MIXERS = ('fox', 'mla', 'sb', 'sgu')
WEIGHT_NAMES = ['mix_norm', 'ffn_norm', 'fox_w_in', 'fox_b_f', 'fox_q_gain', 'fox_k_gain', 'fox_w_out', 'mla_w_in',
                'mla_q_a_gain', 'mla_kv_a_gain', 'mla_w_q_b', 'mla_w_kv_b', 'mla_q_gain', 'mla_k_gain', 'mla_w_out',
                'sb_w_in', 'sb_q_gain', 'sb_k_gain', 'sb_w_out', 'sgu_w_in', 'sgu_v_gain', 'sgu_w_s', 'sgu_b_s',
                'sgu_w_out', 'ffn_w_up', 'ffn_conv_w', 'ffn_conv_b', 'ffn_w_down']
SMALL_SHARDED = {'mla_q_a_gain': 1, 'mla_kv_a_gain': 1, 'sgu_v_gain': 1, 'ffn_conv_w': 2}
BIG = ['fox_w_in', 'fox_w_out', 'mla_w_in', 'mla_w_q_b', 'mla_w_kv_b', 'mla_w_out', 'sb_w_in', 'sb_w_out', 'sgu_w_in',
       'sgu_w_out', 'ffn_w_up', 'ffn_w_down']


def kernel(x, positions, mix_norm, ffn_norm, fox_w_in, fox_b_f, fox_q_gain, fox_k_gain, fox_w_out, mla_w_in, mla_q_a_gain, mla_kv_a_gain, mla_w_q_b, mla_w_kv_b, mla_q_gain, mla_k_gain, mla_w_out, sb_w_in, sb_q_gain, sb_k_gain, sb_w_out, sgu_w_in, sgu_v_gain, sgu_w_s, sgu_b_s, sgu_w_out, ffn_w_up, ffn_conv_w, ffn_conv_b, ffn_w_down, loss_target, m_mix_norm, m_ffn_norm, m_fox_w_in, m_fox_b_f, m_fox_q_gain, m_fox_k_gain, m_fox_w_out, m_mla_w_in, m_mla_q_a_gain, m_mla_kv_a_gain, m_mla_w_q_b, m_mla_w_kv_b, m_mla_q_gain, m_mla_k_gain, m_mla_w_out, m_sb_w_in, m_sb_q_gain, m_sb_k_gain, m_sb_w_out, m_sgu_w_in, m_sgu_v_gain, m_sgu_w_s, m_sgu_b_s, m_sgu_w_out, m_ffn_w_up, m_ffn_conv_w, m_ffn_conv_b, m_ffn_w_down, v_mix_norm, v_ffn_norm, v_fox_w_in, v_fox_b_f, v_fox_q_gain, v_fox_k_gain, v_fox_w_out, v_mla_w_in, v_mla_q_a_gain, v_mla_kv_a_gain, v_mla_w_q_b, v_mla_w_kv_b, v_mla_q_gain, v_mla_k_gain, v_mla_w_out, v_sb_w_in, v_sb_q_gain, v_sb_k_gain, v_sb_w_out, v_sgu_w_in, v_sgu_v_gain, v_sgu_w_s, v_sgu_b_s, v_sgu_w_out, v_ffn_w_up, v_ffn_conv_w, v_ffn_conv_b, v_ffn_w_down):
    args = dict(locals())
    W = {k: args[k] for k in WEIGHT_NAMES}
    M = {k: args['m_' + k] for k in WEIGHT_NAMES}
    V = {k: args['v_' + k] for k in WEIGHT_NAMES}
    depth = mix_norm.shape[0]
    s_len, d_model = x.shape[1], x.shape[2]
    n_heads = d_model // HEAD_DIM
    assert all(W[k].shape[0] == 1 for k in WEIGHT_NAMES if k.split('_')[0] in MIXERS), "one layer per mixer"
    xi, yi, ci = lax.axis_index("x"), lax.axis_index("y"), lax.axis_index("c")
    chip = 2 * xi + yi
    place = tuple(jnp.reshape(v, (1,)).astype(jnp.int32) for v in (xi, yi, ci))

    small_local = _pack([W[k][0] if k != 'ffn_conv_w' else W[k] for k in SMALL_SHARDED])

    def piece_shards(i, part):
        if part == 'ffn':
            return {'ffn_w_up': (W['ffn_w_up'], i), 'ffn_w_down': (W['ffn_w_down'], i)}
        mixer = MIXERS[i % len(MIXERS)]
        shards = {k: W[k][0] for k in BIG if k.startswith(mixer + '_')}
        if i == 0:
            shards['small'] = small_local
        return shards

    pieces = [(i, part) for i in range(depth) for part in ('mixer', 'ffn')]
    gathered = {}
    pname = lambda pc: f"gather_l{pc[0]}_{pc[1]}"
    states = {}
    placed = {pieces[0]: _gather_place(piece_shards(*pieces[0]), place, name=pname(pieces[0]))}
    states[pieces[0]], token = _gather_begin(placed[pieces[0]], mix_norm, name=pname(pieces[0]))
    prev = token
    for pc in pieces[1:]:
        placed[pc] = _gather_place(piece_shards(*pc), place, name=pname(pc), after=prev)
        prev = placed[pc][2][-1]
    gathered[pieces[0]] = _gather_end(states.pop(pieces[0]), placed[pieces[-1]][2][-1], name=pname(pieces[0]))
    first_done = next(iter(gathered[pieces[0]].values()))
    states[pieces[1]], token = _gather_begin(placed[pieces[1]], first_done, name=pname(pieces[1]))
    tokens = [token]
    small_shapes = [W[k][0].shape if k != 'ffn_conv_w' else W[k].shape for k in SMALL_SHARDED]
    per_chip = [_unpack(gathered[pieces[0]]['small'][s], small_shapes) for s in range(N_CHIPS)]
    full_small = {k: jnp.concatenate([per_chip[s][j] for s in range(N_CHIPS)], axis=-1)
                  for j, k in enumerate(SMALL_SHARDED)}

    pos = positions.reshape(s_len).astype(F32)
    inv_freq = ROPE_THETA ** (-jnp.arange(0, MLA_ROPE, 2, dtype=F32) / MLA_ROPE)
    ang = pos[:, None] * inv_freq
    cos_t = _pad_lanes(jnp.concatenate([jnp.cos(ang), jnp.cos(ang)], axis=1))
    sin_t = _pad_lanes(jnp.concatenate([-jnp.sin(ang), jnp.sin(ang)], axis=1))

    def piece_params(i, part):
        mixer = MIXERS[i % len(MIXERS)]
        g = gathered[(i, part)]
        if part == 'ffn':
            return mixer, {'ffn_norm': ffn_norm[i:i + 1], 'ffn_w_up': g['ffn_w_up'],
                           'ffn_w_down': g['ffn_w_down'].reshape(-1, d_model),
                           'ffn_conv_w': full_small['ffn_conv_w'][i], 'ffn_conv_b': ffn_conv_b[i:i + 1]}
        p = {'mix_norm': mix_norm[i:i + 1]}
        rows = lambda w: w.reshape(-1, w.shape[-1])
        if mixer == 'fox':
            w = _unshard_cols(g['fox_w_in'])
            p['w_in'] = jnp.pad(w, ((0, 0), (0, (3 * n_heads + 1) * HEAD_DIM - w.shape[1])))
            p['b_f'] = _pad_lanes(fox_b_f)
            p['qk_gain'] = jnp.stack([fox_q_gain, fox_k_gain])
            p['w_out'] = rows(g['fox_w_out'])
        elif mixer == 'sb':
            p['w_in'] = g['sb_w_in']
            p['qk_gain'] = jnp.stack([sb_q_gain, sb_k_gain])
            p['w_out'] = rows(g['sb_w_out'])
        elif mixer == 'sgu':
            p['w_in'] = g['sgu_w_in']
            p['v_gain'] = full_small['sgu_v_gain'].reshape(1, -1)
            p['w_s'] = sgu_w_s[0]
            p['b_s'] = sgu_b_s[0][:, :, None]
            p['w_out'] = rows(g['sgu_w_out'])
        else:
            w = rows(g['mla_w_in'])
            p['w_in'] = jnp.pad(w, ((0, 0), (0, MLA_Q_RANK + MLA_KV_RANK + LANES - w.shape[1])))
            p['a_gain'] = jnp.concatenate([full_small['mla_q_a_gain'], full_small['mla_kv_a_gain']]).reshape(1, -1)
            wq = _unshard_cols(g['mla_w_q_b']).reshape(MLA_Q_RANK, n_heads, MLA_NOPE + MLA_ROPE)
            p['w_q_b'] = jnp.concatenate([wq[:, :, :MLA_NOPE].reshape(MLA_Q_RANK, -1),
                                          _pad_lanes(wq[:, :, MLA_NOPE:]).reshape(MLA_Q_RANK, -1)], axis=1)
            p['w_kv_b'] = g['mla_w_kv_b']
            p['gq'] = jnp.stack([mla_q_gain[:, :MLA_NOPE], _pad_lanes(mla_q_gain[:, MLA_NOPE:])])
            p['gk'] = jnp.stack([mla_k_gain[:, :MLA_NOPE], _pad_lanes(mla_k_gain[:, MLA_NOPE:])])
            p['cos'], p['sin'] = cos_t, sin_t
            p['w_out'] = rows(g['mla_w_out'])
        return mixer, p

    h = x.reshape(s_len, d_model)
    saved = []
    for n, (i, part) in enumerate(pieces):
        nxt = pieces[n + 1] if n + 1 < len(pieces) else None
        ahead = pieces[n + 2] if n + 2 < len(pieces) else None
        if ahead is not None:
            after = next(iter(gathered[(i, part)].values()))
            states[ahead], token = _gather_begin(placed[ahead], after, name=pname(ahead))
            tokens.append(token)
        mixer, p = piece_params(i, part)
        gain = 'ffn_norm' if part == 'ffn' else 'mix_norm'
        for token in tokens:
            p[gain] = p[gain] + token[0:1, 0:1]
        tokens = []
        tag = f"l{i}_{mixer}"
        if part == 'ffn':
            h, sv = _ffn_fwd(h, p, f"l{i}")
        elif mixer in ('fox', 'sb'):
            h, sv = _qkv_attn_fwd(mixer, h, p, tag, n_heads)
        elif mixer == 'mla':
            h, sv = _mla_fwd(h, p, tag, n_heads)
        else:
            h, sv = _sgu_fwd(h, p, tag)
        saved.append((mixer, p, sv))
        if nxt is not None:
            gathered[nxt] = _gather_end(states.pop(nxt), h, name=pname(nxt))
    loss_row, dh = _loss(h, loss_target.reshape(s_len, d_model))
    loss = lax.psum(loss_row[0, 0], ("x", "y", "c"))

    big_grads, small_grads = {}, {k: [None] * depth for k in ('mix_norm', 'ffn_norm', 'ffn_conv_w', 'ffn_conv_b')}

    def keep(reduced, i):
        for k, v in reduced.items():
            if k.startswith('ffn_'):
                big_grads.setdefault(k, [None] * depth)[i] = v
            else:
                big_grads[k] = v[None]

    state, token, flying = None, jnp.broadcast_to(loss, (8, LANES)), None
    shares = []
    for n in reversed(range(len(pieces))):
        i, part = pieces[n]
        mixer, p, sv = saved[n]
        tag = f"l{i}_{mixer}"
        rname = f"reduce_l{i}_{part}"
        started = []

        def on_big(big, prefix=('' if part == 'ffn' else mixer + '_'), rname=rname, started=started):
            st, tok = _reduce_sibling_start({prefix + k: v for k, v in big.items()}, place[0], name=rname)
            started.append(st)
            return tok

        if part == 'ffn':
            dh, small = _ffn_bwd(dh, sv, p, f"l{i}", token, on_big)
        elif mixer in ('fox', 'sb'):
            dh, small = _qkv_attn_bwd(mixer, dh, sv, p, tag, n_heads, token, on_big)
        elif mixer == 'mla':
            dh, small = _mla_bwd(dh, sv, p, tag, n_heads, token, on_big)
        else:
            dh, small = _sgu_bwd(dh, sv, p, tag, token, on_big)
        if state is not None:
            fname = f"reduce_l{flying[0]}_{flying[1]}"
            shares.append((_reduce_end(state, place, dh, name=fname), flying[0], fname))
        state, token = _reduce_begin(started[0], place, dh, name=rname)
        flying = (i, part)
        for k, v in small.items():
            if k in small_grads:
                small_grads[k][i] = v
            else:
                small_grads[f"{mixer}_{k}"] = v
    last_state = state
    grad_x = dh.reshape(x.shape)
    for k in ('mix_norm', 'ffn_norm', 'ffn_conv_b'):
        small_grads[k] = jnp.concatenate(small_grads[k], axis=0)
    small_grads['ffn_conv_w'] = jnp.stack(small_grads['ffn_conv_w'])

    small_names = [k for k in WEIGHT_NAMES if k not in BIG]
    full_shapes = {k: (W[k].shape[:-1] + (W[k].shape[-1] * N_CHIPS,) if k in SMALL_SHARDED else W[k].shape)
                   for k in small_names}
    packed = _pack([small_grads[k].reshape(full_shapes[k]) for k in small_names])
    last_token = token
    summed = _sum_devices(_broadcast_all(packed, last_token, name="small_bcast"), name="small_sum")
    small_full = dict(zip(small_names, _unpack(summed, [full_shapes[k] for k in small_names])))
    for share_state, i, fname in shares:
        keep(_reduce_finish(share_state, summed, name=fname), i)
    for k in ('ffn_w_up', 'ffn_w_down'):
        big_grads[k] = jnp.stack(big_grads[k])
    grads = dict(big_grads)
    for k in small_names:
        g = small_full[k]
        if k in SMALL_SHARDED:
            n = W[k].shape[-1]
            g = lax.dynamic_slice_in_dim(g, chip * n, n, axis=g.ndim - 1)
        grads[k] = g

    delta, new_m, new_v = {}, {}, {}

    def update(k):
        grads[k] = grads[k].reshape(W[k].shape)
        delta[k], new_m[k], new_v[k] = _adamw(W[k], grads[k], M[k], V[k], after=last_token, name=f"adamw_{k}")

    last_names = [k for k in BIG if k.startswith(MIXERS[0] + '_')]
    for k in WEIGHT_NAMES:
        if k not in last_names:
            update(k)
    share_state = _reduce_end(last_state, place, delta['ffn_w_up'], name="reduce_l0_mixer")
    keep(_reduce_finish(share_state, delta['ffn_w_down'], name="reduce_l0_mixer"), 0)
    for k in last_names:
        grads[k] = big_grads[k]
        update(k)
    return (loss, grad_x, *[grads[k] for k in WEIGHT_NAMES], *[delta[k] for k in WEIGHT_NAMES],
            *[new_m[k] for k in WEIGHT_NAMES], *[new_v[k] for k in WEIGHT_NAMES])
```

```python
import functools

import jax
import jax.numpy as jnp
from jax import lax
from jax.experimental import pallas as pl
from jax.experimental.pallas import tpu as pltpu

F32 = jnp.float32
BF16 = jnp.bfloat16
LANES = 128
HEAD_DIM = 128
NORM_EPS = 1e-6
MLA_Q_RANK = 512
MLA_KV_RANK = 512
MLA_NOPE = 128
MLA_ROPE = 64
ROPE_THETA = 10000.0
SGU_CHUNK = 128
N_CHIPS = 4
ADAM_LR, ADAM_B1, ADAM_B2, ADAM_EPS, ADAM_WD, ADAM_STEP = 0.001, 0.9, 0.999, 1e-08, 0.01, 10
VMEM_LIMIT_BYTES = 56 * 1024 * 1024
MM_VMEM_BUDGET_BYTES = 36 * 1024 * 1024
SHARD_BLOCK_BYTES = 4 * 1024 * 1024
MESH = pl.DeviceIdType.MESH
NEG_BIG = -1e30


def _params(*sem):
    return pltpu.CompilerParams(dimension_semantics=sem, vmem_limit_bytes=VMEM_LIMIT_BYTES)


def _div_block(n, target, mult=LANES):
    if n <= target:
        return n
    best = None
    for b in range(mult, target + 1, mult):
        if n % b == 0:
            best = b
    assert best is not None, (n, target, mult)
    return best


def _after_spec(after):
    if after is None:
        return []
    return [pl.BlockSpec(memory_space=pl.ANY)]


def _after_arg(after):
    return [] if after is None else [after]


def _iota(shape, dim):
    return lax.broadcasted_iota(jnp.int32, shape, dim)


def _dot(a, b, ca, cb):
    return lax.dot_general(a, b, (((ca,), (cb,)), ((), ())), preferred_element_type=F32)


def _mm(a, b, *, name, ta=False, tb=False, a_sh=False, b_sh=None, o_sh=False, res=None, after=None, out_dtype=F32,
        bm=1024, bn=1024, bk=512):
    if a_sh:
        assert not ta
        m, k = a.shape[1], a.shape[0] * a.shape[2]
    else:
        m, k = (a.shape[1], a.shape[0]) if ta else a.shape
    if b_sh == 'n':
        n = b.shape[2] * b.shape[0]
        assert b.shape[1] == k and not tb
    elif b_sh == 'k':
        n = b.shape[1]
        assert b.shape[2] * N_CHIPS == k
    else:
        n = b.shape[0] if tb else b.shape[1]
        assert (b.shape[1] if tb else b.shape[0]) == k
    n_sh = n // N_CHIPS
    k_sh = k // N_CHIPS
    bm = _div_block(m, bm, 8 if not ta else LANES)
    bn_limit = n
    if b_sh == 'n':
        bn_limit = b.shape[2]
    if o_sh:
        bn_limit = min(bn_limit, n_sh)
    bn = _div_block(bn_limit, bn)
    assert (not o_sh or n_sh % bn == 0) and (b_sh != 'n' or b.shape[2] % bn == 0)
    bk_limit = k_sh if b_sh == 'k' else k
    if a_sh:
        bk_limit = min(bk_limit, a.shape[2])

    def footprint(kb):
        io = bm * kb * a.dtype.itemsize + kb * bn * b.dtype.itemsize + bm * bn * jnp.dtype(out_dtype).itemsize
        if res is not None:
            io += bm * bn * res.dtype.itemsize
        return 2 * io + (bm * bn * 4 if kb < k else 0)

    bk = max([kb for kb in range(LANES, bk_limit + 1, LANES)
              if bk_limit % kb == 0 and (footprint(kb) <= MM_VMEM_BUDGET_BYTES or kb <= bk)])
    assert (not a_sh or a.shape[2] % bk == 0) and (b_sh != 'k' or k_sh % bk == 0) and k % bk == 0
    nbo = n_sh // bn if o_sh else 1
    nbb = b.shape[2] // bn if b_sh == 'n' else 1
    nks = k_sh // bk if b_sh == 'k' else 1
    nka = a.shape[2] // bk if a_sh else 1
    nk = k // bk

    if a_sh:
        a_spec = pl.BlockSpec((None, bm, bk), lambda i, j, q: (q // nka, i, q % nka))
    elif ta:
        a_spec = pl.BlockSpec((bk, bm), lambda i, j, q: (q, i))
    else:
        a_spec = pl.BlockSpec((bm, bk), lambda i, j, q: (i, q))
    if b_sh == 'n':
        b_spec = pl.BlockSpec((None, bk, bn), lambda i, j, q: (j // nbb, q, j % nbb))
    elif b_sh == 'k':
        b_spec = pl.BlockSpec((None, bn, bk), lambda i, j, q: (q // nks, j, q % nks))
    elif tb:
        b_spec = pl.BlockSpec((bn, bk), lambda i, j, q: (j, q))
    else:
        b_spec = pl.BlockSpec((bk, bn), lambda i, j, q: (q, j))
    if o_sh:
        o_spec = pl.BlockSpec((None, bm, bn), lambda i, j, q: (j // nbo, i, j % nbo))
        o_shape = jax.ShapeDtypeStruct((N_CHIPS, m, n_sh), out_dtype)
    else:
        o_spec = pl.BlockSpec((bm, bn), lambda i, j, q: (i, j))
        o_shape = jax.ShapeDtypeStruct((m, n), out_dtype)
    tb_eff = tb or b_sh == 'k'

    def body(a_ref, b_ref, *rest):
        rest = list(rest)
        if after is not None:
            rest.pop(0)
        r_ref = rest.pop(0) if res is not None else None
        o_ref = rest.pop(0)
        part = _dot(a_ref[...].astype(BF16), b_ref[...].astype(BF16), 0 if ta else 1, 1 if tb_eff else 0)

        def finish(r):
            if res is not None:
                r = r + r_ref[...].astype(F32)
            o_ref[...] = r.astype(out_dtype)

        if nk == 1:
            finish(part)
            return
        acc, = rest
        q = pl.program_id(2)

        @pl.when(q == 0)
        def _():
            acc[...] = part

        @pl.when(q > 0)
        def _():
            acc[...] += part

        @pl.when(q == nk - 1)
        def _():
            finish(acc[...])

    ins = [a, b]
    in_specs = [a_spec, b_spec]
    if after is not None:
        ins.append(after)
        in_specs.append(pl.BlockSpec((8, LANES), lambda i, j, q: (0, 0)))
    if res is not None:
        assert not o_sh
        ins.append(res)
        in_specs.append(pl.BlockSpec((bm, bn), lambda i, j, q: (i, j)))
    return pl.pallas_call(
        body, name=name, grid=(m // bm, n // bn, nk), in_specs=in_specs, out_specs=o_spec, out_shape=o_shape,
        scratch_shapes=[pltpu.VMEM((bm, bn), F32)] if nk > 1 else [],
        compiler_params=_params("parallel", "parallel", "arbitrary"))(*ins)


def _rms_fwd(x, g, *, name, out_dtype=BF16, br=256):
    r, c = x.shape
    br = _div_block(r, br, 8)

    def body(x_ref, g_ref, o_ref):
        xv = x_ref[...].astype(F32)
        inv = lax.rsqrt(jnp.mean(xv * xv, axis=-1, keepdims=True) + NORM_EPS)
        o_ref[...] = (xv * inv * g_ref[...]).astype(out_dtype)

    return pl.pallas_call(
        body, name=name, grid=(r // br,),
        in_specs=[pl.BlockSpec((br, c), lambda i: (i, 0)), pl.BlockSpec((1, c), lambda i: (0, 0))],
        out_specs=pl.BlockSpec((br, c), lambda i: (i, 0)), out_shape=jax.ShapeDtypeStruct((r, c), out_dtype),
        compiler_params=_params("parallel"))(x, g)


def _rms_bwd_math(xv, gv, dyv, n):
    inv = lax.rsqrt(jnp.sum(xv * xv, axis=-1, keepdims=True) / n + NORM_EPS)
    xh = xv * inv
    dyg = dyv * gv
    dx = inv * (dyg - xh * (jnp.sum(dyg * xh, axis=-1, keepdims=True) / n))
    return dx, dyv * xh


def _rms_bwd(x, g, dy, *, name, res=None, br=256):
    r, c = x.shape
    br = _div_block(r, br, 8)

    def body(x_ref, g_ref, dy_ref, *rest):
        if res is not None:
            r_ref, dx_ref, dg_ref = rest
        else:
            dx_ref, dg_ref = rest
        dx, dgr = _rms_bwd_math(x_ref[...].astype(F32), g_ref[...], dy_ref[...].astype(F32), c)
        if res is not None:
            dx = dx + r_ref[...]
        dx_ref[...] = dx

        @pl.when(pl.program_id(0) == 0)
        def _():
            dg_ref[...] = jnp.zeros_like(dg_ref)

        dg_ref[...] += jnp.sum(dgr, axis=0, keepdims=True)

    row = pl.BlockSpec((br, c), lambda i: (i, 0))
    vec = pl.BlockSpec((1, c), lambda i: (0, 0))
    ins = [x, g, dy] + ([res] if res is not None else [])
    return pl.pallas_call(
        body, name=name, grid=(r // br,), in_specs=[row, vec, row] + ([row] if res is not None else []),
        out_specs=[row, vec], out_shape=[jax.ShapeDtypeStruct((r, c), F32), jax.ShapeDtypeStruct((1, c), F32)],
        compiler_params=_params("arbitrary"))(*ins)


def _loss(y, target, *, name="loss", br=256):
    r, c = y.shape
    br = _div_block(r, br, 8)

    def body(y_ref, t_ref, l_ref, dy_ref):
        d = y_ref[...] - t_ref[...]
        dy_ref[...] = d * (1.0 / c)

        @pl.when(pl.program_id(0) == 0)
        def _():
            l_ref[...] = jnp.zeros_like(l_ref)

        part = jnp.sum(d * d, axis=0, keepdims=True)
        l_ref[...] += (0.5 / c) * jnp.sum(part, axis=1, keepdims=True) * jnp.ones((1, LANES), F32)

    row = pl.BlockSpec((br, c), lambda i: (i, 0))
    return pl.pallas_call(
        body, name=name, grid=(r // br,), in_specs=[row, row],
        out_specs=[pl.BlockSpec((1, LANES), lambda i: (0, 0)), row],
        out_shape=[jax.ShapeDtypeStruct((1, LANES), F32), jax.ShapeDtypeStruct((r, c), F32)],
        compiler_params=_params("arbitrary"))(y, target)


def _split2(x):
    hi = x.astype(BF16)
    lo = (x - hi.astype(F32)).astype(BF16)
    return hi, lo


def _lane_scan(x, *, suffix):
    rows, n = x.shape
    nb = n // LANES
    a, b = _iota((LANES, LANES), 0), _iota((LANES, LANES), 1)
    tri = ((a > b) if suffix else (a < b)).astype(BF16)
    outs = [None] * nb
    run = jnp.zeros((rows, 1), F32)
    order = range(nb - 1, -1, -1) if suffix else range(nb)
    for blk in order:
        xb = x[:, blk * LANES:(blk + 1) * LANES]
        hi, lo = _split2(xb)
        outs[blk] = _dot(hi, tri, 1, 0) + _dot(lo, tri, 1, 0) + run
        run = run + jnp.sum(xb, axis=-1, keepdims=True)
    return jnp.concatenate(outs, axis=1)


def _softplus(z):
    return jnp.maximum(z, 0.0) + jnp.log(1.0 + jnp.exp(-jnp.abs(z)))


def _head_norm(x, g):
    xv = x.astype(F32)
    inv = lax.rsqrt(jnp.mean(xv * xv, axis=-1, keepdims=True) + NORM_EPS)
    return xv * inv * g


def _attn_weights(kind, qn, kn, scale, qi, bq, bias):
    s = _dot(qn, kn, 1, 1) * scale
    row = qi * bq + _iota(s.shape, 0)
    col = _iota(s.shape, 1)
    if kind == 'sb':
        strict = col < row
        sp = _softplus(s)
        after = _lane_scan(jnp.where(strict, -sp, 0.0), suffix=True)
        w = jnp.where(strict, jnp.exp(s - sp + after), 0.0)
        return w, (strict, s - sp)
    if bias is not None:
        s = s + bias
    s = jnp.where(col <= row, s, NEG_BIG)
    mx = jnp.max(s, axis=-1, keepdims=True)
    e = jnp.exp(s - mx)
    return e, jnp.sum(e, axis=-1, keepdims=True)


def _attn_fwd(kind, q, k, v, *, name, n_heads, dqk, qcol, kcol, vcol, scale, gains=None, cq=None, ck=None, bq=256):
    s_len = q.shape[0]
    bq = _div_block(s_len, bq, 8)
    norm, fox = gains is not None, cq is not None

    def body(*refs):
        refs = list(refs)
        q_ref, k_ref, v_ref = refs[:3]
        rest = refs[3:]
        g_ref = rest.pop(0) if norm else None
        cq_ref, ck_ref = (rest.pop(0), rest.pop(0)) if fox else (None, None)
        o_ref, = rest
        qi = pl.program_id(1)

        def step(n_keys):
            if norm:
                qn = _head_norm(q_ref[...], g_ref[0]).astype(BF16)
                kn = _head_norm(k_ref[0:n_keys, :], g_ref[1]).astype(BF16)
            else:
                qn, kn = q_ref[...].astype(BF16), k_ref[0:n_keys, :].astype(BF16)
            bias = (cq_ref[...] - ck_ref[:, 0:n_keys]) if fox else None
            w, aux = _attn_weights(kind, qn, kn, scale, qi, bq, bias)
            o = _dot(w.astype(BF16), v_ref[0:n_keys, :].astype(BF16), 1, 0)
            if kind != 'sb':
                o = o / aux
            o_ref[...] = o.astype(BF16)

        for qv in range(s_len // bq):
            pl.when(qi == qv)(functools.partial(step, (qv + 1) * bq))

    in_specs = [pl.BlockSpec((bq, dqk), lambda h, i: (i, qcol(h))),
                pl.BlockSpec((s_len, dqk), lambda h, i: (0, kcol(h))),
                pl.BlockSpec((s_len, HEAD_DIM), lambda h, i: (0, vcol(h)))]
    ins = [q, k, v]
    if norm:
        in_specs.append(pl.BlockSpec((2, 1, dqk), lambda h, i: (0, 0, 0)))
        ins.append(gains)
    if fox:
        in_specs += [pl.BlockSpec((None, bq, 1), lambda h, i: (h, i, 0)), pl.BlockSpec((None, 1, s_len), lambda h, i: (h, 0, 0))]
        ins += [cq, ck]
    return pl.pallas_call(
        body, name=name, grid=(n_heads, s_len // bq), in_specs=in_specs,
        out_specs=pl.BlockSpec((bq, HEAD_DIM), lambda h, i: (i, h)),
        out_shape=jax.ShapeDtypeStruct((s_len, n_heads * HEAD_DIM), BF16),
        compiler_params=_params("parallel", "parallel"))(*ins)


def _attn_bwd(kind, q, k, v, o, do, *, name, n_heads, dqk, qcol, kcol, vcol, scale, gains=None, cq=None, ck=None,
              bq=256):
    s_len = q.shape[0]
    bq = _div_block(s_len, bq, 8)
    nq = s_len // bq
    norm, fox = gains is not None, cq is not None

    def body(*refs):
        refs = list(refs)
        q_ref, k_ref, v_ref, o_ref, do_ref = refs[:5]
        rest = refs[5:]
        g_ref = rest.pop(0) if norm else None
        cq_ref, ck_ref = (rest.pop(0), rest.pop(0)) if fox else (None, None)
        dq_ref, dk_ref, dv_ref = rest.pop(0), rest.pop(0), rest.pop(0)
        dg_ref = rest.pop(0) if norm else None
        dcq_ref, dck_ref = (rest.pop(0), rest.pop(0)) if fox else (None, None)
        dk_acc, dv_acc = rest
        h, qi = pl.program_id(0), pl.program_id(1)

        @pl.when(qi == 0)
        def _():
            dk_acc[...] = jnp.zeros_like(dk_acc)
            dv_acc[...] = jnp.zeros_like(dv_acc)
            if fox:
                dck_ref[...] = jnp.zeros_like(dck_ref)

        if norm:
            @pl.when((qi == 0) & (h == 0))
            def _():
                dg_ref[...] = jnp.zeros_like(dg_ref)


        def step(n_keys):
            if norm:
                qn = _head_norm(q_ref[...], g_ref[0]).astype(BF16)
                kn = _head_norm(k_ref[0:n_keys, :], g_ref[1]).astype(BF16)
            else:
                qn, kn = q_ref[...].astype(BF16), k_ref[0:n_keys, :].astype(BF16)
            vb = v_ref[0:n_keys, :].astype(BF16)
            dob = do_ref[...].astype(BF16)
            bias = (cq_ref[...] - ck_ref[:, 0:n_keys]) if fox else None
            w, aux = _attn_weights(kind, qn, kn, scale, qi, bq, bias)
            dw = _dot(dob, vb, 1, 1)
            if kind == 'sb':
                strict, log_sig = aux
                g = dw * w
                cc = _lane_scan(g, suffix=False)
                sig = jnp.exp(log_sig)
                ds = jnp.where(strict, g * (1.0 - sig) - cc * sig, 0.0)
                pw = w
            else:
                pw = w / aux
                delta = jnp.sum(do_ref[...].astype(F32) * o_ref[...].astype(F32), axis=-1, keepdims=True)
                ds = pw * (dw - delta)
                if fox:
                    dcq_ref[...] = jnp.sum(ds, axis=1, keepdims=True)
                    dck_ref[:, 0:n_keys] -= jnp.sum(ds, axis=0, keepdims=True)
            dsb = (ds * scale).astype(BF16)
            dqn = _dot(dsb, kn, 1, 0)
            dk_acc[0:n_keys, :] += _dot(dsb, qn, 0, 0)
            dv_acc[0:n_keys, :] += _dot(pw.astype(BF16), dob, 0, 0)
            if norm:
                dq, dgr = _rms_bwd_math(q_ref[...].astype(F32), g_ref[0], dqn, dqk)
                dg_ref[0] += jnp.sum(dgr, axis=0, keepdims=True)
                dq_ref[...] = dq.astype(BF16)
            else:
                dq_ref[...] = dqn.astype(BF16)

        for qv in range(nq):
            pl.when(qi == qv)(functools.partial(step, (qv + 1) * bq))

        @pl.when(qi == nq - 1)
        def _():
            if norm:
                dk, dgr = _rms_bwd_math(k_ref[...].astype(F32), g_ref[1], dk_acc[...], dqk)
                dg_ref[1] += jnp.sum(dgr, axis=0, keepdims=True)
                dk_ref[...] = dk.astype(BF16)
            else:
                dk_ref[...] = dk_acc[...].astype(BF16)
            dv_ref[...] = dv_acc[...].astype(BF16)

    in_specs = [pl.BlockSpec((bq, dqk), lambda h, i: (i, qcol(h))),
                pl.BlockSpec((s_len, dqk), lambda h, i: (0, kcol(h))),
                pl.BlockSpec((s_len, HEAD_DIM), lambda h, i: (0, vcol(h))),
                pl.BlockSpec((bq, HEAD_DIM), lambda h, i: (i, h)),
                pl.BlockSpec((bq, HEAD_DIM), lambda h, i: (i, h))]
    ins = [q, k, v, o, do]
    out_specs = [pl.BlockSpec((bq, dqk), lambda h, i: (i, h)),
                 pl.BlockSpec((s_len, dqk), lambda h, i: (0, h)),
                 pl.BlockSpec((s_len, HEAD_DIM), lambda h, i: (0, h))]
    out_shape = [jax.ShapeDtypeStruct((s_len, n_heads * dqk), BF16), jax.ShapeDtypeStruct((s_len, n_heads * dqk), BF16),
                 jax.ShapeDtypeStruct((s_len, n_heads * HEAD_DIM), BF16)]
    if norm:
        in_specs.append(pl.BlockSpec((2, 1, dqk), lambda h, i: (0, 0, 0)))
        ins.append(gains)
        out_specs.append(pl.BlockSpec((2, 1, dqk), lambda h, i: (0, 0, 0)))
        out_shape.append(jax.ShapeDtypeStruct((2, 1, dqk), F32))
    if fox:
        in_specs += [pl.BlockSpec((None, bq, 1), lambda h, i: (h, i, 0)), pl.BlockSpec((None, 1, s_len), lambda h, i: (h, 0, 0))]
        ins += [cq, ck]
        out_specs += [pl.BlockSpec((None, bq, 1), lambda h, i: (h, i, 0)), pl.BlockSpec((None, 1, s_len), lambda h, i: (h, 0, 0))]
        out_shape += [jax.ShapeDtypeStruct((n_heads, s_len, 1), F32), jax.ShapeDtypeStruct((n_heads, 1, s_len), F32)]
    return pl.pallas_call(
        body, name=name, grid=(n_heads, nq), in_specs=in_specs, out_specs=out_specs, out_shape=out_shape,
        scratch_shapes=[pltpu.VMEM((s_len, dqk), F32), pltpu.VMEM((s_len, HEAD_DIM), F32)],
        compiler_params=_params("arbitrary", "arbitrary"))(*ins)


def _split3(x):
    hi = x.astype(BF16)
    r1 = x - hi.astype(F32)
    mid = r1.astype(BF16)
    lo = (r1 - mid.astype(F32)).astype(BF16)
    return hi, mid, lo


def _seq_scan(x, *, reverse):
    n = x.shape[0] // LANES
    a, b = _iota((LANES, LANES), 0), _iota((LANES, LANES), 1)
    tri = ((b >= a) if reverse else (b <= a)).astype(BF16)
    outs = [None] * n
    run = jnp.zeros((1, x.shape[1]), F32)
    for blk in (range(n - 1, -1, -1) if reverse else range(n)):
        xb = x[blk * LANES:(blk + 1) * LANES, :]
        hi, mid, lo = _split3(xb)
        outs[blk] = _dot(tri, hi, 1, 0) + _dot(tri, mid, 1, 0) + _dot(tri, lo, 1, 0) + run
        run = run + jnp.sum(xb, axis=0, keepdims=True)
    return jnp.concatenate(outs, axis=0)


def _fgate_fwd(qkvf, b_f, *, fcol, name):
    s_len = qkvf.shape[0]

    def body(f_ref, b_ref, cum_ref):
        z = f_ref[...] + b_ref[...]
        cum_ref[...] = _seq_scan(-_softplus(-z), reverse=False)

    return pl.pallas_call(
        body, name=name, grid=(1,),
        in_specs=[pl.BlockSpec((s_len, LANES), lambda i: (0, fcol)), pl.BlockSpec((1, LANES), lambda i: (0, 0))],
        out_specs=pl.BlockSpec((s_len, LANES), lambda i: (0, 0)), out_shape=jax.ShapeDtypeStruct((s_len, LANES), F32),
        compiler_params=_params("arbitrary"))(qkvf, b_f)


def _fgate_bwd(qkvf, b_f, dcum_a, dcum_b, *, fcol, n_heads, name):
    s_len = qkvf.shape[0]

    def body(f_ref, b_ref, da_ref, db_ref, dz_ref, dbias_ref):
        z = f_ref[...] + b_ref[...]
        dlog = _seq_scan(da_ref[...] + db_ref[...], reverse=True)
        dz = dlog * jnp.exp(-_softplus(z))
        dz = jnp.where(_iota(dz.shape, 1) < n_heads, dz, 0.0)
        dz_ref[...] = dz.astype(BF16)
        dbias_ref[...] = jnp.sum(dz, axis=0, keepdims=True)

    full = pl.BlockSpec((s_len, LANES), lambda i: (0, 0))
    vec = pl.BlockSpec((1, LANES), lambda i: (0, 0))
    return pl.pallas_call(
        body, name=name, grid=(1,),
        in_specs=[pl.BlockSpec((s_len, LANES), lambda i: (0, fcol)), vec, full, full],
        out_specs=[full, vec], out_shape=[jax.ShapeDtypeStruct((s_len, LANES), BF16), jax.ShapeDtypeStruct((1, LANES), F32)],
        compiler_params=_params("arbitrary"))(qkvf, b_f, dcum_a, dcum_b)


def _rope_swap(x):
    half = MLA_ROPE // 2
    lane = _iota(x.shape, 1)
    sw = jnp.where(lane < half, pltpu.roll(x, LANES - half, axis=1), pltpu.roll(x, half, axis=1))
    return jnp.where(lane < MLA_ROPE, sw, 0.0)


def _mla_prep_fwd(qp, kv, c, gq, gk, cos_t, sin_t, *, n_heads, name, bs=512):
    s_len = qp.shape[0]
    bs = _div_block(s_len, bs, 8)
    krope_col = (MLA_Q_RANK + MLA_KV_RANK) // LANES

    def body(qn_ref, qr_ref, kn_ref, kr_ref, gq_ref, gk_ref, cos_ref, sin_ref, qc_ref, kc_ref):
        cos_v, sin_v = cos_ref[...], sin_ref[...]

        def rope(x, g):
            xv = x.astype(F32)
            inv = lax.rsqrt(jnp.sum(xv * xv, axis=-1, keepdims=True) / MLA_ROPE + NORM_EPS)
            y = xv * inv * g
            return y * cos_v + _rope_swap(y) * sin_v

        qc_ref[:, :LANES] = _head_norm(qn_ref[...], gq_ref[0]).astype(BF16)
        qc_ref[:, LANES:] = rope(qr_ref[...], gq_ref[1]).astype(BF16)
        kc_ref[:, :LANES] = _head_norm(kn_ref[...], gk_ref[0]).astype(BF16)
        kc_ref[:, LANES:] = rope(kr_ref[...], gk_ref[1]).astype(BF16)

    blk = lambda f: pl.BlockSpec((bs, LANES), f)
    gspec = pl.BlockSpec((2, 1, LANES), lambda i, h: (0, 0, 0))
    tspec = pl.BlockSpec((bs, LANES), lambda i, h: (i, 0))
    ospec = pl.BlockSpec((bs, 2 * LANES), lambda i, h: (i, h))
    oshape = jax.ShapeDtypeStruct((s_len, n_heads * 2 * LANES), BF16)
    return pl.pallas_call(
        body, name=name, grid=(s_len // bs, n_heads),
        in_specs=[blk(lambda i, h: (i, h)), blk(lambda i, h: (i, n_heads + h)), blk(lambda i, h: (i, 2 * h)),
                  blk(lambda i, h: (i, krope_col)), gspec, gspec, tspec, tspec],
        out_specs=[ospec, ospec], out_shape=[oshape, oshape],
        compiler_params=_params("parallel", "parallel"))(qp, qp, kv, c, gq, gk, cos_t, sin_t)


def _mla_prep_bwd(qp, kv, c, gq, gk, cos_t, sin_t, dqc, dkc, dv, *, n_heads, name, bs=512):
    s_len = qp.shape[0]
    bs = _div_block(s_len, bs, 8)
    krope_col = (MLA_Q_RANK + MLA_KV_RANK) // LANES

    def body(qn_ref, qr_ref, kn_ref, kr_ref, gq_ref, gk_ref, cos_ref, sin_ref, dqc_ref, dkc_ref, dv_ref,
             dqn_ref, dqr_ref, dkv_ref, dkr_ref, dgq_ref, dgk_ref):
        i, h = pl.program_id(0), pl.program_id(1)
        cos_v, sin_v = cos_ref[...], sin_ref[...]

        @pl.when((i == 0) & (h == 0))
        def _():
            dgq_ref[...] = jnp.zeros_like(dgq_ref)
            dgk_ref[...] = jnp.zeros_like(dgk_ref)

        @pl.when(h == 0)
        def _():
            dkr_ref[...] = jnp.zeros_like(dkr_ref)

        def unrope(dy):
            dy = dy.astype(F32)
            return dy * cos_v + _rope_swap(dy * sin_v)

        dqn, dg = _rms_bwd_math(qn_ref[...].astype(F32), gq_ref[0], dqc_ref[:, :LANES].astype(F32), MLA_NOPE)
        dgq_ref[0] += jnp.sum(dg, axis=0, keepdims=True)
        dqn_ref[...] = dqn.astype(BF16)
        dqr, dg = _rms_bwd_math(qr_ref[...].astype(F32), gq_ref[1], unrope(dqc_ref[:, LANES:]), MLA_ROPE)
        dgq_ref[1] += jnp.sum(dg, axis=0, keepdims=True)
        dqr_ref[...] = dqr.astype(BF16)
        dkn, dg = _rms_bwd_math(kn_ref[...].astype(F32), gk_ref[0], dkc_ref[:, :LANES].astype(F32), MLA_NOPE)
        dgk_ref[0] += jnp.sum(dg, axis=0, keepdims=True)
        dkv_ref[:, :LANES] = dkn.astype(BF16)
        dkv_ref[:, LANES:] = dv_ref[...]
        dkr, dg = _rms_bwd_math(kr_ref[...].astype(F32), gk_ref[1], unrope(dkc_ref[:, LANES:]), MLA_ROPE)
        dgk_ref[1] += jnp.sum(dg, axis=0, keepdims=True)
        dkr_ref[...] += dkr

    blk = lambda f: pl.BlockSpec((bs, LANES), f)
    gspec = pl.BlockSpec((2, 1, LANES), lambda i, h: (0, 0, 0))
    tspec = pl.BlockSpec((bs, LANES), lambda i, h: (i, 0))
    cat = pl.BlockSpec((bs, 2 * LANES), lambda i, h: (i, h))
    head = blk(lambda i, h: (i, h))
    hshape = jax.ShapeDtypeStruct((s_len, n_heads * LANES), BF16)
    gshape = jax.ShapeDtypeStruct((2, 1, LANES), F32)
    return pl.pallas_call(
        body, name=name, grid=(s_len // bs, n_heads),
        in_specs=[head, blk(lambda i, h: (i, n_heads + h)), blk(lambda i, h: (i, 2 * h)),
                  blk(lambda i, h: (i, krope_col)), gspec, gspec, tspec, tspec, cat, cat, head],
        out_specs=[head, head, cat, tspec, gspec, gspec],
        out_shape=[hshape, hshape, jax.ShapeDtypeStruct((s_len, n_heads * 2 * LANES), BF16),
                   jax.ShapeDtypeStruct((s_len, LANES), F32), gshape, gshape],
        compiler_params=_params("arbitrary", "arbitrary"))(qp, qp, kv, c, gq, gk, cos_t, sin_t, dqc, dkc, dv)


def _mla_latent_fwd(c, ga, *, name, br=256):
    s_len = c.shape[0]
    br = _div_block(s_len, br, 8)

    def body(c_ref, g_ref, o_ref):
        for part in range(2):
            sl = slice(part * MLA_Q_RANK, (part + 1) * MLA_Q_RANK)
            o_ref[:, sl] = _head_norm(c_ref[:, sl], g_ref[:, sl]).astype(BF16)

    w = MLA_Q_RANK + MLA_KV_RANK
    return pl.pallas_call(
        body, name=name, grid=(s_len // br,),
        in_specs=[pl.BlockSpec((br, w), lambda i: (i, 0)), pl.BlockSpec((1, w), lambda i: (0, 0))],
        out_specs=pl.BlockSpec((br, w), lambda i: (i, 0)), out_shape=jax.ShapeDtypeStruct((s_len, w), BF16),
        compiler_params=_params("parallel"))(c, ga)


def _mla_latent_bwd(c, ga, dcn_q, dcn_kv, dk_rope, *, name, br=256):
    s_len, cw = c.shape
    br = _div_block(s_len, br, 8)
    w = MLA_Q_RANK + MLA_KV_RANK

    def body(c_ref, g_ref, dq_ref, dkv_ref, dkr_ref, dc_ref, dg_ref):
        @pl.when(pl.program_id(0) == 0)
        def _():
            dg_ref[...] = jnp.zeros_like(dg_ref)

        for part, d_ref in enumerate((dq_ref, dkv_ref)):
            sl = slice(part * MLA_Q_RANK, (part + 1) * MLA_Q_RANK)
            dx, dg = _rms_bwd_math(c_ref[:, sl].astype(F32), g_ref[:, sl], d_ref[...].astype(F32), MLA_Q_RANK)
            dc_ref[:, sl] = dx.astype(BF16)
            dg_ref[:, sl] += jnp.sum(dg, axis=0, keepdims=True)
        dc_ref[:, w:] = dkr_ref[...].astype(BF16)

    return pl.pallas_call(
        body, name=name, grid=(s_len // br,),
        in_specs=[pl.BlockSpec((br, w), lambda i: (i, 0)), pl.BlockSpec((1, w), lambda i: (0, 0)),
                  pl.BlockSpec((br, MLA_Q_RANK), lambda i: (i, 0)), pl.BlockSpec((br, MLA_KV_RANK), lambda i: (i, 0)),
                  pl.BlockSpec((br, LANES), lambda i: (i, 0))],
        out_specs=[pl.BlockSpec((br, cw), lambda i: (i, 0)), pl.BlockSpec((1, w), lambda i: (0, 0))],
        out_shape=[jax.ShapeDtypeStruct((s_len, cw), BF16), jax.ShapeDtypeStruct((1, w), F32)],
        compiler_params=_params("arbitrary"))(c, ga, dcn_q, dcn_kv, dk_rope)


_GELU_C = 0.7978845608028654


def _gelu(x):
    return 0.5 * x * (1.0 + jnp.tanh(_GELU_C * (x + 0.044715 * x * x * x)))


def _gelu_grad(x):
    t = jnp.tanh(_GELU_C * (x + 0.044715 * x * x * x))
    return 0.5 * (1.0 + t) + 0.5 * x * (1.0 - t * t) * _GELU_C * (1.0 + 3 * 0.044715 * x * x)


def _sgu_act_fwd(uv, vg, *, name, br=256):
    s_len, w2 = uv.shape
    w = w2 // 2
    br = _div_block(s_len, br, 8)

    def body(uv_ref, g_ref, u_ref, v_ref):
        u_ref[...] = _gelu(uv_ref[:, :w])
        v_ref[...] = _head_norm(_gelu(uv_ref[:, w:]), g_ref[...]).astype(BF16)

    row = lambda c: pl.BlockSpec((br, c), lambda i: (i, 0))
    return pl.pallas_call(
        body, name=name, grid=(s_len // br,), in_specs=[row(w2), pl.BlockSpec((1, w), lambda i: (0, 0))],
        out_specs=[row(w), row(w)], out_shape=[jax.ShapeDtypeStruct((s_len, w), F32), jax.ShapeDtypeStruct((s_len, w), BF16)],
        compiler_params=_params("parallel"))(uv, vg)


def _sgu_act_bwd(uv, vg, du, dvn, *, name, br=256):
    s_len, w2 = uv.shape
    w = w2 // 2
    br = _div_block(s_len, br, 8)

    def body(uv_ref, g_ref, du_ref, dvn_ref, duv_ref, dg_ref):
        @pl.when(pl.program_id(0) == 0)
        def _():
            dg_ref[...] = jnp.zeros_like(dg_ref)

        up, vp = uv_ref[:, :w], uv_ref[:, w:]
        duv_ref[:, :w] = (du_ref[...] * _gelu_grad(up)).astype(BF16)
        dva, dg = _rms_bwd_math(_gelu(vp), g_ref[...], dvn_ref[...], w)
        dg_ref[...] += jnp.sum(dg, axis=0, keepdims=True)
        duv_ref[:, w:] = (dva * _gelu_grad(vp)).astype(BF16)

    row = lambda c: pl.BlockSpec((br, c), lambda i: (i, 0))
    vec = pl.BlockSpec((1, w), lambda i: (0, 0))
    return pl.pallas_call(
        body, name=name, grid=(s_len // br,), in_specs=[row(w2), vec, row(w), row(w)], out_specs=[row(w2), vec],
        out_shape=[jax.ShapeDtypeStruct((s_len, w2), BF16), jax.ShapeDtypeStruct((1, w), F32)],
        compiler_params=_params("arbitrary"))(uv, vg, du, dvn)


def _tril_weights(ws_ref):
    t, s = _iota((SGU_CHUNK, SGU_CHUNK), 0), _iota((SGU_CHUNK, SGU_CHUNK), 1)
    keep = s <= t
    return jnp.where(keep, ws_ref[...], 0.0), keep


def _sgu_mix_fwd(u, vn, w_s, b_s, *, name):
    s_len, w = u.shape
    nc = s_len // SGU_CHUNK

    def body(u_ref, v_ref, ws_ref, b_ref, o_ref):
        wm = _tril_weights(ws_ref)[0].astype(BF16)
        for n in range(nc):
            rows = slice(n * SGU_CHUNK, (n + 1) * SGU_CHUNK)
            mixed = _dot(wm, v_ref[rows, :], 1, 0) + b_ref[...]
            o_ref[rows, :] = (u_ref[rows, :] * mixed).astype(BF16)

    col = pl.BlockSpec((s_len, LANES), lambda g: (0, g))
    return pl.pallas_call(
        body, name=name, grid=(w // LANES,),
        in_specs=[col, col, pl.BlockSpec((None, SGU_CHUNK, SGU_CHUNK), lambda g: (g, 0, 0)),
                  pl.BlockSpec((None, SGU_CHUNK, 1), lambda g: (g, 0, 0))],
        out_specs=col, out_shape=jax.ShapeDtypeStruct((s_len, w), BF16),
        compiler_params=_params("parallel"))(u, vn, w_s, b_s)


def _sgu_mix_bwd(u, vn, w_s, b_s, dgated, *, name):
    s_len, w = u.shape
    nc = s_len // SGU_CHUNK

    def body(u_ref, v_ref, ws_ref, b_ref, dg_ref, du_ref, dv_ref, dws_ref, dbs_ref):
        wf, keep = _tril_weights(ws_ref)
        wm = wf.astype(BF16)
        wmt = wf.T.astype(BF16)
        dws = jnp.zeros((SGU_CHUNK, SGU_CHUNK), F32)
        dbs = jnp.zeros((SGU_CHUNK, 1), F32)
        for n in range(nc):
            rows = slice(n * SGU_CHUNK, (n + 1) * SGU_CHUNK)
            vb = v_ref[rows, :]
            dgv = dg_ref[rows, :].astype(F32)
            mixed = _dot(wm, vb, 1, 0) + b_ref[...]
            du_ref[rows, :] = dgv * mixed
            dm = dgv * u_ref[rows, :]
            dmb = dm.astype(BF16)
            dws = dws + _dot(dmb, vb, 1, 1)
            dbs = dbs + jnp.sum(dm, axis=1, keepdims=True)
            dv_ref[rows, :] = _dot(wmt, dmb, 1, 0)
        dws_ref[...] = jnp.where(keep, dws, 0.0)
        dbs_ref[...] = dbs

    col = pl.BlockSpec((s_len, LANES), lambda g: (0, g))
    wspec = pl.BlockSpec((None, SGU_CHUNK, SGU_CHUNK), lambda g: (g, 0, 0))
    bspec = pl.BlockSpec((None, SGU_CHUNK, 1), lambda g: (g, 0, 0))
    return pl.pallas_call(
        body, name=name, grid=(w // LANES,), in_specs=[col, col, wspec, bspec, col],
        out_specs=[col, col, wspec, bspec],
        out_shape=[jax.ShapeDtypeStruct((s_len, w), F32), jax.ShapeDtypeStruct((s_len, w), F32),
                   jax.ShapeDtypeStruct(w_s.shape, F32), jax.ShapeDtypeStruct(b_s.shape, F32)],
        compiler_params=_params("parallel"))(u, vn, w_s, b_s, dgated)


def _shift_down(x, k):
    if k == 0:
        return x
    return jnp.where(_iota(x.shape, 0) >= k, pltpu.roll(x, k, axis=0), 0.0)


def _shift_up(x, k):
    if k == 0:
        return x
    n = x.shape[0]
    return jnp.where(_iota(x.shape, 0) < n - k, pltpu.roll(x, n - k, axis=0), 0.0)


def _conv(u, w_ref, b_ref):
    return b_ref[...] + w_ref[0:1, :] * _shift_down(u, 2) + w_ref[1:2, :] * _shift_down(u, 1) + w_ref[2:3, :] * u


def _sigmoid(x):
    return 0.5 * jnp.tanh(0.5 * x) + 0.5


def _glu_fwd(up, cw, cb, *, name, bc=256):
    s_len, f2 = up.shape
    f = f2 // 2
    bc = _div_block(f, bc)
    nf = f // bc

    def body(ug_ref, uv_ref, wg_ref, wv_ref, bg_ref, bv_ref, o_ref):
        yg = _conv(ug_ref[...], wg_ref, bg_ref)
        yv = _conv(uv_ref[...], wv_ref, bv_ref)
        o_ref[...] = (yg * _sigmoid(yg) * yv).astype(BF16)

    big = lambda off: pl.BlockSpec((s_len, bc), lambda j: (0, j + off))
    wsp = lambda off: pl.BlockSpec((3, bc), lambda j: (0, j + off))
    bsp = lambda off: pl.BlockSpec((1, bc), lambda j: (0, j + off))
    return pl.pallas_call(
        body, name=name, grid=(nf,), in_specs=[big(0), big(nf), wsp(0), wsp(nf), bsp(0), bsp(nf)],
        out_specs=pl.BlockSpec((s_len, bc), lambda j: (0, j)), out_shape=jax.ShapeDtypeStruct((s_len, f), BF16),
        compiler_params=_params("parallel"))(up, up, cw, cw, cb, cb)


def _glu_bwd(up, cw, cb, dact, *, name, bc=256):
    s_len, f2 = up.shape
    f = f2 // 2
    bc = _div_block(f, bc)
    nf = f // bc

    def body(ug_ref, uv_ref, wg_ref, wv_ref, bg_ref, bv_ref, da_ref, du_ref, dw_ref, db_ref):
        ug, uv = ug_ref[...], uv_ref[...]
        yg = _conv(ug, wg_ref, bg_ref)
        yv = _conv(uv, wv_ref, bv_ref)
        da = da_ref[...].astype(F32)
        sg = _sigmoid(yg)
        planes = ((da * yv * (sg * (1.0 + yg * (1.0 - sg))), ug, wg_ref), (da * (yg * sg), uv, wv_ref))
        for plane, (dy, u, w_ref) in enumerate(planes):
            dy1, dy2 = _shift_up(dy, 1), _shift_up(dy, 2)
            db_ref[plane] = jnp.sum(dy, axis=0, keepdims=True)
            dw_ref[plane, 0:1, :] = jnp.sum(dy2 * u, axis=0, keepdims=True)
            dw_ref[plane, 1:2, :] = jnp.sum(dy1 * u, axis=0, keepdims=True)
            dw_ref[plane, 2:3, :] = jnp.sum(dy * u, axis=0, keepdims=True)
            du_ref[plane] = (w_ref[2:3, :] * dy + w_ref[1:2, :] * dy1 + w_ref[0:1, :] * dy2).astype(BF16)

    big = lambda off: pl.BlockSpec((s_len, bc), lambda j: (0, j + off))
    wsp = lambda off: pl.BlockSpec((3, bc), lambda j: (0, j + off))
    bsp = lambda off: pl.BlockSpec((1, bc), lambda j: (0, j + off))
    planes = lambda r: pl.BlockSpec((2, r, bc), lambda j: (0, 0, j))
    return pl.pallas_call(
        body, name=name, grid=(nf,),
        in_specs=[big(0), big(nf), wsp(0), wsp(nf), bsp(0), bsp(nf), pl.BlockSpec((s_len, bc), lambda j: (0, j))],
        out_specs=[planes(s_len), planes(3), planes(1)],
        out_shape=[jax.ShapeDtypeStruct((2, s_len, f), BF16), jax.ShapeDtypeStruct((2, 3, f), F32),
                   jax.ShapeDtypeStruct((2, 1, f), F32)],
        compiler_params=_params("parallel"))(up, up, cw, cw, cb, cb, dact)


def _as2d(a):
    return a.reshape(-1, a.shape[-1]) if a.ndim >= 2 else a.reshape(1, -1)


def _adamw(w, g, m, v, *, name, after=None, target_bytes=1 << 20):
    shape = w.shape
    w2, m2, v2 = _as2d(w), _as2d(m), _as2d(v)
    g2 = g.reshape(w2.shape)
    r, c = w2.shape
    br = r if r * c * 4 <= target_bytes else _div_block(r, max(8, target_bytes // (4 * c) // 8 * 8), 8)
    c1 = 1.0 - ADAM_B1 ** ADAM_STEP
    c2 = 1.0 - ADAM_B2 ** ADAM_STEP

    def body(w_ref, g_ref, m_ref, v_ref, *rest):
        d_ref, nm_ref, nv_ref = rest[-3:]
        gv = g_ref[...]
        nm = ADAM_B1 * m_ref[...] + (1.0 - ADAM_B1) * gv
        nv = ADAM_B2 * v_ref[...] + (1.0 - ADAM_B2) * (gv * gv)
        nm_ref[...] = nm
        nv_ref[...] = nv
        d_ref[...] = -ADAM_LR * ((nm / c1) / (jnp.sqrt(nv / c2) + ADAM_EPS) + ADAM_WD * w_ref[...])

    spec = pl.BlockSpec((br, c), lambda i: (i, 0))
    sds = jax.ShapeDtypeStruct((r, c), F32)
    d, nm, nv = pl.pallas_call(
        body, name=name, grid=(r // br,), in_specs=[spec] * 4 + _after_spec(after), out_specs=[spec] * 3,
        out_shape=[sds] * 3, compiler_params=_params("parallel"))(w2, g2, m2, v2, *_after_arg(after))
    return d.reshape(shape), nm.reshape(shape), nv.reshape(shape)


def _add_halves(g, recv, place, *, name, target_bytes=SHARD_BLOCK_BYTES):
    _, _, r, c = g.shape
    br = _div_block(r, max(16, target_bytes // (2 * c) // 16 * 16), 16)

    def body(x_ref, y_ref, c_ref, g_ref, r_ref, o_ref):
        o_ref[...] = (g_ref[...].astype(F32) + r_ref[...].astype(F32)).astype(BF16)

    return pl.pallas_call(
        body, name=name,
        grid_spec=pltpu.PrefetchScalarGridSpec(
            num_scalar_prefetch=3, grid=(N_CHIPS, r // br),
            in_specs=[pl.BlockSpec((None, None, br, c), lambda s, i, xr, yr, cr: (s, cr[0], i, 0)),
                      pl.BlockSpec((None, br, c), lambda s, i, xr, yr, cr: (s, i, 0))],
            out_specs=pl.BlockSpec((None, br, c), lambda s, i, xr, yr, cr: (s, i, 0))),
        out_shape=jax.ShapeDtypeStruct((N_CHIPS, r, c), BF16),
        compiler_params=_params("parallel", "parallel"))(*place, g, recv)


def _sum_chips(p, landed, place, *, name, target_bytes=SHARD_BLOCK_BYTES):
    _, r, c = p.shape
    br = _div_block(r, max(16, target_bytes // (4 * c) // 16 * 16), 16)

    def body(x_ref, y_ref, c_ref, p_ref, l1_ref, l2_ref, l3_ref, o_ref):
        o_ref[...] = ((p_ref[...].astype(F32) + l1_ref[...].astype(F32)) + l2_ref[...].astype(F32)) + l3_ref[...].astype(F32)

    slot = lambda k: pl.BlockSpec((None, br, c), lambda i, xr, yr, cr: ((2 * xr[0] + yr[0] + k) % N_CHIPS, i, 0))
    return pl.pallas_call(
        body, name=name,
        grid_spec=pltpu.PrefetchScalarGridSpec(
            num_scalar_prefetch=3, grid=(r // br,), in_specs=[slot(0), slot(1), slot(2), slot(3)],
            out_specs=pl.BlockSpec((None, br, c), lambda i, xr, yr, cr: (cr[0], i, 0))),
        out_shape=jax.ShapeDtypeStruct((2, r, c), F32),
        compiler_params=_params("parallel"))(*place, p, landed, landed, landed)


def _place_shard(w, place, *, dtype, name, layer=None, after=None, target_bytes=SHARD_BLOCK_BYTES):
    r, c = w.shape[-2:]
    hr = r // 2
    mult = 16 if dtype == BF16 else 8
    br = _div_block(hr, max(mult, target_bytes // (4 * c) // mult * mult), mult)
    nb = hr // br

    def body(x_ref, y_ref, c_ref, w_ref, *rest):
        rest[-1][...] = w_ref[...].astype(dtype)

    if layer is None:
        w_spec = pl.BlockSpec((br, c), lambda h, i, xr, yr, cr: (h * nb + i, 0))
    else:
        w_spec = pl.BlockSpec((None, br, c), lambda h, i, xr, yr, cr: (layer, h * nb + i, 0))
    return pl.pallas_call(
        body, name=name,
        grid_spec=pltpu.PrefetchScalarGridSpec(
            num_scalar_prefetch=3, grid=(2, nb), in_specs=[w_spec] + _after_spec(after),
            out_specs=pl.BlockSpec((None, None, br, c), lambda h, i, xr, yr, cr: (2 * xr[0] + yr[0], h, i, 0))),
        out_shape=jax.ShapeDtypeStruct((N_CHIPS, 2, hr, c), dtype),
        compiler_params=_params("parallel", "parallel"))(*place, w, *_after_arg(after))


def _sum_devices(x, *, name):
    n, r, c = x.shape
    br = _div_block(r, 512, 8)

    def body(x_ref, o_ref):
        acc = x_ref[0]
        for s in range(1, n):
            acc = acc + x_ref[s]
        o_ref[...] = acc

    return pl.pallas_call(
        body, name=name, grid=(r // br,), in_specs=[pl.BlockSpec((n, br, c), lambda i: (0, i, 0))],
        out_specs=pl.BlockSpec((br, c), lambda i: (i, 0)), out_shape=jax.ShapeDtypeStruct((r, c), F32),
        compiler_params=_params("parallel"))(x)


_ANY = pl.BlockSpec(memory_space=pl.ANY)


def _place():
    x, y, c = lax.axis_index("x"), lax.axis_index("y"), lax.axis_index("c")
    other_chips = [(1 - x, y), (x, 1 - y), (1 - x, 1 - y)]
    return x, y, c, other_chips


_HBM = pl.BlockSpec(memory_space=pltpu.HBM)
_SEM = pl.BlockSpec(memory_space=pltpu.SEMAPHORE)
_EFFECT = pltpu.SideEffectType.DATAFLOW_SIDE_EFFECTING


def _in_hbm(a):
    return pltpu.with_memory_space_constraint(a, pltpu.HBM)


def _token_spec():
    return pl.BlockSpec(memory_space=pltpu.VMEM), jax.ShapeDtypeStruct((8, LANES), F32)


def _gather_ici_start(bufs, after, *, name):
    n = len(bufs)

    def body(*refs):
        b_refs = refs[:n]
        send_sems, recv_sems = refs[n + 1], refs[n + 2]
        token = refs[-1]
        x, y, c, chips = _place()
        me = 2 * x + y
        for i in range(n):
            for j, (px, py) in enumerate(chips):
                pltpu.make_async_remote_copy(src_ref=b_refs[i].at[me, c], dst_ref=b_refs[i].at[me, c],
                                             send_sem=send_sems.at[3 * i + j], recv_sem=recv_sems.at[3 * i + j],
                                             device_id=(px, py, c), device_id_type=MESH).start()
        token[...] = jnp.zeros_like(token)

    tspec, tshape = _token_spec()
    outs = pl.pallas_call(
        body, name=name, in_specs=[_HBM] * n + [_ANY], out_specs=(_SEM, _SEM, *[_HBM] * n, tspec),
        out_shape=(pltpu.SemaphoreType.DMA((3 * n,)), pltpu.SemaphoreType.DMA((3 * n,)),
                   *[pltpu.HBM(a.shape, a.dtype) for a in bufs], tshape),
        input_output_aliases={i: 2 + i for i in range(n)},
        compiler_params=pltpu.CompilerParams(has_side_effects=_EFFECT),
    )(*[_in_hbm(a) for a in bufs], after)
    return outs[0], outs[1], list(outs[2:2 + n]), outs[-1]


def _gather_ici_wait(send_sems, recv_sems, bufs, after, *, name):
    n = len(bufs)

    def body(*refs):
        b_refs = refs[:n]
        send_sems, recv_sems = refs[n], refs[n + 1]
        x, y, c, chips = _place()
        me = 2 * x + y
        for i in range(n):
            for j, (px, py) in enumerate(chips):
                cp = pltpu.make_async_remote_copy(src_ref=b_refs[i].at[me, c], dst_ref=b_refs[i].at[2 * px + py, c],
                                                  send_sem=send_sems.at[3 * i + j], recv_sem=recv_sems.at[3 * i + j],
                                                  device_id=(px, py, c), device_id_type=MESH)
                cp.wait_send()
                cp.wait_recv()

    outs = pl.pallas_call(
        body, name=name, in_specs=[_HBM] * n + [_SEM, _SEM, _ANY], out_specs=[_HBM] * n,
        out_shape=[pltpu.HBM(a.shape, a.dtype) for a in bufs], input_output_aliases={i: i for i in range(n)},
        compiler_params=pltpu.CompilerParams(has_side_effects=_EFFECT),
    )(*bufs, send_sems, recv_sems, after)
    return list(outs)


def _gather_d2d(bufs, *, name):
    n = len(bufs)

    def body(*refs):
        b_refs = refs[n:2 * n]
        send_sems, recv_sems = refs[2 * n:]
        x, y, c, chips = _place()
        sends = []
        for i in range(n):
            for j, (px, py) in enumerate(chips):
                mine = b_refs[i].at[2 * px + py, c]
                cp = pltpu.make_async_remote_copy(src_ref=mine, dst_ref=mine, send_sem=send_sems.at[3 * i + j],
                                                  recv_sem=recv_sems.at[3 * i + j], device_id=(x, y, 1 - c),
                                                  device_id_type=MESH)
                cp.start()
                sends.append((cp, i, j, px, py))
        for cp, i, j, px, py in sends:
            theirs = b_refs[i].at[2 * px + py, 1 - c]
            pltpu.make_async_remote_copy(src_ref=theirs, dst_ref=theirs, send_sem=send_sems.at[3 * i + j],
                                         recv_sem=recv_sems.at[3 * i + j], device_id=(x, y, 1 - c),
                                         device_id_type=MESH).wait_recv()
            cp.wait_send()

    return pl.pallas_call(
        body, name=name, in_specs=[_ANY] * n, out_specs=[_ANY] * n, input_output_aliases={i: i for i in range(n)},
        out_shape=[jax.ShapeDtypeStruct(a.shape, a.dtype) for a in bufs],
        scratch_shapes=[pltpu.SemaphoreType.DMA((3 * n,)), pltpu.SemaphoreType.DMA((3 * n,))],
    )(*bufs)


def _gather_d2d_start(bufs, after, *, name):
    n = len(bufs)

    def body(*refs):
        b_refs = refs[:n]
        send_sems, recv_sems = refs[n + 1], refs[n + 2]
        token = refs[-1]
        x, y, c, chips = _place()
        for i in range(n):
            for j, (px, py) in enumerate(chips):
                mine = b_refs[i].at[2 * px + py, c]
                pltpu.make_async_remote_copy(src_ref=mine, dst_ref=mine, send_sem=send_sems.at[3 * i + j],
                                             recv_sem=recv_sems.at[3 * i + j], device_id=(x, y, 1 - c),
                                             device_id_type=MESH).start()
        token[...] = jnp.zeros_like(token)

    tspec, tshape = _token_spec()
    outs = pl.pallas_call(
        body, name=name, in_specs=[_HBM] * n + [_ANY], out_specs=(_SEM, _SEM, *[_HBM] * n, tspec),
        out_shape=(pltpu.SemaphoreType.DMA((3 * n,)), pltpu.SemaphoreType.DMA((3 * n,)),
                   *[pltpu.HBM(a.shape, a.dtype) for a in bufs], tshape),
        input_output_aliases={i: 2 + i for i in range(n)},
        compiler_params=pltpu.CompilerParams(has_side_effects=_EFFECT),
    )(*[_in_hbm(a) for a in bufs], after)
    return outs[0], outs[1], list(outs[2:2 + n]), outs[-1]


def _gather_d2d_wait(send_sems, recv_sems, bufs, after, *, name):
    n = len(bufs)

    def body(*refs):
        b_refs = refs[:n]
        send_sems, recv_sems = refs[n], refs[n + 1]
        x, y, c, chips = _place()
        for i in range(n):
            for j, (px, py) in enumerate(chips):
                cp = pltpu.make_async_remote_copy(src_ref=b_refs[i].at[2 * px + py, c],
                                                  dst_ref=b_refs[i].at[2 * px + py, 1 - c],
                                                  send_sem=send_sems.at[3 * i + j], recv_sem=recv_sems.at[3 * i + j],
                                                  device_id=(x, y, 1 - c), device_id_type=MESH)
                cp.wait_send()
                cp.wait_recv()

    outs = pl.pallas_call(
        body, name=name, in_specs=[_HBM] * n + [_SEM, _SEM, _ANY], out_specs=[_HBM] * n,
        out_shape=[pltpu.HBM(a.shape, a.dtype) for a in bufs], input_output_aliases={i: i for i in range(n)},
        compiler_params=pltpu.CompilerParams(has_side_effects=_EFFECT),
    )(*bufs, send_sems, recv_sems, after)
    return list(outs)


def _sibling_halves_start(gs, after, *, name):
    n = len(gs)

    def body(*refs):
        g_refs, r_refs = refs[:n], refs[n:2 * n]
        send_sems, recv_sems = refs[2 * n + 1], refs[2 * n + 2]
        token = refs[-1]
        x, y, c, _ = _place()
        for i in range(n):
            for s in range(N_CHIPS):
                k = i * N_CHIPS + s
                pltpu.make_async_remote_copy(src_ref=g_refs[i].at[s, 1 - c], dst_ref=r_refs[i].at[s],
                                             send_sem=send_sems.at[k], recv_sem=recv_sems.at[k],
                                             device_id=(x, y, 1 - c), device_id_type=MESH).start()
        token[...] = jnp.zeros_like(token)

    tspec, tshape = _token_spec()
    rshapes = [(N_CHIPS,) + g.shape[2:] for g in gs]
    recvs = [_in_hbm(lax.empty(sh, g.dtype)) for sh, g in zip(rshapes, gs)]
    outs = pl.pallas_call(
        body, name=name, in_specs=[_HBM] * (2 * n) + [_ANY], out_specs=(_SEM, _SEM, *[_HBM] * (2 * n), tspec),
        out_shape=(pltpu.SemaphoreType.DMA((N_CHIPS * n,)), pltpu.SemaphoreType.DMA((N_CHIPS * n,)),
                   *[pltpu.HBM(g.shape, g.dtype) for g in gs], *[pltpu.HBM(sh, g.dtype) for sh, g in zip(rshapes, gs)],
                   tshape),
        input_output_aliases={i: 2 + i for i in range(2 * n)},
        compiler_params=pltpu.CompilerParams(has_side_effects=_EFFECT),
    )(*[_in_hbm(g) for g in gs], *recvs, after)
    return outs[0], outs[1], list(outs[2:2 + n]), list(outs[2 + n:2 + 2 * n]), outs[-1]


def _sibling_halves_wait(send_sems, recv_sems, gs, recvs, after, *, name):
    n = len(gs)

    def body(*refs):
        g_refs, r_refs = refs[:n], refs[n:2 * n]
        send_sems, recv_sems = refs[2 * n], refs[2 * n + 1]
        x, y, c, _ = _place()
        for i in range(n):
            for s in range(N_CHIPS):
                k = i * N_CHIPS + s
                cp = pltpu.make_async_remote_copy(src_ref=g_refs[i].at[s, 1 - c], dst_ref=r_refs[i].at[s],
                                                  send_sem=send_sems.at[k], recv_sem=recv_sems.at[k],
                                                  device_id=(x, y, 1 - c), device_id_type=MESH)
                cp.wait_send()
                cp.wait_recv()

    outs = pl.pallas_call(
        body, name=name, in_specs=[_HBM] * (2 * n) + [_SEM, _SEM, _ANY], out_specs=[_HBM] * (2 * n),
        out_shape=[pltpu.HBM(a.shape, a.dtype) for a in list(gs) + list(recvs)],
        input_output_aliases={i: i for i in range(2 * n)},
        compiler_params=pltpu.CompilerParams(has_side_effects=_EFFECT),
    )(*gs, *recvs, send_sems, recv_sems, after)
    return list(outs[:n]), list(outs[n:])


def _chip_scatter_start(ps, after, *, name):
    n = len(ps)

    def body(*refs):
        p_refs, l_refs = refs[:n], refs[n:2 * n]
        send_sems, recv_sems = refs[2 * n + 1], refs[2 * n + 2]
        token = refs[-1]
        x, y, c, chips = _place()
        me = 2 * x + y
        for i in range(n):
            for j, (px, py) in enumerate(chips):
                pltpu.make_async_remote_copy(src_ref=p_refs[i].at[2 * px + py], dst_ref=l_refs[i].at[me],
                                             send_sem=send_sems.at[3 * i + j], recv_sem=recv_sems.at[3 * i + j],
                                             device_id=(px, py, c), device_id_type=MESH).start()
        token[...] = jnp.zeros_like(token)

    tspec, tshape = _token_spec()
    lands = [_in_hbm(lax.empty(p.shape, p.dtype)) for p in ps]
    outs = pl.pallas_call(
        body, name=name, in_specs=[_HBM] * (2 * n) + [_ANY], out_specs=(_SEM, _SEM, *[_HBM] * (2 * n), tspec),
        out_shape=(pltpu.SemaphoreType.DMA((3 * n,)), pltpu.SemaphoreType.DMA((3 * n,)),
                   *[pltpu.HBM(p.shape, p.dtype) for p in ps], *[pltpu.HBM(p.shape, p.dtype) for p in ps], tshape),
        input_output_aliases={i: 2 + i for i in range(2 * n)},
        compiler_params=pltpu.CompilerParams(has_side_effects=_EFFECT),
    )(*[_in_hbm(p) for p in ps], *lands, after)
    return outs[0], outs[1], list(outs[2:2 + n]), list(outs[2 + n:2 + 2 * n]), outs[-1]


def _chip_scatter_wait(send_sems, recv_sems, ps, lands, after, *, name):
    n = len(ps)

    def body(*refs):
        p_refs, l_refs = refs[:n], refs[n:2 * n]
        send_sems, recv_sems = refs[2 * n], refs[2 * n + 1]
        x, y, c, chips = _place()
        for i in range(n):
            for j, (px, py) in enumerate(chips):
                cp = pltpu.make_async_remote_copy(src_ref=p_refs[i].at[2 * px + py], dst_ref=l_refs[i].at[2 * px + py],
                                                  send_sem=send_sems.at[3 * i + j], recv_sem=recv_sems.at[3 * i + j],
                                                  device_id=(px, py, c), device_id_type=MESH)
                cp.wait_send()
                cp.wait_recv()

    outs = pl.pallas_call(
        body, name=name, in_specs=[_HBM] * (2 * n) + [_SEM, _SEM, _ANY], out_specs=[_HBM] * (2 * n),
        out_shape=[pltpu.HBM(p.shape, p.dtype) for p in ps] * 2, input_output_aliases={i: i for i in range(2 * n)},
        compiler_params=pltpu.CompilerParams(has_side_effects=_EFFECT),
    )(*ps, *lands, send_sems, recv_sems, after)
    return list(outs[:n]), list(outs[n:])


def _sibling_share_start(bufs, after, *, name):
    n = len(bufs)

    def body(*refs):
        b_refs = refs[:n]
        send_sems, recv_sems = refs[n + 1], refs[n + 2]
        token = refs[-1]
        x, y, c, _ = _place()
        for i in range(n):
            pltpu.make_async_remote_copy(src_ref=b_refs[i].at[c], dst_ref=b_refs[i].at[c], send_sem=send_sems.at[i],
                                         recv_sem=recv_sems.at[i], device_id=(x, y, 1 - c), device_id_type=MESH).start()
        token[...] = jnp.zeros_like(token)

    tspec, tshape = _token_spec()
    outs = pl.pallas_call(
        body, name=name, in_specs=[_HBM] * n + [_ANY], out_specs=(_SEM, _SEM, *[_HBM] * n, tspec),
        out_shape=(pltpu.SemaphoreType.DMA((n,)), pltpu.SemaphoreType.DMA((n,)),
                   *[pltpu.HBM(a.shape, a.dtype) for a in bufs], tshape),
        input_output_aliases={i: 2 + i for i in range(n)},
        compiler_params=pltpu.CompilerParams(has_side_effects=_EFFECT),
    )(*[_in_hbm(a) for a in bufs], after)
    return outs[0], outs[1], list(outs[2:2 + n]), outs[-1]


def _sibling_share_wait(send_sems, recv_sems, bufs, after, *, name):
    n = len(bufs)

    def body(*refs):
        b_refs = refs[:n]
        send_sems, recv_sems = refs[n], refs[n + 1]
        x, y, c, _ = _place()
        for i in range(n):
            cp = pltpu.make_async_remote_copy(src_ref=b_refs[i].at[c], dst_ref=b_refs[i].at[1 - c],
                                              send_sem=send_sems.at[i], recv_sem=recv_sems.at[i],
                                              device_id=(x, y, 1 - c), device_id_type=MESH)
            cp.wait_send()
            cp.wait_recv()

    outs = pl.pallas_call(
        body, name=name, in_specs=[_HBM] * n + [_SEM, _SEM, _ANY], out_specs=[_HBM] * n,
        out_shape=[pltpu.HBM(a.shape, a.dtype) for a in bufs], input_output_aliases={i: i for i in range(n)},
        compiler_params=pltpu.CompilerParams(has_side_effects=_EFFECT),
    )(*bufs, send_sems, recv_sems, after)
    return list(outs)


def _broadcast_all(v, after, *, name):
    def body(v_ref, after_ref, o_ref, send_sems, recv_sems, local_sem):
        x, y, c, _ = _place()
        me = 4 * x + 2 * y + c
        loc = pltpu.make_async_copy(v_ref, o_ref.at[me], local_sem)
        loc.start()
        copies = []
        for k in range(1, 8):
            dx, dy, dc = (k >> 2) & 1, (k >> 1) & 1, k & 1
            to = (1 - x if dx else x, 1 - y if dy else y, 1 - c if dc else c)
            cp = pltpu.make_async_remote_copy(src_ref=v_ref, dst_ref=o_ref.at[me], send_sem=send_sems.at[k - 1],
                                              recv_sem=recv_sems.at[k - 1], device_id=to, device_id_type=MESH)
            cp.start()
            copies.append((cp, k, to))
        for cp, k, to in copies:
            cp.wait_send()
            theirs = o_ref.at[4 * to[0] + 2 * to[1] + to[2]]
            pltpu.make_async_remote_copy(src_ref=theirs, dst_ref=theirs, send_sem=send_sems.at[k - 1],
                                         recv_sem=recv_sems.at[k - 1], device_id=to, device_id_type=MESH).wait_recv()
        loc.wait()

    return pl.pallas_call(
        body, name=name, in_specs=[_ANY, _ANY], out_specs=_ANY,
        out_shape=jax.ShapeDtypeStruct((8,) + v.shape, v.dtype),
        scratch_shapes=[pltpu.SemaphoreType.DMA((7,)), pltpu.SemaphoreType.DMA((7,)), pltpu.SemaphoreType.DMA(())],
    )(v, after)


def _gather_place(shards, place, *, name, after=None):
    names = list(shards)
    bufs, shapes = [], []
    for k in names:
        w, layer = shards[k] if isinstance(shards[k], tuple) else (shards[k], None)
        bufs.append(_place_shard(w, place, dtype=F32 if k == 'small' else BF16, layer=layer, after=after,
                                 name=f"{name}_place_{k}"))
        after = bufs[-1] if after is not None else None
        shapes.append(w.shape[-2:])
    return names, shapes, bufs


def _gather_begin(placed, after, *, name):
    names, shapes, bufs = placed
    send_sems, recv_sems, bufs, token = _gather_ici_start(bufs, after, name=name + "_ici_start")
    return (names, shapes, send_sems, recv_sems, bufs), token


def _gather_end(state, after, *, name):
    names, shapes, send_sems, recv_sems, bufs = state
    bufs = _gather_ici_wait(send_sems, recv_sems, bufs, after, name=name + "_ici_wait")
    bufs = _gather_d2d(bufs, name=name + "_d2d")
    return {k: o.reshape((N_CHIPS,) + sh) for k, o, sh in zip(names, bufs, shapes)}


def _gather_forward(state, after, *, name):
    names, shapes, send_sems, recv_sems, bufs = state
    bufs = _gather_ici_wait(send_sems, recv_sems, bufs, after, name=name + "_ici_wait")
    send_sems, recv_sems, bufs, token = _gather_d2d_start(bufs, after, name=name + "_d2d_start")
    return (names, shapes, send_sems, recv_sems, bufs), token


def _gather_finish(state, after, *, name):
    names, shapes, send_sems, recv_sems, bufs = state
    bufs = _gather_d2d_wait(send_sems, recv_sems, bufs, after, name=name + "_d2d_wait")
    return {k: o.reshape((N_CHIPS,) + sh) for k, o, sh in zip(names, bufs, shapes)}


def _reduce_sibling_start(grads, after, *, name):
    names = list(grads)
    gs = [grads[k].reshape(N_CHIPS, 2, grads[k].shape[1] // 2, grads[k].shape[2]) for k in names]
    send_sems, recv_sems, gs, recvs, token = _sibling_halves_start(gs, after, name=name + "_sib_start")
    return (names, [grads[k].shape[1:] for k in names], send_sems, recv_sems, gs, recvs), token


def _reduce_begin(state, place, after, *, name):
    names, shapes, send_sems, recv_sems, gs, recvs = state
    gs, recvs = _sibling_halves_wait(send_sems, recv_sems, gs, recvs, after, name=name + "_sib_wait")
    ps = [_add_halves(g, r, place, name=f"{name}_add2_{k}") for g, r, k in zip(gs, recvs, names)]
    send_sems, recv_sems, ps, lands, token = _chip_scatter_start(ps, recvs[0], name=name + "_scatter_start")
    return (names, shapes, send_sems, recv_sems, ps, lands), token


def _reduce_end(state, place, after, *, name):
    names, shapes, send_sems, recv_sems, ps, lands = state
    ps, lands = _chip_scatter_wait(send_sems, recv_sems, ps, lands, after, name=name + "_scatter_wait")
    rs = [_sum_chips(p, l, place, name=f"{name}_sum4_{k}") for p, l, k in zip(ps, lands, names)]
    send_sems, recv_sems, rs, _ = _sibling_share_start(rs, lands[0], name=name + "_share_start")
    return names, shapes, send_sems, recv_sems, rs


def _reduce_finish(state, after, *, name):
    names, shapes, send_sems, recv_sems, rs = state
    both = _sibling_share_wait(send_sems, recv_sems, rs, after, name=name + "_share_wait")
    return {k: b.reshape(sh) for k, b, sh in zip(names, both, shapes)}


def _pad_lanes(a, n=LANES):
    return jnp.pad(a, [(0, 0)] * (a.ndim - 1) + [(0, n - a.shape[-1])])


def _unshard_cols(g):
    return jnp.transpose(g, (1, 0, 2)).reshape(g.shape[1], -1)


def _shard_cols(w):
    k, n = w.shape
    return jnp.transpose(w.reshape(k, N_CHIPS, n // N_CHIPS), (1, 0, 2))


def _ffn_fwd(h, p, tag, before_out):
    b = _rms_fwd(h, p['ffn_norm'], name=f"{tag}_ffn_norm")
    up = _mm(b, p['ffn_w_up'], b_sh='n', name=f"{tag}_ffn_up", bn=1408)
    act = _glu_fwd(up, p['ffn_conv_w'], p['ffn_conv_b'], name=f"{tag}_ffn_glu")
    out = _mm(act, p['ffn_w_down'], res=h, after=before_out(act), name=f"{tag}_ffn_down", bk=704)
    return out, (h, b, up, act)


def _ffn_bwd(dh, saved, p, tag, after, on_big):
    h, b, up, act = saved
    dact = _mm(dh, p['ffn_w_down'], tb=True, after=after, out_dtype=BF16, name=f"{tag}_ffn_dact", bn=1408)
    dw_down = _mm(act, dh, ta=True, after=after, out_dtype=BF16, name=f"{tag}_ffn_dwdown", bm=1408)
    dup, dcw, dcb = _glu_bwd(up, p['ffn_conv_w'], p['ffn_conv_b'], dact, name=f"{tag}_ffn_dglu")
    dw_up = _mm(b, dup, ta=True, b_sh='n', o_sh=True, out_dtype=BF16, name=f"{tag}_ffn_dwup", bn=1408)
    sent = on_big({'ffn_w_up': dw_up, 'ffn_w_down': dw_down.reshape(N_CHIPS, -1, dw_down.shape[1])})
    db = _mm(dup, p['ffn_w_up'], a_sh=True, b_sh='k', after=sent, name=f"{tag}_ffn_db", bk=1408)
    dcw = jnp.transpose(dcw, (1, 0, 2)).reshape(dcw.shape[1], -1)
    dcb = dcb.reshape(1, -1)
    dh_in, dg = _rms_bwd(h, p['ffn_norm'], db, res=dh, name=f"{tag}_ffn_dnorm")
    small = {'ffn_norm': dg, 'ffn_conv_w': dcw, 'ffn_conv_b': dcb}
    return dh_in, small


def _qkv_attn_fwd(kind, h, p, tag, n_heads, before_out):
    a = _rms_fwd(h, p['mix_norm'], name=f"{tag}_norm")
    if kind == 'fox':
        qkv = _mm(a, p['w_in'], name=f"{tag}_qkv", bn=896)
        cum = _fgate_fwd(qkv, p['b_f'], fcol=3 * n_heads, name=f"{tag}_fgate")
        cum_t = cum[:, :n_heads].T
        cq, ck = cum_t[:, :, None], cum_t[:, None, :]
    else:
        qkv = _mm(a, p['w_in'], b_sh='n', name=f"{tag}_qkv", bn=768)
        cq = ck = None
    cols = dict(qcol=lambda hh: hh, kcol=lambda hh: n_heads + hh, vcol=lambda hh: 2 * n_heads + hh)
    o = _attn_fwd(kind, qkv, qkv, qkv, name=f"{tag}_attn", n_heads=n_heads, dqk=HEAD_DIM, scale=HEAD_DIM ** -0.5,
                  gains=p['qk_gain'], cq=cq, ck=ck, **cols)
    out = _mm(o, p['w_out'], res=h, after=before_out(o), name=f"{tag}_out")
    return out, (h, a, qkv, o, cq, ck)


def _qkv_attn_bwd(kind, dh, saved, p, tag, n_heads, after, on_big):
    h, a, qkv, o, cq, ck = saved
    do = _mm(dh, p['w_out'], tb=True, after=after, out_dtype=BF16, name=f"{tag}_do")
    dw_out = _mm(o, dh, ta=True, after=after, out_dtype=BF16, name=f"{tag}_dwout")
    cols = dict(qcol=lambda hh: hh, kcol=lambda hh: n_heads + hh, vcol=lambda hh: 2 * n_heads + hh)
    outs = _attn_bwd(kind, qkv, qkv, qkv, o, do, name=f"{tag}_dattn", n_heads=n_heads, dqk=HEAD_DIM,
                     scale=HEAD_DIM ** -0.5, gains=p['qk_gain'], cq=cq, ck=ck, **cols)
    dq, dk, dv, dgain = outs[:4]
    small = {'q_gain': dgain[0], 'k_gain': dgain[1]}
    if kind == 'fox':
        dcq, dck = outs[4:]
        dca = _pad_lanes(dcq[:, :, 0].T)
        dcb = _pad_lanes(dck[:, 0, :].T)
        dflog, dbf = _fgate_bwd(qkv, p['b_f'], dca, dcb, fcol=3 * n_heads, n_heads=n_heads, name=f"{tag}_dfgate")
        small['b_f'] = dbf[:, :n_heads]
        dqkv = jnp.concatenate([dq, dk, dv, dflog], axis=1)
        dw_in = _mm(a, dqkv, ta=True, out_dtype=BF16, name=f"{tag}_dwin", bn=896)
        dw_in = _shard_cols(dw_in[:, :3 * n_heads * HEAD_DIM + n_heads])
    else:
        dqkv = jnp.concatenate([dq, dk, dv], axis=1)
        dw_in = _mm(a, dqkv, ta=True, o_sh=True, out_dtype=BF16, name=f"{tag}_dwin", bn=768)
    sent = on_big({'w_in': dw_in, 'w_out': dw_out.reshape(N_CHIPS, -1, dw_out.shape[1])})
    if kind == 'fox':
        da = _mm(dqkv, p['w_in'], tb=True, after=sent, name=f"{tag}_da", bk=896)
    else:
        da = _mm(dqkv, p['w_in'], b_sh='k', after=sent, name=f"{tag}_da", bk=768)
    dh_in, dg = _rms_bwd(h, p['mix_norm'], da, res=dh, name=f"{tag}_dnorm")
    small['mix_norm'] = dg
    return dh_in, small


def _mla_fwd(h, p, tag, n_heads, before_out):
    a = _rms_fwd(h, p['mix_norm'], name=f"{tag}_norm")
    c = _mm(a, p['w_in'], name=f"{tag}_latent", bn=1152)
    cn = _mla_latent_fwd(c, p['a_gain'], name=f"{tag}_latent_norm")
    qp = _mm(cn[:, :MLA_Q_RANK], p['w_q_b'], name=f"{tag}_q_up")
    kv = _mm(cn[:, MLA_Q_RANK:], p['w_kv_b'], b_sh='n', name=f"{tag}_kv_up")
    qc, kc = _mla_prep_fwd(qp, kv, c, p['gq'], p['gk'], p['cos'], p['sin'], n_heads=n_heads, name=f"{tag}_prep")
    cols = dict(qcol=lambda hh: hh, kcol=lambda hh: hh, vcol=lambda hh: 2 * hh + 1)
    scale = (MLA_NOPE + MLA_ROPE) ** -0.5
    o = _attn_fwd('mla', qc, kc, kv, name=f"{tag}_attn", n_heads=n_heads, dqk=2 * LANES, scale=scale, **cols)
    out = _mm(o, p['w_out'], res=h, after=before_out(o), name=f"{tag}_out")
    return out, (h, a, c, cn, qp, kv, qc, kc, o)


def _mla_bwd(dh, saved, p, tag, n_heads, after, on_big):
    h, a, c, cn, qp, kv, qc, kc, o = saved
    do = _mm(dh, p['w_out'], tb=True, after=after, out_dtype=BF16, name=f"{tag}_do")
    dw_out = _mm(o, dh, ta=True, after=after, out_dtype=BF16, name=f"{tag}_dwout")
    cols = dict(qcol=lambda hh: hh, kcol=lambda hh: hh, vcol=lambda hh: 2 * hh + 1)
    scale = (MLA_NOPE + MLA_ROPE) ** -0.5
    dqc, dkc, dv = _attn_bwd('mla', qc, kc, kv, o, do, name=f"{tag}_dattn", n_heads=n_heads, dqk=2 * LANES,
                             scale=scale, **cols)
    dqn, dqr, dkv, dkr, dgq, dgk = _mla_prep_bwd(qp, kv, c, p['gq'], p['gk'], p['cos'], p['sin'], dqc, dkc, dv,
                                                 n_heads=n_heads, name=f"{tag}_dprep")
    dqp = jnp.concatenate([dqn, dqr], axis=1)
    cn_q, cn_kv = cn[:, :MLA_Q_RANK], cn[:, MLA_Q_RANK:]
    dw_q_b = _mm(cn_q, dqp, ta=True, out_dtype=BF16, name=f"{tag}_dwqb", bm=512)
    dcn_q = _mm(dqp, p['w_q_b'], tb=True, out_dtype=BF16, name=f"{tag}_dcnq")
    dw_kv_b = _mm(cn_kv, dkv, ta=True, o_sh=True, out_dtype=BF16, name=f"{tag}_dwkvb", bm=512)
    dcn_kv = _mm(dkv, p['w_kv_b'], b_sh='k', out_dtype=BF16, name=f"{tag}_dcnkv")
    dc, dga = _mla_latent_bwd(c, p['a_gain'], dcn_q, dcn_kv, dkr, name=f"{tag}_dlatent")
    dw_in = _mm(a, dc, ta=True, out_dtype=BF16, name=f"{tag}_dwin", bn=1152)
    k_rank = dw_q_b.shape[0]
    nope = dw_q_b[:, :n_heads * LANES].reshape(k_rank, n_heads, LANES)
    rope = dw_q_b[:, n_heads * LANES:].reshape(k_rank, n_heads, LANES)[:, :, :MLA_ROPE]
    dw_q_b = jnp.concatenate([nope, rope], axis=2).reshape(k_rank, n_heads * (MLA_NOPE + MLA_ROPE))
    w_in_cols = MLA_Q_RANK + MLA_KV_RANK + MLA_ROPE
    sent = on_big({'w_in': dw_in[:, :w_in_cols].reshape(N_CHIPS, -1, w_in_cols), 'w_q_b': _shard_cols(dw_q_b),
                   'w_kv_b': dw_kv_b, 'w_out': dw_out.reshape(N_CHIPS, -1, dw_out.shape[1])})
    da = _mm(dc, p['w_in'], tb=True, after=sent, name=f"{tag}_da", bk=1152)
    dh_in, dg = _rms_bwd(h, p['mix_norm'], da, res=dh, name=f"{tag}_dnorm")
    small = {'mix_norm': dg, 'q_a_gain': dga[:, :MLA_Q_RANK], 'kv_a_gain': dga[:, MLA_Q_RANK:],
             'q_gain': jnp.concatenate([dgq[0], dgq[1][:, :MLA_ROPE]], axis=1),
             'k_gain': jnp.concatenate([dgk[0], dgk[1][:, :MLA_ROPE]], axis=1)}
    return dh_in, small


def _sgu_fwd(h, p, tag, before_out):
    a = _rms_fwd(h, p['mix_norm'], name=f"{tag}_norm")
    uv = _mm(a, p['w_in'], b_sh='n', name=f"{tag}_in")
    u, vn = _sgu_act_fwd(uv, p['v_gain'], name=f"{tag}_act")
    gated = _sgu_mix_fwd(u, vn, p['w_s'], p['b_s'], name=f"{tag}_mix")
    out = _mm(gated, p['w_out'], res=h, after=before_out(gated), name=f"{tag}_out")
    return out, (h, a, uv, u, vn, gated)


def _sgu_bwd(dh, saved, p, tag, after, on_big):
    h, a, uv, u, vn, gated = saved
    dgated = _mm(dh, p['w_out'], tb=True, after=after, out_dtype=BF16, name=f"{tag}_dgated")
    dw_out = _mm(gated, dh, ta=True, after=after, out_dtype=BF16, name=f"{tag}_dwout")
    du, dvn, dws, dbs = _sgu_mix_bwd(u, vn, p['w_s'], p['b_s'], dgated, name=f"{tag}_dmix")
    duv, dvg = _sgu_act_bwd(uv, p['v_gain'], du, dvn, name=f"{tag}_dact")
    dw_in = _mm(a, duv, ta=True, o_sh=True, out_dtype=BF16, name=f"{tag}_dwin")
    sent = on_big({'w_in': dw_in, 'w_out': dw_out.reshape(N_CHIPS, -1, dw_out.shape[1])})
    da = _mm(duv, p['w_in'], b_sh='k', after=sent, name=f"{tag}_da")
    dh_in, dg = _rms_bwd(h, p['mix_norm'], da, res=dh, name=f"{tag}_dnorm")
    small = {'mix_norm': dg, 'v_gain': dvg, 'w_s': dws, 'b_s': dbs[:, :, 0]}
    return dh_in, small


def _pack(parts):
    flat = jnp.concatenate([p.reshape(-1).astype(F32) for p in parts])
    rows = -(-flat.shape[0] // LANES)
    rows = -(-rows // 32) * 32
    return jnp.pad(flat, (0, rows * LANES - flat.shape[0])).reshape(rows, LANES)


def _unpack(packed, shapes):
    flat = packed.reshape(-1)
    out, off = [], 0
    for s in shapes:
        n = 1
        for d in s:
            n *= d
        out.append(flat[off:off + n].reshape(s))
        off += n
    return out


MIXERS = ('fox', 'mla', 'sb', 'sgu')
WEIGHT_NAMES = ['mix_norm', 'ffn_norm', 'fox_w_in', 'fox_b_f', 'fox_q_gain', 'fox_k_gain', 'fox_w_out', 'mla_w_in',
                'mla_q_a_gain', 'mla_kv_a_gain', 'mla_w_q_b', 'mla_w_kv_b', 'mla_q_gain', 'mla_k_gain', 'mla_w_out',
                'sb_w_in', 'sb_q_gain', 'sb_k_gain', 'sb_w_out', 'sgu_w_in', 'sgu_v_gain', 'sgu_w_s', 'sgu_b_s',
                'sgu_w_out', 'ffn_w_up', 'ffn_conv_w', 'ffn_conv_b', 'ffn_w_down']
SMALL_SHARDED = {'mla_q_a_gain': 1, 'mla_kv_a_gain': 1, 'sgu_v_gain': 1, 'ffn_conv_w': 2}
BIG = ['fox_w_in', 'fox_w_out', 'mla_w_in', 'mla_w_q_b', 'mla_w_kv_b', 'mla_w_out', 'sb_w_in', 'sb_w_out', 'sgu_w_in',
       'sgu_w_out', 'ffn_w_up', 'ffn_w_down']


def kernel(x, positions, mix_norm, ffn_norm, fox_w_in, fox_b_f, fox_q_gain, fox_k_gain, fox_w_out, mla_w_in, mla_q_a_gain, mla_kv_a_gain, mla_w_q_b, mla_w_kv_b, mla_q_gain, mla_k_gain, mla_w_out, sb_w_in, sb_q_gain, sb_k_gain, sb_w_out, sgu_w_in, sgu_v_gain, sgu_w_s, sgu_b_s, sgu_w_out, ffn_w_up, ffn_conv_w, ffn_conv_b, ffn_w_down, loss_target, m_mix_norm, m_ffn_norm, m_fox_w_in, m_fox_b_f, m_fox_q_gain, m_fox_k_gain, m_fox_w_out, m_mla_w_in, m_mla_q_a_gain, m_mla_kv_a_gain, m_mla_w_q_b, m_mla_w_kv_b, m_mla_q_gain, m_mla_k_gain, m_mla_w_out, m_sb_w_in, m_sb_q_gain, m_sb_k_gain, m_sb_w_out, m_sgu_w_in, m_sgu_v_gain, m_sgu_w_s, m_sgu_b_s, m_sgu_w_out, m_ffn_w_up, m_ffn_conv_w, m_ffn_conv_b, m_ffn_w_down, v_mix_norm, v_ffn_norm, v_fox_w_in, v_fox_b_f, v_fox_q_gain, v_fox_k_gain, v_fox_w_out, v_mla_w_in, v_mla_q_a_gain, v_mla_kv_a_gain, v_mla_w_q_b, v_mla_w_kv_b, v_mla_q_gain, v_mla_k_gain, v_mla_w_out, v_sb_w_in, v_sb_q_gain, v_sb_k_gain, v_sb_w_out, v_sgu_w_in, v_sgu_v_gain, v_sgu_w_s, v_sgu_b_s, v_sgu_w_out, v_ffn_w_up, v_ffn_conv_w, v_ffn_conv_b, v_ffn_w_down):
    args = dict(locals())
    W = {k: args[k] for k in WEIGHT_NAMES}
    M = {k: args['m_' + k] for k in WEIGHT_NAMES}
    V = {k: args['v_' + k] for k in WEIGHT_NAMES}
    depth = mix_norm.shape[0]
    s_len, d_model = x.shape[1], x.shape[2]
    n_heads = d_model // HEAD_DIM
    assert all(W[k].shape[0] == 1 for k in WEIGHT_NAMES if k.split('_')[0] in MIXERS), "one layer per mixer"
    xi, yi, ci = lax.axis_index("x"), lax.axis_index("y"), lax.axis_index("c")
    chip = 2 * xi + yi
    place = tuple(jnp.reshape(v, (1,)).astype(jnp.int32) for v in (xi, yi, ci))

    small_local = _pack([W[k][0] if k != 'ffn_conv_w' else W[k] for k in SMALL_SHARDED])

    def piece_shards(i, part):
        if part == 'ffn':
            return {'ffn_w_up': (W['ffn_w_up'], i), 'ffn_w_down': (W['ffn_w_down'], i)}
        mixer = MIXERS[i % len(MIXERS)]
        shards = {k: W[k][0] for k in BIG if k.startswith(mixer + '_')}
        if i == 0:
            shards['small'] = small_local
        return shards

    pieces = [(i, part) for i in range(depth) for part in ('mixer', 'ffn')]
    gathered = {}
    pname = lambda pc: f"gather_l{pc[0]}_{pc[1]}"
    states = {}
    placed = {pieces[0]: _gather_place(piece_shards(*pieces[0]), place, name=pname(pieces[0]))}
    states[pieces[0]], token = _gather_begin(placed[pieces[0]], mix_norm, name=pname(pieces[0]))
    prev = token
    for pc in pieces[1:]:
        placed[pc] = _gather_place(piece_shards(*pc), place, name=pname(pc), after=prev)
        prev = placed[pc][2][-1]
    gathered[pieces[0]] = _gather_end(states.pop(pieces[0]), placed[pieces[-1]][2][-1], name=pname(pieces[0]))
    first_done = next(iter(gathered[pieces[0]].values()))
    states[pieces[1]], token = _gather_begin(placed[pieces[1]], first_done, name=pname(pieces[1]))
    tokens = [token]
    small_shapes = [W[k][0].shape if k != 'ffn_conv_w' else W[k].shape for k in SMALL_SHARDED]
    per_chip = [_unpack(gathered[pieces[0]]['small'][s], small_shapes) for s in range(N_CHIPS)]
    full_small = {k: jnp.concatenate([per_chip[s][j] for s in range(N_CHIPS)], axis=-1)
                  for j, k in enumerate(SMALL_SHARDED)}

    pos = positions.reshape(s_len).astype(F32)
    inv_freq = ROPE_THETA ** (-jnp.arange(0, MLA_ROPE, 2, dtype=F32) / MLA_ROPE)
    ang = pos[:, None] * inv_freq
    cos_t = _pad_lanes(jnp.concatenate([jnp.cos(ang), jnp.cos(ang)], axis=1))
    sin_t = _pad_lanes(jnp.concatenate([-jnp.sin(ang), jnp.sin(ang)], axis=1))

    def piece_params(i, part):
        mixer = MIXERS[i % len(MIXERS)]
        g = gathered[(i, part)]
        if part == 'ffn':
            return mixer, {'ffn_norm': ffn_norm[i:i + 1], 'ffn_w_up': g['ffn_w_up'],
                           'ffn_w_down': g['ffn_w_down'].reshape(-1, d_model),
                           'ffn_conv_w': full_small['ffn_conv_w'][i], 'ffn_conv_b': ffn_conv_b[i:i + 1]}
        p = {'mix_norm': mix_norm[i:i + 1]}
        rows = lambda w: w.reshape(-1, w.shape[-1])
        if mixer == 'fox':
            w = _unshard_cols(g['fox_w_in'])
            p['w_in'] = jnp.pad(w, ((0, 0), (0, (3 * n_heads + 1) * HEAD_DIM - w.shape[1])))
            p['b_f'] = _pad_lanes(fox_b_f)
            p['qk_gain'] = jnp.stack([fox_q_gain, fox_k_gain])
            p['w_out'] = rows(g['fox_w_out'])
        elif mixer == 'sb':
            p['w_in'] = g['sb_w_in']
            p['qk_gain'] = jnp.stack([sb_q_gain, sb_k_gain])
            p['w_out'] = rows(g['sb_w_out'])
        elif mixer == 'sgu':
            p['w_in'] = g['sgu_w_in']
            p['v_gain'] = full_small['sgu_v_gain'].reshape(1, -1)
            p['w_s'] = sgu_w_s[0]
            p['b_s'] = sgu_b_s[0][:, :, None]
            p['w_out'] = rows(g['sgu_w_out'])
        else:
            w = rows(g['mla_w_in'])
            p['w_in'] = jnp.pad(w, ((0, 0), (0, MLA_Q_RANK + MLA_KV_RANK + LANES - w.shape[1])))
            p['a_gain'] = jnp.concatenate([full_small['mla_q_a_gain'], full_small['mla_kv_a_gain']]).reshape(1, -1)
            wq = _unshard_cols(g['mla_w_q_b']).reshape(MLA_Q_RANK, n_heads, MLA_NOPE + MLA_ROPE)
            p['w_q_b'] = jnp.concatenate([wq[:, :, :MLA_NOPE].reshape(MLA_Q_RANK, -1),
                                          _pad_lanes(wq[:, :, MLA_NOPE:]).reshape(MLA_Q_RANK, -1)], axis=1)
            p['w_kv_b'] = g['mla_w_kv_b']
            p['gq'] = jnp.stack([mla_q_gain[:, :MLA_NOPE], _pad_lanes(mla_q_gain[:, MLA_NOPE:])])
            p['gk'] = jnp.stack([mla_k_gain[:, :MLA_NOPE], _pad_lanes(mla_k_gain[:, MLA_NOPE:])])
            p['cos'], p['sin'] = cos_t, sin_t
            p['w_out'] = rows(g['mla_w_out'])
        return mixer, p

    h = x.reshape(s_len, d_model)
    saved = []
    for n, (i, part) in enumerate(pieces):
        nxt = pieces[n + 1] if n + 1 < len(pieces) else None
        ahead = pieces[n + 2] if n + 2 < len(pieces) else None
        if ahead is not None:
            after = next(iter(gathered[(i, part)].values()))
            states[ahead], token = _gather_begin(placed[ahead], after, name=pname(ahead))
            tokens.append(token)
        mixer, p = piece_params(i, part)
        gain = 'ffn_norm' if part == 'ffn' else 'mix_norm'
        for token in tokens:
            p[gain] = p[gain] + token[0:1, 0:1]
        tokens = []
        tag = f"l{i}_{mixer}"
        forwarding = []

        def before_out(made, nxt=nxt, forwarding=forwarding):
            if nxt is None:
                return None
            st, tok = _gather_forward(states.pop(nxt), made, name=pname(nxt))
            forwarding.append(st)
            return tok

        if part == 'ffn':
            h, sv = _ffn_fwd(h, p, f"l{i}", before_out)
        elif mixer in ('fox', 'sb'):
            h, sv = _qkv_attn_fwd(mixer, h, p, tag, n_heads, before_out)
        elif mixer == 'mla':
            h, sv = _mla_fwd(h, p, tag, n_heads, before_out)
        else:
            h, sv = _sgu_fwd(h, p, tag, before_out)
        saved.append((mixer, p, sv))
        if nxt is not None:
            gathered[nxt] = _gather_finish(forwarding[0], h, name=pname(nxt))
    loss_row, dh = _loss(h, loss_target.reshape(s_len, d_model))
    loss = lax.psum(loss_row[0, 0], ("x", "y", "c"))

    big_grads, small_grads = {}, {k: [None] * depth for k in ('mix_norm', 'ffn_norm', 'ffn_conv_w', 'ffn_conv_b')}

    def keep(reduced, i):
        for k, v in reduced.items():
            if k.startswith('ffn_'):
                big_grads.setdefault(k, [None] * depth)[i] = v
            else:
                big_grads[k] = v[None]

    state, token, flying = None, jnp.broadcast_to(loss, (8, LANES)), None
    shares = []
    for n in reversed(range(len(pieces))):
        i, part = pieces[n]
        mixer, p, sv = saved[n]
        tag = f"l{i}_{mixer}"
        rname = f"reduce_l{i}_{part}"
        started = []

        def on_big(big, prefix=('' if part == 'ffn' else mixer + '_'), rname=rname, started=started):
            st, tok = _reduce_sibling_start({prefix + k: v for k, v in big.items()}, place[0], name=rname)
            started.append(st)
            return tok

        if part == 'ffn':
            dh, small = _ffn_bwd(dh, sv, p, f"l{i}", token, on_big)
        elif mixer in ('fox', 'sb'):
            dh, small = _qkv_attn_bwd(mixer, dh, sv, p, tag, n_heads, token, on_big)
        elif mixer == 'mla':
            dh, small = _mla_bwd(dh, sv, p, tag, n_heads, token, on_big)
        else:
            dh, small = _sgu_bwd(dh, sv, p, tag, token, on_big)
        if state is not None:
            fname = f"reduce_l{flying[0]}_{flying[1]}"
            shares.append((_reduce_end(state, place, dh, name=fname), flying[0], fname))
        state, token = _reduce_begin(started[0], place, dh, name=rname)
        flying = (i, part)
        for k, v in small.items():
            if k in small_grads:
                small_grads[k][i] = v
            else:
                small_grads[f"{mixer}_{k}"] = v
    last_state = state
    grad_x = dh.reshape(x.shape)
    for k in ('mix_norm', 'ffn_norm', 'ffn_conv_b'):
        small_grads[k] = jnp.concatenate(small_grads[k], axis=0)
    small_grads['ffn_conv_w'] = jnp.stack(small_grads['ffn_conv_w'])

    small_names = [k for k in WEIGHT_NAMES if k not in BIG]
    full_shapes = {k: (W[k].shape[:-1] + (W[k].shape[-1] * N_CHIPS,) if k in SMALL_SHARDED else W[k].shape)
                   for k in small_names}
    packed = _pack([small_grads[k].reshape(full_shapes[k]) for k in small_names])
    last_token = token
    summed = _sum_devices(_broadcast_all(packed, last_token, name="small_bcast"), name="small_sum")
    small_full = dict(zip(small_names, _unpack(summed, [full_shapes[k] for k in small_names])))
    for share_state, i, fname in shares:
        keep(_reduce_finish(share_state, summed, name=fname), i)
    for k in ('ffn_w_up', 'ffn_w_down'):
        big_grads[k] = jnp.stack(big_grads[k])
    grads = dict(big_grads)
    for k in small_names:
        g = small_full[k]
        if k in SMALL_SHARDED:
            n = W[k].shape[-1]
            g = lax.dynamic_slice_in_dim(g, chip * n, n, axis=g.ndim - 1)
        grads[k] = g

    delta, new_m, new_v = {}, {}, {}

    def update(k):
        grads[k] = grads[k].reshape(W[k].shape)
        delta[k], new_m[k], new_v[k] = _adamw(W[k], grads[k], M[k], V[k], after=last_token, name=f"adamw_{k}")

    last_names = [k for k in BIG if k.startswith(MIXERS[0] + '_')]
    for k in WEIGHT_NAMES:
        if k not in last_names:
            update(k)
    share_state = _reduce_end(last_state, place, delta['ffn_w_up'], name="reduce_l0_mixer")
    keep(_reduce_finish(share_state, delta['ffn_w_down'], name="reduce_l0_mixer"), 0)
    for k in last_names:
        grads[k] = big_grads[k]
        update(k)
    return (loss, grad_x, *[grads[k] for k in WEIGHT_NAMES], *[delta[k] for k in WEIGHT_NAMES],
            *[new_m[k] for k in WEIGHT_NAMES], *[new_v[k] for k in WEIGHT_NAMES])
```

```python
import functools

import jax
import jax.numpy as jnp
from jax import lax
from jax.experimental import pallas as pl
from jax.experimental.pallas import tpu as pltpu

F32 = jnp.float32
BF16 = jnp.bfloat16
LANES = 128
HEAD_DIM = 128
NORM_EPS = 1e-6
MLA_Q_RANK = 512
MLA_KV_RANK = 512
MLA_NOPE = 128
MLA_ROPE = 64
ROPE_THETA = 10000.0
SGU_CHUNK = 128
N_CHIPS = 4
ADAM_LR, ADAM_B1, ADAM_B2, ADAM_EPS, ADAM_WD, ADAM_STEP = 0.001, 0.9, 0.999, 1e-08, 0.01, 10
VMEM_LIMIT_BYTES = 56 * 1024 * 1024
MM_VMEM_BUDGET_BYTES = 36 * 1024 * 1024
SHARD_BLOCK_BYTES = 4 * 1024 * 1024
MESH = pl.DeviceIdType.MESH
NEG_BIG = -1e30


def _params(*sem):
    return pltpu.CompilerParams(dimension_semantics=sem, vmem_limit_bytes=VMEM_LIMIT_BYTES)


def _div_block(n, target, mult=LANES):
    if n <= target:
        return n
    best = None
    for b in range(mult, target + 1, mult):
        if n % b == 0:
            best = b
    assert best is not None, (n, target, mult)
    return best


def _after_spec(after):
    if after is None:
        return []
    return [pl.BlockSpec(memory_space=pl.ANY)]


def _after_arg(after):
    return [] if after is None else [after]


def _iota(shape, dim):
    return lax.broadcasted_iota(jnp.int32, shape, dim)


def _dot(a, b, ca, cb):
    return lax.dot_general(a, b, (((ca,), (cb,)), ((), ())), preferred_element_type=F32)


def _mm(a, b, *, name, ta=False, tb=False, a_sh=False, b_sh=None, o_sh=False, res=None, after=None, out_dtype=F32,
        bm=1024, bn=1024, bk=512):
    if a_sh:
        assert not ta
        m, k = a.shape[1], a.shape[0] * a.shape[2]
    else:
        m, k = (a.shape[1], a.shape[0]) if ta else a.shape
    if b_sh == 'n':
        n = b.shape[2] * b.shape[0]
        assert b.shape[1] == k and not tb
    elif b_sh == 'k':
        n = b.shape[1]
        assert b.shape[2] * N_CHIPS == k
    else:
        n = b.shape[0] if tb else b.shape[1]
        assert (b.shape[1] if tb else b.shape[0]) == k
    n_sh = n // N_CHIPS
    k_sh = k // N_CHIPS
    bm = _div_block(m, bm, 8 if not ta else LANES)
    bn_limit = n
    if b_sh == 'n':
        bn_limit = b.shape[2]
    if o_sh:
        bn_limit = min(bn_limit, n_sh)
    bn = _div_block(bn_limit, bn)
    assert (not o_sh or n_sh % bn == 0) and (b_sh != 'n' or b.shape[2] % bn == 0)
    bk_limit = k_sh if b_sh == 'k' else k
    if a_sh:
        bk_limit = min(bk_limit, a.shape[2])

    def footprint(kb):
        io = bm * kb * a.dtype.itemsize + kb * bn * b.dtype.itemsize + bm * bn * jnp.dtype(out_dtype).itemsize
        if res is not None:
            io += bm * bn * res.dtype.itemsize
        return 2 * io + (bm * bn * 4 if kb < k else 0)

    bk = max([kb for kb in range(LANES, bk_limit + 1, LANES)
              if bk_limit % kb == 0 and (footprint(kb) <= MM_VMEM_BUDGET_BYTES or kb <= bk)])
    assert (not a_sh or a.shape[2] % bk == 0) and (b_sh != 'k' or k_sh % bk == 0) and k % bk == 0
    nbo = n_sh // bn if o_sh else 1
    nbb = b.shape[2] // bn if b_sh == 'n' else 1
    nks = k_sh // bk if b_sh == 'k' else 1
    nka = a.shape[2] // bk if a_sh else 1
    nk = k // bk

    if a_sh:
        a_spec = pl.BlockSpec((None, bm, bk), lambda i, j, q: (q // nka, i, q % nka))
    elif ta:
        a_spec = pl.BlockSpec((bk, bm), lambda i, j, q: (q, i))
    else:
        a_spec = pl.BlockSpec((bm, bk), lambda i, j, q: (i, q))
    if b_sh == 'n':
        b_spec = pl.BlockSpec((None, bk, bn), lambda i, j, q: (j // nbb, q, j % nbb))
    elif b_sh == 'k':
        b_spec = pl.BlockSpec((None, bn, bk), lambda i, j, q: (q // nks, j, q % nks))
    elif tb:
        b_spec = pl.BlockSpec((bn, bk), lambda i, j, q: (j, q))
    else:
        b_spec = pl.BlockSpec((bk, bn), lambda i, j, q: (q, j))
    if o_sh:
        o_spec = pl.BlockSpec((None, bm, bn), lambda i, j, q: (j // nbo, i, j % nbo))
        o_shape = jax.ShapeDtypeStruct((N_CHIPS, m, n_sh), out_dtype)
    else:
        o_spec = pl.BlockSpec((bm, bn), lambda i, j, q: (i, j))
        o_shape = jax.ShapeDtypeStruct((m, n), out_dtype)
    tb_eff = tb or b_sh == 'k'

    def body(a_ref, b_ref, *rest):
        rest = list(rest)
        if after is not None:
            rest.pop(0)
        r_ref = rest.pop(0) if res is not None else None
        o_ref = rest.pop(0)
        part = _dot(a_ref[...].astype(BF16), b_ref[...].astype(BF16), 0 if ta else 1, 1 if tb_eff else 0)

        def finish(r):
            if res is not None:
                r = r + r_ref[...].astype(F32)
            o_ref[...] = r.astype(out_dtype)

        if nk == 1:
            finish(part)
            return
        acc, = rest
        q = pl.program_id(2)

        @pl.when(q == 0)
        def _():
            acc[...] = part

        @pl.when(q > 0)
        def _():
            acc[...] += part

        @pl.when(q == nk - 1)
        def _():
            finish(acc[...])

    ins = [a, b]
    in_specs = [a_spec, b_spec]
    if after is not None:
        ins.append(after)
        in_specs.append(pl.BlockSpec((8, LANES), lambda i, j, q: (0, 0)))
    if res is not None:
        assert not o_sh
        ins.append(res)
        in_specs.append(pl.BlockSpec((bm, bn), lambda i, j, q: (i, j)))
    return pl.pallas_call(
        body, name=name, grid=(m // bm, n // bn, nk), in_specs=in_specs, out_specs=o_spec, out_shape=o_shape,
        scratch_shapes=[pltpu.VMEM((bm, bn), F32)] if nk > 1 else [],
        compiler_params=_params("parallel", "parallel", "arbitrary"))(*ins)


def _rms_fwd(x, g, *, name, out_dtype=BF16, br=256):
    r, c = x.shape
    br = _div_block(r, br, 8)

    def body(x_ref, g_ref, o_ref):
        xv = x_ref[...].astype(F32)
        inv = lax.rsqrt(jnp.mean(xv * xv, axis=-1, keepdims=True) + NORM_EPS)
        o_ref[...] = (xv * inv * g_ref[...]).astype(out_dtype)

    return pl.pallas_call(
        body, name=name, grid=(r // br,),
        in_specs=[pl.BlockSpec((br, c), lambda i: (i, 0)), pl.BlockSpec((1, c), lambda i: (0, 0))],
        out_specs=pl.BlockSpec((br, c), lambda i: (i, 0)), out_shape=jax.ShapeDtypeStruct((r, c), out_dtype),
        compiler_params=_params("parallel"))(x, g)


def _rms_bwd_math(xv, gv, dyv, n):
    inv = lax.rsqrt(jnp.sum(xv * xv, axis=-1, keepdims=True) / n + NORM_EPS)
    xh = xv * inv
    dyg = dyv * gv
    dx = inv * (dyg - xh * (jnp.sum(dyg * xh, axis=-1, keepdims=True) / n))
    return dx, dyv * xh


def _rms_bwd(x, g, dy, *, name, res=None, br=256):
    r, c = x.shape
    br = _div_block(r, br, 8)

    def body(x_ref, g_ref, dy_ref, *rest):
        if res is not None:
            r_ref, dx_ref, dg_ref = rest
        else:
            dx_ref, dg_ref = rest
        dx, dgr = _rms_bwd_math(x_ref[...].astype(F32), g_ref[...], dy_ref[...].astype(F32), c)
        if res is not None:
            dx = dx + r_ref[...]
        dx_ref[...] = dx

        @pl.when(pl.program_id(0) == 0)
        def _():
            dg_ref[...] = jnp.zeros_like(dg_ref)

        dg_ref[...] += jnp.sum(dgr, axis=0, keepdims=True)

    row = pl.BlockSpec((br, c), lambda i: (i, 0))
    vec = pl.BlockSpec((1, c), lambda i: (0, 0))
    ins = [x, g, dy] + ([res] if res is not None else [])
    return pl.pallas_call(
        body, name=name, grid=(r // br,), in_specs=[row, vec, row] + ([row] if res is not None else []),
        out_specs=[row, vec], out_shape=[jax.ShapeDtypeStruct((r, c), F32), jax.ShapeDtypeStruct((1, c), F32)],
        compiler_params=_params("arbitrary"))(*ins)


def _loss(y, target, *, name="loss", br=256):
    r, c = y.shape
    br = _div_block(r, br, 8)

    def body(y_ref, t_ref, l_ref, dy_ref):
        d = y_ref[...] - t_ref[...]
        dy_ref[...] = d * (1.0 / c)

        @pl.when(pl.program_id(0) == 0)
        def _():
            l_ref[...] = jnp.zeros_like(l_ref)

        part = jnp.sum(d * d, axis=0, keepdims=True)
        l_ref[...] += (0.5 / c) * jnp.sum(part, axis=1, keepdims=True) * jnp.ones((1, LANES), F32)

    row = pl.BlockSpec((br, c), lambda i: (i, 0))
    return pl.pallas_call(
        body, name=name, grid=(r // br,), in_specs=[row, row],
        out_specs=[pl.BlockSpec((1, LANES), lambda i: (0, 0)), row],
        out_shape=[jax.ShapeDtypeStruct((1, LANES), F32), jax.ShapeDtypeStruct((r, c), F32)],
        compiler_params=_params("arbitrary"))(y, target)


def _split2(x):
    hi = x.astype(BF16)
    lo = (x - hi.astype(F32)).astype(BF16)
    return hi, lo


def _lane_scan(x, *, suffix):
    rows, n = x.shape
    nb = n // LANES
    a, b = _iota((LANES, LANES), 0), _iota((LANES, LANES), 1)
    tri = ((a > b) if suffix else (a < b)).astype(BF16)
    outs = [None] * nb
    run = jnp.zeros((rows, 1), F32)
    order = range(nb - 1, -1, -1) if suffix else range(nb)
    for blk in order:
        xb = x[:, blk * LANES:(blk + 1) * LANES]
        hi, lo = _split2(xb)
        outs[blk] = _dot(hi, tri, 1, 0) + _dot(lo, tri, 1, 0) + run
        run = run + jnp.sum(xb, axis=-1, keepdims=True)
    return jnp.concatenate(outs, axis=1)


def _softplus(z):
    return jnp.maximum(z, 0.0) + jnp.log(1.0 + jnp.exp(-jnp.abs(z)))


def _head_norm(x, g):
    xv = x.astype(F32)
    inv = lax.rsqrt(jnp.mean(xv * xv, axis=-1, keepdims=True) + NORM_EPS)
    return xv * inv * g


def _attn_weights(kind, qn, kn, scale, qi, bq, bias):
    s = _dot(qn, kn, 1, 1) * scale
    row = qi * bq + _iota(s.shape, 0)
    col = _iota(s.shape, 1)
    if kind == 'sb':
        strict = col < row
        sp = _softplus(s)
        after = _lane_scan(jnp.where(strict, -sp, 0.0), suffix=True)
        w = jnp.where(strict, jnp.exp(s - sp + after), 0.0)
        return w, (strict, s - sp)
    if bias is not None:
        s = s + bias
    s = jnp.where(col <= row, s, NEG_BIG)
    mx = jnp.max(s, axis=-1, keepdims=True)
    e = jnp.exp(s - mx)
    return e, jnp.sum(e, axis=-1, keepdims=True)


def _attn_fwd(kind, q, k, v, *, name, n_heads, dqk, qcol, kcol, vcol, scale, gains=None, cq=None, ck=None, bq=256):
    s_len = q.shape[0]
    bq = _div_block(s_len, bq, 8)
    norm, fox = gains is not None, cq is not None

    def body(*refs):
        refs = list(refs)
        q_ref, k_ref, v_ref = refs[:3]
        rest = refs[3:]
        g_ref = rest.pop(0) if norm else None
        cq_ref, ck_ref = (rest.pop(0), rest.pop(0)) if fox else (None, None)
        o_ref = rest.pop(0)
        qi = pl.program_id(1)
        if norm:
            kn_s, = rest

            @pl.when(qi == 0)
            def _():
                kn_s[...] = _head_norm(k_ref[...], g_ref[1]).astype(BF16)

        def step(n_keys):
            if norm:
                qn = _head_norm(q_ref[...], g_ref[0]).astype(BF16)
                kn = kn_s[0:n_keys, :]
            else:
                qn, kn = q_ref[...].astype(BF16), k_ref[0:n_keys, :].astype(BF16)
            bias = (cq_ref[...] - ck_ref[:, 0:n_keys]) if fox else None
            w, aux = _attn_weights(kind, qn, kn, scale, qi, bq, bias)
            o = _dot(w.astype(BF16), v_ref[0:n_keys, :].astype(BF16), 1, 0)
            if kind != 'sb':
                o = o / aux
            o_ref[...] = o.astype(BF16)

        for qv in range(s_len // bq):
            pl.when(qi == qv)(functools.partial(step, (qv + 1) * bq))

    in_specs = [pl.BlockSpec((bq, dqk), lambda h, i: (i, qcol(h))),
                pl.BlockSpec((s_len, dqk), lambda h, i: (0, kcol(h))),
                pl.BlockSpec((s_len, HEAD_DIM), lambda h, i: (0, vcol(h)))]
    ins = [q, k, v]
    if norm:
        in_specs.append(pl.BlockSpec((2, 1, dqk), lambda h, i: (0, 0, 0)))
        ins.append(gains)
    if fox:
        in_specs += [pl.BlockSpec((None, bq, 1), lambda h, i: (h, i, 0)), pl.BlockSpec((None, 1, s_len), lambda h, i: (h, 0, 0))]
        ins += [cq, ck]
    return pl.pallas_call(
        body, name=name, grid=(n_heads, s_len // bq), in_specs=in_specs,
        out_specs=pl.BlockSpec((bq, HEAD_DIM), lambda h, i: (i, h)),
        out_shape=jax.ShapeDtypeStruct((s_len, n_heads * HEAD_DIM), BF16),
        scratch_shapes=[pltpu.VMEM((s_len, dqk), BF16)] if norm else [],
        compiler_params=_params("parallel", "arbitrary"))(*ins)


def _attn_bwd(kind, q, k, v, o, do, *, name, n_heads, dqk, qcol, kcol, vcol, scale, gains=None, cq=None, ck=None,
              bq=256):
    s_len = q.shape[0]
    bq = _div_block(s_len, bq, 8)
    nq = s_len // bq
    norm, fox = gains is not None, cq is not None

    def body(*refs):
        refs = list(refs)
        q_ref, k_ref, v_ref, o_ref, do_ref = refs[:5]
        rest = refs[5:]
        g_ref = rest.pop(0) if norm else None
        cq_ref, ck_ref = (rest.pop(0), rest.pop(0)) if fox else (None, None)
        dq_ref, dk_ref, dv_ref = rest.pop(0), rest.pop(0), rest.pop(0)
        dg_ref = rest.pop(0) if norm else None
        dcq_ref, dck_ref = (rest.pop(0), rest.pop(0)) if fox else (None, None)
        dk_acc, dv_acc = rest[:2]
        kn_s = rest[2] if norm else None
        h, qi = pl.program_id(0), pl.program_id(1)

        @pl.when(qi == 0)
        def _():
            dk_acc[...] = jnp.zeros_like(dk_acc)
            dv_acc[...] = jnp.zeros_like(dv_acc)
            if fox:
                dck_ref[...] = jnp.zeros_like(dck_ref)
            if norm:
                kn_s[...] = _head_norm(k_ref[...], g_ref[1]).astype(BF16)

        if norm:
            @pl.when((qi == 0) & (h == 0))
            def _():
                dg_ref[...] = jnp.zeros_like(dg_ref)


        def step(n_keys):
            if norm:
                qn = _head_norm(q_ref[...], g_ref[0]).astype(BF16)
                kn = kn_s[0:n_keys, :]
            else:
                qn, kn = q_ref[...].astype(BF16), k_ref[0:n_keys, :].astype(BF16)
            vb = v_ref[0:n_keys, :].astype(BF16)
            dob = do_ref[...].astype(BF16)
            bias = (cq_ref[...] - ck_ref[:, 0:n_keys]) if fox else None
            w, aux = _attn_weights(kind, qn, kn, scale, qi, bq, bias)
            dw = _dot(dob, vb, 1, 1)
            if kind == 'sb':
                strict, log_sig = aux
                g = dw * w
                cc = _lane_scan(g, suffix=False)
                sig = jnp.exp(log_sig)
                ds = jnp.where(strict, g * (1.0 - sig) - cc * sig, 0.0)
                pw = w
            else:
                pw = w / aux
                delta = jnp.sum(do_ref[...].astype(F32) * o_ref[...].astype(F32), axis=-1, keepdims=True)
                ds = pw * (dw - delta)
                if fox:
                    dcq_ref[...] = jnp.sum(ds, axis=1, keepdims=True)
                    dck_ref[:, 0:n_keys] -= jnp.sum(ds, axis=0, keepdims=True)
            dsb = (ds * scale).astype(BF16)
            dqn = _dot(dsb, kn, 1, 0)
            dk_acc[0:n_keys, :] += _dot(dsb, qn, 0, 0)
            dv_acc[0:n_keys, :] += _dot(pw.astype(BF16), dob, 0, 0)
            if norm:
                dq, dgr = _rms_bwd_math(q_ref[...].astype(F32), g_ref[0], dqn, dqk)
                dg_ref[0] += jnp.sum(dgr, axis=0, keepdims=True)
                dq_ref[...] = dq.astype(BF16)
            else:
                dq_ref[...] = dqn.astype(BF16)

        for qv in range(nq):
            pl.when(qi == qv)(functools.partial(step, (qv + 1) * bq))

        @pl.when(qi == nq - 1)
        def _():
            if norm:
                dk, dgr = _rms_bwd_math(k_ref[...].astype(F32), g_ref[1], dk_acc[...], dqk)
                dg_ref[1] += jnp.sum(dgr, axis=0, keepdims=True)
                dk_ref[...] = dk.astype(BF16)
            else:
                dk_ref[...] = dk_acc[...].astype(BF16)
            dv_ref[...] = dv_acc[...].astype(BF16)

    in_specs = [pl.BlockSpec((bq, dqk), lambda h, i: (i, qcol(h))),
                pl.BlockSpec((s_len, dqk), lambda h, i: (0, kcol(h))),
                pl.BlockSpec((s_len, HEAD_DIM), lambda h, i: (0, vcol(h))),
                pl.BlockSpec((bq, HEAD_DIM), lambda h, i: (i, h)),
                pl.BlockSpec((bq, HEAD_DIM), lambda h, i: (i, h))]
    ins = [q, k, v, o, do]
    out_specs = [pl.BlockSpec((bq, dqk), lambda h, i: (i, h)),
                 pl.BlockSpec((s_len, dqk), lambda h, i: (0, h)),
                 pl.BlockSpec((s_len, HEAD_DIM), lambda h, i: (0, h))]
    out_shape = [jax.ShapeDtypeStruct((s_len, n_heads * dqk), BF16), jax.ShapeDtypeStruct((s_len, n_heads * dqk), BF16),
                 jax.ShapeDtypeStruct((s_len, n_heads * HEAD_DIM), BF16)]
    if norm:
        in_specs.append(pl.BlockSpec((2, 1, dqk), lambda h, i: (0, 0, 0)))
        ins.append(gains)
        out_specs.append(pl.BlockSpec((2, 1, dqk), lambda h, i: (0, 0, 0)))
        out_shape.append(jax.ShapeDtypeStruct((2, 1, dqk), F32))
    if fox:
        in_specs += [pl.BlockSpec((None, bq, 1), lambda h, i: (h, i, 0)), pl.BlockSpec((None, 1, s_len), lambda h, i: (h, 0, 0))]
        ins += [cq, ck]
        out_specs += [pl.BlockSpec((None, bq, 1), lambda h, i: (h, i, 0)), pl.BlockSpec((None, 1, s_len), lambda h, i: (h, 0, 0))]
        out_shape += [jax.ShapeDtypeStruct((n_heads, s_len, 1), F32), jax.ShapeDtypeStruct((n_heads, 1, s_len), F32)]
    return pl.pallas_call(
        body, name=name, grid=(n_heads, nq), in_specs=in_specs, out_specs=out_specs, out_shape=out_shape,
        scratch_shapes=[pltpu.VMEM((s_len, dqk), F32), pltpu.VMEM((s_len, HEAD_DIM), F32)]
        + ([pltpu.VMEM((s_len, dqk), BF16)] if norm else []),
        compiler_params=_params("arbitrary", "arbitrary"))(*ins)


def _split3(x):
    hi = x.astype(BF16)
    r1 = x - hi.astype(F32)
    mid = r1.astype(BF16)
    lo = (r1 - mid.astype(F32)).astype(BF16)
    return hi, mid, lo


def _seq_scan(x, *, reverse):
    n = x.shape[0] // LANES
    a, b = _iota((LANES, LANES), 0), _iota((LANES, LANES), 1)
    tri = ((b >= a) if reverse else (b <= a)).astype(BF16)
    outs = [None] * n
    run = jnp.zeros((1, x.shape[1]), F32)
    for blk in (range(n - 1, -1, -1) if reverse else range(n)):
        xb = x[blk * LANES:(blk + 1) * LANES, :]
        hi, mid, lo = _split3(xb)
        outs[blk] = _dot(tri, hi, 1, 0) + _dot(tri, mid, 1, 0) + _dot(tri, lo, 1, 0) + run
        run = run + jnp.sum(xb, axis=0, keepdims=True)
    return jnp.concatenate(outs, axis=0)


def _fgate_fwd(qkvf, b_f, *, fcol, name):
    s_len = qkvf.shape[0]

    def body(f_ref, b_ref, cum_ref):
        z = f_ref[...] + b_ref[...]
        cum_ref[...] = _seq_scan(-_softplus(-z), reverse=False)

    return pl.pallas_call(
        body, name=name, grid=(1,),
        in_specs=[pl.BlockSpec((s_len, LANES), lambda i: (0, fcol)), pl.BlockSpec((1, LANES), lambda i: (0, 0))],
        out_specs=pl.BlockSpec((s_len, LANES), lambda i: (0, 0)), out_shape=jax.ShapeDtypeStruct((s_len, LANES), F32),
        compiler_params=_params("arbitrary"))(qkvf, b_f)


def _fgate_bwd(qkvf, b_f, dcum_a, dcum_b, *, fcol, n_heads, name):
    s_len = qkvf.shape[0]

    def body(f_ref, b_ref, da_ref, db_ref, dz_ref, dbias_ref):
        z = f_ref[...] + b_ref[...]
        dlog = _seq_scan(da_ref[...] + db_ref[...], reverse=True)
        dz = dlog * jnp.exp(-_softplus(z))
        dz = jnp.where(_iota(dz.shape, 1) < n_heads, dz, 0.0)
        dz_ref[...] = dz.astype(BF16)
        dbias_ref[...] = jnp.sum(dz, axis=0, keepdims=True)

    full = pl.BlockSpec((s_len, LANES), lambda i: (0, 0))
    vec = pl.BlockSpec((1, LANES), lambda i: (0, 0))
    return pl.pallas_call(
        body, name=name, grid=(1,),
        in_specs=[pl.BlockSpec((s_len, LANES), lambda i: (0, fcol)), vec, full, full],
        out_specs=[full, vec], out_shape=[jax.ShapeDtypeStruct((s_len, LANES), BF16), jax.ShapeDtypeStruct((1, LANES), F32)],
        compiler_params=_params("arbitrary"))(qkvf, b_f, dcum_a, dcum_b)


def _rope_swap(x):
    half = MLA_ROPE // 2
    lane = _iota(x.shape, 1)
    sw = jnp.where(lane < half, pltpu.roll(x, LANES - half, axis=1), pltpu.roll(x, half, axis=1))
    return jnp.where(lane < MLA_ROPE, sw, 0.0)


def _mla_prep_fwd(qp, kv, c, gq, gk, cos_t, sin_t, *, n_heads, name, bs=512):
    s_len = qp.shape[0]
    bs = _div_block(s_len, bs, 8)
    krope_col = (MLA_Q_RANK + MLA_KV_RANK) // LANES

    def body(qn_ref, qr_ref, kn_ref, kr_ref, gq_ref, gk_ref, cos_ref, sin_ref, qc_ref, kc_ref):
        cos_v, sin_v = cos_ref[...], sin_ref[...]

        def rope(x, g):
            xv = x.astype(F32)
            inv = lax.rsqrt(jnp.sum(xv * xv, axis=-1, keepdims=True) / MLA_ROPE + NORM_EPS)
            y = xv * inv * g
            return y * cos_v + _rope_swap(y) * sin_v

        qc_ref[:, :LANES] = _head_norm(qn_ref[...], gq_ref[0]).astype(BF16)
        qc_ref[:, LANES:] = rope(qr_ref[...], gq_ref[1]).astype(BF16)
        kc_ref[:, :LANES] = _head_norm(kn_ref[...], gk_ref[0]).astype(BF16)
        kc_ref[:, LANES:] = rope(kr_ref[...], gk_ref[1]).astype(BF16)

    blk = lambda f: pl.BlockSpec((bs, LANES), f)
    gspec = pl.BlockSpec((2, 1, LANES), lambda i, h: (0, 0, 0))
    tspec = pl.BlockSpec((bs, LANES), lambda i, h: (i, 0))
    ospec = pl.BlockSpec((bs, 2 * LANES), lambda i, h: (i, h))
    oshape = jax.ShapeDtypeStruct((s_len, n_heads * 2 * LANES), BF16)
    return pl.pallas_call(
        body, name=name, grid=(s_len // bs, n_heads),
        in_specs=[blk(lambda i, h: (i, h)), blk(lambda i, h: (i, n_heads + h)), blk(lambda i, h: (i, 2 * h)),
                  blk(lambda i, h: (i, krope_col)), gspec, gspec, tspec, tspec],
        out_specs=[ospec, ospec], out_shape=[oshape, oshape],
        compiler_params=_params("parallel", "parallel"))(qp, qp, kv, c, gq, gk, cos_t, sin_t)


def _mla_prep_bwd(qp, kv, c, gq, gk, cos_t, sin_t, dqc, dkc, dv, *, n_heads, name, bs=512):
    s_len = qp.shape[0]
    bs = _div_block(s_len, bs, 8)
    krope_col = (MLA_Q_RANK + MLA_KV_RANK) // LANES

    def body(qn_ref, qr_ref, kn_ref, kr_ref, gq_ref, gk_ref, cos_ref, sin_ref, dqc_ref, dkc_ref, dv_ref,
             dqn_ref, dqr_ref, dkv_ref, dkr_ref, dgq_ref, dgk_ref):
        i, h = pl.program_id(0), pl.program_id(1)
        cos_v, sin_v = cos_ref[...], sin_ref[...]

        @pl.when((i == 0) & (h == 0))
        def _():
            dgq_ref[...] = jnp.zeros_like(dgq_ref)
            dgk_ref[...] = jnp.zeros_like(dgk_ref)

        @pl.when(h == 0)
        def _():
            dkr_ref[...] = jnp.zeros_like(dkr_ref)

        def unrope(dy):
            dy = dy.astype(F32)
            return dy * cos_v + _rope_swap(dy * sin_v)

        dqn, dg = _rms_bwd_math(qn_ref[...].astype(F32), gq_ref[0], dqc_ref[:, :LANES].astype(F32), MLA_NOPE)
        dgq_ref[0] += jnp.sum(dg, axis=0, keepdims=True)
        dqn_ref[...] = dqn.astype(BF16)
        dqr, dg = _rms_bwd_math(qr_ref[...].astype(F32), gq_ref[1], unrope(dqc_ref[:, LANES:]), MLA_ROPE)
        dgq_ref[1] += jnp.sum(dg, axis=0, keepdims=True)
        dqr_ref[...] = dqr.astype(BF16)
        dkn, dg = _rms_bwd_math(kn_ref[...].astype(F32), gk_ref[0], dkc_ref[:, :LANES].astype(F32), MLA_NOPE)
        dgk_ref[0] += jnp.sum(dg, axis=0, keepdims=True)
        dkv_ref[:, :LANES] = dkn.astype(BF16)
        dkv_ref[:, LANES:] = dv_ref[...]
        dkr, dg = _rms_bwd_math(kr_ref[...].astype(F32), gk_ref[1], unrope(dkc_ref[:, LANES:]), MLA_ROPE)
        dgk_ref[1] += jnp.sum(dg, axis=0, keepdims=True)
        dkr_ref[...] += dkr

    blk = lambda f: pl.BlockSpec((bs, LANES), f)
    gspec = pl.BlockSpec((2, 1, LANES), lambda i, h: (0, 0, 0))
    tspec = pl.BlockSpec((bs, LANES), lambda i, h: (i, 0))
    cat = pl.BlockSpec((bs, 2 * LANES), lambda i, h: (i, h))
    head = blk(lambda i, h: (i, h))
    hshape = jax.ShapeDtypeStruct((s_len, n_heads * LANES), BF16)
    gshape = jax.ShapeDtypeStruct((2, 1, LANES), F32)
    return pl.pallas_call(
        body, name=name, grid=(s_len // bs, n_heads),
        in_specs=[head, blk(lambda i, h: (i, n_heads + h)), blk(lambda i, h: (i, 2 * h)),
                  blk(lambda i, h: (i, krope_col)), gspec, gspec, tspec, tspec, cat, cat, head],
        out_specs=[head, head, cat, tspec, gspec, gspec],
        out_shape=[hshape, hshape, jax.ShapeDtypeStruct((s_len, n_heads * 2 * LANES), BF16),
                   jax.ShapeDtypeStruct((s_len, LANES), F32), gshape, gshape],
        compiler_params=_params("arbitrary", "arbitrary"))(qp, qp, kv, c, gq, gk, cos_t, sin_t, dqc, dkc, dv)


def _mla_latent_fwd(c, ga, *, name, br=256):
    s_len = c.shape[0]
    br = _div_block(s_len, br, 8)

    def body(c_ref, g_ref, o_ref):
        for part in range(2):
            sl = slice(part * MLA_Q_RANK, (part + 1) * MLA_Q_RANK)
            o_ref[:, sl] = _head_norm(c_ref[:, sl], g_ref[:, sl]).astype(BF16)

    w = MLA_Q_RANK + MLA_KV_RANK
    return pl.pallas_call(
        body, name=name, grid=(s_len // br,),
        in_specs=[pl.BlockSpec((br, w), lambda i: (i, 0)), pl.BlockSpec((1, w), lambda i: (0, 0))],
        out_specs=pl.BlockSpec((br, w), lambda i: (i, 0)), out_shape=jax.ShapeDtypeStruct((s_len, w), BF16),
        compiler_params=_params("parallel"))(c, ga)


def _mla_latent_bwd(c, ga, dcn_q, dcn_kv, dk_rope, *, name, br=256):
    s_len, cw = c.shape
    br = _div_block(s_len, br, 8)
    w = MLA_Q_RANK + MLA_KV_RANK

    def body(c_ref, g_ref, dq_ref, dkv_ref, dkr_ref, dc_ref, dg_ref):
        @pl.when(pl.program_id(0) == 0)
        def _():
            dg_ref[...] = jnp.zeros_like(dg_ref)

        for part, d_ref in enumerate((dq_ref, dkv_ref)):
            sl = slice(part * MLA_Q_RANK, (part + 1) * MLA_Q_RANK)
            dx, dg = _rms_bwd_math(c_ref[:, sl].astype(F32), g_ref[:, sl], d_ref[...].astype(F32), MLA_Q_RANK)
            dc_ref[:, sl] = dx.astype(BF16)
            dg_ref[:, sl] += jnp.sum(dg, axis=0, keepdims=True)
        dc_ref[:, w:] = dkr_ref[...].astype(BF16)

    return pl.pallas_call(
        body, name=name, grid=(s_len // br,),
        in_specs=[pl.BlockSpec((br, w), lambda i: (i, 0)), pl.BlockSpec((1, w), lambda i: (0, 0)),
                  pl.BlockSpec((br, MLA_Q_RANK), lambda i: (i, 0)), pl.BlockSpec((br, MLA_KV_RANK), lambda i: (i, 0)),
                  pl.BlockSpec((br, LANES), lambda i: (i, 0))],
        out_specs=[pl.BlockSpec((br, cw), lambda i: (i, 0)), pl.BlockSpec((1, w), lambda i: (0, 0))],
        out_shape=[jax.ShapeDtypeStruct((s_len, cw), BF16), jax.ShapeDtypeStruct((1, w), F32)],
        compiler_params=_params("arbitrary"))(c, ga, dcn_q, dcn_kv, dk_rope)


_GELU_C = 0.7978845608028654


def _gelu(x):
    return 0.5 * x * (1.0 + jnp.tanh(_GELU_C * (x + 0.044715 * x * x * x)))


def _gelu_grad(x):
    t = jnp.tanh(_GELU_C * (x + 0.044715 * x * x * x))
    return 0.5 * (1.0 + t) + 0.5 * x * (1.0 - t * t) * _GELU_C * (1.0 + 3 * 0.044715 * x * x)


def _sgu_act_fwd(uv, vg, *, name, br=256):
    s_len, w2 = uv.shape
    w = w2 // 2
    br = _div_block(s_len, br, 8)

    def body(uv_ref, g_ref, u_ref, v_ref):
        u_ref[...] = _gelu(uv_ref[:, :w])
        v_ref[...] = _head_norm(_gelu(uv_ref[:, w:]), g_ref[...]).astype(BF16)

    row = lambda c: pl.BlockSpec((br, c), lambda i: (i, 0))
    return pl.pallas_call(
        body, name=name, grid=(s_len // br,), in_specs=[row(w2), pl.BlockSpec((1, w), lambda i: (0, 0))],
        out_specs=[row(w), row(w)], out_shape=[jax.ShapeDtypeStruct((s_len, w), F32), jax.ShapeDtypeStruct((s_len, w), BF16)],
        compiler_params=_params("parallel"))(uv, vg)


def _sgu_act_bwd(uv, vg, du, dvn, *, name, br=256):
    s_len, w2 = uv.shape
    w = w2 // 2
    br = _div_block(s_len, br, 8)

    def body(uv_ref, g_ref, du_ref, dvn_ref, duv_ref, dg_ref):
        @pl.when(pl.program_id(0) == 0)
        def _():
            dg_ref[...] = jnp.zeros_like(dg_ref)

        up, vp = uv_ref[:, :w], uv_ref[:, w:]
        duv_ref[:, :w] = (du_ref[...] * _gelu_grad(up)).astype(BF16)
        dva, dg = _rms_bwd_math(_gelu(vp), g_ref[...], dvn_ref[...], w)
        dg_ref[...] += jnp.sum(dg, axis=0, keepdims=True)
        duv_ref[:, w:] = (dva * _gelu_grad(vp)).astype(BF16)

    row = lambda c: pl.BlockSpec((br, c), lambda i: (i, 0))
    vec = pl.BlockSpec((1, w), lambda i: (0, 0))
    return pl.pallas_call(
        body, name=name, grid=(s_len // br,), in_specs=[row(w2), vec, row(w), row(w)], out_specs=[row(w2), vec],
        out_shape=[jax.ShapeDtypeStruct((s_len, w2), BF16), jax.ShapeDtypeStruct((1, w), F32)],
        compiler_params=_params("arbitrary"))(uv, vg, du, dvn)


def _tril_weights(ws_ref):
    t, s = _iota((SGU_CHUNK, SGU_CHUNK), 0), _iota((SGU_CHUNK, SGU_CHUNK), 1)
    keep = s <= t
    return jnp.where(keep, ws_ref[...], 0.0), keep


def _sgu_mix_fwd(u, vn, w_s, b_s, *, name):
    s_len, w = u.shape
    nc = s_len // SGU_CHUNK

    def body(u_ref, v_ref, ws_ref, b_ref, o_ref):
        wm = _tril_weights(ws_ref)[0].astype(BF16)
        for n in range(nc):
            rows = slice(n * SGU_CHUNK, (n + 1) * SGU_CHUNK)
            mixed = _dot(wm, v_ref[rows, :], 1, 0) + b_ref[...]
            o_ref[rows, :] = (u_ref[rows, :] * mixed).astype(BF16)

    col = pl.BlockSpec((s_len, LANES), lambda g: (0, g))
    return pl.pallas_call(
        body, name=name, grid=(w // LANES,),
        in_specs=[col, col, pl.BlockSpec((None, SGU_CHUNK, SGU_CHUNK), lambda g: (g, 0, 0)),
                  pl.BlockSpec((None, SGU_CHUNK, 1), lambda g: (g, 0, 0))],
        out_specs=col, out_shape=jax.ShapeDtypeStruct((s_len, w), BF16),
        compiler_params=_params("parallel"))(u, vn, w_s, b_s)


def _sgu_mix_bwd(u, vn, w_s, b_s, dgated, *, name):
    s_len, w = u.shape
    nc = s_len // SGU_CHUNK

    def body(u_ref, v_ref, ws_ref, b_ref, dg_ref, du_ref, dv_ref, dws_ref, dbs_ref):
        wf, keep = _tril_weights(ws_ref)
        wm = wf.astype(BF16)
        wmt = wf.T.astype(BF16)
        dws = jnp.zeros((SGU_CHUNK, SGU_CHUNK), F32)
        dbs = jnp.zeros((SGU_CHUNK, 1), F32)
        for n in range(nc):
            rows = slice(n * SGU_CHUNK, (n + 1) * SGU_CHUNK)
            vb = v_ref[rows, :]
            dgv = dg_ref[rows, :].astype(F32)
            mixed = _dot(wm, vb, 1, 0) + b_ref[...]
            du_ref[rows, :] = dgv * mixed
            dm = dgv * u_ref[rows, :]
            dmb = dm.astype(BF16)
            dws = dws + _dot(dmb, vb, 1, 1)
            dbs = dbs + jnp.sum(dm, axis=1, keepdims=True)
            dv_ref[rows, :] = _dot(wmt, dmb, 1, 0)
        dws_ref[...] = jnp.where(keep, dws, 0.0)
        dbs_ref[...] = dbs

    col = pl.BlockSpec((s_len, LANES), lambda g: (0, g))
    wspec = pl.BlockSpec((None, SGU_CHUNK, SGU_CHUNK), lambda g: (g, 0, 0))
    bspec = pl.BlockSpec((None, SGU_CHUNK, 1), lambda g: (g, 0, 0))
    return pl.pallas_call(
        body, name=name, grid=(w // LANES,), in_specs=[col, col, wspec, bspec, col],
        out_specs=[col, col, wspec, bspec],
        out_shape=[jax.ShapeDtypeStruct((s_len, w), F32), jax.ShapeDtypeStruct((s_len, w), F32),
                   jax.ShapeDtypeStruct(w_s.shape, F32), jax.ShapeDtypeStruct(b_s.shape, F32)],
        compiler_params=_params("parallel"))(u, vn, w_s, b_s, dgated)


def _shift_down(x, k):
    if k == 0:
        return x
    return jnp.where(_iota(x.shape, 0) >= k, pltpu.roll(x, k, axis=0), 0.0)


def _shift_up(x, k):
    if k == 0:
        return x
    n = x.shape[0]
    return jnp.where(_iota(x.shape, 0) < n - k, pltpu.roll(x, n - k, axis=0), 0.0)


def _conv(u, w_ref, b_ref):
    return b_ref[...] + w_ref[0:1, :] * _shift_down(u, 2) + w_ref[1:2, :] * _shift_down(u, 1) + w_ref[2:3, :] * u


def _sigmoid(x):
    return 0.5 * jnp.tanh(0.5 * x) + 0.5


def _glu_fwd(up, cw, cb, *, name, bc=256):
    s_len, f2 = up.shape
    f = f2 // 2
    bc = _div_block(f, bc)
    nf = f // bc

    def body(ug_ref, uv_ref, wg_ref, wv_ref, bg_ref, bv_ref, o_ref):
        yg = _conv(ug_ref[...], wg_ref, bg_ref)
        yv = _conv(uv_ref[...], wv_ref, bv_ref)
        o_ref[...] = (yg * _sigmoid(yg) * yv).astype(BF16)

    big = lambda off: pl.BlockSpec((s_len, bc), lambda j: (0, j + off))
    wsp = lambda off: pl.BlockSpec((3, bc), lambda j: (0, j + off))
    bsp = lambda off: pl.BlockSpec((1, bc), lambda j: (0, j + off))
    return pl.pallas_call(
        body, name=name, grid=(nf,), in_specs=[big(0), big(nf), wsp(0), wsp(nf), bsp(0), bsp(nf)],
        out_specs=pl.BlockSpec((s_len, bc), lambda j: (0, j)), out_shape=jax.ShapeDtypeStruct((s_len, f), BF16),
        compiler_params=_params("parallel"))(up, up, cw, cw, cb, cb)


def _glu_bwd(up, cw, cb, dact, *, name, bc=256):
    s_len, f2 = up.shape
    f = f2 // 2
    bc = _div_block(f, bc)
    nf = f // bc

    def body(ug_ref, uv_ref, wg_ref, wv_ref, bg_ref, bv_ref, da_ref, du_ref, dw_ref, db_ref):
        ug, uv = ug_ref[...], uv_ref[...]
        yg = _conv(ug, wg_ref, bg_ref)
        yv = _conv(uv, wv_ref, bv_ref)
        da = da_ref[...].astype(F32)
        sg = _sigmoid(yg)
        planes = ((da * yv * (sg * (1.0 + yg * (1.0 - sg))), ug, wg_ref), (da * (yg * sg), uv, wv_ref))
        for plane, (dy, u, w_ref) in enumerate(planes):
            dy1, dy2 = _shift_up(dy, 1), _shift_up(dy, 2)
            db_ref[plane] = jnp.sum(dy, axis=0, keepdims=True)
            dw_ref[plane, 0:1, :] = jnp.sum(dy2 * u, axis=0, keepdims=True)
            dw_ref[plane, 1:2, :] = jnp.sum(dy1 * u, axis=0, keepdims=True)
            dw_ref[plane, 2:3, :] = jnp.sum(dy * u, axis=0, keepdims=True)
            du_ref[plane] = (w_ref[2:3, :] * dy + w_ref[1:2, :] * dy1 + w_ref[0:1, :] * dy2).astype(BF16)

    big = lambda off: pl.BlockSpec((s_len, bc), lambda j: (0, j + off))
    wsp = lambda off: pl.BlockSpec((3, bc), lambda j: (0, j + off))
    bsp = lambda off: pl.BlockSpec((1, bc), lambda j: (0, j + off))
    planes = lambda r: pl.BlockSpec((2, r, bc), lambda j: (0, 0, j))
    return pl.pallas_call(
        body, name=name, grid=(nf,),
        in_specs=[big(0), big(nf), wsp(0), wsp(nf), bsp(0), bsp(nf), pl.BlockSpec((s_len, bc), lambda j: (0, j))],
        out_specs=[planes(s_len), planes(3), planes(1)],
        out_shape=[jax.ShapeDtypeStruct((2, s_len, f), BF16), jax.ShapeDtypeStruct((2, 3, f), F32),
                   jax.ShapeDtypeStruct((2, 1, f), F32)],
        compiler_params=_params("parallel"))(up, up, cw, cw, cb, cb, dact)


def _as2d(a):
    return a.reshape(-1, a.shape[-1]) if a.ndim >= 2 else a.reshape(1, -1)


def _adamw(w, g, m, v, *, name, after=None, target_bytes=1 << 20):
    shape = w.shape
    w2, m2, v2 = _as2d(w), _as2d(m), _as2d(v)
    g2 = g.reshape(w2.shape)
    r, c = w2.shape
    br = r if r * c * 4 <= target_bytes else _div_block(r, max(8, target_bytes // (4 * c) // 8 * 8), 8)
    c1 = 1.0 - ADAM_B1 ** ADAM_STEP
    c2 = 1.0 - ADAM_B2 ** ADAM_STEP

    def body(w_ref, g_ref, m_ref, v_ref, *rest):
        d_ref, nm_ref, nv_ref = rest[-3:]
        gv = g_ref[...]
        nm = ADAM_B1 * m_ref[...] + (1.0 - ADAM_B1) * gv
        nv = ADAM_B2 * v_ref[...] + (1.0 - ADAM_B2) * (gv * gv)
        nm_ref[...] = nm
        nv_ref[...] = nv
        d_ref[...] = -ADAM_LR * ((nm / c1) / (jnp.sqrt(nv / c2) + ADAM_EPS) + ADAM_WD * w_ref[...])

    spec = pl.BlockSpec((br, c), lambda i: (i, 0))
    sds = jax.ShapeDtypeStruct((r, c), F32)
    d, nm, nv = pl.pallas_call(
        body, name=name, grid=(r // br,), in_specs=[spec] * 4 + _after_spec(after), out_specs=[spec] * 3,
        out_shape=[sds] * 3, compiler_params=_params("parallel"))(w2, g2, m2, v2, *_after_arg(after))
    return d.reshape(shape), nm.reshape(shape), nv.reshape(shape)


def _add_halves(g, recv, place, *, name, target_bytes=SHARD_BLOCK_BYTES):
    _, _, r, c = g.shape
    br = _div_block(r, max(16, target_bytes // (2 * c) // 16 * 16), 16)

    def body(x_ref, y_ref, c_ref, g_ref, r_ref, o_ref):
        o_ref[...] = (g_ref[...].astype(F32) + r_ref[...].astype(F32)).astype(BF16)

    return pl.pallas_call(
        body, name=name,
        grid_spec=pltpu.PrefetchScalarGridSpec(
            num_scalar_prefetch=3, grid=(N_CHIPS, r // br),
            in_specs=[pl.BlockSpec((None, None, br, c), lambda s, i, xr, yr, cr: (s, cr[0], i, 0)),
                      pl.BlockSpec((None, br, c), lambda s, i, xr, yr, cr: (s, i, 0))],
            out_specs=pl.BlockSpec((None, br, c), lambda s, i, xr, yr, cr: (s, i, 0))),
        out_shape=jax.ShapeDtypeStruct((N_CHIPS, r, c), BF16),
        compiler_params=_params("parallel", "parallel"))(*place, g, recv)


def _sum_chips(p, landed, place, *, name, target_bytes=SHARD_BLOCK_BYTES):
    _, r, c = p.shape
    br = _div_block(r, max(16, target_bytes // (4 * c) // 16 * 16), 16)

    def body(x_ref, y_ref, c_ref, p_ref, l1_ref, l2_ref, l3_ref, o_ref):
        o_ref[...] = ((p_ref[...].astype(F32) + l1_ref[...].astype(F32)) + l2_ref[...].astype(F32)) + l3_ref[...].astype(F32)

    slot = lambda k: pl.BlockSpec((None, br, c), lambda i, xr, yr, cr: ((2 * xr[0] + yr[0] + k) % N_CHIPS, i, 0))
    return pl.pallas_call(
        body, name=name,
        grid_spec=pltpu.PrefetchScalarGridSpec(
            num_scalar_prefetch=3, grid=(r // br,), in_specs=[slot(0), slot(1), slot(2), slot(3)],
            out_specs=pl.BlockSpec((None, br, c), lambda i, xr, yr, cr: (cr[0], i, 0))),
        out_shape=jax.ShapeDtypeStruct((2, r, c), F32),
        compiler_params=_params("parallel"))(*place, p, landed, landed, landed)


def _place_shard(w, place, *, dtype, name, layer=None, after=None, target_bytes=SHARD_BLOCK_BYTES):
    r, c = w.shape[-2:]
    hr = r // 2
    mult = 16 if dtype == BF16 else 8
    br = _div_block(hr, max(mult, target_bytes // (4 * c) // mult * mult), mult)
    nb = hr // br

    def body(x_ref, y_ref, c_ref, w_ref, *rest):
        rest[-1][...] = w_ref[...].astype(dtype)

    if layer is None:
        w_spec = pl.BlockSpec((br, c), lambda h, i, xr, yr, cr: (h * nb + i, 0))
    else:
        w_spec = pl.BlockSpec((None, br, c), lambda h, i, xr, yr, cr: (layer, h * nb + i, 0))
    return pl.pallas_call(
        body, name=name,
        grid_spec=pltpu.PrefetchScalarGridSpec(
            num_scalar_prefetch=3, grid=(2, nb), in_specs=[w_spec] + _after_spec(after),
            out_specs=pl.BlockSpec((None, None, br, c), lambda h, i, xr, yr, cr: (2 * xr[0] + yr[0], h, i, 0))),
        out_shape=jax.ShapeDtypeStruct((N_CHIPS, 2, hr, c), dtype),
        compiler_params=_params("parallel", "parallel"))(*place, w, *_after_arg(after))


def _sum_devices(x, *, name):
    n, r, c = x.shape
    br = _div_block(r, 512, 8)

    def body(x_ref, o_ref):
        acc = x_ref[0]
        for s in range(1, n):
            acc = acc + x_ref[s]
        o_ref[...] = acc

    return pl.pallas_call(
        body, name=name, grid=(r // br,), in_specs=[pl.BlockSpec((n, br, c), lambda i: (0, i, 0))],
        out_specs=pl.BlockSpec((br, c), lambda i: (i, 0)), out_shape=jax.ShapeDtypeStruct((r, c), F32),
        compiler_params=_params("parallel"))(x)


_ANY = pl.BlockSpec(memory_space=pl.ANY)


def _place():
    x, y, c = lax.axis_index("x"), lax.axis_index("y"), lax.axis_index("c")
    other_chips = [(1 - x, y), (x, 1 - y), (1 - x, 1 - y)]
    return x, y, c, other_chips


_HBM = pl.BlockSpec(memory_space=pltpu.HBM)
_SEM = pl.BlockSpec(memory_space=pltpu.SEMAPHORE)
_EFFECT = pltpu.SideEffectType.DATAFLOW_SIDE_EFFECTING


def _in_hbm(a):
    return pltpu.with_memory_space_constraint(a, pltpu.HBM)


def _token_spec():
    return pl.BlockSpec(memory_space=pltpu.VMEM), jax.ShapeDtypeStruct((8, LANES), F32)


def _gather_ici_start(bufs, after, *, name):
    n = len(bufs)

    def body(*refs):
        b_refs = refs[:n]
        send_sems, recv_sems = refs[n + 1], refs[n + 2]
        token = refs[-1]
        x, y, c, chips = _place()
        me = 2 * x + y
        for i in range(n):
            for j, (px, py) in enumerate(chips):
                pltpu.make_async_remote_copy(src_ref=b_refs[i].at[me, c], dst_ref=b_refs[i].at[me, c],
                                             send_sem=send_sems.at[3 * i + j], recv_sem=recv_sems.at[3 * i + j],
                                             device_id=(px, py, c), device_id_type=MESH).start()
        token[...] = jnp.zeros_like(token)

    tspec, tshape = _token_spec()
    outs = pl.pallas_call(
        body, name=name, in_specs=[_HBM] * n + [_ANY], out_specs=(_SEM, _SEM, *[_HBM] * n, tspec),
        out_shape=(pltpu.SemaphoreType.DMA((3 * n,)), pltpu.SemaphoreType.DMA((3 * n,)),
                   *[pltpu.HBM(a.shape, a.dtype) for a in bufs], tshape),
        input_output_aliases={i: 2 + i for i in range(n)},
        compiler_params=pltpu.CompilerParams(has_side_effects=_EFFECT),
    )(*[_in_hbm(a) for a in bufs], after)
    return outs[0], outs[1], list(outs[2:2 + n]), outs[-1]


def _gather_ici_wait(send_sems, recv_sems, bufs, after, *, name):
    n = len(bufs)

    def body(*refs):
        b_refs = refs[:n]
        send_sems, recv_sems = refs[n], refs[n + 1]
        x, y, c, chips = _place()
        me = 2 * x + y
        for i in range(n):
            for j, (px, py) in enumerate(chips):
                cp = pltpu.make_async_remote_copy(src_ref=b_refs[i].at[me, c], dst_ref=b_refs[i].at[2 * px + py, c],
                                                  send_sem=send_sems.at[3 * i + j], recv_sem=recv_sems.at[3 * i + j],
                                                  device_id=(px, py, c), device_id_type=MESH)
                cp.wait_send()
                cp.wait_recv()

    outs = pl.pallas_call(
        body, name=name, in_specs=[_HBM] * n + [_SEM, _SEM, _ANY], out_specs=[_HBM] * n,
        out_shape=[pltpu.HBM(a.shape, a.dtype) for a in bufs], input_output_aliases={i: i for i in range(n)},
        compiler_params=pltpu.CompilerParams(has_side_effects=_EFFECT),
    )(*bufs, send_sems, recv_sems, after)
    return list(outs)


def _gather_d2d(bufs, *, name):
    n = len(bufs)

    def body(*refs):
        b_refs = refs[n:2 * n]
        send_sems, recv_sems = refs[2 * n:]
        x, y, c, chips = _place()
        sends = []
        for i in range(n):
            for j, (px, py) in enumerate(chips):
                mine = b_refs[i].at[2 * px + py, c]
                cp = pltpu.make_async_remote_copy(src_ref=mine, dst_ref=mine, send_sem=send_sems.at[3 * i + j],
                                                  recv_sem=recv_sems.at[3 * i + j], device_id=(x, y, 1 - c),
                                                  device_id_type=MESH)
                cp.start()
                sends.append((cp, i, j, px, py))
        for cp, i, j, px, py in sends:
            theirs = b_refs[i].at[2 * px + py, 1 - c]
            pltpu.make_async_remote_copy(src_ref=theirs, dst_ref=theirs, send_sem=send_sems.at[3 * i + j],
                                         recv_sem=recv_sems.at[3 * i + j], device_id=(x, y, 1 - c),
                                         device_id_type=MESH).wait_recv()
            cp.wait_send()

    return pl.pallas_call(
        body, name=name, in_specs=[_ANY] * n, out_specs=[_ANY] * n, input_output_aliases={i: i for i in range(n)},
        out_shape=[jax.ShapeDtypeStruct(a.shape, a.dtype) for a in bufs],
        scratch_shapes=[pltpu.SemaphoreType.DMA((3 * n,)), pltpu.SemaphoreType.DMA((3 * n,))],
    )(*bufs)


def _gather_d2d_start(bufs, after, *, name):
    n = len(bufs)

    def body(*refs):
        b_refs = refs[:n]
        send_sems, recv_sems = refs[n + 1], refs[n + 2]
        token = refs[-1]
        x, y, c, chips = _place()
        for i in range(n):
            for j, (px, py) in enumerate(chips):
                mine = b_refs[i].at[2 * px + py, c]
                pltpu.make_async_remote_copy(src_ref=mine, dst_ref=mine, send_sem=send_sems.at[3 * i + j],
                                             recv_sem=recv_sems.at[3 * i + j], device_id=(x, y, 1 - c),
                                             device_id_type=MESH).start()
        token[...] = jnp.zeros_like(token)

    tspec, tshape = _token_spec()
    outs = pl.pallas_call(
        body, name=name, in_specs=[_HBM] * n + [_ANY], out_specs=(_SEM, _SEM, *[_HBM] * n, tspec),
        out_shape=(pltpu.SemaphoreType.DMA((3 * n,)), pltpu.SemaphoreType.DMA((3 * n,)),
                   *[pltpu.HBM(a.shape, a.dtype) for a in bufs], tshape),
        input_output_aliases={i: 2 + i for i in range(n)},
        compiler_params=pltpu.CompilerParams(has_side_effects=_EFFECT),
    )(*[_in_hbm(a) for a in bufs], after)
    return outs[0], outs[1], list(outs[2:2 + n]), outs[-1]


def _gather_d2d_wait(send_sems, recv_sems, bufs, after, *, name):
    n = len(bufs)

    def body(*refs):
        b_refs = refs[:n]
        send_sems, recv_sems = refs[n], refs[n + 1]
        x, y, c, chips = _place()
        for i in range(n):
            for j, (px, py) in enumerate(chips):
                cp = pltpu.make_async_remote_copy(src_ref=b_refs[i].at[2 * px + py, c],
                                                  dst_ref=b_refs[i].at[2 * px + py, 1 - c],
                                                  send_sem=send_sems.at[3 * i + j], recv_sem=recv_sems.at[3 * i + j],
                                                  device_id=(x, y, 1 - c), device_id_type=MESH)
                cp.wait_send()
                cp.wait_recv()

    outs = pl.pallas_call(
        body, name=name, in_specs=[_HBM] * n + [_SEM, _SEM, _ANY], out_specs=[_HBM] * n,
        out_shape=[pltpu.HBM(a.shape, a.dtype) for a in bufs], input_output_aliases={i: i for i in range(n)},
        compiler_params=pltpu.CompilerParams(has_side_effects=_EFFECT),
    )(*bufs, send_sems, recv_sems, after)
    return list(outs)


def _sibling_halves_start(gs, after, *, name):
    n = len(gs)

    def body(*refs):
        g_refs, r_refs = refs[:n], refs[n:2 * n]
        send_sems, recv_sems = refs[2 * n + 1], refs[2 * n + 2]
        token = refs[-1]
        x, y, c, _ = _place()
        for i in range(n):
            for s in range(N_CHIPS):
                k = i * N_CHIPS + s
                pltpu.make_async_remote_copy(src_ref=g_refs[i].at[s, 1 - c], dst_ref=r_refs[i].at[s],
                                             send_sem=send_sems.at[k], recv_sem=recv_sems.at[k],
                                             device_id=(x, y, 1 - c), device_id_type=MESH).start()
        token[...] = jnp.zeros_like(token)

    tspec, tshape = _token_spec()
    rshapes = [(N_CHIPS,) + g.shape[2:] for g in gs]
    recvs = [_in_hbm(lax.empty(sh, g.dtype)) for sh, g in zip(rshapes, gs)]
    outs = pl.pallas_call(
        body, name=name, in_specs=[_HBM] * (2 * n) + [_ANY], out_specs=(_SEM, _SEM, *[_HBM] * (2 * n), tspec),
        out_shape=(pltpu.SemaphoreType.DMA((N_CHIPS * n,)), pltpu.SemaphoreType.DMA((N_CHIPS * n,)),
                   *[pltpu.HBM(g.shape, g.dtype) for g in gs], *[pltpu.HBM(sh, g.dtype) for sh, g in zip(rshapes, gs)],
                   tshape),
        input_output_aliases={i: 2 + i for i in range(2 * n)},
        compiler_params=pltpu.CompilerParams(has_side_effects=_EFFECT),
    )(*[_in_hbm(g) for g in gs], *recvs, after)
    return outs[0], outs[1], list(outs[2:2 + n]), list(outs[2 + n:2 + 2 * n]), outs[-1]


def _sibling_halves_wait(send_sems, recv_sems, gs, recvs, after, *, name):
    n = len(gs)

    def body(*refs):
        g_refs, r_refs = refs[:n], refs[n:2 * n]
        send_sems, recv_sems = refs[2 * n], refs[2 * n + 1]
        x, y, c, _ = _place()
        for i in range(n):
            for s in range(N_CHIPS):
                k = i * N_CHIPS + s
                cp = pltpu.make_async_remote_copy(src_ref=g_refs[i].at[s, 1 - c], dst_ref=r_refs[i].at[s],
                                                  send_sem=send_sems.at[k], recv_sem=recv_sems.at[k],
                                                  device_id=(x, y, 1 - c), device_id_type=MESH)
                cp.wait_send()
                cp.wait_recv()

    outs = pl.pallas_call(
        body, name=name, in_specs=[_HBM] * (2 * n) + [_SEM, _SEM, _ANY], out_specs=[_HBM] * (2 * n),
        out_shape=[pltpu.HBM(a.shape, a.dtype) for a in list(gs) + list(recvs)],
        input_output_aliases={i: i for i in range(2 * n)},
        compiler_params=pltpu.CompilerParams(has_side_effects=_EFFECT),
    )(*gs, *recvs, send_sems, recv_sems, after)
    return list(outs[:n]), list(outs[n:])


def _chip_scatter_start(ps, after, *, name):
    n = len(ps)

    def body(*refs):
        p_refs, l_refs = refs[:n], refs[n:2 * n]
        send_sems, recv_sems = refs[2 * n + 1], refs[2 * n + 2]
        token = refs[-1]
        x, y, c, chips = _place()
        me = 2 * x + y
        for i in range(n):
            for j, (px, py) in enumerate(chips):
                pltpu.make_async_remote_copy(src_ref=p_refs[i].at[2 * px + py], dst_ref=l_refs[i].at[me],
                                             send_sem=send_sems.at[3 * i + j], recv_sem=recv_sems.at[3 * i + j],
                                             device_id=(px, py, c), device_id_type=MESH).start()
        token[...] = jnp.zeros_like(token)

    tspec, tshape = _token_spec()
    lands = [_in_hbm(lax.empty(p.shape, p.dtype)) for p in ps]
    outs = pl.pallas_call(
        body, name=name, in_specs=[_HBM] * (2 * n) + [_ANY], out_specs=(_SEM, _SEM, *[_HBM] * (2 * n), tspec),
        out_shape=(pltpu.SemaphoreType.DMA((3 * n,)), pltpu.SemaphoreType.DMA((3 * n,)),
                   *[pltpu.HBM(p.shape, p.dtype) for p in ps], *[pltpu.HBM(p.shape, p.dtype) for p in ps], tshape),
        input_output_aliases={i: 2 + i for i in range(2 * n)},
        compiler_params=pltpu.CompilerParams(has_side_effects=_EFFECT),
    )(*[_in_hbm(p) for p in ps], *lands, after)
    return outs[0], outs[1], list(outs[2:2 + n]), list(outs[2 + n:2 + 2 * n]), outs[-1]


def _chip_scatter_wait(send_sems, recv_sems, ps, lands, after, *, name):
    n = len(ps)

    def body(*refs):
        p_refs, l_refs = refs[:n], refs[n:2 * n]
        send_sems, recv_sems = refs[2 * n], refs[2 * n + 1]
        x, y, c, chips = _place()
        for i in range(n):
            for j, (px, py) in enumerate(chips):
                cp = pltpu.make_async_remote_copy(src_ref=p_refs[i].at[2 * px + py], dst_ref=l_refs[i].at[2 * px + py],
                                                  send_sem=send_sems.at[3 * i + j], recv_sem=recv_sems.at[3 * i + j],
                                                  device_id=(px, py, c), device_id_type=MESH)
                cp.wait_send()
                cp.wait_recv()

    outs = pl.pallas_call(
        body, name=name, in_specs=[_HBM] * (2 * n) + [_SEM, _SEM, _ANY], out_specs=[_HBM] * (2 * n),
        out_shape=[pltpu.HBM(p.shape, p.dtype) for p in ps] * 2, input_output_aliases={i: i for i in range(2 * n)},
        compiler_params=pltpu.CompilerParams(has_side_effects=_EFFECT),
    )(*ps, *lands, send_sems, recv_sems, after)
    return list(outs[:n]), list(outs[n:])


def _sibling_share_start(bufs, after, *, name):
    n = len(bufs)

    def body(*refs):
        b_refs = refs[:n]
        send_sems, recv_sems = refs[n + 1], refs[n + 2]
        token = refs[-1]
        x, y, c, _ = _place()
        for i in range(n):
            pltpu.make_async_remote_copy(src_ref=b_refs[i].at[c], dst_ref=b_refs[i].at[c], send_sem=send_sems.at[i],
                                         recv_sem=recv_sems.at[i], device_id=(x, y, 1 - c), device_id_type=MESH).start()
        token[...] = jnp.zeros_like(token)

    tspec, tshape = _token_spec()
    outs = pl.pallas_call(
        body, name=name, in_specs=[_HBM] * n + [_ANY], out_specs=(_SEM, _SEM, *[_HBM] * n, tspec),
        out_shape=(pltpu.SemaphoreType.DMA((n,)), pltpu.SemaphoreType.DMA((n,)),
                   *[pltpu.HBM(a.shape, a.dtype) for a in bufs], tshape),
        input_output_aliases={i: 2 + i for i in range(n)},
        compiler_params=pltpu.CompilerParams(has_side_effects=_EFFECT),
    )(*[_in_hbm(a) for a in bufs], after)
    return outs[0], outs[1], list(outs[2:2 + n]), outs[-1]


def _sibling_share_wait(send_sems, recv_sems, bufs, after, *, name):
    n = len(bufs)

    def body(*refs):
        b_refs = refs[:n]
        send_sems, recv_sems = refs[n], refs[n + 1]
        x, y, c, _ = _place()
        for i in range(n):
            cp = pltpu.make_async_remote_copy(src_ref=b_refs[i].at[c], dst_ref=b_refs[i].at[1 - c],
                                              send_sem=send_sems.at[i], recv_sem=recv_sems.at[i],
                                              device_id=(x, y, 1 - c), device_id_type=MESH)
            cp.wait_send()
            cp.wait_recv()

    outs = pl.pallas_call(
        body, name=name, in_specs=[_HBM] * n + [_SEM, _SEM, _ANY], out_specs=[_HBM] * n,
        out_shape=[pltpu.HBM(a.shape, a.dtype) for a in bufs], input_output_aliases={i: i for i in range(n)},
        compiler_params=pltpu.CompilerParams(has_side_effects=_EFFECT),
    )(*bufs, send_sems, recv_sems, after)
    return list(outs)


def _broadcast_all(v, after, *, name):
    def body(v_ref, after_ref, o_ref, send_sems, recv_sems, local_sem):
        x, y, c, _ = _place()
        me = 4 * x + 2 * y + c
        loc = pltpu.make_async_copy(v_ref, o_ref.at[me], local_sem)
        loc.start()
        copies = []
        for k in range(1, 8):
            dx, dy, dc = (k >> 2) & 1, (k >> 1) & 1, k & 1
            to = (1 - x if dx else x, 1 - y if dy else y, 1 - c if dc else c)
            cp = pltpu.make_async_remote_copy(src_ref=v_ref, dst_ref=o_ref.at[me], send_sem=send_sems.at[k - 1],
                                              recv_sem=recv_sems.at[k - 1], device_id=to, device_id_type=MESH)
            cp.start()
            copies.append((cp, k, to))
        for cp, k, to in copies:
            cp.wait_send()
            theirs = o_ref.at[4 * to[0] + 2 * to[1] + to[2]]
            pltpu.make_async_remote_copy(src_ref=theirs, dst_ref=theirs, send_sem=send_sems.at[k - 1],
                                         recv_sem=recv_sems.at[k - 1], device_id=to, device_id_type=MESH).wait_recv()
        loc.wait()

    return pl.pallas_call(
        body, name=name, in_specs=[_ANY, _ANY], out_specs=_ANY,
        out_shape=jax.ShapeDtypeStruct((8,) + v.shape, v.dtype),
        scratch_shapes=[pltpu.SemaphoreType.DMA((7,)), pltpu.SemaphoreType.DMA((7,)), pltpu.SemaphoreType.DMA(())],
    )(v, after)


def _gather_place(shards, place, *, name, after=None):
    names = list(shards)
    bufs, shapes = [], []
    for k in names:
        w, layer = shards[k] if isinstance(shards[k], tuple) else (shards[k], None)
        bufs.append(_place_shard(w, place, dtype=F32 if k == 'small' else BF16, layer=layer, after=after,
                                 name=f"{name}_place_{k}"))
        after = bufs[-1] if after is not None else None
        shapes.append(w.shape[-2:])
    return names, shapes, bufs


def _gather_begin(placed, after, *, name):
    names, shapes, bufs = placed
    send_sems, recv_sems, bufs, token = _gather_ici_start(bufs, after, name=name + "_ici_start")
    return (names, shapes, send_sems, recv_sems, bufs), token


def _gather_end(state, after, *, name):
    names, shapes, send_sems, recv_sems, bufs = state
    bufs = _gather_ici_wait(send_sems, recv_sems, bufs, after, name=name + "_ici_wait")
    bufs = _gather_d2d(bufs, name=name + "_d2d")
    return {k: o.reshape((N_CHIPS,) + sh) for k, o, sh in zip(names, bufs, shapes)}


def _gather_forward(state, after, *, name):
    names, shapes, send_sems, recv_sems, bufs = state
    bufs = _gather_ici_wait(send_sems, recv_sems, bufs, after, name=name + "_ici_wait")
    send_sems, recv_sems, bufs, token = _gather_d2d_start(bufs, after, name=name + "_d2d_start")
    return (names, shapes, send_sems, recv_sems, bufs), token


def _gather_finish(state, after, *, name):
    names, shapes, send_sems, recv_sems, bufs = state
    bufs = _gather_d2d_wait(send_sems, recv_sems, bufs, after, name=name + "_d2d_wait")
    return {k: o.reshape((N_CHIPS,) + sh) for k, o, sh in zip(names, bufs, shapes)}


def _reduce_sibling_start(grads, after, *, name):
    names = list(grads)
    gs = [grads[k].reshape(N_CHIPS, 2, grads[k].shape[1] // 2, grads[k].shape[2]) for k in names]
    send_sems, recv_sems, gs, recvs, token = _sibling_halves_start(gs, after, name=name + "_sib_start")
    return (names, [grads[k].shape[1:] for k in names], send_sems, recv_sems, gs, recvs), token


def _reduce_begin(state, place, after, *, name):
    names, shapes, send_sems, recv_sems, gs, recvs = state
    gs, recvs = _sibling_halves_wait(send_sems, recv_sems, gs, recvs, after, name=name + "_sib_wait")
    ps = [_add_halves(g, r, place, name=f"{name}_add2_{k}") for g, r, k in zip(gs, recvs, names)]
    send_sems, recv_sems, ps, lands, token = _chip_scatter_start(ps, recvs[0], name=name + "_scatter_start")
    return (names, shapes, send_sems, recv_sems, ps, lands), token


def _reduce_end(state, place, after, *, name):
    names, shapes, send_sems, recv_sems, ps, lands = state
    ps, lands = _chip_scatter_wait(send_sems, recv_sems, ps, lands, after, name=name + "_scatter_wait")
    rs = [_sum_chips(p, l, place, name=f"{name}_sum4_{k}") for p, l, k in zip(ps, lands, names)]
    send_sems, recv_sems, rs, _ = _sibling_share_start(rs, lands[0], name=name + "_share_start")
    return names, shapes, send_sems, recv_sems, rs


def _reduce_finish(state, after, *, name):
    names, shapes, send_sems, recv_sems, rs = state
    both = _sibling_share_wait(send_sems, recv_sems, rs, after, name=name + "_share_wait")
    return {k: b.reshape(sh) for k, b, sh in zip(names, both, shapes)}


def _pad_lanes(a, n=LANES):
    return jnp.pad(a, [(0, 0)] * (a.ndim - 1) + [(0, n - a.shape[-1])])


def _unshard_cols(g):
    return jnp.transpose(g, (1, 0, 2)).reshape(g.shape[1], -1)


def _shard_cols(w):
    k, n = w.shape
    return jnp.transpose(w.reshape(k, N_CHIPS, n // N_CHIPS), (1, 0, 2))


def _ffn_fwd(h, p, tag, before_out):
    b = _rms_fwd(h, p['ffn_norm'], name=f"{tag}_ffn_norm")
    up = _mm(b, p['ffn_w_up'], b_sh='n', name=f"{tag}_ffn_up", bn=1408)
    act = _glu_fwd(up, p['ffn_conv_w'], p['ffn_conv_b'], name=f"{tag}_ffn_glu")
    out = _mm(act, p['ffn_w_down'], res=h, after=before_out(act), name=f"{tag}_ffn_down", bk=704)
    return out, (h, b, up, act)


def _ffn_bwd(dh, saved, p, tag, after, on_big):
    h, b, up, act = saved
    dact = _mm(dh, p['ffn_w_down'], tb=True, after=after, out_dtype=BF16, name=f"{tag}_ffn_dact", bn=1408)
    dw_down = _mm(act, dh, ta=True, after=after, out_dtype=BF16, name=f"{tag}_ffn_dwdown", bm=1408)
    dup, dcw, dcb = _glu_bwd(up, p['ffn_conv_w'], p['ffn_conv_b'], dact, name=f"{tag}_ffn_dglu")
    dw_up = _mm(b, dup, ta=True, b_sh='n', o_sh=True, out_dtype=BF16, name=f"{tag}_ffn_dwup", bn=1408)
    sent = on_big({'ffn_w_up': dw_up, 'ffn_w_down': dw_down.reshape(N_CHIPS, -1, dw_down.shape[1])})
    db = _mm(dup, p['ffn_w_up'], a_sh=True, b_sh='k', after=sent, name=f"{tag}_ffn_db", bk=1408)
    dcw = jnp.transpose(dcw, (1, 0, 2)).reshape(dcw.shape[1], -1)
    dcb = dcb.reshape(1, -1)
    dh_in, dg = _rms_bwd(h, p['ffn_norm'], db, res=dh, name=f"{tag}_ffn_dnorm")
    small = {'ffn_norm': dg, 'ffn_conv_w': dcw, 'ffn_conv_b': dcb}
    return dh_in, small


def _qkv_attn_fwd(kind, h, p, tag, n_heads, before_out):
    a = _rms_fwd(h, p['mix_norm'], name=f"{tag}_norm")
    if kind == 'fox':
        qkv = _mm(a, p['w_in'], name=f"{tag}_qkv", bn=896)
        cum = _fgate_fwd(qkv, p['b_f'], fcol=3 * n_heads, name=f"{tag}_fgate")
        cum_t = cum[:, :n_heads].T
        cq, ck = cum_t[:, :, None], cum_t[:, None, :]
    else:
        qkv = _mm(a, p['w_in'], b_sh='n', name=f"{tag}_qkv", bn=768)
        cq = ck = None
    cols = dict(qcol=lambda hh: hh, kcol=lambda hh: n_heads + hh, vcol=lambda hh: 2 * n_heads + hh)
    o = _attn_fwd(kind, qkv, qkv, qkv, name=f"{tag}_attn", n_heads=n_heads, dqk=HEAD_DIM, scale=HEAD_DIM ** -0.5,
                  gains=p['qk_gain'], cq=cq, ck=ck, **cols)
    out = _mm(o, p['w_out'], res=h, after=before_out(o), name=f"{tag}_out")
    return out, (h, a, qkv, o, cq, ck)


def _qkv_attn_bwd(kind, dh, saved, p, tag, n_heads, after, on_big):
    h, a, qkv, o, cq, ck = saved
    do = _mm(dh, p['w_out'], tb=True, after=after, out_dtype=BF16, name=f"{tag}_do")
    dw_out = _mm(o, dh, ta=True, after=after, out_dtype=BF16, name=f"{tag}_dwout")
    cols = dict(qcol=lambda hh: hh, kcol=lambda hh: n_heads + hh, vcol=lambda hh: 2 * n_heads + hh)
    outs = _attn_bwd(kind, qkv, qkv, qkv, o, do, name=f"{tag}_dattn", n_heads=n_heads, dqk=HEAD_DIM,
                     scale=HEAD_DIM ** -0.5, gains=p['qk_gain'], cq=cq, ck=ck, **cols)
    dq, dk, dv, dgain = outs[:4]
    small = {'q_gain': dgain[0], 'k_gain': dgain[1]}
    if kind == 'fox':
        dcq, dck = outs[4:]
        dca = _pad_lanes(dcq[:, :, 0].T)
        dcb = _pad_lanes(dck[:, 0, :].T)
        dflog, dbf = _fgate_bwd(qkv, p['b_f'], dca, dcb, fcol=3 * n_heads, n_heads=n_heads, name=f"{tag}_dfgate")
        small['b_f'] = dbf[:, :n_heads]
        dqkv = jnp.concatenate([dq, dk, dv, dflog], axis=1)
        dw_in = _mm(a, dqkv, ta=True, out_dtype=BF16, name=f"{tag}_dwin", bn=896)
        dw_in = _shard_cols(dw_in[:, :3 * n_heads * HEAD_DIM + n_heads])
    else:
        dqkv = jnp.concatenate([dq, dk, dv], axis=1)
        dw_in = _mm(a, dqkv, ta=True, o_sh=True, out_dtype=BF16, name=f"{tag}_dwin", bn=768)
    sent = on_big({'w_in': dw_in, 'w_out': dw_out.reshape(N_CHIPS, -1, dw_out.shape[1])})
    if kind == 'fox':
        da = _mm(dqkv, p['w_in'], tb=True, after=sent, name=f"{tag}_da", bk=896)
    else:
        da = _mm(dqkv, p['w_in'], b_sh='k', after=sent, name=f"{tag}_da", bk=768)
    dh_in, dg = _rms_bwd(h, p['mix_norm'], da, res=dh, name=f"{tag}_dnorm")
    small['mix_norm'] = dg
    return dh_in, small


def _mla_fwd(h, p, tag, n_heads, before_out):
    a = _rms_fwd(h, p['mix_norm'], name=f"{tag}_norm")
    c = _mm(a, p['w_in'], name=f"{tag}_latent", bn=1152)
    cn = _mla_latent_fwd(c, p['a_gain'], name=f"{tag}_latent_norm")
    qp = _mm(cn[:, :MLA_Q_RANK], p['w_q_b'], name=f"{tag}_q_up")
    kv = _mm(cn[:, MLA_Q_RANK:], p['w_kv_b'], b_sh='n', name=f"{tag}_kv_up")
    qc, kc = _mla_prep_fwd(qp, kv, c, p['gq'], p['gk'], p['cos'], p['sin'], n_heads=n_heads, name=f"{tag}_prep")
    cols = dict(qcol=lambda hh: hh, kcol=lambda hh: hh, vcol=lambda hh: 2 * hh + 1)
    scale = (MLA_NOPE + MLA_ROPE) ** -0.5
    o = _attn_fwd('mla', qc, kc, kv, name=f"{tag}_attn", n_heads=n_heads, dqk=2 * LANES, scale=scale, **cols)
    out = _mm(o, p['w_out'], res=h, after=before_out(o), name=f"{tag}_out")
    return out, (h, a, c, cn, qp, kv, qc, kc, o)


def _mla_bwd(dh, saved, p, tag, n_heads, after, on_big):
    h, a, c, cn, qp, kv, qc, kc, o = saved
    do = _mm(dh, p['w_out'], tb=True, after=after, out_dtype=BF16, name=f"{tag}_do")
    dw_out = _mm(o, dh, ta=True, after=after, out_dtype=BF16, name=f"{tag}_dwout")
    cols = dict(qcol=lambda hh: hh, kcol=lambda hh: hh, vcol=lambda hh: 2 * hh + 1)
    scale = (MLA_NOPE + MLA_ROPE) ** -0.5
    dqc, dkc, dv = _attn_bwd('mla', qc, kc, kv, o, do, name=f"{tag}_dattn", n_heads=n_heads, dqk=2 * LANES,
                             scale=scale, **cols)
    dqn, dqr, dkv, dkr, dgq, dgk = _mla_prep_bwd(qp, kv, c, p['gq'], p['gk'], p['cos'], p['sin'], dqc, dkc, dv,
                                                 n_heads=n_heads, name=f"{tag}_dprep")
    dqp = jnp.concatenate([dqn, dqr], axis=1)
    cn_q, cn_kv = cn[:, :MLA_Q_RANK], cn[:, MLA_Q_RANK:]
    dw_q_b = _mm(cn_q, dqp, ta=True, out_dtype=BF16, name=f"{tag}_dwqb", bm=512)
    dcn_q = _mm(dqp, p['w_q_b'], tb=True, out_dtype=BF16, name=f"{tag}_dcnq")
    dw_kv_b = _mm(cn_kv, dkv, ta=True, o_sh=True, out_dtype=BF16, name=f"{tag}_dwkvb", bm=512)
    dcn_kv = _mm(dkv, p['w_kv_b'], b_sh='k', out_dtype=BF16, name=f"{tag}_dcnkv")
    dc, dga = _mla_latent_bwd(c, p['a_gain'], dcn_q, dcn_kv, dkr, name=f"{tag}_dlatent")
    dw_in = _mm(a, dc, ta=True, out_dtype=BF16, name=f"{tag}_dwin", bn=1152)
    k_rank = dw_q_b.shape[0]
    nope = dw_q_b[:, :n_heads * LANES].reshape(k_rank, n_heads, LANES)
    rope = dw_q_b[:, n_heads * LANES:].reshape(k_rank, n_heads, LANES)[:, :, :MLA_ROPE]
    dw_q_b = jnp.concatenate([nope, rope], axis=2).reshape(k_rank, n_heads * (MLA_NOPE + MLA_ROPE))
    w_in_cols = MLA_Q_RANK + MLA_KV_RANK + MLA_ROPE
    sent = on_big({'w_in': dw_in[:, :w_in_cols].reshape(N_CHIPS, -1, w_in_cols), 'w_q_b': _shard_cols(dw_q_b),
                   'w_kv_b': dw_kv_b, 'w_out': dw_out.reshape(N_CHIPS, -1, dw_out.shape[1])})
    da = _mm(dc, p['w_in'], tb=True, after=sent, name=f"{tag}_da", bk=1152)
    dh_in, dg = _rms_bwd(h, p['mix_norm'], da, res=dh, name=f"{tag}_dnorm")
    small = {'mix_norm': dg, 'q_a_gain': dga[:, :MLA_Q_RANK], 'kv_a_gain': dga[:, MLA_Q_RANK:],
             'q_gain': jnp.concatenate([dgq[0], dgq[1][:, :MLA_ROPE]], axis=1),
             'k_gain': jnp.concatenate([dgk[0], dgk[1][:, :MLA_ROPE]], axis=1)}
    return dh_in, small


def _sgu_fwd(h, p, tag, before_out):
    a = _rms_fwd(h, p['mix_norm'], name=f"{tag}_norm")
    uv = _mm(a, p['w_in'], b_sh='n', name=f"{tag}_in")
    u, vn = _sgu_act_fwd(uv, p['v_gain'], name=f"{tag}_act")
    gated = _sgu_mix_fwd(u, vn, p['w_s'], p['b_s'], name=f"{tag}_mix")
    out = _mm(gated, p['w_out'], res=h, after=before_out(gated), name=f"{tag}_out")
    return out, (h, a, uv, u, vn, gated)


def _sgu_bwd(dh, saved, p, tag, after, on_big):
    h, a, uv, u, vn, gated = saved
    dgated = _mm(dh, p['w_out'], tb=True, after=after, out_dtype=BF16, name=f"{tag}_dgated")
    dw_out = _mm(gated, dh, ta=True, after=after, out_dtype=BF16, name=f"{tag}_dwout")
    du, dvn, dws, dbs = _sgu_mix_bwd(u, vn, p['w_s'], p['b_s'], dgated, name=f"{tag}_dmix")
    duv, dvg = _sgu_act_bwd(uv, p['v_gain'], du, dvn, name=f"{tag}_dact")
    dw_in = _mm(a, duv, ta=True, o_sh=True, out_dtype=BF16, name=f"{tag}_dwin")
    sent = on_big({'w_in': dw_in, 'w_out': dw_out.reshape(N_CHIPS, -1, dw_out.shape[1])})
    da = _mm(duv, p['w_in'], b_sh='k', after=sent, name=f"{tag}_da")
    dh_in, dg = _rms_bwd(h, p['mix_norm'], da, res=dh, name=f"{tag}_dnorm")
    small = {'mix_norm': dg, 'v_gain': dvg, 'w_s': dws, 'b_s': dbs[:, :, 0]}
    return dh_in, small


def _pack(parts):
    flat = jnp.concatenate([p.reshape(-1).astype(F32) for p in parts])
    rows = -(-flat.shape[0] // LANES)
    rows = -(-rows // 32) * 32
    return jnp.pad(flat, (0, rows * LANES - flat.shape[0])).reshape(rows, LANES)


def _unpack(packed, shapes):
    flat = packed.reshape(-1)
    out, off = [], 0
    for s in shapes:
        n = 1
        for d in s:
            n *= d
        out.append(flat[off:off + n].reshape(s))
        off += n
    return out


MIXERS = ('fox', 'mla', 'sb', 'sgu')
WEIGHT_NAMES = ['mix_norm', 'ffn_norm', 'fox_w_in', 'fox_b_f', 'fox_q_gain', 'fox_k_gain', 'fox_w_out', 'mla_w_in',
                'mla_q_a_gain', 'mla_kv_a_gain', 'mla_w_q_b', 'mla_w_kv_b', 'mla_q_gain', 'mla_k_gain', 'mla_w_out',
                'sb_w_in', 'sb_q_gain', 'sb_k_gain', 'sb_w_out', 'sgu_w_in', 'sgu_v_gain', 'sgu_w_s', 'sgu_b_s',
                'sgu_w_out', 'ffn_w_up', 'ffn_conv_w', 'ffn_conv_b', 'ffn_w_down']
SMALL_SHARDED = {'mla_q_a_gain': 1, 'mla_kv_a_gain': 1, 'sgu_v_gain': 1, 'ffn_conv_w': 2}
BIG = ['fox_w_in', 'fox_w_out', 'mla_w_in', 'mla_w_q_b', 'mla_w_kv_b', 'mla_w_out', 'sb_w_in', 'sb_w_out', 'sgu_w_in',
       'sgu_w_out', 'ffn_w_up', 'ffn_w_down']


def kernel(x, positions, mix_norm, ffn_norm, fox_w_in, fox_b_f, fox_q_gain, fox_k_gain, fox_w_out, mla_w_in, mla_q_a_gain, mla_kv_a_gain, mla_w_q_b, mla_w_kv_b, mla_q_gain, mla_k_gain, mla_w_out, sb_w_in, sb_q_gain, sb_k_gain, sb_w_out, sgu_w_in, sgu_v_gain, sgu_w_s, sgu_b_s, sgu_w_out, ffn_w_up, ffn_conv_w, ffn_conv_b, ffn_w_down, loss_target, m_mix_norm, m_ffn_norm, m_fox_w_in, m_fox_b_f, m_fox_q_gain, m_fox_k_gain, m_fox_w_out, m_mla_w_in, m_mla_q_a_gain, m_mla_kv_a_gain, m_mla_w_q_b, m_mla_w_kv_b, m_mla_q_gain, m_mla_k_gain, m_mla_w_out, m_sb_w_in, m_sb_q_gain, m_sb_k_gain, m_sb_w_out, m_sgu_w_in, m_sgu_v_gain, m_sgu_w_s, m_sgu_b_s, m_sgu_w_out, m_ffn_w_up, m_ffn_conv_w, m_ffn_conv_b, m_ffn_w_down, v_mix_norm, v_ffn_norm, v_fox_w_in, v_fox_b_f, v_fox_q_gain, v_fox_k_gain, v_fox_w_out, v_mla_w_in, v_mla_q_a_gain, v_mla_kv_a_gain, v_mla_w_q_b, v_mla_w_kv_b, v_mla_q_gain, v_mla_k_gain, v_mla_w_out, v_sb_w_in, v_sb_q_gain, v_sb_k_gain, v_sb_w_out, v_sgu_w_in, v_sgu_v_gain, v_sgu_w_s, v_sgu_b_s, v_sgu_w_out, v_ffn_w_up, v_ffn_conv_w, v_ffn_conv_b, v_ffn_w_down):
    args = dict(locals())
    W = {k: args[k] for k in WEIGHT_NAMES}
    M = {k: args['m_' + k] for k in WEIGHT_NAMES}
    V = {k: args['v_' + k] for k in WEIGHT_NAMES}
    depth = mix_norm.shape[0]
    s_len, d_model = x.shape[1], x.shape[2]
    n_heads = d_model // HEAD_DIM
    assert all(W[k].shape[0] == 1 for k in WEIGHT_NAMES if k.split('_')[0] in MIXERS), "one layer per mixer"
    xi, yi, ci = lax.axis_index("x"), lax.axis_index("y"), lax.axis_index("c")
    chip = 2 * xi + yi
    place = tuple(jnp.reshape(v, (1,)).astype(jnp.int32) for v in (xi, yi, ci))

    small_local = _pack([W[k][0] if k != 'ffn_conv_w' else W[k] for k in SMALL_SHARDED])

    def piece_shards(i, part):
        if part == 'ffn':
            return {'ffn_w_up': (W['ffn_w_up'], i), 'ffn_w_down': (W['ffn_w_down'], i)}
        mixer = MIXERS[i % len(MIXERS)]
        shards = {k: W[k][0] for k in BIG if k.startswith(mixer + '_')}
        if i == 0:
            shards['small'] = small_local
        return shards

    pieces = [(i, part) for i in range(depth) for part in ('mixer', 'ffn')]
    gathered = {}
    pname = lambda pc: f"gather_l{pc[0]}_{pc[1]}"
    states = {}
    placed = {pieces[0]: _gather_place(piece_shards(*pieces[0]), place, name=pname(pieces[0]))}
    states[pieces[0]], token = _gather_begin(placed[pieces[0]], mix_norm, name=pname(pieces[0]))
    prev = token
    for pc in pieces[1:]:
        placed[pc] = _gather_place(piece_shards(*pc), place, name=pname(pc), after=prev)
        prev = placed[pc][2][-1]
    gathered[pieces[0]] = _gather_end(states.pop(pieces[0]), placed[pieces[-1]][2][-1], name=pname(pieces[0]))
    first_done = next(iter(gathered[pieces[0]].values()))
    states[pieces[1]], token = _gather_begin(placed[pieces[1]], first_done, name=pname(pieces[1]))
    tokens = [token]
    small_shapes = [W[k][0].shape if k != 'ffn_conv_w' else W[k].shape for k in SMALL_SHARDED]
    per_chip = [_unpack(gathered[pieces[0]]['small'][s], small_shapes) for s in range(N_CHIPS)]
    full_small = {k: jnp.concatenate([per_chip[s][j] for s in range(N_CHIPS)], axis=-1)
                  for j, k in enumerate(SMALL_SHARDED)}

    pos = positions.reshape(s_len).astype(F32)
    inv_freq = ROPE_THETA ** (-jnp.arange(0, MLA_ROPE, 2, dtype=F32) / MLA_ROPE)
    ang = pos[:, None] * inv_freq
    cos_t = _pad_lanes(jnp.concatenate([jnp.cos(ang), jnp.cos(ang)], axis=1))
    sin_t = _pad_lanes(jnp.concatenate([-jnp.sin(ang), jnp.sin(ang)], axis=1))

    def piece_params(i, part):
        mixer = MIXERS[i % len(MIXERS)]
        g = gathered[(i, part)]
        if part == 'ffn':
            return mixer, {'ffn_norm': ffn_norm[i:i + 1], 'ffn_w_up': g['ffn_w_up'],
                           'ffn_w_down': g['ffn_w_down'].reshape(-1, d_model),
                           'ffn_conv_w': full_small['ffn_conv_w'][i], 'ffn_conv_b': ffn_conv_b[i:i + 1]}
        p = {'mix_norm': mix_norm[i:i + 1]}
        rows = lambda w: w.reshape(-1, w.shape[-1])
        if mixer == 'fox':
            w = _unshard_cols(g['fox_w_in'])
            p['w_in'] = jnp.pad(w, ((0, 0), (0, (3 * n_heads + 1) * HEAD_DIM - w.shape[1])))
            p['b_f'] = _pad_lanes(fox_b_f)
            p['qk_gain'] = jnp.stack([fox_q_gain, fox_k_gain])
            p['w_out'] = rows(g['fox_w_out'])
        elif mixer == 'sb':
            p['w_in'] = g['sb_w_in']
            p['qk_gain'] = jnp.stack([sb_q_gain, sb_k_gain])
            p['w_out'] = rows(g['sb_w_out'])
        elif mixer == 'sgu':
            p['w_in'] = g['sgu_w_in']
            p['v_gain'] = full_small['sgu_v_gain'].reshape(1, -1)
            p['w_s'] = sgu_w_s[0]
            p['b_s'] = sgu_b_s[0][:, :, None]
            p['w_out'] = rows(g['sgu_w_out'])
        else:
            w = rows(g['mla_w_in'])
            p['w_in'] = jnp.pad(w, ((0, 0), (0, MLA_Q_RANK + MLA_KV_RANK + LANES - w.shape[1])))
            p['a_gain'] = jnp.concatenate([full_small['mla_q_a_gain'], full_small['mla_kv_a_gain']]).reshape(1, -1)
            wq = _unshard_cols(g['mla_w_q_b']).reshape(MLA_Q_RANK, n_heads, MLA_NOPE + MLA_ROPE)
            p['w_q_b'] = jnp.concatenate([wq[:, :, :MLA_NOPE].reshape(MLA_Q_RANK, -1),
                                          _pad_lanes(wq[:, :, MLA_NOPE:]).reshape(MLA_Q_RANK, -1)], axis=1)
            p['w_kv_b'] = g['mla_w_kv_b']
            p['gq'] = jnp.stack([mla_q_gain[:, :MLA_NOPE], _pad_lanes(mla_q_gain[:, MLA_NOPE:])])
            p['gk'] = jnp.stack([mla_k_gain[:, :MLA_NOPE], _pad_lanes(mla_k_gain[:, MLA_NOPE:])])
            p['cos'], p['sin'] = cos_t, sin_t
            p['w_out'] = rows(g['mla_w_out'])
        return mixer, p

    h = x.reshape(s_len, d_model)
    saved = []
    for n, (i, part) in enumerate(pieces):
        nxt = pieces[n + 1] if n + 1 < len(pieces) else None
        ahead = pieces[n + 2] if n + 2 < len(pieces) else None
        if ahead is not None:
            after = next(iter(gathered[(i, part)].values()))
            states[ahead], token = _gather_begin(placed[ahead], after, name=pname(ahead))
            tokens.append(token)
        mixer, p = piece_params(i, part)
        gain = 'ffn_norm' if part == 'ffn' else 'mix_norm'
        for token in tokens:
            p[gain] = p[gain] + token[0:1, 0:1]
        tokens = []
        tag = f"l{i}_{mixer}"
        forwarding = []

        def before_out(made, nxt=nxt, forwarding=forwarding):
            if nxt is None:
                return None
            st, tok = _gather_forward(states.pop(nxt), made, name=pname(nxt))
            forwarding.append(st)
            return tok

        if part == 'ffn':
            h, sv = _ffn_fwd(h, p, f"l{i}", before_out)
        elif mixer in ('fox', 'sb'):
            h, sv = _qkv_attn_fwd(mixer, h, p, tag, n_heads, before_out)
        elif mixer == 'mla':
            h, sv = _mla_fwd(h, p, tag, n_heads, before_out)
        else:
            h, sv = _sgu_fwd(h, p, tag, before_out)
        saved.append((mixer, p, sv))
        if nxt is not None:
            gathered[nxt] = _gather_finish(forwarding[0], h, name=pname(nxt))
    loss_row, dh = _loss(h, loss_target.reshape(s_len, d_model))
    loss = lax.psum(loss_row[0, 0], ("x", "y", "c"))

    big_grads, small_grads = {}, {k: [None] * depth for k in ('mix_norm', 'ffn_norm', 'ffn_conv_w', 'ffn_conv_b')}

    def keep(reduced, i):
        for k, v in reduced.items():
            if k.startswith('ffn_'):
                big_grads.setdefault(k, [None] * depth)[i] = v
            else:
                big_grads[k] = v[None]

    state, token, flying = None, jnp.broadcast_to(loss, (8, LANES)), None
    shares = []
    for n in reversed(range(len(pieces))):
        i, part = pieces[n]
        mixer, p, sv = saved[n]
        tag = f"l{i}_{mixer}"
        rname = f"reduce_l{i}_{part}"
        started = []

        def on_big(big, prefix=('' if part == 'ffn' else mixer + '_'), rname=rname, started=started):
            st, tok = _reduce_sibling_start({prefix + k: v for k, v in big.items()}, place[0], name=rname)
            started.append(st)
            return tok

        if part == 'ffn':
            dh, small = _ffn_bwd(dh, sv, p, f"l{i}", token, on_big)
        elif mixer in ('fox', 'sb'):
            dh, small = _qkv_attn_bwd(mixer, dh, sv, p, tag, n_heads, token, on_big)
        elif mixer == 'mla':
            dh, small = _mla_bwd(dh, sv, p, tag, n_heads, token, on_big)
        else:
            dh, small = _sgu_bwd(dh, sv, p, tag, token, on_big)
        if state is not None:
            fname = f"reduce_l{flying[0]}_{flying[1]}"
            shares.append((_reduce_end(state, place, dh, name=fname), flying[0], fname))
        state, token = _reduce_begin(started[0], place, dh, name=rname)
        flying = (i, part)
        for k, v in small.items():
            if k in small_grads:
                small_grads[k][i] = v
            else:
                small_grads[f"{mixer}_{k}"] = v
    last_state = state
    grad_x = dh.reshape(x.shape)
    for k in ('mix_norm', 'ffn_norm', 'ffn_conv_b'):
        small_grads[k] = jnp.concatenate(small_grads[k], axis=0)
    small_grads['ffn_conv_w'] = jnp.stack(small_grads['ffn_conv_w'])

    small_names = [k for k in WEIGHT_NAMES if k not in BIG]
    full_shapes = {k: (W[k].shape[:-1] + (W[k].shape[-1] * N_CHIPS,) if k in SMALL_SHARDED else W[k].shape)
                   for k in small_names}
    packed = _pack([small_grads[k].reshape(full_shapes[k]) for k in small_names])
    last_token = token
    summed = _sum_devices(_broadcast_all(packed, last_token, name="small_bcast"), name="small_sum")
    small_full = dict(zip(small_names, _unpack(summed, [full_shapes[k] for k in small_names])))
    for share_state, i, fname in shares:
        keep(_reduce_finish(share_state, summed, name=fname), i)
    for k in ('ffn_w_up', 'ffn_w_down'):
        big_grads[k] = jnp.stack(big_grads[k])
    grads = dict(big_grads)
    for k in small_names:
        g = small_full[k]
        if k in SMALL_SHARDED:
            n = W[k].shape[-1]
            g = lax.dynamic_slice_in_dim(g, chip * n, n, axis=g.ndim - 1)
        grads[k] = g

    delta, new_m, new_v = {}, {}, {}

    def update(k):
        grads[k] = grads[k].reshape(W[k].shape)
        delta[k], new_m[k], new_v[k] = _adamw(W[k], grads[k], M[k], V[k], after=last_token, name=f"adamw_{k}")

    last_names = [k for k in BIG if k.startswith(MIXERS[0] + '_')]
    for k in WEIGHT_NAMES:
        if k not in last_names:
            update(k)
    share_state = _reduce_end(last_state, place, delta['ffn_w_up'], name="reduce_l0_mixer")
    keep(_reduce_finish(share_state, delta['ffn_w_down'], name="reduce_l0_mixer"), 0)
    for k in last_names:
        grads[k] = big_grads[k]
        update(k)
    return (loss, grad_x, *[grads[k] for k in WEIGHT_NAMES], *[delta[k] for k in WEIGHT_NAMES],
            *[new_m[k] for k in WEIGHT_NAMES], *[new_v[k] for k in WEIGHT_NAMES])
```

```python
import functools

import jax
import jax.numpy as jnp
from jax import lax
from jax.experimental import pallas as pl
from jax.experimental.pallas import tpu as pltpu

F32 = jnp.float32
BF16 = jnp.bfloat16
LANES = 128
HEAD_DIM = 128
NORM_EPS = 1e-6
MLA_Q_RANK = 512
MLA_KV_RANK = 512
MLA_NOPE = 128
MLA_ROPE = 64
ROPE_THETA = 10000.0
SGU_CHUNK = 128
N_CHIPS = 4
ADAM_LR, ADAM_B1, ADAM_B2, ADAM_EPS, ADAM_WD, ADAM_STEP = 0.001, 0.9, 0.999, 1e-08, 0.01, 10
VMEM_LIMIT_BYTES = 56 * 1024 * 1024
MM_VMEM_BUDGET_BYTES = 36 * 1024 * 1024
SHARD_BLOCK_BYTES = 4 * 1024 * 1024
GLU_ROWS, GLU_HALO = 128, 16
MESH = pl.DeviceIdType.MESH
NEG_BIG = -1e30


def _params(*sem):
    return pltpu.CompilerParams(dimension_semantics=sem, vmem_limit_bytes=VMEM_LIMIT_BYTES)


def _div_block(n, target, mult=LANES):
    if n <= target:
        return n
    best = None
    for b in range(mult, target + 1, mult):
        if n % b == 0:
            best = b
    assert best is not None, (n, target, mult)
    return best


def _after_spec(after):
    if after is None:
        return []
    return [pl.BlockSpec(memory_space=pl.ANY)]


def _after_arg(after):
    return [] if after is None else [after]


def _iota(shape, dim):
    return lax.broadcasted_iota(jnp.int32, shape, dim)


def _dot(a, b, ca, cb):
    return lax.dot_general(a, b, (((ca,), (cb,)), ((), ())), preferred_element_type=F32)


def _mm(a, b, *, name, ta=False, tb=False, a_sh=False, b_sh=None, o_sh=False, res=None, after=None, out_dtype=F32,
        bm=1024, bn=1024, bk=512):
    if a_sh:
        assert not ta
        m, k = a.shape[1], a.shape[0] * a.shape[2]
    else:
        m, k = (a.shape[1], a.shape[0]) if ta else a.shape
    if b_sh == 'n':
        n = b.shape[2] * b.shape[0]
        assert b.shape[1] == k and not tb
    elif b_sh == 'k':
        n = b.shape[1]
        assert b.shape[2] * N_CHIPS == k
    else:
        n = b.shape[0] if tb else b.shape[1]
        assert (b.shape[1] if tb else b.shape[0]) == k
    n_sh = n // N_CHIPS
    k_sh = k // N_CHIPS
    bm = _div_block(m, bm, 8 if not ta else LANES)
    bn_limit = n
    if b_sh == 'n':
        bn_limit = b.shape[2]
    if o_sh:
        bn_limit = min(bn_limit, n_sh)
    bn = _div_block(bn_limit, bn)
    assert (not o_sh or n_sh % bn == 0) and (b_sh != 'n' or b.shape[2] % bn == 0)
    bk_limit = k_sh if b_sh == 'k' else k
    if a_sh:
        bk_limit = min(bk_limit, a.shape[2])

    def footprint(kb):
        io = bm * kb * a.dtype.itemsize + kb * bn * b.dtype.itemsize + bm * bn * jnp.dtype(out_dtype).itemsize
        if res is not None:
            io += bm * bn * res.dtype.itemsize
        return 2 * io + (bm * bn * 4 if kb < k else 0)

    bk = max([kb for kb in range(LANES, bk_limit + 1, LANES)
              if bk_limit % kb == 0 and (footprint(kb) <= MM_VMEM_BUDGET_BYTES or kb <= bk)])
    assert (not a_sh or a.shape[2] % bk == 0) and (b_sh != 'k' or k_sh % bk == 0) and k % bk == 0
    nbo = n_sh // bn if o_sh else 1
    nbb = b.shape[2] // bn if b_sh == 'n' else 1
    nks = k_sh // bk if b_sh == 'k' else 1
    nka = a.shape[2] // bk if a_sh else 1
    nk = k // bk

    if a_sh:
        a_spec = pl.BlockSpec((None, bm, bk), lambda i, j, q: (q // nka, i, q % nka))
    elif ta:
        a_spec = pl.BlockSpec((bk, bm), lambda i, j, q: (q, i))
    else:
        a_spec = pl.BlockSpec((bm, bk), lambda i, j, q: (i, q))
    if b_sh == 'n':
        b_spec = pl.BlockSpec((None, bk, bn), lambda i, j, q: (j // nbb, q, j % nbb))
    elif b_sh == 'k':
        b_spec = pl.BlockSpec((None, bn, bk), lambda i, j, q: (q // nks, j, q % nks))
    elif tb:
        b_spec = pl.BlockSpec((bn, bk), lambda i, j, q: (j, q))
    else:
        b_spec = pl.BlockSpec((bk, bn), lambda i, j, q: (q, j))
    if o_sh:
        o_spec = pl.BlockSpec((None, bm, bn), lambda i, j, q: (j // nbo, i, j % nbo))
        o_shape = jax.ShapeDtypeStruct((N_CHIPS, m, n_sh), out_dtype)
    else:
        o_spec = pl.BlockSpec((bm, bn), lambda i, j, q: (i, j))
        o_shape = jax.ShapeDtypeStruct((m, n), out_dtype)
    tb_eff = tb or b_sh == 'k'

    def body(a_ref, b_ref, *rest):
        rest = list(rest)
        if after is not None:
            rest.pop(0)
        r_ref = rest.pop(0) if res is not None else None
        o_ref = rest.pop(0)
        part = _dot(a_ref[...].astype(BF16), b_ref[...].astype(BF16), 0 if ta else 1, 1 if tb_eff else 0)

        def finish(r):
            if res is not None:
                r = r + r_ref[...].astype(F32)
            o_ref[...] = r.astype(out_dtype)

        if nk == 1:
            finish(part)
            return
        acc, = rest
        q = pl.program_id(2)

        @pl.when(q == 0)
        def _():
            acc[...] = part

        @pl.when(q > 0)
        def _():
            acc[...] += part

        @pl.when(q == nk - 1)
        def _():
            finish(acc[...])

    ins = [a, b]
    in_specs = [a_spec, b_spec]
    if after is not None:
        ins.append(after)
        in_specs.append(pl.BlockSpec((8, LANES), lambda i, j, q: (0, 0)))
    if res is not None:
        assert not o_sh
        ins.append(res)
        in_specs.append(pl.BlockSpec((bm, bn), lambda i, j, q: (i, j)))
    return pl.pallas_call(
        body, name=name, grid=(m // bm, n // bn, nk), in_specs=in_specs, out_specs=o_spec, out_shape=o_shape,
        scratch_shapes=[pltpu.VMEM((bm, bn), F32)] if nk > 1 else [],
        compiler_params=_params("parallel", "parallel", "arbitrary"))(*ins)


def _rms_fwd(x, g, *, name, out_dtype=BF16, br=256):
    r, c = x.shape
    br = _div_block(r, br, 8)

    def body(x_ref, g_ref, o_ref):
        xv = x_ref[...].astype(F32)
        inv = lax.rsqrt(jnp.mean(xv * xv, axis=-1, keepdims=True) + NORM_EPS)
        o_ref[...] = (xv * inv * g_ref[...]).astype(out_dtype)

    return pl.pallas_call(
        body, name=name, grid=(r // br,),
        in_specs=[pl.BlockSpec((br, c), lambda i: (i, 0)), pl.BlockSpec((1, c), lambda i: (0, 0))],
        out_specs=pl.BlockSpec((br, c), lambda i: (i, 0)), out_shape=jax.ShapeDtypeStruct((r, c), out_dtype),
        compiler_params=_params("parallel"))(x, g)


def _rms_bwd_math(xv, gv, dyv, n):
    inv = lax.rsqrt(jnp.sum(xv * xv, axis=-1, keepdims=True) / n + NORM_EPS)
    xh = xv * inv
    dyg = dyv * gv
    dx = inv * (dyg - xh * (jnp.sum(dyg * xh, axis=-1, keepdims=True) / n))
    return dx, dyv * xh


def _rms_bwd(x, g, dy, *, name, res=None, br=256):
    r, c = x.shape
    br = _div_block(r, br, 8)

    def body(x_ref, g_ref, dy_ref, *rest):
        if res is not None:
            r_ref, dx_ref, dg_ref = rest
        else:
            dx_ref, dg_ref = rest
        dx, dgr = _rms_bwd_math(x_ref[...].astype(F32), g_ref[...], dy_ref[...].astype(F32), c)
        if res is not None:
            dx = dx + r_ref[...]
        dx_ref[...] = dx

        @pl.when(pl.program_id(0) == 0)
        def _():
            dg_ref[...] = jnp.zeros_like(dg_ref)

        dg_ref[...] += jnp.sum(dgr, axis=0, keepdims=True)

    row = pl.BlockSpec((br, c), lambda i: (i, 0))
    vec = pl.BlockSpec((1, c), lambda i: (0, 0))
    ins = [x, g, dy] + ([res] if res is not None else [])
    return pl.pallas_call(
        body, name=name, grid=(r // br,), in_specs=[row, vec, row] + ([row] if res is not None else []),
        out_specs=[row, vec], out_shape=[jax.ShapeDtypeStruct((r, c), F32), jax.ShapeDtypeStruct((1, c), F32)],
        compiler_params=_params("arbitrary"))(*ins)


def _loss(y, target, *, name="loss", br=256):
    r, c = y.shape
    br = _div_block(r, br, 8)

    def body(y_ref, t_ref, l_ref, dy_ref):
        d = y_ref[...] - t_ref[...]
        dy_ref[...] = d * (1.0 / c)

        @pl.when(pl.program_id(0) == 0)
        def _():
            l_ref[...] = jnp.zeros_like(l_ref)

        part = jnp.sum(d * d, axis=0, keepdims=True)
        l_ref[...] += (0.5 / c) * jnp.sum(part, axis=1, keepdims=True) * jnp.ones((1, LANES), F32)

    row = pl.BlockSpec((br, c), lambda i: (i, 0))
    return pl.pallas_call(
        body, name=name, grid=(r // br,), in_specs=[row, row],
        out_specs=[pl.BlockSpec((1, LANES), lambda i: (0, 0)), row],
        out_shape=[jax.ShapeDtypeStruct((1, LANES), F32), jax.ShapeDtypeStruct((r, c), F32)],
        compiler_params=_params("arbitrary"))(y, target)


def _split2(x):
    hi = x.astype(BF16)
    lo = (x - hi.astype(F32)).astype(BF16)
    return hi, lo


def _lane_scan(x, *, suffix):
    rows, n = x.shape
    nb = n // LANES
    a, b = _iota((LANES, LANES), 0), _iota((LANES, LANES), 1)
    tri = ((a > b) if suffix else (a < b)).astype(BF16)
    outs = [None] * nb
    run = jnp.zeros((rows, 1), F32)
    order = range(nb - 1, -1, -1) if suffix else range(nb)
    for blk in order:
        xb = x[:, blk * LANES:(blk + 1) * LANES]
        hi, lo = _split2(xb)
        outs[blk] = _dot(hi, tri, 1, 0) + _dot(lo, tri, 1, 0) + run
        run = run + jnp.sum(xb, axis=-1, keepdims=True)
    return jnp.concatenate(outs, axis=1)


def _softplus(z):
    return jnp.maximum(z, 0.0) + jnp.log(1.0 + jnp.exp(-jnp.abs(z)))


def _head_norm(x, g):
    xv = x.astype(F32)
    inv = lax.rsqrt(jnp.mean(xv * xv, axis=-1, keepdims=True) + NORM_EPS)
    return xv * inv * g


def _attn_weights(kind, qn, kn, scale, qi, bq, bias):
    s = _dot(qn, kn, 1, 1) * scale
    row = qi * bq + _iota(s.shape, 0)
    col = _iota(s.shape, 1)
    if kind == 'sb':
        strict = col < row
        sp = _softplus(s)
        after = _lane_scan(jnp.where(strict, -sp, 0.0), suffix=True)
        w = jnp.where(strict, jnp.exp(s - sp + after), 0.0)
        return w, (strict, s - sp)
    if bias is not None:
        s = s + bias
    s = jnp.where(col <= row, s, NEG_BIG)
    mx = jnp.max(s, axis=-1, keepdims=True)
    e = jnp.exp(s - mx)
    return e, jnp.sum(e, axis=-1, keepdims=True)


def _attn_fwd(kind, q, k, v, *, name, n_heads, dqk, qcol, kcol, vcol, scale, gains=None, cq=None, ck=None, bq=256):
    s_len = q.shape[0]
    bq = _div_block(s_len, bq, 8)
    norm, fox = gains is not None, cq is not None

    def body(*refs):
        refs = list(refs)
        q_ref, k_ref, v_ref = refs[:3]
        rest = refs[3:]
        g_ref = rest.pop(0) if norm else None
        cq_ref, ck_ref = (rest.pop(0), rest.pop(0)) if fox else (None, None)
        o_ref = rest.pop(0)
        qi = pl.program_id(1)
        if norm:
            kn_s, = rest

            @pl.when(qi == 0)
            def _():
                kn_s[...] = _head_norm(k_ref[...], g_ref[1]).astype(BF16)

        def step(n_keys):
            if norm:
                qn = _head_norm(q_ref[...], g_ref[0]).astype(BF16)
                kn = kn_s[0:n_keys, :]
            else:
                qn, kn = q_ref[...].astype(BF16), k_ref[0:n_keys, :].astype(BF16)
            bias = (cq_ref[...] - ck_ref[:, 0:n_keys]) if fox else None
            w, aux = _attn_weights(kind, qn, kn, scale, qi, bq, bias)
            o = _dot(w.astype(BF16), v_ref[0:n_keys, :].astype(BF16), 1, 0)
            if kind != 'sb':
                o = o / aux
            o_ref[...] = o.astype(BF16)

        for qv in range(s_len // bq):
            pl.when(qi == qv)(functools.partial(step, (qv + 1) * bq))

    in_specs = [pl.BlockSpec((bq, dqk), lambda h, i: (i, qcol(h))),
                pl.BlockSpec((s_len, dqk), lambda h, i: (0, kcol(h))),
                pl.BlockSpec((s_len, HEAD_DIM), lambda h, i: (0, vcol(h)))]
    ins = [q, k, v]
    if norm:
        in_specs.append(pl.BlockSpec((2, 1, dqk), lambda h, i: (0, 0, 0)))
        ins.append(gains)
    if fox:
        in_specs += [pl.BlockSpec((None, bq, 1), lambda h, i: (h, i, 0)), pl.BlockSpec((None, 1, s_len), lambda h, i: (h, 0, 0))]
        ins += [cq, ck]
    return pl.pallas_call(
        body, name=name, grid=(n_heads, s_len // bq), in_specs=in_specs,
        out_specs=pl.BlockSpec((bq, HEAD_DIM), lambda h, i: (i, h)),
        out_shape=jax.ShapeDtypeStruct((s_len, n_heads * HEAD_DIM), BF16),
        scratch_shapes=[pltpu.VMEM((s_len, dqk), BF16)] if norm else [],
        compiler_params=_params("parallel", "arbitrary"))(*ins)


def _attn_bwd(kind, q, k, v, o, do, *, name, n_heads, dqk, qcol, kcol, vcol, scale, gains=None, cq=None, ck=None,
              bq=256):
    s_len = q.shape[0]
    bq = _div_block(s_len, bq, 8)
    nq = s_len // bq
    norm, fox = gains is not None, cq is not None

    def body(*refs):
        refs = list(refs)
        q_ref, k_ref, v_ref, o_ref, do_ref = refs[:5]
        rest = refs[5:]
        g_ref = rest.pop(0) if norm else None
        cq_ref, ck_ref = (rest.pop(0), rest.pop(0)) if fox else (None, None)
        dq_ref, dk_ref, dv_ref = rest.pop(0), rest.pop(0), rest.pop(0)
        dg_ref = rest.pop(0) if norm else None
        dcq_ref, dck_ref = (rest.pop(0), rest.pop(0)) if fox else (None, None)
        dk_acc, dv_acc = rest[:2]
        kn_s = rest[2] if norm else None
        h, qi = pl.program_id(0), pl.program_id(1)

        @pl.when(qi == 0)
        def _():
            dk_acc[...] = jnp.zeros_like(dk_acc)
            dv_acc[...] = jnp.zeros_like(dv_acc)
            if fox:
                dck_ref[...] = jnp.zeros_like(dck_ref)
            if norm:
                kn_s[...] = _head_norm(k_ref[...], g_ref[1]).astype(BF16)

        if norm:
            @pl.when((qi == 0) & (h == 0))
            def _():
                dg_ref[...] = jnp.zeros_like(dg_ref)


        def step(n_keys):
            if norm:
                qn = _head_norm(q_ref[...], g_ref[0]).astype(BF16)
                kn = kn_s[0:n_keys, :]
            else:
                qn, kn = q_ref[...].astype(BF16), k_ref[0:n_keys, :].astype(BF16)
            vb = v_ref[0:n_keys, :].astype(BF16)
            dob = do_ref[...].astype(BF16)
            bias = (cq_ref[...] - ck_ref[:, 0:n_keys]) if fox else None
            w, aux = _attn_weights(kind, qn, kn, scale, qi, bq, bias)
            dw = _dot(dob, vb, 1, 1)
            if kind == 'sb':
                strict, log_sig = aux
                g = dw * w
                cc = _lane_scan(g, suffix=False)
                sig = jnp.exp(log_sig)
                ds = jnp.where(strict, g * (1.0 - sig) - cc * sig, 0.0)
                pw = w
            else:
                pw = w / aux
                delta = jnp.sum(do_ref[...].astype(F32) * o_ref[...].astype(F32), axis=-1, keepdims=True)
                ds = pw * (dw - delta)
                if fox:
                    dcq_ref[...] = jnp.sum(ds, axis=1, keepdims=True)
                    dck_ref[:, 0:n_keys] -= jnp.sum(ds, axis=0, keepdims=True)
            dsb = (ds * scale).astype(BF16)
            dqn = _dot(dsb, kn, 1, 0)
            dk_acc[0:n_keys, :] += _dot(dsb, qn, 0, 0)
            dv_acc[0:n_keys, :] += _dot(pw.astype(BF16), dob, 0, 0)
            if norm:
                dq, dgr = _rms_bwd_math(q_ref[...].astype(F32), g_ref[0], dqn, dqk)
                dg_ref[0] += jnp.sum(dgr, axis=0, keepdims=True)
                dq_ref[...] = dq.astype(BF16)
            else:
                dq_ref[...] = dqn.astype(BF16)

        for qv in range(nq):
            pl.when(qi == qv)(functools.partial(step, (qv + 1) * bq))

        @pl.when(qi == nq - 1)
        def _():
            if norm:
                dk, dgr = _rms_bwd_math(k_ref[...].astype(F32), g_ref[1], dk_acc[...], dqk)
                dg_ref[1] += jnp.sum(dgr, axis=0, keepdims=True)
                dk_ref[...] = dk.astype(BF16)
            else:
                dk_ref[...] = dk_acc[...].astype(BF16)
            dv_ref[...] = dv_acc[...].astype(BF16)

    in_specs = [pl.BlockSpec((bq, dqk), lambda h, i: (i, qcol(h))),
                pl.BlockSpec((s_len, dqk), lambda h, i: (0, kcol(h))),
                pl.BlockSpec((s_len, HEAD_DIM), lambda h, i: (0, vcol(h))),
                pl.BlockSpec((bq, HEAD_DIM), lambda h, i: (i, h)),
                pl.BlockSpec((bq, HEAD_DIM), lambda h, i: (i, h))]
    ins = [q, k, v, o, do]
    out_specs = [pl.BlockSpec((bq, dqk), lambda h, i: (i, h)),
                 pl.BlockSpec((s_len, dqk), lambda h, i: (0, h)),
                 pl.BlockSpec((s_len, HEAD_DIM), lambda h, i: (0, h))]
    out_shape = [jax.ShapeDtypeStruct((s_len, n_heads * dqk), BF16), jax.ShapeDtypeStruct((s_len, n_heads * dqk), BF16),
                 jax.ShapeDtypeStruct((s_len, n_heads * HEAD_DIM), BF16)]
    if norm:
        in_specs.append(pl.BlockSpec((2, 1, dqk), lambda h, i: (0, 0, 0)))
        ins.append(gains)
        out_specs.append(pl.BlockSpec((2, 1, dqk), lambda h, i: (0, 0, 0)))
        out_shape.append(jax.ShapeDtypeStruct((2, 1, dqk), F32))
    if fox:
        in_specs += [pl.BlockSpec((None, bq, 1), lambda h, i: (h, i, 0)), pl.BlockSpec((None, 1, s_len), lambda h, i: (h, 0, 0))]
        ins += [cq, ck]
        out_specs += [pl.BlockSpec((None, bq, 1), lambda h, i: (h, i, 0)), pl.BlockSpec((None, 1, s_len), lambda h, i: (h, 0, 0))]
        out_shape += [jax.ShapeDtypeStruct((n_heads, s_len, 1), F32), jax.ShapeDtypeStruct((n_heads, 1, s_len), F32)]
    return pl.pallas_call(
        body, name=name, grid=(n_heads, nq), in_specs=in_specs, out_specs=out_specs, out_shape=out_shape,
        scratch_shapes=[pltpu.VMEM((s_len, dqk), F32), pltpu.VMEM((s_len, HEAD_DIM), F32)]
        + ([pltpu.VMEM((s_len, dqk), BF16)] if norm else []),
        compiler_params=_params("arbitrary", "arbitrary"))(*ins)


def _split3(x):
    hi = x.astype(BF16)
    r1 = x - hi.astype(F32)
    mid = r1.astype(BF16)
    lo = (r1 - mid.astype(F32)).astype(BF16)
    return hi, mid, lo


def _seq_scan(x, *, reverse):
    n = x.shape[0] // LANES
    a, b = _iota((LANES, LANES), 0), _iota((LANES, LANES), 1)
    tri = ((b >= a) if reverse else (b <= a)).astype(BF16)
    outs = [None] * n
    run = jnp.zeros((1, x.shape[1]), F32)
    for blk in (range(n - 1, -1, -1) if reverse else range(n)):
        xb = x[blk * LANES:(blk + 1) * LANES, :]
        hi, mid, lo = _split3(xb)
        outs[blk] = _dot(tri, hi, 1, 0) + _dot(tri, mid, 1, 0) + _dot(tri, lo, 1, 0) + run
        run = run + jnp.sum(xb, axis=0, keepdims=True)
    return jnp.concatenate(outs, axis=0)


def _fgate_fwd(qkvf, b_f, *, fcol, name):
    s_len = qkvf.shape[0]

    def body(f_ref, b_ref, cum_ref):
        z = f_ref[...] + b_ref[...]
        cum_ref[...] = _seq_scan(-_softplus(-z), reverse=False)

    return pl.pallas_call(
        body, name=name, grid=(1,),
        in_specs=[pl.BlockSpec((s_len, LANES), lambda i: (0, fcol)), pl.BlockSpec((1, LANES), lambda i: (0, 0))],
        out_specs=pl.BlockSpec((s_len, LANES), lambda i: (0, 0)), out_shape=jax.ShapeDtypeStruct((s_len, LANES), F32),
        compiler_params=_params("arbitrary"))(qkvf, b_f)


def _fgate_bwd(qkvf, b_f, dcum_a, dcum_b, *, fcol, n_heads, name):
    s_len = qkvf.shape[0]

    def body(f_ref, b_ref, da_ref, db_ref, dz_ref, dbias_ref):
        z = f_ref[...] + b_ref[...]
        dlog = _seq_scan(da_ref[...] + db_ref[...], reverse=True)
        dz = dlog * jnp.exp(-_softplus(z))
        dz = jnp.where(_iota(dz.shape, 1) < n_heads, dz, 0.0)
        dz_ref[...] = dz.astype(BF16)
        dbias_ref[...] = jnp.sum(dz, axis=0, keepdims=True)

    full = pl.BlockSpec((s_len, LANES), lambda i: (0, 0))
    vec = pl.BlockSpec((1, LANES), lambda i: (0, 0))
    return pl.pallas_call(
        body, name=name, grid=(1,),
        in_specs=[pl.BlockSpec((s_len, LANES), lambda i: (0, fcol)), vec, full, full],
        out_specs=[full, vec], out_shape=[jax.ShapeDtypeStruct((s_len, LANES), BF16), jax.ShapeDtypeStruct((1, LANES), F32)],
        compiler_params=_params("arbitrary"))(qkvf, b_f, dcum_a, dcum_b)


def _rope_swap(x):
    half = MLA_ROPE // 2
    lane = _iota(x.shape, 1)
    sw = jnp.where(lane < half, pltpu.roll(x, LANES - half, axis=1), pltpu.roll(x, half, axis=1))
    return jnp.where(lane < MLA_ROPE, sw, 0.0)


def _mla_prep_fwd(qp, kv, c, gq, gk, cos_t, sin_t, *, n_heads, name, bs=512):
    s_len = qp.shape[0]
    bs = _div_block(s_len, bs, 8)
    krope_col = (MLA_Q_RANK + MLA_KV_RANK) // LANES

    def body(qn_ref, qr_ref, kn_ref, kr_ref, gq_ref, gk_ref, cos_ref, sin_ref, qc_ref, kc_ref):
        cos_v, sin_v = cos_ref[...], sin_ref[...]

        def rope(x, g):
            xv = x.astype(F32)
            inv = lax.rsqrt(jnp.sum(xv * xv, axis=-1, keepdims=True) / MLA_ROPE + NORM_EPS)
            y = xv * inv * g
            return y * cos_v + _rope_swap(y) * sin_v

        qc_ref[:, :LANES] = _head_norm(qn_ref[...], gq_ref[0]).astype(BF16)
        qc_ref[:, LANES:] = rope(qr_ref[...], gq_ref[1]).astype(BF16)
        kc_ref[:, :LANES] = _head_norm(kn_ref[...], gk_ref[0]).astype(BF16)
        kc_ref[:, LANES:] = rope(kr_ref[...], gk_ref[1]).astype(BF16)

    blk = lambda f: pl.BlockSpec((bs, LANES), f)
    gspec = pl.BlockSpec((2, 1, LANES), lambda i, h: (0, 0, 0))
    tspec = pl.BlockSpec((bs, LANES), lambda i, h: (i, 0))
    ospec = pl.BlockSpec((bs, 2 * LANES), lambda i, h: (i, h))
    oshape = jax.ShapeDtypeStruct((s_len, n_heads * 2 * LANES), BF16)
    return pl.pallas_call(
        body, name=name, grid=(s_len // bs, n_heads),
        in_specs=[blk(lambda i, h: (i, h)), blk(lambda i, h: (i, n_heads + h)), blk(lambda i, h: (i, 2 * h)),
                  blk(lambda i, h: (i, krope_col)), gspec, gspec, tspec, tspec],
        out_specs=[ospec, ospec], out_shape=[oshape, oshape],
        compiler_params=_params("parallel", "parallel"))(qp, qp, kv, c, gq, gk, cos_t, sin_t)


def _mla_prep_bwd(qp, kv, c, gq, gk, cos_t, sin_t, dqc, dkc, dv, *, n_heads, name, bs=512):
    s_len = qp.shape[0]
    bs = _div_block(s_len, bs, 8)
    krope_col = (MLA_Q_RANK + MLA_KV_RANK) // LANES

    def body(qn_ref, qr_ref, kn_ref, kr_ref, gq_ref, gk_ref, cos_ref, sin_ref, dqc_ref, dkc_ref, dv_ref,
             dqn_ref, dqr_ref, dkv_ref, dkr_ref, dgq_ref, dgk_ref):
        i, h = pl.program_id(0), pl.program_id(1)
        cos_v, sin_v = cos_ref[...], sin_ref[...]

        @pl.when((i == 0) & (h == 0))
        def _():
            dgq_ref[...] = jnp.zeros_like(dgq_ref)
            dgk_ref[...] = jnp.zeros_like(dgk_ref)

        @pl.when(h == 0)
        def _():
            dkr_ref[...] = jnp.zeros_like(dkr_ref)

        def unrope(dy):
            dy = dy.astype(F32)
            return dy * cos_v + _rope_swap(dy * sin_v)

        dqn, dg = _rms_bwd_math(qn_ref[...].astype(F32), gq_ref[0], dqc_ref[:, :LANES].astype(F32), MLA_NOPE)
        dgq_ref[0] += jnp.sum(dg, axis=0, keepdims=True)
        dqn_ref[...] = dqn.astype(BF16)
        dqr, dg = _rms_bwd_math(qr_ref[...].astype(F32), gq_ref[1], unrope(dqc_ref[:, LANES:]), MLA_ROPE)
        dgq_ref[1] += jnp.sum(dg, axis=0, keepdims=True)
        dqr_ref[...] = dqr.astype(BF16)
        dkn, dg = _rms_bwd_math(kn_ref[...].astype(F32), gk_ref[0], dkc_ref[:, :LANES].astype(F32), MLA_NOPE)
        dgk_ref[0] += jnp.sum(dg, axis=0, keepdims=True)
        dkv_ref[:, :LANES] = dkn.astype(BF16)
        dkv_ref[:, LANES:] = dv_ref[...]
        dkr, dg = _rms_bwd_math(kr_ref[...].astype(F32), gk_ref[1], unrope(dkc_ref[:, LANES:]), MLA_ROPE)
        dgk_ref[1] += jnp.sum(dg, axis=0, keepdims=True)
        dkr_ref[...] += dkr

    blk = lambda f: pl.BlockSpec((bs, LANES), f)
    gspec = pl.BlockSpec((2, 1, LANES), lambda i, h: (0, 0, 0))
    tspec = pl.BlockSpec((bs, LANES), lambda i, h: (i, 0))
    cat = pl.BlockSpec((bs, 2 * LANES), lambda i, h: (i, h))
    head = blk(lambda i, h: (i, h))
    hshape = jax.ShapeDtypeStruct((s_len, n_heads * LANES), BF16)
    gshape = jax.ShapeDtypeStruct((2, 1, LANES), F32)
    return pl.pallas_call(
        body, name=name, grid=(s_len // bs, n_heads),
        in_specs=[head, blk(lambda i, h: (i, n_heads + h)), blk(lambda i, h: (i, 2 * h)),
                  blk(lambda i, h: (i, krope_col)), gspec, gspec, tspec, tspec, cat, cat, head],
        out_specs=[head, head, cat, tspec, gspec, gspec],
        out_shape=[hshape, hshape, jax.ShapeDtypeStruct((s_len, n_heads * 2 * LANES), BF16),
                   jax.ShapeDtypeStruct((s_len, LANES), F32), gshape, gshape],
        compiler_params=_params("arbitrary", "arbitrary"))(qp, qp, kv, c, gq, gk, cos_t, sin_t, dqc, dkc, dv)


def _mla_latent_fwd(c, ga, *, name, br=256):
    s_len = c.shape[0]
    br = _div_block(s_len, br, 8)

    def body(c_ref, g_ref, o_ref):
        for part in range(2):
            sl = slice(part * MLA_Q_RANK, (part + 1) * MLA_Q_RANK)
            o_ref[:, sl] = _head_norm(c_ref[:, sl], g_ref[:, sl]).astype(BF16)

    w = MLA_Q_RANK + MLA_KV_RANK
    return pl.pallas_call(
        body, name=name, grid=(s_len // br,),
        in_specs=[pl.BlockSpec((br, w), lambda i: (i, 0)), pl.BlockSpec((1, w), lambda i: (0, 0))],
        out_specs=pl.BlockSpec((br, w), lambda i: (i, 0)), out_shape=jax.ShapeDtypeStruct((s_len, w), BF16),
        compiler_params=_params("parallel"))(c, ga)


def _mla_latent_bwd(c, ga, dcn_q, dcn_kv, dk_rope, *, name, br=256):
    s_len, cw = c.shape
    br = _div_block(s_len, br, 8)
    w = MLA_Q_RANK + MLA_KV_RANK

    def body(c_ref, g_ref, dq_ref, dkv_ref, dkr_ref, dc_ref, dg_ref):
        @pl.when(pl.program_id(0) == 0)
        def _():
            dg_ref[...] = jnp.zeros_like(dg_ref)

        for part, d_ref in enumerate((dq_ref, dkv_ref)):
            sl = slice(part * MLA_Q_RANK, (part + 1) * MLA_Q_RANK)
            dx, dg = _rms_bwd_math(c_ref[:, sl].astype(F32), g_ref[:, sl], d_ref[...].astype(F32), MLA_Q_RANK)
            dc_ref[:, sl] = dx.astype(BF16)
            dg_ref[:, sl] += jnp.sum(dg, axis=0, keepdims=True)
        dc_ref[:, w:] = dkr_ref[...].astype(BF16)

    return pl.pallas_call(
        body, name=name, grid=(s_len // br,),
        in_specs=[pl.BlockSpec((br, w), lambda i: (i, 0)), pl.BlockSpec((1, w), lambda i: (0, 0)),
                  pl.BlockSpec((br, MLA_Q_RANK), lambda i: (i, 0)), pl.BlockSpec((br, MLA_KV_RANK), lambda i: (i, 0)),
                  pl.BlockSpec((br, LANES), lambda i: (i, 0))],
        out_specs=[pl.BlockSpec((br, cw), lambda i: (i, 0)), pl.BlockSpec((1, w), lambda i: (0, 0))],
        out_shape=[jax.ShapeDtypeStruct((s_len, cw), BF16), jax.ShapeDtypeStruct((1, w), F32)],
        compiler_params=_params("arbitrary"))(c, ga, dcn_q, dcn_kv, dk_rope)


_GELU_C = 0.7978845608028654


def _gelu(x):
    return 0.5 * x * (1.0 + jnp.tanh(_GELU_C * (x + 0.044715 * x * x * x)))


def _gelu_grad(x):
    t = jnp.tanh(_GELU_C * (x + 0.044715 * x * x * x))
    return 0.5 * (1.0 + t) + 0.5 * x * (1.0 - t * t) * _GELU_C * (1.0 + 3 * 0.044715 * x * x)


def _sgu_act_fwd(uv, vg, *, name, br=256):
    s_len, w2 = uv.shape
    w = w2 // 2
    br = _div_block(s_len, br, 8)

    def body(uv_ref, g_ref, u_ref, v_ref):
        u_ref[...] = _gelu(uv_ref[:, :w])
        v_ref[...] = _head_norm(_gelu(uv_ref[:, w:]), g_ref[...]).astype(BF16)

    row = lambda c: pl.BlockSpec((br, c), lambda i: (i, 0))
    return pl.pallas_call(
        body, name=name, grid=(s_len // br,), in_specs=[row(w2), pl.BlockSpec((1, w), lambda i: (0, 0))],
        out_specs=[row(w), row(w)], out_shape=[jax.ShapeDtypeStruct((s_len, w), F32), jax.ShapeDtypeStruct((s_len, w), BF16)],
        compiler_params=_params("parallel"))(uv, vg)


def _sgu_act_bwd(uv, vg, du, dvn, *, name, br=256):
    s_len, w2 = uv.shape
    w = w2 // 2
    br = _div_block(s_len, br, 8)

    def body(uv_ref, g_ref, du_ref, dvn_ref, duv_ref, dg_ref):
        @pl.when(pl.program_id(0) == 0)
        def _():
            dg_ref[...] = jnp.zeros_like(dg_ref)

        up, vp = uv_ref[:, :w], uv_ref[:, w:]
        duv_ref[:, :w] = (du_ref[...] * _gelu_grad(up)).astype(BF16)
        dva, dg = _rms_bwd_math(_gelu(vp), g_ref[...], dvn_ref[...], w)
        dg_ref[...] += jnp.sum(dg, axis=0, keepdims=True)
        duv_ref[:, w:] = (dva * _gelu_grad(vp)).astype(BF16)

    row = lambda c: pl.BlockSpec((br, c), lambda i: (i, 0))
    vec = pl.BlockSpec((1, w), lambda i: (0, 0))
    return pl.pallas_call(
        body, name=name, grid=(s_len // br,), in_specs=[row(w2), vec, row(w), row(w)], out_specs=[row(w2), vec],
        out_shape=[jax.ShapeDtypeStruct((s_len, w2), BF16), jax.ShapeDtypeStruct((1, w), F32)],
        compiler_params=_params("arbitrary"))(uv, vg, du, dvn)


def _tril_weights(ws_ref):
    t, s = _iota((SGU_CHUNK, SGU_CHUNK), 0), _iota((SGU_CHUNK, SGU_CHUNK), 1)
    keep = s <= t
    return jnp.where(keep, ws_ref[...], 0.0), keep


def _sgu_mix_fwd(u, vn, w_s, b_s, *, name):
    s_len, w = u.shape
    nc = s_len // SGU_CHUNK

    def body(u_ref, v_ref, ws_ref, b_ref, o_ref):
        wm = _tril_weights(ws_ref)[0].astype(BF16)
        for n in range(nc):
            rows = slice(n * SGU_CHUNK, (n + 1) * SGU_CHUNK)
            mixed = _dot(wm, v_ref[rows, :], 1, 0) + b_ref[...]
            o_ref[rows, :] = (u_ref[rows, :] * mixed).astype(BF16)

    col = pl.BlockSpec((s_len, LANES), lambda g: (0, g))
    return pl.pallas_call(
        body, name=name, grid=(w // LANES,),
        in_specs=[col, col, pl.BlockSpec((None, SGU_CHUNK, SGU_CHUNK), lambda g: (g, 0, 0)),
                  pl.BlockSpec((None, SGU_CHUNK, 1), lambda g: (g, 0, 0))],
        out_specs=col, out_shape=jax.ShapeDtypeStruct((s_len, w), BF16),
        compiler_params=_params("parallel"))(u, vn, w_s, b_s)


def _sgu_mix_bwd(u, vn, w_s, b_s, dgated, *, name):
    s_len, w = u.shape
    nc = s_len // SGU_CHUNK

    def body(u_ref, v_ref, ws_ref, b_ref, dg_ref, du_ref, dv_ref, dws_ref, dbs_ref):
        wf, keep = _tril_weights(ws_ref)
        wm = wf.astype(BF16)
        wmt = wf.T.astype(BF16)
        dws = jnp.zeros((SGU_CHUNK, SGU_CHUNK), F32)
        dbs = jnp.zeros((SGU_CHUNK, 1), F32)
        for n in range(nc):
            rows = slice(n * SGU_CHUNK, (n + 1) * SGU_CHUNK)
            vb = v_ref[rows, :]
            dgv = dg_ref[rows, :].astype(F32)
            mixed = _dot(wm, vb, 1, 0) + b_ref[...]
            du_ref[rows, :] = dgv * mixed
            dm = dgv * u_ref[rows, :]
            dmb = dm.astype(BF16)
            dws = dws + _dot(dmb, vb, 1, 1)
            dbs = dbs + jnp.sum(dm, axis=1, keepdims=True)
            dv_ref[rows, :] = _dot(wmt, dmb, 1, 0)
        dws_ref[...] = jnp.where(keep, dws, 0.0)
        dbs_ref[...] = dbs

    col = pl.BlockSpec((s_len, LANES), lambda g: (0, g))
    wspec = pl.BlockSpec((None, SGU_CHUNK, SGU_CHUNK), lambda g: (g, 0, 0))
    bspec = pl.BlockSpec((None, SGU_CHUNK, 1), lambda g: (g, 0, 0))
    return pl.pallas_call(
        body, name=name, grid=(w // LANES,), in_specs=[col, col, wspec, bspec, col],
        out_specs=[col, col, wspec, bspec],
        out_shape=[jax.ShapeDtypeStruct((s_len, w), F32), jax.ShapeDtypeStruct((s_len, w), F32),
                   jax.ShapeDtypeStruct(w_s.shape, F32), jax.ShapeDtypeStruct(b_s.shape, F32)],
        compiler_params=_params("parallel"))(u, vn, w_s, b_s, dgated)


def _shift_down(x, k):
    if k == 0:
        return x
    return jnp.where(_iota(x.shape, 0) >= k, pltpu.roll(x, k, axis=0), 0.0)


def _shift_up(x, k):
    if k == 0:
        return x
    n = x.shape[0]
    return jnp.where(_iota(x.shape, 0) < n - k, pltpu.roll(x, n - k, axis=0), 0.0)


def _conv(u, w_ref, b_ref):
    return b_ref[...] + w_ref[0:1, :] * _shift_down(u, 2) + w_ref[1:2, :] * _shift_down(u, 1) + w_ref[2:3, :] * u


def _sigmoid(x):
    return 0.5 * jnp.tanh(0.5 * x) + 0.5


def _glu_fwd(up, cw, cb, *, name, bc=256):
    s_len, f2 = up.shape
    f = f2 // 2
    bc = _div_block(f, bc)
    nf = f // bc

    def body(ug_ref, uv_ref, wg_ref, wv_ref, bg_ref, bv_ref, o_ref):
        yg = _conv(ug_ref[...], wg_ref, bg_ref)
        yv = _conv(uv_ref[...], wv_ref, bv_ref)
        o_ref[...] = (yg * _sigmoid(yg) * yv).astype(BF16)

    big = lambda off: pl.BlockSpec((s_len, bc), lambda j: (0, j + off))
    wsp = lambda off: pl.BlockSpec((3, bc), lambda j: (0, j + off))
    bsp = lambda off: pl.BlockSpec((1, bc), lambda j: (0, j + off))
    return pl.pallas_call(
        body, name=name, grid=(nf,), in_specs=[big(0), big(nf), wsp(0), wsp(nf), bsp(0), bsp(nf)],
        out_specs=pl.BlockSpec((s_len, bc), lambda j: (0, j)), out_shape=jax.ShapeDtypeStruct((s_len, f), BF16),
        compiler_params=_params("parallel"))(up, up, cw, cw, cb, cb)


def _glu_bwd(up, cw, cb, dact, *, name, bc=256):
    s_len, f2 = up.shape
    f = f2 // 2
    bc = _div_block(f, bc)
    nf = f // bc

    def body(ug_ref, uv_ref, wg_ref, wv_ref, bg_ref, bv_ref, da_ref, du_ref, dw_ref, db_ref):
        win = GLU_ROWS + 2 * GLU_HALO
        acc = [[jnp.zeros((1, bc), F32) for _ in range(4)] for _ in range(2)]
        for r0 in range(0, s_len, GLU_ROWS):
            ws = min(max(r0 - GLU_HALO, 0), s_len - win)
            keep = slice(r0 - ws, r0 - ws + GLU_ROWS)
            grow = ws + _iota((win, bc), 0)

            def down(x, k):
                return jnp.where(grow >= k, pltpu.roll(x, k, axis=0), 0.0)

            def up_(x, k):
                return jnp.where(grow < s_len - k, pltpu.roll(x, win - k, axis=0), 0.0)

            ug, uv = ug_ref[ws:ws + win, :], uv_ref[ws:ws + win, :]
            yg = bg_ref[...] + wg_ref[0:1, :] * down(ug, 2) + wg_ref[1:2, :] * down(ug, 1) + wg_ref[2:3, :] * ug
            yv = bv_ref[...] + wv_ref[0:1, :] * down(uv, 2) + wv_ref[1:2, :] * down(uv, 1) + wv_ref[2:3, :] * uv
            da = da_ref[ws:ws + win, :].astype(F32)
            sg = _sigmoid(yg)
            planes = ((da * yv * (sg * (1.0 + yg * (1.0 - sg))), ug, wg_ref), (da * (yg * sg), uv, wv_ref))
            for plane, (dy, u, w_ref) in enumerate(planes):
                dy1, dy2 = up_(dy, 1), up_(dy, 2)
                for slot, term in enumerate((dy, dy2 * u, dy1 * u, dy * u)):
                    acc[plane][slot] = acc[plane][slot] + jnp.sum(term[keep], axis=0, keepdims=True)
                du = w_ref[2:3, :] * dy + w_ref[1:2, :] * dy1 + w_ref[0:1, :] * dy2
                du_ref[plane, r0:r0 + GLU_ROWS, :] = du[keep].astype(BF16)
        for plane in range(2):
            db_ref[plane] = acc[plane][0]
            for k in range(3):
                dw_ref[plane, k:k + 1, :] = acc[plane][1 + k]

    big = lambda off: pl.BlockSpec((s_len, bc), lambda j: (0, j + off))
    wsp = lambda off: pl.BlockSpec((3, bc), lambda j: (0, j + off))
    bsp = lambda off: pl.BlockSpec((1, bc), lambda j: (0, j + off))
    planes = lambda r: pl.BlockSpec((2, r, bc), lambda j: (0, 0, j))
    return pl.pallas_call(
        body, name=name, grid=(nf,),
        in_specs=[big(0), big(nf), wsp(0), wsp(nf), bsp(0), bsp(nf), pl.BlockSpec((s_len, bc), lambda j: (0, j))],
        out_specs=[planes(s_len), planes(3), planes(1)],
        out_shape=[jax.ShapeDtypeStruct((2, s_len, f), BF16), jax.ShapeDtypeStruct((2, 3, f), F32),
                   jax.ShapeDtypeStruct((2, 1, f), F32)],
        compiler_params=_params("parallel"))(up, up, cw, cw, cb, cb, dact)


def _as2d(a):
    return a.reshape(-1, a.shape[-1]) if a.ndim >= 2 else a.reshape(1, -1)


def _adamw(w, g, m, v, *, name, after=None, target_bytes=1 << 20):
    shape = w.shape
    w2, m2, v2 = _as2d(w), _as2d(m), _as2d(v)
    g2 = g.reshape(w2.shape)
    r, c = w2.shape
    br = r if r * c * 4 <= target_bytes else _div_block(r, max(8, target_bytes // (4 * c) // 8 * 8), 8)
    c1 = 1.0 - ADAM_B1 ** ADAM_STEP
    c2 = 1.0 - ADAM_B2 ** ADAM_STEP

    def body(w_ref, g_ref, m_ref, v_ref, *rest):
        d_ref, nm_ref, nv_ref = rest[-3:]
        gv = g_ref[...]
        nm = ADAM_B1 * m_ref[...] + (1.0 - ADAM_B1) * gv
        nv = ADAM_B2 * v_ref[...] + (1.0 - ADAM_B2) * (gv * gv)
        nm_ref[...] = nm
        nv_ref[...] = nv
        d_ref[...] = -ADAM_LR * ((nm / c1) / (jnp.sqrt(nv / c2) + ADAM_EPS) + ADAM_WD * w_ref[...])

    spec = pl.BlockSpec((br, c), lambda i: (i, 0))
    sds = jax.ShapeDtypeStruct((r, c), F32)
    d, nm, nv = pl.pallas_call(
        body, name=name, grid=(r // br,), in_specs=[spec] * 4 + _after_spec(after), out_specs=[spec] * 3,
        out_shape=[sds] * 3, compiler_params=_params("parallel"))(w2, g2, m2, v2, *_after_arg(after))
    return d.reshape(shape), nm.reshape(shape), nv.reshape(shape)


def _add_halves(g, recv, place, *, name, target_bytes=SHARD_BLOCK_BYTES):
    _, _, r, c = g.shape
    br = _div_block(r, max(16, target_bytes // (2 * c) // 16 * 16), 16)

    def body(x_ref, y_ref, c_ref, g_ref, r_ref, o_ref):
        o_ref[...] = (g_ref[...].astype(F32) + r_ref[...].astype(F32)).astype(BF16)

    return pl.pallas_call(
        body, name=name,
        grid_spec=pltpu.PrefetchScalarGridSpec(
            num_scalar_prefetch=3, grid=(N_CHIPS, r // br),
            in_specs=[pl.BlockSpec((None, None, br, c), lambda s, i, xr, yr, cr: (s, cr[0], i, 0)),
                      pl.BlockSpec((None, br, c), lambda s, i, xr, yr, cr: (s, i, 0))],
            out_specs=pl.BlockSpec((None, br, c), lambda s, i, xr, yr, cr: (s, i, 0))),
        out_shape=jax.ShapeDtypeStruct((N_CHIPS, r, c), BF16),
        compiler_params=_params("parallel", "parallel"))(*place, g, recv)


def _sum_chips(p, landed, place, *, name, target_bytes=SHARD_BLOCK_BYTES):
    _, r, c = p.shape
    br = _div_block(r, max(16, target_bytes // (4 * c) // 16 * 16), 16)

    def body(x_ref, y_ref, c_ref, p_ref, l1_ref, l2_ref, l3_ref, o_ref):
        o_ref[...] = ((p_ref[...].astype(F32) + l1_ref[...].astype(F32)) + l2_ref[...].astype(F32)) + l3_ref[...].astype(F32)

    slot = lambda k: pl.BlockSpec((None, br, c), lambda i, xr, yr, cr: ((2 * xr[0] + yr[0] + k) % N_CHIPS, i, 0))
    return pl.pallas_call(
        body, name=name,
        grid_spec=pltpu.PrefetchScalarGridSpec(
            num_scalar_prefetch=3, grid=(r // br,), in_specs=[slot(0), slot(1), slot(2), slot(3)],
            out_specs=pl.BlockSpec((None, br, c), lambda i, xr, yr, cr: (cr[0], i, 0))),
        out_shape=jax.ShapeDtypeStruct((2, r, c), F32),
        compiler_params=_params("parallel"))(*place, p, landed, landed, landed)


def _place_shard(w, place, *, dtype, name, layer=None, after=None, target_bytes=SHARD_BLOCK_BYTES):
    r, c = w.shape[-2:]
    hr = r // 2
    mult = 16 if dtype == BF16 else 8
    br = _div_block(hr, max(mult, target_bytes // (4 * c) // mult * mult), mult)
    nb = hr // br

    def body(x_ref, y_ref, c_ref, w_ref, *rest):
        rest[-1][...] = w_ref[...].astype(dtype)

    if layer is None:
        w_spec = pl.BlockSpec((br, c), lambda h, i, xr, yr, cr: (h * nb + i, 0))
    else:
        w_spec = pl.BlockSpec((None, br, c), lambda h, i, xr, yr, cr: (layer, h * nb + i, 0))
    return pl.pallas_call(
        body, name=name,
        grid_spec=pltpu.PrefetchScalarGridSpec(
            num_scalar_prefetch=3, grid=(2, nb), in_specs=[w_spec] + _after_spec(after),
            out_specs=pl.BlockSpec((None, None, br, c), lambda h, i, xr, yr, cr: (2 * xr[0] + yr[0], h, i, 0))),
        out_shape=jax.ShapeDtypeStruct((N_CHIPS, 2, hr, c), dtype),
        compiler_params=_params("parallel", "parallel"))(*place, w, *_after_arg(after))


def _sum_devices(x, *, name):
    n, r, c = x.shape
    br = _div_block(r, 512, 8)

    def body(x_ref, o_ref):
        acc = x_ref[0]
        for s in range(1, n):
            acc = acc + x_ref[s]
        o_ref[...] = acc

    return pl.pallas_call(
        body, name=name, grid=(r // br,), in_specs=[pl.BlockSpec((n, br, c), lambda i: (0, i, 0))],
        out_specs=pl.BlockSpec((br, c), lambda i: (i, 0)), out_shape=jax.ShapeDtypeStruct((r, c), F32),
        compiler_params=_params("parallel"))(x)


_ANY = pl.BlockSpec(memory_space=pl.ANY)


def _place():
    x, y, c = lax.axis_index("x"), lax.axis_index("y"), lax.axis_index("c")
    other_chips = [(1 - x, y), (x, 1 - y), (1 - x, 1 - y)]
    return x, y, c, other_chips


_HBM = pl.BlockSpec(memory_space=pltpu.HBM)
_SEM = pl.BlockSpec(memory_space=pltpu.SEMAPHORE)
_EFFECT = pltpu.SideEffectType.DATAFLOW_SIDE_EFFECTING


def _in_hbm(a):
    return pltpu.with_memory_space_constraint(a, pltpu.HBM)


def _token_spec():
    return pl.BlockSpec(memory_space=pltpu.VMEM), jax.ShapeDtypeStruct((8, LANES), F32)


def _gather_ici_start(bufs, after, *, name):
    n = len(bufs)

    def body(*refs):
        b_refs = refs[:n]
        send_sems, recv_sems = refs[n + 1], refs[n + 2]
        token = refs[-1]
        x, y, c, chips = _place()
        me = 2 * x + y
        for i in range(n):
            for j, (px, py) in enumerate(chips):
                pltpu.make_async_remote_copy(src_ref=b_refs[i].at[me, c], dst_ref=b_refs[i].at[me, c],
                                             send_sem=send_sems.at[3 * i + j], recv_sem=recv_sems.at[3 * i + j],
                                             device_id=(px, py, c), device_id_type=MESH).start()
        token[...] = jnp.zeros_like(token)

    tspec, tshape = _token_spec()
    outs = pl.pallas_call(
        body, name=name, in_specs=[_HBM] * n + [_ANY], out_specs=(_SEM, _SEM, *[_HBM] * n, tspec),
        out_shape=(pltpu.SemaphoreType.DMA((3 * n,)), pltpu.SemaphoreType.DMA((3 * n,)),
                   *[pltpu.HBM(a.shape, a.dtype) for a in bufs], tshape),
        input_output_aliases={i: 2 + i for i in range(n)},
        compiler_params=pltpu.CompilerParams(has_side_effects=_EFFECT),
    )(*[_in_hbm(a) for a in bufs], after)
    return outs[0], outs[1], list(outs[2:2 + n]), outs[-1]


def _gather_ici_wait(send_sems, recv_sems, bufs, after, *, name):
    n = len(bufs)

    def body(*refs):
        b_refs = refs[:n]
        send_sems, recv_sems = refs[n], refs[n + 1]
        x, y, c, chips = _place()
        me = 2 * x + y
        for i in range(n):
            for j, (px, py) in enumerate(chips):
                cp = pltpu.make_async_remote_copy(src_ref=b_refs[i].at[me, c], dst_ref=b_refs[i].at[2 * px + py, c],
                                                  send_sem=send_sems.at[3 * i + j], recv_sem=recv_sems.at[3 * i + j],
                                                  device_id=(px, py, c), device_id_type=MESH)
                cp.wait_send()
                cp.wait_recv()

    outs = pl.pallas_call(
        body, name=name, in_specs=[_HBM] * n + [_SEM, _SEM, _ANY], out_specs=[_HBM] * n,
        out_shape=[pltpu.HBM(a.shape, a.dtype) for a in bufs], input_output_aliases={i: i for i in range(n)},
        compiler_params=pltpu.CompilerParams(has_side_effects=_EFFECT),
    )(*bufs, send_sems, recv_sems, after)
    return list(outs)


def _gather_d2d(bufs, *, name):
    n = len(bufs)

    def body(*refs):
        b_refs = refs[n:2 * n]
        send_sems, recv_sems = refs[2 * n:]
        x, y, c, chips = _place()
        sends = []
        for i in range(n):
            for j, (px, py) in enumerate(chips):
                mine = b_refs[i].at[2 * px + py, c]
                cp = pltpu.make_async_remote_copy(src_ref=mine, dst_ref=mine, send_sem=send_sems.at[3 * i + j],
                                                  recv_sem=recv_sems.at[3 * i + j], device_id=(x, y, 1 - c),
                                                  device_id_type=MESH)
                cp.start()
                sends.append((cp, i, j, px, py))
        for cp, i, j, px, py in sends:
            theirs = b_refs[i].at[2 * px + py, 1 - c]
            pltpu.make_async_remote_copy(src_ref=theirs, dst_ref=theirs, send_sem=send_sems.at[3 * i + j],
                                         recv_sem=recv_sems.at[3 * i + j], device_id=(x, y, 1 - c),
                                         device_id_type=MESH).wait_recv()
            cp.wait_send()

    return pl.pallas_call(
        body, name=name, in_specs=[_ANY] * n, out_specs=[_ANY] * n, input_output_aliases={i: i for i in range(n)},
        out_shape=[jax.ShapeDtypeStruct(a.shape, a.dtype) for a in bufs],
        scratch_shapes=[pltpu.SemaphoreType.DMA((3 * n,)), pltpu.SemaphoreType.DMA((3 * n,))],
    )(*bufs)


def _gather_d2d_start(bufs, after, *, name):
    n = len(bufs)

    def body(*refs):
        b_refs = refs[:n]
        send_sems, recv_sems = refs[n + 1], refs[n + 2]
        token = refs[-1]
        x, y, c, chips = _place()
        for i in range(n):
            for j, (px, py) in enumerate(chips):
                mine = b_refs[i].at[2 * px + py, c]
                pltpu.make_async_remote_copy(src_ref=mine, dst_ref=mine, send_sem=send_sems.at[3 * i + j],
                                             recv_sem=recv_sems.at[3 * i + j], device_id=(x, y, 1 - c),
                                             device_id_type=MESH).start()
        token[...] = jnp.zeros_like(token)

    tspec, tshape = _token_spec()
    outs = pl.pallas_call(
        body, name=name, in_specs=[_HBM] * n + [_ANY], out_specs=(_SEM, _SEM, *[_HBM] * n, tspec),
        out_shape=(pltpu.SemaphoreType.DMA((3 * n,)), pltpu.SemaphoreType.DMA((3 * n,)),
                   *[pltpu.HBM(a.shape, a.dtype) for a in bufs], tshape),
        input_output_aliases={i: 2 + i for i in range(n)},
        compiler_params=pltpu.CompilerParams(has_side_effects=_EFFECT),
    )(*[_in_hbm(a) for a in bufs], after)
    return outs[0], outs[1], list(outs[2:2 + n]), outs[-1]


def _gather_d2d_wait(send_sems, recv_sems, bufs, after, *, name):
    n = len(bufs)

    def body(*refs):
        b_refs = refs[:n]
        send_sems, recv_sems = refs[n], refs[n + 1]
        x, y, c, chips = _place()
        for i in range(n):
            for j, (px, py) in enumerate(chips):
                cp = pltpu.make_async_remote_copy(src_ref=b_refs[i].at[2 * px + py, c],
                                                  dst_ref=b_refs[i].at[2 * px + py, 1 - c],
                                                  send_sem=send_sems.at[3 * i + j], recv_sem=recv_sems.at[3 * i + j],
                                                  device_id=(x, y, 1 - c), device_id_type=MESH)
                cp.wait_send()
                cp.wait_recv()

    outs = pl.pallas_call(
        body, name=name, in_specs=[_HBM] * n + [_SEM, _SEM, _ANY], out_specs=[_HBM] * n,
        out_shape=[pltpu.HBM(a.shape, a.dtype) for a in bufs], input_output_aliases={i: i for i in range(n)},
        compiler_params=pltpu.CompilerParams(has_side_effects=_EFFECT),
    )(*bufs, send_sems, recv_sems, after)
    return list(outs)


def _sibling_halves_start(gs, after, *, name):
    n = len(gs)

    def body(*refs):
        g_refs, r_refs = refs[:n], refs[n:2 * n]
        send_sems, recv_sems = refs[2 * n + 1], refs[2 * n + 2]
        token = refs[-1]
        x, y, c, _ = _place()
        for i in range(n):
            for s in range(N_CHIPS):
                k = i * N_CHIPS + s
                pltpu.make_async_remote_copy(src_ref=g_refs[i].at[s, 1 - c], dst_ref=r_refs[i].at[s],
                                             send_sem=send_sems.at[k], recv_sem=recv_sems.at[k],
                                             device_id=(x, y, 1 - c), device_id_type=MESH).start()
        token[...] = jnp.zeros_like(token)

    tspec, tshape = _token_spec()
    rshapes = [(N_CHIPS,) + g.shape[2:] for g in gs]
    recvs = [_in_hbm(lax.empty(sh, g.dtype)) for sh, g in zip(rshapes, gs)]
    outs = pl.pallas_call(
        body, name=name, in_specs=[_HBM] * (2 * n) + [_ANY], out_specs=(_SEM, _SEM, *[_HBM] * (2 * n), tspec),
        out_shape=(pltpu.SemaphoreType.DMA((N_CHIPS * n,)), pltpu.SemaphoreType.DMA((N_CHIPS * n,)),
                   *[pltpu.HBM(g.shape, g.dtype) for g in gs], *[pltpu.HBM(sh, g.dtype) for sh, g in zip(rshapes, gs)],
                   tshape),
        input_output_aliases={i: 2 + i for i in range(2 * n)},
        compiler_params=pltpu.CompilerParams(has_side_effects=_EFFECT),
    )(*[_in_hbm(g) for g in gs], *recvs, after)
    return outs[0], outs[1], list(outs[2:2 + n]), list(outs[2 + n:2 + 2 * n]), outs[-1]


def _sibling_halves_wait(send_sems, recv_sems, gs, recvs, after, *, name):
    n = len(gs)

    def body(*refs):
        g_refs, r_refs = refs[:n], refs[n:2 * n]
        send_sems, recv_sems = refs[2 * n], refs[2 * n + 1]
        x, y, c, _ = _place()
        for i in range(n):
            for s in range(N_CHIPS):
                k = i * N_CHIPS + s
                cp = pltpu.make_async_remote_copy(src_ref=g_refs[i].at[s, 1 - c], dst_ref=r_refs[i].at[s],
                                                  send_sem=send_sems.at[k], recv_sem=recv_sems.at[k],
                                                  device_id=(x, y, 1 - c), device_id_type=MESH)
                cp.wait_send()
                cp.wait_recv()

    outs = pl.pallas_call(
        body, name=name, in_specs=[_HBM] * (2 * n) + [_SEM, _SEM, _ANY], out_specs=[_HBM] * (2 * n),
        out_shape=[pltpu.HBM(a.shape, a.dtype) for a in list(gs) + list(recvs)],
        input_output_aliases={i: i for i in range(2 * n)},
        compiler_params=pltpu.CompilerParams(has_side_effects=_EFFECT),
    )(*gs, *recvs, send_sems, recv_sems, after)
    return list(outs[:n]), list(outs[n:])


def _chip_scatter_start(ps, after, *, name):
    n = len(ps)

    def body(*refs):
        p_refs, l_refs = refs[:n], refs[n:2 * n]
        send_sems, recv_sems = refs[2 * n + 1], refs[2 * n + 2]
        token = refs[-1]
        x, y, c, chips = _place()
        me = 2 * x + y
        for i in range(n):
            for j, (px, py) in enumerate(chips):
                pltpu.make_async_remote_copy(src_ref=p_refs[i].at[2 * px + py], dst_ref=l_refs[i].at[me],
                                             send_sem=send_sems.at[3 * i + j], recv_sem=recv_sems.at[3 * i + j],
                                             device_id=(px, py, c), device_id_type=MESH).start()
        token[...] = jnp.zeros_like(token)

    tspec, tshape = _token_spec()
    lands = [_in_hbm(lax.empty(p.shape, p.dtype)) for p in ps]
    outs = pl.pallas_call(
        body, name=name, in_specs=[_HBM] * (2 * n) + [_ANY], out_specs=(_SEM, _SEM, *[_HBM] * (2 * n), tspec),
        out_shape=(pltpu.SemaphoreType.DMA((3 * n,)), pltpu.SemaphoreType.DMA((3 * n,)),
                   *[pltpu.HBM(p.shape, p.dtype) for p in ps], *[pltpu.HBM(p.shape, p.dtype) for p in ps], tshape),
        input_output_aliases={i: 2 + i for i in range(2 * n)},
        compiler_params=pltpu.CompilerParams(has_side_effects=_EFFECT),
    )(*[_in_hbm(p) for p in ps], *lands, after)
    return outs[0], outs[1], list(outs[2:2 + n]), list(outs[2 + n:2 + 2 * n]), outs[-1]


def _chip_scatter_wait(send_sems, recv_sems, ps, lands, after, *, name):
    n = len(ps)

    def body(*refs):
        p_refs, l_refs = refs[:n], refs[n:2 * n]
        send_sems, recv_sems = refs[2 * n], refs[2 * n + 1]
        x, y, c, chips = _place()
        for i in range(n):
            for j, (px, py) in enumerate(chips):
                cp = pltpu.make_async_remote_copy(src_ref=p_refs[i].at[2 * px + py], dst_ref=l_refs[i].at[2 * px + py],
                                                  send_sem=send_sems.at[3 * i + j], recv_sem=recv_sems.at[3 * i + j],
                                                  device_id=(px, py, c), device_id_type=MESH)
                cp.wait_send()
                cp.wait_recv()

    outs = pl.pallas_call(
        body, name=name, in_specs=[_HBM] * (2 * n) + [_SEM, _SEM, _ANY], out_specs=[_HBM] * (2 * n),
        out_shape=[pltpu.HBM(p.shape, p.dtype) for p in ps] * 2, input_output_aliases={i: i for i in range(2 * n)},
        compiler_params=pltpu.CompilerParams(has_side_effects=_EFFECT),
    )(*ps, *lands, send_sems, recv_sems, after)
    return list(outs[:n]), list(outs[n:])


def _sibling_share_start(bufs, after, *, name):
    n = len(bufs)

    def body(*refs):
        b_refs = refs[:n]
        send_sems, recv_sems = refs[n + 1], refs[n + 2]
        token = refs[-1]
        x, y, c, _ = _place()
        for i in range(n):
            pltpu.make_async_remote_copy(src_ref=b_refs[i].at[c], dst_ref=b_refs[i].at[c], send_sem=send_sems.at[i],
                                         recv_sem=recv_sems.at[i], device_id=(x, y, 1 - c), device_id_type=MESH).start()
        token[...] = jnp.zeros_like(token)

    tspec, tshape = _token_spec()
    outs = pl.pallas_call(
        body, name=name, in_specs=[_HBM] * n + [_ANY], out_specs=(_SEM, _SEM, *[_HBM] * n, tspec),
        out_shape=(pltpu.SemaphoreType.DMA((n,)), pltpu.SemaphoreType.DMA((n,)),
                   *[pltpu.HBM(a.shape, a.dtype) for a in bufs], tshape),
        input_output_aliases={i: 2 + i for i in range(n)},
        compiler_params=pltpu.CompilerParams(has_side_effects=_EFFECT),
    )(*[_in_hbm(a) for a in bufs], after)
    return outs[0], outs[1], list(outs[2:2 + n]), outs[-1]


def _sibling_share_wait(send_sems, recv_sems, bufs, after, *, name):
    n = len(bufs)

    def body(*refs):
        b_refs = refs[:n]
        send_sems, recv_sems = refs[n], refs[n + 1]
        x, y, c, _ = _place()
        for i in range(n):
            cp = pltpu.make_async_remote_copy(src_ref=b_refs[i].at[c], dst_ref=b_refs[i].at[1 - c],
                                              send_sem=send_sems.at[i], recv_sem=recv_sems.at[i],
                                              device_id=(x, y, 1 - c), device_id_type=MESH)
            cp.wait_send()
            cp.wait_recv()

    outs = pl.pallas_call(
        body, name=name, in_specs=[_HBM] * n + [_SEM, _SEM, _ANY], out_specs=[_HBM] * n,
        out_shape=[pltpu.HBM(a.shape, a.dtype) for a in bufs], input_output_aliases={i: i for i in range(n)},
        compiler_params=pltpu.CompilerParams(has_side_effects=_EFFECT),
    )(*bufs, send_sems, recv_sems, after)
    return list(outs)


def _broadcast_all(v, after, *, name):
    def body(v_ref, after_ref, o_ref, send_sems, recv_sems, local_sem):
        x, y, c, _ = _place()
        me = 4 * x + 2 * y + c
        loc = pltpu.make_async_copy(v_ref, o_ref.at[me], local_sem)
        loc.start()
        copies = []
        for k in range(1, 8):
            dx, dy, dc = (k >> 2) & 1, (k >> 1) & 1, k & 1
            to = (1 - x if dx else x, 1 - y if dy else y, 1 - c if dc else c)
            cp = pltpu.make_async_remote_copy(src_ref=v_ref, dst_ref=o_ref.at[me], send_sem=send_sems.at[k - 1],
                                              recv_sem=recv_sems.at[k - 1], device_id=to, device_id_type=MESH)
            cp.start()
            copies.append((cp, k, to))
        for cp, k, to in copies:
            cp.wait_send()
            theirs = o_ref.at[4 * to[0] + 2 * to[1] + to[2]]
            pltpu.make_async_remote_copy(src_ref=theirs, dst_ref=theirs, send_sem=send_sems.at[k - 1],
                                         recv_sem=recv_sems.at[k - 1], device_id=to, device_id_type=MESH).wait_recv()
        loc.wait()

    return pl.pallas_call(
        body, name=name, in_specs=[_ANY, _ANY], out_specs=_ANY,
        out_shape=jax.ShapeDtypeStruct((8,) + v.shape, v.dtype),
        scratch_shapes=[pltpu.SemaphoreType.DMA((7,)), pltpu.SemaphoreType.DMA((7,)), pltpu.SemaphoreType.DMA(())],
    )(v, after)


def _gather_place(shards, place, *, name, after=None):
    names = list(shards)
    bufs, shapes = [], []
    for k in names:
        w, layer = shards[k] if isinstance(shards[k], tuple) else (shards[k], None)
        bufs.append(_place_shard(w, place, dtype=F32 if k == 'small' else BF16, layer=layer, after=after,
                                 name=f"{name}_place_{k}"))
        after = bufs[-1] if after is not None else None
        shapes.append(w.shape[-2:])
    return names, shapes, bufs


def _gather_begin(placed, after, *, name):
    names, shapes, bufs = placed
    send_sems, recv_sems, bufs, token = _gather_ici_start(bufs, after, name=name + "_ici_start")
    return (names, shapes, send_sems, recv_sems, bufs), token


def _gather_end(state, after, *, name):
    names, shapes, send_sems, recv_sems, bufs = state
    bufs = _gather_ici_wait(send_sems, recv_sems, bufs, after, name=name + "_ici_wait")
    bufs = _gather_d2d(bufs, name=name + "_d2d")
    return {k: o.reshape((N_CHIPS,) + sh) for k, o, sh in zip(names, bufs, shapes)}


def _gather_forward(state, after, *, name):
    names, shapes, send_sems, recv_sems, bufs = state
    bufs = _gather_ici_wait(send_sems, recv_sems, bufs, after, name=name + "_ici_wait")
    send_sems, recv_sems, bufs, token = _gather_d2d_start(bufs, after, name=name + "_d2d_start")
    return (names, shapes, send_sems, recv_sems, bufs), token


def _gather_finish(state, after, *, name):
    names, shapes, send_sems, recv_sems, bufs = state
    bufs = _gather_d2d_wait(send_sems, recv_sems, bufs, after, name=name + "_d2d_wait")
    return {k: o.reshape((N_CHIPS,) + sh) for k, o, sh in zip(names, bufs, shapes)}


def _reduce_sibling_start(grads, after, *, name):
    names = list(grads)
    gs = [grads[k].reshape(N_CHIPS, 2, grads[k].shape[1] // 2, grads[k].shape[2]) for k in names]
    send_sems, recv_sems, gs, recvs, token = _sibling_halves_start(gs, after, name=name + "_sib_start")
    return (names, [grads[k].shape[1:] for k in names], send_sems, recv_sems, gs, recvs), token


def _reduce_begin(state, place, after, *, name):
    names, shapes, send_sems, recv_sems, gs, recvs = state
    gs, recvs = _sibling_halves_wait(send_sems, recv_sems, gs, recvs, after, name=name + "_sib_wait")
    ps = [_add_halves(g, r, place, name=f"{name}_add2_{k}") for g, r, k in zip(gs, recvs, names)]
    send_sems, recv_sems, ps, lands, token = _chip_scatter_start(ps, recvs[0], name=name + "_scatter_start")
    return (names, shapes, send_sems, recv_sems, ps, lands), token


def _reduce_end(state, place, after, *, name):
    names, shapes, send_sems, recv_sems, ps, lands = state
    ps, lands = _chip_scatter_wait(send_sems, recv_sems, ps, lands, after, name=name + "_scatter_wait")
    rs = [_sum_chips(p, l, place, name=f"{name}_sum4_{k}") for p, l, k in zip(ps, lands, names)]
    send_sems, recv_sems, rs, _ = _sibling_share_start(rs, lands[0], name=name + "_share_start")
    return names, shapes, send_sems, recv_sems, rs


def _reduce_finish(state, after, *, name):
    names, shapes, send_sems, recv_sems, rs = state
    both = _sibling_share_wait(send_sems, recv_sems, rs, after, name=name + "_share_wait")
    return {k: b.reshape(sh) for k, b, sh in zip(names, both, shapes)}


def _pad_lanes(a, n=LANES):
    return jnp.pad(a, [(0, 0)] * (a.ndim - 1) + [(0, n - a.shape[-1])])


def _unshard_cols(g):
    return jnp.transpose(g, (1, 0, 2)).reshape(g.shape[1], -1)


def _shard_cols(w):
    k, n = w.shape
    return jnp.transpose(w.reshape(k, N_CHIPS, n // N_CHIPS), (1, 0, 2))


def _ffn_fwd(h, p, tag, before_out):
    b = _rms_fwd(h, p['ffn_norm'], name=f"{tag}_ffn_norm")
    up = _mm(b, p['ffn_w_up'], b_sh='n', name=f"{tag}_ffn_up", bn=1408)
    act = _glu_fwd(up, p['ffn_conv_w'], p['ffn_conv_b'], name=f"{tag}_ffn_glu")
    out = _mm(act, p['ffn_w_down'], res=h, after=before_out(act), name=f"{tag}_ffn_down", bk=704)
    return out, (h, b, up, act)


def _ffn_bwd(dh, saved, p, tag, after, on_big):
    h, b, up, act = saved
    dact = _mm(dh, p['ffn_w_down'], tb=True, after=after, out_dtype=BF16, name=f"{tag}_ffn_dact", bn=1408)
    dw_down = _mm(act, dh, ta=True, after=after, out_dtype=BF16, name=f"{tag}_ffn_dwdown", bm=1408)
    dup, dcw, dcb = _glu_bwd(up, p['ffn_conv_w'], p['ffn_conv_b'], dact, name=f"{tag}_ffn_dglu")
    dw_up = _mm(b, dup, ta=True, b_sh='n', o_sh=True, out_dtype=BF16, name=f"{tag}_ffn_dwup", bn=1408)
    sent = on_big({'ffn_w_up': dw_up, 'ffn_w_down': dw_down.reshape(N_CHIPS, -1, dw_down.shape[1])})
    db = _mm(dup, p['ffn_w_up'], a_sh=True, b_sh='k', after=sent, name=f"{tag}_ffn_db", bk=1408)
    dcw = jnp.transpose(dcw, (1, 0, 2)).reshape(dcw.shape[1], -1)
    dcb = dcb.reshape(1, -1)
    dh_in, dg = _rms_bwd(h, p['ffn_norm'], db, res=dh, name=f"{tag}_ffn_dnorm")
    small = {'ffn_norm': dg, 'ffn_conv_w': dcw, 'ffn_conv_b': dcb}
    return dh_in, small


def _qkv_attn_fwd(kind, h, p, tag, n_heads, before_out):
    a = _rms_fwd(h, p['mix_norm'], name=f"{tag}_norm")
    if kind == 'fox':
        qkv = _mm(a, p['w_in'], name=f"{tag}_qkv", bn=896)
        cum = _fgate_fwd(qkv, p['b_f'], fcol=3 * n_heads, name=f"{tag}_fgate")
        cum_t = cum[:, :n_heads].T
        cq, ck = cum_t[:, :, None], cum_t[:, None, :]
    else:
        qkv = _mm(a, p['w_in'], b_sh='n', name=f"{tag}_qkv", bn=768)
        cq = ck = None
    cols = dict(qcol=lambda hh: hh, kcol=lambda hh: n_heads + hh, vcol=lambda hh: 2 * n_heads + hh)
    o = _attn_fwd(kind, qkv, qkv, qkv, name=f"{tag}_attn", n_heads=n_heads, dqk=HEAD_DIM, scale=HEAD_DIM ** -0.5,
                  gains=p['qk_gain'], cq=cq, ck=ck, **cols)
    out = _mm(o, p['w_out'], res=h, after=before_out(o), name=f"{tag}_out")
    return out, (h, a, qkv, o, cq, ck)


def _qkv_attn_bwd(kind, dh, saved, p, tag, n_heads, after, on_big):
    h, a, qkv, o, cq, ck = saved
    do = _mm(dh, p['w_out'], tb=True, after=after, out_dtype=BF16, name=f"{tag}_do")
    dw_out = _mm(o, dh, ta=True, after=after, out_dtype=BF16, name=f"{tag}_dwout")
    cols = dict(qcol=lambda hh: hh, kcol=lambda hh: n_heads + hh, vcol=lambda hh: 2 * n_heads + hh)
    outs = _attn_bwd(kind, qkv, qkv, qkv, o, do, name=f"{tag}_dattn", n_heads=n_heads, dqk=HEAD_DIM,
                     scale=HEAD_DIM ** -0.5, gains=p['qk_gain'], cq=cq, ck=ck, **cols)
    dq, dk, dv, dgain = outs[:4]
    small = {'q_gain': dgain[0], 'k_gain': dgain[1]}
    if kind == 'fox':
        dcq, dck = outs[4:]
        dca = _pad_lanes(dcq[:, :, 0].T)
        dcb = _pad_lanes(dck[:, 0, :].T)
        dflog, dbf = _fgate_bwd(qkv, p['b_f'], dca, dcb, fcol=3 * n_heads, n_heads=n_heads, name=f"{tag}_dfgate")
        small['b_f'] = dbf[:, :n_heads]
        dqkv = jnp.concatenate([dq, dk, dv, dflog], axis=1)
        dw_in = _mm(a, dqkv, ta=True, out_dtype=BF16, name=f"{tag}_dwin", bn=896)
        dw_in = _shard_cols(dw_in[:, :3 * n_heads * HEAD_DIM + n_heads])
    else:
        dqkv = jnp.concatenate([dq, dk, dv], axis=1)
        dw_in = _mm(a, dqkv, ta=True, o_sh=True, out_dtype=BF16, name=f"{tag}_dwin", bn=768)
    sent = on_big({'w_in': dw_in, 'w_out': dw_out.reshape(N_CHIPS, -1, dw_out.shape[1])})
    if kind == 'fox':
        da = _mm(dqkv, p['w_in'], tb=True, after=sent, name=f"{tag}_da", bk=896)
    else:
        da = _mm(dqkv, p['w_in'], b_sh='k', after=sent, name=f"{tag}_da", bk=768)
    dh_in, dg = _rms_bwd(h, p['mix_norm'], da, res=dh, name=f"{tag}_dnorm")
    small['mix_norm'] = dg
    return dh_in, small


def _mla_fwd(h, p, tag, n_heads, before_out):
    a = _rms_fwd(h, p['mix_norm'], name=f"{tag}_norm")
    c = _mm(a, p['w_in'], name=f"{tag}_latent", bn=1152)
    cn = _mla_latent_fwd(c, p['a_gain'], name=f"{tag}_latent_norm")
    qp = _mm(cn[:, :MLA_Q_RANK], p['w_q_b'], name=f"{tag}_q_up")
    kv = _mm(cn[:, MLA_Q_RANK:], p['w_kv_b'], b_sh='n', name=f"{tag}_kv_up")
    qc, kc = _mla_prep_fwd(qp, kv, c, p['gq'], p['gk'], p['cos'], p['sin'], n_heads=n_heads, name=f"{tag}_prep")
    cols = dict(qcol=lambda hh: hh, kcol=lambda hh: hh, vcol=lambda hh: 2 * hh + 1)
    scale = (MLA_NOPE + MLA_ROPE) ** -0.5
    o = _attn_fwd('mla', qc, kc, kv, name=f"{tag}_attn", n_heads=n_heads, dqk=2 * LANES, scale=scale, **cols)
    out = _mm(o, p['w_out'], res=h, after=before_out(o), name=f"{tag}_out")
    return out, (h, a, c, cn, qp, kv, qc, kc, o)


def _mla_bwd(dh, saved, p, tag, n_heads, after, on_big):
    h, a, c, cn, qp, kv, qc, kc, o = saved
    do = _mm(dh, p['w_out'], tb=True, after=after, out_dtype=BF16, name=f"{tag}_do")
    dw_out = _mm(o, dh, ta=True, after=after, out_dtype=BF16, name=f"{tag}_dwout")
    cols = dict(qcol=lambda hh: hh, kcol=lambda hh: hh, vcol=lambda hh: 2 * hh + 1)
    scale = (MLA_NOPE + MLA_ROPE) ** -0.5
    dqc, dkc, dv = _attn_bwd('mla', qc, kc, kv, o, do, name=f"{tag}_dattn", n_heads=n_heads, dqk=2 * LANES,
                             scale=scale, **cols)
    dqn, dqr, dkv, dkr, dgq, dgk = _mla_prep_bwd(qp, kv, c, p['gq'], p['gk'], p['cos'], p['sin'], dqc, dkc, dv,
                                                 n_heads=n_heads, name=f"{tag}_dprep")
    dqp = jnp.concatenate([dqn, dqr], axis=1)
    cn_q, cn_kv = cn[:, :MLA_Q_RANK], cn[:, MLA_Q_RANK:]
    dw_q_b = _mm(cn_q, dqp, ta=True, out_dtype=BF16, name=f"{tag}_dwqb", bm=512)
    dcn_q = _mm(dqp, p['w_q_b'], tb=True, out_dtype=BF16, name=f"{tag}_dcnq")
    dw_kv_b = _mm(cn_kv, dkv, ta=True, o_sh=True, out_dtype=BF16, name=f"{tag}_dwkvb", bm=512)
    dcn_kv = _mm(dkv, p['w_kv_b'], b_sh='k', out_dtype=BF16, name=f"{tag}_dcnkv")
    dc, dga = _mla_latent_bwd(c, p['a_gain'], dcn_q, dcn_kv, dkr, name=f"{tag}_dlatent")
    dw_in = _mm(a, dc, ta=True, out_dtype=BF16, name=f"{tag}_dwin", bn=1152)
    k_rank = dw_q_b.shape[0]
    nope = dw_q_b[:, :n_heads * LANES].reshape(k_rank, n_heads, LANES)
    rope = dw_q_b[:, n_heads * LANES:].reshape(k_rank, n_heads, LANES)[:, :, :MLA_ROPE]
    dw_q_b = jnp.concatenate([nope, rope], axis=2).reshape(k_rank, n_heads * (MLA_NOPE + MLA_ROPE))
    w_in_cols = MLA_Q_RANK + MLA_KV_RANK + MLA_ROPE
    sent = on_big({'w_in': dw_in[:, :w_in_cols].reshape(N_CHIPS, -1, w_in_cols), 'w_q_b': _shard_cols(dw_q_b),
                   'w_kv_b': dw_kv_b, 'w_out': dw_out.reshape(N_CHIPS, -1, dw_out.shape[1])})
    da = _mm(dc, p['w_in'], tb=True, after=sent, name=f"{tag}_da", bk=1152)
    dh_in, dg = _rms_bwd(h, p['mix_norm'], da, res=dh, name=f"{tag}_dnorm")
    small = {'mix_norm': dg, 'q_a_gain': dga[:, :MLA_Q_RANK], 'kv_a_gain': dga[:, MLA_Q_RANK:],
             'q_gain': jnp.concatenate([dgq[0], dgq[1][:, :MLA_ROPE]], axis=1),
             'k_gain': jnp.concatenate([dgk[0], dgk[1][:, :MLA_ROPE]], axis=1)}
    return dh_in, small


def _sgu_fwd(h, p, tag, before_out):
    a = _rms_fwd(h, p['mix_norm'], name=f"{tag}_norm")
    uv = _mm(a, p['w_in'], b_sh='n', name=f"{tag}_in")
    u, vn = _sgu_act_fwd(uv, p['v_gain'], name=f"{tag}_act")
    gated = _sgu_mix_fwd(u, vn, p['w_s'], p['b_s'], name=f"{tag}_mix")
    out = _mm(gated, p['w_out'], res=h, after=before_out(gated), name=f"{tag}_out")
    return out, (h, a, uv, u, vn, gated)


def _sgu_bwd(dh, saved, p, tag, after, on_big):
    h, a, uv, u, vn, gated = saved
    dgated = _mm(dh, p['w_out'], tb=True, after=after, out_dtype=BF16, name=f"{tag}_dgated")
    dw_out = _mm(gated, dh, ta=True, after=after, out_dtype=BF16, name=f"{tag}_dwout")
    du, dvn, dws, dbs = _sgu_mix_bwd(u, vn, p['w_s'], p['b_s'], dgated, name=f"{tag}_dmix")
    duv, dvg = _sgu_act_bwd(uv, p['v_gain'], du, dvn, name=f"{tag}_dact")
    dw_in = _mm(a, duv, ta=True, o_sh=True, out_dtype=BF16, name=f"{tag}_dwin")
    sent = on_big({'w_in': dw_in, 'w_out': dw_out.reshape(N_CHIPS, -1, dw_out.shape[1])})
    da = _mm(duv, p['w_in'], b_sh='k', after=sent, name=f"{tag}_da")
    dh_in, dg = _rms_bwd(h, p['mix_norm'], da, res=dh, name=f"{tag}_dnorm")
    small = {'mix_norm': dg, 'v_gain': dvg, 'w_s': dws, 'b_s': dbs[:, :, 0]}
    return dh_in, small


def _pack(parts):
    flat = jnp.concatenate([p.reshape(-1).astype(F32) for p in parts])
    rows = -(-flat.shape[0] // LANES)
    rows = -(-rows // 32) * 32
    return jnp.pad(flat, (0, rows * LANES - flat.shape[0])).reshape(rows, LANES)


def _unpack(packed, shapes):
    flat = packed.reshape(-1)
    out, off = [], 0
    for s in shapes:
        n = 1
        for d in s:
            n *= d
        out.append(flat[off:off + n].reshape(s))
        off += n
    return out


MIXERS = ('fox', 'mla', 'sb', 'sgu')
WEIGHT_NAMES = ['mix_norm', 'ffn_norm', 'fox_w_in', 'fox_b_f', 'fox_q_gain', 'fox_k_gain', 'fox_w_out', 'mla_w_in',
                'mla_q_a_gain', 'mla_kv_a_gain', 'mla_w_q_b', 'mla_w_kv_b', 'mla_q_gain', 'mla_k_gain', 'mla_w_out',
                'sb_w_in', 'sb_q_gain', 'sb_k_gain', 'sb_w_out', 'sgu_w_in', 'sgu_v_gain', 'sgu_w_s', 'sgu_b_s',
                'sgu_w_out', 'ffn_w_up', 'ffn_conv_w', 'ffn_conv_b', 'ffn_w_down']
SMALL_SHARDED = {'mla_q_a_gain': 1, 'mla_kv_a_gain': 1, 'sgu_v_gain': 1, 'ffn_conv_w': 2}
BIG = ['fox_w_in', 'fox_w_out', 'mla_w_in', 'mla_w_q_b', 'mla_w_kv_b', 'mla_w_out', 'sb_w_in', 'sb_w_out', 'sgu_w_in',
       'sgu_w_out', 'ffn_w_up', 'ffn_w_down']


def kernel(x, positions, mix_norm, ffn_norm, fox_w_in, fox_b_f, fox_q_gain, fox_k_gain, fox_w_out, mla_w_in, mla_q_a_gain, mla_kv_a_gain, mla_w_q_b, mla_w_kv_b, mla_q_gain, mla_k_gain, mla_w_out, sb_w_in, sb_q_gain, sb_k_gain, sb_w_out, sgu_w_in, sgu_v_gain, sgu_w_s, sgu_b_s, sgu_w_out, ffn_w_up, ffn_conv_w, ffn_conv_b, ffn_w_down, loss_target, m_mix_norm, m_ffn_norm, m_fox_w_in, m_fox_b_f, m_fox_q_gain, m_fox_k_gain, m_fox_w_out, m_mla_w_in, m_mla_q_a_gain, m_mla_kv_a_gain, m_mla_w_q_b, m_mla_w_kv_b, m_mla_q_gain, m_mla_k_gain, m_mla_w_out, m_sb_w_in, m_sb_q_gain, m_sb_k_gain, m_sb_w_out, m_sgu_w_in, m_sgu_v_gain, m_sgu_w_s, m_sgu_b_s, m_sgu_w_out, m_ffn_w_up, m_ffn_conv_w, m_ffn_conv_b, m_ffn_w_down, v_mix_norm, v_ffn_norm, v_fox_w_in, v_fox_b_f, v_fox_q_gain, v_fox_k_gain, v_fox_w_out, v_mla_w_in, v_mla_q_a_gain, v_mla_kv_a_gain, v_mla_w_q_b, v_mla_w_kv_b, v_mla_q_gain, v_mla_k_gain, v_mla_w_out, v_sb_w_in, v_sb_q_gain, v_sb_k_gain, v_sb_w_out, v_sgu_w_in, v_sgu_v_gain, v_sgu_w_s, v_sgu_b_s, v_sgu_w_out, v_ffn_w_up, v_ffn_conv_w, v_ffn_conv_b, v_ffn_w_down):
    args = dict(locals())
    W = {k: args[k] for k in WEIGHT_NAMES}
    M = {k: args['m_' + k] for k in WEIGHT_NAMES}
    V = {k: args['v_' + k] for k in WEIGHT_NAMES}
    depth = mix_norm.shape[0]
    s_len, d_model = x.shape[1], x.shape[2]
    n_heads = d_model // HEAD_DIM
    assert all(W[k].shape[0] == 1 for k in WEIGHT_NAMES if k.split('_')[0] in MIXERS), "one layer per mixer"
    xi, yi, ci = lax.axis_index("x"), lax.axis_index("y"), lax.axis_index("c")
    chip = 2 * xi + yi
    place = tuple(jnp.reshape(v, (1,)).astype(jnp.int32) for v in (xi, yi, ci))

    small_local = _pack([W[k][0] if k != 'ffn_conv_w' else W[k] for k in SMALL_SHARDED])

    def piece_shards(i, part):
        if part == 'ffn':
            return {'ffn_w_up': (W['ffn_w_up'], i), 'ffn_w_down': (W['ffn_w_down'], i)}
        mixer = MIXERS[i % len(MIXERS)]
        shards = {k: W[k][0] for k in BIG if k.startswith(mixer + '_')}
        if i == 0:
            shards['small'] = small_local
        return shards

    pieces = [(i, part) for i in range(depth) for part in ('mixer', 'ffn')]
    gathered = {}
    pname = lambda pc: f"gather_l{pc[0]}_{pc[1]}"
    states = {}
    placed = {pieces[0]: _gather_place(piece_shards(*pieces[0]), place, name=pname(pieces[0]))}
    states[pieces[0]], token = _gather_begin(placed[pieces[0]], mix_norm, name=pname(pieces[0]))
    prev = token
    for pc in pieces[1:]:
        placed[pc] = _gather_place(piece_shards(*pc), place, name=pname(pc), after=prev)
        prev = placed[pc][2][-1]
    gathered[pieces[0]] = _gather_end(states.pop(pieces[0]), placed[pieces[-1]][2][-1], name=pname(pieces[0]))
    first_done = next(iter(gathered[pieces[0]].values()))
    states[pieces[1]], token = _gather_begin(placed[pieces[1]], first_done, name=pname(pieces[1]))
    tokens = [token]
    small_shapes = [W[k][0].shape if k != 'ffn_conv_w' else W[k].shape for k in SMALL_SHARDED]
    per_chip = [_unpack(gathered[pieces[0]]['small'][s], small_shapes) for s in range(N_CHIPS)]
    full_small = {k: jnp.concatenate([per_chip[s][j] for s in range(N_CHIPS)], axis=-1)
                  for j, k in enumerate(SMALL_SHARDED)}

    pos = positions.reshape(s_len).astype(F32)
    inv_freq = ROPE_THETA ** (-jnp.arange(0, MLA_ROPE, 2, dtype=F32) / MLA_ROPE)
    ang = pos[:, None] * inv_freq
    cos_t = _pad_lanes(jnp.concatenate([jnp.cos(ang), jnp.cos(ang)], axis=1))
    sin_t = _pad_lanes(jnp.concatenate([-jnp.sin(ang), jnp.sin(ang)], axis=1))

    def piece_params(i, part):
        mixer = MIXERS[i % len(MIXERS)]
        g = gathered[(i, part)]
        if part == 'ffn':
            return mixer, {'ffn_norm': ffn_norm[i:i + 1], 'ffn_w_up': g['ffn_w_up'],
                           'ffn_w_down': g['ffn_w_down'].reshape(-1, d_model),
                           'ffn_conv_w': full_small['ffn_conv_w'][i], 'ffn_conv_b': ffn_conv_b[i:i + 1]}
        p = {'mix_norm': mix_norm[i:i + 1]}
        rows = lambda w: w.reshape(-1, w.shape[-1])
        if mixer == 'fox':
            w = _unshard_cols(g['fox_w_in'])
            p['w_in'] = jnp.pad(w, ((0, 0), (0, (3 * n_heads + 1) * HEAD_DIM - w.shape[1])))
            p['b_f'] = _pad_lanes(fox_b_f)
            p['qk_gain'] = jnp.stack([fox_q_gain, fox_k_gain])
            p['w_out'] = rows(g['fox_w_out'])
        elif mixer == 'sb':
            p['w_in'] = g['sb_w_in']
            p['qk_gain'] = jnp.stack([sb_q_gain, sb_k_gain])
            p['w_out'] = rows(g['sb_w_out'])
        elif mixer == 'sgu':
            p['w_in'] = g['sgu_w_in']
            p['v_gain'] = full_small['sgu_v_gain'].reshape(1, -1)
            p['w_s'] = sgu_w_s[0]
            p['b_s'] = sgu_b_s[0][:, :, None]
            p['w_out'] = rows(g['sgu_w_out'])
        else:
            w = rows(g['mla_w_in'])
            p['w_in'] = jnp.pad(w, ((0, 0), (0, MLA_Q_RANK + MLA_KV_RANK + LANES - w.shape[1])))
            p['a_gain'] = jnp.concatenate([full_small['mla_q_a_gain'], full_small['mla_kv_a_gain']]).reshape(1, -1)
            wq = _unshard_cols(g['mla_w_q_b']).reshape(MLA_Q_RANK, n_heads, MLA_NOPE + MLA_ROPE)
            p['w_q_b'] = jnp.concatenate([wq[:, :, :MLA_NOPE].reshape(MLA_Q_RANK, -1),
                                          _pad_lanes(wq[:, :, MLA_NOPE:]).reshape(MLA_Q_RANK, -1)], axis=1)
            p['w_kv_b'] = g['mla_w_kv_b']
            p['gq'] = jnp.stack([mla_q_gain[:, :MLA_NOPE], _pad_lanes(mla_q_gain[:, MLA_NOPE:])])
            p['gk'] = jnp.stack([mla_k_gain[:, :MLA_NOPE], _pad_lanes(mla_k_gain[:, MLA_NOPE:])])
            p['cos'], p['sin'] = cos_t, sin_t
            p['w_out'] = rows(g['mla_w_out'])
        return mixer, p

    h = x.reshape(s_len, d_model)
    saved = []
    for n, (i, part) in enumerate(pieces):
        nxt = pieces[n + 1] if n + 1 < len(pieces) else None
        ahead = pieces[n + 2] if n + 2 < len(pieces) else None
        if ahead is not None:
            after = next(iter(gathered[(i, part)].values()))
            states[ahead], token = _gather_begin(placed[ahead], after, name=pname(ahead))
            tokens.append(token)
        mixer, p = piece_params(i, part)
        gain = 'ffn_norm' if part == 'ffn' else 'mix_norm'
        for token in tokens:
            p[gain] = p[gain] + token[0:1, 0:1]
        tokens = []
        tag = f"l{i}_{mixer}"
        forwarding = []

        def before_out(made, nxt=nxt, forwarding=forwarding):
            if nxt is None:
                return None
            st, tok = _gather_forward(states.pop(nxt), made, name=pname(nxt))
            forwarding.append(st)
            return tok

        if part == 'ffn':
            h, sv = _ffn_fwd(h, p, f"l{i}", before_out)
        elif mixer in ('fox', 'sb'):
            h, sv = _qkv_attn_fwd(mixer, h, p, tag, n_heads, before_out)
        elif mixer == 'mla':
            h, sv = _mla_fwd(h, p, tag, n_heads, before_out)
        else:
            h, sv = _sgu_fwd(h, p, tag, before_out)
        saved.append((mixer, p, sv))
        if nxt is not None:
            gathered[nxt] = _gather_finish(forwarding[0], h, name=pname(nxt))
    loss_row, dh = _loss(h, loss_target.reshape(s_len, d_model))
    loss = lax.psum(loss_row[0, 0], ("x", "y", "c"))

    big_grads, small_grads = {}, {k: [None] * depth for k in ('mix_norm', 'ffn_norm', 'ffn_conv_w', 'ffn_conv_b')}

    def keep(reduced, i):
        for k, v in reduced.items():
            if k.startswith('ffn_'):
                big_grads.setdefault(k, [None] * depth)[i] = v
            else:
                big_grads[k] = v[None]

    state, token, flying = None, jnp.broadcast_to(loss, (8, LANES)), None
    shares = []
    for n in reversed(range(len(pieces))):
        i, part = pieces[n]
        mixer, p, sv = saved[n]
        tag = f"l{i}_{mixer}"
        rname = f"reduce_l{i}_{part}"
        started = []

        def on_big(big, prefix=('' if part == 'ffn' else mixer + '_'), rname=rname, started=started):
            st, tok = _reduce_sibling_start({prefix + k: v for k, v in big.items()}, place[0], name=rname)
            started.append(st)
            return tok

        if part == 'ffn':
            dh, small = _ffn_bwd(dh, sv, p, f"l{i}", token, on_big)
        elif mixer in ('fox', 'sb'):
            dh, small = _qkv_attn_bwd(mixer, dh, sv, p, tag, n_heads, token, on_big)
        elif mixer == 'mla':
            dh, small = _mla_bwd(dh, sv, p, tag, n_heads, token, on_big)
        else:
            dh, small = _sgu_bwd(dh, sv, p, tag, token, on_big)
        if state is not None:
            fname = f"reduce_l{flying[0]}_{flying[1]}"
            shares.append((_reduce_end(state, place, dh, name=fname), flying[0], fname))
        state, token = _reduce_begin(started[0], place, dh, name=rname)
        flying = (i, part)
        for k, v in small.items():
            if k in small_grads:
                small_grads[k][i] = v
            else:
                small_grads[f"{mixer}_{k}"] = v
    last_state = state
    grad_x = dh.reshape(x.shape)
    for k in ('mix_norm', 'ffn_norm', 'ffn_conv_b'):
        small_grads[k] = jnp.concatenate(small_grads[k], axis=0)
    small_grads['ffn_conv_w'] = jnp.stack(small_grads['ffn_conv_w'])

    small_names = [k for k in WEIGHT_NAMES if k not in BIG]
    full_shapes = {k: (W[k].shape[:-1] + (W[k].shape[-1] * N_CHIPS,) if k in SMALL_SHARDED else W[k].shape)
                   for k in small_names}
    packed = _pack([small_grads[k].reshape(full_shapes[k]) for k in small_names])
    last_token = token
    summed = _sum_devices(_broadcast_all(packed, last_token, name="small_bcast"), name="small_sum")
    small_full = dict(zip(small_names, _unpack(summed, [full_shapes[k] for k in small_names])))
    for share_state, i, fname in shares:
        keep(_reduce_finish(share_state, summed, name=fname), i)
    for k in ('ffn_w_up', 'ffn_w_down'):
        big_grads[k] = jnp.stack(big_grads[k])
    grads = dict(big_grads)
    for k in small_names:
        g = small_full[k]
        if k in SMALL_SHARDED:
            n = W[k].shape[-1]
            g = lax.dynamic_slice_in_dim(g, chip * n, n, axis=g.ndim - 1)
        grads[k] = g

    delta, new_m, new_v = {}, {}, {}

    def update(k):
        grads[k] = grads[k].reshape(W[k].shape)
        delta[k], new_m[k], new_v[k] = _adamw(W[k], grads[k], M[k], V[k], after=last_token, name=f"adamw_{k}")

    last_names = [k for k in BIG if k.startswith(MIXERS[0] + '_')]
    for k in WEIGHT_NAMES:
        if k not in last_names:
            update(k)
    share_state = _reduce_end(last_state, place, delta['ffn_w_up'], name="reduce_l0_mixer")
    keep(_reduce_finish(share_state, delta['ffn_w_down'], name="reduce_l0_mixer"), 0)
    for k in last_names:
        grads[k] = big_grads[k]
        update(k)
    return (loss, grad_x, *[grads[k] for k in WEIGHT_NAMES], *[delta[k] for k in WEIGHT_NAMES],
            *[new_m[k] for k in WEIGHT_NAMES], *[new_v[k] for k in WEIGHT_NAMES])
```
